```python
import jax, jax.numpy as jnp
from jax import lax
import numpy as np

D_MODEL = 1024
BATCH = 16
SEQ = 2048
DEPTH = 4

MIX_WIDTH = D_MODEL
N_BRANCH = 3
CONV_A_K = 3
CONF_K = 31
CHUNK = 128
SGU_HEADS = 8
SGU_HEAD_DIM = MIX_WIDTH // SGU_HEADS
D_FF = 4 * D_MODEL
EPS = 1e-6
IN_COLS = 3 * MIX_WIDTH + 2 * MIX_WIDTH + 2 * MIX_WIDTH + N_BRANCH * D_MODEL
SPLITS = tuple(int(s) for s in np.cumsum([MIX_WIDTH] * 7 + [D_MODEL] * 2))

kernel_name = "hybrid_conv_sgu_gated_block"


def rms_norm(x, g):
    xf = x.astype(jnp.float32)
    y = xf * lax.rsqrt(jnp.mean(xf * xf, axis=-1, keepdims=True) + EPS)
    return (y * g.astype(jnp.float32)).astype(x.dtype)


def layer_norm(x, g, b):
    xf = x.astype(jnp.float32)
    mu = jnp.mean(xf, axis=-1, keepdims=True)
    var = jnp.mean(jnp.square(xf - mu), axis=-1, keepdims=True)
    y = (xf - mu) * lax.rsqrt(var + EPS)
    return (y * g.astype(jnp.float32) + b.astype(jnp.float32)).astype(x.dtype)


def causal_dwconv(x, w):
    k, c = w.shape
    return lax.conv_general_dilated(
        x, w[:, None, :].astype(x.dtype), window_strides=(1,), padding=[(k - 1, 0)],
        dimension_numbers=("NWC", "WIO", "NWC"), feature_group_count=c)


def spatial_gating(u, v, ln_g, ln_b, ws, bs):
    b, s, w = v.shape
    v = layer_norm(v, ln_g, ln_b).reshape(b, s // CHUNK, CHUNK, SGU_HEADS, SGU_HEAD_DIM)
    causal = jnp.tril(jnp.ones((CHUNK, CHUNK), dtype=bool))
    ws_c = jnp.where(causal[None], ws, jnp.zeros_like(ws))
    mixed = jnp.einsum("hts,bcshd->bcthd", ws_c, v) + jnp.transpose(bs)[:, :, None]
    return u * mixed.reshape(b, s, w)


def hybrid_mixer(xn, w_in, conv_a_w, conf_dw_w, conf_dw_b, conf_ln_g, conf_ln_b,
                 sgu_ln_g, sgu_ln_b, sgu_ws, sgu_b, w_branch, w_out):
    z = xn @ w_in
    a_h, a_b, a_c, c_a, c_g, s_u, s_v, g_a, g_c, g_s = jnp.split(z, SPLITS, axis=-1)
    y_a = a_b * causal_dwconv(a_c * a_h, conv_a_w)
    y_c = causal_dwconv(c_a * jax.nn.sigmoid(c_g), conf_dw_w) + conf_dw_b
    y_c = jax.nn.silu(layer_norm(y_c, conf_ln_g, conf_ln_b))
    y_s = spatial_gating(jax.nn.gelu(s_u), jax.nn.gelu(s_v), sgu_ln_g, sgu_ln_b, sgu_ws, sgu_b)
    merged = (jax.nn.sigmoid(g_a) * (y_a @ w_branch[0])
              + jax.nn.sigmoid(g_c) * (y_c @ w_branch[1])
              + jax.nn.sigmoid(g_s) * (y_s @ w_branch[2]))
    return merged @ w_out


def _fwd_setup_inputs(seed: int = 0) -> dict:
    key = jax.random.key(seed)
    ks = jax.random.split(key, 24)
    n = jax.random.normal
    f32 = jnp.float32

    def gain(k, shape):
        return 1.0 + 0.05 * n(k, shape, f32)

    return {
        "x": n(ks[0], (BATCH, SEQ, D_MODEL), f32),
        "norm_mix_pre": gain(ks[1], (DEPTH, D_MODEL)),
        "norm_mix_post": gain(ks[2], (DEPTH, D_MODEL)),
        "norm_ffn_pre": gain(ks[3], (DEPTH, D_MODEL)),
        "norm_ffn_post": gain(ks[4], (DEPTH, D_MODEL)),
        "w_in": n(ks[5], (DEPTH, D_MODEL, IN_COLS), f32) * D_MODEL ** -0.5,
        "conv_a_w": n(ks[6], (DEPTH, CONV_A_K, MIX_WIDTH), f32) * CONV_A_K ** -0.5,
        "conf_dw_w": n(ks[7], (DEPTH, CONF_K, MIX_WIDTH), f32) * CONF_K ** -0.5,
        "conf_dw_b": 0.02 * n(ks[8], (DEPTH, MIX_WIDTH), f32),
        "conf_ln_g": gain(ks[9], (DEPTH, MIX_WIDTH)),
        "conf_ln_b": 0.02 * n(ks[10], (DEPTH, MIX_WIDTH), f32),
        "sgu_ln_g": gain(ks[11], (DEPTH, MIX_WIDTH)),
        "sgu_ln_b": 0.02 * n(ks[12], (DEPTH, MIX_WIDTH), f32),
        "sgu_ws": n(ks[13], (DEPTH, SGU_HEADS, CHUNK, CHUNK), f32) * CHUNK ** -0.5,
        "sgu_b": 1.0 + 0.1 * n(ks[14], (DEPTH, SGU_HEADS, CHUNK), f32),
        "w_branch": n(ks[15], (DEPTH, N_BRANCH, MIX_WIDTH, D_MODEL), f32) * MIX_WIDTH ** -0.5,
        "w_out": n(ks[16], (DEPTH, D_MODEL, D_MODEL), f32) * D_MODEL ** -0.5,
        "w_ff1": n(ks[17], (DEPTH, D_MODEL, D_FF), f32) * D_MODEL ** -0.5,
        "w_ff2": n(ks[18], (DEPTH, D_FF, D_MODEL), f32) * D_FF ** -0.5,
    }


def _fwd_reference(x, norm_mix_pre, norm_mix_post, norm_ffn_pre, norm_ffn_post, w_in, conv_a_w,
              conf_dw_w, conf_dw_b, conf_ln_g, conf_ln_b, sgu_ln_g, sgu_ln_b, sgu_ws, sgu_b,
              w_branch, w_out, w_ff1, w_ff2):
    for l in range(DEPTH):
        h = rms_norm(x, norm_mix_pre[l])
        m = hybrid_mixer(h, w_in[l], conv_a_w[l], conf_dw_w[l], conf_dw_b[l], conf_ln_g[l],
                         conf_ln_b[l], sgu_ln_g[l], sgu_ln_b[l], sgu_ws[l], sgu_b[l],
                         w_branch[l], w_out[l])
        x = x + rms_norm(m, norm_mix_post[l])
        h = rms_norm(x, norm_ffn_pre[l])
        f = jnp.square(jax.nn.relu(h @ w_ff1[l])) @ w_ff2[l]
        x = x + rms_norm(f, norm_ffn_post[l])
    return x


import jax as _jax
import jax.numpy as _jnp

TWIN_FORMAT = 'train_step'
FWD_PARAMS = ['x', 'norm_mix_pre', 'norm_mix_post', 'norm_ffn_pre', 'norm_ffn_post', 'w_in', 'conv_a_w', 'conf_dw_w', 'conf_dw_b', 'conf_ln_g', 'conf_ln_b', 'sgu_ln_g', 'sgu_ln_b', 'sgu_ws', 'sgu_b', 'w_branch', 'w_out', 'w_ff1', 'w_ff2']
TWIN_WEIGHTS = ['norm_mix_pre', 'norm_mix_post', 'norm_ffn_pre', 'norm_ffn_post', 'w_in', 'conv_a_w', 'conf_dw_w', 'conf_dw_b', 'conf_ln_g', 'conf_ln_b', 'sgu_ln_g', 'sgu_ln_b', 'sgu_ws', 'sgu_b', 'w_branch', 'w_out', 'w_ff1', 'w_ff2']
TWIN_DIFF_INPUT = 'x'
TWIN_INPUTS = ['x', 'norm_mix_pre', 'norm_mix_post', 'norm_ffn_pre', 'norm_ffn_post', 'w_in', 'conv_a_w', 'conf_dw_w', 'conf_dw_b', 'conf_ln_g', 'conf_ln_b', 'sgu_ln_g', 'sgu_ln_b', 'sgu_ws', 'sgu_b', 'w_branch', 'w_out', 'w_ff1', 'w_ff2', 'loss_target', 'm_norm_mix_pre', 'm_norm_mix_post', 'm_norm_ffn_pre', 'm_norm_ffn_post', 'm_w_in', 'm_conv_a_w', 'm_conf_dw_w', 'm_conf_dw_b', 'm_conf_ln_g', 'm_conf_ln_b', 'm_sgu_ln_g', 'm_sgu_ln_b', 'm_sgu_ws', 'm_sgu_b', 'm_w_branch', 'm_w_out', 'm_w_ff1', 'm_w_ff2', 'v_norm_mix_pre', 'v_norm_mix_post', 'v_norm_ffn_pre', 'v_norm_ffn_post', 'v_w_in', 'v_conv_a_w', 'v_conf_dw_w', 'v_conf_dw_b', 'v_conf_ln_g', 'v_conf_ln_b', 'v_sgu_ln_g', 'v_sgu_ln_b', 'v_sgu_ws', 'v_sgu_b', 'v_w_branch', 'v_w_out', 'v_w_ff1', 'v_w_ff2']
TWIN_OUTPUTS = ['loss', 'grad_x', 'grad_norm_mix_pre', 'grad_norm_mix_post', 'grad_norm_ffn_pre', 'grad_norm_ffn_post', 'grad_w_in', 'grad_conv_a_w', 'grad_conf_dw_w', 'grad_conf_dw_b', 'grad_conf_ln_g', 'grad_conf_ln_b', 'grad_sgu_ln_g', 'grad_sgu_ln_b', 'grad_sgu_ws', 'grad_sgu_b', 'grad_w_branch', 'grad_w_out', 'grad_w_ff1', 'grad_w_ff2', 'delta_norm_mix_pre', 'delta_norm_mix_post', 'delta_norm_ffn_pre', 'delta_norm_ffn_post', 'delta_w_in', 'delta_conv_a_w', 'delta_conf_dw_w', 'delta_conf_dw_b', 'delta_conf_ln_g', 'delta_conf_ln_b', 'delta_sgu_ln_g', 'delta_sgu_ln_b', 'delta_sgu_ws', 'delta_sgu_b', 'delta_w_branch', 'delta_w_out', 'delta_w_ff1', 'delta_w_ff2', 'new_m_norm_mix_pre', 'new_m_norm_mix_post', 'new_m_norm_ffn_pre', 'new_m_norm_ffn_post', 'new_m_w_in', 'new_m_conv_a_w', 'new_m_conf_dw_w', 'new_m_conf_dw_b', 'new_m_conf_ln_g', 'new_m_conf_ln_b', 'new_m_sgu_ln_g', 'new_m_sgu_ln_b', 'new_m_sgu_ws', 'new_m_sgu_b', 'new_m_w_branch', 'new_m_w_out', 'new_m_w_ff1', 'new_m_w_ff2', 'new_v_norm_mix_pre', 'new_v_norm_mix_post', 'new_v_norm_ffn_pre', 'new_v_norm_ffn_post', 'new_v_w_in', 'new_v_conv_a_w', 'new_v_conf_dw_w', 'new_v_conf_dw_b', 'new_v_conf_ln_g', 'new_v_conf_ln_b', 'new_v_sgu_ln_g', 'new_v_sgu_ln_b', 'new_v_sgu_ws', 'new_v_sgu_b', 'new_v_w_branch', 'new_v_w_out', 'new_v_w_ff1', 'new_v_w_ff2']
TWIN_LEAF_KINDS = {'loss': 'loss', 'grad_x': 'grad_x', 'grad_norm_mix_pre': 'grad_w', 'grad_norm_mix_post': 'grad_w', 'grad_norm_ffn_pre': 'grad_w', 'grad_norm_ffn_post': 'grad_w', 'grad_w_in': 'grad_w', 'grad_conv_a_w': 'grad_w', 'grad_conf_dw_w': 'grad_w', 'grad_conf_dw_b': 'grad_w', 'grad_conf_ln_g': 'grad_w', 'grad_conf_ln_b': 'grad_w', 'grad_sgu_ln_g': 'grad_w', 'grad_sgu_ln_b': 'grad_w', 'grad_sgu_ws': 'grad_w', 'grad_sgu_b': 'grad_w', 'grad_w_branch': 'grad_w', 'grad_w_out': 'grad_w', 'grad_w_ff1': 'grad_w', 'grad_w_ff2': 'grad_w', 'delta_norm_mix_pre': 'delta_w', 'delta_norm_mix_post': 'delta_w', 'delta_norm_ffn_pre': 'delta_w', 'delta_norm_ffn_post': 'delta_w', 'delta_w_in': 'delta_w', 'delta_conv_a_w': 'delta_w', 'delta_conf_dw_w': 'delta_w', 'delta_conf_dw_b': 'delta_w', 'delta_conf_ln_g': 'delta_w', 'delta_conf_ln_b': 'delta_w', 'delta_sgu_ln_g': 'delta_w', 'delta_sgu_ln_b': 'delta_w', 'delta_sgu_ws': 'delta_w', 'delta_sgu_b': 'delta_w', 'delta_w_branch': 'delta_w', 'delta_w_out': 'delta_w', 'delta_w_ff1': 'delta_w', 'delta_w_ff2': 'delta_w', 'new_m_norm_mix_pre': 'new_m', 'new_m_norm_mix_post': 'new_m', 'new_m_norm_ffn_pre': 'new_m', 'new_m_norm_ffn_post': 'new_m', 'new_m_w_in': 'new_m', 'new_m_conv_a_w': 'new_m', 'new_m_conf_dw_w': 'new_m', 'new_m_conf_dw_b': 'new_m', 'new_m_conf_ln_g': 'new_m', 'new_m_conf_ln_b': 'new_m', 'new_m_sgu_ln_g': 'new_m', 'new_m_sgu_ln_b': 'new_m', 'new_m_sgu_ws': 'new_m', 'new_m_sgu_b': 'new_m', 'new_m_w_branch': 'new_m', 'new_m_w_out': 'new_m', 'new_m_w_ff1': 'new_m', 'new_m_w_ff2': 'new_m', 'new_v_norm_mix_pre': 'new_v', 'new_v_norm_mix_post': 'new_v', 'new_v_norm_ffn_pre': 'new_v', 'new_v_norm_ffn_post': 'new_v', 'new_v_w_in': 'new_v', 'new_v_conv_a_w': 'new_v', 'new_v_conf_dw_w': 'new_v', 'new_v_conf_dw_b': 'new_v', 'new_v_conf_ln_g': 'new_v', 'new_v_conf_ln_b': 'new_v', 'new_v_sgu_ln_g': 'new_v', 'new_v_sgu_ln_b': 'new_v', 'new_v_sgu_ws': 'new_v', 'new_v_sgu_b': 'new_v', 'new_v_w_branch': 'new_v', 'new_v_w_out': 'new_v', 'new_v_w_ff1': 'new_v', 'new_v_w_ff2': 'new_v'}


def _forward(args):
    return _fwd_reference(*[args[k] for k in FWD_PARAMS])


def _output_shape():
    out = _jax.eval_shape(lambda: _forward(_fwd_setup_inputs(0)))
    return out.shape, out.dtype

N_MICROBATCH = 1
ADAM_LR = 0.001
ADAM_B1 = 0.9
ADAM_B2 = 0.999
ADAM_EPS = 1e-08
ADAM_WD = 0.01
ADAM_STEP = 10
PER_EXAMPLE_BATCH_AXIS = {'x': 0, 'loss_target': 0}
SHARED_INPUTS = []
_WEIGHT_DTYPES = {'norm_mix_pre': _jnp.float32, 'norm_mix_post': _jnp.float32, 'norm_ffn_pre': _jnp.float32, 'norm_ffn_post': _jnp.float32, 'w_in': _jnp.float32, 'conv_a_w': _jnp.float32, 'conf_dw_w': _jnp.float32, 'conf_dw_b': _jnp.float32, 'conf_ln_g': _jnp.float32, 'conf_ln_b': _jnp.float32, 'sgu_ln_g': _jnp.float32, 'sgu_ln_b': _jnp.float32, 'sgu_ws': _jnp.float32, 'sgu_b': _jnp.float32, 'w_branch': _jnp.float32, 'w_out': _jnp.float32, 'w_ff1': _jnp.float32, 'w_ff2': _jnp.float32}
MOMENT_SCALE = {'norm_mix_pre': 3.386814e+00, 'norm_mix_post': 3.320212e+01, 'norm_ffn_pre': 5.419399e+00, 'norm_ffn_post': 3.643946e+01, 'w_in': 1.125323e+00, 'conv_a_w': 9.107233e-01, 'conf_dw_w': 3.086640e+00, 'conf_dw_b': 2.041552e+01, 'conf_ln_g': 8.912479e+00, 'conf_ln_b': 1.220777e+01, 'sgu_ln_g': 3.761627e-01, 'sgu_ln_b': 4.200762e-01, 'sgu_ws': 3.718046e-01, 'sgu_b': 5.880967e-01, 'w_branch': 4.981915e+00, 'w_out': 8.763561e+00, 'w_ff1': 2.692528e+00, 'w_ff2': 1.442261e+01}


def _to_microbatches(a, axis):
    t = _jnp.moveaxis(a, axis, 0)
    t = t.reshape((N_MICROBATCH, t.shape[0] // N_MICROBATCH) + t.shape[1:])
    return _jnp.moveaxis(t, 1, axis + 1)


def setup_inputs(seed: int = 0) -> dict:
    inp = _fwd_setup_inputs(seed)
    key = _jax.random.fold_in(_jax.random.key(seed), 7919)
    shape, _ = _output_shape()
    out = dict(inp)
    out["loss_target"] = _jax.random.normal(_jax.random.fold_in(key, 0), shape, _jnp.float32)
    for i, name in enumerate(TWIN_WEIGHTS):
        w = inp[name].astype(_jnp.float32)
        if MOMENT_SCALE is None:
            s = _jnp.sqrt(_jnp.mean(_jnp.square(w)) + 1e-30)
        else:
            s = MOMENT_SCALE[name]
        km, kv = _jax.random.split(_jax.random.fold_in(key, i + 1))
        out[name] = w
        out["m_" + name] = s * _jax.random.normal(km, w.shape, _jnp.float32)
        out["v_" + name] = (s * s) * _jax.random.uniform(kv, w.shape, _jnp.float32, 0.5, 1.5)
    if N_MICROBATCH > 1:
        for name, axis in PER_EXAMPLE_BATCH_AXIS.items():
            out[name] = _to_microbatches(out[name], axis)
    return {'x': out['x'], 'norm_mix_pre': out['norm_mix_pre'], 'norm_mix_post': out['norm_mix_post'], 'norm_ffn_pre': out['norm_ffn_pre'], 'norm_ffn_post': out['norm_ffn_post'], 'w_in': out['w_in'], 'conv_a_w': out['conv_a_w'], 'conf_dw_w': out['conf_dw_w'], 'conf_dw_b': out['conf_dw_b'], 'conf_ln_g': out['conf_ln_g'], 'conf_ln_b': out['conf_ln_b'], 'sgu_ln_g': out['sgu_ln_g'], 'sgu_ln_b': out['sgu_ln_b'], 'sgu_ws': out['sgu_ws'], 'sgu_b': out['sgu_b'], 'w_branch': out['w_branch'], 'w_out': out['w_out'], 'w_ff1': out['w_ff1'], 'w_ff2': out['w_ff2'], 'loss_target': out['loss_target'], 'm_norm_mix_pre': out['m_norm_mix_pre'], 'm_norm_mix_post': out['m_norm_mix_post'], 'm_norm_ffn_pre': out['m_norm_ffn_pre'], 'm_norm_ffn_post': out['m_norm_ffn_post'], 'm_w_in': out['m_w_in'], 'm_conv_a_w': out['m_conv_a_w'], 'm_conf_dw_w': out['m_conf_dw_w'], 'm_conf_dw_b': out['m_conf_dw_b'], 'm_conf_ln_g': out['m_conf_ln_g'], 'm_conf_ln_b': out['m_conf_ln_b'], 'm_sgu_ln_g': out['m_sgu_ln_g'], 'm_sgu_ln_b': out['m_sgu_ln_b'], 'm_sgu_ws': out['m_sgu_ws'], 'm_sgu_b': out['m_sgu_b'], 'm_w_branch': out['m_w_branch'], 'm_w_out': out['m_w_out'], 'm_w_ff1': out['m_w_ff1'], 'm_w_ff2': out['m_w_ff2'], 'v_norm_mix_pre': out['v_norm_mix_pre'], 'v_norm_mix_post': out['v_norm_mix_post'], 'v_norm_ffn_pre': out['v_norm_ffn_pre'], 'v_norm_ffn_post': out['v_norm_ffn_post'], 'v_w_in': out['v_w_in'], 'v_conv_a_w': out['v_conv_a_w'], 'v_conf_dw_w': out['v_conf_dw_w'], 'v_conf_dw_b': out['v_conf_dw_b'], 'v_conf_ln_g': out['v_conf_ln_g'], 'v_conf_ln_b': out['v_conf_ln_b'], 'v_sgu_ln_g': out['v_sgu_ln_g'], 'v_sgu_ln_b': out['v_sgu_ln_b'], 'v_sgu_ws': out['v_sgu_ws'], 'v_sgu_b': out['v_sgu_b'], 'v_w_branch': out['v_w_branch'], 'v_w_out': out['v_w_out'], 'v_w_ff1': out['v_w_ff1'], 'v_w_ff2': out['v_w_ff2']}


def _loss(weights, diff, rest, loss_target):
    with _jax.named_scope("forward"):
        args = {**rest, TWIN_DIFF_INPUT: diff, **{k: w.astype(_WEIGHT_DTYPES[k]) for k, w in weights.items()}}
        y = _forward(args)
    with _jax.named_scope("loss_head"):
        err = _jnp.square(y.astype(_jnp.float32) - loss_target)
        return 0.5 * _jnp.sum(_jnp.mean(err, axis=-1)) if err.ndim else 0.5 * err


def _adamw(w, g, m, v):
    m = ADAM_B1 * m + (1.0 - ADAM_B1) * g
    v = ADAM_B2 * v + (1.0 - ADAM_B2) * _jnp.square(g)
    m_hat = m / (1.0 - ADAM_B1 ** ADAM_STEP)
    v_hat = v / (1.0 - ADAM_B2 ** ADAM_STEP)
    delta = -ADAM_LR * (m_hat / (_jnp.sqrt(v_hat) + ADAM_EPS) + ADAM_WD * w)
    return delta, m, v


def reference(x, norm_mix_pre, norm_mix_post, norm_ffn_pre, norm_ffn_post, w_in, conv_a_w, conf_dw_w, conf_dw_b, conf_ln_g, conf_ln_b, sgu_ln_g, sgu_ln_b, sgu_ws, sgu_b, w_branch, w_out, w_ff1, w_ff2, loss_target, m_norm_mix_pre, m_norm_mix_post, m_norm_ffn_pre, m_norm_ffn_post, m_w_in, m_conv_a_w, m_conf_dw_w, m_conf_dw_b, m_conf_ln_g, m_conf_ln_b, m_sgu_ln_g, m_sgu_ln_b, m_sgu_ws, m_sgu_b, m_w_branch, m_w_out, m_w_ff1, m_w_ff2, v_norm_mix_pre, v_norm_mix_post, v_norm_ffn_pre, v_norm_ffn_post, v_w_in, v_conv_a_w, v_conf_dw_w, v_conf_dw_b, v_conf_ln_g, v_conf_ln_b, v_sgu_ln_g, v_sgu_ln_b, v_sgu_ws, v_sgu_b, v_w_branch, v_w_out, v_w_ff1, v_w_ff2):
    given = dict(x=x, norm_mix_pre=norm_mix_pre, norm_mix_post=norm_mix_post, norm_ffn_pre=norm_ffn_pre, norm_ffn_post=norm_ffn_post, w_in=w_in, conv_a_w=conv_a_w, conf_dw_w=conf_dw_w, conf_dw_b=conf_dw_b, conf_ln_g=conf_ln_g, conf_ln_b=conf_ln_b, sgu_ln_g=sgu_ln_g, sgu_ln_b=sgu_ln_b, sgu_ws=sgu_ws, sgu_b=sgu_b, w_branch=w_branch, w_out=w_out, w_ff1=w_ff1, w_ff2=w_ff2, loss_target=loss_target, m_norm_mix_pre=m_norm_mix_pre, m_norm_mix_post=m_norm_mix_post, m_norm_ffn_pre=m_norm_ffn_pre, m_norm_ffn_post=m_norm_ffn_post, m_w_in=m_w_in, m_conv_a_w=m_conv_a_w, m_conf_dw_w=m_conf_dw_w, m_conf_dw_b=m_conf_dw_b, m_conf_ln_g=m_conf_ln_g, m_conf_ln_b=m_conf_ln_b, m_sgu_ln_g=m_sgu_ln_g, m_sgu_ln_b=m_sgu_ln_b, m_sgu_ws=m_sgu_ws, m_sgu_b=m_sgu_b, m_w_branch=m_w_branch, m_w_out=m_w_out, m_w_ff1=m_w_ff1, m_w_ff2=m_w_ff2, v_norm_mix_pre=v_norm_mix_pre, v_norm_mix_post=v_norm_mix_post, v_norm_ffn_pre=v_norm_ffn_pre, v_norm_ffn_post=v_norm_ffn_post, v_w_in=v_w_in, v_conv_a_w=v_conv_a_w, v_conf_dw_w=v_conf_dw_w, v_conf_dw_b=v_conf_dw_b, v_conf_ln_g=v_conf_ln_g, v_conf_ln_b=v_conf_ln_b, v_sgu_ln_g=v_sgu_ln_g, v_sgu_ln_b=v_sgu_ln_b, v_sgu_ws=v_sgu_ws, v_sgu_b=v_sgu_b, v_w_branch=v_w_branch, v_w_out=v_w_out, v_w_ff1=v_w_ff1, v_w_ff2=v_w_ff2)
    weights = {n: given[n] for n in TWIN_WEIGHTS}
    shared = {n: given[n] for n in SHARED_INPUTS}
    per_example = {n: given[n] for n in ['x']}
    grad_fn = _jax.value_and_grad(_loss, argnums=(0, 1))

    def one_microbatch(ex, loss_target):
        ex = dict(ex)
        diff = ex.pop(TWIN_DIFF_INPUT)
        return grad_fn(weights, diff, {**shared, **ex}, loss_target)

    if N_MICROBATCH == 1:
        loss, (grad_w, grad_x) = one_microbatch(per_example, given["loss_target"])
    else:
        def body(carry, xs):
            loss_sum, grad_sum = carry
            l_k, (gw_k, gx_k) = one_microbatch(xs[0], xs[1])
            with _jax.named_scope("update"):
                return (loss_sum + l_k, _jax.tree.map(_jnp.add, grad_sum, gw_k)), gx_k

        init = (_jnp.zeros((), _jnp.float32), _jax.tree.map(_jnp.zeros_like, weights))
        (loss, grad_w), grad_x = _jax.lax.scan(body, init, (per_example, given["loss_target"]))
    with _jax.named_scope("update"):
        delta_w, new_m, new_v = {}, {}, {}
        for n in TWIN_WEIGHTS:
            delta_w[n], new_m[n], new_v[n] = _adamw(weights[n], grad_w[n], given["m_" + n], given["v_" + n])
    return (loss, grad_x, *[grad_w[n] for n in TWIN_WEIGHTS], *[delta_w[n] for n in TWIN_WEIGHTS],
            *[new_m[n] for n in TWIN_WEIGHTS], *[new_v[n] for n in TWIN_WEIGHTS])
```

```python
import functools

import jax
import jax.numpy as jnp
from jax import lax
from jax.experimental import pallas as pl
from jax.experimental.pallas import tpu as pltpu

D = 1024
HEADS = 8
CHUNK = 128
KA = 3
KC = 31
HALO = 32
NSH = 4
NDEV = 8
EPS = 1e-6
BF = jnp.bfloat16
F32 = jnp.float32
VMEM_LIMIT = 56 * 1024 * 1024

ADAM_LR = 0.001
ADAM_B1 = 0.9
ADAM_B2 = 0.999
ADAM_EPS = 1e-08
ADAM_WD = 0.01
ADAM_STEP = 10

MESH = pl.DeviceIdType.MESH
ANY = pl.BlockSpec(memory_space=pl.ANY)


def _cp(*sem):
    return pltpu.CompilerParams(dimension_semantics=sem, vmem_limit_bytes=VMEM_LIMIT)


def _sig(x):
    return 1.0 / (1.0 + jnp.exp(-x))


_GC = 0.7978845608028654


def _gelu(x):
    x2 = x * x
    t = jnp.tanh(_GC * x * (1.0 + 0.044715 * x2))
    y = 0.5 * x * (1.0 + t)
    dy = 0.5 * (1.0 + t) + 0.5 * x * (1.0 - t * t) * _GC * (1.0 + 3.0 * 0.044715 * x2)
    return y, dy


def _rms_fwd(x, g):
    r = lax.rsqrt(jnp.mean(x * x, axis=-1, keepdims=True) + EPS)
    return x * r * g


def _rms_bwd(dy, x, g):
    r = lax.rsqrt(jnp.mean(x * x, axis=-1, keepdims=True) + EPS)
    xn = x * r
    dyg = dy * g
    dx = r * (dyg - xn * jnp.mean(dyg * xn, axis=-1, keepdims=True))
    return dx, jnp.sum(dy * xn, axis=0, keepdims=True)


def _ln_stats(x):
    mu = jnp.mean(x, axis=-1, keepdims=True)
    xc = x - mu
    r = lax.rsqrt(jnp.mean(xc * xc, axis=-1, keepdims=True) + EPS)
    return xc * r, r


def _ln_bwd(dn, n, r):
    return r * (dn - jnp.mean(dn, axis=-1, keepdims=True) - n * jnp.mean(dn * n, axis=-1, keepdims=True))


def _dot(a, b):
    return jnp.dot(a, b, preferred_element_type=F32)


def _dot_nt(a, b):
    return lax.dot_general(a, b, (((1,), (1,)), ((), ())), preferred_element_type=F32)


def _dot_tn(a, b):
    return lax.dot_general(a, b, (((0,), (0,)), ((), ())), preferred_element_type=F32)


def _in_proj(x, g, w):
    T = x.shape[0]
    nc = w.shape[2]
    tm = min(T, 1024)
    tn = 1280
    nj = nc // tn

    def body(x_ref, g_ref, w_ref, h_ref, z_ref, h_scr):
        @pl.when((pl.program_id(1) == 0) & (pl.program_id(2) == 0))
        def _():
            h = _rms_fwd(x_ref[...], g_ref[...]).astype(BF)
            h_scr[...] = h
            h_ref[...] = h
        z_ref[...] = _dot(h_scr[...], w_ref[...]).astype(BF)

    return pl.pallas_call(
        body, name="in_proj", grid=(T // tm, NSH, nj),
        in_specs=[pl.BlockSpec((tm, D), lambda i, k, j: (i, 0)),
                  pl.BlockSpec((1, D), lambda i, k, j: (0, 0)),
                  pl.BlockSpec((None, D, tn), lambda i, k, j: (k, 0, j))],
        out_specs=[pl.BlockSpec((tm, D), lambda i, k, j: (i, 0)),
                   pl.BlockSpec((tm, tn), lambda i, k, j: (i, k * nj + j))],
        out_shape=[jax.ShapeDtypeStruct((T, D), BF), jax.ShapeDtypeStruct((T, NSH * nc), BF)],
        scratch_shapes=[pltpu.VMEM((tm, D), BF)],
        compiler_params=_cp("arbitrary", "arbitrary", "arbitrary"),
    )(x, g, w)


def _tile_specs(tt, nt_total, reverse):
    def tile(i):
        return (nt_total - 1 - i) if reverse else i

    def cur(c):
        return pl.BlockSpec((tt, D), lambda i, *_: (tile(i), c))

    def halo(c):
        return pl.BlockSpec((HALO, D), lambda i, *_: (jnp.maximum(tile(i) * (tt // HALO) - 1, 0), c))

    def row(r=1):
        return pl.BlockSpec((r, D), lambda i, *_: (0, 0))

    return tile, cur, halo, row


def _causal_conv(ext, w_ref, ntap, tt):
    acc = None
    for k in range(ntap):
        term = w_ref[k:k + 1, :] * ext[pl.ds(HALO - (ntap - 1) + k, tt), :]
        acc = term if acc is None else acc + term
    return acc


def _anticausal_conv(ext, w_ref, ntap, tt):
    acc = None
    for k in range(ntap):
        term = w_ref[k:k + 1, :] * ext[pl.ds(ntap - 1 - k, tt), :]
        acc = term if acc is None else acc + term
    return acc


def _conv_wgrad(dw_ref, dout, ext, ntap, tt):
    for k in range(ntap):
        dw_ref[k:k + 1, :] += jnp.sum(dout * ext[pl.ds(HALO - (ntap - 1) + k, tt), :], axis=0, keepdims=True)


def _mix_a_fwd(z, wa, S):
    T = z.shape[0]
    tt = min(S, 256)
    nt = S // tt
    _, cur, halo, row = _tile_specs(tt, T // tt, False)

    def body(ah, ab, ac, ah_h, ac_h, w_ref, y_ref, ext):
        first = (pl.program_id(0) % nt) == 0
        ph = ah_h[...].astype(F32) * ac_h[...].astype(F32)
        ext[0:HALO, :] = jnp.where(first, 0.0, ph)
        ext[HALO:, :] = ah[...].astype(F32) * ac[...].astype(F32)
        q = _causal_conv(ext, w_ref, KA, tt)
        y_ref[...] = (ab[...].astype(F32) * q).astype(BF)

    return pl.pallas_call(
        body, name="mix_a_fwd", grid=(T // tt,),
        in_specs=[cur(0), cur(1), cur(2), halo(0), halo(2), row(KA)],
        out_specs=pl.BlockSpec((tt, D), lambda i: (i, 0)),
        out_shape=jax.ShapeDtypeStruct((T, D), BF),
        scratch_shapes=[pltpu.VMEM((HALO + tt, D), F32)],
        compiler_params=_cp("arbitrary"),
    )(z, z, z, z, z, wa)


def _mix_b_fwd(z, wc, bc, lg, lb, S):
    T = z.shape[0]
    tt = min(S, 256)
    nt = S // tt
    _, cur, halo, row = _tile_specs(tt, T // tt, False)

    def body(ca, cg, ca_h, cg_h, w_ref, bc_ref, lg_ref, lb_ref, y_ref, s_ref, ext):
        first = (pl.program_id(0) % nt) == 0
        rh = ca_h[...].astype(F32) * _sig(cg_h[...].astype(F32))
        ext[0:HALO, :] = jnp.where(first, 0.0, rh)
        ext[HALO:, :] = ca[...].astype(F32) * _sig(cg[...].astype(F32))
        s = _causal_conv(ext, w_ref, KC, tt) + bc_ref[...]
        s_ref[...] = s.astype(BF)
        n, _ = _ln_stats(s)
        t = n * lg_ref[...] + lb_ref[...]
        y_ref[...] = (t * _sig(t)).astype(BF)

    return pl.pallas_call(
        body, name="mix_b_fwd", grid=(T // tt,),
        in_specs=[cur(3), cur(4), halo(3), halo(4), row(KC), row(), row(), row()],
        out_specs=[pl.BlockSpec((tt, D), lambda i: (i, 0))] * 2,
        out_shape=[jax.ShapeDtypeStruct((T, D), BF)] * 2,
        scratch_shapes=[pltpu.VMEM((HALO + tt, D), F32)],
        compiler_params=_cp("arbitrary"),
    )(z, z, z, z, wc, bc, lg, lb)


def _causal_mask(transposed):
    r = lax.broadcasted_iota(jnp.int32, (CHUNK, CHUNK), 0)
    c = lax.broadcasted_iota(jnp.int32, (CHUNK, CHUNK), 1)
    return (c >= r) if transposed else (r >= c)


def _mix_s_fwd(z, lg, lb, ws, bst, S):
    T = z.shape[0]
    tt = min(S, 256)
    _, cur, _, row = _tile_specs(tt, T // tt, False)

    def body(su, sv, lg_ref, lb_ref, ws_ref, bst_ref, y_ref, u_scr, vn_scr):
        u_scr[...] = _gelu(su[...].astype(F32))[0]
        n, _ = _ln_stats(_gelu(sv[...].astype(F32))[0])
        vn_scr[...] = (n * lg_ref[...] + lb_ref[...]).astype(BF)
        mask = _causal_mask(False)
        for h in range(HEADS):
            wm = jnp.where(mask, ws_ref[h], 0.0).astype(BF)
            cols = slice(h * CHUNK, (h + 1) * CHUNK)
            for c in range(tt // CHUNK):
                rows = slice(c * CHUNK, (c + 1) * CHUNK)
                mixed = _dot(wm, vn_scr[rows, cols]) + bst_ref[:, h:h + 1]
                y_ref[rows, cols] = (u_scr[rows, cols] * mixed).astype(BF)

    return pl.pallas_call(
        body, name="mix_s_fwd", grid=(T // tt,),
        in_specs=[cur(5), cur(6), row(), row(),
                  pl.BlockSpec((HEADS, CHUNK, CHUNK), lambda i: (0, 0, 0)),
                  pl.BlockSpec((CHUNK, HEADS), lambda i: (0, 0))],
        out_specs=pl.BlockSpec((tt, D), lambda i: (i, 0)),
        out_shape=jax.ShapeDtypeStruct((T, D), BF),
        scratch_shapes=[pltpu.VMEM((tt, D), F32), pltpu.VMEM((tt, D), BF)],
        compiler_params=_cp("arbitrary"),
    )(z, z, lg, lb, ws, bst)


def _mix_out_fwd(ya, yc, ys, z, x, wb, wo, gp):
    T = x.shape[0]
    tm = min(T, 256)
    rk = D // NSH

    def body(ya_ref, yc_ref, ys_ref, ga, gc, gs, x_ref, wb_ref, wo_ref, gp_ref, p_ref, mg_ref, m_ref, x1_ref):
        acc = None
        for b, (y_ref, g_ref) in enumerate(((ya_ref, ga), (yc_ref, gc), (ys_ref, gs))):
            pb = None
            for k in range(NSH):
                part = _dot(y_ref[:, k * rk:(k + 1) * rk], wb_ref[k, b])
                pb = part if pb is None else pb + part
            p_ref[b] = pb.astype(BF)
            term = _sig(g_ref[...].astype(F32)) * pb
            acc = term if acc is None else acc + term
        mg = acc.astype(BF)
        mg_ref[...] = mg
        m = _dot(mg, wo_ref[...])
        m_ref[...] = m.astype(BF)
        x1_ref[...] = x_ref[...] + _rms_fwd(m, gp_ref[...])

    rowblk = pl.BlockSpec((tm, D), lambda i: (i, 0))
    return pl.pallas_call(
        body, name="mix_out_fwd", grid=(T // tm,),
        in_specs=[rowblk, rowblk, rowblk,
                  pl.BlockSpec((tm, D), lambda i: (i, 7)), pl.BlockSpec((tm, D), lambda i: (i, 8)),
                  pl.BlockSpec((tm, D), lambda i: (i, 9)), rowblk,
                  pl.BlockSpec((NSH, 3, rk, D), lambda i: (0, 0, 0, 0)),
                  pl.BlockSpec((D, D), lambda i: (0, 0)),
                  pl.BlockSpec((1, D), lambda i: (0, 0))],
        out_specs=[pl.BlockSpec((3, tm, D), lambda i: (0, i, 0)), rowblk, rowblk, rowblk],
        out_shape=[jax.ShapeDtypeStruct((3, T, D), BF), jax.ShapeDtypeStruct((T, D), BF),
                   jax.ShapeDtypeStruct((T, D), BF), jax.ShapeDtypeStruct((T, D), F32)],
        compiler_params=_cp("arbitrary"),
    )(ya, yc, ys, z, z, z, x, wb, wo, gp)


def _ffn_fwd(x1, g3, w1, w2, g4):
    T = x1.shape[0]
    tm = min(T, 512)

    def body(x_ref, g3_ref, w1_ref, w2_ref, g4_ref, h_ref, a_ref, f_ref, x2_ref, h_scr, acc):
        k = pl.program_id(1)

        @pl.when(k == 0)
        def _():
            h = _rms_fwd(x_ref[...], g3_ref[...]).astype(BF)
            h_scr[...] = h
            h_ref[...] = h
            acc[...] = jnp.zeros_like(acc)

        a = _dot(h_scr[...], w1_ref[...])
        a_ref[...] = a.astype(BF)
        r = jnp.maximum(a, 0.0)
        acc[...] += _dot((r * r).astype(BF), w2_ref[...])

        @pl.when(k == NSH - 1)
        def _():
            f = acc[...]
            f_ref[...] = f.astype(BF)
            x2_ref[...] = x_ref[...] + _rms_fwd(f, g4_ref[...])

    rowblk = pl.BlockSpec((tm, D), lambda i, k: (i, 0))
    vec = pl.BlockSpec((1, D), lambda i, k: (0, 0))
    return pl.pallas_call(
        body, name="ffn_fwd", grid=(T // tm, NSH),
        in_specs=[rowblk, vec, pl.BlockSpec((None, D, D), lambda i, k: (k, 0, 0)),
                  pl.BlockSpec((D, D), lambda i, k: (k, 0)), vec],
        out_specs=[rowblk, pl.BlockSpec((tm, D), lambda i, k: (i, k)), rowblk, rowblk],
        out_shape=[jax.ShapeDtypeStruct((T, D), BF), jax.ShapeDtypeStruct((T, NSH * D), BF),
                   jax.ShapeDtypeStruct((T, D), BF), jax.ShapeDtypeStruct((T, D), F32)],
        scratch_shapes=[pltpu.VMEM((tm, D), BF), pltpu.VMEM((tm, D), F32)],
        compiler_params=_cp("arbitrary", "arbitrary"),
    )(x1, g3, w1, w2, g4)


def _loss_head(y, target):
    T = y.shape[0]
    tm = min(T, 512)

    def body(y_ref, t_ref, dy_ref, l_ref):
        @pl.when(pl.program_id(0) == 0)
        def _():
            l_ref[...] = jnp.zeros_like(l_ref)
        e = y_ref[...] - t_ref[...]
        dy_ref[...] = e * (1.0 / D)
        l_ref[...] += jnp.sum(e * e) * (0.5 / D)

    rowblk = pl.BlockSpec((tm, D), lambda i: (i, 0))
    return pl.pallas_call(
        body, name="loss_head", grid=(T // tm,),
        in_specs=[rowblk, rowblk],
        out_specs=[rowblk, pl.BlockSpec((1, 128), lambda i: (0, 0))],
        out_shape=[jax.ShapeDtypeStruct((T, D), F32), jax.ShapeDtypeStruct((1, 128), F32)],
        compiler_params=_cp("arbitrary"),
    )(y, target)


def _ffn_bwd(dx2, f, g4, a, w2, w1, x1, g3):
    T = dx2.shape[0]
    tm = min(T, 512)

    def body(dx2_ref, f_ref, g4_ref, a_ref, w2_ref, w1_ref, x1_ref, g3_ref,
             df_ref, da_ref, dx1_ref, dg4_ref, dg3_ref, df_scr, acc):
        i, k = pl.program_id(0), pl.program_id(1)

        @pl.when((i == 0) & (k == 0))
        def _():
            dg4_ref[...] = jnp.zeros_like(dg4_ref)
            dg3_ref[...] = jnp.zeros_like(dg3_ref)

        @pl.when(k == 0)
        def _():
            df, dg = _rms_bwd(dx2_ref[...], f_ref[...].astype(F32), g4_ref[...])
            dg4_ref[...] += dg
            dfb = df.astype(BF)
            df_scr[...] = dfb
            df_ref[...] = dfb
            acc[...] = jnp.zeros_like(acc)

        av = a_ref[...].astype(F32)
        da = (_dot_nt(df_scr[...], w2_ref[...]) * (2.0 * jnp.maximum(av, 0.0))).astype(BF)
        da_ref[...] = da
        acc[...] += _dot_nt(da, w1_ref[...])

        @pl.when(k == NSH - 1)
        def _():
            dx, dg = _rms_bwd(acc[...], x1_ref[...], g3_ref[...])
            dg3_ref[...] += dg
            dx1_ref[...] = dx2_ref[...] + dx

    rowblk = pl.BlockSpec((tm, D), lambda i, k: (i, 0))
    vec = pl.BlockSpec((1, D), lambda i, k: (0, 0))
    return pl.pallas_call(
        body, name="ffn_bwd", grid=(T // tm, NSH),
        in_specs=[rowblk, rowblk, vec, pl.BlockSpec((tm, D), lambda i, k: (i, k)),
                  pl.BlockSpec((D, D), lambda i, k: (k, 0)),
                  pl.BlockSpec((None, D, D), lambda i, k: (k, 0, 0)), rowblk, vec],
        out_specs=[rowblk, pl.BlockSpec((tm, D), lambda i, k: (i, k)), rowblk, vec, vec],
        out_shape=[jax.ShapeDtypeStruct((T, D), BF), jax.ShapeDtypeStruct((T, NSH * D), BF),
                   jax.ShapeDtypeStruct((T, D), F32), jax.ShapeDtypeStruct((1, D), F32),
                   jax.ShapeDtypeStruct((1, D), F32)],
        scratch_shapes=[pltpu.VMEM((tm, D), BF), pltpu.VMEM((tm, D), F32)],
        compiler_params=_cp("arbitrary", "arbitrary"),
    )(dx2, f, g4, a, w2, w1, x1, g3)


def _wgrad(name, ops, grid, in_specs, out_spec, out_shape, acc_shape, pick=None, relu2=False):
    nt = grid[-1]
    na = len(ops) - 1

    def body(*refs):
        a_refs, b_ref, o_ref, acc = refs[:na], refs[na], refs[na + 1], refs[na + 2]
        t = pl.program_id(len(grid) - 1)

        @pl.when(t == 0)
        def _():
            acc[...] = jnp.zeros_like(acc)

        def add(a_ref):
            av = a_ref[...]
            if relu2:
                r = jnp.maximum(av.astype(F32), 0.0)
                av = (r * r).astype(BF)
            acc[...] += _dot_tn(av, b_ref[...])

        if na == 1:
            add(a_refs[0])
        else:
            sel = pick()
            for n in range(na):
                pl.when(sel == n)(functools.partial(add, a_refs[n]))

        @pl.when(t == nt - 1)
        def _():
            o_ref[...] = acc[...].astype(o_ref.dtype)

    return pl.pallas_call(
        body, name=name, grid=grid, in_specs=in_specs, out_specs=out_spec, out_shape=out_shape,
        scratch_shapes=[pltpu.VMEM(acc_shape, F32)],
        compiler_params=_cp(*(["arbitrary"] * len(grid))),
    )(*ops)


def _mix_out_bwd(dx1, m, gp, wo, p3, z, wb):
    T = dx1.shape[0]
    tm = min(T, 256)
    rk = D // NSH

    def body(dx1_ref, m_ref, gp_ref, wo_ref, p_ref, g_ref, wb_ref,
             dm_ref, dp_ref, dy_ref, dz_ref, dgp_ref, dmg):
        i, b = pl.program_id(0), pl.program_id(1)

        @pl.when((i == 0) & (b == 0))
        def _():
            dgp_ref[...] = jnp.zeros_like(dgp_ref)

        @pl.when(b == 0)
        def _():
            dm, dg = _rms_bwd(dx1_ref[...], m_ref[...].astype(F32), gp_ref[...])
            dgp_ref[...] += dg
            dmb = dm.astype(BF)
            dm_ref[...] = dmb
            dmg[...] = _dot_nt(dmb, wo_ref[...])

        gate = _sig(g_ref[...].astype(F32))
        d = dmg[...]
        dp = (d * gate).astype(BF)
        dp_ref[...] = dp
        dz_ref[...] = (d * p_ref[...].astype(F32) * gate * (1.0 - gate)).astype(BF)
        for k in range(NSH):
            dy_ref[:, k * rk:(k + 1) * rk] = _dot_nt(dp, wb_ref[k]).astype(BF)

    rowblk = pl.BlockSpec((tm, D), lambda i, b: (i, 0))
    br = pl.BlockSpec((None, tm, D), lambda i, b: (b, i, 0))
    vec = pl.BlockSpec((1, D), lambda i, b: (0, 0))
    return pl.pallas_call(
        body, name="mix_out_bwd", grid=(T // tm, 3),
        in_specs=[rowblk, rowblk, vec, pl.BlockSpec((D, D), lambda i, b: (0, 0)), br,
                  pl.BlockSpec((tm, D), lambda i, b: (i, 7 + b)),
                  pl.BlockSpec((NSH, None, rk, D), lambda i, b: (0, b, 0, 0))],
        out_specs=[rowblk, br, br, pl.BlockSpec((tm, D), lambda i, b: (i, 7 + b)), vec],
        out_shape=[jax.ShapeDtypeStruct((T, D), BF), jax.ShapeDtypeStruct((3, T, D), BF),
                   jax.ShapeDtypeStruct((3, T, D), BF), jax.ShapeDtypeStruct((T, 10 * D), BF),
                   jax.ShapeDtypeStruct((1, D), F32)],
        scratch_shapes=[pltpu.VMEM((tm, D), F32)],
        compiler_params=_cp("arbitrary", "arbitrary"),
    )(dx1, m, gp, wo, p3, z, wb)


def _mix_a_bwd(dz, dy3, z, wa, S):
    T = z.shape[0]
    tt = min(S, 256)
    nt = S // tt
    ntt = T // tt
    tile, cur, halo, row = _tile_specs(tt, ntt, True)

    def body(dz_in, dy_ref, ah, ab, ac, ah_h, ac_h, w_ref, dz_ref, dw_ref, ext_p, ext_d, stage):
        i, b = pl.program_id(0), pl.program_id(1)
        ti = ntt - 1 - i

        @pl.when((i == 0) & (b == 0))
        def _():
            dw_ref[...] = jnp.zeros_like(dw_ref)
            ext_d[...] = jnp.zeros_like(ext_d)

        @pl.when(b == 0)
        def _():
            first = (ti % nt) == 0
            last = (ti % nt) == nt - 1
            ahv, acv, abv = ah[...].astype(F32), ac[...].astype(F32), ab[...].astype(F32)
            ext_p[0:HALO, :] = jnp.where(first, 0.0, ah_h[...].astype(F32) * ac_h[...].astype(F32))
            ext_p[HALO:, :] = ahv * acv
            q = _causal_conv(ext_p, w_ref, KA, tt)
            dy = dy_ref[...].astype(F32)
            dq = dy * abv
            stage[1] = (dy * q).astype(BF)
            _conv_wgrad(dw_ref, dq, ext_p, KA, tt)
            ext_d[tt:, :] = jnp.where(last, 0.0, ext_d[0:HALO, :])
            ext_d[0:tt, :] = dq
            dp = _anticausal_conv(ext_d, w_ref, KA, tt)
            stage[0] = (dp * acv).astype(BF)
            stage[2] = (dp * ahv).astype(BF)

        dz_ref[...] = stage[b]

    return pl.pallas_call(
        body, name="mix_a_bwd", grid=(ntt, 3),
        in_specs=[ANY, pl.BlockSpec((None, tt, D), lambda i, b: (0, tile(i), 0)),
                  cur(0), cur(1), cur(2), halo(0), halo(2), row(KA)],
        out_specs=[pl.BlockSpec((tt, D), lambda i, b: (tile(i), b)), pl.BlockSpec((KA, D), lambda i, b: (0, 0))],
        out_shape=[jax.ShapeDtypeStruct(dz.shape, BF), jax.ShapeDtypeStruct((KA, D), F32)],
        scratch_shapes=[pltpu.VMEM((HALO + tt, D), F32), pltpu.VMEM((tt + HALO, D), F32),
                        pltpu.VMEM((3, tt, D), BF)],
        input_output_aliases={0: 0},
        compiler_params=_cp("arbitrary", "arbitrary"),
    )(dz, dy3, z, z, z, z, z, wa)


def _mix_b_bwd(dz, dy3, s, z, wc, lg, lb, S):
    T = z.shape[0]
    tt = min(S, 256)
    nt = S // tt
    ntt = T // tt
    tile, cur, halo, row = _tile_specs(tt, ntt, True)

    def body(dz_in, dy_ref, s_ref, ca, cg, ca_h, cg_h, w_ref, lg_ref, lb_ref,
             dz_ref, dw_ref, dbc_ref, dlg_ref, dlb_ref, ext_r, ext_d, stage):
        i, b = pl.program_id(0), pl.program_id(1)
        ti = ntt - 1 - i

        @pl.when((i == 0) & (b == 0))
        def _():
            dw_ref[...] = jnp.zeros_like(dw_ref)
            dbc_ref[...] = jnp.zeros_like(dbc_ref)
            dlg_ref[...] = jnp.zeros_like(dlg_ref)
            dlb_ref[...] = jnp.zeros_like(dlb_ref)
            ext_d[...] = jnp.zeros_like(ext_d)

        @pl.when(b == 0)
        def _():
            first = (ti % nt) == 0
            last = (ti % nt) == nt - 1
            n, r = _ln_stats(s_ref[...].astype(F32))
            t = n * lg_ref[...] + lb_ref[...]
            sg = _sig(t)
            dt = dy_ref[...].astype(F32) * (sg * (1.0 + t * (1.0 - sg)))
            dlg_ref[...] += jnp.sum(dt * n, axis=0, keepdims=True)
            dlb_ref[...] += jnp.sum(dt, axis=0, keepdims=True)
            ds = _ln_bwd(dt * lg_ref[...], n, r)
            dbc_ref[...] += jnp.sum(ds, axis=0, keepdims=True)
            cav = ca[...].astype(F32)
            sgc = _sig(cg[...].astype(F32))
            ext_r[0:HALO, :] = jnp.where(first, 0.0, ca_h[...].astype(F32) * _sig(cg_h[...].astype(F32)))
            ext_r[HALO:, :] = cav * sgc
            _conv_wgrad(dw_ref, ds, ext_r, KC, tt)
            ext_d[tt:, :] = jnp.where(last, 0.0, ext_d[0:HALO, :])
            ext_d[0:tt, :] = ds
            dr = _anticausal_conv(ext_d, w_ref, KC, tt)
            stage[0] = (dr * sgc).astype(BF)
            stage[1] = (dr * cav * sgc * (1.0 - sgc)).astype(BF)

        dz_ref[...] = stage[b]

    vec = pl.BlockSpec((1, D), lambda i, b: (0, 0))
    return pl.pallas_call(
        body, name="mix_b_bwd", grid=(ntt, 2),
        in_specs=[ANY, pl.BlockSpec((None, tt, D), lambda i, b: (1, tile(i), 0)),
                  pl.BlockSpec((tt, D), lambda i, b: (tile(i), 0)),
                  cur(3), cur(4), halo(3), halo(4), row(KC), row(), row()],
        out_specs=[pl.BlockSpec((tt, D), lambda i, b: (tile(i), 3 + b)),
                   pl.BlockSpec((KC, D), lambda i, b: (0, 0)), vec, vec, vec],
        out_shape=[jax.ShapeDtypeStruct(dz.shape, BF), jax.ShapeDtypeStruct((KC, D), F32)]
        + [jax.ShapeDtypeStruct((1, D), F32)] * 3,
        scratch_shapes=[pltpu.VMEM((HALO + tt, D), F32), pltpu.VMEM((tt + HALO, D), F32),
                        pltpu.VMEM((2, tt, D), BF)],
        input_output_aliases={0: 0},
        compiler_params=_cp("arbitrary", "arbitrary"),
    )(dz, dy3, s, z, z, z, z, wc, lg, lb)


def _mix_s_bwd(dz, dy3, z, lg, lb, ws, wst, bst, S):
    T = z.shape[0]
    tt = min(S, 256)
    ntt = T // tt
    _, cur, _, row = _tile_specs(tt, ntt, False)

    def body(dz_in, dy_ref, su, sv, lg_ref, lb_ref, ws_ref, wst_ref, bst_ref,
             dz_ref, dws_ref, dbst_ref, dlg_ref, dlb_ref, u_scr, vn_scr, dvn_scr, stage):
        i, b = pl.program_id(0), pl.program_id(1)

        @pl.when((i == 0) & (b == 0))
        def _():
            dws_ref[...] = jnp.zeros_like(dws_ref)
            dbst_ref[...] = jnp.zeros_like(dbst_ref)
            dlg_ref[...] = jnp.zeros_like(dlg_ref)
            dlb_ref[...] = jnp.zeros_like(dlb_ref)

        @pl.when(b == 0)
        def _():
            u, du_dx = _gelu(su[...].astype(F32))
            v, dv_dx = _gelu(sv[...].astype(F32))
            u_scr[...] = u
            n, r = _ln_stats(v)
            vn_scr[...] = (n * lg_ref[...] + lb_ref[...]).astype(BF)
            mask = _causal_mask(False)
            mask_t = _causal_mask(True)
            for h in range(HEADS):
                wm = jnp.where(mask, ws_ref[h], 0.0).astype(BF)
                wmt = jnp.where(mask_t, wst_ref[h], 0.0).astype(BF)
                cols = slice(h * CHUNK, (h + 1) * CHUNK)
                for c in range(tt // CHUNK):
                    rows = slice(c * CHUNK, (c + 1) * CHUNK)
                    vb = vn_scr[rows, cols]
                    mixed = _dot(wm, vb) + bst_ref[:, h:h + 1]
                    dy = dy_ref[rows, cols].astype(F32)
                    dmix = dy * u_scr[rows, cols]
                    u_scr[rows, cols] = dy * mixed
                    dbst_ref[:, h:h + 1] += jnp.sum(dmix, axis=1, keepdims=True)
                    dmb = dmix.astype(BF)
                    dws_ref[h] += _dot_nt(dmb, vb)
                    dvn_scr[rows, cols] = _dot(wmt, dmb)
            stage[0] = (u_scr[...] * du_dx).astype(BF)
            dvn = dvn_scr[...]
            dlg_ref[...] += jnp.sum(dvn * n, axis=0, keepdims=True)
            dlb_ref[...] += jnp.sum(dvn, axis=0, keepdims=True)
            stage[1] = (_ln_bwd(dvn * lg_ref[...], n, r) * dv_dx).astype(BF)

        dz_ref[...] = stage[b]

    vec = pl.BlockSpec((1, D), lambda i, b: (0, 0))
    wsp = pl.BlockSpec((HEADS, CHUNK, CHUNK), lambda i, b: (0, 0, 0))
    bsp = pl.BlockSpec((CHUNK, HEADS), lambda i, b: (0, 0))
    return pl.pallas_call(
        body, name="mix_s_bwd", grid=(ntt, 2),
        in_specs=[ANY, pl.BlockSpec((None, tt, D), lambda i, b: (2, i, 0)),
                  cur(5), cur(6), row(), row(), wsp, wsp, bsp],
        out_specs=[pl.BlockSpec((tt, D), lambda i, b: (i, 5 + b)), wsp, bsp, vec, vec],
        out_shape=[jax.ShapeDtypeStruct(dz.shape, BF), jax.ShapeDtypeStruct((HEADS, CHUNK, CHUNK), F32),
                   jax.ShapeDtypeStruct((CHUNK, HEADS), F32), jax.ShapeDtypeStruct((1, D), F32),
                   jax.ShapeDtypeStruct((1, D), F32)],
        scratch_shapes=[pltpu.VMEM((tt, D), F32), pltpu.VMEM((tt, D), BF), pltpu.VMEM((tt, D), F32),
                        pltpu.VMEM((2, tt, D), BF)],
        input_output_aliases={0: 0},
        compiler_params=_cp("arbitrary", "arbitrary"),
    )(dz, dy3, z, z, lg, lb, ws, wst, bst)


def _in_proj_bwd(dz, w, x, g, dx1):
    T = x.shape[0]
    nc = w.shape[2]
    tm = min(T, 512)
    tn = 1280
    nj = nc // tn

    def body(dz_ref, w_ref, x_ref, g_ref, dx1_ref, dx_ref, dg_ref, acc):
        i, k, j = pl.program_id(0), pl.program_id(1), pl.program_id(2)

        @pl.when((i == 0) & (k == 0) & (j == 0))
        def _():
            dg_ref[...] = jnp.zeros_like(dg_ref)

        @pl.when((k == 0) & (j == 0))
        def _():
            acc[...] = jnp.zeros_like(acc)

        acc[...] += _dot_nt(dz_ref[...], w_ref[...])

        @pl.when((k == NSH - 1) & (j == nj - 1))
        def _():
            dx, dg = _rms_bwd(acc[...], x_ref[...], g_ref[...])
            dg_ref[...] += dg
            dx_ref[...] = dx1_ref[...] + dx

    rowblk = pl.BlockSpec((tm, D), lambda i, k, j: (i, 0))
    vec = pl.BlockSpec((1, D), lambda i, k, j: (0, 0))
    return pl.pallas_call(
        body, name="in_proj_bwd", grid=(T // tm, NSH, nj),
        in_specs=[pl.BlockSpec((tm, tn), lambda i, k, j: (i, k * nj + j)),
                  pl.BlockSpec((None, D, tn), lambda i, k, j: (k, 0, j)), rowblk, vec, rowblk],
        out_specs=[rowblk, vec],
        out_shape=[jax.ShapeDtypeStruct((T, D), F32), jax.ShapeDtypeStruct((1, D), F32)],
        scratch_shapes=[pltpu.VMEM((tm, D), F32)],
        compiler_params=_cp("arbitrary", "arbitrary", "arbitrary"),
    )(dz, w, x, g, dx1)


def _layer_fwd(x, p, S):
    h, z = _in_proj(x, p["g_mix_pre"], p["w_in"])
    ya = _mix_a_fwd(z, p["conv_a_w"], S)
    yc, s = _mix_b_fwd(z, p["conf_dw_w"], p["conf_dw_b"], p["conf_ln_g"], p["conf_ln_b"], S)
    ys = _mix_s_fwd(z, p["sgu_ln_g"], p["sgu_ln_b"], p["sgu_ws"], p["sgu_bt"], S)
    p3, merged, m, x1 = _mix_out_fwd(ya, yc, ys, z, x, p["w_branch"], p["w_out"], p["g_mix_post"])
    h2, a, f, x2 = _ffn_fwd(x1, p["g_ffn_pre"], p["w_ff1"], p["w_ff2"], p["g_ffn_post"])
    saved = dict(x=x, h=h, z=z, ya=ya, yc=yc, ys=ys, s=s, p3=p3, merged=merged, m=m, x1=x1, h2=h2, a=a, f=f)
    return x2, saved


def _layer_bwd(dx2, p, sv, S):
    T = dx2.shape[0]
    bt = min(T, 512)
    nt = T // bt
    rk = D // NSH
    df, da, dx1, dg_ffn_post, dg_ffn_pre = _ffn_bwd(dx2, sv["f"], p["g_ffn_post"], sv["a"], p["w_ff2"],
                                                    p["w_ff1"], sv["x1"], p["g_ffn_pre"])
    dw_ff2 = _wgrad("wgrad_ff2", (sv["a"], df), (NSH, nt),
                    [pl.BlockSpec((bt, D), lambda k, t: (t, k)), pl.BlockSpec((bt, D), lambda k, t: (t, 0))],
                    pl.BlockSpec((None, D, D), lambda k, t: (k, 0, 0)),
                    jax.ShapeDtypeStruct((NSH, D, D), BF), (D, D), relu2=True)
    dw_ff1 = _wgrad("wgrad_ff1", (sv["h2"], da), (NSH, nt),
                    [pl.BlockSpec((bt, D), lambda k, t: (t, 0)), pl.BlockSpec((bt, D), lambda k, t: (t, k))],
                    pl.BlockSpec((None, D, D), lambda k, t: (k, 0, 0)),
                    jax.ShapeDtypeStruct((NSH, D, D), BF), (D, D))
    dm, dp3, dy3, dz, dg_mix_post = _mix_out_bwd(dx1, sv["m"], p["g_mix_post"], p["w_out"], sv["p3"], sv["z"],
                                                 p["w_branch"])
    dw_out = _wgrad("wgrad_out", (sv["merged"], dm), (nt,),
                    [pl.BlockSpec((bt, D), lambda t: (t, 0)), pl.BlockSpec((bt, D), lambda t: (t, 0))],
                    pl.BlockSpec((D, D), lambda t: (0, 0)),
                    jax.ShapeDtypeStruct((D, D), BF), (D, D)).reshape(NSH, rk, D)
    ysp = pl.BlockSpec((bt, rk), lambda b, k, t: (t, k))
    dw_br = _wgrad("wgrad_branch", (sv["ya"], sv["yc"], sv["ys"], dp3), (3, NSH, nt),
                   [ysp, ysp, ysp, pl.BlockSpec((None, bt, D), lambda b, k, t: (b, t, 0))],
                   pl.BlockSpec((None, None, rk, D), lambda b, k, t: (k, b, 0, 0)),
                   jax.ShapeDtypeStruct((NSH, 3, rk, D), BF), (rk, D), pick=lambda: pl.program_id(0))
    dz, dwa = _mix_a_bwd(dz, dy3, sv["z"], p["conv_a_w"], S)
    dz, dwc, dbc, dclg, dclb = _mix_b_bwd(dz, dy3, sv["s"], sv["z"], p["conf_dw_w"], p["conf_ln_g"],
                                          p["conf_ln_b"], S)
    dz, dws, dbst, dslg, dslb = _mix_s_bwd(dz, dy3, sv["z"], p["sgu_ln_g"], p["sgu_ln_b"], p["sgu_ws"],
                                           p["sgu_wst"], p["sgu_bt"], S)
    dx, dg_mix_pre = _in_proj_bwd(dz, p["w_in"], sv["x"], p["g_mix_pre"], dx1)
    tn = 1280
    nj = p["w_in"].shape[2] // tn
    dw_in = _wgrad("wgrad_in", (sv["h"], dz), (NSH, nj, nt),
                   [pl.BlockSpec((bt, D), lambda k, j, t: (t, 0)),
                    pl.BlockSpec((bt, tn), lambda k, j, t: (t, k * nj + j))],
                   pl.BlockSpec((None, D, tn), lambda k, j, t: (k, 0, j)),
                   jax.ShapeDtypeStruct(p["w_in"].shape, BF), (D, tn))
    tril = jnp.tril(jnp.ones((CHUNK, CHUNK), bool))
    small = dict(norm_mix_pre=dg_mix_pre, norm_mix_post=dg_mix_post, norm_ffn_pre=dg_ffn_pre,
                 norm_ffn_post=dg_ffn_post, conv_a_w=dwa, conf_dw_w=dwc, conf_dw_b=dbc, conf_ln_g=dclg,
                 conf_ln_b=dclb, sgu_ln_g=dslg, sgu_ln_b=dslb,
                 sgu_ws=jnp.where(tril[None], dws, 0.0), sgu_b=dbst.T)
    big = dict(w_in=dw_in, w_branch=dw_br, w_out=dw_out, w_ff1=dw_ff1, w_ff2=dw_ff2)
    return dx, big, small


SMALL_NAMES = ("norm_mix_pre", "norm_mix_post", "norm_ffn_pre", "norm_ffn_post", "conv_a_w", "conf_dw_w",
               "conf_dw_b", "conf_ln_g", "conf_ln_b", "sgu_ln_g", "sgu_ln_b", "sgu_b", "sgu_ws")
SMALL_ROWS = dict(norm_mix_pre=1, norm_mix_post=1, norm_ffn_pre=1, norm_ffn_post=1, conv_a_w=KA, conf_dw_w=KC,
                  conf_dw_b=1, conf_ln_g=1, conf_ln_b=1, sgu_ln_g=1, sgu_ln_b=1, sgu_b=1, sgu_ws=CHUNK)
SUBLANES = 8


def _pad8(r):
    return -(-r // SUBLANES) * SUBLANES


PACK_ROWS = sum(_pad8(r) for r in SMALL_ROWS.values())


def _pack_small(d):
    parts = []
    for n in SMALL_NAMES:
        r = SMALL_ROWS[n]
        parts.append(jnp.pad(d[n].reshape(r, D).astype(F32), ((0, _pad8(r) - r), (0, 0))))
    return jnp.concatenate(parts, axis=0)


def _unpack_small(a, shapes):
    out, r = {}, 0
    for n in SMALL_NAMES:
        out[n] = a[:, r:r + SMALL_ROWS[n]].reshape((a.shape[0],) + tuple(shapes[n]))
        r += _pad8(SMALL_ROWS[n])
    return out


def _local_step(x, target, layers, S):
    saved = []
    for p in layers:
        x, sv = _layer_fwd(x, p, S)
        saved.append(sv)
    dx, loss = _loss_head(x, target)
    bigs, smalls = [], []
    for p, sv in zip(reversed(layers), reversed(saved)):
        dx, big, small = _layer_bwd(dx, p, sv, S)
        bigs.append(big)
        smalls.append(_pack_small(small))
    return loss, dx, bigs[::-1], smalls[::-1]


def _me():
    return lax.axis_index("x"), lax.axis_index("y"), lax.axis_index("c")


def _slab(ref, q, a, h=None):
    r = ref.shape[1]
    rows = slice(None) if h is None else pl.ds(h * (r // 2), r // 2)
    return ref.at[pl.ds(q * a, a), rows, :]


def _rows(ref, h):
    r = ref.shape[-2]
    lead = (slice(None),) * (len(ref.shape) - 2)
    return ref.at[lead + (pl.ds(h * (r // 2), r // 2), slice(None))]


def _rcopy(src, dst, sems, idx, dev):
    return pltpu.make_async_remote_copy(src_ref=src, dst_ref=dst, send_sem=sems[0].at[idx], recv_sem=sems[1].at[idx],
                                        device_id=dev, device_id_type=MESH)


def _gather_weights(srcs, layer):
    n = len(srcs)

    def body(*refs):
        src, dst = refs[:n], refs[n:2 * n]
        sems = refs[2 * n:2 * n + 2]
        lsem = refs[2 * n + 2]
        x, y, c = _me()
        k = 2 * x + y
        chips = [(1 - x, y), (x, 1 - y), (1 - x, 1 - y)]
        av = [s.shape[1] for s in src]
        own = [pltpu.make_async_copy(src[i].at[layer], _slab(dst[i], k, av[i]), lsem.at[i]) for i in range(n)]
        for cp in own:
            cp.start()
        first = []
        for j, (qx, qy) in enumerate(chips):
            for i in range(n):
                first.append(_rcopy(_rows(src[i].at[layer], c), _slab(dst[i], k, av[i], c), sems, j * n + i,
                                    (qx, qy, c)))
        for cp in first:
            cp.start()
        passed = []
        for j, (qx, qy) in enumerate(chips):
            kq = 2 * qx + qy
            for i in range(n):
                got = _slab(dst[i], kq, av[i], c)
                _rcopy(got, got, sems, j * n + i, (x, y, c)).wait_recv()
                cp = _rcopy(got, got, sems, 3 * n + j * n + i, (x, y, 1 - c))
                cp.start()
                passed.append(cp)
        for j, (qx, qy) in enumerate(chips):
            kq = 2 * qx + qy
            for i in range(n):
                other = _slab(dst[i], kq, av[i], 1 - c)
                _rcopy(other, other, sems, 3 * n + j * n + i, (x, y, c)).wait_recv()
        for cp in first + passed:
            cp.wait_send()
        for cp in own:
            cp.wait()

    outs = [jax.ShapeDtypeStruct((NSH * s.shape[1],) + s.shape[2:], s.dtype) for s in srcs]
    return pl.pallas_call(
        body, name="gather_weights", in_specs=[ANY] * n, out_specs=[ANY] * n, out_shape=outs,
        scratch_shapes=[pltpu.SemaphoreType.DMA((6 * n,)), pltpu.SemaphoreType.DMA((6 * n,)),
                        pltpu.SemaphoreType.DMA((n,))],
    )(*srcs)


def _send_halves_to_sibling(parts):
    n = len(parts)

    def body(*refs):
        src, dst = refs[:n], refs[n:2 * n]
        sems = refs[2 * n:2 * n + 2]
        x, y, c = _me()
        cps = [_rcopy(_rows(src[i], 1 - c), dst[i], sems, i, (x, y, 1 - c)) for i in range(n)]
        for cp in cps:
            cp.start()
        for cp in cps:
            cp.wait()

    outs = [jax.ShapeDtypeStruct((p.shape[0], p.shape[1] // 2, p.shape[2]), p.dtype) for p in parts]
    return pl.pallas_call(
        body, name="pair_exchange", in_specs=[ANY] * n, out_specs=[ANY] * n, out_shape=outs,
        scratch_shapes=[pltpu.SemaphoreType.DMA((n,)), pltpu.SemaphoreType.DMA((n,))],
    )(*parts)


def _pair_add(part, sib, c):
    A, R, C = part.shape
    hr = R // 2
    br = min(hr, 512)
    nb = hr // br

    def body(c_ref, p_ref, s_ref, o_ref):
        o_ref[...] = (p_ref[...].astype(F32) + s_ref[...].astype(F32)).astype(BF)

    return pl.pallas_call(
        body, name="pair_add",
        grid_spec=pltpu.PrefetchScalarGridSpec(
            num_scalar_prefetch=1, grid=(A, nb),
            in_specs=[pl.BlockSpec((None, br, C), lambda a, i, c_ref: (a, c_ref[0] * nb + i, 0)),
                      pl.BlockSpec((None, br, C), lambda a, i, c_ref: (a, i, 0))],
            out_specs=pl.BlockSpec((None, br, C), lambda a, i, c_ref: (a, i, 0))),
        out_shape=jax.ShapeDtypeStruct((A, hr, C), BF),
        compiler_params=_cp("arbitrary", "arbitrary"),
    )(c, part, sib)


def _scatter_chip_sums(sums):
    n = len(sums)

    def body(*refs):
        src, dst = refs[:n], refs[n:2 * n]
        sems = refs[2 * n:2 * n + 2]
        lsem = refs[2 * n + 2]
        x, y, c = _me()
        k = 2 * x + y
        chips = [(1 - x, y), (x, 1 - y), (1 - x, 1 - y)]
        av = [s.shape[0] // NSH for s in src]
        own = [pltpu.make_async_copy(_slab(src[i], k, av[i]), _slab(dst[i], k, av[i]), lsem.at[i]) for i in range(n)]
        for cp in own:
            cp.start()
        cps = []
        for j, (qx, qy) in enumerate(chips):
            for i in range(n):
                cps.append(_rcopy(_slab(src[i], 2 * qx + qy, av[i]), _slab(dst[i], k, av[i]), sems, j * n + i,
                                  (qx, qy, c)))
        for cp in cps:
            cp.start()
        for j, (qx, qy) in enumerate(chips):
            for i in range(n):
                slot = _slab(dst[i], 2 * qx + qy, av[i])
                _rcopy(slot, slot, sems, j * n + i, (x, y, c)).wait_recv()
        for cp in cps:
            cp.wait_send()
        for cp in own:
            cp.wait()

    outs = [jax.ShapeDtypeStruct(s.shape, s.dtype) for s in sums]
    return pl.pallas_call(
        body, name="scatter_chip_sums", in_specs=[ANY] * n, out_specs=[ANY] * n, out_shape=outs,
        scratch_shapes=[pltpu.SemaphoreType.DMA((3 * n,)), pltpu.SemaphoreType.DMA((3 * n,)),
                        pltpu.SemaphoreType.DMA((n,))],
    )(*sums)


def _sum_chips(rcv, acc, layer, nlayers):
    A, hr, C = rcv.shape
    a = A // NSH
    br = min(hr, 512)
    nb = hr // br

    def body(*refs):
        r0, r1, r2, r3 = refs[:4]
        o_ref = refs[-1]
        o_ref[...] = ((r0[...].astype(F32) + r1[...].astype(F32)) + r2[...].astype(F32)) + r3[...].astype(F32)

    slot = lambda s: pl.BlockSpec((None, br, C), lambda e, i: (s * a + e, i, 0))
    ops = [rcv] * 4
    in_specs = [slot(0), slot(1), slot(2), slot(3)]
    aliases = {}
    if acc is not None:
        ops.append(acc)
        in_specs.append(ANY)
        aliases = {4: 0}
    return pl.pallas_call(
        body, name="sum_chips", grid=(a, nb), in_specs=in_specs,
        out_specs=pl.BlockSpec((None, None, br, C), lambda e, i: (layer, e, i, 0)),
        out_shape=jax.ShapeDtypeStruct((nlayers, a, hr, C), F32), input_output_aliases=aliases,
        compiler_params=_cp("arbitrary", "arbitrary"),
    )(*ops)


def _join_halves(halves):
    n = len(halves)

    def body(*refs):
        src, dst = refs[:n], refs[n:2 * n]
        sems = refs[2 * n:2 * n + 2]
        lsem = refs[2 * n + 2]
        x, y, c = _me()
        own = [pltpu.make_async_copy(src[i], _rows(dst[i], c), lsem.at[i]) for i in range(n)]
        cps = [_rcopy(src[i], _rows(dst[i], c), sems, i, (x, y, 1 - c)) for i in range(n)]
        for cp in own + cps:
            cp.start()
        for i in range(n):
            _rcopy(src[i], _rows(dst[i], 1 - c), sems, i, (x, y, c)).wait_recv()
        for cp in cps:
            cp.wait_send()
        for cp in own:
            cp.wait()

    outs = [jax.ShapeDtypeStruct(h.shape[:2] + (2 * h.shape[2], h.shape[3]), h.dtype) for h in halves]
    return pl.pallas_call(
        body, name="join_halves", in_specs=[ANY] * n, out_specs=[ANY] * n, out_shape=outs,
        scratch_shapes=[pltpu.SemaphoreType.DMA((n,)), pltpu.SemaphoreType.DMA((n,)),
                        pltpu.SemaphoreType.DMA((n,))],
    )(*halves)


def _gather_all(block):
    R, C = block.shape

    def body(src, dst, ssem, rsem, lsem):
        sems = (ssem, rsem)
        x, y, c = _me()
        chips = [(1 - x, y), (x, 1 - y), (1 - x, 1 - y)]

        def at(px, py, pc):
            return dst.at[4 * px + 2 * py + pc]

        own = pltpu.make_async_copy(src, at(x, y, c), lsem)
        own.start()
        first = [_rcopy(src, at(x, y, c), sems, 0, (x, y, 1 - c))]
        first += [_rcopy(src, at(x, y, c), sems, 1 + j, (qx, qy, c)) for j, (qx, qy) in enumerate(chips)]
        for cp in first:
            cp.start()
        passed = []
        for j, (qx, qy) in enumerate(chips):
            got = at(qx, qy, c)
            _rcopy(got, got, sems, 1 + j, (x, y, c)).wait_recv()
            cp = _rcopy(got, got, sems, 4 + j, (x, y, 1 - c))
            cp.start()
            passed.append(cp)
        sib = at(x, y, 1 - c)
        _rcopy(sib, sib, sems, 0, (x, y, c)).wait_recv()
        for j, (qx, qy) in enumerate(chips):
            other = at(qx, qy, 1 - c)
            _rcopy(other, other, sems, 4 + j, (x, y, c)).wait_recv()
        for cp in first + passed:
            cp.wait_send()
        own.wait()

    return pl.pallas_call(
        body, name="gather_all", in_specs=[ANY], out_specs=ANY,
        out_shape=jax.ShapeDtypeStruct((NDEV, R, C), block.dtype),
        scratch_shapes=[pltpu.SemaphoreType.DMA((7,)), pltpu.SemaphoreType.DMA((7,)), pltpu.SemaphoreType.DMA],
    )(block)


def _sum_devices(g):
    _, R, C = g.shape
    br = 264 if R % 264 == 0 else R

    def body(g_ref, o_ref):
        acc = g_ref[0]
        for d in range(1, NDEV):
            acc = acc + g_ref[d]
        o_ref[...] = acc

    return pl.pallas_call(
        body, name="sum_devices", grid=(R // br,),
        in_specs=[pl.BlockSpec((NDEV, br, C), lambda i: (0, i, 0))],
        out_specs=pl.BlockSpec((br, C), lambda i: (i, 0)),
        out_shape=jax.ShapeDtypeStruct((R, C), F32),
        compiler_params=_cp("arbitrary"),
    )(g)


def _adamw(w, g, m, v):
    shape = w.shape
    C = shape[-1]
    R = shape[-2]
    A = 1
    for s in shape[:-2]:
        A *= s
    br = R
    while br * C > 256 * 1024 and br % 16 == 0:
        br //= 2
    c1 = 1.0 / (1.0 - ADAM_B1 ** ADAM_STEP)
    c2 = 1.0 / (1.0 - ADAM_B2 ** ADAM_STEP)

    def body(w_ref, g_ref, m_ref, v_ref, d_ref, nm_ref, nv_ref):
        gv = g_ref[...]
        nm = ADAM_B1 * m_ref[...] + (1.0 - ADAM_B1) * gv
        nv = ADAM_B2 * v_ref[...] + (1.0 - ADAM_B2) * (gv * gv)
        nm_ref[...] = nm
        nv_ref[...] = nv
        d_ref[...] = -ADAM_LR * ((nm * c1) / (jnp.sqrt(nv * c2) + ADAM_EPS) + ADAM_WD * w_ref[...])

    blk = pl.BlockSpec((None, br, C), lambda a, i: (a, i, 0))
    outs = pl.pallas_call(
        body, name="adamw", grid=(A, R // br), in_specs=[blk] * 4, out_specs=[blk] * 3,
        out_shape=[jax.ShapeDtypeStruct((A, R, C), F32)] * 3,
        compiler_params=_cp("arbitrary", "arbitrary"),
    )(*(t.reshape(A, R, C) for t in (w, g, m, v)))
    return tuple(o.reshape(shape) for o in outs)


WEIGHTS = ("norm_mix_pre", "norm_mix_post", "norm_ffn_pre", "norm_ffn_post", "w_in", "conv_a_w", "conf_dw_w",
           "conf_dw_b", "conf_ln_g", "conf_ln_b", "sgu_ln_g", "sgu_ln_b", "sgu_ws", "sgu_b", "w_branch", "w_out",
           "w_ff1", "w_ff2")
BIG = ("w_in", "w_branch", "w_out", "w_ff1", "w_ff2")
CONV_ROWS = 48


def kernel(x, norm_mix_pre, norm_mix_post, norm_ffn_pre, norm_ffn_post, w_in, conv_a_w, conf_dw_w, conf_dw_b, conf_ln_g, conf_ln_b, sgu_ln_g, sgu_ln_b, sgu_ws, sgu_b, w_branch, w_out, w_ff1, w_ff2, loss_target, m_norm_mix_pre, m_norm_mix_post, m_norm_ffn_pre, m_norm_ffn_post, m_w_in, m_conv_a_w, m_conf_dw_w, m_conf_dw_b, m_conf_ln_g, m_conf_ln_b, m_sgu_ln_g, m_sgu_ln_b, m_sgu_ws, m_sgu_b, m_w_branch, m_w_out, m_w_ff1, m_w_ff2, v_norm_mix_pre, v_norm_mix_post, v_norm_ffn_pre, v_norm_ffn_post, v_w_in, v_conv_a_w, v_conf_dw_w, v_conf_dw_b, v_conf_ln_g, v_conf_ln_b, v_sgu_ln_g, v_sgu_ln_b, v_sgu_ws, v_sgu_b, v_w_branch, v_w_out, v_w_ff1, v_w_ff2):
    w = dict(norm_mix_pre=norm_mix_pre, norm_mix_post=norm_mix_post, norm_ffn_pre=norm_ffn_pre,
             norm_ffn_post=norm_ffn_post, w_in=w_in, conv_a_w=conv_a_w, conf_dw_w=conf_dw_w, conf_dw_b=conf_dw_b,
             conf_ln_g=conf_ln_g, conf_ln_b=conf_ln_b, sgu_ln_g=sgu_ln_g, sgu_ln_b=sgu_ln_b, sgu_ws=sgu_ws,
             sgu_b=sgu_b, w_branch=w_branch, w_out=w_out, w_ff1=w_ff1, w_ff2=w_ff2)
    mom = dict(norm_mix_pre=m_norm_mix_pre, norm_mix_post=m_norm_mix_post, norm_ffn_pre=m_norm_ffn_pre,
               norm_ffn_post=m_norm_ffn_post, w_in=m_w_in, conv_a_w=m_conv_a_w, conf_dw_w=m_conf_dw_w,
               conf_dw_b=m_conf_dw_b, conf_ln_g=m_conf_ln_g, conf_ln_b=m_conf_ln_b, sgu_ln_g=m_sgu_ln_g,
               sgu_ln_b=m_sgu_ln_b, sgu_ws=m_sgu_ws, sgu_b=m_sgu_b, w_branch=m_w_branch, w_out=m_w_out,
               w_ff1=m_w_ff1, w_ff2=m_w_ff2)
    var = dict(norm_mix_pre=v_norm_mix_pre, norm_mix_post=v_norm_mix_post, norm_ffn_pre=v_norm_ffn_pre,
               norm_ffn_post=v_norm_ffn_post, w_in=v_w_in, conv_a_w=v_conv_a_w, conf_dw_w=v_conf_dw_w,
               conf_dw_b=v_conf_dw_b, conf_ln_g=v_conf_ln_g, conf_ln_b=v_conf_ln_b, sgu_ln_g=v_sgu_ln_g,
               sgu_ln_b=v_sgu_ln_b, sgu_ws=v_sgu_ws, sgu_b=v_sgu_b, w_branch=v_w_branch, w_out=v_w_out,
               w_ff1=v_w_ff1, w_ff2=v_w_ff2)
    L = w_in.shape[0]
    nseq, S, _ = x.shape
    T = nseq * S
    rk = D // NSH
    mx, my, mc = _me()
    k_chip = 2 * mx + my

    big_src = [w_in.astype(BF).reshape(L, 1, D, w_in.shape[2]), w_branch.astype(BF),
               w_out.astype(BF).reshape(L, 1, rk, D), w_ff1.astype(BF).reshape(L, 1, D, w_ff1.shape[2]),
               w_ff2.astype(BF).reshape(L, 1, w_ff2.shape[1], D)]
    conv_src = jnp.concatenate(
        [jnp.pad(conv_a_w, ((0, 0), (0, SUBLANES - KA), (0, 0))), jnp.pad(conf_dw_w, ((0, 0), (0, 1), (0, 0))),
         jnp.zeros((L, CONV_ROWS - SUBLANES - KC - 1, rk), F32)], axis=1)[None]
    (conv_g,) = _gather_weights([conv_src], 0)
    conv_full = conv_g.reshape(NSH, L, CONV_ROWS, rk).transpose(1, 2, 0, 3).reshape(L, CONV_ROWS, D)

    layers = []
    for l in range(L):
        g_in, g_br, g_out, g_ff1, g_ff2 = _gather_weights(big_src, l)
        layers.append(dict(
            g_mix_pre=norm_mix_pre[l][None], g_mix_post=norm_mix_post[l][None], g_ffn_pre=norm_ffn_pre[l][None],
            g_ffn_post=norm_ffn_post[l][None], w_in=g_in, conv_a_w=conv_full[l, :KA],
            conf_dw_w=conv_full[l, SUBLANES:SUBLANES + KC], conf_dw_b=conf_dw_b[l][None],
            conf_ln_g=conf_ln_g[l][None], conf_ln_b=conf_ln_b[l][None], sgu_ln_g=sgu_ln_g[l][None],
            sgu_ln_b=sgu_ln_b[l][None], sgu_ws=sgu_ws[l], sgu_wst=jnp.swapaxes(sgu_ws[l], 1, 2),
            sgu_bt=sgu_b[l].T, w_branch=g_br.reshape(NSH, 3, rk, D), w_out=g_out.reshape(D, D), w_ff1=g_ff1,
            w_ff2=g_ff2.reshape(NSH * w_ff2.shape[1], D)))

    loss_row, dx, bigs, smalls = _local_step(x.reshape(T, D), loss_target.reshape(T, D), layers, S)
    loss = lax.psum(loss_row[0, 0], ("x", "y", "c"))

    c_arr = jnp.reshape(mc, (1,)).astype(jnp.int32)
    halves = [None] * len(BIG)
    for l in range(L):
        parts = [bigs[l]["w_in"], bigs[l]["w_branch"].reshape(NSH * 3, rk, D), bigs[l]["w_out"],
                 bigs[l]["w_ff1"], bigs[l]["w_ff2"]]
        sib = _send_halves_to_sibling(parts)
        sums = [_pair_add(p, s, c_arr) for p, s in zip(parts, sib)]
        rcv = _scatter_chip_sums(sums)
        halves = [_sum_chips(r, h, l, L) for r, h in zip(rcv, halves)]
    full = _join_halves(halves)
    grads = {n: f.reshape(w[n].shape) for n, f in zip(BIG, full)}

    packed = jnp.concatenate(smalls, axis=0)
    small_sum = _sum_devices(_gather_all(packed)).reshape(L, PACK_ROWS, D)
    shapes = {n: (w[n].shape[1:] if n not in ("conv_a_w", "conf_dw_w") else (w[n].shape[1], D)) for n in SMALL_NAMES}
    sg = _unpack_small(small_sum, shapes)
    for n in SMALL_NAMES:
        if n in ("conv_a_w", "conf_dw_w"):
            grads[n] = lax.dynamic_slice_in_dim(sg[n], k_chip * rk, rk, axis=2)
        else:
            grads[n] = sg[n]

    delta, new_m, new_v = {}, {}, {}
    for n in BIG:
        delta[n], new_m[n], new_v[n] = _adamw(w[n], grads[n], mom[n], var[n])
    for n in SMALL_NAMES:
        sh = w[n].shape
        flat = (sh[0] * sh[1], sh[2]) if n in ("conv_a_w", "conf_dw_w") else (-1, D)
        d, nm, nv = _adamw(*(t.reshape(flat) for t in (w[n], grads[n], mom[n], var[n])))
        delta[n], new_m[n], new_v[n] = d.reshape(sh), nm.reshape(sh), nv.reshape(sh)

    return (loss, dx.reshape(x.shape), *[grads[n] for n in WEIGHTS], *[delta[n] for n in WEIGHTS],
            *[new_m[n] for n in WEIGHTS], *[new_v[n] for n in WEIGHTS])
```

```python
import functools

import jax
import jax.numpy as jnp
from jax import lax
from jax.experimental import pallas as pl
from jax.experimental.pallas import tpu as pltpu

D = 1024
HEADS = 8
CHUNK = 128
KA = 3
KC = 31
HALO = 32
NSH = 4
NDEV = 8
EPS = 1e-6
BF = jnp.bfloat16
F32 = jnp.float32
VMEM_LIMIT = 56 * 1024 * 1024

ADAM_LR = 0.001
ADAM_B1 = 0.9
ADAM_B2 = 0.999
ADAM_EPS = 1e-08
ADAM_WD = 0.01
ADAM_STEP = 10

MESH = pl.DeviceIdType.MESH
ANY = pl.BlockSpec(memory_space=pl.ANY)


def _cp(*sem):
    return pltpu.CompilerParams(dimension_semantics=sem, vmem_limit_bytes=VMEM_LIMIT)


def _sig(x):
    return 1.0 / (1.0 + jnp.exp(-x))


_GC = 0.7978845608028654


def _gelu(x):
    x2 = x * x
    t = jnp.tanh(_GC * x * (1.0 + 0.044715 * x2))
    y = 0.5 * x * (1.0 + t)
    dy = 0.5 * (1.0 + t) + 0.5 * x * (1.0 - t * t) * _GC * (1.0 + 3.0 * 0.044715 * x2)
    return y, dy


def _rms_fwd(x, g):
    r = lax.rsqrt(jnp.mean(x * x, axis=-1, keepdims=True) + EPS)
    return x * r * g


def _rms_bwd(dy, x, g):
    r = lax.rsqrt(jnp.mean(x * x, axis=-1, keepdims=True) + EPS)
    xn = x * r
    dyg = dy * g
    dx = r * (dyg - xn * jnp.mean(dyg * xn, axis=-1, keepdims=True))
    return dx, jnp.sum(dy * xn, axis=0, keepdims=True)


def _ln_stats(x):
    mu = jnp.mean(x, axis=-1, keepdims=True)
    xc = x - mu
    r = lax.rsqrt(jnp.mean(xc * xc, axis=-1, keepdims=True) + EPS)
    return xc * r, r


def _ln_bwd(dn, n, r):
    return r * (dn - jnp.mean(dn, axis=-1, keepdims=True) - n * jnp.mean(dn * n, axis=-1, keepdims=True))


def _dot(a, b):
    return jnp.dot(a, b, preferred_element_type=F32)


def _dot_nt(a, b):
    return lax.dot_general(a, b, (((1,), (1,)), ((), ())), preferred_element_type=F32)


def _dot_tn(a, b):
    return lax.dot_general(a, b, (((0,), (0,)), ((), ())), preferred_element_type=F32)


def _in_proj(x, g, w, dep):
    T = x.shape[0]
    nc = w.shape[2]
    tm = min(T, 1024)
    tn = 1280
    nj = nc // tn

    def body(x_ref, g_ref, w_ref, dep_ref, h_ref, z_ref, h_scr):
        @pl.when((pl.program_id(1) == 0) & (pl.program_id(2) == 0))
        def _():
            h = _rms_fwd(x_ref[...], g_ref[...]).astype(BF)
            h_scr[...] = h
            h_ref[...] = h
        z_ref[...] = _dot(h_scr[...], w_ref[...]).astype(BF)

    return pl.pallas_call(
        body, name="in_proj", grid=(T // tm, NSH, nj),
        in_specs=[pl.BlockSpec((tm, D), lambda i, k, j: (i, 0)),
                  pl.BlockSpec((1, D), lambda i, k, j: (0, 0)),
                  pl.BlockSpec((None, D, tn), lambda i, k, j: (k, 0, j)), ANY],
        out_specs=[pl.BlockSpec((tm, D), lambda i, k, j: (i, 0)),
                   pl.BlockSpec((tm, tn), lambda i, k, j: (i, k * nj + j))],
        out_shape=[jax.ShapeDtypeStruct((T, D), BF), jax.ShapeDtypeStruct((T, NSH * nc), BF)],
        scratch_shapes=[pltpu.VMEM((tm, D), BF)],
        compiler_params=_cp("arbitrary", "arbitrary", "arbitrary"),
    )(x, g, w, dep)


def _tile_specs(tt, nt_total, reverse):
    def tile(i):
        return (nt_total - 1 - i) if reverse else i

    def cur(c):
        return pl.BlockSpec((tt, D), lambda i, *_: (tile(i), c))

    def halo(c):
        return pl.BlockSpec((HALO, D), lambda i, *_: (jnp.maximum(tile(i) * (tt // HALO) - 1, 0), c))

    def row(r=1):
        return pl.BlockSpec((r, D), lambda i, *_: (0, 0))

    return tile, cur, halo, row


def _causal_conv(ext, w_ref, ntap, tt):
    acc = None
    for k in range(ntap):
        term = w_ref[k:k + 1, :] * ext[pl.ds(HALO - (ntap - 1) + k, tt), :]
        acc = term if acc is None else acc + term
    return acc


def _anticausal_conv(ext, w_ref, ntap, tt):
    acc = None
    for k in range(ntap):
        term = w_ref[k:k + 1, :] * ext[pl.ds(ntap - 1 - k, tt), :]
        acc = term if acc is None else acc + term
    return acc


def _conv_wgrad(dw_ref, dout, ext, ntap, tt):
    for k in range(ntap):
        dw_ref[k:k + 1, :] += jnp.sum(dout * ext[pl.ds(HALO - (ntap - 1) + k, tt), :], axis=0, keepdims=True)


def _mix_a_fwd(z, wa, S):
    T = z.shape[0]
    tt = min(S, 256)
    nt = S // tt
    _, cur, halo, row = _tile_specs(tt, T // tt, False)

    def body(ah, ab, ac, ah_h, ac_h, w_ref, y_ref, ext):
        first = (pl.program_id(0) % nt) == 0
        ph = ah_h[...].astype(F32) * ac_h[...].astype(F32)
        ext[0:HALO, :] = jnp.where(first, 0.0, ph)
        ext[HALO:, :] = ah[...].astype(F32) * ac[...].astype(F32)
        q = _causal_conv(ext, w_ref, KA, tt)
        y_ref[...] = (ab[...].astype(F32) * q).astype(BF)

    return pl.pallas_call(
        body, name="mix_a_fwd", grid=(T // tt,),
        in_specs=[cur(0), cur(1), cur(2), halo(0), halo(2), row(KA)],
        out_specs=pl.BlockSpec((tt, D), lambda i: (i, 0)),
        out_shape=jax.ShapeDtypeStruct((T, D), BF),
        scratch_shapes=[pltpu.VMEM((HALO + tt, D), F32)],
        compiler_params=_cp("arbitrary"),
    )(z, z, z, z, z, wa)


def _mix_b_fwd(z, wc, bc, lg, lb, S):
    T = z.shape[0]
    tt = min(S, 256)
    nt = S // tt
    _, cur, halo, row = _tile_specs(tt, T // tt, False)

    def body(ca, cg, ca_h, cg_h, w_ref, bc_ref, lg_ref, lb_ref, y_ref, s_ref, ext):
        first = (pl.program_id(0) % nt) == 0
        rh = ca_h[...].astype(F32) * _sig(cg_h[...].astype(F32))
        ext[0:HALO, :] = jnp.where(first, 0.0, rh)
        ext[HALO:, :] = ca[...].astype(F32) * _sig(cg[...].astype(F32))
        s = _causal_conv(ext, w_ref, KC, tt) + bc_ref[...]
        s_ref[...] = s.astype(BF)
        n, _ = _ln_stats(s)
        t = n * lg_ref[...] + lb_ref[...]
        y_ref[...] = (t * _sig(t)).astype(BF)

    return pl.pallas_call(
        body, name="mix_b_fwd", grid=(T // tt,),
        in_specs=[cur(3), cur(4), halo(3), halo(4), row(KC), row(), row(), row()],
        out_specs=[pl.BlockSpec((tt, D), lambda i: (i, 0))] * 2,
        out_shape=[jax.ShapeDtypeStruct((T, D), BF)] * 2,
        scratch_shapes=[pltpu.VMEM((HALO + tt, D), F32)],
        compiler_params=_cp("arbitrary"),
    )(z, z, z, z, wc, bc, lg, lb)


def _causal_mask(transposed):
    r = lax.broadcasted_iota(jnp.int32, (CHUNK, CHUNK), 0)
    c = lax.broadcasted_iota(jnp.int32, (CHUNK, CHUNK), 1)
    return (c >= r) if transposed else (r >= c)


def _mix_s_fwd(z, lg, lb, ws, bst, S):
    T = z.shape[0]
    tt = min(S, 256)
    _, cur, _, row = _tile_specs(tt, T // tt, False)

    def body(su, sv, lg_ref, lb_ref, ws_ref, bst_ref, y_ref, u_scr, vn_scr):
        u_scr[...] = _gelu(su[...].astype(F32))[0]
        n, _ = _ln_stats(_gelu(sv[...].astype(F32))[0])
        vn_scr[...] = (n * lg_ref[...] + lb_ref[...]).astype(BF)
        mask = _causal_mask(False)
        for h in range(HEADS):
            wm = jnp.where(mask, ws_ref[h], 0.0).astype(BF)
            cols = slice(h * CHUNK, (h + 1) * CHUNK)
            for c in range(tt // CHUNK):
                rows = slice(c * CHUNK, (c + 1) * CHUNK)
                mixed = _dot(wm, vn_scr[rows, cols]) + bst_ref[:, h:h + 1]
                y_ref[rows, cols] = (u_scr[rows, cols] * mixed).astype(BF)

    return pl.pallas_call(
        body, name="mix_s_fwd", grid=(T // tt,),
        in_specs=[cur(5), cur(6), row(), row(),
                  pl.BlockSpec((HEADS, CHUNK, CHUNK), lambda i: (0, 0, 0)),
                  pl.BlockSpec((CHUNK, HEADS), lambda i: (0, 0))],
        out_specs=pl.BlockSpec((tt, D), lambda i: (i, 0)),
        out_shape=jax.ShapeDtypeStruct((T, D), BF),
        scratch_shapes=[pltpu.VMEM((tt, D), F32), pltpu.VMEM((tt, D), BF)],
        compiler_params=_cp("arbitrary"),
    )(z, z, lg, lb, ws, bst)


def _mix_out_fwd(ya, yc, ys, z, x, wb, wo, gp):
    T = x.shape[0]
    tm = min(T, 256)
    rk = D // NSH

    def body(ya_ref, yc_ref, ys_ref, ga, gc, gs, x_ref, wb_ref, wo_ref, gp_ref, p_ref, mg_ref, m_ref, x1_ref):
        acc = None
        for b, (y_ref, g_ref) in enumerate(((ya_ref, ga), (yc_ref, gc), (ys_ref, gs))):
            pb = None
            for k in range(NSH):
                part = _dot(y_ref[:, k * rk:(k + 1) * rk], wb_ref[k, b])
                pb = part if pb is None else pb + part
            p_ref[b] = pb.astype(BF)
            term = _sig(g_ref[...].astype(F32)) * pb
            acc = term if acc is None else acc + term
        mg = acc.astype(BF)
        mg_ref[...] = mg
        m = _dot(mg, wo_ref[...])
        m_ref[...] = m.astype(BF)
        x1_ref[...] = x_ref[...] + _rms_fwd(m, gp_ref[...])

    rowblk = pl.BlockSpec((tm, D), lambda i: (i, 0))
    return pl.pallas_call(
        body, name="mix_out_fwd", grid=(T // tm,),
        in_specs=[rowblk, rowblk, rowblk,
                  pl.BlockSpec((tm, D), lambda i: (i, 7)), pl.BlockSpec((tm, D), lambda i: (i, 8)),
                  pl.BlockSpec((tm, D), lambda i: (i, 9)), rowblk,
                  pl.BlockSpec((NSH, 3, rk, D), lambda i: (0, 0, 0, 0)),
                  pl.BlockSpec((D, D), lambda i: (0, 0)),
                  pl.BlockSpec((1, D), lambda i: (0, 0))],
        out_specs=[pl.BlockSpec((3, tm, D), lambda i: (0, i, 0)), rowblk, rowblk, rowblk],
        out_shape=[jax.ShapeDtypeStruct((3, T, D), BF), jax.ShapeDtypeStruct((T, D), BF),
                   jax.ShapeDtypeStruct((T, D), BF), jax.ShapeDtypeStruct((T, D), F32)],
        compiler_params=_cp("arbitrary"),
    )(ya, yc, ys, z, z, z, x, wb, wo, gp)


def _ffn_fwd(x1, g3, w1, w2, g4):
    T = x1.shape[0]
    tm = min(T, 512)

    def body(x_ref, g3_ref, w1_ref, w2_ref, g4_ref, h_ref, a_ref, f_ref, x2_ref, h_scr, acc):
        k = pl.program_id(1)

        @pl.when(k == 0)
        def _():
            h = _rms_fwd(x_ref[...], g3_ref[...]).astype(BF)
            h_scr[...] = h
            h_ref[...] = h
            acc[...] = jnp.zeros_like(acc)

        a = _dot(h_scr[...], w1_ref[...])
        a_ref[...] = a.astype(BF)
        r = jnp.maximum(a, 0.0)
        acc[...] += _dot((r * r).astype(BF), w2_ref[...])

        @pl.when(k == NSH - 1)
        def _():
            f = acc[...]
            f_ref[...] = f.astype(BF)
            x2_ref[...] = x_ref[...] + _rms_fwd(f, g4_ref[...])

    rowblk = pl.BlockSpec((tm, D), lambda i, k: (i, 0))
    vec = pl.BlockSpec((1, D), lambda i, k: (0, 0))
    return pl.pallas_call(
        body, name="ffn_fwd", grid=(T // tm, NSH),
        in_specs=[rowblk, vec, pl.BlockSpec((None, D, D), lambda i, k: (k, 0, 0)),
                  pl.BlockSpec((D, D), lambda i, k: (k, 0)), vec],
        out_specs=[rowblk, pl.BlockSpec((tm, D), lambda i, k: (i, k)), rowblk, rowblk],
        out_shape=[jax.ShapeDtypeStruct((T, D), BF), jax.ShapeDtypeStruct((T, NSH * D), BF),
                   jax.ShapeDtypeStruct((T, D), BF), jax.ShapeDtypeStruct((T, D), F32)],
        scratch_shapes=[pltpu.VMEM((tm, D), BF), pltpu.VMEM((tm, D), F32)],
        compiler_params=_cp("arbitrary", "arbitrary"),
    )(x1, g3, w1, w2, g4)


def _loss_head(y, target):
    T = y.shape[0]
    tm = min(T, 512)

    def body(y_ref, t_ref, dy_ref, l_ref):
        @pl.when(pl.program_id(0) == 0)
        def _():
            l_ref[...] = jnp.zeros_like(l_ref)
        e = y_ref[...] - t_ref[...]
        dy_ref[...] = e * (1.0 / D)
        l_ref[...] += jnp.sum(e * e) * (0.5 / D)

    rowblk = pl.BlockSpec((tm, D), lambda i: (i, 0))
    return pl.pallas_call(
        body, name="loss_head", grid=(T // tm,),
        in_specs=[rowblk, rowblk],
        out_specs=[rowblk, pl.BlockSpec((1, 128), lambda i: (0, 0))],
        out_shape=[jax.ShapeDtypeStruct((T, D), F32), jax.ShapeDtypeStruct((1, 128), F32)],
        compiler_params=_cp("arbitrary"),
    )(y, target)


def _ffn_bwd(dx2, f, g4, a, w2, w1, x1, g3, dep):
    T = dx2.shape[0]
    tm = min(T, 512)

    def body(dx2_ref, f_ref, g4_ref, a_ref, w2_ref, w1_ref, x1_ref, g3_ref, dep_ref,
             df_ref, da_ref, dx1_ref, dg4_ref, dg3_ref, df_scr, acc):
        i, k = pl.program_id(0), pl.program_id(1)

        @pl.when((i == 0) & (k == 0))
        def _():
            dg4_ref[...] = jnp.zeros_like(dg4_ref)
            dg3_ref[...] = jnp.zeros_like(dg3_ref)

        @pl.when(k == 0)
        def _():
            df, dg = _rms_bwd(dx2_ref[...], f_ref[...].astype(F32), g4_ref[...])
            dg4_ref[...] += dg
            dfb = df.astype(BF)
            df_scr[...] = dfb
            df_ref[...] = dfb
            acc[...] = jnp.zeros_like(acc)

        av = a_ref[...].astype(F32)
        da = (_dot_nt(df_scr[...], w2_ref[...]) * (2.0 * jnp.maximum(av, 0.0))).astype(BF)
        da_ref[...] = da
        acc[...] += _dot_nt(da, w1_ref[...])

        @pl.when(k == NSH - 1)
        def _():
            dx, dg = _rms_bwd(acc[...], x1_ref[...], g3_ref[...])
            dg3_ref[...] += dg
            dx1_ref[...] = dx2_ref[...] + dx

    rowblk = pl.BlockSpec((tm, D), lambda i, k: (i, 0))
    vec = pl.BlockSpec((1, D), lambda i, k: (0, 0))
    return pl.pallas_call(
        body, name="ffn_bwd", grid=(T // tm, NSH),
        in_specs=[rowblk, rowblk, vec, pl.BlockSpec((tm, D), lambda i, k: (i, k)),
                  pl.BlockSpec((D, D), lambda i, k: (k, 0)),
                  pl.BlockSpec((None, D, D), lambda i, k: (k, 0, 0)), rowblk, vec, ANY],
        out_specs=[rowblk, pl.BlockSpec((tm, D), lambda i, k: (i, k)), rowblk, vec, vec],
        out_shape=[jax.ShapeDtypeStruct((T, D), BF), jax.ShapeDtypeStruct((T, NSH * D), BF),
                   jax.ShapeDtypeStruct((T, D), F32), jax.ShapeDtypeStruct((1, D), F32),
                   jax.ShapeDtypeStruct((1, D), F32)],
        scratch_shapes=[pltpu.VMEM((tm, D), BF), pltpu.VMEM((tm, D), F32)],
        compiler_params=_cp("arbitrary", "arbitrary"),
    )(dx2, f, g4, a, w2, w1, x1, g3, dep)


def _wgrad(name, ops, grid, in_specs, out_spec, out_shape, acc_shape, pick=None, relu2=False):
    nt = grid[-1]
    na = len(ops) - 1

    def body(*refs):
        a_refs, b_ref, o_ref, acc = refs[:na], refs[na], refs[na + 1], refs[na + 2]
        t = pl.program_id(len(grid) - 1)

        @pl.when(t == 0)
        def _():
            acc[...] = jnp.zeros_like(acc)

        def add(a_ref):
            av = a_ref[...]
            if relu2:
                r = jnp.maximum(av.astype(F32), 0.0)
                av = (r * r).astype(BF)
            acc[...] += _dot_tn(av, b_ref[...])

        if na == 1:
            add(a_refs[0])
        else:
            sel = pick()
            for n in range(na):
                pl.when(sel == n)(functools.partial(add, a_refs[n]))

        @pl.when(t == nt - 1)
        def _():
            o_ref[...] = acc[...].astype(o_ref.dtype)

    return pl.pallas_call(
        body, name=name, grid=grid, in_specs=in_specs, out_specs=out_spec, out_shape=out_shape,
        scratch_shapes=[pltpu.VMEM(acc_shape, F32)],
        compiler_params=_cp(*(["arbitrary"] * len(grid))),
    )(*ops)


def _mix_out_bwd(dx1, m, gp, wo, p3, z, wb):
    T = dx1.shape[0]
    tm = min(T, 256)
    rk = D // NSH

    def body(dx1_ref, m_ref, gp_ref, wo_ref, p_ref, g_ref, wb_ref,
             dm_ref, dp_ref, dy_ref, dz_ref, dgp_ref, dmg):
        i, b = pl.program_id(0), pl.program_id(1)

        @pl.when((i == 0) & (b == 0))
        def _():
            dgp_ref[...] = jnp.zeros_like(dgp_ref)

        @pl.when(b == 0)
        def _():
            dm, dg = _rms_bwd(dx1_ref[...], m_ref[...].astype(F32), gp_ref[...])
            dgp_ref[...] += dg
            dmb = dm.astype(BF)
            dm_ref[...] = dmb
            dmg[...] = _dot_nt(dmb, wo_ref[...])

        gate = _sig(g_ref[...].astype(F32))
        d = dmg[...]
        dp = (d * gate).astype(BF)
        dp_ref[...] = dp
        dz_ref[...] = (d * p_ref[...].astype(F32) * gate * (1.0 - gate)).astype(BF)
        for k in range(NSH):
            dy_ref[:, k * rk:(k + 1) * rk] = _dot_nt(dp, wb_ref[k]).astype(BF)

    rowblk = pl.BlockSpec((tm, D), lambda i, b: (i, 0))
    br = pl.BlockSpec((None, tm, D), lambda i, b: (b, i, 0))
    vec = pl.BlockSpec((1, D), lambda i, b: (0, 0))
    return pl.pallas_call(
        body, name="mix_out_bwd", grid=(T // tm, 3),
        in_specs=[rowblk, rowblk, vec, pl.BlockSpec((D, D), lambda i, b: (0, 0)), br,
                  pl.BlockSpec((tm, D), lambda i, b: (i, 7 + b)),
                  pl.BlockSpec((NSH, None, rk, D), lambda i, b: (0, b, 0, 0))],
        out_specs=[rowblk, br, br, pl.BlockSpec((tm, D), lambda i, b: (i, 7 + b)), vec],
        out_shape=[jax.ShapeDtypeStruct((T, D), BF), jax.ShapeDtypeStruct((3, T, D), BF),
                   jax.ShapeDtypeStruct((3, T, D), BF), jax.ShapeDtypeStruct((T, 10 * D), BF),
                   jax.ShapeDtypeStruct((1, D), F32)],
        scratch_shapes=[pltpu.VMEM((tm, D), F32)],
        compiler_params=_cp("arbitrary", "arbitrary"),
    )(dx1, m, gp, wo, p3, z, wb)


def _mix_a_bwd(dz, dy3, z, wa, S):
    T = z.shape[0]
    tt = min(S, 256)
    nt = S // tt
    ntt = T // tt
    tile, cur, halo, row = _tile_specs(tt, ntt, True)

    def body(dz_in, dy_ref, ah, ab, ac, ah_h, ac_h, w_ref, dz_ref, dw_ref, ext_p, ext_d, stage):
        i, b = pl.program_id(0), pl.program_id(1)
        ti = ntt - 1 - i

        @pl.when((i == 0) & (b == 0))
        def _():
            dw_ref[...] = jnp.zeros_like(dw_ref)
            ext_d[...] = jnp.zeros_like(ext_d)

        @pl.when(b == 0)
        def _():
            first = (ti % nt) == 0
            last = (ti % nt) == nt - 1
            ahv, acv, abv = ah[...].astype(F32), ac[...].astype(F32), ab[...].astype(F32)
            ext_p[0:HALO, :] = jnp.where(first, 0.0, ah_h[...].astype(F32) * ac_h[...].astype(F32))
            ext_p[HALO:, :] = ahv * acv
            q = _causal_conv(ext_p, w_ref, KA, tt)
            dy = dy_ref[...].astype(F32)
            dq = dy * abv
            stage[1] = (dy * q).astype(BF)
            _conv_wgrad(dw_ref, dq, ext_p, KA, tt)
            ext_d[tt:, :] = jnp.where(last, 0.0, ext_d[0:HALO, :])
            ext_d[0:tt, :] = dq
            dp = _anticausal_conv(ext_d, w_ref, KA, tt)
            stage[0] = (dp * acv).astype(BF)
            stage[2] = (dp * ahv).astype(BF)

        dz_ref[...] = stage[b]

    return pl.pallas_call(
        body, name="mix_a_bwd", grid=(ntt, 3),
        in_specs=[ANY, pl.BlockSpec((None, tt, D), lambda i, b: (0, tile(i), 0)),
                  cur(0), cur(1), cur(2), halo(0), halo(2), row(KA)],
        out_specs=[pl.BlockSpec((tt, D), lambda i, b: (tile(i), b)), pl.BlockSpec((KA, D), lambda i, b: (0, 0))],
        out_shape=[jax.ShapeDtypeStruct(dz.shape, BF), jax.ShapeDtypeStruct((KA, D), F32)],
        scratch_shapes=[pltpu.VMEM((HALO + tt, D), F32), pltpu.VMEM((tt + HALO, D), F32),
                        pltpu.VMEM((3, tt, D), BF)],
        input_output_aliases={0: 0},
        compiler_params=_cp("arbitrary", "arbitrary"),
    )(dz, dy3, z, z, z, z, z, wa)


def _mix_b_bwd(dz, dy3, s, z, wc, lg, lb, S):
    T = z.shape[0]
    tt = min(S, 256)
    nt = S // tt
    ntt = T // tt
    tile, cur, halo, row = _tile_specs(tt, ntt, True)

    def body(dz_in, dy_ref, s_ref, ca, cg, ca_h, cg_h, w_ref, lg_ref, lb_ref,
             dz_ref, dw_ref, dbc_ref, dlg_ref, dlb_ref, ext_r, ext_d, stage):
        i, b = pl.program_id(0), pl.program_id(1)
        ti = ntt - 1 - i

        @pl.when((i == 0) & (b == 0))
        def _():
            dw_ref[...] = jnp.zeros_like(dw_ref)
            dbc_ref[...] = jnp.zeros_like(dbc_ref)
            dlg_ref[...] = jnp.zeros_like(dlg_ref)
            dlb_ref[...] = jnp.zeros_like(dlb_ref)
            ext_d[...] = jnp.zeros_like(ext_d)

        @pl.when(b == 0)
        def _():
            first = (ti % nt) == 0
            last = (ti % nt) == nt - 1
            n, r = _ln_stats(s_ref[...].astype(F32))
            t = n * lg_ref[...] + lb_ref[...]
            sg = _sig(t)
            dt = dy_ref[...].astype(F32) * (sg * (1.0 + t * (1.0 - sg)))
            dlg_ref[...] += jnp.sum(dt * n, axis=0, keepdims=True)
            dlb_ref[...] += jnp.sum(dt, axis=0, keepdims=True)
            ds = _ln_bwd(dt * lg_ref[...], n, r)
            dbc_ref[...] += jnp.sum(ds, axis=0, keepdims=True)
            cav = ca[...].astype(F32)
            sgc = _sig(cg[...].astype(F32))
            ext_r[0:HALO, :] = jnp.where(first, 0.0, ca_h[...].astype(F32) * _sig(cg_h[...].astype(F32)))
            ext_r[HALO:, :] = cav * sgc
            _conv_wgrad(dw_ref, ds, ext_r, KC, tt)
            ext_d[tt:, :] = jnp.where(last, 0.0, ext_d[0:HALO, :])
            ext_d[0:tt, :] = ds
            dr = _anticausal_conv(ext_d, w_ref, KC, tt)
            stage[0] = (dr * sgc).astype(BF)
            stage[1] = (dr * cav * sgc * (1.0 - sgc)).astype(BF)

        dz_ref[...] = stage[b]

    vec = pl.BlockSpec((1, D), lambda i, b: (0, 0))
    return pl.pallas_call(
        body, name="mix_b_bwd", grid=(ntt, 2),
        in_specs=[ANY, pl.BlockSpec((None, tt, D), lambda i, b: (1, tile(i), 0)),
                  pl.BlockSpec((tt, D), lambda i, b: (tile(i), 0)),
                  cur(3), cur(4), halo(3), halo(4), row(KC), row(), row()],
        out_specs=[pl.BlockSpec((tt, D), lambda i, b: (tile(i), 3 + b)),
                   pl.BlockSpec((KC, D), lambda i, b: (0, 0)), vec, vec, vec],
        out_shape=[jax.ShapeDtypeStruct(dz.shape, BF), jax.ShapeDtypeStruct((KC, D), F32)]
        + [jax.ShapeDtypeStruct((1, D), F32)] * 3,
        scratch_shapes=[pltpu.VMEM((HALO + tt, D), F32), pltpu.VMEM((tt + HALO, D), F32),
                        pltpu.VMEM((2, tt, D), BF)],
        input_output_aliases={0: 0},
        compiler_params=_cp("arbitrary", "arbitrary"),
    )(dz, dy3, s, z, z, z, z, wc, lg, lb)


def _mix_s_bwd(dz, dy3, z, lg, lb, ws, wst, bst, S):
    T = z.shape[0]
    tt = min(S, 256)
    ntt = T // tt
    _, cur, _, row = _tile_specs(tt, ntt, False)

    def body(dz_in, dy_ref, su, sv, lg_ref, lb_ref, ws_ref, wst_ref, bst_ref,
             dz_ref, dws_ref, dbst_ref, dlg_ref, dlb_ref, u_scr, vn_scr, dvn_scr, stage):
        i, b = pl.program_id(0), pl.program_id(1)

        @pl.when((i == 0) & (b == 0))
        def _():
            dws_ref[...] = jnp.zeros_like(dws_ref)
            dbst_ref[...] = jnp.zeros_like(dbst_ref)
            dlg_ref[...] = jnp.zeros_like(dlg_ref)
            dlb_ref[...] = jnp.zeros_like(dlb_ref)

        @pl.when(b == 0)
        def _():
            u, du_dx = _gelu(su[...].astype(F32))
            v, dv_dx = _gelu(sv[...].astype(F32))
            u_scr[...] = u
            n, r = _ln_stats(v)
            vn_scr[...] = (n * lg_ref[...] + lb_ref[...]).astype(BF)
            mask = _causal_mask(False)
            mask_t = _causal_mask(True)
            for h in range(HEADS):
                wm = jnp.where(mask, ws_ref[h], 0.0).astype(BF)
                wmt = jnp.where(mask_t, wst_ref[h], 0.0).astype(BF)
                cols = slice(h * CHUNK, (h + 1) * CHUNK)
                for c in range(tt // CHUNK):
                    rows = slice(c * CHUNK, (c + 1) * CHUNK)
                    vb = vn_scr[rows, cols]
                    mixed = _dot(wm, vb) + bst_ref[:, h:h + 1]
                    dy = dy_ref[rows, cols].astype(F32)
                    dmix = dy * u_scr[rows, cols]
                    u_scr[rows, cols] = dy * mixed
                    dbst_ref[:, h:h + 1] += jnp.sum(dmix, axis=1, keepdims=True)
                    dmb = dmix.astype(BF)
                    dws_ref[h] += _dot_nt(dmb, vb)
                    dvn_scr[rows, cols] = _dot(wmt, dmb)
            stage[0] = (u_scr[...] * du_dx).astype(BF)
            dvn = dvn_scr[...]
            dlg_ref[...] += jnp.sum(dvn * n, axis=0, keepdims=True)
            dlb_ref[...] += jnp.sum(dvn, axis=0, keepdims=True)
            stage[1] = (_ln_bwd(dvn * lg_ref[...], n, r) * dv_dx).astype(BF)

        dz_ref[...] = stage[b]

    vec = pl.BlockSpec((1, D), lambda i, b: (0, 0))
    wsp = pl.BlockSpec((HEADS, CHUNK, CHUNK), lambda i, b: (0, 0, 0))
    bsp = pl.BlockSpec((CHUNK, HEADS), lambda i, b: (0, 0))
    return pl.pallas_call(
        body, name="mix_s_bwd", grid=(ntt, 2),
        in_specs=[ANY, pl.BlockSpec((None, tt, D), lambda i, b: (2, i, 0)),
                  cur(5), cur(6), row(), row(), wsp, wsp, bsp],
        out_specs=[pl.BlockSpec((tt, D), lambda i, b: (i, 5 + b)), wsp, bsp, vec, vec],
        out_shape=[jax.ShapeDtypeStruct(dz.shape, BF), jax.ShapeDtypeStruct((HEADS, CHUNK, CHUNK), F32),
                   jax.ShapeDtypeStruct((CHUNK, HEADS), F32), jax.ShapeDtypeStruct((1, D), F32),
                   jax.ShapeDtypeStruct((1, D), F32)],
        scratch_shapes=[pltpu.VMEM((tt, D), F32), pltpu.VMEM((tt, D), BF), pltpu.VMEM((tt, D), F32),
                        pltpu.VMEM((2, tt, D), BF)],
        input_output_aliases={0: 0},
        compiler_params=_cp("arbitrary", "arbitrary"),
    )(dz, dy3, z, z, lg, lb, ws, wst, bst)


def _in_proj_bwd(dz, w, x, g, dx1):
    T = x.shape[0]
    nc = w.shape[2]
    tm = min(T, 512)
    tn = 1280
    nj = nc // tn

    def body(dz_ref, w_ref, x_ref, g_ref, dx1_ref, dx_ref, dg_ref, acc):
        i, k, j = pl.program_id(0), pl.program_id(1), pl.program_id(2)

        @pl.when((i == 0) & (k == 0) & (j == 0))
        def _():
            dg_ref[...] = jnp.zeros_like(dg_ref)

        @pl.when((k == 0) & (j == 0))
        def _():
            acc[...] = jnp.zeros_like(acc)

        acc[...] += _dot_nt(dz_ref[...], w_ref[...])

        @pl.when((k == NSH - 1) & (j == nj - 1))
        def _():
            dx, dg = _rms_bwd(acc[...], x_ref[...], g_ref[...])
            dg_ref[...] += dg
            dx_ref[...] = dx1_ref[...] + dx

    rowblk = pl.BlockSpec((tm, D), lambda i, k, j: (i, 0))
    vec = pl.BlockSpec((1, D), lambda i, k, j: (0, 0))
    return pl.pallas_call(
        body, name="in_proj_bwd", grid=(T // tm, NSH, nj),
        in_specs=[pl.BlockSpec((tm, tn), lambda i, k, j: (i, k * nj + j)),
                  pl.BlockSpec((None, D, tn), lambda i, k, j: (k, 0, j)), rowblk, vec, rowblk],
        out_specs=[rowblk, vec],
        out_shape=[jax.ShapeDtypeStruct((T, D), F32), jax.ShapeDtypeStruct((1, D), F32)],
        scratch_shapes=[pltpu.VMEM((tm, D), F32)],
        compiler_params=_cp("arbitrary", "arbitrary", "arbitrary"),
    )(dz, w, x, g, dx1)


def _layer_fwd(x, p, S, dep):
    h, z = _in_proj(x, p["g_mix_pre"], p["w_in"], dep)
    ya = _mix_a_fwd(z, p["conv_a_w"], S)
    yc, s = _mix_b_fwd(z, p["conf_dw_w"], p["conf_dw_b"], p["conf_ln_g"], p["conf_ln_b"], S)
    ys = _mix_s_fwd(z, p["sgu_ln_g"], p["sgu_ln_b"], p["sgu_ws"], p["sgu_bt"], S)
    p3, merged, m, x1 = _mix_out_fwd(ya, yc, ys, z, x, p["w_branch"], p["w_out"], p["g_mix_post"])
    h2, a, f, x2 = _ffn_fwd(x1, p["g_ffn_pre"], p["w_ff1"], p["w_ff2"], p["g_ffn_post"])
    saved = dict(x=x, h=h, z=z, ya=ya, yc=yc, ys=ys, s=s, p3=p3, merged=merged, m=m, x1=x1, h2=h2, a=a, f=f)
    return x2, saved


def _layer_bwd(dx2, p, sv, S, dep):
    T = dx2.shape[0]
    bt = min(T, 512)
    nt = T // bt
    rk = D // NSH
    df, da, dx1, dg_ffn_post, dg_ffn_pre = _ffn_bwd(dx2, sv["f"], p["g_ffn_post"], sv["a"], p["w_ff2"],
                                                    p["w_ff1"], sv["x1"], p["g_ffn_pre"], dep)
    dw_ff2 = _wgrad("wgrad_ff2", (sv["a"], df), (NSH, nt),
                    [pl.BlockSpec((bt, D), lambda k, t: (t, k)), pl.BlockSpec((bt, D), lambda k, t: (t, 0))],
                    pl.BlockSpec((None, D, D), lambda k, t: (k, 0, 0)),
                    jax.ShapeDtypeStruct((NSH, D, D), BF), (D, D), relu2=True)
    dw_ff1 = _wgrad("wgrad_ff1", (sv["h2"], da), (NSH, nt),
                    [pl.BlockSpec((bt, D), lambda k, t: (t, 0)), pl.BlockSpec((bt, D), lambda k, t: (t, k))],
                    pl.BlockSpec((None, D, D), lambda k, t: (k, 0, 0)),
                    jax.ShapeDtypeStruct((NSH, D, D), BF), (D, D))
    dm, dp3, dy3, dz, dg_mix_post = _mix_out_bwd(dx1, sv["m"], p["g_mix_post"], p["w_out"], sv["p3"], sv["z"],
                                                 p["w_branch"])
    dw_out = _wgrad("wgrad_out", (sv["merged"], dm), (nt,),
                    [pl.BlockSpec((bt, D), lambda t: (t, 0)), pl.BlockSpec((bt, D), lambda t: (t, 0))],
                    pl.BlockSpec((D, D), lambda t: (0, 0)),
                    jax.ShapeDtypeStruct((D, D), BF), (D, D)).reshape(NSH, rk, D)
    ysp = pl.BlockSpec((bt, rk), lambda b, k, t: (t, k))
    dw_br = _wgrad("wgrad_branch", (sv["ya"], sv["yc"], sv["ys"], dp3), (3, NSH, nt),
                   [ysp, ysp, ysp, pl.BlockSpec((None, bt, D), lambda b, k, t: (b, t, 0))],
                   pl.BlockSpec((None, None, rk, D), lambda b, k, t: (k, b, 0, 0)),
                   jax.ShapeDtypeStruct((NSH, 3, rk, D), BF), (rk, D), pick=lambda: pl.program_id(0))
    dz, dwa = _mix_a_bwd(dz, dy3, sv["z"], p["conv_a_w"], S)
    dz, dwc, dbc, dclg, dclb = _mix_b_bwd(dz, dy3, sv["s"], sv["z"], p["conf_dw_w"], p["conf_ln_g"],
                                          p["conf_ln_b"], S)
    dz, dws, dbst, dslg, dslb = _mix_s_bwd(dz, dy3, sv["z"], p["sgu_ln_g"], p["sgu_ln_b"], p["sgu_ws"],
                                           p["sgu_wst"], p["sgu_bt"], S)
    dx, dg_mix_pre = _in_proj_bwd(dz, p["w_in"], sv["x"], p["g_mix_pre"], dx1)
    tn = 1280
    nj = p["w_in"].shape[2] // tn
    dw_in = _wgrad("wgrad_in", (sv["h"], dz), (NSH, nj, nt),
                   [pl.BlockSpec((bt, D), lambda k, j, t: (t, 0)),
                    pl.BlockSpec((bt, tn), lambda k, j, t: (t, k * nj + j))],
                   pl.BlockSpec((None, D, tn), lambda k, j, t: (k, 0, j)),
                   jax.ShapeDtypeStruct(p["w_in"].shape, BF), (D, tn))
    tril = jnp.tril(jnp.ones((CHUNK, CHUNK), bool))
    small = dict(norm_mix_pre=dg_mix_pre, norm_mix_post=dg_mix_post, norm_ffn_pre=dg_ffn_pre,
                 norm_ffn_post=dg_ffn_post, conv_a_w=dwa, conf_dw_w=dwc, conf_dw_b=dbc, conf_ln_g=dclg,
                 conf_ln_b=dclb, sgu_ln_g=dslg, sgu_ln_b=dslb,
                 sgu_ws=jnp.where(tril[None], dws, 0.0), sgu_b=dbst.T)
    big = dict(w_in=dw_in, w_branch=dw_br, w_out=dw_out, w_ff1=dw_ff1, w_ff2=dw_ff2)
    return dx, big, small


SMALL_NAMES = ("norm_mix_pre", "norm_mix_post", "norm_ffn_pre", "norm_ffn_post", "conv_a_w", "conf_dw_w",
               "conf_dw_b", "conf_ln_g", "conf_ln_b", "sgu_ln_g", "sgu_ln_b", "sgu_b", "sgu_ws")
SMALL_ROWS = dict(norm_mix_pre=1, norm_mix_post=1, norm_ffn_pre=1, norm_ffn_post=1, conv_a_w=KA, conf_dw_w=KC,
                  conf_dw_b=1, conf_ln_g=1, conf_ln_b=1, sgu_ln_g=1, sgu_ln_b=1, sgu_b=1, sgu_ws=CHUNK)
SUBLANES = 8


def _pad8(r):
    return -(-r // SUBLANES) * SUBLANES


PACK_ROWS = sum(_pad8(r) for r in SMALL_ROWS.values())


def _pack_small(d):
    parts = []
    for n in SMALL_NAMES:
        r = SMALL_ROWS[n]
        parts.append(jnp.pad(d[n].reshape(r, D).astype(F32), ((0, _pad8(r) - r), (0, 0))))
    return jnp.concatenate(parts, axis=0)


def _unpack_small(a, shapes):
    out, r = {}, 0
    for n in SMALL_NAMES:
        out[n] = a[:, r:r + SMALL_ROWS[n]].reshape((a.shape[0],) + tuple(shapes[n]))
        r += _pad8(SMALL_ROWS[n])
    return out


def _me():
    return lax.axis_index("x"), lax.axis_index("y"), lax.axis_index("c")


def _slab(ref, q, a, h=None):
    r = ref.shape[1]
    rows = slice(None) if h is None else pl.ds(h * (r // 2), r // 2)
    return ref.at[pl.ds(q * a, a), rows, :]


def _rows(ref, h):
    r = ref.shape[-2]
    lead = (slice(None),) * (len(ref.shape) - 2)
    return ref.at[lead + (pl.ds(h * (r // 2), r // 2), slice(None))]


def _rcopy(src, dst, sems, idx, dev):
    return pltpu.make_async_remote_copy(src_ref=src, dst_ref=dst, send_sem=sems[0].at[idx], recv_sem=sems[1].at[idx],
                                        device_id=dev, device_id_type=MESH)


def _gather_weights(srcs, layer):
    n = len(srcs)

    def body(*refs):
        src, dst = refs[:n], refs[n:2 * n]
        sems = refs[2 * n:2 * n + 2]
        lsem = refs[2 * n + 2]
        x, y, c = _me()
        k = 2 * x + y
        chips = [(1 - x, y), (x, 1 - y), (1 - x, 1 - y)]
        av = [s.shape[1] for s in src]
        own = [pltpu.make_async_copy(src[i].at[layer], _slab(dst[i], k, av[i]), lsem.at[i]) for i in range(n)]
        for cp in own:
            cp.start()
        first = []
        for j, (qx, qy) in enumerate(chips):
            for i in range(n):
                first.append(_rcopy(_rows(src[i].at[layer], c), _slab(dst[i], k, av[i], c), sems, j * n + i,
                                    (qx, qy, c)))
        for cp in first:
            cp.start()
        passed = []
        for j, (qx, qy) in enumerate(chips):
            kq = 2 * qx + qy
            for i in range(n):
                got = _slab(dst[i], kq, av[i], c)
                _rcopy(got, got, sems, j * n + i, (x, y, c)).wait_recv()
                cp = _rcopy(got, got, sems, 3 * n + j * n + i, (x, y, 1 - c))
                cp.start()
                passed.append(cp)
        for j, (qx, qy) in enumerate(chips):
            kq = 2 * qx + qy
            for i in range(n):
                other = _slab(dst[i], kq, av[i], 1 - c)
                _rcopy(other, other, sems, 3 * n + j * n + i, (x, y, c)).wait_recv()
        for cp in first + passed:
            cp.wait_send()
        for cp in own:
            cp.wait()

    outs = [jax.ShapeDtypeStruct((NSH * s.shape[1],) + s.shape[2:], s.dtype) for s in srcs]
    return pl.pallas_call(
        body, name="gather_weights", in_specs=[ANY] * n, out_specs=[ANY] * n, out_shape=outs,
        scratch_shapes=[pltpu.SemaphoreType.DMA((6 * n,)), pltpu.SemaphoreType.DMA((6 * n,)),
                        pltpu.SemaphoreType.DMA((n,))],
    )(*srcs)


def _send_halves_to_sibling(parts):
    n = len(parts)

    def body(*refs):
        src, dst = refs[:n], refs[n:2 * n]
        sems = refs[2 * n:2 * n + 2]
        x, y, c = _me()
        cps = [_rcopy(_rows(src[i], 1 - c), dst[i], sems, i, (x, y, 1 - c)) for i in range(n)]
        for cp in cps:
            cp.start()
        for cp in cps:
            cp.wait()

    outs = [jax.ShapeDtypeStruct((p.shape[0], p.shape[1] // 2, p.shape[2]), p.dtype) for p in parts]
    return pl.pallas_call(
        body, name="pair_exchange", in_specs=[ANY] * n, out_specs=[ANY] * n, out_shape=outs,
        scratch_shapes=[pltpu.SemaphoreType.DMA((n,)), pltpu.SemaphoreType.DMA((n,))],
    )(*parts)


def _pair_add(part, sib, c):
    A, R, C = part.shape
    hr = R // 2
    br = min(hr, 512)
    nb = hr // br

    def body(c_ref, p_ref, s_ref, o_ref):
        o_ref[...] = (p_ref[...].astype(F32) + s_ref[...].astype(F32)).astype(BF)

    return pl.pallas_call(
        body, name="pair_add",
        grid_spec=pltpu.PrefetchScalarGridSpec(
            num_scalar_prefetch=1, grid=(A, nb),
            in_specs=[pl.BlockSpec((None, br, C), lambda a, i, c_ref: (a, c_ref[0] * nb + i, 0)),
                      pl.BlockSpec((None, br, C), lambda a, i, c_ref: (a, i, 0))],
            out_specs=pl.BlockSpec((None, br, C), lambda a, i, c_ref: (a, i, 0))),
        out_shape=jax.ShapeDtypeStruct((A, hr, C), BF),
        compiler_params=_cp("arbitrary", "arbitrary"),
    )(c, part, sib)


def _other_chips(x, y):
    return [(1 - x, y), (x, 1 - y), (1 - x, 1 - y)]


def _split_call(name, copies, srcs, lands, sems=None, after=None):
    n, m = len(srcs), len(lands)
    hbm = lambda t: pltpu.HBM(t.shape, t.dtype)
    pin = lambda t: pltpu.with_memory_space_constraint(t, pltpu.HBM)
    thru = [hbm(t) for t in srcs] + [hbm(t) for t in lands]
    sem_spec = pl.BlockSpec(memory_space=pltpu.SEMAPHORE)
    effect = pltpu.CompilerParams(has_side_effects=pltpu.SideEffectType.DATAFLOW_SIDE_EFFECTING)
    if sems is None:
        def start_body(*refs):
            src, land = refs[:n], refs[n:n + m]
            ssem, rsem = refs[n + m], refs[n + m + 1]
            token = refs[-1]
            cps = copies(src, land, (ssem, rsem))
            for cp in cps:
                cp.start()
            token[...] = jnp.zeros_like(token)

        ncp = copies.count
        out = pl.pallas_call(
            start_body, name=name,
            out_shape=(pltpu.SemaphoreType.DMA((ncp,)), pltpu.SemaphoreType.DMA((ncp,)), *thru,
                       jax.ShapeDtypeStruct((8, 128), F32)),
            in_specs=[ANY] * (n + m),
            out_specs=(sem_spec, sem_spec, *([ANY] * (n + m)), pl.BlockSpec(memory_space=pltpu.VMEM)),
            input_output_aliases={i: 2 + i for i in range(n + m)},
            compiler_params=effect,
        )(*[pin(t) for t in srcs], *[pin(t) for t in lands])
        return out[0], out[1], list(out[2:2 + n]), list(out[2 + n:2 + n + m]), out[-1]

    def wait_body(*refs):
        src, land = refs[:n], refs[n:n + m]
        ssem, rsem = refs[n + m], refs[n + m + 1]
        for cp in copies(src, land, (ssem, rsem)):
            cp.wait_send()
            cp.wait_recv()

    out = pl.pallas_call(
        wait_body, name=name, out_shape=tuple(thru),
        in_specs=[ANY] * (n + m) + [sem_spec, sem_spec, ANY],
        out_specs=tuple([ANY] * (n + m)),
        input_output_aliases={i: i for i in range(n + m)},
        compiler_params=effect,
    )(*srcs, *lands, sems[0], sems[1], after)
    return list(out[:n]), list(out[n:])


class _GatherCopies:
    def __init__(self, n, layer):
        self.n, self.layer, self.count = n, layer, 3 * n

    def __call__(self, src, land, sems):
        x, y, c = _me()
        k = 2 * x + y
        cps = []
        for j, (qx, qy) in enumerate(_other_chips(x, y)):
            for i in range(self.n):
                a = src[i].shape[1]
                cps.append(_rcopy(_rows(src[i].at[self.layer], c), _slab(land[i], k, a, c), sems, j * self.n + i,
                                  (qx, qy, c)))
        return cps


def _gather_finish(srcs, lands, layer):
    n = len(srcs)

    def body(*refs):
        src, land_in, dst = refs[:n], refs[n:2 * n], refs[2 * n:3 * n]
        sems = refs[3 * n:3 * n + 2]
        lsem = refs[3 * n + 2]
        x, y, c = _me()
        k = 2 * x + y
        av = [s.shape[1] for s in src]
        own = [pltpu.make_async_copy(src[i].at[layer], _slab(dst[i], k, av[i]), lsem.at[i]) for i in range(n)]
        cps = []
        for j, (qx, qy) in enumerate(_other_chips(x, y)):
            for i in range(n):
                got = _slab(dst[i], 2 * qx + qy, av[i], c)
                cps.append(_rcopy(got, got, sems, j * n + i, (x, y, 1 - c)))
        for cp in own + cps:
            cp.start()
        for j, (qx, qy) in enumerate(_other_chips(x, y)):
            for i in range(n):
                other = _slab(dst[i], 2 * qx + qy, av[i], 1 - c)
                _rcopy(other, other, sems, j * n + i, (x, y, c)).wait_recv()
        for cp in cps:
            cp.wait_send()
        for cp in own:
            cp.wait()

    return pl.pallas_call(
        body, name="gather_finish", in_specs=[ANY] * (2 * n), out_specs=[ANY] * n,
        out_shape=[jax.ShapeDtypeStruct(t.shape, t.dtype) for t in lands],
        input_output_aliases={n + i: i for i in range(n)},
        scratch_shapes=[pltpu.SemaphoreType.DMA((3 * n,)), pltpu.SemaphoreType.DMA((3 * n,)),
                        pltpu.SemaphoreType.DMA((n,))],
    )(*srcs, *lands)


class _ScatterCopies:
    def __init__(self, n):
        self.n, self.count = n, 3 * n

    def __call__(self, src, land, sems):
        x, y, c = _me()
        k = 2 * x + y
        cps = []
        for j, (qx, qy) in enumerate(_other_chips(x, y)):
            for i in range(self.n):
                a = src[i].shape[0] // NSH
                cps.append(_rcopy(_slab(src[i], 2 * qx + qy, a), _slab(land[i], k, a), sems, j * self.n + i,
                                  (qx, qy, c)))
        return cps


def _sum_chips(own, rcv, acc, layer, nlayers, idx):
    A, hr, C = rcv.shape
    a = A // NSH
    br = min(hr, 512)
    nb = hr // br

    def body(*refs):
        r0, r1, r2, r3 = refs[1:5]
        o_ref = refs[-1]
        o_ref[...] = ((r0[...].astype(F32) + r1[...].astype(F32)) + r2[...].astype(F32)) + r3[...].astype(F32)

    slot = lambda s: pl.BlockSpec((None, br, C), lambda e, i, ix: (ix[s] * a + e, i, 0))
    ops = [own, rcv, rcv, rcv]
    in_specs = [slot(0), slot(1), slot(2), slot(3)]
    aliases = {}
    if acc is not None:
        ops.append(acc)
        in_specs.append(ANY)
        aliases = {5: 0}
    return pl.pallas_call(
        body, name="sum_chips",
        grid_spec=pltpu.PrefetchScalarGridSpec(
            num_scalar_prefetch=1, grid=(a, nb), in_specs=in_specs,
            out_specs=pl.BlockSpec((None, None, br, C), lambda e, i, ix: (layer, e, ix[4] * nb + i, 0))),
        out_shape=jax.ShapeDtypeStruct((nlayers, a, 2 * hr, C), F32), input_output_aliases=aliases,
        compiler_params=_cp("arbitrary", "arbitrary"),
    )(idx, *ops)


def _join_halves(fulls):
    n = len(fulls)

    def body(*refs):
        buf = refs[n:2 * n]
        sems = refs[2 * n:2 * n + 2]
        x, y, c = _me()
        cps = [_rcopy(_rows(buf[i], c), _rows(buf[i], c), sems, i, (x, y, 1 - c)) for i in range(n)]
        for cp in cps:
            cp.start()
        for i in range(n):
            _rcopy(_rows(buf[i], 1 - c), _rows(buf[i], 1 - c), sems, i, (x, y, c)).wait_recv()
        for cp in cps:
            cp.wait_send()

    return pl.pallas_call(
        body, name="join_halves", in_specs=[ANY] * n, out_specs=[ANY] * n,
        out_shape=[jax.ShapeDtypeStruct(t.shape, t.dtype) for t in fulls],
        input_output_aliases={i: i for i in range(n)},
        scratch_shapes=[pltpu.SemaphoreType.DMA((n,)), pltpu.SemaphoreType.DMA((n,))],
    )(*fulls)


def _gather_all(block):
    R, C = block.shape

    def body(src, dst, ssem, rsem, lsem):
        sems = (ssem, rsem)
        x, y, c = _me()
        chips = [(1 - x, y), (x, 1 - y), (1 - x, 1 - y)]

        def at(px, py, pc):
            return dst.at[4 * px + 2 * py + pc]

        own = pltpu.make_async_copy(src, at(x, y, c), lsem)
        own.start()
        first = [_rcopy(src, at(x, y, c), sems, 0, (x, y, 1 - c))]
        first += [_rcopy(src, at(x, y, c), sems, 1 + j, (qx, qy, c)) for j, (qx, qy) in enumerate(chips)]
        for cp in first:
            cp.start()
        passed = []
        for j, (qx, qy) in enumerate(chips):
            got = at(qx, qy, c)
            _rcopy(got, got, sems, 1 + j, (x, y, c)).wait_recv()
            cp = _rcopy(got, got, sems, 4 + j, (x, y, 1 - c))
            cp.start()
            passed.append(cp)
        sib = at(x, y, 1 - c)
        _rcopy(sib, sib, sems, 0, (x, y, c)).wait_recv()
        for j, (qx, qy) in enumerate(chips):
            other = at(qx, qy, 1 - c)
            _rcopy(other, other, sems, 4 + j, (x, y, c)).wait_recv()
        for cp in first + passed:
            cp.wait_send()
        own.wait()

    return pl.pallas_call(
        body, name="gather_all", in_specs=[ANY], out_specs=ANY,
        out_shape=jax.ShapeDtypeStruct((NDEV, R, C), block.dtype),
        scratch_shapes=[pltpu.SemaphoreType.DMA((7,)), pltpu.SemaphoreType.DMA((7,)), pltpu.SemaphoreType.DMA],
    )(block)


def _sum_devices(g):
    _, R, C = g.shape
    br = 264 if R % 264 == 0 else R

    def body(g_ref, o_ref):
        acc = g_ref[0]
        for d in range(1, NDEV):
            acc = acc + g_ref[d]
        o_ref[...] = acc

    return pl.pallas_call(
        body, name="sum_devices", grid=(R // br,),
        in_specs=[pl.BlockSpec((NDEV, br, C), lambda i: (0, i, 0))],
        out_specs=pl.BlockSpec((br, C), lambda i: (i, 0)),
        out_shape=jax.ShapeDtypeStruct((R, C), F32),
        compiler_params=_cp("arbitrary"),
    )(g)


def _adamw(w, g, m, v):
    shape = w.shape
    C = shape[-1]
    R = shape[-2]
    A = 1
    for s in shape[:-2]:
        A *= s
    br = R
    while br * C > 256 * 1024 and br % 16 == 0:
        br //= 2
    c1 = 1.0 / (1.0 - ADAM_B1 ** ADAM_STEP)
    c2 = 1.0 / (1.0 - ADAM_B2 ** ADAM_STEP)

    def body(w_ref, g_ref, m_ref, v_ref, d_ref, nm_ref, nv_ref):
        gv = g_ref[...]
        nm = ADAM_B1 * m_ref[...] + (1.0 - ADAM_B1) * gv
        nv = ADAM_B2 * v_ref[...] + (1.0 - ADAM_B2) * (gv * gv)
        nm_ref[...] = nm
        nv_ref[...] = nv
        d_ref[...] = -ADAM_LR * ((nm * c1) / (jnp.sqrt(nv * c2) + ADAM_EPS) + ADAM_WD * w_ref[...])

    blk = pl.BlockSpec((None, br, C), lambda a, i: (a, i, 0))
    outs = pl.pallas_call(
        body, name="adamw", grid=(A, R // br), in_specs=[blk] * 4, out_specs=[blk] * 3,
        out_shape=[jax.ShapeDtypeStruct((A, R, C), F32)] * 3,
        compiler_params=_cp("arbitrary", "arbitrary"),
    )(*(t.reshape(A, R, C) for t in (w, g, m, v)))
    return tuple(o.reshape(shape) for o in outs)


WEIGHTS = ("norm_mix_pre", "norm_mix_post", "norm_ffn_pre", "norm_ffn_post", "w_in", "conv_a_w", "conf_dw_w",
           "conf_dw_b", "conf_ln_g", "conf_ln_b", "sgu_ln_g", "sgu_ln_b", "sgu_ws", "sgu_b", "w_branch", "w_out",
           "w_ff1", "w_ff2")
BIG = ("w_in", "w_branch", "w_out", "w_ff1", "w_ff2")
CONV_ROWS = 48


def kernel(x, norm_mix_pre, norm_mix_post, norm_ffn_pre, norm_ffn_post, w_in, conv_a_w, conf_dw_w, conf_dw_b, conf_ln_g, conf_ln_b, sgu_ln_g, sgu_ln_b, sgu_ws, sgu_b, w_branch, w_out, w_ff1, w_ff2, loss_target, m_norm_mix_pre, m_norm_mix_post, m_norm_ffn_pre, m_norm_ffn_post, m_w_in, m_conv_a_w, m_conf_dw_w, m_conf_dw_b, m_conf_ln_g, m_conf_ln_b, m_sgu_ln_g, m_sgu_ln_b, m_sgu_ws, m_sgu_b, m_w_branch, m_w_out, m_w_ff1, m_w_ff2, v_norm_mix_pre, v_norm_mix_post, v_norm_ffn_pre, v_norm_ffn_post, v_w_in, v_conv_a_w, v_conf_dw_w, v_conf_dw_b, v_conf_ln_g, v_conf_ln_b, v_sgu_ln_g, v_sgu_ln_b, v_sgu_ws, v_sgu_b, v_w_branch, v_w_out, v_w_ff1, v_w_ff2):
    w = dict(norm_mix_pre=norm_mix_pre, norm_mix_post=norm_mix_post, norm_ffn_pre=norm_ffn_pre,
             norm_ffn_post=norm_ffn_post, w_in=w_in, conv_a_w=conv_a_w, conf_dw_w=conf_dw_w, conf_dw_b=conf_dw_b,
             conf_ln_g=conf_ln_g, conf_ln_b=conf_ln_b, sgu_ln_g=sgu_ln_g, sgu_ln_b=sgu_ln_b, sgu_ws=sgu_ws,
             sgu_b=sgu_b, w_branch=w_branch, w_out=w_out, w_ff1=w_ff1, w_ff2=w_ff2)
    mom = dict(norm_mix_pre=m_norm_mix_pre, norm_mix_post=m_norm_mix_post, norm_ffn_pre=m_norm_ffn_pre,
               norm_ffn_post=m_norm_ffn_post, w_in=m_w_in, conv_a_w=m_conv_a_w, conf_dw_w=m_conf_dw_w,
               conf_dw_b=m_conf_dw_b, conf_ln_g=m_conf_ln_g, conf_ln_b=m_conf_ln_b, sgu_ln_g=m_sgu_ln_g,
               sgu_ln_b=m_sgu_ln_b, sgu_ws=m_sgu_ws, sgu_b=m_sgu_b, w_branch=m_w_branch, w_out=m_w_out,
               w_ff1=m_w_ff1, w_ff2=m_w_ff2)
    var = dict(norm_mix_pre=v_norm_mix_pre, norm_mix_post=v_norm_mix_post, norm_ffn_pre=v_norm_ffn_pre,
               norm_ffn_post=v_norm_ffn_post, w_in=v_w_in, conv_a_w=v_conv_a_w, conf_dw_w=v_conf_dw_w,
               conf_dw_b=v_conf_dw_b, conf_ln_g=v_conf_ln_g, conf_ln_b=v_conf_ln_b, sgu_ln_g=v_sgu_ln_g,
               sgu_ln_b=v_sgu_ln_b, sgu_ws=v_sgu_ws, sgu_b=v_sgu_b, w_branch=v_w_branch, w_out=v_w_out,
               w_ff1=v_w_ff1, w_ff2=v_w_ff2)
    L = w_in.shape[0]
    nseq, S, _ = x.shape
    T = nseq * S
    rk = D // NSH
    mx, my, mc = _me()
    k_chip = 2 * mx + my

    big_src = [w_in.astype(BF).reshape(L, 1, D, w_in.shape[2]), w_branch.astype(BF),
               w_out.astype(BF).reshape(L, 1, rk, D), w_ff1.astype(BF).reshape(L, 1, D, w_ff1.shape[2]),
               w_ff2.astype(BF).reshape(L, 1, w_ff2.shape[1], D)]
    conv_src = jnp.concatenate(
        [jnp.pad(conv_a_w, ((0, 0), (0, SUBLANES - KA), (0, 0))), jnp.pad(conf_dw_w, ((0, 0), (0, 1), (0, 0))),
         jnp.zeros((L, CONV_ROWS - SUBLANES - KC - 1, rk), F32)], axis=1)[None]
    (conv_g,) = _gather_weights([conv_src], 0)
    conv_full = conv_g.reshape(NSH, L, CONV_ROWS, rk).transpose(1, 2, 0, 3).reshape(L, CONV_ROWS, D)

    def layer_params(l, gathered):
        g_in, g_br, g_out, g_ff1, g_ff2 = gathered
        return dict(
            g_mix_pre=norm_mix_pre[l][None], g_mix_post=norm_mix_post[l][None], g_ffn_pre=norm_ffn_pre[l][None],
            g_ffn_post=norm_ffn_post[l][None], w_in=g_in, conv_a_w=conv_full[l, :KA],
            conf_dw_w=conv_full[l, SUBLANES:SUBLANES + KC], conf_dw_b=conf_dw_b[l][None],
            conf_ln_g=conf_ln_g[l][None], conf_ln_b=conf_ln_b[l][None], sgu_ln_g=sgu_ln_g[l][None],
            sgu_ln_b=sgu_ln_b[l][None], sgu_ws=sgu_ws[l], sgu_wst=jnp.swapaxes(sgu_ws[l], 1, 2),
            sgu_bt=sgu_b[l].T, w_branch=g_br.reshape(NSH, 3, rk, D), w_out=g_out.reshape(D, D), w_ff1=g_ff1,
            w_ff2=g_ff2.reshape(NSH * w_ff2.shape[1], D))

    def gather_start(l, srcs):
        lands = [lax.empty((NSH * s.shape[1],) + s.shape[2:], s.dtype) for s in srcs]
        return _split_call(f"gather_start_{l}", _GatherCopies(len(srcs), l), srcs, lands)

    xt = x.reshape(T, D)
    layers, saved = [], []
    flight = gather_start(0, big_src)
    after = xt
    for l in range(L):
        ssem, rsem, srcs, lands, _ = flight
        srcs, lands = _split_call(f"gather_wait_{l}", _GatherCopies(len(srcs), l), srcs, lands, (ssem, rsem), after)
        if l + 1 < L:
            flight = gather_start(l + 1, srcs)
            dep = flight[4]
        else:
            dep = jnp.zeros((8, 128), F32)
        p = layer_params(l, _gather_finish(srcs, lands, l))
        xt, sv = _layer_fwd(xt, p, S, dep)
        layers.append(p)
        saved.append(sv)
        after = xt
    dx, loss_row = _loss_head(xt, loss_target.reshape(T, D))
    loss = lax.psum(loss_row[0, 0], ("x", "y", "c"))

    c_arr = jnp.reshape(mc, (1,)).astype(jnp.int32)
    idx = jnp.stack([k_chip, k_chip ^ 2, k_chip ^ 1, k_chip ^ 3, mc]).astype(jnp.int32)
    fulls = [None] * len(BIG)
    smalls = [None] * L
    flight = None
    dep = jnp.zeros((8, 128), F32)

    def land_scatter(l, fl, after, fulls):
        ssem, rsem, sums, rcv, _ = fl
        sums, rcv = _split_call(f"scatter_wait_{l}", _ScatterCopies(len(sums)), sums, rcv, (ssem, rsem), after)
        return [_sum_chips(o, r, f, l, L, idx) for o, r, f in zip(sums, rcv, fulls)]

    for l in reversed(range(L)):
        dx, big, small = _layer_bwd(dx, layers[l], saved[l], S, dep)
        smalls[l] = _pack_small(small)
        if flight is not None:
            fulls = land_scatter(l + 1, flight, dx, fulls)
        parts = [big["w_in"], big["w_branch"].reshape(NSH * 3, rk, D), big["w_out"], big["w_ff1"], big["w_ff2"]]
        sib = _send_halves_to_sibling(parts)
        sums = [_pair_add(p, s, c_arr) for p, s in zip(parts, sib)]
        rcv = [lax.empty(s.shape, s.dtype) for s in sums]
        flight = _split_call(f"scatter_start_{l}", _ScatterCopies(len(sums)), sums, rcv)
        dep = flight[4]
    fulls = land_scatter(0, flight, dx, fulls)
    full = _join_halves(fulls)
    grads = {n: f.reshape(w[n].shape) for n, f in zip(BIG, full)}

    packed = jnp.concatenate(smalls, axis=0)
    small_sum = _sum_devices(_gather_all(packed)).reshape(L, PACK_ROWS, D)
    shapes = {n: (w[n].shape[1:] if n not in ("conv_a_w", "conf_dw_w") else (w[n].shape[1], D)) for n in SMALL_NAMES}
    sg = _unpack_small(small_sum, shapes)
    for n in SMALL_NAMES:
        if n in ("conv_a_w", "conf_dw_w"):
            grads[n] = lax.dynamic_slice_in_dim(sg[n], k_chip * rk, rk, axis=2)
        else:
            grads[n] = sg[n]

    delta, new_m, new_v = {}, {}, {}
    for n in BIG:
        delta[n], new_m[n], new_v[n] = _adamw(w[n], grads[n], mom[n], var[n])
    for n in SMALL_NAMES:
        sh = w[n].shape
        flat = (sh[0] * sh[1], sh[2]) if n in ("conv_a_w", "conf_dw_w") else (-1, D)
        d, nm, nv = _adamw(*(t.reshape(flat) for t in (w[n], grads[n], mom[n], var[n])))
        delta[n], new_m[n], new_v[n] = d.reshape(sh), nm.reshape(sh), nv.reshape(sh)

    return (loss, dx.reshape(x.shape), *[grads[n] for n in WEIGHTS], *[delta[n] for n in WEIGHTS],
            *[new_m[n] for n in WEIGHTS], *[new_v[n] for n in WEIGHTS])
```

```python
import functools

import jax
import jax.numpy as jnp
from jax import lax
from jax.experimental import pallas as pl
from jax.experimental.pallas import tpu as pltpu

D = 1024
HEADS = 8
CHUNK = 128
KA = 3
KC = 31
HALO = 32
NSH = 4
NDEV = 8
EPS = 1e-6
BF = jnp.bfloat16
F32 = jnp.float32
VMEM_LIMIT = 56 * 1024 * 1024

ADAM_LR = 0.001
ADAM_B1 = 0.9
ADAM_B2 = 0.999
ADAM_EPS = 1e-08
ADAM_WD = 0.01
ADAM_STEP = 10

MESH = pl.DeviceIdType.MESH
ANY = pl.BlockSpec(memory_space=pl.ANY)


def _cp(*sem):
    return pltpu.CompilerParams(dimension_semantics=sem, vmem_limit_bytes=VMEM_LIMIT)


def _sig(x):
    return 1.0 / (1.0 + jnp.exp(-x))


_GC = 0.7978845608028654


def _gelu(x):
    x2 = x * x
    t = jnp.tanh(_GC * x * (1.0 + 0.044715 * x2))
    y = 0.5 * x * (1.0 + t)
    dy = 0.5 * (1.0 + t) + 0.5 * x * (1.0 - t * t) * _GC * (1.0 + 3.0 * 0.044715 * x2)
    return y, dy


def _rms_fwd(x, g):
    r = lax.rsqrt(jnp.mean(x * x, axis=-1, keepdims=True) + EPS)
    return x * r * g


def _rms_bwd(dy, x, g):
    r = lax.rsqrt(jnp.mean(x * x, axis=-1, keepdims=True) + EPS)
    xn = x * r
    dyg = dy * g
    dx = r * (dyg - xn * jnp.mean(dyg * xn, axis=-1, keepdims=True))
    return dx, jnp.sum(dy * xn, axis=0, keepdims=True)


def _ln_stats(x):
    mu = jnp.mean(x, axis=-1, keepdims=True)
    xc = x - mu
    r = lax.rsqrt(jnp.mean(xc * xc, axis=-1, keepdims=True) + EPS)
    return xc * r, r


def _ln_bwd(dn, n, r):
    return r * (dn - jnp.mean(dn, axis=-1, keepdims=True) - n * jnp.mean(dn * n, axis=-1, keepdims=True))


def _dot(a, b):
    return jnp.dot(a, b, preferred_element_type=F32)


def _dot_nt(a, b):
    return lax.dot_general(a, b, (((1,), (1,)), ((), ())), preferred_element_type=F32)


def _dot_tn(a, b):
    return lax.dot_general(a, b, (((0,), (0,)), ((), ())), preferred_element_type=F32)


def _in_proj(x, g, w, dep):
    T = x.shape[0]
    nc = w.shape[2]
    tm = min(T, 1024)
    tn = 1280
    nj = nc // tn

    def body(x_ref, g_ref, w_ref, dep_ref, h_ref, z_ref, h_scr):
        @pl.when((pl.program_id(1) == 0) & (pl.program_id(2) == 0))
        def _():
            h = _rms_fwd(x_ref[...], g_ref[...]).astype(BF)
            h_scr[...] = h
            h_ref[...] = h
        z_ref[...] = _dot(h_scr[...], w_ref[...]).astype(BF)

    return pl.pallas_call(
        body, name="in_proj", grid=(T // tm, NSH, nj),
        in_specs=[pl.BlockSpec((tm, D), lambda i, k, j: (i, 0)),
                  pl.BlockSpec((1, D), lambda i, k, j: (0, 0)),
                  pl.BlockSpec((None, D, tn), lambda i, k, j: (k, 0, j)), ANY],
        out_specs=[pl.BlockSpec((tm, D), lambda i, k, j: (i, 0)),
                   pl.BlockSpec((tm, tn), lambda i, k, j: (i, k * nj + j))],
        out_shape=[jax.ShapeDtypeStruct((T, D), BF), jax.ShapeDtypeStruct((T, NSH * nc), BF)],
        scratch_shapes=[pltpu.VMEM((tm, D), BF)],
        compiler_params=_cp("arbitrary", "arbitrary", "arbitrary"),
    )(x, g, w, dep)


def _tile_specs(tt, nt_total, reverse):
    def tile(i):
        return (nt_total - 1 - i) if reverse else i

    def cur(c):
        return pl.BlockSpec((tt, D), lambda i, *_: (tile(i), c))

    def halo(c):
        return pl.BlockSpec((HALO, D), lambda i, *_: (jnp.maximum(tile(i) * (tt // HALO) - 1, 0), c))

    def row(r=1):
        return pl.BlockSpec((r, D), lambda i, *_: (0, 0))

    return tile, cur, halo, row


def _causal_conv(ext, w_ref, ntap, tt):
    acc = None
    for k in range(ntap):
        term = w_ref[k:k + 1, :] * ext[pl.ds(HALO - (ntap - 1) + k, tt), :]
        acc = term if acc is None else acc + term
    return acc


def _anticausal_conv(ext, w_ref, ntap, tt):
    acc = None
    for k in range(ntap):
        term = w_ref[k:k + 1, :] * ext[pl.ds(ntap - 1 - k, tt), :]
        acc = term if acc is None else acc + term
    return acc


def _conv_wgrad(dw_ref, dout, ext, ntap, tt):
    for k in range(ntap):
        dw_ref[k:k + 1, :] += jnp.sum(dout * ext[pl.ds(HALO - (ntap - 1) + k, tt), :], axis=0, keepdims=True)


def _mix_a_fwd(z, wa, S):
    T = z.shape[0]
    tt = min(S, 256)
    nt = S // tt
    _, cur, halo, row = _tile_specs(tt, T // tt, False)

    def body(ah, ab, ac, ah_h, ac_h, w_ref, y_ref, ext):
        first = (pl.program_id(0) % nt) == 0
        ph = ah_h[...].astype(F32) * ac_h[...].astype(F32)
        ext[0:HALO, :] = jnp.where(first, 0.0, ph)
        ext[HALO:, :] = ah[...].astype(F32) * ac[...].astype(F32)
        q = _causal_conv(ext, w_ref, KA, tt)
        y_ref[...] = (ab[...].astype(F32) * q).astype(BF)

    return pl.pallas_call(
        body, name="mix_a_fwd", grid=(T // tt,),
        in_specs=[cur(0), cur(1), cur(2), halo(0), halo(2), row(KA)],
        out_specs=pl.BlockSpec((tt, D), lambda i: (i, 0)),
        out_shape=jax.ShapeDtypeStruct((T, D), BF),
        scratch_shapes=[pltpu.VMEM((HALO + tt, D), F32)],
        compiler_params=_cp("arbitrary"),
    )(z, z, z, z, z, wa)


def _mix_b_fwd(z, wc, bc, lg, lb, S):
    T = z.shape[0]
    tt = min(S, 256)
    nt = S // tt
    _, cur, halo, row = _tile_specs(tt, T // tt, False)

    def body(ca, cg, ca_h, cg_h, w_ref, bc_ref, lg_ref, lb_ref, y_ref, s_ref, ext):
        first = (pl.program_id(0) % nt) == 0
        rh = ca_h[...].astype(F32) * _sig(cg_h[...].astype(F32))
        ext[0:HALO, :] = jnp.where(first, 0.0, rh)
        ext[HALO:, :] = ca[...].astype(F32) * _sig(cg[...].astype(F32))
        s = _causal_conv(ext, w_ref, KC, tt) + bc_ref[...]
        s_ref[...] = s.astype(BF)
        n, _ = _ln_stats(s)
        t = n * lg_ref[...] + lb_ref[...]
        y_ref[...] = (t * _sig(t)).astype(BF)

    return pl.pallas_call(
        body, name="mix_b_fwd", grid=(T // tt,),
        in_specs=[cur(3), cur(4), halo(3), halo(4), row(KC), row(), row(), row()],
        out_specs=[pl.BlockSpec((tt, D), lambda i: (i, 0))] * 2,
        out_shape=[jax.ShapeDtypeStruct((T, D), BF)] * 2,
        scratch_shapes=[pltpu.VMEM((HALO + tt, D), F32)],
        compiler_params=_cp("arbitrary"),
    )(z, z, z, z, wc, bc, lg, lb)


def _causal_mask(transposed):
    r = lax.broadcasted_iota(jnp.int32, (CHUNK, CHUNK), 0)
    c = lax.broadcasted_iota(jnp.int32, (CHUNK, CHUNK), 1)
    return (c >= r) if transposed else (r >= c)


def _mix_s_fwd(z, lg, lb, ws, bst, S):
    T = z.shape[0]
    tt = min(S, 256)
    _, cur, _, row = _tile_specs(tt, T // tt, False)

    def body(su, sv, lg_ref, lb_ref, ws_ref, bst_ref, y_ref, u_scr, vn_scr):
        u_scr[...] = _gelu(su[...].astype(F32))[0]
        n, _ = _ln_stats(_gelu(sv[...].astype(F32))[0])
        vn_scr[...] = (n * lg_ref[...] + lb_ref[...]).astype(BF)
        mask = _causal_mask(False)
        for h in range(HEADS):
            wm = jnp.where(mask, ws_ref[h], 0.0).astype(BF)
            cols = slice(h * CHUNK, (h + 1) * CHUNK)
            for c in range(tt // CHUNK):
                rows = slice(c * CHUNK, (c + 1) * CHUNK)
                mixed = _dot(wm, vn_scr[rows, cols]) + bst_ref[:, h:h + 1]
                y_ref[rows, cols] = (u_scr[rows, cols] * mixed).astype(BF)

    return pl.pallas_call(
        body, name="mix_s_fwd", grid=(T // tt,),
        in_specs=[cur(5), cur(6), row(), row(),
                  pl.BlockSpec((HEADS, CHUNK, CHUNK), lambda i: (0, 0, 0)),
                  pl.BlockSpec((CHUNK, HEADS), lambda i: (0, 0))],
        out_specs=pl.BlockSpec((tt, D), lambda i: (i, 0)),
        out_shape=jax.ShapeDtypeStruct((T, D), BF),
        scratch_shapes=[pltpu.VMEM((tt, D), F32), pltpu.VMEM((tt, D), BF)],
        compiler_params=_cp("arbitrary"),
    )(z, z, lg, lb, ws, bst)


def _mix_out_fwd(ya, yc, ys, z, x, wb, wo, gp):
    T = x.shape[0]
    tm = min(T, 256)
    rk = D // NSH

    def body(ya_ref, yc_ref, ys_ref, ga, gc, gs, x_ref, wb_ref, wo_ref, gp_ref, p_ref, mg_ref, m_ref, x1_ref):
        acc = None
        for b, (y_ref, g_ref) in enumerate(((ya_ref, ga), (yc_ref, gc), (ys_ref, gs))):
            pb = None
            for k in range(NSH):
                part = _dot(y_ref[:, k * rk:(k + 1) * rk], wb_ref[k, b])
                pb = part if pb is None else pb + part
            p_ref[b] = pb.astype(BF)
            term = _sig(g_ref[...].astype(F32)) * pb
            acc = term if acc is None else acc + term
        mg = acc.astype(BF)
        mg_ref[...] = mg
        m = _dot(mg, wo_ref[...])
        m_ref[...] = m.astype(BF)
        x1_ref[...] = x_ref[...] + _rms_fwd(m, gp_ref[...])

    rowblk = pl.BlockSpec((tm, D), lambda i: (i, 0))
    return pl.pallas_call(
        body, name="mix_out_fwd", grid=(T // tm,),
        in_specs=[rowblk, rowblk, rowblk,
                  pl.BlockSpec((tm, D), lambda i: (i, 7)), pl.BlockSpec((tm, D), lambda i: (i, 8)),
                  pl.BlockSpec((tm, D), lambda i: (i, 9)), rowblk,
                  pl.BlockSpec((NSH, 3, rk, D), lambda i: (0, 0, 0, 0)),
                  pl.BlockSpec((D, D), lambda i: (0, 0)),
                  pl.BlockSpec((1, D), lambda i: (0, 0))],
        out_specs=[pl.BlockSpec((3, tm, D), lambda i: (0, i, 0)), rowblk, rowblk, rowblk],
        out_shape=[jax.ShapeDtypeStruct((3, T, D), BF), jax.ShapeDtypeStruct((T, D), BF),
                   jax.ShapeDtypeStruct((T, D), BF), jax.ShapeDtypeStruct((T, D), F32)],
        compiler_params=_cp("arbitrary"),
    )(ya, yc, ys, z, z, z, x, wb, wo, gp)


def _ffn_fwd(x1, g3, w1, w2, g4):
    T = x1.shape[0]
    tm = min(T, 512)

    def body(x_ref, g3_ref, w1_ref, w2_ref, g4_ref, h_ref, a_ref, f_ref, x2_ref, h_scr, acc):
        k = pl.program_id(1)

        @pl.when(k == 0)
        def _():
            h = _rms_fwd(x_ref[...], g3_ref[...]).astype(BF)
            h_scr[...] = h
            h_ref[...] = h
            acc[...] = jnp.zeros_like(acc)

        a = _dot(h_scr[...], w1_ref[...])
        a_ref[...] = a.astype(BF)
        r = jnp.maximum(a, 0.0)
        acc[...] += _dot((r * r).astype(BF), w2_ref[...])

        @pl.when(k == NSH - 1)
        def _():
            f = acc[...]
            f_ref[...] = f.astype(BF)
            x2_ref[...] = x_ref[...] + _rms_fwd(f, g4_ref[...])

    rowblk = pl.BlockSpec((tm, D), lambda i, k: (i, 0))
    vec = pl.BlockSpec((1, D), lambda i, k: (0, 0))
    return pl.pallas_call(
        body, name="ffn_fwd", grid=(T // tm, NSH),
        in_specs=[rowblk, vec, pl.BlockSpec((None, D, D), lambda i, k: (k, 0, 0)),
                  pl.BlockSpec((D, D), lambda i, k: (k, 0)), vec],
        out_specs=[rowblk, pl.BlockSpec((tm, D), lambda i, k: (i, k)), rowblk, rowblk],
        out_shape=[jax.ShapeDtypeStruct((T, D), BF), jax.ShapeDtypeStruct((T, NSH * D), BF),
                   jax.ShapeDtypeStruct((T, D), BF), jax.ShapeDtypeStruct((T, D), F32)],
        scratch_shapes=[pltpu.VMEM((tm, D), BF), pltpu.VMEM((tm, D), F32)],
        compiler_params=_cp("arbitrary", "arbitrary"),
    )(x1, g3, w1, w2, g4)


def _loss_head(y, target):
    T = y.shape[0]
    tm = min(T, 512)

    def body(y_ref, t_ref, dy_ref, l_ref):
        @pl.when(pl.program_id(0) == 0)
        def _():
            l_ref[...] = jnp.zeros_like(l_ref)
        e = y_ref[...] - t_ref[...]
        dy_ref[...] = e * (1.0 / D)
        l_ref[...] += jnp.sum(e * e) * (0.5 / D)

    rowblk = pl.BlockSpec((tm, D), lambda i: (i, 0))
    return pl.pallas_call(
        body, name="loss_head", grid=(T // tm,),
        in_specs=[rowblk, rowblk],
        out_specs=[rowblk, pl.BlockSpec((1, 128), lambda i: (0, 0))],
        out_shape=[jax.ShapeDtypeStruct((T, D), F32), jax.ShapeDtypeStruct((1, 128), F32)],
        compiler_params=_cp("arbitrary"),
    )(y, target)


def _ffn_bwd(dx2, f, g4, a, w2, w1, x1, g3, dep):
    T = dx2.shape[0]
    tm = min(T, 512)

    def body(dx2_ref, f_ref, g4_ref, a_ref, w2_ref, w1_ref, x1_ref, g3_ref, dep_ref,
             df_ref, da_ref, dx1_ref, dg4_ref, dg3_ref, df_scr, acc):
        i, k = pl.program_id(0), pl.program_id(1)

        @pl.when((i == 0) & (k == 0))
        def _():
            dg4_ref[...] = jnp.zeros_like(dg4_ref)
            dg3_ref[...] = jnp.zeros_like(dg3_ref)

        @pl.when(k == 0)
        def _():
            df, dg = _rms_bwd(dx2_ref[...], f_ref[...].astype(F32), g4_ref[...])
            dg4_ref[...] += dg
            dfb = df.astype(BF)
            df_scr[...] = dfb
            df_ref[...] = dfb
            acc[...] = jnp.zeros_like(acc)

        av = a_ref[...].astype(F32)
        da = (_dot_nt(df_scr[...], w2_ref[...]) * (2.0 * jnp.maximum(av, 0.0))).astype(BF)
        da_ref[...] = da
        acc[...] += _dot_nt(da, w1_ref[...])

        @pl.when(k == NSH - 1)
        def _():
            dx, dg = _rms_bwd(acc[...], x1_ref[...], g3_ref[...])
            dg3_ref[...] += dg
            dx1_ref[...] = dx2_ref[...] + dx

    rowblk = pl.BlockSpec((tm, D), lambda i, k: (i, 0))
    vec = pl.BlockSpec((1, D), lambda i, k: (0, 0))
    return pl.pallas_call(
        body, name="ffn_bwd", grid=(T // tm, NSH),
        in_specs=[rowblk, rowblk, vec, pl.BlockSpec((tm, D), lambda i, k: (i, k)),
                  pl.BlockSpec((D, D), lambda i, k: (k, 0)),
                  pl.BlockSpec((None, D, D), lambda i, k: (k, 0, 0)), rowblk, vec, ANY],
        out_specs=[rowblk, pl.BlockSpec((tm, D), lambda i, k: (i, k)), rowblk, vec, vec],
        out_shape=[jax.ShapeDtypeStruct((T, D), BF), jax.ShapeDtypeStruct((T, NSH * D), BF),
                   jax.ShapeDtypeStruct((T, D), F32), jax.ShapeDtypeStruct((1, D), F32),
                   jax.ShapeDtypeStruct((1, D), F32)],
        scratch_shapes=[pltpu.VMEM((tm, D), BF), pltpu.VMEM((tm, D), F32)],
        compiler_params=_cp("arbitrary", "arbitrary"),
    )(dx2, f, g4, a, w2, w1, x1, g3, dep)


def _wgrad(name, ops, grid, in_specs, out_spec, out_shape, acc_shape, pick=None, relu2=False):
    nt = grid[-1]
    na = len(ops) - 1

    def body(*refs):
        a_refs, b_ref, o_ref, acc = refs[:na], refs[na], refs[na + 1], refs[na + 2]
        t = pl.program_id(len(grid) - 1)

        @pl.when(t == 0)
        def _():
            acc[...] = jnp.zeros_like(acc)

        def add(a_ref):
            av = a_ref[...]
            if relu2:
                r = jnp.maximum(av.astype(F32), 0.0)
                av = (r * r).astype(BF)
            acc[...] += _dot_tn(av, b_ref[...])

        if na == 1:
            add(a_refs[0])
        else:
            sel = pick()
            for n in range(na):
                pl.when(sel == n)(functools.partial(add, a_refs[n]))

        @pl.when(t == nt - 1)
        def _():
            o_ref[...] = acc[...].astype(o_ref.dtype)

    return pl.pallas_call(
        body, name=name, grid=grid, in_specs=in_specs, out_specs=out_spec, out_shape=out_shape,
        scratch_shapes=[pltpu.VMEM(acc_shape, F32)],
        compiler_params=_cp(*(["arbitrary"] * len(grid))),
    )(*ops)


def _mix_out_bwd(dx1, m, gp, wo, p3, z, wb):
    T = dx1.shape[0]
    tm = min(T, 256)
    rk = D // NSH

    def body(dx1_ref, m_ref, gp_ref, wo_ref, p_ref, g_ref, wb_ref,
             dm_ref, dp_ref, dy_ref, dz_ref, dgp_ref, dmg):
        i, b = pl.program_id(0), pl.program_id(1)

        @pl.when((i == 0) & (b == 0))
        def _():
            dgp_ref[...] = jnp.zeros_like(dgp_ref)

        @pl.when(b == 0)
        def _():
            dm, dg = _rms_bwd(dx1_ref[...], m_ref[...].astype(F32), gp_ref[...])
            dgp_ref[...] += dg
            dmb = dm.astype(BF)
            dm_ref[...] = dmb
            dmg[...] = _dot_nt(dmb, wo_ref[...])

        gate = _sig(g_ref[...].astype(F32))
        d = dmg[...]
        dp = (d * gate).astype(BF)
        dp_ref[...] = dp
        dz_ref[...] = (d * p_ref[...].astype(F32) * gate * (1.0 - gate)).astype(BF)
        for k in range(NSH):
            dy_ref[:, k * rk:(k + 1) * rk] = _dot_nt(dp, wb_ref[k]).astype(BF)

    rowblk = pl.BlockSpec((tm, D), lambda i, b: (i, 0))
    br = pl.BlockSpec((None, tm, D), lambda i, b: (b, i, 0))
    vec = pl.BlockSpec((1, D), lambda i, b: (0, 0))
    return pl.pallas_call(
        body, name="mix_out_bwd", grid=(T // tm, 3),
        in_specs=[rowblk, rowblk, vec, pl.BlockSpec((D, D), lambda i, b: (0, 0)), br,
                  pl.BlockSpec((tm, D), lambda i, b: (i, 7 + b)),
                  pl.BlockSpec((NSH, None, rk, D), lambda i, b: (0, b, 0, 0))],
        out_specs=[rowblk, br, br, pl.BlockSpec((tm, D), lambda i, b: (i, 7 + b)), vec],
        out_shape=[jax.ShapeDtypeStruct((T, D), BF), jax.ShapeDtypeStruct((3, T, D), BF),
                   jax.ShapeDtypeStruct((3, T, D), BF), jax.ShapeDtypeStruct((T, 10 * D), BF),
                   jax.ShapeDtypeStruct((1, D), F32)],
        scratch_shapes=[pltpu.VMEM((tm, D), F32)],
        compiler_params=_cp("arbitrary", "arbitrary"),
    )(dx1, m, gp, wo, p3, z, wb)


def _mix_a_bwd(dz, dy3, z, wa, S):
    T = z.shape[0]
    tt = min(S, 256)
    nt = S // tt
    ntt = T // tt
    tile, cur, halo, row = _tile_specs(tt, ntt, True)

    def body(dz_in, dy_ref, ah, ab, ac, ah_h, ac_h, w_ref, dz_ref, dw_ref, ext_p, ext_d, stage):
        i, b = pl.program_id(0), pl.program_id(1)
        ti = ntt - 1 - i

        @pl.when((i == 0) & (b == 0))
        def _():
            dw_ref[...] = jnp.zeros_like(dw_ref)
            ext_d[...] = jnp.zeros_like(ext_d)

        @pl.when(b == 0)
        def _():
            first = (ti % nt) == 0
            last = (ti % nt) == nt - 1
            ahv, acv, abv = ah[...].astype(F32), ac[...].astype(F32), ab[...].astype(F32)
            ext_p[0:HALO, :] = jnp.where(first, 0.0, ah_h[...].astype(F32) * ac_h[...].astype(F32))
            ext_p[HALO:, :] = ahv * acv
            q = _causal_conv(ext_p, w_ref, KA, tt)
            dy = dy_ref[...].astype(F32)
            dq = dy * abv
            stage[1] = (dy * q).astype(BF)
            _conv_wgrad(dw_ref, dq, ext_p, KA, tt)
            ext_d[tt:, :] = jnp.where(last, 0.0, ext_d[0:HALO, :])
            ext_d[0:tt, :] = dq
            dp = _anticausal_conv(ext_d, w_ref, KA, tt)
            stage[0] = (dp * acv).astype(BF)
            stage[2] = (dp * ahv).astype(BF)

        dz_ref[...] = stage[b]

    return pl.pallas_call(
        body, name="mix_a_bwd", grid=(ntt, 3),
        in_specs=[ANY, pl.BlockSpec((None, tt, D), lambda i, b: (0, tile(i), 0)),
                  cur(0), cur(1), cur(2), halo(0), halo(2), row(KA)],
        out_specs=[pl.BlockSpec((tt, D), lambda i, b: (tile(i), b)), pl.BlockSpec((KA, D), lambda i, b: (0, 0))],
        out_shape=[jax.ShapeDtypeStruct(dz.shape, BF), jax.ShapeDtypeStruct((KA, D), F32)],
        scratch_shapes=[pltpu.VMEM((HALO + tt, D), F32), pltpu.VMEM((tt + HALO, D), F32),
                        pltpu.VMEM((3, tt, D), BF)],
        input_output_aliases={0: 0},
        compiler_params=_cp("arbitrary", "arbitrary"),
    )(dz, dy3, z, z, z, z, z, wa)


def _mix_b_bwd(dz, dy3, s, z, wc, lg, lb, S):
    T = z.shape[0]
    tt = min(S, 256)
    nt = S // tt
    ntt = T // tt
    tile, cur, halo, row = _tile_specs(tt, ntt, True)

    def body(dz_in, dy_ref, s_ref, ca, cg, ca_h, cg_h, w_ref, lg_ref, lb_ref,
             dz_ref, dw_ref, dbc_ref, dlg_ref, dlb_ref, ext_r, ext_d, stage):
        i, b = pl.program_id(0), pl.program_id(1)
        ti = ntt - 1 - i

        @pl.when((i == 0) & (b == 0))
        def _():
            dw_ref[...] = jnp.zeros_like(dw_ref)
            dbc_ref[...] = jnp.zeros_like(dbc_ref)
            dlg_ref[...] = jnp.zeros_like(dlg_ref)
            dlb_ref[...] = jnp.zeros_like(dlb_ref)
            ext_d[...] = jnp.zeros_like(ext_d)

        @pl.when(b == 0)
        def _():
            first = (ti % nt) == 0
            last = (ti % nt) == nt - 1
            n, r = _ln_stats(s_ref[...].astype(F32))
            t = n * lg_ref[...] + lb_ref[...]
            sg = _sig(t)
            dt = dy_ref[...].astype(F32) * (sg * (1.0 + t * (1.0 - sg)))
            dlg_ref[...] += jnp.sum(dt * n, axis=0, keepdims=True)
            dlb_ref[...] += jnp.sum(dt, axis=0, keepdims=True)
            ds = _ln_bwd(dt * lg_ref[...], n, r)
            dbc_ref[...] += jnp.sum(ds, axis=0, keepdims=True)
            cav = ca[...].astype(F32)
            sgc = _sig(cg[...].astype(F32))
            ext_r[0:HALO, :] = jnp.where(first, 0.0, ca_h[...].astype(F32) * _sig(cg_h[...].astype(F32)))
            ext_r[HALO:, :] = cav * sgc
            _conv_wgrad(dw_ref, ds, ext_r, KC, tt)
            ext_d[tt:, :] = jnp.where(last, 0.0, ext_d[0:HALO, :])
            ext_d[0:tt, :] = ds
            dr = _anticausal_conv(ext_d, w_ref, KC, tt)
            stage[0] = (dr * sgc).astype(BF)
            stage[1] = (dr * cav * sgc * (1.0 - sgc)).astype(BF)

        dz_ref[...] = stage[b]

    vec = pl.BlockSpec((1, D), lambda i, b: (0, 0))
    return pl.pallas_call(
        body, name="mix_b_bwd", grid=(ntt, 2),
        in_specs=[ANY, pl.BlockSpec((None, tt, D), lambda i, b: (1, tile(i), 0)),
                  pl.BlockSpec((tt, D), lambda i, b: (tile(i), 0)),
                  cur(3), cur(4), halo(3), halo(4), row(KC), row(), row()],
        out_specs=[pl.BlockSpec((tt, D), lambda i, b: (tile(i), 3 + b)),
                   pl.BlockSpec((KC, D), lambda i, b: (0, 0)), vec, vec, vec],
        out_shape=[jax.ShapeDtypeStruct(dz.shape, BF), jax.ShapeDtypeStruct((KC, D), F32)]
        + [jax.ShapeDtypeStruct((1, D), F32)] * 3,
        scratch_shapes=[pltpu.VMEM((HALO + tt, D), F32), pltpu.VMEM((tt + HALO, D), F32),
                        pltpu.VMEM((2, tt, D), BF)],
        input_output_aliases={0: 0},
        compiler_params=_cp("arbitrary", "arbitrary"),
    )(dz, dy3, s, z, z, z, z, wc, lg, lb)


def _mix_s_bwd(dz, dy3, z, lg, lb, ws, wst, bst, S):
    T = z.shape[0]
    tt = min(S, 256)
    ntt = T // tt
    _, cur, _, row = _tile_specs(tt, ntt, False)

    def body(dz_in, dy_ref, su, sv, lg_ref, lb_ref, ws_ref, wst_ref, bst_ref,
             dz_ref, dws_ref, dbst_ref, dlg_ref, dlb_ref, u_scr, vn_scr, dvn_scr, stage):
        i, b = pl.program_id(0), pl.program_id(1)

        @pl.when((i == 0) & (b == 0))
        def _():
            dws_ref[...] = jnp.zeros_like(dws_ref)
            dbst_ref[...] = jnp.zeros_like(dbst_ref)
            dlg_ref[...] = jnp.zeros_like(dlg_ref)
            dlb_ref[...] = jnp.zeros_like(dlb_ref)

        @pl.when(b == 0)
        def _():
            u, du_dx = _gelu(su[...].astype(F32))
            v, dv_dx = _gelu(sv[...].astype(F32))
            u_scr[...] = u
            n, r = _ln_stats(v)
            vn_scr[...] = (n * lg_ref[...] + lb_ref[...]).astype(BF)
            mask = _causal_mask(False)
            mask_t = _causal_mask(True)
            for h in range(HEADS):
                wm = jnp.where(mask, ws_ref[h], 0.0).astype(BF)
                wmt = jnp.where(mask_t, wst_ref[h], 0.0).astype(BF)
                cols = slice(h * CHUNK, (h + 1) * CHUNK)
                for c in range(tt // CHUNK):
                    rows = slice(c * CHUNK, (c + 1) * CHUNK)
                    vb = vn_scr[rows, cols]
                    mixed = _dot(wm, vb) + bst_ref[:, h:h + 1]
                    dy = dy_ref[rows, cols].astype(F32)
                    dmix = dy * u_scr[rows, cols]
                    u_scr[rows, cols] = dy * mixed
                    dbst_ref[:, h:h + 1] += jnp.sum(dmix, axis=1, keepdims=True)
                    dmb = dmix.astype(BF)
                    dws_ref[h] += _dot_nt(dmb, vb)
                    dvn_scr[rows, cols] = _dot(wmt, dmb)
            stage[0] = (u_scr[...] * du_dx).astype(BF)
            dvn = dvn_scr[...]
            dlg_ref[...] += jnp.sum(dvn * n, axis=0, keepdims=True)
            dlb_ref[...] += jnp.sum(dvn, axis=0, keepdims=True)
            stage[1] = (_ln_bwd(dvn * lg_ref[...], n, r) * dv_dx).astype(BF)

        dz_ref[...] = stage[b]

    vec = pl.BlockSpec((1, D), lambda i, b: (0, 0))
    wsp = pl.BlockSpec((HEADS, CHUNK, CHUNK), lambda i, b: (0, 0, 0))
    bsp = pl.BlockSpec((CHUNK, HEADS), lambda i, b: (0, 0))
    return pl.pallas_call(
        body, name="mix_s_bwd", grid=(ntt, 2),
        in_specs=[ANY, pl.BlockSpec((None, tt, D), lambda i, b: (2, i, 0)),
                  cur(5), cur(6), row(), row(), wsp, wsp, bsp],
        out_specs=[pl.BlockSpec((tt, D), lambda i, b: (i, 5 + b)), wsp, bsp, vec, vec],
        out_shape=[jax.ShapeDtypeStruct(dz.shape, BF), jax.ShapeDtypeStruct((HEADS, CHUNK, CHUNK), F32),
                   jax.ShapeDtypeStruct((CHUNK, HEADS), F32), jax.ShapeDtypeStruct((1, D), F32),
                   jax.ShapeDtypeStruct((1, D), F32)],
        scratch_shapes=[pltpu.VMEM((tt, D), F32), pltpu.VMEM((tt, D), BF), pltpu.VMEM((tt, D), F32),
                        pltpu.VMEM((2, tt, D), BF)],
        input_output_aliases={0: 0},
        compiler_params=_cp("arbitrary", "arbitrary"),
    )(dz, dy3, z, z, lg, lb, ws, wst, bst)


def _in_proj_bwd(dz, w, x, g, dx1):
    T = x.shape[0]
    nc = w.shape[2]
    tm = min(T, 512)
    tn = 1280
    nj = nc // tn

    def body(dz_ref, w_ref, x_ref, g_ref, dx1_ref, dx_ref, dg_ref, acc):
        i, k, j = pl.program_id(0), pl.program_id(1), pl.program_id(2)

        @pl.when((i == 0) & (k == 0) & (j == 0))
        def _():
            dg_ref[...] = jnp.zeros_like(dg_ref)

        @pl.when((k == 0) & (j == 0))
        def _():
            acc[...] = jnp.zeros_like(acc)

        acc[...] += _dot_nt(dz_ref[...], w_ref[...])

        @pl.when((k == NSH - 1) & (j == nj - 1))
        def _():
            dx, dg = _rms_bwd(acc[...], x_ref[...], g_ref[...])
            dg_ref[...] += dg
            dx_ref[...] = dx1_ref[...] + dx

    rowblk = pl.BlockSpec((tm, D), lambda i, k, j: (i, 0))
    vec = pl.BlockSpec((1, D), lambda i, k, j: (0, 0))
    return pl.pallas_call(
        body, name="in_proj_bwd", grid=(T // tm, NSH, nj),
        in_specs=[pl.BlockSpec((tm, tn), lambda i, k, j: (i, k * nj + j)),
                  pl.BlockSpec((None, D, tn), lambda i, k, j: (k, 0, j)), rowblk, vec, rowblk],
        out_specs=[rowblk, vec],
        out_shape=[jax.ShapeDtypeStruct((T, D), F32), jax.ShapeDtypeStruct((1, D), F32)],
        scratch_shapes=[pltpu.VMEM((tm, D), F32)],
        compiler_params=_cp("arbitrary", "arbitrary", "arbitrary"),
    )(dz, w, x, g, dx1)


def _layer_fwd(x, p, S, dep):
    h, z = _in_proj(x, p["g_mix_pre"], p["w_in"], dep)
    ya = _mix_a_fwd(z, p["conv_a_w"], S)
    yc, s = _mix_b_fwd(z, p["conf_dw_w"], p["conf_dw_b"], p["conf_ln_g"], p["conf_ln_b"], S)
    ys = _mix_s_fwd(z, p["sgu_ln_g"], p["sgu_ln_b"], p["sgu_ws"], p["sgu_bt"], S)
    p3, merged, m, x1 = _mix_out_fwd(ya, yc, ys, z, x, p["w_branch"], p["w_out"], p["g_mix_post"])
    h2, a, f, x2 = _ffn_fwd(x1, p["g_ffn_pre"], p["w_ff1"], p["w_ff2"], p["g_ffn_post"])
    saved = dict(x=x, h=h, z=z, ya=ya, yc=yc, ys=ys, s=s, p3=p3, merged=merged, m=m, x1=x1, h2=h2, a=a, f=f)
    return x2, saved


def _layer_bwd(dx2, p, sv, S, dep):
    T = dx2.shape[0]
    bt = min(T, 512)
    nt = T // bt
    rk = D // NSH
    df, da, dx1, dg_ffn_post, dg_ffn_pre = _ffn_bwd(dx2, sv["f"], p["g_ffn_post"], sv["a"], p["w_ff2"],
                                                    p["w_ff1"], sv["x1"], p["g_ffn_pre"], dep)
    dw_ff2 = _wgrad("wgrad_ff2", (sv["a"], df), (NSH, nt),
                    [pl.BlockSpec((bt, D), lambda k, t: (t, k)), pl.BlockSpec((bt, D), lambda k, t: (t, 0))],
                    pl.BlockSpec((None, D, D), lambda k, t: (k, 0, 0)),
                    jax.ShapeDtypeStruct((NSH, D, D), BF), (D, D), relu2=True)
    dw_ff1 = _wgrad("wgrad_ff1", (sv["h2"], da), (NSH, nt),
                    [pl.BlockSpec((bt, D), lambda k, t: (t, 0)), pl.BlockSpec((bt, D), lambda k, t: (t, k))],
                    pl.BlockSpec((None, D, D), lambda k, t: (k, 0, 0)),
                    jax.ShapeDtypeStruct((NSH, D, D), BF), (D, D))
    dm, dp3, dy3, dz, dg_mix_post = _mix_out_bwd(dx1, sv["m"], p["g_mix_post"], p["w_out"], sv["p3"], sv["z"],
                                                 p["w_branch"])
    dw_out = _wgrad("wgrad_out", (sv["merged"], dm), (nt,),
                    [pl.BlockSpec((bt, D), lambda t: (t, 0)), pl.BlockSpec((bt, D), lambda t: (t, 0))],
                    pl.BlockSpec((D, D), lambda t: (0, 0)),
                    jax.ShapeDtypeStruct((D, D), BF), (D, D)).reshape(NSH, rk, D)
    ysp = pl.BlockSpec((bt, rk), lambda b, k, t: (t, k))
    dw_br = _wgrad("wgrad_branch", (sv["ya"], sv["yc"], sv["ys"], dp3), (3, NSH, nt),
                   [ysp, ysp, ysp, pl.BlockSpec((None, bt, D), lambda b, k, t: (b, t, 0))],
                   pl.BlockSpec((None, None, rk, D), lambda b, k, t: (k, b, 0, 0)),
                   jax.ShapeDtypeStruct((NSH, 3, rk, D), BF), (rk, D), pick=lambda: pl.program_id(0))
    dz, dwa = _mix_a_bwd(dz, dy3, sv["z"], p["conv_a_w"], S)
    dz, dwc, dbc, dclg, dclb = _mix_b_bwd(dz, dy3, sv["s"], sv["z"], p["conf_dw_w"], p["conf_ln_g"],
                                          p["conf_ln_b"], S)
    dz, dws, dbst, dslg, dslb = _mix_s_bwd(dz, dy3, sv["z"], p["sgu_ln_g"], p["sgu_ln_b"], p["sgu_ws"],
                                           p["sgu_wst"], p["sgu_bt"], S)
    dx, dg_mix_pre = _in_proj_bwd(dz, p["w_in"], sv["x"], p["g_mix_pre"], dx1)
    tn = 1280
    nj = p["w_in"].shape[2] // tn
    dw_in = _wgrad("wgrad_in", (sv["h"], dz), (NSH, nj, nt),
                   [pl.BlockSpec((bt, D), lambda k, j, t: (t, 0)),
                    pl.BlockSpec((bt, tn), lambda k, j, t: (t, k * nj + j))],
                   pl.BlockSpec((None, D, tn), lambda k, j, t: (k, 0, j)),
                   jax.ShapeDtypeStruct(p["w_in"].shape, BF), (D, tn))
    tril = jnp.tril(jnp.ones((CHUNK, CHUNK), bool))
    small = dict(norm_mix_pre=dg_mix_pre, norm_mix_post=dg_mix_post, norm_ffn_pre=dg_ffn_pre,
                 norm_ffn_post=dg_ffn_post, conv_a_w=dwa, conf_dw_w=dwc, conf_dw_b=dbc, conf_ln_g=dclg,
                 conf_ln_b=dclb, sgu_ln_g=dslg, sgu_ln_b=dslb,
                 sgu_ws=jnp.where(tril[None], dws, 0.0), sgu_b=dbst.T)
    big = dict(w_in=dw_in, w_branch=dw_br, w_out=dw_out, w_ff1=dw_ff1, w_ff2=dw_ff2)
    return dx, big, small


SMALL_NAMES = ("norm_mix_pre", "norm_mix_post", "norm_ffn_pre", "norm_ffn_post", "conv_a_w", "conf_dw_w",
               "conf_dw_b", "conf_ln_g", "conf_ln_b", "sgu_ln_g", "sgu_ln_b", "sgu_b", "sgu_ws")
SMALL_ROWS = dict(norm_mix_pre=1, norm_mix_post=1, norm_ffn_pre=1, norm_ffn_post=1, conv_a_w=KA, conf_dw_w=KC,
                  conf_dw_b=1, conf_ln_g=1, conf_ln_b=1, sgu_ln_g=1, sgu_ln_b=1, sgu_b=1, sgu_ws=CHUNK)
SUBLANES = 8


def _pad8(r):
    return -(-r // SUBLANES) * SUBLANES


PACK_ROWS = sum(_pad8(r) for r in SMALL_ROWS.values())


def _pack_small(d):
    parts = []
    for n in SMALL_NAMES:
        r = SMALL_ROWS[n]
        parts.append(jnp.pad(d[n].reshape(r, D).astype(F32), ((0, _pad8(r) - r), (0, 0))))
    return jnp.concatenate(parts, axis=0)


def _unpack_small(a, shapes):
    out, r = {}, 0
    for n in SMALL_NAMES:
        out[n] = a[:, r:r + SMALL_ROWS[n]].reshape((a.shape[0],) + tuple(shapes[n]))
        r += _pad8(SMALL_ROWS[n])
    return out


def _me():
    return lax.axis_index("x"), lax.axis_index("y"), lax.axis_index("c")


def _slab(ref, q, a, h=None):
    r = ref.shape[1]
    rows = slice(None) if h is None else pl.ds(h * (r // 2), r // 2)
    return ref.at[pl.ds(q * a, a), rows, :]


def _rows(ref, h):
    r = ref.shape[-2]
    lead = (slice(None),) * (len(ref.shape) - 2)
    return ref.at[lead + (pl.ds(h * (r // 2), r // 2), slice(None))]


def _rcopy(src, dst, sems, idx, dev):
    return pltpu.make_async_remote_copy(src_ref=src, dst_ref=dst, send_sem=sems[0].at[idx], recv_sem=sems[1].at[idx],
                                        device_id=dev, device_id_type=MESH)


def _gather_weights(srcs, layer):
    n = len(srcs)

    def body(*refs):
        src, dst = refs[:n], refs[n:2 * n]
        sems = refs[2 * n:2 * n + 2]
        lsem = refs[2 * n + 2]
        x, y, c = _me()
        k = 2 * x + y
        chips = [(1 - x, y), (x, 1 - y), (1 - x, 1 - y)]
        av = [s.shape[1] for s in src]
        own = [pltpu.make_async_copy(src[i].at[layer], _slab(dst[i], k, av[i]), lsem.at[i]) for i in range(n)]
        for cp in own:
            cp.start()
        first = []
        for j, (qx, qy) in enumerate(chips):
            for i in range(n):
                first.append(_rcopy(_rows(src[i].at[layer], c), _slab(dst[i], k, av[i], c), sems, j * n + i,
                                    (qx, qy, c)))
        for cp in first:
            cp.start()
        passed = []
        for j, (qx, qy) in enumerate(chips):
            kq = 2 * qx + qy
            for i in range(n):
                got = _slab(dst[i], kq, av[i], c)
                _rcopy(got, got, sems, j * n + i, (x, y, c)).wait_recv()
                cp = _rcopy(got, got, sems, 3 * n + j * n + i, (x, y, 1 - c))
                cp.start()
                passed.append(cp)
        for j, (qx, qy) in enumerate(chips):
            kq = 2 * qx + qy
            for i in range(n):
                other = _slab(dst[i], kq, av[i], 1 - c)
                _rcopy(other, other, sems, 3 * n + j * n + i, (x, y, c)).wait_recv()
        for cp in first + passed:
            cp.wait_send()
        for cp in own:
            cp.wait()

    outs = [jax.ShapeDtypeStruct((NSH * s.shape[1],) + s.shape[2:], s.dtype) for s in srcs]
    return pl.pallas_call(
        body, name="gather_weights", in_specs=[ANY] * n, out_specs=[ANY] * n, out_shape=outs,
        scratch_shapes=[pltpu.SemaphoreType.DMA((6 * n,)), pltpu.SemaphoreType.DMA((6 * n,)),
                        pltpu.SemaphoreType.DMA((n,))],
    )(*srcs)


def _send_halves_to_sibling(parts):
    n = len(parts)

    def body(*refs):
        src, dst = refs[:n], refs[n:2 * n]
        sems = refs[2 * n:2 * n + 2]
        x, y, c = _me()
        cps = [_rcopy(_rows(src[i], 1 - c), dst[i], sems, i, (x, y, 1 - c)) for i in range(n)]
        for cp in cps:
            cp.start()
        for cp in cps:
            cp.wait()

    outs = [jax.ShapeDtypeStruct((p.shape[0], p.shape[1] // 2, p.shape[2]), p.dtype) for p in parts]
    return pl.pallas_call(
        body, name="pair_exchange", in_specs=[ANY] * n, out_specs=[ANY] * n, out_shape=outs,
        scratch_shapes=[pltpu.SemaphoreType.DMA((n,)), pltpu.SemaphoreType.DMA((n,))],
    )(*parts)


def _pair_add(part, sib, c):
    A, R, C = part.shape
    hr = R // 2
    br = min(hr, 512)
    nb = hr // br

    def body(c_ref, p_ref, s_ref, o_ref):
        o_ref[...] = (p_ref[...].astype(F32) + s_ref[...].astype(F32)).astype(BF)

    return pl.pallas_call(
        body, name="pair_add",
        grid_spec=pltpu.PrefetchScalarGridSpec(
            num_scalar_prefetch=1, grid=(A, nb),
            in_specs=[pl.BlockSpec((None, br, C), lambda a, i, c_ref: (a, c_ref[0] * nb + i, 0)),
                      pl.BlockSpec((None, br, C), lambda a, i, c_ref: (a, i, 0))],
            out_specs=pl.BlockSpec((None, br, C), lambda a, i, c_ref: (a, i, 0))),
        out_shape=jax.ShapeDtypeStruct((A, hr, C), BF),
        compiler_params=_cp("arbitrary", "arbitrary"),
    )(c, part, sib)


def _other_chips(x, y):
    return [(1 - x, y), (x, 1 - y), (1 - x, 1 - y)]


def _split_call(name, copies, srcs, lands, sems=None, after=None):
    n, m = len(srcs), len(lands)
    hbm = lambda t: pltpu.HBM(t.shape, t.dtype)
    pin = lambda t: pltpu.with_memory_space_constraint(t, pltpu.HBM)
    thru = [hbm(t) for t in srcs] + [hbm(t) for t in lands]
    sem_spec = pl.BlockSpec(memory_space=pltpu.SEMAPHORE)
    effect = pltpu.CompilerParams(has_side_effects=pltpu.SideEffectType.DATAFLOW_SIDE_EFFECTING)
    if sems is None:
        def start_body(*refs):
            src, land = refs[:n], refs[n:n + m]
            ssem, rsem = refs[n + m], refs[n + m + 1]
            token = refs[-1]
            cps = copies(src, land, (ssem, rsem))
            for cp in cps:
                cp.start()
            token[...] = jnp.zeros_like(token)

        ncp = copies.count
        out = pl.pallas_call(
            start_body, name=name,
            out_shape=(pltpu.SemaphoreType.DMA((ncp,)), pltpu.SemaphoreType.DMA((ncp,)), *thru,
                       jax.ShapeDtypeStruct((8, 128), F32)),
            in_specs=[ANY] * (n + m),
            out_specs=(sem_spec, sem_spec, *([ANY] * (n + m)), pl.BlockSpec(memory_space=pltpu.VMEM)),
            input_output_aliases={i: 2 + i for i in range(n + m)},
            compiler_params=effect,
        )(*[pin(t) for t in srcs], *[pin(t) for t in lands])
        return out[0], out[1], list(out[2:2 + n]), list(out[2 + n:2 + n + m]), out[-1]

    def wait_body(*refs):
        src, land = refs[:n], refs[n:n + m]
        ssem, rsem = refs[n + m], refs[n + m + 1]
        for cp in copies(src, land, (ssem, rsem)):
            cp.wait_send()
            cp.wait_recv()

    out = pl.pallas_call(
        wait_body, name=name, out_shape=tuple(thru),
        in_specs=[ANY] * (n + m) + [sem_spec, sem_spec, ANY],
        out_specs=tuple([ANY] * (n + m)),
        input_output_aliases={i: i for i in range(n + m)},
        compiler_params=effect,
    )(*srcs, *lands, sems[0], sems[1], after)
    return list(out[:n]), list(out[n:])


def _cast_into(w, land, layer, kidx, dep):
    _, a, R, C = w.shape
    br = R
    while br * C > 256 * 1024 and br % 32 == 0:
        br //= 2

    def body(k_ref, w_ref, land_ref, dep_ref, o_ref):
        o_ref[...] = w_ref[...].astype(BF)

    return pl.pallas_call(
        body, name="cast_into",
        grid_spec=pltpu.PrefetchScalarGridSpec(
            num_scalar_prefetch=1, grid=(a, R // br),
            in_specs=[pl.BlockSpec((None, None, br, C), lambda e, i, k: (layer, e, i, 0)), ANY, ANY],
            out_specs=pl.BlockSpec((None, br, C), lambda e, i, k: (k[0] * a + e, i, 0))),
        out_shape=jax.ShapeDtypeStruct(land.shape, BF), input_output_aliases={2: 0},
        compiler_params=_cp("arbitrary", "arbitrary"),
    )(kidx, w, land, dep)


class _GatherCopies:
    def __init__(self, n):
        self.n, self.count = n, 3 * n

    def __call__(self, src, land, sems):
        x, y, c = _me()
        k = 2 * x + y
        cps = []
        for j, (qx, qy) in enumerate(_other_chips(x, y)):
            for i in range(self.n):
                mine = _slab(land[i], k, land[i].shape[0] // NSH, c)
                cps.append(_rcopy(mine, mine, sems, j * self.n + i, (qx, qy, c)))
        return cps


def _gather_finish(lands):
    n = len(lands)

    def body(*refs):
        dst = refs[n:2 * n]
        sems = refs[2 * n:2 * n + 2]
        x, y, c = _me()
        av = [d.shape[0] // NSH for d in dst]
        cps = []
        for j, (qx, qy) in enumerate(_other_chips(x, y)):
            for i in range(n):
                got = _slab(dst[i], 2 * qx + qy, av[i], c)
                cps.append(_rcopy(got, got, sems, j * n + i, (x, y, 1 - c)))
        for cp in cps:
            cp.start()
        for j, (qx, qy) in enumerate(_other_chips(x, y)):
            for i in range(n):
                other = _slab(dst[i], 2 * qx + qy, av[i], 1 - c)
                _rcopy(other, other, sems, j * n + i, (x, y, c)).wait_recv()
        for cp in cps:
            cp.wait_send()

    return pl.pallas_call(
        body, name="gather_finish", in_specs=[ANY] * n, out_specs=[ANY] * n,
        out_shape=[jax.ShapeDtypeStruct(t.shape, t.dtype) for t in lands],
        input_output_aliases={i: i for i in range(n)},
        scratch_shapes=[pltpu.SemaphoreType.DMA((3 * n,)), pltpu.SemaphoreType.DMA((3 * n,))],
    )(*lands)


class _ScatterCopies:
    def __init__(self, n):
        self.n, self.count = n, 3 * n

    def __call__(self, src, land, sems):
        x, y, c = _me()
        k = 2 * x + y
        cps = []
        for j, (qx, qy) in enumerate(_other_chips(x, y)):
            for i in range(self.n):
                a = src[i].shape[0] // NSH
                cps.append(_rcopy(_slab(src[i], 2 * qx + qy, a), _slab(land[i], k, a), sems, j * self.n + i,
                                  (qx, qy, c)))
        return cps


def _sum_chips(own, rcv, acc, layer, nlayers, idx):
    A, hr, C = rcv.shape
    a = A // NSH
    br = min(hr, 512)
    nb = hr // br

    def body(*refs):
        r0, r1, r2, r3 = refs[1:5]
        o_ref = refs[-1]
        o_ref[...] = ((r0[...].astype(F32) + r1[...].astype(F32)) + r2[...].astype(F32)) + r3[...].astype(F32)

    slot = lambda s: pl.BlockSpec((None, br, C), lambda e, i, ix: (ix[s] * a + e, i, 0))
    ops = [own, rcv, rcv, rcv]
    in_specs = [slot(0), slot(1), slot(2), slot(3)]
    aliases = {}
    if acc is not None:
        ops.append(acc)
        in_specs.append(ANY)
        aliases = {5: 0}
    return pl.pallas_call(
        body, name="sum_chips",
        grid_spec=pltpu.PrefetchScalarGridSpec(
            num_scalar_prefetch=1, grid=(a, nb), in_specs=in_specs,
            out_specs=pl.BlockSpec((None, None, br, C), lambda e, i, ix: (layer, e, ix[4] * nb + i, 0))),
        out_shape=jax.ShapeDtypeStruct((nlayers, a, 2 * hr, C), F32), input_output_aliases=aliases,
        compiler_params=_cp("arbitrary", "arbitrary"),
    )(idx, *ops)


def _join_halves(fulls):
    n = len(fulls)

    def body(*refs):
        buf = refs[n:2 * n]
        sems = refs[2 * n:2 * n + 2]
        x, y, c = _me()
        cps = [_rcopy(_rows(buf[i], c), _rows(buf[i], c), sems, i, (x, y, 1 - c)) for i in range(n)]
        for cp in cps:
            cp.start()
        for i in range(n):
            _rcopy(_rows(buf[i], 1 - c), _rows(buf[i], 1 - c), sems, i, (x, y, c)).wait_recv()
        for cp in cps:
            cp.wait_send()

    return pl.pallas_call(
        body, name="join_halves", in_specs=[ANY] * n, out_specs=[ANY] * n,
        out_shape=[jax.ShapeDtypeStruct(t.shape, t.dtype) for t in fulls],
        input_output_aliases={i: i for i in range(n)},
        scratch_shapes=[pltpu.SemaphoreType.DMA((n,)), pltpu.SemaphoreType.DMA((n,))],
    )(*fulls)


def _gather_all(block):
    R, C = block.shape

    def body(src, dst, ssem, rsem, lsem):
        sems = (ssem, rsem)
        x, y, c = _me()
        chips = [(1 - x, y), (x, 1 - y), (1 - x, 1 - y)]

        def at(px, py, pc):
            return dst.at[4 * px + 2 * py + pc]

        own = pltpu.make_async_copy(src, at(x, y, c), lsem)
        own.start()
        first = [_rcopy(src, at(x, y, c), sems, 0, (x, y, 1 - c))]
        first += [_rcopy(src, at(x, y, c), sems, 1 + j, (qx, qy, c)) for j, (qx, qy) in enumerate(chips)]
        for cp in first:
            cp.start()
        passed = []
        for j, (qx, qy) in enumerate(chips):
            got = at(qx, qy, c)
            _rcopy(got, got, sems, 1 + j, (x, y, c)).wait_recv()
            cp = _rcopy(got, got, sems, 4 + j, (x, y, 1 - c))
            cp.start()
            passed.append(cp)
        sib = at(x, y, 1 - c)
        _rcopy(sib, sib, sems, 0, (x, y, c)).wait_recv()
        for j, (qx, qy) in enumerate(chips):
            other = at(qx, qy, 1 - c)
            _rcopy(other, other, sems, 4 + j, (x, y, c)).wait_recv()
        for cp in first + passed:
            cp.wait_send()
        own.wait()

    return pl.pallas_call(
        body, name="gather_all", in_specs=[ANY], out_specs=ANY,
        out_shape=jax.ShapeDtypeStruct((NDEV, R, C), block.dtype),
        scratch_shapes=[pltpu.SemaphoreType.DMA((7,)), pltpu.SemaphoreType.DMA((7,)), pltpu.SemaphoreType.DMA],
    )(block)


def _sum_devices(g):
    _, R, C = g.shape
    br = 264 if R % 264 == 0 else R

    def body(g_ref, o_ref):
        acc = g_ref[0]
        for d in range(1, NDEV):
            acc = acc + g_ref[d]
        o_ref[...] = acc

    return pl.pallas_call(
        body, name="sum_devices", grid=(R // br,),
        in_specs=[pl.BlockSpec((NDEV, br, C), lambda i: (0, i, 0))],
        out_specs=pl.BlockSpec((br, C), lambda i: (i, 0)),
        out_shape=jax.ShapeDtypeStruct((R, C), F32),
        compiler_params=_cp("arbitrary"),
    )(g)


def _adamw(w, g, m, v):
    shape = w.shape
    C = shape[-1]
    R = shape[-2]
    A = 1
    for s in shape[:-2]:
        A *= s
    br = R
    while br * C > 256 * 1024 and br % 16 == 0:
        br //= 2
    c1 = 1.0 / (1.0 - ADAM_B1 ** ADAM_STEP)
    c2 = 1.0 / (1.0 - ADAM_B2 ** ADAM_STEP)

    def body(w_ref, g_ref, m_ref, v_ref, d_ref, nm_ref, nv_ref):
        gv = g_ref[...]
        nm = ADAM_B1 * m_ref[...] + (1.0 - ADAM_B1) * gv
        nv = ADAM_B2 * v_ref[...] + (1.0 - ADAM_B2) * (gv * gv)
        nm_ref[...] = nm
        nv_ref[...] = nv
        d_ref[...] = -ADAM_LR * ((nm * c1) / (jnp.sqrt(nv * c2) + ADAM_EPS) + ADAM_WD * w_ref[...])

    blk = pl.BlockSpec((None, br, C), lambda a, i: (a, i, 0))
    outs = pl.pallas_call(
        body, name="adamw", grid=(A, R // br), in_specs=[blk] * 4, out_specs=[blk] * 3,
        out_shape=[jax.ShapeDtypeStruct((A, R, C), F32)] * 3,
        compiler_params=_cp("arbitrary", "arbitrary"),
    )(*(t.reshape(A, R, C) for t in (w, g, m, v)))
    return tuple(o.reshape(shape) for o in outs)


WEIGHTS = ("norm_mix_pre", "norm_mix_post", "norm_ffn_pre", "norm_ffn_post", "w_in", "conv_a_w", "conf_dw_w",
           "conf_dw_b", "conf_ln_g", "conf_ln_b", "sgu_ln_g", "sgu_ln_b", "sgu_ws", "sgu_b", "w_branch", "w_out",
           "w_ff1", "w_ff2")
BIG = ("w_in", "w_branch", "w_out", "w_ff1", "w_ff2")
CONV_ROWS = 48


def kernel(x, norm_mix_pre, norm_mix_post, norm_ffn_pre, norm_ffn_post, w_in, conv_a_w, conf_dw_w, conf_dw_b, conf_ln_g, conf_ln_b, sgu_ln_g, sgu_ln_b, sgu_ws, sgu_b, w_branch, w_out, w_ff1, w_ff2, loss_target, m_norm_mix_pre, m_norm_mix_post, m_norm_ffn_pre, m_norm_ffn_post, m_w_in, m_conv_a_w, m_conf_dw_w, m_conf_dw_b, m_conf_ln_g, m_conf_ln_b, m_sgu_ln_g, m_sgu_ln_b, m_sgu_ws, m_sgu_b, m_w_branch, m_w_out, m_w_ff1, m_w_ff2, v_norm_mix_pre, v_norm_mix_post, v_norm_ffn_pre, v_norm_ffn_post, v_w_in, v_conv_a_w, v_conf_dw_w, v_conf_dw_b, v_conf_ln_g, v_conf_ln_b, v_sgu_ln_g, v_sgu_ln_b, v_sgu_ws, v_sgu_b, v_w_branch, v_w_out, v_w_ff1, v_w_ff2):
    w = dict(norm_mix_pre=norm_mix_pre, norm_mix_post=norm_mix_post, norm_ffn_pre=norm_ffn_pre,
             norm_ffn_post=norm_ffn_post, w_in=w_in, conv_a_w=conv_a_w, conf_dw_w=conf_dw_w, conf_dw_b=conf_dw_b,
             conf_ln_g=conf_ln_g, conf_ln_b=conf_ln_b, sgu_ln_g=sgu_ln_g, sgu_ln_b=sgu_ln_b, sgu_ws=sgu_ws,
             sgu_b=sgu_b, w_branch=w_branch, w_out=w_out, w_ff1=w_ff1, w_ff2=w_ff2)
    mom = dict(norm_mix_pre=m_norm_mix_pre, norm_mix_post=m_norm_mix_post, norm_ffn_pre=m_norm_ffn_pre,
               norm_ffn_post=m_norm_ffn_post, w_in=m_w_in, conv_a_w=m_conv_a_w, conf_dw_w=m_conf_dw_w,
               conf_dw_b=m_conf_dw_b, conf_ln_g=m_conf_ln_g, conf_ln_b=m_conf_ln_b, sgu_ln_g=m_sgu_ln_g,
               sgu_ln_b=m_sgu_ln_b, sgu_ws=m_sgu_ws, sgu_b=m_sgu_b, w_branch=m_w_branch, w_out=m_w_out,
               w_ff1=m_w_ff1, w_ff2=m_w_ff2)
    var = dict(norm_mix_pre=v_norm_mix_pre, norm_mix_post=v_norm_mix_post, norm_ffn_pre=v_norm_ffn_pre,
               norm_ffn_post=v_norm_ffn_post, w_in=v_w_in, conv_a_w=v_conv_a_w, conf_dw_w=v_conf_dw_w,
               conf_dw_b=v_conf_dw_b, conf_ln_g=v_conf_ln_g, conf_ln_b=v_conf_ln_b, sgu_ln_g=v_sgu_ln_g,
               sgu_ln_b=v_sgu_ln_b, sgu_ws=v_sgu_ws, sgu_b=v_sgu_b, w_branch=v_w_branch, w_out=v_w_out,
               w_ff1=v_w_ff1, w_ff2=v_w_ff2)
    L = w_in.shape[0]
    nseq, S, _ = x.shape
    T = nseq * S
    rk = D // NSH
    mx, my, mc = _me()
    k_chip = 2 * mx + my

    big_src = [w_in.reshape(L, 1, D, w_in.shape[2]), w_branch, w_out.reshape(L, 1, rk, D),
               w_ff1.reshape(L, 1, D, w_ff1.shape[2]), w_ff2.reshape(L, 1, w_ff2.shape[1], D)]
    kidx = jnp.reshape(k_chip, (1,)).astype(jnp.int32)
    conv_src = jnp.concatenate(
        [jnp.pad(conv_a_w, ((0, 0), (0, SUBLANES - KA), (0, 0))), jnp.pad(conf_dw_w, ((0, 0), (0, 1), (0, 0))),
         jnp.zeros((L, CONV_ROWS - SUBLANES - KC - 1, rk), F32)], axis=1)[None]
    (conv_g,) = _gather_weights([conv_src], 0)
    conv_full = conv_g.reshape(NSH, L, CONV_ROWS, rk).transpose(1, 2, 0, 3).reshape(L, CONV_ROWS, D)

    def layer_params(l, gathered):
        g_in, g_br, g_out, g_ff1, g_ff2 = gathered
        return dict(
            g_mix_pre=norm_mix_pre[l][None], g_mix_post=norm_mix_post[l][None], g_ffn_pre=norm_ffn_pre[l][None],
            g_ffn_post=norm_ffn_post[l][None], w_in=g_in, conv_a_w=conv_full[l, :KA],
            conf_dw_w=conv_full[l, SUBLANES:SUBLANES + KC], conf_dw_b=conf_dw_b[l][None],
            conf_ln_g=conf_ln_g[l][None], conf_ln_b=conf_ln_b[l][None], sgu_ln_g=sgu_ln_g[l][None],
            sgu_ln_b=sgu_ln_b[l][None], sgu_ws=sgu_ws[l], sgu_wst=jnp.swapaxes(sgu_ws[l], 1, 2),
            sgu_bt=sgu_b[l].T, w_branch=g_br.reshape(NSH, 3, rk, D), w_out=g_out.reshape(D, D), w_ff1=g_ff1,
            w_ff2=g_ff2.reshape(NSH * w_ff2.shape[1], D))

    def gather_start(l, dep):
        lands = [_cast_into(s, lax.empty((NSH * s.shape[1],) + s.shape[2:], BF), l, kidx, dep) for s in big_src]
        return _split_call(f"gather_start_{l}", _GatherCopies(len(lands)), [], lands)

    xt = x.reshape(T, D)
    layers, saved = [], []
    flight = gather_start(0, kidx)
    after = xt
    for l in range(L):
        ssem, rsem, _, lands, _ = flight
        _, lands = _split_call(f"gather_wait_{l}", _GatherCopies(len(lands)), [], lands, (ssem, rsem), after)
        gathered = _gather_finish(lands)
        if l + 1 < L:
            flight = gather_start(l + 1, gathered[0])
            dep = flight[4]
        else:
            dep = jnp.zeros((8, 128), F32)
        p = layer_params(l, gathered)
        xt, sv = _layer_fwd(xt, p, S, dep)
        layers.append(p)
        saved.append(sv)
        after = xt
    dx, loss_row = _loss_head(xt, loss_target.reshape(T, D))
    loss = lax.psum(loss_row[0, 0], ("x", "y", "c"))

    c_arr = jnp.reshape(mc, (1,)).astype(jnp.int32)
    idx = jnp.stack([k_chip, k_chip ^ 2, k_chip ^ 1, k_chip ^ 3, mc]).astype(jnp.int32)
    fulls = [None] * len(BIG)
    smalls = [None] * L
    flight = None
    dep = jnp.zeros((8, 128), F32)

    def land_scatter(l, fl, after, fulls):
        ssem, rsem, sums, rcv, _ = fl
        sums, rcv = _split_call(f"scatter_wait_{l}", _ScatterCopies(len(sums)), sums, rcv, (ssem, rsem), after)
        return [_sum_chips(o, r, f, l, L, idx) for o, r, f in zip(sums, rcv, fulls)]

    for l in reversed(range(L)):
        dx, big, small = _layer_bwd(dx, layers[l], saved[l], S, dep)
        smalls[l] = _pack_small(small)
        if flight is not None:
            fulls = land_scatter(l + 1, flight, dx, fulls)
        parts = [big["w_in"], big["w_branch"].reshape(NSH * 3, rk, D), big["w_out"], big["w_ff1"], big["w_ff2"]]
        sib = _send_halves_to_sibling(parts)
        sums = [_pair_add(p, s, c_arr) for p, s in zip(parts, sib)]
        rcv = [lax.empty(s.shape, s.dtype) for s in sums]
        flight = _split_call(f"scatter_start_{l}", _ScatterCopies(len(sums)), sums, rcv)
        dep = flight[4]
    fulls = land_scatter(0, flight, dx, fulls)
    full = _join_halves(fulls)
    grads = {n: f.reshape(w[n].shape) for n, f in zip(BIG, full)}

    packed = jnp.concatenate(smalls, axis=0)
    small_sum = _sum_devices(_gather_all(packed)).reshape(L, PACK_ROWS, D)
    shapes = {n: (w[n].shape[1:] if n not in ("conv_a_w", "conf_dw_w") else (w[n].shape[1], D)) for n in SMALL_NAMES}
    sg = _unpack_small(small_sum, shapes)
    for n in SMALL_NAMES:
        if n in ("conv_a_w", "conf_dw_w"):
            grads[n] = lax.dynamic_slice_in_dim(sg[n], k_chip * rk, rk, axis=2)
        else:
            grads[n] = sg[n]

    delta, new_m, new_v = {}, {}, {}
    for n in BIG:
        delta[n], new_m[n], new_v[n] = _adamw(w[n], grads[n], mom[n], var[n])
    for n in SMALL_NAMES:
        sh = w[n].shape
        flat = (sh[0] * sh[1], sh[2]) if n in ("conv_a_w", "conf_dw_w") else (-1, D)
        d, nm, nv = _adamw(*(t.reshape(flat) for t in (w[n], grads[n], mom[n], var[n])))
        delta[n], new_m[n], new_v[n] = d.reshape(sh), nm.reshape(sh), nv.reshape(sh)

    return (loss, dx.reshape(x.shape), *[grads[n] for n in WEIGHTS], *[delta[n] for n in WEIGHTS],
            *[new_m[n] for n in WEIGHTS], *[new_v[n] for n in WEIGHTS])
```

```python
import functools

import jax
import jax.numpy as jnp
from jax import lax
from jax.experimental import pallas as pl
from jax.experimental.pallas import tpu as pltpu

D = 1024
HEADS = 8
CHUNK = 128
KA = 3
KC = 31
HALO = 32
NSH = 4
NDEV = 8
EPS = 1e-6
BF = jnp.bfloat16
F32 = jnp.float32
VMEM_LIMIT = 56 * 1024 * 1024

ADAM_LR = 0.001
ADAM_B1 = 0.9
ADAM_B2 = 0.999
ADAM_EPS = 1e-08
ADAM_WD = 0.01
ADAM_STEP = 10

MESH = pl.DeviceIdType.MESH
ANY = pl.BlockSpec(memory_space=pl.ANY)


def _cp(*sem):
    return pltpu.CompilerParams(dimension_semantics=sem, vmem_limit_bytes=VMEM_LIMIT)


def _sig(x):
    return 1.0 / (1.0 + jnp.exp(-x))


_GC = 0.7978845608028654


def _gelu(x):
    x2 = x * x
    t = jnp.tanh(_GC * x * (1.0 + 0.044715 * x2))
    y = 0.5 * x * (1.0 + t)
    dy = 0.5 * (1.0 + t) + 0.5 * x * (1.0 - t * t) * _GC * (1.0 + 3.0 * 0.044715 * x2)
    return y, dy


def _rms_fwd(x, g):
    r = lax.rsqrt(jnp.mean(x * x, axis=-1, keepdims=True) + EPS)
    return x * r * g


def _rms_bwd(dy, x, g):
    r = lax.rsqrt(jnp.mean(x * x, axis=-1, keepdims=True) + EPS)
    xn = x * r
    dyg = dy * g
    dx = r * (dyg - xn * jnp.mean(dyg * xn, axis=-1, keepdims=True))
    return dx, jnp.sum(dy * xn, axis=0, keepdims=True)


def _ln_stats(x):
    mu = jnp.mean(x, axis=-1, keepdims=True)
    xc = x - mu
    r = lax.rsqrt(jnp.mean(xc * xc, axis=-1, keepdims=True) + EPS)
    return xc * r, r


def _ln_bwd(dn, n, r):
    return r * (dn - jnp.mean(dn, axis=-1, keepdims=True) - n * jnp.mean(dn * n, axis=-1, keepdims=True))


def _dot(a, b):
    return jnp.dot(a, b, preferred_element_type=F32)


def _dot_nt(a, b):
    return lax.dot_general(a, b, (((1,), (1,)), ((), ())), preferred_element_type=F32)


def _dot_tn(a, b):
    return lax.dot_general(a, b, (((0,), (0,)), ((), ())), preferred_element_type=F32)


def _in_proj(x, g, w, dep):
    T = x.shape[0]
    nc = w.shape[2]
    tm = min(T, 1024)
    tn = 1280
    nj = nc // tn

    def body(x_ref, g_ref, w_ref, dep_ref, h_ref, z_ref, h_scr):
        @pl.when((pl.program_id(1) == 0) & (pl.program_id(2) == 0))
        def _():
            h = _rms_fwd(x_ref[...], g_ref[...]).astype(BF)
            h_scr[...] = h
            h_ref[...] = h
        z_ref[...] = _dot(h_scr[...], w_ref[...]).astype(BF)

    return pl.pallas_call(
        body, name="in_proj", grid=(T // tm, NSH, nj),
        in_specs=[pl.BlockSpec((tm, D), lambda i, k, j: (i, 0)),
                  pl.BlockSpec((1, D), lambda i, k, j: (0, 0)),
                  pl.BlockSpec((None, D, tn), lambda i, k, j: (k, 0, j)), ANY],
        out_specs=[pl.BlockSpec((tm, D), lambda i, k, j: (i, 0)),
                   pl.BlockSpec((tm, tn), lambda i, k, j: (i, k * nj + j))],
        out_shape=[jax.ShapeDtypeStruct((T, D), BF), jax.ShapeDtypeStruct((T, NSH * nc), BF)],
        scratch_shapes=[pltpu.VMEM((tm, D), BF)],
        compiler_params=_cp("arbitrary", "arbitrary", "arbitrary"),
    )(x, g, w, dep)


def _tile_specs(tt, nt_total, reverse):
    def tile(i):
        return (nt_total - 1 - i) if reverse else i

    def cur(c):
        return pl.BlockSpec((tt, D), lambda i, *_: (tile(i), c))

    def halo(c):
        return pl.BlockSpec((HALO, D), lambda i, *_: (jnp.maximum(tile(i) * (tt // HALO) - 1, 0), c))

    def row(r=1):
        return pl.BlockSpec((r, D), lambda i, *_: (0, 0))

    return tile, cur, halo, row


def _causal_conv(ext, w_ref, ntap, tt):
    acc = None
    for k in range(ntap):
        term = w_ref[k:k + 1, :] * ext[pl.ds(HALO - (ntap - 1) + k, tt), :]
        acc = term if acc is None else acc + term
    return acc


def _anticausal_conv(ext, w_ref, ntap, tt):
    acc = None
    for k in range(ntap):
        term = w_ref[k:k + 1, :] * ext[pl.ds(ntap - 1 - k, tt), :]
        acc = term if acc is None else acc + term
    return acc


def _conv_wgrad(dw_ref, dout, ext, ntap, tt):
    for k in range(ntap):
        dw_ref[k:k + 1, :] += jnp.sum(dout * ext[pl.ds(HALO - (ntap - 1) + k, tt), :], axis=0, keepdims=True)


RC = 16


def _chunks(tt, fn, group=2):
    def step(c, carry):
        for u in range(group):
            fn(pl.multiple_of((c * group + u) * RC, RC))
        return carry
    lax.fori_loop(0, tt // (RC * group), step, 0)


def _shifted_copies(ext, sh, nrows):
    for s in range(SUBLANES):
        sh[s] = ext[pl.ds(s, nrows), :]


def _window(sh, o, r0):
    return sh[o % SUBLANES, pl.ds(r0 + (o // SUBLANES) * SUBLANES, RC), :]


def _fill_taps(wb, w_ref, ntap):
    for k in range(ntap):
        wb[k * SUBLANES:(k + 1) * SUBLANES, :] = jnp.broadcast_to(w_ref[k:k + 1, :], (SUBLANES, D))


def _conv_chunk(sh, wb, offs, r0):
    acc = None
    for k, o in enumerate(offs):
        wk = wb[k * SUBLANES:(k + 1) * SUBLANES, :]
        term = jnp.concatenate([wk] * (RC // SUBLANES), axis=0) * _window(sh, o, r0)
        acc = term if acc is None else acc + term
    return acc


def _conv_wgrad_chunked(dw_ref, d_ref, sh, offs, tt):
    group = 4
    for k, o in enumerate(offs):
        def step(c, acc, o=o):
            prods = []
            for u in range(group):
                r0 = pl.multiple_of((c * group + u) * RC, RC)
                prods.append(d_ref[pl.ds(r0, RC), :] * _window(sh, o, r0))
            return acc + ((prods[0] + prods[1]) + (prods[2] + prods[3]))
        acc = lax.fori_loop(0, tt // (RC * group), step, jnp.zeros((RC, D), F32))
        dw_ref[k:k + 1, :] += jnp.sum(acc, axis=0, keepdims=True)


def _causal_offsets(ntap):
    return [HALO - (ntap - 1) + k for k in range(ntap)]


def _anticausal_offsets(ntap):
    return [ntap - 1 - k for k in range(ntap)]


def _mix_a_fwd(z, wa, S):
    T = z.shape[0]
    tt = min(S, 256)
    nt = S // tt
    _, cur, halo, row = _tile_specs(tt, T // tt, False)

    def body(ah, ab, ac, ah_h, ac_h, w_ref, y_ref, ext):
        first = (pl.program_id(0) % nt) == 0
        ph = ah_h[...].astype(F32) * ac_h[...].astype(F32)
        ext[0:HALO, :] = jnp.where(first, 0.0, ph)
        ext[HALO:, :] = ah[...].astype(F32) * ac[...].astype(F32)
        q = _causal_conv(ext, w_ref, KA, tt)
        y_ref[...] = (ab[...].astype(F32) * q).astype(BF)

    return pl.pallas_call(
        body, name="mix_a_fwd", grid=(T // tt,),
        in_specs=[cur(0), cur(1), cur(2), halo(0), halo(2), row(KA)],
        out_specs=pl.BlockSpec((tt, D), lambda i: (i, 0)),
        out_shape=jax.ShapeDtypeStruct((T, D), BF),
        scratch_shapes=[pltpu.VMEM((HALO + tt, D), F32)],
        compiler_params=_cp("arbitrary"),
    )(z, z, z, z, z, wa)


def _mix_b_fwd(z, wc, bc, lg, lb, S):
    T = z.shape[0]
    tt = min(S, 256)
    nt = S // tt
    _, cur, halo, row = _tile_specs(tt, T // tt, False)

    nrows = HALO + tt
    offs = _causal_offsets(KC)

    def body(ca, cg, ca_h, cg_h, w_ref, bc_ref, lg_ref, lb_ref, y_ref, s_ref, ext, sh, wb):
        @pl.when(pl.program_id(0) == 0)
        def _():
            _fill_taps(wb, w_ref, KC)
            ext[nrows:, :] = jnp.zeros((SUBLANES, D), F32)

        first = (pl.program_id(0) % nt) == 0
        rh = ca_h[...].astype(F32) * _sig(cg_h[...].astype(F32))
        ext[0:HALO, :] = jnp.where(first, 0.0, rh)

        def glu(r0):
            rows = pl.ds(r0, RC)
            ext[pl.ds(HALO + r0, RC), :] = ca[rows, :].astype(F32) * _sig(cg[rows, :].astype(F32))
        _chunks(tt, glu)
        _shifted_copies(ext, sh, nrows)

        def conv(r0):
            rows = pl.ds(r0, RC)
            s = _conv_chunk(sh, wb, offs, r0) + bc_ref[...]
            s_ref[rows, :] = s.astype(BF)
            n, _ = _ln_stats(s)
            t = n * lg_ref[...] + lb_ref[...]
            y_ref[rows, :] = (t * _sig(t)).astype(BF)
        _chunks(tt, conv)

    return pl.pallas_call(
        body, name="mix_b_fwd", grid=(T // tt,),
        in_specs=[cur(3), cur(4), halo(3), halo(4), row(KC), row(), row(), row()],
        out_specs=[pl.BlockSpec((tt, D), lambda i: (i, 0))] * 2,
        out_shape=[jax.ShapeDtypeStruct((T, D), BF)] * 2,
        scratch_shapes=[pltpu.VMEM((nrows + SUBLANES, D), F32), pltpu.VMEM((SUBLANES, nrows, D), F32),
                        pltpu.VMEM((KC * SUBLANES, D), F32)],
        compiler_params=_cp("arbitrary"),
    )(z, z, z, z, wc, bc, lg, lb)


def _causal_mask(transposed):
    r = lax.broadcasted_iota(jnp.int32, (CHUNK, CHUNK), 0)
    c = lax.broadcasted_iota(jnp.int32, (CHUNK, CHUNK), 1)
    return (c >= r) if transposed else (r >= c)


def _mix_s_fwd(z, lg, lb, ws, bst, S):
    T = z.shape[0]
    tt = min(S, 256)
    _, cur, _, row = _tile_specs(tt, T // tt, False)

    def body(su, sv, lg_ref, lb_ref, ws_ref, bst_ref, y_ref, u_scr, vn_scr):
        u_scr[...] = _gelu(su[...].astype(F32))[0]
        n, _ = _ln_stats(_gelu(sv[...].astype(F32))[0])
        vn_scr[...] = (n * lg_ref[...] + lb_ref[...]).astype(BF)
        mask = _causal_mask(False)
        for h in range(HEADS):
            wm = jnp.where(mask, ws_ref[h], 0.0).astype(BF)
            cols = slice(h * CHUNK, (h + 1) * CHUNK)
            for c in range(tt // CHUNK):
                rows = slice(c * CHUNK, (c + 1) * CHUNK)
                mixed = _dot(wm, vn_scr[rows, cols]) + bst_ref[:, h:h + 1]
                y_ref[rows, cols] = (u_scr[rows, cols] * mixed).astype(BF)

    return pl.pallas_call(
        body, name="mix_s_fwd", grid=(T // tt,),
        in_specs=[cur(5), cur(6), row(), row(),
                  pl.BlockSpec((HEADS, CHUNK, CHUNK), lambda i: (0, 0, 0)),
                  pl.BlockSpec((CHUNK, HEADS), lambda i: (0, 0))],
        out_specs=pl.BlockSpec((tt, D), lambda i: (i, 0)),
        out_shape=jax.ShapeDtypeStruct((T, D), BF),
        scratch_shapes=[pltpu.VMEM((tt, D), F32), pltpu.VMEM((tt, D), BF)],
        compiler_params=_cp("arbitrary"),
    )(z, z, lg, lb, ws, bst)


def _mix_out_fwd(ya, yc, ys, z, x, wb, wo, gp):
    T = x.shape[0]
    tm = min(T, 256)
    rk = D // NSH

    def body(ya_ref, yc_ref, ys_ref, ga, gc, gs, x_ref, wb_ref, wo_ref, gp_ref, p_ref, mg_ref, m_ref, x1_ref):
        acc = None
        for b, (y_ref, g_ref) in enumerate(((ya_ref, ga), (yc_ref, gc), (ys_ref, gs))):
            pb = None
            for k in range(NSH):
                part = _dot(y_ref[:, k * rk:(k + 1) * rk], wb_ref[k, b])
                pb = part if pb is None else pb + part
            p_ref[b] = pb.astype(BF)
            term = _sig(g_ref[...].astype(F32)) * pb
            acc = term if acc is None else acc + term
        mg = acc.astype(BF)
        mg_ref[...] = mg
        m = _dot(mg, wo_ref[...])
        m_ref[...] = m.astype(BF)
        x1_ref[...] = x_ref[...] + _rms_fwd(m, gp_ref[...])

    rowblk = pl.BlockSpec((tm, D), lambda i: (i, 0))
    return pl.pallas_call(
        body, name="mix_out_fwd", grid=(T // tm,),
        in_specs=[rowblk, rowblk, rowblk,
                  pl.BlockSpec((tm, D), lambda i: (i, 7)), pl.BlockSpec((tm, D), lambda i: (i, 8)),
                  pl.BlockSpec((tm, D), lambda i: (i, 9)), rowblk,
                  pl.BlockSpec((NSH, 3, rk, D), lambda i: (0, 0, 0, 0)),
                  pl.BlockSpec((D, D), lambda i: (0, 0)),
                  pl.BlockSpec((1, D), lambda i: (0, 0))],
        out_specs=[pl.BlockSpec((3, tm, D), lambda i: (0, i, 0)), rowblk, rowblk, rowblk],
        out_shape=[jax.ShapeDtypeStruct((3, T, D), BF), jax.ShapeDtypeStruct((T, D), BF),
                   jax.ShapeDtypeStruct((T, D), BF), jax.ShapeDtypeStruct((T, D), F32)],
        compiler_params=_cp("arbitrary"),
    )(ya, yc, ys, z, z, z, x, wb, wo, gp)


def _ffn_fwd(x1, g3, w1, w2, g4):
    T = x1.shape[0]
    tm = min(T, 512)

    def body(x_ref, g3_ref, w1_ref, w2_ref, g4_ref, h_ref, a_ref, f_ref, x2_ref, h_scr, acc):
        k = pl.program_id(1)

        @pl.when(k == 0)
        def _():
            h = _rms_fwd(x_ref[...], g3_ref[...]).astype(BF)
            h_scr[...] = h
            h_ref[...] = h
            acc[...] = jnp.zeros_like(acc)

        a = _dot(h_scr[...], w1_ref[...])
        a_ref[...] = a.astype(BF)
        r = jnp.maximum(a, 0.0)
        acc[...] += _dot((r * r).astype(BF), w2_ref[...])

        @pl.when(k == NSH - 1)
        def _():
            f = acc[...]
            f_ref[...] = f.astype(BF)
            x2_ref[...] = x_ref[...] + _rms_fwd(f, g4_ref[...])

    rowblk = pl.BlockSpec((tm, D), lambda i, k: (i, 0))
    vec = pl.BlockSpec((1, D), lambda i, k: (0, 0))
    return pl.pallas_call(
        body, name="ffn_fwd", grid=(T // tm, NSH),
        in_specs=[rowblk, vec, pl.BlockSpec((None, D, D), lambda i, k: (k, 0, 0)),
                  pl.BlockSpec((D, D), lambda i, k: (k, 0)), vec],
        out_specs=[rowblk, pl.BlockSpec((tm, D), lambda i, k: (i, k)), rowblk, rowblk],
        out_shape=[jax.ShapeDtypeStruct((T, D), BF), jax.ShapeDtypeStruct((T, NSH * D), BF),
                   jax.ShapeDtypeStruct((T, D), BF), jax.ShapeDtypeStruct((T, D), F32)],
        scratch_shapes=[pltpu.VMEM((tm, D), BF), pltpu.VMEM((tm, D), F32)],
        compiler_params=_cp("arbitrary", "arbitrary"),
    )(x1, g3, w1, w2, g4)


def _loss_head(y, target):
    T = y.shape[0]
    tm = min(T, 512)

    def body(y_ref, t_ref, dy_ref, l_ref):
        @pl.when(pl.program_id(0) == 0)
        def _():
            l_ref[...] = jnp.zeros_like(l_ref)
        e = y_ref[...] - t_ref[...]
        dy_ref[...] = e * (1.0 / D)
        l_ref[...] += jnp.sum(e * e) * (0.5 / D)

    rowblk = pl.BlockSpec((tm, D), lambda i: (i, 0))
    return pl.pallas_call(
        body, name="loss_head", grid=(T // tm,),
        in_specs=[rowblk, rowblk],
        out_specs=[rowblk, pl.BlockSpec((1, 128), lambda i: (0, 0))],
        out_shape=[jax.ShapeDtypeStruct((T, D), F32), jax.ShapeDtypeStruct((1, 128), F32)],
        compiler_params=_cp("arbitrary"),
    )(y, target)


def _ffn_bwd(dx2, f, g4, a, w2, w1, x1, g3, dep):
    T = dx2.shape[0]
    tm = min(T, 512)

    def body(dx2_ref, f_ref, g4_ref, a_ref, w2_ref, w1_ref, x1_ref, g3_ref, dep_ref,
             df_ref, da_ref, dx1_ref, dg4_ref, dg3_ref, df_scr, acc):
        i, k = pl.program_id(0), pl.program_id(1)

        @pl.when((i == 0) & (k == 0))
        def _():
            dg4_ref[...] = jnp.zeros_like(dg4_ref)
            dg3_ref[...] = jnp.zeros_like(dg3_ref)

        @pl.when(k == 0)
        def _():
            df, dg = _rms_bwd(dx2_ref[...], f_ref[...].astype(F32), g4_ref[...])
            dg4_ref[...] += dg
            dfb = df.astype(BF)
            df_scr[...] = dfb
            df_ref[...] = dfb
            acc[...] = jnp.zeros_like(acc)

        av = a_ref[...].astype(F32)
        da = (_dot_nt(df_scr[...], w2_ref[...]) * (2.0 * jnp.maximum(av, 0.0))).astype(BF)
        da_ref[...] = da
        acc[...] += _dot_nt(da, w1_ref[...])

        @pl.when(k == NSH - 1)
        def _():
            dx, dg = _rms_bwd(acc[...], x1_ref[...], g3_ref[...])
            dg3_ref[...] += dg
            dx1_ref[...] = dx2_ref[...] + dx

    rowblk = pl.BlockSpec((tm, D), lambda i, k: (i, 0))
    vec = pl.BlockSpec((1, D), lambda i, k: (0, 0))
    return pl.pallas_call(
        body, name="ffn_bwd", grid=(T // tm, NSH),
        in_specs=[rowblk, rowblk, vec, pl.BlockSpec((tm, D), lambda i, k: (i, k)),
                  pl.BlockSpec((D, D), lambda i, k: (k, 0)),
                  pl.BlockSpec((None, D, D), lambda i, k: (k, 0, 0)), rowblk, vec, ANY],
        out_specs=[rowblk, pl.BlockSpec((tm, D), lambda i, k: (i, k)), rowblk, vec, vec],
        out_shape=[jax.ShapeDtypeStruct((T, D), BF), jax.ShapeDtypeStruct((T, NSH * D), BF),
                   jax.ShapeDtypeStruct((T, D), F32), jax.ShapeDtypeStruct((1, D), F32),
                   jax.ShapeDtypeStruct((1, D), F32)],
        scratch_shapes=[pltpu.VMEM((tm, D), BF), pltpu.VMEM((tm, D), F32)],
        compiler_params=_cp("arbitrary", "arbitrary"),
    )(dx2, f, g4, a, w2, w1, x1, g3, dep)


def _wgrad(name, ops, grid, in_specs, out_spec, out_shape, acc_shape, pick=None, relu2=False):
    nt = grid[-1]
    na = len(ops) - 1

    def body(*refs):
        a_refs, b_ref, o_ref, acc = refs[:na], refs[na], refs[na + 1], refs[na + 2]
        t = pl.program_id(len(grid) - 1)

        @pl.when(t == 0)
        def _():
            acc[...] = jnp.zeros_like(acc)

        def add(a_ref):
            av = a_ref[...]
            if relu2:
                r = jnp.maximum(av.astype(F32), 0.0)
                av = (r * r).astype(BF)
            acc[...] += _dot_tn(av, b_ref[...])

        if na == 1:
            add(a_refs[0])
        else:
            sel = pick()
            for n in range(na):
                pl.when(sel == n)(functools.partial(add, a_refs[n]))

        @pl.when(t == nt - 1)
        def _():
            if len(o_ref.shape) == 2:
                o_ref[...] = acc[...].astype(o_ref.dtype)
            else:
                rs = o_ref.shape[1]
                for q in range(o_ref.shape[0]):
                    o_ref[q] = acc[q * rs:(q + 1) * rs, :].astype(o_ref.dtype)

    return pl.pallas_call(
        body, name=name, grid=grid, in_specs=in_specs, out_specs=out_spec, out_shape=out_shape,
        scratch_shapes=[pltpu.VMEM(acc_shape, F32)],
        compiler_params=_cp(*(["arbitrary"] * len(grid))),
    )(*ops)


def _mix_out_bwd(dx1, m, gp, wo, p3, z, wb):
    T = dx1.shape[0]
    tm = min(T, 512)
    rk = D // NSH

    def body(dx1_ref, m_ref, gp_ref, wo_ref, p_ref, g_ref, wb_ref,
             dm_ref, dp_ref, dy_ref, dz_ref, dgp_ref, dmg):
        i, b = pl.program_id(0), pl.program_id(1)

        @pl.when((i == 0) & (b == 0))
        def _():
            dgp_ref[...] = jnp.zeros_like(dgp_ref)

        @pl.when(b == 0)
        def _():
            dm, dg = _rms_bwd(dx1_ref[...], m_ref[...].astype(F32), gp_ref[...])
            dgp_ref[...] += dg
            dmb = dm.astype(BF)
            dm_ref[...] = dmb
            dmg[...] = _dot_nt(dmb, wo_ref[...])

        gate = _sig(g_ref[...].astype(F32))
        d = dmg[...]
        dp = (d * gate).astype(BF)
        dp_ref[...] = dp
        dz_ref[...] = (d * p_ref[...].astype(F32) * gate * (1.0 - gate)).astype(BF)
        for k in range(NSH):
            dy_ref[:, k * rk:(k + 1) * rk] = _dot_nt(dp, wb_ref[k, b]).astype(BF)

    rowblk = pl.BlockSpec((tm, D), lambda i, b: (i, 0))
    br = pl.BlockSpec((None, tm, D), lambda i, b: (b, i, 0))
    vec = pl.BlockSpec((1, D), lambda i, b: (0, 0))
    return pl.pallas_call(
        body, name="mix_out_bwd", grid=(T // tm, 3),
        in_specs=[rowblk, rowblk, vec, pl.BlockSpec((D, D), lambda i, b: (0, 0)), br,
                  pl.BlockSpec((tm, D), lambda i, b: (i, 7 + b)),
                  pl.BlockSpec((NSH, 3, rk, D), lambda i, b: (0, 0, 0, 0))],
        out_specs=[rowblk, br, br, pl.BlockSpec((tm, D), lambda i, b: (i, 7 + b)), vec],
        out_shape=[jax.ShapeDtypeStruct((T, D), BF), jax.ShapeDtypeStruct((3, T, D), BF),
                   jax.ShapeDtypeStruct((3, T, D), BF), jax.ShapeDtypeStruct((T, 10 * D), BF),
                   jax.ShapeDtypeStruct((1, D), F32)],
        scratch_shapes=[pltpu.VMEM((tm, D), F32)],
        compiler_params=_cp("arbitrary", "arbitrary"),
    )(dx1, m, gp, wo, p3, z, wb)


def _mix_a_bwd(dz, dy3, z, wa, S):
    T = z.shape[0]
    tt = min(S, 256)
    nt = S // tt
    ntt = T // tt
    tile, cur, halo, row = _tile_specs(tt, ntt, True)

    def body(dz_in, dy_ref, ah, ab, ac, ah_h, ac_h, w_ref, dz_ref, dw_ref, ext_p, ext_d, stage):
        i, b = pl.program_id(0), pl.program_id(1)
        ti = ntt - 1 - i

        @pl.when((i == 0) & (b == 0))
        def _():
            dw_ref[...] = jnp.zeros_like(dw_ref)
            ext_d[...] = jnp.zeros_like(ext_d)

        @pl.when(b == 0)
        def _():
            first = (ti % nt) == 0
            last = (ti % nt) == nt - 1
            ahv, acv, abv = ah[...].astype(F32), ac[...].astype(F32), ab[...].astype(F32)
            ext_p[0:HALO, :] = jnp.where(first, 0.0, ah_h[...].astype(F32) * ac_h[...].astype(F32))
            ext_p[HALO:, :] = ahv * acv
            q = _causal_conv(ext_p, w_ref, KA, tt)
            dy = dy_ref[...].astype(F32)
            dq = dy * abv
            stage[1] = (dy * q).astype(BF)
            _conv_wgrad(dw_ref, dq, ext_p, KA, tt)
            ext_d[tt:, :] = jnp.where(last, 0.0, ext_d[0:HALO, :])
            ext_d[0:tt, :] = dq
            dp = _anticausal_conv(ext_d, w_ref, KA, tt)
            stage[0] = (dp * acv).astype(BF)
            stage[2] = (dp * ahv).astype(BF)

        dz_ref[...] = stage[b]

    return pl.pallas_call(
        body, name="mix_a_bwd", grid=(ntt, 3),
        in_specs=[ANY, pl.BlockSpec((None, tt, D), lambda i, b: (0, tile(i), 0)),
                  cur(0), cur(1), cur(2), halo(0), halo(2), row(KA)],
        out_specs=[pl.BlockSpec((tt, D), lambda i, b: (tile(i), b)), pl.BlockSpec((KA, D), lambda i, b: (0, 0))],
        out_shape=[jax.ShapeDtypeStruct(dz.shape, BF), jax.ShapeDtypeStruct((KA, D), F32)],
        scratch_shapes=[pltpu.VMEM((HALO + tt, D), F32), pltpu.VMEM((tt + HALO, D), F32),
                        pltpu.VMEM((3, tt, D), BF)],
        input_output_aliases={0: 0},
        compiler_params=_cp("arbitrary", "arbitrary"),
    )(dz, dy3, z, z, z, z, z, wa)


def _mix_b_bwd(dz, dy3, s, z, wc, lg, lb, S):
    T = z.shape[0]
    tt = min(S, 256)
    nt = S // tt
    ntt = T // tt
    tile, cur, halo, row = _tile_specs(tt, ntt, True)

    nrows = HALO + tt

    def body(dz_in, dy_ref, s_ref, ca, cg, ca_h, cg_h, w_ref, lg_ref, lb_ref,
             dz_ref, dw_ref, dbc_ref, dlg_ref, dlb_ref, ext_r, ext_d, sh, wb, accs, stage):
        i, b = pl.program_id(0), pl.program_id(1)
        ti = ntt - 1 - i

        @pl.when((i == 0) & (b == 0))
        def _():
            dw_ref[...] = jnp.zeros_like(dw_ref)
            dbc_ref[...] = jnp.zeros_like(dbc_ref)
            dlg_ref[...] = jnp.zeros_like(dlg_ref)
            dlb_ref[...] = jnp.zeros_like(dlb_ref)
            ext_d[...] = jnp.zeros_like(ext_d)
            ext_r[nrows:, :] = jnp.zeros((SUBLANES, D), F32)
            _fill_taps(wb, w_ref, KC)

        @pl.when(b == 0)
        def _():
            first = (ti % nt) == 0
            last = (ti % nt) == nt - 1
            ext_r[0:HALO, :] = jnp.where(first, 0.0, ca_h[...].astype(F32) * _sig(cg_h[...].astype(F32)))
            ext_d[tt:nrows, :] = jnp.where(last, 0.0, ext_d[0:HALO, :])
            accs[...] = jnp.zeros_like(accs)

            def point(r0):
                rows = pl.ds(r0, RC)
                n, r = _ln_stats(s_ref[rows, :].astype(F32))
                t = n * lg_ref[...] + lb_ref[...]
                sg = _sig(t)
                dt = dy_ref[rows, :].astype(F32) * (sg * (1.0 + t * (1.0 - sg)))
                accs[0] += dt * n
                accs[1] += dt
                ds = _ln_bwd(dt * lg_ref[...], n, r)
                accs[2] += ds
                ext_d[rows, :] = ds
                ext_r[pl.ds(HALO + r0, RC), :] = ca[rows, :].astype(F32) * _sig(cg[rows, :].astype(F32))
            _chunks(tt, point)
            dlg_ref[...] += jnp.sum(accs[0], axis=0, keepdims=True)
            dlb_ref[...] += jnp.sum(accs[1], axis=0, keepdims=True)
            dbc_ref[...] += jnp.sum(accs[2], axis=0, keepdims=True)

            _shifted_copies(ext_r, sh, nrows)
            _conv_wgrad_chunked(dw_ref, ext_d, sh, _causal_offsets(KC), tt)
            _shifted_copies(ext_d, sh, nrows)

            def conv(r0):
                rows = pl.ds(r0, RC)
                dr = _conv_chunk(sh, wb, _anticausal_offsets(KC), r0)
                cav = ca[rows, :].astype(F32)
                sgc = _sig(cg[rows, :].astype(F32))
                stage[0, rows, :] = (dr * sgc).astype(BF)
                stage[1, rows, :] = (dr * cav * sgc * (1.0 - sgc)).astype(BF)
            _chunks(tt, conv)

        dz_ref[...] = stage[b]

    vec = pl.BlockSpec((1, D), lambda i, b: (0, 0))
    return pl.pallas_call(
        body, name="mix_b_bwd", grid=(ntt, 2),
        in_specs=[ANY, pl.BlockSpec((None, tt, D), lambda i, b: (1, tile(i), 0)),
                  pl.BlockSpec((tt, D), lambda i, b: (tile(i), 0)),
                  cur(3), cur(4), halo(3), halo(4), row(KC), row(), row()],
        out_specs=[pl.BlockSpec((tt, D), lambda i, b: (tile(i), 3 + b)),
                   pl.BlockSpec((KC, D), lambda i, b: (0, 0)), vec, vec, vec],
        out_shape=[jax.ShapeDtypeStruct(dz.shape, BF), jax.ShapeDtypeStruct((KC, D), F32)]
        + [jax.ShapeDtypeStruct((1, D), F32)] * 3,
        scratch_shapes=[pltpu.VMEM((nrows + SUBLANES, D), F32), pltpu.VMEM((nrows + SUBLANES, D), F32),
                        pltpu.VMEM((SUBLANES, nrows, D), F32), pltpu.VMEM((KC * SUBLANES, D), F32),
                        pltpu.VMEM((3, RC, D), F32), pltpu.VMEM((2, tt, D), BF)],
        input_output_aliases={0: 0},
        compiler_params=_cp("arbitrary", "arbitrary"),
    )(dz, dy3, s, z, z, z, z, wc, lg, lb)


def _mix_s_bwd(dz, dy3, z, lg, lb, ws, wst, bst, S):
    T = z.shape[0]
    tt = min(S, 256)
    ntt = T // tt
    _, cur, _, row = _tile_specs(tt, ntt, False)

    def body(dz_in, dy_ref, su, sv, lg_ref, lb_ref, ws_ref, wst_ref, bst_ref,
             dz_ref, dws_ref, dbst_ref, dlg_ref, dlb_ref, u_scr, vn_scr, dvn_scr, stage):
        i, b = pl.program_id(0), pl.program_id(1)

        @pl.when((i == 0) & (b == 0))
        def _():
            dws_ref[...] = jnp.zeros_like(dws_ref)
            dbst_ref[...] = jnp.zeros_like(dbst_ref)
            dlg_ref[...] = jnp.zeros_like(dlg_ref)
            dlb_ref[...] = jnp.zeros_like(dlb_ref)

        @pl.when(b == 0)
        def _():
            u, du_dx = _gelu(su[...].astype(F32))
            v, dv_dx = _gelu(sv[...].astype(F32))
            u_scr[...] = u
            n, r = _ln_stats(v)
            vn_scr[...] = (n * lg_ref[...] + lb_ref[...]).astype(BF)
            mask = _causal_mask(False)
            mask_t = _causal_mask(True)
            for h in range(HEADS):
                wm = jnp.where(mask, ws_ref[h], 0.0).astype(BF)
                wmt = jnp.where(mask_t, wst_ref[h], 0.0).astype(BF)
                cols = slice(h * CHUNK, (h + 1) * CHUNK)
                for c in range(tt // CHUNK):
                    rows = slice(c * CHUNK, (c + 1) * CHUNK)
                    vb = vn_scr[rows, cols]
                    mixed = _dot(wm, vb) + bst_ref[:, h:h + 1]
                    dy = dy_ref[rows, cols].astype(F32)
                    dmix = dy * u_scr[rows, cols]
                    u_scr[rows, cols] = dy * mixed
                    dbst_ref[:, h:h + 1] += jnp.sum(dmix, axis=1, keepdims=True)
                    dmb = dmix.astype(BF)
                    dws_ref[h] += _dot_nt(dmb, vb)
                    dvn_scr[rows, cols] = _dot(wmt, dmb)
            stage[0] = (u_scr[...] * du_dx).astype(BF)
            dvn = dvn_scr[...]
            dlg_ref[...] += jnp.sum(dvn * n, axis=0, keepdims=True)
            dlb_ref[...] += jnp.sum(dvn, axis=0, keepdims=True)
            stage[1] = (_ln_bwd(dvn * lg_ref[...], n, r) * dv_dx).astype(BF)

        dz_ref[...] = stage[b]

    vec = pl.BlockSpec((1, D), lambda i, b: (0, 0))
    wsp = pl.BlockSpec((HEADS, CHUNK, CHUNK), lambda i, b: (0, 0, 0))
    bsp = pl.BlockSpec((CHUNK, HEADS), lambda i, b: (0, 0))
    return pl.pallas_call(
        body, name="mix_s_bwd", grid=(ntt, 2),
        in_specs=[ANY, pl.BlockSpec((None, tt, D), lambda i, b: (2, i, 0)),
                  cur(5), cur(6), row(), row(), wsp, wsp, bsp],
        out_specs=[pl.BlockSpec((tt, D), lambda i, b: (i, 5 + b)), wsp, bsp, vec, vec],
        out_shape=[jax.ShapeDtypeStruct(dz.shape, BF), jax.ShapeDtypeStruct((HEADS, CHUNK, CHUNK), F32),
                   jax.ShapeDtypeStruct((CHUNK, HEADS), F32), jax.ShapeDtypeStruct((1, D), F32),
                   jax.ShapeDtypeStruct((1, D), F32)],
        scratch_shapes=[pltpu.VMEM((tt, D), F32), pltpu.VMEM((tt, D), BF), pltpu.VMEM((tt, D), F32),
                        pltpu.VMEM((2, tt, D), BF)],
        input_output_aliases={0: 0},
        compiler_params=_cp("arbitrary", "arbitrary"),
    )(dz, dy3, z, z, lg, lb, ws, wst, bst)


def _in_proj_bwd(dz, w, x, g, dx1):
    T = x.shape[0]
    nc = w.shape[2]
    tm = min(T, 512)
    tn = 1280
    nj = nc // tn

    def body(dz_ref, w_ref, x_ref, g_ref, dx1_ref, dx_ref, dg_ref, acc):
        i, k, j = pl.program_id(0), pl.program_id(1), pl.program_id(2)

        @pl.when((i == 0) & (k == 0) & (j == 0))
        def _():
            dg_ref[...] = jnp.zeros_like(dg_ref)

        @pl.when((k == 0) & (j == 0))
        def _():
            acc[...] = jnp.zeros_like(acc)

        acc[...] += _dot_nt(dz_ref[...], w_ref[...])

        @pl.when((k == NSH - 1) & (j == nj - 1))
        def _():
            dx, dg = _rms_bwd(acc[...], x_ref[...], g_ref[...])
            dg_ref[...] += dg
            dx_ref[...] = dx1_ref[...] + dx

    rowblk = pl.BlockSpec((tm, D), lambda i, k, j: (i, 0))
    vec = pl.BlockSpec((1, D), lambda i, k, j: (0, 0))
    return pl.pallas_call(
        body, name="in_proj_bwd", grid=(T // tm, NSH, nj),
        in_specs=[pl.BlockSpec((tm, tn), lambda i, k, j: (i, k * nj + j)),
                  pl.BlockSpec((None, D, tn), lambda i, k, j: (k, 0, j)), rowblk, vec, rowblk],
        out_specs=[rowblk, vec],
        out_shape=[jax.ShapeDtypeStruct((T, D), F32), jax.ShapeDtypeStruct((1, D), F32)],
        scratch_shapes=[pltpu.VMEM((tm, D), F32)],
        compiler_params=_cp("arbitrary", "arbitrary", "arbitrary"),
    )(dz, w, x, g, dx1)


def _layer_fwd(x, p, S, dep):
    h, z = _in_proj(x, p["g_mix_pre"], p["w_in"], dep)
    ya = _mix_a_fwd(z, p["conv_a_w"], S)
    yc, s = _mix_b_fwd(z, p["conf_dw_w"], p["conf_dw_b"], p["conf_ln_g"], p["conf_ln_b"], S)
    ys = _mix_s_fwd(z, p["sgu_ln_g"], p["sgu_ln_b"], p["sgu_ws"], p["sgu_bt"], S)
    p3, merged, m, x1 = _mix_out_fwd(ya, yc, ys, z, x, p["w_branch"], p["w_out"], p["g_mix_post"])
    h2, a, f, x2 = _ffn_fwd(x1, p["g_ffn_pre"], p["w_ff1"], p["w_ff2"], p["g_ffn_post"])
    saved = dict(x=x, h=h, z=z, ya=ya, yc=yc, ys=ys, s=s, p3=p3, merged=merged, m=m, x1=x1, h2=h2, a=a, f=f)
    return x2, saved


def _layer_bwd(dx2, p, sv, S, dep):
    T = dx2.shape[0]
    bt = min(T, 512)
    nt = T // bt
    rk = D // NSH
    df, da, dx1, dg_ffn_post, dg_ffn_pre = _ffn_bwd(dx2, sv["f"], p["g_ffn_post"], sv["a"], p["w_ff2"],
                                                    p["w_ff1"], sv["x1"], p["g_ffn_pre"], dep)
    dw_ff2 = _wgrad("wgrad_ff2", (sv["a"], df), (NSH, nt),
                    [pl.BlockSpec((bt, D), lambda k, t: (t, k)), pl.BlockSpec((bt, D), lambda k, t: (t, 0))],
                    pl.BlockSpec((None, D, D), lambda k, t: (k, 0, 0)),
                    jax.ShapeDtypeStruct((NSH, D, D), BF), (D, D), relu2=True)
    dw_ff1 = _wgrad("wgrad_ff1", (sv["h2"], da), (NSH, nt),
                    [pl.BlockSpec((bt, D), lambda k, t: (t, 0)), pl.BlockSpec((bt, D), lambda k, t: (t, k))],
                    pl.BlockSpec((None, D, D), lambda k, t: (k, 0, 0)),
                    jax.ShapeDtypeStruct((NSH, D, D), BF), (D, D))
    dm, dp3, dy3, dz, dg_mix_post = _mix_out_bwd(dx1, sv["m"], p["g_mix_post"], p["w_out"], sv["p3"], sv["z"],
                                                 p["w_branch"])
    dw_out = _wgrad("wgrad_out", (sv["merged"], dm), (nt,),
                    [pl.BlockSpec((bt, D), lambda t: (t, 0)), pl.BlockSpec((bt, D), lambda t: (t, 0))],
                    pl.BlockSpec((D, D), lambda t: (0, 0)),
                    jax.ShapeDtypeStruct((D, D), BF), (D, D)).reshape(NSH, rk, D)
    ysp = lambda n: pl.BlockSpec((bt, D), lambda b, t: (jnp.where(b == n, t, 0), 0))
    dw_br = _wgrad("wgrad_branch", (sv["ya"], sv["yc"], sv["ys"], dp3), (3, nt),
                   [ysp(0), ysp(1), ysp(2), pl.BlockSpec((None, bt, D), lambda b, t: (b, t, 0))],
                   pl.BlockSpec((NSH, None, rk, D), lambda b, t: (0, b, 0, 0)),
                   jax.ShapeDtypeStruct((NSH, 3, rk, D), BF), (D, D), pick=lambda: pl.program_id(0))
    dz, dwa = _mix_a_bwd(dz, dy3, sv["z"], p["conv_a_w"], S)
    dz, dwc, dbc, dclg, dclb = _mix_b_bwd(dz, dy3, sv["s"], sv["z"], p["conf_dw_w"], p["conf_ln_g"],
                                          p["conf_ln_b"], S)
    dz, dws, dbst, dslg, dslb = _mix_s_bwd(dz, dy3, sv["z"], p["sgu_ln_g"], p["sgu_ln_b"], p["sgu_ws"],
                                           p["sgu_wst"], p["sgu_bt"], S)
    dx, dg_mix_pre = _in_proj_bwd(dz, p["w_in"], sv["x"], p["g_mix_pre"], dx1)
    tn = 1280
    nj = p["w_in"].shape[2] // tn
    dw_in = _wgrad("wgrad_in", (sv["h"], dz), (NSH, nj, nt),
                   [pl.BlockSpec((bt, D), lambda k, j, t: (t, 0)),
                    pl.BlockSpec((bt, tn), lambda k, j, t: (t, k * nj + j))],
                   pl.BlockSpec((None, D, tn), lambda k, j, t: (k, 0, j)),
                   jax.ShapeDtypeStruct(p["w_in"].shape, BF), (D, tn))
    tril = jnp.tril(jnp.ones((CHUNK, CHUNK), bool))
    small = dict(norm_mix_pre=dg_mix_pre, norm_mix_post=dg_mix_post, norm_ffn_pre=dg_ffn_pre,
                 norm_ffn_post=dg_ffn_post, conv_a_w=dwa, conf_dw_w=dwc, conf_dw_b=dbc, conf_ln_g=dclg,
                 conf_ln_b=dclb, sgu_ln_g=dslg, sgu_ln_b=dslb,
                 sgu_ws=jnp.where(tril[None], dws, 0.0), sgu_b=dbst.T)
    big = dict(w_in=dw_in, w_branch=dw_br, w_out=dw_out, w_ff1=dw_ff1, w_ff2=dw_ff2)
    return dx, big, small


SMALL_NAMES = ("norm_mix_pre", "norm_mix_post", "norm_ffn_pre", "norm_ffn_post", "conv_a_w", "conf_dw_w",
               "conf_dw_b", "conf_ln_g", "conf_ln_b", "sgu_ln_g", "sgu_ln_b", "sgu_b", "sgu_ws")
SMALL_ROWS = dict(norm_mix_pre=1, norm_mix_post=1, norm_ffn_pre=1, norm_ffn_post=1, conv_a_w=KA, conf_dw_w=KC,
                  conf_dw_b=1, conf_ln_g=1, conf_ln_b=1, sgu_ln_g=1, sgu_ln_b=1, sgu_b=1, sgu_ws=CHUNK)
SUBLANES = 8


def _pad8(r):
    return -(-r // SUBLANES) * SUBLANES


PACK_ROWS = sum(_pad8(r) for r in SMALL_ROWS.values())


def _pack_small(d):
    parts = []
    for n in SMALL_NAMES:
        r = SMALL_ROWS[n]
        parts.append(jnp.pad(d[n].reshape(r, D).astype(F32), ((0, _pad8(r) - r), (0, 0))))
    return jnp.concatenate(parts, axis=0)


def _unpack_small(a, shapes):
    out, r = {}, 0
    for n in SMALL_NAMES:
        out[n] = a[:, r:r + SMALL_ROWS[n]].reshape((a.shape[0],) + tuple(shapes[n]))
        r += _pad8(SMALL_ROWS[n])
    return out


def _me():
    return lax.axis_index("x"), lax.axis_index("y"), lax.axis_index("c")


def _slab(ref, q, a, h=None):
    r = ref.shape[1]
    rows = slice(None) if h is None else pl.ds(h * (r // 2), r // 2)
    return ref.at[pl.ds(q * a, a), rows, :]


def _rows(ref, h):
    r = ref.shape[-2]
    lead = (slice(None),) * (len(ref.shape) - 2)
    return ref.at[lead + (pl.ds(h * (r // 2), r // 2), slice(None))]


def _rcopy(src, dst, sems, idx, dev):
    return pltpu.make_async_remote_copy(src_ref=src, dst_ref=dst, send_sem=sems[0].at[idx], recv_sem=sems[1].at[idx],
                                        device_id=dev, device_id_type=MESH)


def _gather_weights(srcs, layer):
    n = len(srcs)

    def body(*refs):
        src, dst = refs[:n], refs[n:2 * n]
        sems = refs[2 * n:2 * n + 2]
        lsem = refs[2 * n + 2]
        x, y, c = _me()
        k = 2 * x + y
        chips = [(1 - x, y), (x, 1 - y), (1 - x, 1 - y)]
        av = [s.shape[1] for s in src]
        own = [pltpu.make_async_copy(src[i].at[layer], _slab(dst[i], k, av[i]), lsem.at[i]) for i in range(n)]
        for cp in own:
            cp.start()
        first = []
        for j, (qx, qy) in enumerate(chips):
            for i in range(n):
                first.append(_rcopy(_rows(src[i].at[layer], c), _slab(dst[i], k, av[i], c), sems, j * n + i,
                                    (qx, qy, c)))
        for cp in first:
            cp.start()
        passed = []
        for j, (qx, qy) in enumerate(chips):
            kq = 2 * qx + qy
            for i in range(n):
                got = _slab(dst[i], kq, av[i], c)
                _rcopy(got, got, sems, j * n + i, (x, y, c)).wait_recv()
                cp = _rcopy(got, got, sems, 3 * n + j * n + i, (x, y, 1 - c))
                cp.start()
                passed.append(cp)
        for j, (qx, qy) in enumerate(chips):
            kq = 2 * qx + qy
            for i in range(n):
                other = _slab(dst[i], kq, av[i], 1 - c)
                _rcopy(other, other, sems, 3 * n + j * n + i, (x, y, c)).wait_recv()
        for cp in first + passed:
            cp.wait_send()
        for cp in own:
            cp.wait()

    outs = [jax.ShapeDtypeStruct((NSH * s.shape[1],) + s.shape[2:], s.dtype) for s in srcs]
    return pl.pallas_call(
        body, name="gather_weights", in_specs=[ANY] * n, out_specs=[ANY] * n, out_shape=outs,
        scratch_shapes=[pltpu.SemaphoreType.DMA((6 * n,)), pltpu.SemaphoreType.DMA((6 * n,)),
                        pltpu.SemaphoreType.DMA((n,))],
    )(*srcs)


def _send_halves_to_sibling(parts):
    n = len(parts)

    def body(*refs):
        src, dst = refs[:n], refs[n:2 * n]
        sems = refs[2 * n:2 * n + 2]
        x, y, c = _me()
        cps = [_rcopy(_rows(src[i], 1 - c), dst[i], sems, i, (x, y, 1 - c)) for i in range(n)]
        for cp in cps:
            cp.start()
        for cp in cps:
            cp.wait()

    outs = [jax.ShapeDtypeStruct((p.shape[0], p.shape[1] // 2, p.shape[2]), p.dtype) for p in parts]
    return pl.pallas_call(
        body, name="pair_exchange", in_specs=[ANY] * n, out_specs=[ANY] * n, out_shape=outs,
        scratch_shapes=[pltpu.SemaphoreType.DMA((n,)), pltpu.SemaphoreType.DMA((n,))],
    )(*parts)


def _pair_add(part, sib, c):
    A, R, C = part.shape
    hr = R // 2
    br = min(hr, 512)
    nb = hr // br

    def body(c_ref, p_ref, s_ref, o_ref):
        o_ref[...] = (p_ref[...].astype(F32) + s_ref[...].astype(F32)).astype(BF)

    return pl.pallas_call(
        body, name="pair_add",
        grid_spec=pltpu.PrefetchScalarGridSpec(
            num_scalar_prefetch=1, grid=(A, nb),
            in_specs=[pl.BlockSpec((None, br, C), lambda a, i, c_ref: (a, c_ref[0] * nb + i, 0)),
                      pl.BlockSpec((None, br, C), lambda a, i, c_ref: (a, i, 0))],
            out_specs=pl.BlockSpec((None, br, C), lambda a, i, c_ref: (a, i, 0))),
        out_shape=jax.ShapeDtypeStruct((A, hr, C), BF),
        compiler_params=_cp("arbitrary", "arbitrary"),
    )(c, part, sib)


def _other_chips(x, y):
    return [(1 - x, y), (x, 1 - y), (1 - x, 1 - y)]


def _split_call(name, copies, srcs, lands, sems=None, after=None):
    n, m = len(srcs), len(lands)
    hbm = lambda t: pltpu.HBM(t.shape, t.dtype)
    pin = lambda t: pltpu.with_memory_space_constraint(t, pltpu.HBM)
    thru = [hbm(t) for t in srcs] + [hbm(t) for t in lands]
    sem_spec = pl.BlockSpec(memory_space=pltpu.SEMAPHORE)
    effect = pltpu.CompilerParams(has_side_effects=pltpu.SideEffectType.DATAFLOW_SIDE_EFFECTING)
    if sems is None:
        def start_body(*refs):
            src, land = refs[:n], refs[n:n + m]
            ssem, rsem = refs[n + m], refs[n + m + 1]
            token = refs[-1]
            cps = copies(src, land, (ssem, rsem))
            for cp in cps:
                cp.start()
            token[...] = jnp.zeros_like(token)

        ncp = copies.count
        out = pl.pallas_call(
            start_body, name=name,
            out_shape=(pltpu.SemaphoreType.DMA((ncp,)), pltpu.SemaphoreType.DMA((ncp,)), *thru,
                       jax.ShapeDtypeStruct((8, 128), F32)),
            in_specs=[ANY] * (n + m),
            out_specs=(sem_spec, sem_spec, *([ANY] * (n + m)), pl.BlockSpec(memory_space=pltpu.VMEM)),
            input_output_aliases={i: 2 + i for i in range(n + m)},
            compiler_params=effect,
        )(*[pin(t) for t in srcs], *[pin(t) for t in lands])
        return out[0], out[1], list(out[2:2 + n]), list(out[2 + n:2 + n + m]), out[-1]

    def wait_body(*refs):
        src, land = refs[:n], refs[n:n + m]
        ssem, rsem = refs[n + m], refs[n + m + 1]
        for cp in copies(src, land, (ssem, rsem)):
            cp.wait_send()
            cp.wait_recv()

    out = pl.pallas_call(
        wait_body, name=name, out_shape=tuple(thru),
        in_specs=[ANY] * (n + m) + [sem_spec, sem_spec, ANY],
        out_specs=tuple([ANY] * (n + m)),
        input_output_aliases={i: i for i in range(n + m)},
        compiler_params=effect,
    )(*srcs, *lands, sems[0], sems[1], after)
    return list(out[:n]), list(out[n:])


def _cast_into(w, land, layer, kidx, dep):
    _, a, R, C = w.shape
    br = R
    while br * C > 256 * 1024 and br % 32 == 0:
        br //= 2

    def body(k_ref, w_ref, land_ref, dep_ref, o_ref):
        o_ref[...] = w_ref[...].astype(BF)

    return pl.pallas_call(
        body, name="cast_into",
        grid_spec=pltpu.PrefetchScalarGridSpec(
            num_scalar_prefetch=1, grid=(a, R // br),
            in_specs=[pl.BlockSpec((None, None, br, C), lambda e, i, k: (layer, e, i, 0)), ANY, ANY],
            out_specs=pl.BlockSpec((None, br, C), lambda e, i, k: (k[0] * a + e, i, 0))),
        out_shape=jax.ShapeDtypeStruct(land.shape, BF), input_output_aliases={2: 0},
        compiler_params=_cp("arbitrary", "arbitrary"),
    )(kidx, w, land, dep)


class _GatherCopies:
    def __init__(self, n):
        self.n, self.count = n, 3 * n

    def __call__(self, src, land, sems):
        x, y, c = _me()
        k = 2 * x + y
        cps = []
        for j, (qx, qy) in enumerate(_other_chips(x, y)):
            for i in range(self.n):
                mine = _slab(land[i], k, land[i].shape[0] // NSH, c)
                cps.append(_rcopy(mine, mine, sems, j * self.n + i, (qx, qy, c)))
        return cps


def _gather_finish(lands):
    n = len(lands)

    def body(*refs):
        dst = refs[n:2 * n]
        sems = refs[2 * n:2 * n + 2]
        x, y, c = _me()
        av = [d.shape[0] // NSH for d in dst]
        cps = []
        for j, (qx, qy) in enumerate(_other_chips(x, y)):
            for i in range(n):
                got = _slab(dst[i], 2 * qx + qy, av[i], c)
                cps.append(_rcopy(got, got, sems, j * n + i, (x, y, 1 - c)))
        for cp in cps:
            cp.start()
        for j, (qx, qy) in enumerate(_other_chips(x, y)):
            for i in range(n):
                other = _slab(dst[i], 2 * qx + qy, av[i], 1 - c)
                _rcopy(other, other, sems, j * n + i, (x, y, c)).wait_recv()
        for cp in cps:
            cp.wait_send()

    return pl.pallas_call(
        body, name="gather_finish", in_specs=[ANY] * n, out_specs=[ANY] * n,
        out_shape=[jax.ShapeDtypeStruct(t.shape, t.dtype) for t in lands],
        input_output_aliases={i: i for i in range(n)},
        scratch_shapes=[pltpu.SemaphoreType.DMA((3 * n,)), pltpu.SemaphoreType.DMA((3 * n,))],
    )(*lands)


class _ScatterCopies:
    def __init__(self, n):
        self.n, self.count = n, 3 * n

    def __call__(self, src, land, sems):
        x, y, c = _me()
        k = 2 * x + y
        cps = []
        for j, (qx, qy) in enumerate(_other_chips(x, y)):
            for i in range(self.n):
                a = src[i].shape[0] // NSH
                cps.append(_rcopy(_slab(src[i], 2 * qx + qy, a), _slab(land[i], k, a), sems, j * self.n + i,
                                  (qx, qy, c)))
        return cps


def _sum_chips(own, rcv, acc, layer, nlayers, idx):
    A, hr, C = rcv.shape
    a = A // NSH
    br = min(hr, 512)
    nb = hr // br

    def body(*refs):
        r0, r1, r2, r3 = refs[1:5]
        o_ref = refs[-1]
        o_ref[...] = ((r0[...].astype(F32) + r1[...].astype(F32)) + r2[...].astype(F32)) + r3[...].astype(F32)

    slot = lambda s: pl.BlockSpec((None, br, C), lambda e, i, ix: (ix[s] * a + e, i, 0))
    ops = [own, rcv, rcv, rcv]
    in_specs = [slot(0), slot(1), slot(2), slot(3)]
    aliases = {}
    if acc is not None:
        ops.append(acc)
        in_specs.append(ANY)
        aliases = {5: 0}
    return pl.pallas_call(
        body, name="sum_chips",
        grid_spec=pltpu.PrefetchScalarGridSpec(
            num_scalar_prefetch=1, grid=(a, nb), in_specs=in_specs,
            out_specs=pl.BlockSpec((None, None, br, C), lambda e, i, ix: (layer, e, ix[4] * nb + i, 0))),
        out_shape=jax.ShapeDtypeStruct((nlayers, a, 2 * hr, C), F32), input_output_aliases=aliases,
        compiler_params=_cp("arbitrary", "arbitrary"),
    )(idx, *ops)


def _join_halves(fulls):
    n = len(fulls)

    def body(*refs):
        buf = refs[n:2 * n]
        sems = refs[2 * n:2 * n + 2]
        x, y, c = _me()
        cps = [_rcopy(_rows(buf[i], c), _rows(buf[i], c), sems, i, (x, y, 1 - c)) for i in range(n)]
        for cp in cps:
            cp.start()
        for i in range(n):
            _rcopy(_rows(buf[i], 1 - c), _rows(buf[i], 1 - c), sems, i, (x, y, c)).wait_recv()
        for cp in cps:
            cp.wait_send()

    return pl.pallas_call(
        body, name="join_halves", in_specs=[ANY] * n, out_specs=[ANY] * n,
        out_shape=[jax.ShapeDtypeStruct(t.shape, t.dtype) for t in fulls],
        input_output_aliases={i: i for i in range(n)},
        scratch_shapes=[pltpu.SemaphoreType.DMA((n,)), pltpu.SemaphoreType.DMA((n,))],
    )(*fulls)


def _gather_all(block):
    R, C = block.shape

    def body(src, dst, ssem, rsem, lsem):
        sems = (ssem, rsem)
        x, y, c = _me()
        chips = [(1 - x, y), (x, 1 - y), (1 - x, 1 - y)]

        def at(px, py, pc):
            return dst.at[4 * px + 2 * py + pc]

        own = pltpu.make_async_copy(src, at(x, y, c), lsem)
        own.start()
        first = [_rcopy(src, at(x, y, c), sems, 0, (x, y, 1 - c))]
        first += [_rcopy(src, at(x, y, c), sems, 1 + j, (qx, qy, c)) for j, (qx, qy) in enumerate(chips)]
        for cp in first:
            cp.start()
        passed = []
        for j, (qx, qy) in enumerate(chips):
            got = at(qx, qy, c)
            _rcopy(got, got, sems, 1 + j, (x, y, c)).wait_recv()
            cp = _rcopy(got, got, sems, 4 + j, (x, y, 1 - c))
            cp.start()
            passed.append(cp)
        sib = at(x, y, 1 - c)
        _rcopy(sib, sib, sems, 0, (x, y, c)).wait_recv()
        for j, (qx, qy) in enumerate(chips):
            other = at(qx, qy, 1 - c)
            _rcopy(other, other, sems, 4 + j, (x, y, c)).wait_recv()
        for cp in first + passed:
            cp.wait_send()
        own.wait()

    return pl.pallas_call(
        body, name="gather_all", in_specs=[ANY], out_specs=ANY,
        out_shape=jax.ShapeDtypeStruct((NDEV, R, C), block.dtype),
        scratch_shapes=[pltpu.SemaphoreType.DMA((7,)), pltpu.SemaphoreType.DMA((7,)), pltpu.SemaphoreType.DMA],
    )(block)


def _sum_devices(g):
    _, R, C = g.shape
    br = 264 if R % 264 == 0 else R

    def body(g_ref, o_ref):
        acc = g_ref[0]
        for d in range(1, NDEV):
            acc = acc + g_ref[d]
        o_ref[...] = acc

    return pl.pallas_call(
        body, name="sum_devices", grid=(R // br,),
        in_specs=[pl.BlockSpec((NDEV, br, C), lambda i: (0, i, 0))],
        out_specs=pl.BlockSpec((br, C), lambda i: (i, 0)),
        out_shape=jax.ShapeDtypeStruct((R, C), F32),
        compiler_params=_cp("arbitrary"),
    )(g)


def _adamw(w, g, m, v):
    shape = w.shape
    C = shape[-1]
    R = shape[-2]
    A = 1
    for s in shape[:-2]:
        A *= s
    br = R
    while br * C > 256 * 1024 and br % 16 == 0:
        br //= 2
    c1 = 1.0 / (1.0 - ADAM_B1 ** ADAM_STEP)
    c2 = 1.0 / (1.0 - ADAM_B2 ** ADAM_STEP)

    def body(w_ref, g_ref, m_ref, v_ref, d_ref, nm_ref, nv_ref):
        gv = g_ref[...]
        nm = ADAM_B1 * m_ref[...] + (1.0 - ADAM_B1) * gv
        nv = ADAM_B2 * v_ref[...] + (1.0 - ADAM_B2) * (gv * gv)
        nm_ref[...] = nm
        nv_ref[...] = nv
        d_ref[...] = -ADAM_LR * ((nm * c1) / (jnp.sqrt(nv * c2) + ADAM_EPS) + ADAM_WD * w_ref[...])

    blk = pl.BlockSpec((None, br, C), lambda a, i: (a, i, 0))
    outs = pl.pallas_call(
        body, name="adamw", grid=(A, R // br), in_specs=[blk] * 4, out_specs=[blk] * 3,
        out_shape=[jax.ShapeDtypeStruct((A, R, C), F32)] * 3,
        compiler_params=_cp("arbitrary", "arbitrary"),
    )(*(t.reshape(A, R, C) for t in (w, g, m, v)))
    return tuple(o.reshape(shape) for o in outs)


WEIGHTS = ("norm_mix_pre", "norm_mix_post", "norm_ffn_pre", "norm_ffn_post", "w_in", "conv_a_w", "conf_dw_w",
           "conf_dw_b", "conf_ln_g", "conf_ln_b", "sgu_ln_g", "sgu_ln_b", "sgu_ws", "sgu_b", "w_branch", "w_out",
           "w_ff1", "w_ff2")
BIG = ("w_in", "w_branch", "w_out", "w_ff1", "w_ff2")
CONV_ROWS = 48


def kernel(x, norm_mix_pre, norm_mix_post, norm_ffn_pre, norm_ffn_post, w_in, conv_a_w, conf_dw_w, conf_dw_b, conf_ln_g, conf_ln_b, sgu_ln_g, sgu_ln_b, sgu_ws, sgu_b, w_branch, w_out, w_ff1, w_ff2, loss_target, m_norm_mix_pre, m_norm_mix_post, m_norm_ffn_pre, m_norm_ffn_post, m_w_in, m_conv_a_w, m_conf_dw_w, m_conf_dw_b, m_conf_ln_g, m_conf_ln_b, m_sgu_ln_g, m_sgu_ln_b, m_sgu_ws, m_sgu_b, m_w_branch, m_w_out, m_w_ff1, m_w_ff2, v_norm_mix_pre, v_norm_mix_post, v_norm_ffn_pre, v_norm_ffn_post, v_w_in, v_conv_a_w, v_conf_dw_w, v_conf_dw_b, v_conf_ln_g, v_conf_ln_b, v_sgu_ln_g, v_sgu_ln_b, v_sgu_ws, v_sgu_b, v_w_branch, v_w_out, v_w_ff1, v_w_ff2):
    w = dict(norm_mix_pre=norm_mix_pre, norm_mix_post=norm_mix_post, norm_ffn_pre=norm_ffn_pre,
             norm_ffn_post=norm_ffn_post, w_in=w_in, conv_a_w=conv_a_w, conf_dw_w=conf_dw_w, conf_dw_b=conf_dw_b,
             conf_ln_g=conf_ln_g, conf_ln_b=conf_ln_b, sgu_ln_g=sgu_ln_g, sgu_ln_b=sgu_ln_b, sgu_ws=sgu_ws,
             sgu_b=sgu_b, w_branch=w_branch, w_out=w_out, w_ff1=w_ff1, w_ff2=w_ff2)
    mom = dict(norm_mix_pre=m_norm_mix_pre, norm_mix_post=m_norm_mix_post, norm_ffn_pre=m_norm_ffn_pre,
               norm_ffn_post=m_norm_ffn_post, w_in=m_w_in, conv_a_w=m_conv_a_w, conf_dw_w=m_conf_dw_w,
               conf_dw_b=m_conf_dw_b, conf_ln_g=m_conf_ln_g, conf_ln_b=m_conf_ln_b, sgu_ln_g=m_sgu_ln_g,
               sgu_ln_b=m_sgu_ln_b, sgu_ws=m_sgu_ws, sgu_b=m_sgu_b, w_branch=m_w_branch, w_out=m_w_out,
               w_ff1=m_w_ff1, w_ff2=m_w_ff2)
    var = dict(norm_mix_pre=v_norm_mix_pre, norm_mix_post=v_norm_mix_post, norm_ffn_pre=v_norm_ffn_pre,
               norm_ffn_post=v_norm_ffn_post, w_in=v_w_in, conv_a_w=v_conv_a_w, conf_dw_w=v_conf_dw_w,
               conf_dw_b=v_conf_dw_b, conf_ln_g=v_conf_ln_g, conf_ln_b=v_conf_ln_b, sgu_ln_g=v_sgu_ln_g,
               sgu_ln_b=v_sgu_ln_b, sgu_ws=v_sgu_ws, sgu_b=v_sgu_b, w_branch=v_w_branch, w_out=v_w_out,
               w_ff1=v_w_ff1, w_ff2=v_w_ff2)
    L = w_in.shape[0]
    nseq, S, _ = x.shape
    T = nseq * S
    rk = D // NSH
    mx, my, mc = _me()
    k_chip = 2 * mx + my

    big_src = [w_in.reshape(L, 1, D, w_in.shape[2]), w_branch, w_out.reshape(L, 1, rk, D),
               w_ff1.reshape(L, 1, D, w_ff1.shape[2]), w_ff2.reshape(L, 1, w_ff2.shape[1], D)]
    kidx = jnp.reshape(k_chip, (1,)).astype(jnp.int32)
    conv_src = jnp.concatenate(
        [jnp.pad(conv_a_w, ((0, 0), (0, SUBLANES - KA), (0, 0))), jnp.pad(conf_dw_w, ((0, 0), (0, 1), (0, 0))),
         jnp.zeros((L, CONV_ROWS - SUBLANES - KC - 1, rk), F32)], axis=1)[None]
    (conv_g,) = _gather_weights([conv_src], 0)
    conv_full = conv_g.reshape(NSH, L, CONV_ROWS, rk).transpose(1, 2, 0, 3).reshape(L, CONV_ROWS, D)

    def layer_params(l, gathered):
        g_in, g_br, g_out, g_ff1, g_ff2 = gathered
        return dict(
            g_mix_pre=norm_mix_pre[l][None], g_mix_post=norm_mix_post[l][None], g_ffn_pre=norm_ffn_pre[l][None],
            g_ffn_post=norm_ffn_post[l][None], w_in=g_in, conv_a_w=conv_full[l, :KA],
            conf_dw_w=conv_full[l, SUBLANES:SUBLANES + KC], conf_dw_b=conf_dw_b[l][None],
            conf_ln_g=conf_ln_g[l][None], conf_ln_b=conf_ln_b[l][None], sgu_ln_g=sgu_ln_g[l][None],
            sgu_ln_b=sgu_ln_b[l][None], sgu_ws=sgu_ws[l], sgu_wst=jnp.swapaxes(sgu_ws[l], 1, 2),
            sgu_bt=sgu_b[l].T, w_branch=g_br.reshape(NSH, 3, rk, D), w_out=g_out.reshape(D, D), w_ff1=g_ff1,
            w_ff2=g_ff2.reshape(NSH * w_ff2.shape[1], D))

    def gather_start(l, dep):
        lands = [_cast_into(s, lax.empty((NSH * s.shape[1],) + s.shape[2:], BF), l, kidx, dep) for s in big_src]
        return _split_call(f"gather_start_{l}", _GatherCopies(len(lands)), [], lands)

    xt = x.reshape(T, D)
    layers, saved = [], []
    flight = gather_start(0, kidx)
    after = xt
    for l in range(L):
        ssem, rsem, _, lands, _ = flight
        _, lands = _split_call(f"gather_wait_{l}", _GatherCopies(len(lands)), [], lands, (ssem, rsem), after)
        gathered = _gather_finish(lands)
        if l + 1 < L:
            flight = gather_start(l + 1, gathered[0])
            dep = flight[4]
        else:
            dep = jnp.zeros((8, 128), F32)
        p = layer_params(l, gathered)
        xt, sv = _layer_fwd(xt, p, S, dep)
        layers.append(p)
        saved.append(sv)
        after = xt
    dx, loss_row = _loss_head(xt, loss_target.reshape(T, D))
    loss = lax.psum(loss_row[0, 0], ("x", "y", "c"))

    c_arr = jnp.reshape(mc, (1,)).astype(jnp.int32)
    idx = jnp.stack([k_chip, k_chip ^ 2, k_chip ^ 1, k_chip ^ 3, mc]).astype(jnp.int32)
    fulls = [None] * len(BIG)
    smalls = [None] * L
    flight = None
    dep = jnp.zeros((8, 128), F32)

    def land_scatter(l, fl, after, fulls):
        ssem, rsem, sums, rcv, _ = fl
        sums, rcv = _split_call(f"scatter_wait_{l}", _ScatterCopies(len(sums)), sums, rcv, (ssem, rsem), after)
        return [_sum_chips(o, r, f, l, L, idx) for o, r, f in zip(sums, rcv, fulls)]

    for l in reversed(range(L)):
        dx, big, small = _layer_bwd(dx, layers[l], saved[l], S, dep)
        smalls[l] = _pack_small(small)
        if flight is not None:
            fulls = land_scatter(l + 1, flight, dx, fulls)
        parts = [big["w_in"], big["w_branch"].reshape(NSH * 3, rk, D), big["w_out"], big["w_ff1"], big["w_ff2"]]
        sib = _send_halves_to_sibling(parts)
        sums = [_pair_add(p, s, c_arr) for p, s in zip(parts, sib)]
        rcv = [lax.empty(s.shape, s.dtype) for s in sums]
        flight = _split_call(f"scatter_start_{l}", _ScatterCopies(len(sums)), sums, rcv)
        dep = flight[4]
    fulls = land_scatter(0, flight, dx, fulls)
    full = _join_halves(fulls)
    grads = {n: f.reshape(w[n].shape) for n, f in zip(BIG, full)}

    packed = jnp.concatenate(smalls, axis=0)
    small_sum = _sum_devices(_gather_all(packed)).reshape(L, PACK_ROWS, D)
    shapes = {n: (w[n].shape[1:] if n not in ("conv_a_w", "conf_dw_w") else (w[n].shape[1], D)) for n in SMALL_NAMES}
    sg = _unpack_small(small_sum, shapes)
    for n in SMALL_NAMES:
        if n in ("conv_a_w", "conf_dw_w"):
            grads[n] = lax.dynamic_slice_in_dim(sg[n], k_chip * rk, rk, axis=2)
        else:
            grads[n] = sg[n]

    delta, new_m, new_v = {}, {}, {}
    for n in BIG:
        delta[n], new_m[n], new_v[n] = _adamw(w[n], grads[n], mom[n], var[n])
    for n in SMALL_NAMES:
        sh = w[n].shape
        flat = (sh[0] * sh[1], sh[2]) if n in ("conv_a_w", "conf_dw_w") else (-1, D)
        d, nm, nv = _adamw(*(t.reshape(flat) for t in (w[n], grads[n], mom[n], var[n])))
        delta[n], new_m[n], new_v[n] = d.reshape(sh), nm.reshape(sh), nv.reshape(sh)

    return (loss, dx.reshape(x.shape), *[grads[n] for n in WEIGHTS], *[delta[n] for n in WEIGHTS],
            *[new_m[n] for n in WEIGHTS], *[new_v[n] for n in WEIGHTS])
```

```python
import functools

import jax
import jax.numpy as jnp
from jax import lax
from jax.experimental import pallas as pl
from jax.experimental.pallas import tpu as pltpu

D = 1024
HEADS = 8
CHUNK = 128
KA = 3
KC = 31
HALO = 32
NSH = 4
NDEV = 8
EPS = 1e-6
BF = jnp.bfloat16
F32 = jnp.float32
VMEM_LIMIT = 56 * 1024 * 1024

ADAM_LR = 0.001
ADAM_B1 = 0.9
ADAM_B2 = 0.999
ADAM_EPS = 1e-08
ADAM_WD = 0.01
ADAM_STEP = 10

MESH = pl.DeviceIdType.MESH
ANY = pl.BlockSpec(memory_space=pl.ANY)


def _cp(*sem):
    return pltpu.CompilerParams(dimension_semantics=sem, vmem_limit_bytes=VMEM_LIMIT)


def _sig(x):
    return 1.0 / (1.0 + jnp.exp(-x))


_GC = 0.7978845608028654


def _gelu(x):
    x2 = x * x
    t = jnp.tanh(_GC * x * (1.0 + 0.044715 * x2))
    y = 0.5 * x * (1.0 + t)
    dy = 0.5 * (1.0 + t) + 0.5 * x * (1.0 - t * t) * _GC * (1.0 + 3.0 * 0.044715 * x2)
    return y, dy


def _rms_fwd(x, g):
    r = lax.rsqrt(jnp.mean(x * x, axis=-1, keepdims=True) + EPS)
    return x * r * g


def _rms_bwd(dy, x, g):
    r = lax.rsqrt(jnp.mean(x * x, axis=-1, keepdims=True) + EPS)
    xn = x * r
    dyg = dy * g
    dx = r * (dyg - xn * jnp.mean(dyg * xn, axis=-1, keepdims=True))
    return dx, jnp.sum(dy * xn, axis=0, keepdims=True)


def _ln_stats(x):
    mu = jnp.mean(x, axis=-1, keepdims=True)
    xc = x - mu
    r = lax.rsqrt(jnp.mean(xc * xc, axis=-1, keepdims=True) + EPS)
    return xc * r, r


def _ln_bwd(dn, n, r):
    return r * (dn - jnp.mean(dn, axis=-1, keepdims=True) - n * jnp.mean(dn * n, axis=-1, keepdims=True))


def _dot(a, b):
    return jnp.dot(a, b, preferred_element_type=F32)


def _dot_nt(a, b):
    return lax.dot_general(a, b, (((1,), (1,)), ((), ())), preferred_element_type=F32)


def _dot_tn(a, b):
    return lax.dot_general(a, b, (((0,), (0,)), ((), ())), preferred_element_type=F32)


def _in_proj(x, g, w, dep):
    T = x.shape[0]
    nc = w.shape[2]
    tm = min(T, 1024)
    tn = 1280
    nj = nc // tn

    def body(x_ref, g_ref, w_ref, dep_ref, h_ref, z_ref, h_scr):
        @pl.when((pl.program_id(1) == 0) & (pl.program_id(2) == 0))
        def _():
            h = _rms_fwd(x_ref[...], g_ref[...]).astype(BF)
            h_scr[...] = h
            h_ref[...] = h
        z_ref[...] = _dot(h_scr[...], w_ref[...]).astype(BF)

    return pl.pallas_call(
        body, name="in_proj", grid=(T // tm, NSH, nj),
        in_specs=[pl.BlockSpec((tm, D), lambda i, k, j: (i, 0)),
                  pl.BlockSpec((1, D), lambda i, k, j: (0, 0)),
                  pl.BlockSpec((None, D, tn), lambda i, k, j: (k, 0, j)), ANY],
        out_specs=[pl.BlockSpec((tm, D), lambda i, k, j: (i, 0)),
                   pl.BlockSpec((tm, tn), lambda i, k, j: (i, k * nj + j))],
        out_shape=[jax.ShapeDtypeStruct((T, D), BF), jax.ShapeDtypeStruct((T, NSH * nc), BF)],
        scratch_shapes=[pltpu.VMEM((tm, D), BF)],
        compiler_params=_cp("arbitrary", "arbitrary", "arbitrary"),
    )(x, g, w, dep)


def _tile_specs(tt, nt_total, reverse):
    def tile(i):
        return (nt_total - 1 - i) if reverse else i

    def cur(c):
        return pl.BlockSpec((tt, D), lambda i, *_: (tile(i), c))

    def halo(c):
        return pl.BlockSpec((HALO, D), lambda i, *_: (jnp.maximum(tile(i) * (tt // HALO) - 1, 0), c))

    def row(r=1):
        return pl.BlockSpec((r, D), lambda i, *_: (0, 0))

    return tile, cur, halo, row


def _causal_conv(ext, w_ref, ntap, tt):
    acc = None
    for k in range(ntap):
        term = w_ref[k:k + 1, :] * ext[pl.ds(HALO - (ntap - 1) + k, tt), :]
        acc = term if acc is None else acc + term
    return acc


def _anticausal_conv(ext, w_ref, ntap, tt):
    acc = None
    for k in range(ntap):
        term = w_ref[k:k + 1, :] * ext[pl.ds(ntap - 1 - k, tt), :]
        acc = term if acc is None else acc + term
    return acc


def _conv_wgrad(dw_ref, dout, ext, ntap, tt):
    for k in range(ntap):
        dw_ref[k:k + 1, :] += jnp.sum(dout * ext[pl.ds(HALO - (ntap - 1) + k, tt), :], axis=0, keepdims=True)


RC = 16


def _chunks(tt, fn, group=2):
    def step(c, carry):
        for u in range(group):
            fn(pl.multiple_of((c * group + u) * RC, RC))
        return carry
    lax.fori_loop(0, tt // (RC * group), step, 0)


def _shifted_copies(ext, sh, nrows):
    for s in range(SUBLANES):
        sh[s] = ext[pl.ds(s, nrows), :]


def _window(sh, o, r0):
    return sh[o % SUBLANES, pl.ds(r0 + (o // SUBLANES) * SUBLANES, RC), :]


def _fill_taps(wb, w_ref, ntap):
    for k in range(ntap):
        wb[k * SUBLANES:(k + 1) * SUBLANES, :] = jnp.broadcast_to(w_ref[k:k + 1, :], (SUBLANES, D))


def _conv_chunk(sh, wb, offs, r0):
    acc = None
    for k, o in enumerate(offs):
        wk = wb[k * SUBLANES:(k + 1) * SUBLANES, :]
        term = jnp.concatenate([wk] * (RC // SUBLANES), axis=0) * _window(sh, o, r0)
        acc = term if acc is None else acc + term
    return acc


def _conv_wgrad_chunked(dw_ref, d_ref, sh, offs, tt):
    group = 4
    for k, o in enumerate(offs):
        def step(c, acc, o=o):
            prods = []
            for u in range(group):
                r0 = pl.multiple_of((c * group + u) * RC, RC)
                prods.append(d_ref[pl.ds(r0, RC), :] * _window(sh, o, r0))
            return acc + ((prods[0] + prods[1]) + (prods[2] + prods[3]))
        acc = lax.fori_loop(0, tt // (RC * group), step, jnp.zeros((RC, D), F32))
        dw_ref[k:k + 1, :] += jnp.sum(acc, axis=0, keepdims=True)


def _causal_offsets(ntap):
    return [HALO - (ntap - 1) + k for k in range(ntap)]


def _anticausal_offsets(ntap):
    return [ntap - 1 - k for k in range(ntap)]


def _mix_a_fwd(z, wa, S):
    T = z.shape[0]
    tt = min(S, 256)
    nt = S // tt
    _, cur, halo, row = _tile_specs(tt, T // tt, False)

    def body(ah, ab, ac, ah_h, ac_h, w_ref, y_ref, ext):
        first = (pl.program_id(0) % nt) == 0
        ph = ah_h[...].astype(F32) * ac_h[...].astype(F32)
        ext[0:HALO, :] = jnp.where(first, 0.0, ph)
        ext[HALO:, :] = ah[...].astype(F32) * ac[...].astype(F32)
        q = _causal_conv(ext, w_ref, KA, tt)
        y_ref[...] = (ab[...].astype(F32) * q).astype(BF)

    return pl.pallas_call(
        body, name="mix_a_fwd", grid=(T // tt,),
        in_specs=[cur(0), cur(1), cur(2), halo(0), halo(2), row(KA)],
        out_specs=pl.BlockSpec((tt, D), lambda i: (i, 0)),
        out_shape=jax.ShapeDtypeStruct((T, D), BF),
        scratch_shapes=[pltpu.VMEM((HALO + tt, D), F32)],
        compiler_params=_cp("arbitrary"),
    )(z, z, z, z, z, wa)


def _mix_b_fwd(z, wc, bc, lg, lb, S):
    T = z.shape[0]
    tt = min(S, 256)
    nt = S // tt
    _, cur, halo, row = _tile_specs(tt, T // tt, False)

    nrows = HALO + tt
    offs = _causal_offsets(KC)

    def body(ca, cg, ca_h, cg_h, w_ref, bc_ref, lg_ref, lb_ref, y_ref, s_ref, ext, sh, wb):
        @pl.when(pl.program_id(0) == 0)
        def _():
            _fill_taps(wb, w_ref, KC)
            ext[nrows:, :] = jnp.zeros((SUBLANES, D), F32)

        first = (pl.program_id(0) % nt) == 0
        rh = ca_h[...].astype(F32) * _sig(cg_h[...].astype(F32))
        ext[0:HALO, :] = jnp.where(first, 0.0, rh)

        def glu(r0):
            rows = pl.ds(r0, RC)
            ext[pl.ds(HALO + r0, RC), :] = ca[rows, :].astype(F32) * _sig(cg[rows, :].astype(F32))
        _chunks(tt, glu)
        _shifted_copies(ext, sh, nrows)

        def conv(r0):
            rows = pl.ds(r0, RC)
            s = _conv_chunk(sh, wb, offs, r0) + bc_ref[...]
            s_ref[rows, :] = s.astype(BF)
            n, _ = _ln_stats(s)
            t = n * lg_ref[...] + lb_ref[...]
            y_ref[rows, :] = (t * _sig(t)).astype(BF)
        _chunks(tt, conv)

    return pl.pallas_call(
        body, name="mix_b_fwd", grid=(T // tt,),
        in_specs=[cur(3), cur(4), halo(3), halo(4), row(KC), row(), row(), row()],
        out_specs=[pl.BlockSpec((tt, D), lambda i: (i, 0))] * 2,
        out_shape=[jax.ShapeDtypeStruct((T, D), BF)] * 2,
        scratch_shapes=[pltpu.VMEM((nrows + SUBLANES, D), F32), pltpu.VMEM((SUBLANES, nrows, D), F32),
                        pltpu.VMEM((KC * SUBLANES, D), F32)],
        compiler_params=_cp("arbitrary"),
    )(z, z, z, z, wc, bc, lg, lb)


def _causal_mask(transposed):
    r = lax.broadcasted_iota(jnp.int32, (CHUNK, CHUNK), 0)
    c = lax.broadcasted_iota(jnp.int32, (CHUNK, CHUNK), 1)
    return (c >= r) if transposed else (r >= c)


def _mix_s_fwd(z, lg, lb, ws, bst, S):
    T = z.shape[0]
    tt = min(S, 256)
    _, cur, _, row = _tile_specs(tt, T // tt, False)

    def body(su, sv, lg_ref, lb_ref, ws_ref, bst_ref, y_ref, u_scr, vn_scr):
        u_scr[...] = _gelu(su[...].astype(F32))[0]
        n, _ = _ln_stats(_gelu(sv[...].astype(F32))[0])
        vn_scr[...] = (n * lg_ref[...] + lb_ref[...]).astype(BF)
        mask = _causal_mask(False)
        for h in range(HEADS):
            wm = jnp.where(mask, ws_ref[h], 0.0).astype(BF)
            cols = slice(h * CHUNK, (h + 1) * CHUNK)
            for c in range(tt // CHUNK):
                rows = slice(c * CHUNK, (c + 1) * CHUNK)
                mixed = _dot(wm, vn_scr[rows, cols]) + bst_ref[:, h:h + 1]
                y_ref[rows, cols] = (u_scr[rows, cols] * mixed).astype(BF)

    return pl.pallas_call(
        body, name="mix_s_fwd", grid=(T // tt,),
        in_specs=[cur(5), cur(6), row(), row(),
                  pl.BlockSpec((HEADS, CHUNK, CHUNK), lambda i: (0, 0, 0)),
                  pl.BlockSpec((CHUNK, HEADS), lambda i: (0, 0))],
        out_specs=pl.BlockSpec((tt, D), lambda i: (i, 0)),
        out_shape=jax.ShapeDtypeStruct((T, D), BF),
        scratch_shapes=[pltpu.VMEM((tt, D), F32), pltpu.VMEM((tt, D), BF)],
        compiler_params=_cp("arbitrary"),
    )(z, z, lg, lb, ws, bst)


def _mix_out_fwd(ya, yc, ys, z, x, wb, wo, gp):
    T = x.shape[0]
    tm = min(T, 256)
    rk = D // NSH

    def body(ya_ref, yc_ref, ys_ref, ga, gc, gs, x_ref, wb_ref, wo_ref, gp_ref, p_ref, mg_ref, m_ref, x1_ref):
        acc = None
        for b, (y_ref, g_ref) in enumerate(((ya_ref, ga), (yc_ref, gc), (ys_ref, gs))):
            pb = None
            for k in range(NSH):
                part = _dot(y_ref[:, k * rk:(k + 1) * rk], wb_ref[k, b])
                pb = part if pb is None else pb + part
            p_ref[b] = pb.astype(BF)
            term = _sig(g_ref[...].astype(F32)) * pb
            acc = term if acc is None else acc + term
        mg = acc.astype(BF)
        mg_ref[...] = mg
        m = _dot(mg, wo_ref[...])
        m_ref[...] = m.astype(BF)
        x1_ref[...] = x_ref[...] + _rms_fwd(m, gp_ref[...])

    rowblk = pl.BlockSpec((tm, D), lambda i: (i, 0))
    return pl.pallas_call(
        body, name="mix_out_fwd", grid=(T // tm,),
        in_specs=[rowblk, rowblk, rowblk,
                  pl.BlockSpec((tm, D), lambda i: (i, 7)), pl.BlockSpec((tm, D), lambda i: (i, 8)),
                  pl.BlockSpec((tm, D), lambda i: (i, 9)), rowblk,
                  pl.BlockSpec((NSH, 3, rk, D), lambda i: (0, 0, 0, 0)),
                  pl.BlockSpec((D, D), lambda i: (0, 0)),
                  pl.BlockSpec((1, D), lambda i: (0, 0))],
        out_specs=[pl.BlockSpec((3, tm, D), lambda i: (0, i, 0)), rowblk, rowblk, rowblk],
        out_shape=[jax.ShapeDtypeStruct((3, T, D), BF), jax.ShapeDtypeStruct((T, D), BF),
                   jax.ShapeDtypeStruct((T, D), BF), jax.ShapeDtypeStruct((T, D), F32)],
        compiler_params=_cp("arbitrary"),
    )(ya, yc, ys, z, z, z, x, wb, wo, gp)


def _ffn_fwd(x1, g3, w1, w2, g4):
    T = x1.shape[0]
    tm = min(T, 512)

    def body(x_ref, g3_ref, w1_ref, w2_ref, g4_ref, h_ref, a_ref, f_ref, x2_ref, h_scr, acc):
        k = pl.program_id(1)

        @pl.when(k == 0)
        def _():
            h = _rms_fwd(x_ref[...], g3_ref[...]).astype(BF)
            h_scr[...] = h
            h_ref[...] = h
            acc[...] = jnp.zeros_like(acc)

        a = _dot(h_scr[...], w1_ref[...])
        a_ref[...] = a.astype(BF)
        r = jnp.maximum(a, 0.0)
        acc[...] += _dot((r * r).astype(BF), w2_ref[...])

        @pl.when(k == NSH - 1)
        def _():
            f = acc[...]
            f_ref[...] = f.astype(BF)
            x2_ref[...] = x_ref[...] + _rms_fwd(f, g4_ref[...])

    rowblk = pl.BlockSpec((tm, D), lambda i, k: (i, 0))
    vec = pl.BlockSpec((1, D), lambda i, k: (0, 0))
    return pl.pallas_call(
        body, name="ffn_fwd", grid=(T // tm, NSH),
        in_specs=[rowblk, vec, pl.BlockSpec((None, D, D), lambda i, k: (k, 0, 0)),
                  pl.BlockSpec((D, D), lambda i, k: (k, 0)), vec],
        out_specs=[rowblk, pl.BlockSpec((tm, D), lambda i, k: (i, k)), rowblk, rowblk],
        out_shape=[jax.ShapeDtypeStruct((T, D), BF), jax.ShapeDtypeStruct((T, NSH * D), BF),
                   jax.ShapeDtypeStruct((T, D), BF), jax.ShapeDtypeStruct((T, D), F32)],
        scratch_shapes=[pltpu.VMEM((tm, D), BF), pltpu.VMEM((tm, D), F32)],
        compiler_params=_cp("arbitrary", "arbitrary"),
    )(x1, g3, w1, w2, g4)


def _loss_head(y, target):
    T = y.shape[0]
    tm = min(T, 512)

    def body(y_ref, t_ref, dy_ref, l_ref):
        @pl.when(pl.program_id(0) == 0)
        def _():
            l_ref[...] = jnp.zeros_like(l_ref)
        e = y_ref[...] - t_ref[...]
        dy_ref[...] = e * (1.0 / D)
        l_ref[...] += jnp.sum(e * e) * (0.5 / D)

    rowblk = pl.BlockSpec((tm, D), lambda i: (i, 0))
    return pl.pallas_call(
        body, name="loss_head", grid=(T // tm,),
        in_specs=[rowblk, rowblk],
        out_specs=[rowblk, pl.BlockSpec((1, 128), lambda i: (0, 0))],
        out_shape=[jax.ShapeDtypeStruct((T, D), F32), jax.ShapeDtypeStruct((1, 128), F32)],
        compiler_params=_cp("arbitrary"),
    )(y, target)


def _ffn_bwd(dx2, f, g4, a, w2, w1, x1, g3, dep):
    T = dx2.shape[0]
    tm = min(T, 512)

    def body(dx2_ref, f_ref, g4_ref, a_ref, w2_ref, w1_ref, x1_ref, g3_ref, dep_ref,
             df_ref, da_ref, dx1_ref, dg4_ref, dg3_ref, df_scr, acc):
        i, k = pl.program_id(0), pl.program_id(1)

        @pl.when((i == 0) & (k == 0))
        def _():
            dg4_ref[...] = jnp.zeros_like(dg4_ref)
            dg3_ref[...] = jnp.zeros_like(dg3_ref)

        @pl.when(k == 0)
        def _():
            df, dg = _rms_bwd(dx2_ref[...], f_ref[...].astype(F32), g4_ref[...])
            dg4_ref[...] += dg
            dfb = df.astype(BF)
            df_scr[...] = dfb
            df_ref[...] = dfb
            acc[...] = jnp.zeros_like(acc)

        av = a_ref[...].astype(F32)
        da = (_dot_nt(df_scr[...], w2_ref[...]) * (2.0 * jnp.maximum(av, 0.0))).astype(BF)
        da_ref[...] = da
        acc[...] += _dot_nt(da, w1_ref[...])

        @pl.when(k == NSH - 1)
        def _():
            dx, dg = _rms_bwd(acc[...], x1_ref[...], g3_ref[...])
            dg3_ref[...] += dg
            dx1_ref[...] = dx2_ref[...] + dx

    rowblk = pl.BlockSpec((tm, D), lambda i, k: (i, 0))
    vec = pl.BlockSpec((1, D), lambda i, k: (0, 0))
    return pl.pallas_call(
        body, name="ffn_bwd", grid=(T // tm, NSH),
        in_specs=[rowblk, rowblk, vec, pl.BlockSpec((tm, D), lambda i, k: (i, k)),
                  pl.BlockSpec((D, D), lambda i, k: (k, 0)),
                  pl.BlockSpec((None, D, D), lambda i, k: (k, 0, 0)), rowblk, vec, ANY],
        out_specs=[rowblk, pl.BlockSpec((tm, D), lambda i, k: (i, k)), rowblk, vec, vec],
        out_shape=[jax.ShapeDtypeStruct((T, D), BF), jax.ShapeDtypeStruct((T, NSH * D), BF),
                   jax.ShapeDtypeStruct((T, D), F32), jax.ShapeDtypeStruct((1, D), F32),
                   jax.ShapeDtypeStruct((1, D), F32)],
        scratch_shapes=[pltpu.VMEM((tm, D), BF), pltpu.VMEM((tm, D), F32)],
        compiler_params=_cp("arbitrary", "arbitrary"),
    )(dx2, f, g4, a, w2, w1, x1, g3, dep)


def _wgrad(name, ops, grid, in_specs, out_spec, out_shape, acc_shape, pick=None, relu2=False):
    nt = grid[-1]
    na = len(ops) - 1

    def body(*refs):
        a_refs, b_ref, o_ref, acc = refs[:na], refs[na], refs[na + 1], refs[na + 2]
        t = pl.program_id(len(grid) - 1)

        @pl.when(t == 0)
        def _():
            acc[...] = jnp.zeros_like(acc)

        def add(a_ref):
            av = a_ref[...]
            if relu2:
                r = jnp.maximum(av.astype(F32), 0.0)
                av = (r * r).astype(BF)
            acc[...] += _dot_tn(av, b_ref[...])

        if na == 1:
            add(a_refs[0])
        else:
            sel = pick()
            for n in range(na):
                pl.when(sel == n)(functools.partial(add, a_refs[n]))

        @pl.when(t == nt - 1)
        def _():
            if len(o_ref.shape) == 2:
                o_ref[...] = acc[...].astype(o_ref.dtype)
            else:
                rs = o_ref.shape[1]
                for q in range(o_ref.shape[0]):
                    o_ref[q] = acc[q * rs:(q + 1) * rs, :].astype(o_ref.dtype)

    return pl.pallas_call(
        body, name=name, grid=grid, in_specs=in_specs, out_specs=out_spec, out_shape=out_shape,
        scratch_shapes=[pltpu.VMEM(acc_shape, F32)],
        compiler_params=_cp(*(["arbitrary"] * len(grid))),
    )(*ops)


def _mix_out_bwd(dx1, m, gp, wo, p3, z, wb):
    T = dx1.shape[0]
    tm = min(T, 512)
    rk = D // NSH

    def body(dx1_ref, m_ref, gp_ref, wo_ref, p_ref, g_ref, wb_ref,
             dm_ref, dp_ref, dy_ref, dz_ref, dgp_ref, dmg):
        i, b = pl.program_id(0), pl.program_id(1)

        @pl.when((i == 0) & (b == 0))
        def _():
            dgp_ref[...] = jnp.zeros_like(dgp_ref)

        @pl.when(b == 0)
        def _():
            dm, dg = _rms_bwd(dx1_ref[...], m_ref[...].astype(F32), gp_ref[...])
            dgp_ref[...] += dg
            dmb = dm.astype(BF)
            dm_ref[...] = dmb
            dmg[...] = _dot_nt(dmb, wo_ref[...])

        gate = _sig(g_ref[...].astype(F32))
        d = dmg[...]
        dp = (d * gate).astype(BF)
        dp_ref[...] = dp
        dz_ref[...] = (d * p_ref[...].astype(F32) * gate * (1.0 - gate)).astype(BF)
        for k in range(NSH):
            dy_ref[:, k * rk:(k + 1) * rk] = _dot_nt(dp, wb_ref[k, b]).astype(BF)

    rowblk = pl.BlockSpec((tm, D), lambda i, b: (i, 0))
    br = pl.BlockSpec((None, tm, D), lambda i, b: (b, i, 0))
    vec = pl.BlockSpec((1, D), lambda i, b: (0, 0))
    return pl.pallas_call(
        body, name="mix_out_bwd", grid=(T // tm, 3),
        in_specs=[rowblk, rowblk, vec, pl.BlockSpec((D, D), lambda i, b: (0, 0)), br,
                  pl.BlockSpec((tm, D), lambda i, b: (i, 7 + b)),
                  pl.BlockSpec((NSH, 3, rk, D), lambda i, b: (0, 0, 0, 0))],
        out_specs=[rowblk, br, br, pl.BlockSpec((tm, D), lambda i, b: (i, 7 + b)), vec],
        out_shape=[jax.ShapeDtypeStruct((T, D), BF), jax.ShapeDtypeStruct((3, T, D), BF),
                   jax.ShapeDtypeStruct((3, T, D), BF), jax.ShapeDtypeStruct((T, 10 * D), BF),
                   jax.ShapeDtypeStruct((1, D), F32)],
        scratch_shapes=[pltpu.VMEM((tm, D), F32)],
        compiler_params=_cp("arbitrary", "arbitrary"),
    )(dx1, m, gp, wo, p3, z, wb)


def _mix_a_bwd(dz, dy3, z, wa, S):
    T = z.shape[0]
    tt = min(S, 256)
    nt = S // tt
    ntt = T // tt
    tile, cur, halo, row = _tile_specs(tt, ntt, True)

    def body(dz_in, dy_ref, ah, ab, ac, ah_h, ac_h, w_ref, dz_ref, dw_ref, ext_p, ext_d, stage):
        i, b = pl.program_id(0), pl.program_id(1)
        ti = ntt - 1 - i

        @pl.when((i == 0) & (b == 0))
        def _():
            dw_ref[...] = jnp.zeros_like(dw_ref)
            ext_d[...] = jnp.zeros_like(ext_d)

        @pl.when(b == 0)
        def _():
            first = (ti % nt) == 0
            last = (ti % nt) == nt - 1
            ahv, acv, abv = ah[...].astype(F32), ac[...].astype(F32), ab[...].astype(F32)
            ext_p[0:HALO, :] = jnp.where(first, 0.0, ah_h[...].astype(F32) * ac_h[...].astype(F32))
            ext_p[HALO:, :] = ahv * acv
            q = _causal_conv(ext_p, w_ref, KA, tt)
            dy = dy_ref[...].astype(F32)
            dq = dy * abv
            stage[1] = (dy * q).astype(BF)
            _conv_wgrad(dw_ref, dq, ext_p, KA, tt)
            ext_d[tt:, :] = jnp.where(last, 0.0, ext_d[0:HALO, :])
            ext_d[0:tt, :] = dq
            dp = _anticausal_conv(ext_d, w_ref, KA, tt)
            stage[0] = (dp * acv).astype(BF)
            stage[2] = (dp * ahv).astype(BF)

        dz_ref[...] = stage[b]

    return pl.pallas_call(
        body, name="mix_a_bwd", grid=(ntt, 3),
        in_specs=[ANY, pl.BlockSpec((None, tt, D), lambda i, b: (0, tile(i), 0)),
                  cur(0), cur(1), cur(2), halo(0), halo(2), row(KA)],
        out_specs=[pl.BlockSpec((tt, D), lambda i, b: (tile(i), b)), pl.BlockSpec((KA, D), lambda i, b: (0, 0))],
        out_shape=[jax.ShapeDtypeStruct(dz.shape, BF), jax.ShapeDtypeStruct((KA, D), F32)],
        scratch_shapes=[pltpu.VMEM((HALO + tt, D), F32), pltpu.VMEM((tt + HALO, D), F32),
                        pltpu.VMEM((3, tt, D), BF)],
        input_output_aliases={0: 0},
        compiler_params=_cp("arbitrary", "arbitrary"),
    )(dz, dy3, z, z, z, z, z, wa)


def _mix_b_bwd(dz, dy3, s, z, wc, lg, lb, S):
    T = z.shape[0]
    tt = min(S, 256)
    nt = S // tt
    ntt = T // tt
    tile, cur, halo, row = _tile_specs(tt, ntt, True)

    nrows = HALO + tt

    def body(dz_in, dy_ref, s_ref, ca, cg, ca_h, cg_h, w_ref, lg_ref, lb_ref,
             dz_ref, dw_ref, dbc_ref, dlg_ref, dlb_ref, ext_r, ext_d, sh, wb, accs, stage):
        i, b = pl.program_id(0), pl.program_id(1)
        ti = ntt - 1 - i

        @pl.when((i == 0) & (b == 0))
        def _():
            dw_ref[...] = jnp.zeros_like(dw_ref)
            dbc_ref[...] = jnp.zeros_like(dbc_ref)
            dlg_ref[...] = jnp.zeros_like(dlg_ref)
            dlb_ref[...] = jnp.zeros_like(dlb_ref)
            ext_d[...] = jnp.zeros_like(ext_d)
            ext_r[nrows:, :] = jnp.zeros((SUBLANES, D), F32)
            _fill_taps(wb, w_ref, KC)

        @pl.when(b == 0)
        def _():
            first = (ti % nt) == 0
            last = (ti % nt) == nt - 1
            ext_r[0:HALO, :] = jnp.where(first, 0.0, ca_h[...].astype(F32) * _sig(cg_h[...].astype(F32)))
            ext_d[tt:nrows, :] = jnp.where(last, 0.0, ext_d[0:HALO, :])
            accs[...] = jnp.zeros_like(accs)

            def point(r0):
                rows = pl.ds(r0, RC)
                n, r = _ln_stats(s_ref[rows, :].astype(F32))
                t = n * lg_ref[...] + lb_ref[...]
                sg = _sig(t)
                dt = dy_ref[rows, :].astype(F32) * (sg * (1.0 + t * (1.0 - sg)))
                accs[0] += dt * n
                accs[1] += dt
                ds = _ln_bwd(dt * lg_ref[...], n, r)
                accs[2] += ds
                ext_d[rows, :] = ds
                ext_r[pl.ds(HALO + r0, RC), :] = ca[rows, :].astype(F32) * _sig(cg[rows, :].astype(F32))
            _chunks(tt, point)
            dlg_ref[...] += jnp.sum(accs[0], axis=0, keepdims=True)
            dlb_ref[...] += jnp.sum(accs[1], axis=0, keepdims=True)
            dbc_ref[...] += jnp.sum(accs[2], axis=0, keepdims=True)

            _shifted_copies(ext_r, sh, nrows)
            _conv_wgrad_chunked(dw_ref, ext_d, sh, _causal_offsets(KC), tt)
            _shifted_copies(ext_d, sh, nrows)

            def conv(r0):
                rows = pl.ds(r0, RC)
                dr = _conv_chunk(sh, wb, _anticausal_offsets(KC), r0)
                cav = ca[rows, :].astype(F32)
                sgc = _sig(cg[rows, :].astype(F32))
                stage[0, rows, :] = (dr * sgc).astype(BF)
                stage[1, rows, :] = (dr * cav * sgc * (1.0 - sgc)).astype(BF)
            _chunks(tt, conv)

        dz_ref[...] = stage[b]

    vec = pl.BlockSpec((1, D), lambda i, b: (0, 0))
    return pl.pallas_call(
        body, name="mix_b_bwd", grid=(ntt, 2),
        in_specs=[ANY, pl.BlockSpec((None, tt, D), lambda i, b: (1, tile(i), 0)),
                  pl.BlockSpec((tt, D), lambda i, b: (tile(i), 0)),
                  cur(3), cur(4), halo(3), halo(4), row(KC), row(), row()],
        out_specs=[pl.BlockSpec((tt, D), lambda i, b: (tile(i), 3 + b)),
                   pl.BlockSpec((KC, D), lambda i, b: (0, 0)), vec, vec, vec],
        out_shape=[jax.ShapeDtypeStruct(dz.shape, BF), jax.ShapeDtypeStruct((KC, D), F32)]
        + [jax.ShapeDtypeStruct((1, D), F32)] * 3,
        scratch_shapes=[pltpu.VMEM((nrows + SUBLANES, D), F32), pltpu.VMEM((nrows + SUBLANES, D), F32),
                        pltpu.VMEM((SUBLANES, nrows, D), F32), pltpu.VMEM((KC * SUBLANES, D), F32),
                        pltpu.VMEM((3, RC, D), F32), pltpu.VMEM((2, tt, D), BF)],
        input_output_aliases={0: 0},
        compiler_params=_cp("arbitrary", "arbitrary"),
    )(dz, dy3, s, z, z, z, z, wc, lg, lb)


def _mix_s_bwd(dz, dy3, z, lg, lb, ws, wst, bst, S):
    T = z.shape[0]
    tt = min(S, 256)
    ntt = T // tt
    _, cur, _, row = _tile_specs(tt, ntt, False)

    def body(dz_in, dy_ref, su, sv, lg_ref, lb_ref, ws_ref, wst_ref, bst_ref,
             dz_ref, dws_ref, dbst_ref, dlg_ref, dlb_ref, u_scr, vn_scr, dvn_scr, stage):
        i, b = pl.program_id(0), pl.program_id(1)

        @pl.when((i == 0) & (b == 0))
        def _():
            dws_ref[...] = jnp.zeros_like(dws_ref)
            dbst_ref[...] = jnp.zeros_like(dbst_ref)
            dlg_ref[...] = jnp.zeros_like(dlg_ref)
            dlb_ref[...] = jnp.zeros_like(dlb_ref)

        @pl.when(b == 0)
        def _():
            u, du_dx = _gelu(su[...].astype(F32))
            v, dv_dx = _gelu(sv[...].astype(F32))
            u_scr[...] = u
            n, r = _ln_stats(v)
            vn_scr[...] = (n * lg_ref[...] + lb_ref[...]).astype(BF)
            mask = _causal_mask(False)
            mask_t = _causal_mask(True)
            for h in range(HEADS):
                wm = jnp.where(mask, ws_ref[h], 0.0).astype(BF)
                wmt = jnp.where(mask_t, wst_ref[h], 0.0).astype(BF)
                cols = slice(h * CHUNK, (h + 1) * CHUNK)
                for c in range(tt // CHUNK):
                    rows = slice(c * CHUNK, (c + 1) * CHUNK)
                    vb = vn_scr[rows, cols]
                    mixed = _dot(wm, vb) + bst_ref[:, h:h + 1]
                    dy = dy_ref[rows, cols].astype(F32)
                    dmix = dy * u_scr[rows, cols]
                    u_scr[rows, cols] = dy * mixed
                    dbst_ref[:, h:h + 1] += jnp.sum(dmix, axis=1, keepdims=True)
                    dmb = dmix.astype(BF)
                    dws_ref[h] += _dot_nt(dmb, vb)
                    dvn_scr[rows, cols] = _dot(wmt, dmb)
            stage[0] = (u_scr[...] * du_dx).astype(BF)
            dvn = dvn_scr[...]
            dlg_ref[...] += jnp.sum(dvn * n, axis=0, keepdims=True)
            dlb_ref[...] += jnp.sum(dvn, axis=0, keepdims=True)
            stage[1] = (_ln_bwd(dvn * lg_ref[...], n, r) * dv_dx).astype(BF)

        dz_ref[...] = stage[b]

    vec = pl.BlockSpec((1, D), lambda i, b: (0, 0))
    wsp = pl.BlockSpec((HEADS, CHUNK, CHUNK), lambda i, b: (0, 0, 0))
    bsp = pl.BlockSpec((CHUNK, HEADS), lambda i, b: (0, 0))
    return pl.pallas_call(
        body, name="mix_s_bwd", grid=(ntt, 2),
        in_specs=[ANY, pl.BlockSpec((None, tt, D), lambda i, b: (2, i, 0)),
                  cur(5), cur(6), row(), row(), wsp, wsp, bsp],
        out_specs=[pl.BlockSpec((tt, D), lambda i, b: (i, 5 + b)), wsp, bsp, vec, vec],
        out_shape=[jax.ShapeDtypeStruct(dz.shape, BF), jax.ShapeDtypeStruct((HEADS, CHUNK, CHUNK), F32),
                   jax.ShapeDtypeStruct((CHUNK, HEADS), F32), jax.ShapeDtypeStruct((1, D), F32),
                   jax.ShapeDtypeStruct((1, D), F32)],
        scratch_shapes=[pltpu.VMEM((tt, D), F32), pltpu.VMEM((tt, D), BF), pltpu.VMEM((tt, D), F32),
                        pltpu.VMEM((2, tt, D), BF)],
        input_output_aliases={0: 0},
        compiler_params=_cp("arbitrary", "arbitrary"),
    )(dz, dy3, z, z, lg, lb, ws, wst, bst)


def _in_proj_bwd(dz, w, x, g, dx1):
    T = x.shape[0]
    nc = w.shape[2]
    tm = min(T, 1024)
    tn = 1280
    nj = nc // tn
    ep = min(tm, 128)

    def body(dz_ref, w_ref, x_ref, g_ref, dx1_ref, dx_ref, dg_ref, acc):
        i, k, j = pl.program_id(0), pl.program_id(1), pl.program_id(2)

        @pl.when((i == 0) & (k == 0) & (j == 0))
        def _():
            dg_ref[...] = jnp.zeros_like(dg_ref)

        @pl.when((k == 0) & (j == 0))
        def _():
            acc[...] = jnp.zeros_like(acc)

        acc[...] += _dot_nt(dz_ref[...], w_ref[...])

        @pl.when((k == NSH - 1) & (j == nj - 1))
        def _():
            def step(c, dg):
                rows = pl.ds(pl.multiple_of(c * ep, ep), ep)
                dx, dgc = _rms_bwd(acc[rows, :], x_ref[rows, :], g_ref[...])
                dx_ref[rows, :] = dx1_ref[rows, :] + dx
                return dg + dgc
            dg_ref[...] += lax.fori_loop(0, tm // ep, step, jnp.zeros((1, D), F32))

    rowblk = pl.BlockSpec((tm, D), lambda i, k, j: (i, 0))
    vec = pl.BlockSpec((1, D), lambda i, k, j: (0, 0))
    return pl.pallas_call(
        body, name="in_proj_bwd", grid=(T // tm, NSH, nj),
        in_specs=[pl.BlockSpec((tm, tn), lambda i, k, j: (i, k * nj + j)),
                  pl.BlockSpec((None, D, tn), lambda i, k, j: (k, 0, j)), rowblk, vec, rowblk],
        out_specs=[rowblk, vec],
        out_shape=[jax.ShapeDtypeStruct((T, D), F32), jax.ShapeDtypeStruct((1, D), F32)],
        scratch_shapes=[pltpu.VMEM((tm, D), F32)],
        compiler_params=_cp("arbitrary", "arbitrary", "arbitrary"),
    )(dz, w, x, g, dx1)


def _layer_fwd(x, p, S, dep, late=None):
    h, z = _in_proj(x, p["g_mix_pre"], p["w_in"], dep)
    ya = _mix_a_fwd(z, p["conv_a_w"], S)
    yc, s = _mix_b_fwd(z, p["conf_dw_w"], p["conf_dw_b"], p["conf_ln_g"], p["conf_ln_b"], S)
    ys = _mix_s_fwd(z, p["sgu_ln_g"], p["sgu_ln_b"], p["sgu_ws"], p["sgu_bt"], S)
    if late is not None:
        p.update(late(ys))
    p3, merged, m, x1 = _mix_out_fwd(ya, yc, ys, z, x, p["w_branch"], p["w_out"], p["g_mix_post"])
    h2, a, f, x2 = _ffn_fwd(x1, p["g_ffn_pre"], p["w_ff1"], p["w_ff2"], p["g_ffn_post"])
    saved = dict(x=x, h=h, z=z, ya=ya, yc=yc, ys=ys, s=s, p3=p3, merged=merged, m=m, x1=x1, h2=h2, a=a, f=f)
    return x2, saved


def _layer_bwd(dx2, p, sv, S, dep):
    T = dx2.shape[0]
    bt = min(T, 512)
    nt = T // bt
    rk = D // NSH
    df, da, dx1, dg_ffn_post, dg_ffn_pre = _ffn_bwd(dx2, sv["f"], p["g_ffn_post"], sv["a"], p["w_ff2"],
                                                    p["w_ff1"], sv["x1"], p["g_ffn_pre"], dep)
    dw_ff2 = _wgrad("wgrad_ff2", (sv["a"], df), (NSH, nt),
                    [pl.BlockSpec((bt, D), lambda k, t: (t, k)), pl.BlockSpec((bt, D), lambda k, t: (t, 0))],
                    pl.BlockSpec((None, D, D), lambda k, t: (k, 0, 0)),
                    jax.ShapeDtypeStruct((NSH, D, D), BF), (D, D), relu2=True)
    dw_ff1 = _wgrad("wgrad_ff1", (sv["h2"], da), (NSH, nt),
                    [pl.BlockSpec((bt, D), lambda k, t: (t, 0)), pl.BlockSpec((bt, D), lambda k, t: (t, k))],
                    pl.BlockSpec((None, D, D), lambda k, t: (k, 0, 0)),
                    jax.ShapeDtypeStruct((NSH, D, D), BF), (D, D))
    dm, dp3, dy3, dz, dg_mix_post = _mix_out_bwd(dx1, sv["m"], p["g_mix_post"], p["w_out"], sv["p3"], sv["z"],
                                                 p["w_branch"])
    dw_out = _wgrad("wgrad_out", (sv["merged"], dm), (nt,),
                    [pl.BlockSpec((bt, D), lambda t: (t, 0)), pl.BlockSpec((bt, D), lambda t: (t, 0))],
                    pl.BlockSpec((D, D), lambda t: (0, 0)),
                    jax.ShapeDtypeStruct((D, D), BF), (D, D)).reshape(NSH, rk, D)
    ysp = lambda n: pl.BlockSpec((bt, D), lambda b, t: (jnp.where(b == n, t, 0), 0))
    dw_br = _wgrad("wgrad_branch", (sv["ya"], sv["yc"], sv["ys"], dp3), (3, nt),
                   [ysp(0), ysp(1), ysp(2), pl.BlockSpec((None, bt, D), lambda b, t: (b, t, 0))],
                   pl.BlockSpec((NSH, None, rk, D), lambda b, t: (0, b, 0, 0)),
                   jax.ShapeDtypeStruct((NSH, 3, rk, D), BF), (D, D), pick=lambda: pl.program_id(0))
    dz, dwa = _mix_a_bwd(dz, dy3, sv["z"], p["conv_a_w"], S)
    dz, dwc, dbc, dclg, dclb = _mix_b_bwd(dz, dy3, sv["s"], sv["z"], p["conf_dw_w"], p["conf_ln_g"],
                                          p["conf_ln_b"], S)
    dz, dws, dbst, dslg, dslb = _mix_s_bwd(dz, dy3, sv["z"], p["sgu_ln_g"], p["sgu_ln_b"], p["sgu_ws"],
                                           p["sgu_wst"], p["sgu_bt"], S)
    dx, dg_mix_pre = _in_proj_bwd(dz, p["w_in"], sv["x"], p["g_mix_pre"], dx1)
    tn = 1280
    nj = p["w_in"].shape[2] // tn
    dw_in = _wgrad("wgrad_in", (sv["h"], dz), (NSH, nj, nt),
                   [pl.BlockSpec((bt, D), lambda k, j, t: (t, 0)),
                    pl.BlockSpec((bt, tn), lambda k, j, t: (t, k * nj + j))],
                   pl.BlockSpec((None, D, tn), lambda k, j, t: (k, 0, j)),
                   jax.ShapeDtypeStruct(p["w_in"].shape, BF), (D, tn))
    tril = jnp.tril(jnp.ones((CHUNK, CHUNK), bool))
    small = dict(norm_mix_pre=dg_mix_pre, norm_mix_post=dg_mix_post, norm_ffn_pre=dg_ffn_pre,
                 norm_ffn_post=dg_ffn_post, conv_a_w=dwa, conf_dw_w=dwc, conf_dw_b=dbc, conf_ln_g=dclg,
                 conf_ln_b=dclb, sgu_ln_g=dslg, sgu_ln_b=dslb,
                 sgu_ws=jnp.where(tril[None], dws, 0.0), sgu_b=dbst.T)
    big = dict(w_in=dw_in, w_branch=dw_br, w_out=dw_out, w_ff1=dw_ff1, w_ff2=dw_ff2)
    return dx, big, small


SMALL_NAMES = ("norm_mix_pre", "norm_mix_post", "norm_ffn_pre", "norm_ffn_post", "conv_a_w", "conf_dw_w",
               "conf_dw_b", "conf_ln_g", "conf_ln_b", "sgu_ln_g", "sgu_ln_b", "sgu_b", "sgu_ws")
SMALL_ROWS = dict(norm_mix_pre=1, norm_mix_post=1, norm_ffn_pre=1, norm_ffn_post=1, conv_a_w=KA, conf_dw_w=KC,
                  conf_dw_b=1, conf_ln_g=1, conf_ln_b=1, sgu_ln_g=1, sgu_ln_b=1, sgu_b=1, sgu_ws=CHUNK)
SUBLANES = 8


def _pad8(r):
    return -(-r // SUBLANES) * SUBLANES


PACK_ROWS = sum(_pad8(r) for r in SMALL_ROWS.values())


def _pack_small(d):
    parts = []
    for n in SMALL_NAMES:
        r = SMALL_ROWS[n]
        parts.append(jnp.pad(d[n].reshape(r, D).astype(F32), ((0, _pad8(r) - r), (0, 0))))
    return jnp.concatenate(parts, axis=0)


def _unpack_small(a, shapes):
    out, r = {}, 0
    for n in SMALL_NAMES:
        out[n] = a[:, r:r + SMALL_ROWS[n]].reshape((a.shape[0],) + tuple(shapes[n]))
        r += _pad8(SMALL_ROWS[n])
    return out


def _me():
    return lax.axis_index("x"), lax.axis_index("y"), lax.axis_index("c")


def _slab(ref, q, a, h=None):
    r = ref.shape[1]
    rows = slice(None) if h is None else pl.ds(h * (r // 2), r // 2)
    return ref.at[pl.ds(q * a, a), rows, :]


def _rows(ref, h):
    r = ref.shape[-2]
    lead = (slice(None),) * (len(ref.shape) - 2)
    return ref.at[lead + (pl.ds(h * (r // 2), r // 2), slice(None))]


def _rcopy(src, dst, sems, idx, dev):
    return pltpu.make_async_remote_copy(src_ref=src, dst_ref=dst, send_sem=sems[0].at[idx], recv_sem=sems[1].at[idx],
                                        device_id=dev, device_id_type=MESH)


def _send_halves_to_sibling(parts):
    n = len(parts)

    def body(*refs):
        src, dst = refs[:n], refs[n:2 * n]
        sems = refs[2 * n:2 * n + 2]
        x, y, c = _me()
        cps = [_rcopy(_rows(src[i], 1 - c), dst[i], sems, i, (x, y, 1 - c)) for i in range(n)]
        for cp in cps:
            cp.start()
        for cp in cps:
            cp.wait()

    outs = [jax.ShapeDtypeStruct((p.shape[0], p.shape[1] // 2, p.shape[2]), p.dtype) for p in parts]
    return pl.pallas_call(
        body, name="pair_exchange", in_specs=[ANY] * n, out_specs=[ANY] * n, out_shape=outs,
        scratch_shapes=[pltpu.SemaphoreType.DMA((n,)), pltpu.SemaphoreType.DMA((n,))],
    )(*parts)


def _pair_add(part, sib, c):
    A, R, C = part.shape
    hr = R // 2
    br = min(hr, 512)
    nb = hr // br

    def body(c_ref, p_ref, s_ref, o_ref):
        o_ref[...] = (p_ref[...].astype(F32) + s_ref[...].astype(F32)).astype(BF)

    return pl.pallas_call(
        body, name="pair_add",
        grid_spec=pltpu.PrefetchScalarGridSpec(
            num_scalar_prefetch=1, grid=(A, nb),
            in_specs=[pl.BlockSpec((None, br, C), lambda a, i, c_ref: (a, c_ref[0] * nb + i, 0)),
                      pl.BlockSpec((None, br, C), lambda a, i, c_ref: (a, i, 0))],
            out_specs=pl.BlockSpec((None, br, C), lambda a, i, c_ref: (a, i, 0))),
        out_shape=jax.ShapeDtypeStruct((A, hr, C), BF),
        compiler_params=_cp("arbitrary", "arbitrary"),
    )(c, part, sib)


def _other_chips(x, y):
    return [(1 - x, y), (x, 1 - y), (1 - x, 1 - y)]


def _split_call(name, copies, srcs, lands, sems=None, after=None):
    n, m = len(srcs), len(lands)
    hbm = lambda t: pltpu.HBM(t.shape, t.dtype)
    pin = lambda t: pltpu.with_memory_space_constraint(t, pltpu.HBM)
    thru = [hbm(t) for t in srcs] + [hbm(t) for t in lands]
    sem_spec = pl.BlockSpec(memory_space=pltpu.SEMAPHORE)
    effect = pltpu.CompilerParams(has_side_effects=pltpu.SideEffectType.DATAFLOW_SIDE_EFFECTING)
    if sems is None:
        def start_body(*refs):
            src, land = refs[:n], refs[n:n + m]
            ssem, rsem = refs[n + m], refs[n + m + 1]
            token = refs[-1]
            cps = copies(src, land, (ssem, rsem))
            for cp in cps:
                cp.start()
            token[...] = jnp.zeros_like(token)

        ncp = copies.count
        out = pl.pallas_call(
            start_body, name=name,
            out_shape=(pltpu.SemaphoreType.DMA((ncp,)), pltpu.SemaphoreType.DMA((ncp,)), *thru,
                       jax.ShapeDtypeStruct((8, 128), F32)),
            in_specs=[ANY] * (n + m),
            out_specs=(sem_spec, sem_spec, *([ANY] * (n + m)), pl.BlockSpec(memory_space=pltpu.VMEM)),
            input_output_aliases={i: 2 + i for i in range(n + m)},
            compiler_params=effect,
        )(*[pin(t) for t in srcs], *[pin(t) for t in lands])
        return out[0], out[1], list(out[2:2 + n]), list(out[2 + n:2 + n + m]), out[-1]

    def wait_body(*refs):
        src, land = refs[:n], refs[n:n + m]
        ssem, rsem = refs[n + m], refs[n + m + 1]
        for cp in copies(src, land, (ssem, rsem)):
            cp.wait_send()
            cp.wait_recv()

    out = pl.pallas_call(
        wait_body, name=name, out_shape=tuple(thru),
        in_specs=[ANY] * (n + m) + [sem_spec, sem_spec, ANY],
        out_specs=tuple([ANY] * (n + m)),
        input_output_aliases={i: i for i in range(n + m)},
        compiler_params=effect,
    )(*srcs, *lands, sems[0], sems[1], after)
    return list(out[:n]), list(out[n:])


def _cast_into(w, land, layer, kidx, dep):
    _, a, R, C = w.shape
    br = R
    while br * C > 256 * 1024 and br % 32 == 0:
        br //= 2

    def body(k_ref, w_ref, land_ref, dep_ref, o_ref):
        o_ref[...] = w_ref[...].astype(o_ref.dtype)

    return pl.pallas_call(
        body, name="cast_into",
        grid_spec=pltpu.PrefetchScalarGridSpec(
            num_scalar_prefetch=1, grid=(a, R // br),
            in_specs=[pl.BlockSpec((None, None, br, C), lambda e, i, k: (layer, e, i, 0)), ANY, ANY],
            out_specs=pl.BlockSpec((None, br, C), lambda e, i, k: (k[0] * a + e, i, 0))),
        out_shape=jax.ShapeDtypeStruct(land.shape, land.dtype), input_output_aliases={2: 0},
        compiler_params=_cp("arbitrary", "arbitrary"),
    )(kidx, w, land, dep)


class _GatherCopies:
    def __init__(self, n, halves=True):
        self.n, self.count, self.halves = n, 3 * n, halves

    def __call__(self, src, land, sems):
        x, y, c = _me()
        k = 2 * x + y
        cps = []
        for j, (qx, qy) in enumerate(_other_chips(x, y)):
            for i in range(self.n):
                mine = _slab(land[i], k, land[i].shape[0] // NSH, c if self.halves else None)
                cps.append(_rcopy(mine, mine, sems, j * self.n + i, (qx, qy, c)))
        return cps


def _gather_finish(lands):
    n = len(lands)

    def body(*refs):
        dst = refs[n:2 * n]
        sems = refs[2 * n:2 * n + 2]
        x, y, c = _me()
        av = [d.shape[0] // NSH for d in dst]
        cps = []
        for j, (qx, qy) in enumerate(_other_chips(x, y)):
            for i in range(n):
                got = _slab(dst[i], 2 * qx + qy, av[i], c)
                cps.append(_rcopy(got, got, sems, j * n + i, (x, y, 1 - c)))
        for cp in cps:
            cp.start()
        for j, (qx, qy) in enumerate(_other_chips(x, y)):
            for i in range(n):
                other = _slab(dst[i], 2 * qx + qy, av[i], 1 - c)
                _rcopy(other, other, sems, j * n + i, (x, y, c)).wait_recv()
        for cp in cps:
            cp.wait_send()

    return pl.pallas_call(
        body, name="gather_finish", in_specs=[ANY] * n, out_specs=[ANY] * n,
        out_shape=[jax.ShapeDtypeStruct(t.shape, t.dtype) for t in lands],
        input_output_aliases={i: i for i in range(n)},
        scratch_shapes=[pltpu.SemaphoreType.DMA((3 * n,)), pltpu.SemaphoreType.DMA((3 * n,))],
    )(*lands)


class _ScatterCopies:
    def __init__(self, n):
        self.n, self.count = n, 3 * n

    def __call__(self, src, land, sems):
        x, y, c = _me()
        k = 2 * x + y
        cps = []
        for j, (qx, qy) in enumerate(_other_chips(x, y)):
            for i in range(self.n):
                a = src[i].shape[0] // NSH
                cps.append(_rcopy(_slab(src[i], 2 * qx + qy, a), _slab(land[i], k, a), sems, j * self.n + i,
                                  (qx, qy, c)))
        return cps


def _sum_chips(own, rcv, acc, layer, nlayers, idx):
    A, hr, C = rcv.shape
    a = A // NSH
    br = min(hr, 512)
    nb = hr // br

    def body(*refs):
        r0, r1, r2, r3 = refs[1:5]
        o_ref = refs[-1]
        o_ref[...] = ((r0[...].astype(F32) + r1[...].astype(F32)) + r2[...].astype(F32)) + r3[...].astype(F32)

    slot = lambda s: pl.BlockSpec((None, br, C), lambda e, i, ix: (ix[s] * a + e, i, 0))
    ops = [own, rcv, rcv, rcv]
    in_specs = [slot(0), slot(1), slot(2), slot(3)]
    aliases = {}
    if acc is not None:
        ops.append(acc)
        in_specs.append(ANY)
        aliases = {5: 0}
    return pl.pallas_call(
        body, name="sum_chips",
        grid_spec=pltpu.PrefetchScalarGridSpec(
            num_scalar_prefetch=1, grid=(a, nb), in_specs=in_specs,
            out_specs=pl.BlockSpec((None, None, br, C), lambda e, i, ix: (layer, e, ix[4] * nb + i, 0))),
        out_shape=jax.ShapeDtypeStruct((nlayers, a, 2 * hr, C), F32), input_output_aliases=aliases,
        compiler_params=_cp("arbitrary", "arbitrary"),
    )(idx, *ops)


def _join_halves(fulls):
    n = len(fulls)

    def body(*refs):
        buf = refs[n:2 * n]
        sems = refs[2 * n:2 * n + 2]
        x, y, c = _me()
        cps = [_rcopy(_rows(buf[i], c), _rows(buf[i], c), sems, i, (x, y, 1 - c)) for i in range(n)]
        for cp in cps:
            cp.start()
        for i in range(n):
            _rcopy(_rows(buf[i], 1 - c), _rows(buf[i], 1 - c), sems, i, (x, y, c)).wait_recv()
        for cp in cps:
            cp.wait_send()

    return pl.pallas_call(
        body, name="join_halves", in_specs=[ANY] * n, out_specs=[ANY] * n,
        out_shape=[jax.ShapeDtypeStruct(t.shape, t.dtype) for t in fulls],
        input_output_aliases={i: i for i in range(n)},
        scratch_shapes=[pltpu.SemaphoreType.DMA((n,)), pltpu.SemaphoreType.DMA((n,))],
    )(*fulls)


def _small_blocks(hr):
    br = hr
    while br > 512 and br % 16 == 0:
        br //= 2
    return br, hr // br


def _pair_sum_slot(part, sib, ck):
    R, C = part.shape
    hr = R // 2
    br, nb = _small_blocks(hr)

    def body(ix, p_ref, s_ref, o_ref):
        o_ref[...] = p_ref[...] + s_ref[...]

    return pl.pallas_call(
        body, name="pair_sum_slot",
        grid_spec=pltpu.PrefetchScalarGridSpec(
            num_scalar_prefetch=1, grid=(nb,),
            in_specs=[pl.BlockSpec((br, C), lambda i, ix: (ix[0] * nb + i, 0)),
                      pl.BlockSpec((br, C), lambda i, ix: (i, 0))],
            out_specs=pl.BlockSpec((None, br, C), lambda i, ix: (ix[1], i, 0))),
        out_shape=jax.ShapeDtypeStruct((NSH, hr, C), F32),
        compiler_params=_cp("arbitrary"),
    )(ck, part, sib)


def _sum_slots(slots, ck):
    _, hr, C = slots.shape
    br, nb = _small_blocks(hr)

    def body(ix, s_ref, o_ref):
        o_ref[...] = ((s_ref[0] + s_ref[1]) + s_ref[2]) + s_ref[3]

    return pl.pallas_call(
        body, name="sum_slots",
        grid_spec=pltpu.PrefetchScalarGridSpec(
            num_scalar_prefetch=1, grid=(nb,),
            in_specs=[pl.BlockSpec((NSH, br, C), lambda i, ix: (0, i, 0))],
            out_specs=pl.BlockSpec((br, C), lambda i, ix: (ix[0] * nb + i, 0))),
        out_shape=jax.ShapeDtypeStruct((2 * hr, C), F32),
        compiler_params=_cp("arbitrary"),
    )(ck, slots)


def _adamw(w, g, m, v):
    shape = w.shape
    C = shape[-1]
    R = shape[-2]
    A = 1
    for s in shape[:-2]:
        A *= s
    br = R
    while br * C > 256 * 1024 and br % 16 == 0:
        br //= 2
    c1 = 1.0 / (1.0 - ADAM_B1 ** ADAM_STEP)
    c2 = 1.0 / (1.0 - ADAM_B2 ** ADAM_STEP)

    def body(w_ref, g_ref, m_ref, v_ref, d_ref, nm_ref, nv_ref):
        gv = g_ref[...]
        nm = ADAM_B1 * m_ref[...] + (1.0 - ADAM_B1) * gv
        nv = ADAM_B2 * v_ref[...] + (1.0 - ADAM_B2) * (gv * gv)
        nm_ref[...] = nm
        nv_ref[...] = nv
        d_ref[...] = -ADAM_LR * ((nm * c1) / (jnp.sqrt(nv * c2) + ADAM_EPS) + ADAM_WD * w_ref[...])

    blk = pl.BlockSpec((None, br, C), lambda a, i: (a, i, 0))
    outs = pl.pallas_call(
        body, name="adamw", grid=(A, R // br), in_specs=[blk] * 4, out_specs=[blk] * 3,
        out_shape=[jax.ShapeDtypeStruct((A, R, C), F32)] * 3,
        compiler_params=_cp("arbitrary", "arbitrary"),
    )(*(t.reshape(A, R, C) for t in (w, g, m, v)))
    return tuple(o.reshape(shape) for o in outs)


WEIGHTS = ("norm_mix_pre", "norm_mix_post", "norm_ffn_pre", "norm_ffn_post", "w_in", "conv_a_w", "conf_dw_w",
           "conf_dw_b", "conf_ln_g", "conf_ln_b", "sgu_ln_g", "sgu_ln_b", "sgu_ws", "sgu_b", "w_branch", "w_out",
           "w_ff1", "w_ff2")
BIG = ("w_in", "w_branch", "w_out", "w_ff1", "w_ff2")
CONV_ROWS = 48


def kernel(x, norm_mix_pre, norm_mix_post, norm_ffn_pre, norm_ffn_post, w_in, conv_a_w, conf_dw_w, conf_dw_b, conf_ln_g, conf_ln_b, sgu_ln_g, sgu_ln_b, sgu_ws, sgu_b, w_branch, w_out, w_ff1, w_ff2, loss_target, m_norm_mix_pre, m_norm_mix_post, m_norm_ffn_pre, m_norm_ffn_post, m_w_in, m_conv_a_w, m_conf_dw_w, m_conf_dw_b, m_conf_ln_g, m_conf_ln_b, m_sgu_ln_g, m_sgu_ln_b, m_sgu_ws, m_sgu_b, m_w_branch, m_w_out, m_w_ff1, m_w_ff2, v_norm_mix_pre, v_norm_mix_post, v_norm_ffn_pre, v_norm_ffn_post, v_w_in, v_conv_a_w, v_conf_dw_w, v_conf_dw_b, v_conf_ln_g, v_conf_ln_b, v_sgu_ln_g, v_sgu_ln_b, v_sgu_ws, v_sgu_b, v_w_branch, v_w_out, v_w_ff1, v_w_ff2):
    w = dict(norm_mix_pre=norm_mix_pre, norm_mix_post=norm_mix_post, norm_ffn_pre=norm_ffn_pre,
             norm_ffn_post=norm_ffn_post, w_in=w_in, conv_a_w=conv_a_w, conf_dw_w=conf_dw_w, conf_dw_b=conf_dw_b,
             conf_ln_g=conf_ln_g, conf_ln_b=conf_ln_b, sgu_ln_g=sgu_ln_g, sgu_ln_b=sgu_ln_b, sgu_ws=sgu_ws,
             sgu_b=sgu_b, w_branch=w_branch, w_out=w_out, w_ff1=w_ff1, w_ff2=w_ff2)
    mom = dict(norm_mix_pre=m_norm_mix_pre, norm_mix_post=m_norm_mix_post, norm_ffn_pre=m_norm_ffn_pre,
               norm_ffn_post=m_norm_ffn_post, w_in=m_w_in, conv_a_w=m_conv_a_w, conf_dw_w=m_conf_dw_w,
               conf_dw_b=m_conf_dw_b, conf_ln_g=m_conf_ln_g, conf_ln_b=m_conf_ln_b, sgu_ln_g=m_sgu_ln_g,
               sgu_ln_b=m_sgu_ln_b, sgu_ws=m_sgu_ws, sgu_b=m_sgu_b, w_branch=m_w_branch, w_out=m_w_out,
               w_ff1=m_w_ff1, w_ff2=m_w_ff2)
    var = dict(norm_mix_pre=v_norm_mix_pre, norm_mix_post=v_norm_mix_post, norm_ffn_pre=v_norm_ffn_pre,
               norm_ffn_post=v_norm_ffn_post, w_in=v_w_in, conv_a_w=v_conv_a_w, conf_dw_w=v_conf_dw_w,
               conf_dw_b=v_conf_dw_b, conf_ln_g=v_conf_ln_g, conf_ln_b=v_conf_ln_b, sgu_ln_g=v_sgu_ln_g,
               sgu_ln_b=v_sgu_ln_b, sgu_ws=v_sgu_ws, sgu_b=v_sgu_b, w_branch=v_w_branch, w_out=v_w_out,
               w_ff1=v_w_ff1, w_ff2=v_w_ff2)
    L = w_in.shape[0]
    nseq, S, _ = x.shape
    T = nseq * S
    rk = D // NSH
    mx, my, mc = _me()
    k_chip = 2 * mx + my

    big_src = [w_in.reshape(L, 1, D, w_in.shape[2]), w_branch, w_out.reshape(L, 1, rk, D),
               w_ff1.reshape(L, 1, D, w_ff1.shape[2]), w_ff2.reshape(L, 1, w_ff2.shape[1], D)]
    kidx = jnp.reshape(k_chip, (1,)).astype(jnp.int32)
    conv_src = jnp.concatenate(
        [jnp.pad(conv_a_w, ((0, 0), (0, SUBLANES - KA), (0, 0))), jnp.pad(conf_dw_w, ((0, 0), (0, 1), (0, 0))),
         jnp.zeros((L, CONV_ROWS - SUBLANES - KC - 1, rk), F32)], axis=1)[None]

    def early_params(l, g_in, conv_full):
        return dict(
            g_mix_pre=norm_mix_pre[l][None], g_mix_post=norm_mix_post[l][None], g_ffn_pre=norm_ffn_pre[l][None],
            g_ffn_post=norm_ffn_post[l][None], w_in=g_in, conv_a_w=conv_full[l, :KA],
            conf_dw_w=conv_full[l, SUBLANES:SUBLANES + KC], conf_dw_b=conf_dw_b[l][None],
            conf_ln_g=conf_ln_g[l][None], conf_ln_b=conf_ln_b[l][None], sgu_ln_g=sgu_ln_g[l][None],
            sgu_ln_b=sgu_ln_b[l][None], sgu_ws=sgu_ws[l], sgu_wst=jnp.swapaxes(sgu_ws[l], 1, 2),
            sgu_bt=sgu_b[l].T)

    def late_params(gathered):
        g_br, g_out, g_ff1, g_ff2 = gathered
        return dict(w_branch=g_br.reshape(NSH, 3, rk, D), w_out=g_out.reshape(D, D), w_ff1=g_ff1,
                    w_ff2=g_ff2.reshape(NSH * w_ff2.shape[1], D))

    def gather_start(name, srcs, l, dep):
        lands = [_cast_into(s, lax.empty((NSH * s.shape[1],) + s.shape[2:], F32 if s is conv_src else BF), l, kidx,
                            dep) for s in srcs]
        return _split_call(name, _GatherCopies(len(lands)), [], lands)

    def gather_land(name, flight, after):
        ssem, rsem, _, lands, _ = flight
        _, lands = _split_call(name, _GatherCopies(len(lands)), [], lands, (ssem, rsem), after)
        return _gather_finish(lands)

    xt = x.reshape(T, D)
    layers, saved = [], []
    head = gather_start("gather_start_0a", [big_src[0], conv_src], 0, kidx)
    tail = gather_start("gather_start_0b", big_src[1:], 0, head[4])
    g_in, conv_g = gather_land("gather_wait_0a", head, xt)
    conv_full = conv_g.reshape(NSH, L, CONV_ROWS, rk).transpose(1, 2, 0, 3).reshape(L, CONV_ROWS, D)
    for l in range(L):
        late = None
        if l == 0:
            p = early_params(l, g_in, conv_full)
            late = lambda after: late_params(gather_land("gather_wait_0b", tail, after))
        else:
            gathered = gather_land(f"gather_wait_{l}", flight, xt)
            g_in = gathered[0]
            p = dict(early_params(l, g_in, conv_full), **late_params(gathered[1:]))
        if l + 1 < L:
            flight = gather_start(f"gather_start_{l + 1}", big_src, l + 1, g_in)
            dep = flight[4]
        else:
            dep = jnp.zeros((8, 128), F32)
        xt, sv = _layer_fwd(xt, p, S, dep, late)
        layers.append(p)
        saved.append(sv)
    dx, loss_row = _loss_head(xt, loss_target.reshape(T, D))
    loss = lax.psum(loss_row[0, 0], ("x", "y", "c"))

    c_arr = jnp.reshape(mc, (1,)).astype(jnp.int32)
    idx = jnp.stack([k_chip, k_chip ^ 2, k_chip ^ 1, k_chip ^ 3, mc]).astype(jnp.int32)
    fulls = [None] * len(BIG)
    smalls = [None] * L
    flight = None
    dep = jnp.zeros((8, 128), F32)

    def land_scatter(l, fl, after, fulls):
        ssem, rsem, sums, rcv, _ = fl
        sums, rcv = _split_call(f"scatter_wait_{l}", _ScatterCopies(len(sums)), sums, rcv, (ssem, rsem), after)
        return [_sum_chips(o, r, f, l, L, idx) for o, r, f in zip(sums, rcv, fulls)]

    for l in reversed(range(L)):
        dx, big, small = _layer_bwd(dx, layers[l], saved[l], S, dep)
        smalls[l] = _pack_small(small)
        if flight is not None:
            fulls = land_scatter(l + 1, flight, dx, fulls)
        parts = [big["w_in"], big["w_branch"].reshape(NSH * 3, rk, D), big["w_out"], big["w_ff1"], big["w_ff2"]]
        sib = _send_halves_to_sibling(parts)
        sums = [_pair_add(p, s, c_arr) for p, s in zip(parts, sib)]
        rcv = [lax.empty(s.shape, s.dtype) for s in sums]
        flight = _split_call(f"scatter_start_{l}", _ScatterCopies(len(sums)), sums, rcv)
        dep = flight[4]
    packed = jnp.concatenate(smalls, axis=0)
    nrow = packed.shape[0]
    ck = jnp.stack([mc, k_chip]).astype(jnp.int32)
    (sib,) = _send_halves_to_sibling([packed.reshape(1, nrow, D)])
    slots = _pair_sum_slot(packed, sib.reshape(nrow // 2, D), ck)
    small_flight = _split_call("small_start", _GatherCopies(1, halves=False), [], [slots])

    fulls = land_scatter(0, flight, small_flight[4], fulls)
    full = _join_halves(fulls)
    grads = {n: f.reshape(w[n].shape) for n, f in zip(BIG, full)}
    delta, new_m, new_v = {}, {}, {}
    for n in BIG:
        delta[n], new_m[n], new_v[n] = _adamw(w[n], grads[n], mom[n], var[n])

    _, (slots,) = _split_call("small_wait", _GatherCopies(1, halves=False), [], small_flight[3],
                              (small_flight[0], small_flight[1]), delta[BIG[-1]])
    (small_sum,) = _join_halves([_sum_slots(slots, ck).reshape(1, 1, nrow, D)])
    shapes = {n: (w[n].shape[1:] if n not in ("conv_a_w", "conf_dw_w") else (w[n].shape[1], D)) for n in SMALL_NAMES}
    sg = _unpack_small(small_sum.reshape(L, PACK_ROWS, D), shapes)
    for n in SMALL_NAMES:
        if n in ("conv_a_w", "conf_dw_w"):
            grads[n] = lax.dynamic_slice_in_dim(sg[n], k_chip * rk, rk, axis=2)
        else:
            grads[n] = sg[n]

    for n in SMALL_NAMES:
        sh = w[n].shape
        flat = (sh[0] * sh[1], sh[2]) if n in ("conv_a_w", "conf_dw_w") else (-1, D)
        d, nm, nv = _adamw(*(t.reshape(flat) for t in (w[n], grads[n], mom[n], var[n])))
        delta[n], new_m[n], new_v[n] = d.reshape(sh), nm.reshape(sh), nv.reshape(sh)

    return (loss, dx.reshape(x.shape), *[grads[n] for n in WEIGHTS], *[delta[n] for n in WEIGHTS],
            *[new_m[n] for n in WEIGHTS], *[new_v[n] for n in WEIGHTS])
```

```python
import functools

import jax
import jax.numpy as jnp
from jax import lax
from jax.experimental import pallas as pl
from jax.experimental.pallas import tpu as pltpu

D = 1024
HEADS = 8
CHUNK = 128
KA = 3
KC = 31
HALO = 32
SUBLANES = 8
NSH = 4
NDEV = 8
EPS = 1e-6
BF = jnp.bfloat16
F32 = jnp.float32
VMEM_LIMIT = 56 * 1024 * 1024

ADAM_LR = 0.001
ADAM_B1 = 0.9
ADAM_B2 = 0.999
ADAM_EPS = 1e-08
ADAM_WD = 0.01
ADAM_STEP = 10

MESH = pl.DeviceIdType.MESH
ANY = pl.BlockSpec(memory_space=pl.ANY)


def _cp(*sem):
    return pltpu.CompilerParams(dimension_semantics=sem, vmem_limit_bytes=VMEM_LIMIT)


def _sig(x):
    return 1.0 / (1.0 + jnp.exp(-x))


_GC = 0.7978845608028654


def _gelu(x):
    x2 = x * x
    t = jnp.tanh(_GC * x * (1.0 + 0.044715 * x2))
    y = 0.5 * x * (1.0 + t)
    dy = 0.5 * (1.0 + t) + 0.5 * x * (1.0 - t * t) * _GC * (1.0 + 3.0 * 0.044715 * x2)
    return y, dy


def _rms_fwd(x, g):
    r = lax.rsqrt(jnp.mean(x * x, axis=-1, keepdims=True) + EPS)
    return x * r * g


def _rms_bwd(dy, x, g):
    r = lax.rsqrt(jnp.mean(x * x, axis=-1, keepdims=True) + EPS)
    xn = x * r
    dyg = dy * g
    dx = r * (dyg - xn * jnp.mean(dyg * xn, axis=-1, keepdims=True))
    return dx, jnp.sum(dy * xn, axis=0, keepdims=True)


def _ln_stats(x):
    mu = jnp.mean(x, axis=-1, keepdims=True)
    xc = x - mu
    r = lax.rsqrt(jnp.mean(xc * xc, axis=-1, keepdims=True) + EPS)
    return xc * r, r


def _ln_bwd(dn, n, r):
    return r * (dn - jnp.mean(dn, axis=-1, keepdims=True) - n * jnp.mean(dn * n, axis=-1, keepdims=True))


def _dot(a, b):
    return jnp.dot(a, b, preferred_element_type=F32)


def _dot_nt(a, b):
    return lax.dot_general(a, b, (((1,), (1,)), ((), ())), preferred_element_type=F32)


def _dot_tn(a, b):
    return lax.dot_general(a, b, (((0,), (0,)), ((), ())), preferred_element_type=F32)


def _in_proj(x, g, w, dep):
    T = x.shape[0]
    nc = w.shape[2]
    tm = min(T, 1024)
    tn = 1280
    nj = nc // tn

    def body(x_ref, g_ref, w_ref, dep_ref, h_ref, z_ref, h_scr):
        @pl.when((pl.program_id(1) == 0) & (pl.program_id(2) == 0))
        def _():
            h = _rms_fwd(x_ref[...], g_ref[...]).astype(BF)
            h_scr[...] = h
            h_ref[...] = h
        z_ref[...] = _dot(h_scr[...], w_ref[...]).astype(BF)

    return pl.pallas_call(
        body, name="in_proj", grid=(T // tm, NSH, nj),
        in_specs=[pl.BlockSpec((tm, D), lambda i, k, j: (i, 0)),
                  pl.BlockSpec((1, D), lambda i, k, j: (0, 0)),
                  pl.BlockSpec((None, D, tn), lambda i, k, j: (k, 0, j)), ANY],
        out_specs=[pl.BlockSpec((tm, D), lambda i, k, j: (i, 0)),
                   pl.BlockSpec((tm, tn), lambda i, k, j: (i, k * nj + j))],
        out_shape=[jax.ShapeDtypeStruct((T, D), BF), jax.ShapeDtypeStruct((T, NSH * nc), BF)],
        scratch_shapes=[pltpu.VMEM((tm, D), BF)],
        compiler_params=_cp("arbitrary", "arbitrary", "arbitrary"),
    )(x, g, w, dep)


def _tile_specs(tt, nt_total, reverse):
    def tile(i):
        return (nt_total - 1 - i) if reverse else i

    def cur(c):
        return pl.BlockSpec((tt, D), lambda i, *_: (tile(i), c))

    def halo(c):
        return pl.BlockSpec((HALO, D), lambda i, *_: (jnp.maximum(tile(i) * (tt // HALO) - 1, 0), c))

    def row(r=1):
        return pl.BlockSpec((r, D), lambda i, *_: (0, 0))

    return tile, cur, halo, row


RC = 16


def _chunks(tt, fn, group=2):
    def step(c, carry):
        for u in range(group):
            fn(pl.multiple_of((c * group + u) * RC, RC))
        return carry
    lax.fori_loop(0, tt // (RC * group), step, 0)


ALL_SHIFTS = tuple(range(SUBLANES))


def _shifts_of(offs):
    return tuple(sorted({o % SUBLANES for o in offs}))


def _shifted_copies(ext, sh, nrows, shifts=ALL_SHIFTS):
    for i, s in enumerate(shifts):
        sh[i] = ext[pl.ds(s, nrows), :]


def _window(sh, o, r0, shifts=ALL_SHIFTS):
    return sh[shifts.index(o % SUBLANES), pl.ds(r0 + (o // SUBLANES) * SUBLANES, RC), :]


def _fill_taps(wb, w_ref, ntap):
    for k in range(ntap):
        wb[k * SUBLANES:(k + 1) * SUBLANES, :] = jnp.broadcast_to(w_ref[k:k + 1, :], (SUBLANES, D))


def _conv_chunk(sh, wb, offs, r0, shifts=ALL_SHIFTS):
    acc = None
    for k, o in enumerate(offs):
        wk = wb[k * SUBLANES:(k + 1) * SUBLANES, :]
        term = jnp.concatenate([wk] * (RC // SUBLANES), axis=0) * _window(sh, o, r0, shifts)
        acc = term if acc is None else acc + term
    return acc


def _conv_wgrad_chunked(dw_ref, d_ref, sh, offs, tt, shifts=ALL_SHIFTS):
    group = 4
    for k, o in enumerate(offs):
        def step(c, acc, o=o):
            prods = []
            for u in range(group):
                r0 = pl.multiple_of((c * group + u) * RC, RC)
                prods.append(d_ref[pl.ds(r0, RC), :] * _window(sh, o, r0, shifts))
            return acc + ((prods[0] + prods[1]) + (prods[2] + prods[3]))
        acc = lax.fori_loop(0, tt // (RC * group), step, jnp.zeros((RC, D), F32))
        dw_ref[k:k + 1, :] += jnp.sum(acc, axis=0, keepdims=True)


def _causal_offsets(ntap):
    return [HALO - (ntap - 1) + k for k in range(ntap)]


def _anticausal_offsets(ntap):
    return [ntap - 1 - k for k in range(ntap)]


def _mix_a_fwd(z, wa, S):
    T = z.shape[0]
    tt = min(S, 256)
    nt = S // tt
    _, cur, halo, row = _tile_specs(tt, T // tt, False)

    nrows = HALO + tt
    offs = _causal_offsets(KA)
    shifts = _shifts_of(offs)

    def body(ah, ab, ac, ah_h, ac_h, w_ref, y_ref, ext, sh, wb):
        @pl.when(pl.program_id(0) == 0)
        def _():
            _fill_taps(wb, w_ref, KA)
            ext[nrows:, :] = jnp.zeros((SUBLANES, D), F32)

        first = (pl.program_id(0) % nt) == 0
        ph = ah_h[...].astype(F32) * ac_h[...].astype(F32)
        ext[0:HALO, :] = jnp.where(first, 0.0, ph)

        def prod(r0):
            rows = pl.ds(r0, RC)
            ext[pl.ds(HALO + r0, RC), :] = ah[rows, :].astype(F32) * ac[rows, :].astype(F32)
        _chunks(tt, prod)
        _shifted_copies(ext, sh, nrows, shifts)

        def conv(r0):
            rows = pl.ds(r0, RC)
            q = _conv_chunk(sh, wb, offs, r0, shifts)
            y_ref[rows, :] = (ab[rows, :].astype(F32) * q).astype(BF)
        _chunks(tt, conv)

    return pl.pallas_call(
        body, name="mix_a_fwd", grid=(T // tt,),
        in_specs=[cur(0), cur(1), cur(2), halo(0), halo(2), row(KA)],
        out_specs=pl.BlockSpec((tt, D), lambda i: (i, 0)),
        out_shape=jax.ShapeDtypeStruct((T, D), BF),
        scratch_shapes=[pltpu.VMEM((nrows + SUBLANES, D), F32), pltpu.VMEM((len(shifts), nrows, D), F32),
                        pltpu.VMEM((KA * SUBLANES, D), F32)],
        compiler_params=_cp("arbitrary"),
    )(z, z, z, z, z, wa)


def _mix_b_fwd(z, wc, bc, lg, lb, S):
    T = z.shape[0]
    tt = min(S, 256)
    nt = S // tt
    _, cur, halo, row = _tile_specs(tt, T // tt, False)

    nrows = HALO + tt
    offs = _causal_offsets(KC)

    def body(ca, cg, ca_h, cg_h, w_ref, bc_ref, lg_ref, lb_ref, y_ref, s_ref, ext, sh, wb):
        @pl.when(pl.program_id(0) == 0)
        def _():
            _fill_taps(wb, w_ref, KC)
            ext[nrows:, :] = jnp.zeros((SUBLANES, D), F32)

        first = (pl.program_id(0) % nt) == 0
        rh = ca_h[...].astype(F32) * _sig(cg_h[...].astype(F32))
        ext[0:HALO, :] = jnp.where(first, 0.0, rh)

        def glu(r0):
            rows = pl.ds(r0, RC)
            ext[pl.ds(HALO + r0, RC), :] = ca[rows, :].astype(F32) * _sig(cg[rows, :].astype(F32))
        _chunks(tt, glu)
        _shifted_copies(ext, sh, nrows)

        def conv(r0):
            rows = pl.ds(r0, RC)
            s = _conv_chunk(sh, wb, offs, r0) + bc_ref[...]
            s_ref[rows, :] = s.astype(BF)
            n, _ = _ln_stats(s)
            t = n * lg_ref[...] + lb_ref[...]
            y_ref[rows, :] = (t * _sig(t)).astype(BF)
        _chunks(tt, conv)

    return pl.pallas_call(
        body, name="mix_b_fwd", grid=(T // tt,),
        in_specs=[cur(3), cur(4), halo(3), halo(4), row(KC), row(), row(), row()],
        out_specs=[pl.BlockSpec((tt, D), lambda i: (i, 0))] * 2,
        out_shape=[jax.ShapeDtypeStruct((T, D), BF)] * 2,
        scratch_shapes=[pltpu.VMEM((nrows + SUBLANES, D), F32), pltpu.VMEM((SUBLANES, nrows, D), F32),
                        pltpu.VMEM((KC * SUBLANES, D), F32)],
        compiler_params=_cp("arbitrary"),
    )(z, z, z, z, wc, bc, lg, lb)


def _causal_mask(transposed):
    r = lax.broadcasted_iota(jnp.int32, (CHUNK, CHUNK), 0)
    c = lax.broadcasted_iota(jnp.int32, (CHUNK, CHUNK), 1)
    return (c >= r) if transposed else (r >= c)


def _mix_s_fwd(z, lg, lb, ws, bst, S):
    T = z.shape[0]
    tt = min(S, 256)
    _, cur, _, row = _tile_specs(tt, T // tt, False)

    def body(su, sv, lg_ref, lb_ref, ws_ref, bst_ref, y_ref, u_scr, vn_scr):
        u_scr[...] = _gelu(su[...].astype(F32))[0]
        n, _ = _ln_stats(_gelu(sv[...].astype(F32))[0])
        vn_scr[...] = (n * lg_ref[...] + lb_ref[...]).astype(BF)
        mask = _causal_mask(False)
        for h in range(HEADS):
            wm = jnp.where(mask, ws_ref[h], 0.0).astype(BF)
            cols = slice(h * CHUNK, (h + 1) * CHUNK)
            for c in range(tt // CHUNK):
                rows = slice(c * CHUNK, (c + 1) * CHUNK)
                mixed = _dot(wm, vn_scr[rows, cols]) + bst_ref[:, h:h + 1]
                y_ref[rows, cols] = (u_scr[rows, cols] * mixed).astype(BF)

    return pl.pallas_call(
        body, name="mix_s_fwd", grid=(T // tt,),
        in_specs=[cur(5), cur(6), row(), row(),
                  pl.BlockSpec((HEADS, CHUNK, CHUNK), lambda i: (0, 0, 0)),
                  pl.BlockSpec((CHUNK, HEADS), lambda i: (0, 0))],
        out_specs=pl.BlockSpec((tt, D), lambda i: (i, 0)),
        out_shape=jax.ShapeDtypeStruct((T, D), BF),
        scratch_shapes=[pltpu.VMEM((tt, D), F32), pltpu.VMEM((tt, D), BF)],
        compiler_params=_cp("arbitrary"),
    )(z, z, lg, lb, ws, bst)


def _mix_out_fwd(ya, yc, ys, z, x, wb, wo, gp, dep):
    T = x.shape[0]
    tm = min(T, 256)
    rk = D // NSH

    def body(ya_ref, yc_ref, ys_ref, ga, gc, gs, x_ref, wb_ref, wo_ref, gp_ref, dep_ref,
             p_ref, mg_ref, m_ref, x1_ref):
        acc = None
        for b, (y_ref, g_ref) in enumerate(((ya_ref, ga), (yc_ref, gc), (ys_ref, gs))):
            pb = None
            for k in range(NSH):
                part = _dot(y_ref[:, k * rk:(k + 1) * rk], wb_ref[k, b])
                pb = part if pb is None else pb + part
            p_ref[b] = pb.astype(BF)
            term = _sig(g_ref[...].astype(F32)) * pb
            acc = term if acc is None else acc + term
        mg = acc.astype(BF)
        mg_ref[...] = mg
        m = _dot(mg, wo_ref[...])
        m_ref[...] = m.astype(BF)
        x1_ref[...] = x_ref[...] + _rms_fwd(m, gp_ref[...])

    rowblk = pl.BlockSpec((tm, D), lambda i: (i, 0))
    return pl.pallas_call(
        body, name="mix_out_fwd", grid=(T // tm,),
        in_specs=[rowblk, rowblk, rowblk,
                  pl.BlockSpec((tm, D), lambda i: (i, 7)), pl.BlockSpec((tm, D), lambda i: (i, 8)),
                  pl.BlockSpec((tm, D), lambda i: (i, 9)), rowblk,
                  pl.BlockSpec((NSH, 3, rk, D), lambda i: (0, 0, 0, 0)),
                  pl.BlockSpec((D, D), lambda i: (0, 0)),
                  pl.BlockSpec((1, D), lambda i: (0, 0)), ANY],
        out_specs=[pl.BlockSpec((3, tm, D), lambda i: (0, i, 0)), rowblk, rowblk, rowblk],
        out_shape=[jax.ShapeDtypeStruct((3, T, D), BF), jax.ShapeDtypeStruct((T, D), BF),
                   jax.ShapeDtypeStruct((T, D), BF), jax.ShapeDtypeStruct((T, D), F32)],
        compiler_params=_cp("arbitrary"),
    )(ya, yc, ys, z, z, z, x, wb, wo, gp, dep)


def _ffn_fwd(x1, g3, w1, w2, g4):
    T = x1.shape[0]
    tm = min(T, 512)

    def body(x_ref, g3_ref, w1_ref, w2_ref, g4_ref, h_ref, a_ref, f_ref, x2_ref, h_scr, acc):
        k = pl.program_id(1)

        @pl.when(k == 0)
        def _():
            h = _rms_fwd(x_ref[...], g3_ref[...]).astype(BF)
            h_scr[...] = h
            h_ref[...] = h
            acc[...] = jnp.zeros_like(acc)

        a = _dot(h_scr[...], w1_ref[...])
        a_ref[...] = a.astype(BF)
        r = jnp.maximum(a, 0.0)
        acc[...] += _dot((r * r).astype(BF), w2_ref[...])

        @pl.when(k == NSH - 1)
        def _():
            f = acc[...]
            f_ref[...] = f.astype(BF)
            x2_ref[...] = x_ref[...] + _rms_fwd(f, g4_ref[...])

    rowblk = pl.BlockSpec((tm, D), lambda i, k: (i, 0))
    vec = pl.BlockSpec((1, D), lambda i, k: (0, 0))
    return pl.pallas_call(
        body, name="ffn_fwd", grid=(T // tm, NSH),
        in_specs=[rowblk, vec, pl.BlockSpec((None, D, D), lambda i, k: (k, 0, 0)),
                  pl.BlockSpec((D, D), lambda i, k: (k, 0)), vec],
        out_specs=[rowblk, pl.BlockSpec((tm, D), lambda i, k: (i, k)), rowblk, rowblk],
        out_shape=[jax.ShapeDtypeStruct((T, D), BF), jax.ShapeDtypeStruct((T, NSH * D), BF),
                   jax.ShapeDtypeStruct((T, D), BF), jax.ShapeDtypeStruct((T, D), F32)],
        scratch_shapes=[pltpu.VMEM((tm, D), BF), pltpu.VMEM((tm, D), F32)],
        compiler_params=_cp("arbitrary", "arbitrary"),
    )(x1, g3, w1, w2, g4)


def _loss_head(y, target):
    T = y.shape[0]
    tm = min(T, 512)

    def body(y_ref, t_ref, dy_ref, l_ref):
        @pl.when(pl.program_id(0) == 0)
        def _():
            l_ref[...] = jnp.zeros_like(l_ref)
        e = y_ref[...] - t_ref[...]
        dy_ref[...] = e * (1.0 / D)
        l_ref[...] += jnp.sum(e * e) * (0.5 / D)

    rowblk = pl.BlockSpec((tm, D), lambda i: (i, 0))
    return pl.pallas_call(
        body, name="loss_head", grid=(T // tm,),
        in_specs=[rowblk, rowblk],
        out_specs=[rowblk, pl.BlockSpec((1, 128), lambda i: (0, 0))],
        out_shape=[jax.ShapeDtypeStruct((T, D), F32), jax.ShapeDtypeStruct((1, 128), F32)],
        compiler_params=_cp("arbitrary"),
    )(y, target)


def _ffn_bwd(dx2, f, g4, a, w2, w1, x1, g3, dep):
    T = dx2.shape[0]
    tm = min(T, 512)

    def body(dx2_ref, f_ref, g4_ref, a_ref, w2_ref, w1_ref, x1_ref, g3_ref, dep_ref,
             df_ref, da_ref, dx1_ref, dg4_ref, dg3_ref, df_scr, acc):
        i, k = pl.program_id(0), pl.program_id(1)

        @pl.when((i == 0) & (k == 0))
        def _():
            dg4_ref[...] = jnp.zeros_like(dg4_ref)
            dg3_ref[...] = jnp.zeros_like(dg3_ref)

        @pl.when(k == 0)
        def _():
            df, dg = _rms_bwd(dx2_ref[...], f_ref[...].astype(F32), g4_ref[...])
            dg4_ref[...] += dg
            dfb = df.astype(BF)
            df_scr[...] = dfb
            df_ref[...] = dfb
            acc[...] = jnp.zeros_like(acc)

        av = a_ref[...].astype(F32)
        da = (_dot_nt(df_scr[...], w2_ref[...]) * (2.0 * jnp.maximum(av, 0.0))).astype(BF)
        da_ref[...] = da
        acc[...] += _dot_nt(da, w1_ref[...])

        @pl.when(k == NSH - 1)
        def _():
            dx, dg = _rms_bwd(acc[...], x1_ref[...], g3_ref[...])
            dg3_ref[...] += dg
            dx1_ref[...] = dx2_ref[...] + dx

    rowblk = pl.BlockSpec((tm, D), lambda i, k: (i, 0))
    vec = pl.BlockSpec((1, D), lambda i, k: (0, 0))
    return pl.pallas_call(
        body, name="ffn_bwd", grid=(T // tm, NSH),
        in_specs=[rowblk, rowblk, vec, pl.BlockSpec((tm, D), lambda i, k: (i, k)),
                  pl.BlockSpec((D, D), lambda i, k: (k, 0)),
                  pl.BlockSpec((None, D, D), lambda i, k: (k, 0, 0)), rowblk, vec, ANY],
        out_specs=[rowblk, pl.BlockSpec((tm, D), lambda i, k: (i, k)), rowblk, vec, vec],
        out_shape=[jax.ShapeDtypeStruct((T, D), BF), jax.ShapeDtypeStruct((T, NSH * D), BF),
                   jax.ShapeDtypeStruct((T, D), F32), jax.ShapeDtypeStruct((1, D), F32),
                   jax.ShapeDtypeStruct((1, D), F32)],
        scratch_shapes=[pltpu.VMEM((tm, D), BF), pltpu.VMEM((tm, D), F32)],
        compiler_params=_cp("arbitrary", "arbitrary"),
    )(dx2, f, g4, a, w2, w1, x1, g3, dep)


def _wgrad(name, ops, grid, in_specs, out_spec, out_shape, acc_shape, pick=None, relu2=False):
    nt = grid[-1]
    na = len(ops) - 1

    def body(*refs):
        a_refs, b_ref, o_ref, acc = refs[:na], refs[na], refs[na + 1], refs[na + 2]
        t = pl.program_id(len(grid) - 1)

        @pl.when(t == 0)
        def _():
            acc[...] = jnp.zeros_like(acc)

        def add(a_ref):
            av = a_ref[...]
            if relu2:
                r = jnp.maximum(av.astype(F32), 0.0)
                av = (r * r).astype(BF)
            acc[...] += _dot_tn(av, b_ref[...])

        if na == 1:
            add(a_refs[0])
        else:
            sel = pick()
            for n in range(na):
                pl.when(sel == n)(functools.partial(add, a_refs[n]))

        @pl.when(t == nt - 1)
        def _():
            if len(o_ref.shape) == 2:
                o_ref[...] = acc[...].astype(o_ref.dtype)
            else:
                rs = o_ref.shape[1]
                for q in range(o_ref.shape[0]):
                    o_ref[q] = acc[q * rs:(q + 1) * rs, :].astype(o_ref.dtype)

    return pl.pallas_call(
        body, name=name, grid=grid, in_specs=in_specs, out_specs=out_spec, out_shape=out_shape,
        scratch_shapes=[pltpu.VMEM(acc_shape, F32)],
        compiler_params=_cp(*(["arbitrary"] * len(grid))),
    )(*ops)


def _mix_out_bwd(dx1, m, gp, wo, p3, z, wb):
    T = dx1.shape[0]
    tm = min(T, 512)
    rk = D // NSH

    def body(dx1_ref, m_ref, gp_ref, wo_ref, p_ref, g_ref, wb_ref,
             dm_ref, dp_ref, dy_ref, dz_ref, dgp_ref, dmg):
        i, b = pl.program_id(0), pl.program_id(1)

        @pl.when((i == 0) & (b == 0))
        def _():
            dgp_ref[...] = jnp.zeros_like(dgp_ref)

        @pl.when(b == 0)
        def _():
            dm, dg = _rms_bwd(dx1_ref[...], m_ref[...].astype(F32), gp_ref[...])
            dgp_ref[...] += dg
            dmb = dm.astype(BF)
            dm_ref[...] = dmb
            dmg[...] = _dot_nt(dmb, wo_ref[...])

        gate = _sig(g_ref[...].astype(F32))
        d = dmg[...]
        dp = (d * gate).astype(BF)
        dp_ref[...] = dp
        dz_ref[...] = (d * p_ref[...].astype(F32) * gate * (1.0 - gate)).astype(BF)
        for k in range(NSH):
            dy_ref[:, k * rk:(k + 1) * rk] = _dot_nt(dp, wb_ref[k, b]).astype(BF)

    rowblk = pl.BlockSpec((tm, D), lambda i, b: (i, 0))
    br = pl.BlockSpec((None, tm, D), lambda i, b: (b, i, 0))
    vec = pl.BlockSpec((1, D), lambda i, b: (0, 0))
    return pl.pallas_call(
        body, name="mix_out_bwd", grid=(T // tm, 3),
        in_specs=[rowblk, rowblk, vec, pl.BlockSpec((D, D), lambda i, b: (0, 0)), br,
                  pl.BlockSpec((tm, D), lambda i, b: (i, 7 + b)),
                  pl.BlockSpec((NSH, 3, rk, D), lambda i, b: (0, 0, 0, 0))],
        out_specs=[rowblk, br, br, pl.BlockSpec((tm, D), lambda i, b: (i, 7 + b)), vec],
        out_shape=[jax.ShapeDtypeStruct((T, D), BF), jax.ShapeDtypeStruct((3, T, D), BF),
                   jax.ShapeDtypeStruct((3, T, D), BF), jax.ShapeDtypeStruct((T, 10 * D), BF),
                   jax.ShapeDtypeStruct((1, D), F32)],
        scratch_shapes=[pltpu.VMEM((tm, D), F32)],
        compiler_params=_cp("arbitrary", "arbitrary"),
    )(dx1, m, gp, wo, p3, z, wb)


def _mix_a_bwd(dz, dy3, z, wa, S):
    T = z.shape[0]
    tt = min(S, 256)
    nt = S // tt
    ntt = T // tt
    tile, cur, halo, row = _tile_specs(tt, ntt, True)

    nrows = HALO + tt
    coffs, aoffs = _causal_offsets(KA), _anticausal_offsets(KA)
    cshifts, ashifts = _shifts_of(coffs), _shifts_of(aoffs)

    def body(dz_in, dy_ref, ah, ab, ac, ah_h, ac_h, w_ref, dz_ref, dw_ref, ext_p, ext_d, sh, wb, stage):
        i, b = pl.program_id(0), pl.program_id(1)
        ti = ntt - 1 - i

        @pl.when((i == 0) & (b == 0))
        def _():
            dw_ref[...] = jnp.zeros_like(dw_ref)
            ext_d[...] = jnp.zeros_like(ext_d)
            ext_p[nrows:, :] = jnp.zeros((SUBLANES, D), F32)
            _fill_taps(wb, w_ref, KA)

        @pl.when(b == 0)
        def _():
            first = (ti % nt) == 0
            last = (ti % nt) == nt - 1
            ext_p[0:HALO, :] = jnp.where(first, 0.0, ah_h[...].astype(F32) * ac_h[...].astype(F32))
            ext_d[tt:nrows, :] = jnp.where(last, 0.0, ext_d[0:HALO, :])

            def prod(r0):
                rows = pl.ds(r0, RC)
                ext_p[pl.ds(HALO + r0, RC), :] = ah[rows, :].astype(F32) * ac[rows, :].astype(F32)
            _chunks(tt, prod)
            _shifted_copies(ext_p, sh, nrows, cshifts)

            def mid(r0):
                rows = pl.ds(r0, RC)
                q = _conv_chunk(sh, wb, coffs, r0, cshifts)
                dy = dy_ref[rows, :].astype(F32)
                stage[1, rows, :] = (dy * q).astype(BF)
                ext_d[rows, :] = dy * ab[rows, :].astype(F32)
            _chunks(tt, mid)
            _conv_wgrad_chunked(dw_ref, ext_d, sh, coffs, tt, cshifts)
            _shifted_copies(ext_d, sh, nrows, ashifts)

            def fin(r0):
                rows = pl.ds(r0, RC)
                dp = _conv_chunk(sh, wb, aoffs, r0, ashifts)
                stage[0, rows, :] = (dp * ac[rows, :].astype(F32)).astype(BF)
                stage[2, rows, :] = (dp * ah[rows, :].astype(F32)).astype(BF)
            _chunks(tt, fin)

        dz_ref[...] = stage[b]

    return pl.pallas_call(
        body, name="mix_a_bwd", grid=(ntt, 3),
        in_specs=[ANY, pl.BlockSpec((None, tt, D), lambda i, b: (0, tile(i), 0)),
                  cur(0), cur(1), cur(2), halo(0), halo(2), row(KA)],
        out_specs=[pl.BlockSpec((tt, D), lambda i, b: (tile(i), b)), pl.BlockSpec((KA, D), lambda i, b: (0, 0))],
        out_shape=[jax.ShapeDtypeStruct(dz.shape, BF), jax.ShapeDtypeStruct((KA, D), F32)],
        scratch_shapes=[pltpu.VMEM((nrows + SUBLANES, D), F32), pltpu.VMEM((nrows + SUBLANES, D), F32),
                        pltpu.VMEM((max(len(cshifts), len(ashifts)), nrows, D), F32),
                        pltpu.VMEM((KA * SUBLANES, D), F32), pltpu.VMEM((3, tt, D), BF)],
        input_output_aliases={0: 0},
        compiler_params=_cp("arbitrary", "arbitrary"),
    )(dz, dy3, z, z, z, z, z, wa)


def _mix_b_bwd(dz, dy3, s, z, wc, lg, lb, S):
    T = z.shape[0]
    tt = min(S, 256)
    nt = S // tt
    ntt = T // tt
    tile, cur, halo, row = _tile_specs(tt, ntt, True)

    nrows = HALO + tt

    def body(dz_in, dy_ref, s_ref, ca, cg, ca_h, cg_h, w_ref, lg_ref, lb_ref,
             dz_ref, dw_ref, dbc_ref, dlg_ref, dlb_ref, ext_r, ext_d, sh, wb, accs, stage):
        i, b = pl.program_id(0), pl.program_id(1)
        ti = ntt - 1 - i

        @pl.when((i == 0) & (b == 0))
        def _():
            dw_ref[...] = jnp.zeros_like(dw_ref)
            dbc_ref[...] = jnp.zeros_like(dbc_ref)
            dlg_ref[...] = jnp.zeros_like(dlg_ref)
            dlb_ref[...] = jnp.zeros_like(dlb_ref)
            ext_d[...] = jnp.zeros_like(ext_d)
            ext_r[nrows:, :] = jnp.zeros((SUBLANES, D), F32)
            _fill_taps(wb, w_ref, KC)

        @pl.when(b == 0)
        def _():
            first = (ti % nt) == 0
            last = (ti % nt) == nt - 1
            ext_r[0:HALO, :] = jnp.where(first, 0.0, ca_h[...].astype(F32) * _sig(cg_h[...].astype(F32)))
            ext_d[tt:nrows, :] = jnp.where(last, 0.0, ext_d[0:HALO, :])
            accs[...] = jnp.zeros_like(accs)

            def point(r0):
                rows = pl.ds(r0, RC)
                n, r = _ln_stats(s_ref[rows, :].astype(F32))
                t = n * lg_ref[...] + lb_ref[...]
                sg = _sig(t)
                dt = dy_ref[rows, :].astype(F32) * (sg * (1.0 + t * (1.0 - sg)))
                accs[0] += dt * n
                accs[1] += dt
                ds = _ln_bwd(dt * lg_ref[...], n, r)
                accs[2] += ds
                ext_d[rows, :] = ds
                ext_r[pl.ds(HALO + r0, RC), :] = ca[rows, :].astype(F32) * _sig(cg[rows, :].astype(F32))
            _chunks(tt, point)
            dlg_ref[...] += jnp.sum(accs[0], axis=0, keepdims=True)
            dlb_ref[...] += jnp.sum(accs[1], axis=0, keepdims=True)
            dbc_ref[...] += jnp.sum(accs[2], axis=0, keepdims=True)

            _shifted_copies(ext_r, sh, nrows)
            _conv_wgrad_chunked(dw_ref, ext_d, sh, _causal_offsets(KC), tt)
            _shifted_copies(ext_d, sh, nrows)

            def conv(r0):
                rows = pl.ds(r0, RC)
                dr = _conv_chunk(sh, wb, _anticausal_offsets(KC), r0)
                cav = ca[rows, :].astype(F32)
                sgc = _sig(cg[rows, :].astype(F32))
                stage[0, rows, :] = (dr * sgc).astype(BF)
                stage[1, rows, :] = (dr * cav * sgc * (1.0 - sgc)).astype(BF)
            _chunks(tt, conv)

        dz_ref[...] = stage[b]

    vec = pl.BlockSpec((1, D), lambda i, b: (0, 0))
    return pl.pallas_call(
        body, name="mix_b_bwd", grid=(ntt, 2),
        in_specs=[ANY, pl.BlockSpec((None, tt, D), lambda i, b: (1, tile(i), 0)),
                  pl.BlockSpec((tt, D), lambda i, b: (tile(i), 0)),
                  cur(3), cur(4), halo(3), halo(4), row(KC), row(), row()],
        out_specs=[pl.BlockSpec((tt, D), lambda i, b: (tile(i), 3 + b)),
                   pl.BlockSpec((KC, D), lambda i, b: (0, 0)), vec, vec, vec],
        out_shape=[jax.ShapeDtypeStruct(dz.shape, BF), jax.ShapeDtypeStruct((KC, D), F32)]
        + [jax.ShapeDtypeStruct((1, D), F32)] * 3,
        scratch_shapes=[pltpu.VMEM((nrows + SUBLANES, D), F32), pltpu.VMEM((nrows + SUBLANES, D), F32),
                        pltpu.VMEM((SUBLANES, nrows, D), F32), pltpu.VMEM((KC * SUBLANES, D), F32),
                        pltpu.VMEM((3, RC, D), F32), pltpu.VMEM((2, tt, D), BF)],
        input_output_aliases={0: 0},
        compiler_params=_cp("arbitrary", "arbitrary"),
    )(dz, dy3, s, z, z, z, z, wc, lg, lb)


def _mix_s_bwd(dz, dy3, z, lg, lb, ws, wst, bst, S):
    T = z.shape[0]
    tt = min(S, 256)
    ntt = T // tt
    _, cur, _, row = _tile_specs(tt, ntt, False)

    def body(dz_in, dy_ref, su, sv, lg_ref, lb_ref, ws_ref, wst_ref, bst_ref,
             dz_ref, dws_ref, dbst_ref, dlg_ref, dlb_ref, u_scr, vn_scr, dvn_scr, stage):
        i, b = pl.program_id(0), pl.program_id(1)

        @pl.when((i == 0) & (b == 0))
        def _():
            dws_ref[...] = jnp.zeros_like(dws_ref)
            dbst_ref[...] = jnp.zeros_like(dbst_ref)
            dlg_ref[...] = jnp.zeros_like(dlg_ref)
            dlb_ref[...] = jnp.zeros_like(dlb_ref)

        @pl.when(b == 0)
        def _():
            u, du_dx = _gelu(su[...].astype(F32))
            v, dv_dx = _gelu(sv[...].astype(F32))
            u_scr[...] = u
            n, r = _ln_stats(v)
            vn_scr[...] = (n * lg_ref[...] + lb_ref[...]).astype(BF)
            mask = _causal_mask(False)
            mask_t = _causal_mask(True)
            for h in range(HEADS):
                wm = jnp.where(mask, ws_ref[h], 0.0).astype(BF)
                wmt = jnp.where(mask_t, wst_ref[h], 0.0).astype(BF)
                cols = slice(h * CHUNK, (h + 1) * CHUNK)
                for c in range(tt // CHUNK):
                    rows = slice(c * CHUNK, (c + 1) * CHUNK)
                    vb = vn_scr[rows, cols]
                    mixed = _dot(wm, vb) + bst_ref[:, h:h + 1]
                    dy = dy_ref[rows, cols].astype(F32)
                    dmix = dy * u_scr[rows, cols]
                    u_scr[rows, cols] = dy * mixed
                    dbst_ref[:, h:h + 1] += jnp.sum(dmix, axis=1, keepdims=True)
                    dmb = dmix.astype(BF)
                    dws_ref[h] += _dot_nt(dmb, vb)
                    dvn_scr[rows, cols] = _dot(wmt, dmb)
            stage[0] = (u_scr[...] * du_dx).astype(BF)
            dvn = dvn_scr[...]
            dlg_ref[...] += jnp.sum(dvn * n, axis=0, keepdims=True)
            dlb_ref[...] += jnp.sum(dvn, axis=0, keepdims=True)
            stage[1] = (_ln_bwd(dvn * lg_ref[...], n, r) * dv_dx).astype(BF)

        dz_ref[...] = stage[b]

    vec = pl.BlockSpec((1, D), lambda i, b: (0, 0))
    wsp = pl.BlockSpec((HEADS, CHUNK, CHUNK), lambda i, b: (0, 0, 0))
    bsp = pl.BlockSpec((CHUNK, HEADS), lambda i, b: (0, 0))
    return pl.pallas_call(
        body, name="mix_s_bwd", grid=(ntt, 2),
        in_specs=[ANY, pl.BlockSpec((None, tt, D), lambda i, b: (2, i, 0)),
                  cur(5), cur(6), row(), row(), wsp, wsp, bsp],
        out_specs=[pl.BlockSpec((tt, D), lambda i, b: (i, 5 + b)), wsp, bsp, vec, vec],
        out_shape=[jax.ShapeDtypeStruct(dz.shape, BF), jax.ShapeDtypeStruct((HEADS, CHUNK, CHUNK), F32),
                   jax.ShapeDtypeStruct((CHUNK, HEADS), F32), jax.ShapeDtypeStruct((1, D), F32),
                   jax.ShapeDtypeStruct((1, D), F32)],
        scratch_shapes=[pltpu.VMEM((tt, D), F32), pltpu.VMEM((tt, D), BF), pltpu.VMEM((tt, D), F32),
                        pltpu.VMEM((2, tt, D), BF)],
        input_output_aliases={0: 0},
        compiler_params=_cp("arbitrary", "arbitrary"),
    )(dz, dy3, z, z, lg, lb, ws, wst, bst)


def _in_proj_bwd(dz, w, x, g, dx1, dep):
    T = x.shape[0]
    nc = w.shape[2]
    tm = min(T, 1024)
    tn = 1280
    nj = nc // tn
    ep = min(tm, 128)

    def body(dz_ref, w_ref, x_ref, g_ref, dx1_ref, dep_ref, dx_ref, dg_ref, acc):
        i, k, j = pl.program_id(0), pl.program_id(1), pl.program_id(2)

        @pl.when((i == 0) & (k == 0) & (j == 0))
        def _():
            dg_ref[...] = jnp.zeros_like(dg_ref)

        @pl.when((k == 0) & (j == 0))
        def _():
            acc[...] = jnp.zeros_like(acc)

        acc[...] += _dot_nt(dz_ref[...], w_ref[...])

        @pl.when((k == NSH - 1) & (j == nj - 1))
        def _():
            def step(c, dg):
                rows = pl.ds(pl.multiple_of(c * ep, ep), ep)
                dx, dgc = _rms_bwd(acc[rows, :], x_ref[rows, :], g_ref[...])
                dx_ref[rows, :] = dx1_ref[rows, :] + dx
                return dg + dgc
            dg_ref[...] += lax.fori_loop(0, tm // ep, step, jnp.zeros((1, D), F32))

    rowblk = pl.BlockSpec((tm, D), lambda i, k, j: (i, 0))
    vec = pl.BlockSpec((1, D), lambda i, k, j: (0, 0))
    return pl.pallas_call(
        body, name="in_proj_bwd", grid=(T // tm, NSH, nj),
        in_specs=[pl.BlockSpec((tm, tn), lambda i, k, j: (i, k * nj + j)),
                  pl.BlockSpec((None, D, tn), lambda i, k, j: (k, 0, j)), rowblk, vec, rowblk, ANY],
        out_specs=[rowblk, vec],
        out_shape=[jax.ShapeDtypeStruct((T, D), F32), jax.ShapeDtypeStruct((1, D), F32)],
        scratch_shapes=[pltpu.VMEM((tm, D), F32)],
        compiler_params=_cp("arbitrary", "arbitrary", "arbitrary"),
    )(dz, w, x, g, dx1, dep)


def _layer_fwd(x, p, S, dep, late):
    h, z = _in_proj(x, p["g_mix_pre"], p["w_in"], dep)
    ya = _mix_a_fwd(z, p["conv_a_w"], S)
    yc, s = _mix_b_fwd(z, p["conf_dw_w"], p["conf_dw_b"], p["conf_ln_g"], p["conf_ln_b"], S)
    ys = _mix_s_fwd(z, p["sgu_ln_g"], p["sgu_ln_b"], p["sgu_ws"], p["sgu_bt"], S)
    more, dep2 = late(ys)
    p.update(more)
    p3, merged, m, x1 = _mix_out_fwd(ya, yc, ys, z, x, p["w_branch"], p["w_out"], p["g_mix_post"], dep2)
    h2, a, f, x2 = _ffn_fwd(x1, p["g_ffn_pre"], p["w_ff1"], p["w_ff2"], p["g_ffn_post"])
    saved = dict(x=x, h=h, z=z, ya=ya, yc=yc, ys=ys, s=s, p3=p3, merged=merged, m=m, x1=x1, h2=h2, a=a, f=f)
    return x2, saved


def _layer_bwd(dx2, p, sv, S, dep, early):
    T = dx2.shape[0]
    bt = min(T, 512)
    nt = T // bt
    rk = D // NSH
    df, da, dx1, dg_ffn_post, dg_ffn_pre = _ffn_bwd(dx2, sv["f"], p["g_ffn_post"], sv["a"], p["w_ff2"],
                                                    p["w_ff1"], sv["x1"], p["g_ffn_pre"], dep)
    dw_ff2 = _wgrad("wgrad_ff2", (sv["a"], df), (NSH, nt),
                    [pl.BlockSpec((bt, D), lambda k, t: (t, k)), pl.BlockSpec((bt, D), lambda k, t: (t, 0))],
                    pl.BlockSpec((None, D, D), lambda k, t: (k, 0, 0)),
                    jax.ShapeDtypeStruct((NSH, D, D), BF), (D, D), relu2=True)
    dw_ff1 = _wgrad("wgrad_ff1", (sv["h2"], da), (NSH, nt),
                    [pl.BlockSpec((bt, D), lambda k, t: (t, 0)), pl.BlockSpec((bt, D), lambda k, t: (t, k))],
                    pl.BlockSpec((None, D, D), lambda k, t: (k, 0, 0)),
                    jax.ShapeDtypeStruct((NSH, D, D), BF), (D, D))
    dm, dp3, dy3, dz, dg_mix_post = _mix_out_bwd(dx1, sv["m"], p["g_mix_post"], p["w_out"], sv["p3"], sv["z"],
                                                 p["w_branch"])
    dw_out = _wgrad("wgrad_out", (sv["merged"], dm), (nt,),
                    [pl.BlockSpec((bt, D), lambda t: (t, 0)), pl.BlockSpec((bt, D), lambda t: (t, 0))],
                    pl.BlockSpec((D, D), lambda t: (0, 0)),
                    jax.ShapeDtypeStruct((D, D), BF), (D, D)).reshape(NSH, rk, D)
    ysp = lambda n: pl.BlockSpec((bt, D), lambda b, t: (jnp.where(b == n, t, 0), 0))
    dw_br = _wgrad("wgrad_branch", (sv["ya"], sv["yc"], sv["ys"], dp3), (3, nt),
                   [ysp(0), ysp(1), ysp(2), pl.BlockSpec((None, bt, D), lambda b, t: (b, t, 0))],
                   pl.BlockSpec((NSH, None, rk, D), lambda b, t: (0, b, 0, 0)),
                   jax.ShapeDtypeStruct((NSH, 3, rk, D), BF), (D, D), pick=lambda: pl.program_id(0))
    dep2 = early([dw_br, dw_out, dw_ff1, dw_ff2])
    dz, dwa = _mix_a_bwd(dz, dy3, sv["z"], p["conv_a_w"], S)
    dz, dwc, dbc, dclg, dclb = _mix_b_bwd(dz, dy3, sv["s"], sv["z"], p["conf_dw_w"], p["conf_ln_g"],
                                          p["conf_ln_b"], S)
    dz, dws, dbst, dslg, dslb = _mix_s_bwd(dz, dy3, sv["z"], p["sgu_ln_g"], p["sgu_ln_b"], p["sgu_ws"],
                                           p["sgu_wst"], p["sgu_bt"], S)
    dx, dg_mix_pre = _in_proj_bwd(dz, p["w_in"], sv["x"], p["g_mix_pre"], dx1, dep2)
    tn = 1280
    nj = p["w_in"].shape[2] // tn
    dw_in = _wgrad("wgrad_in", (sv["h"], dz), (NSH, nj, nt),
                   [pl.BlockSpec((bt, D), lambda k, j, t: (t, 0)),
                    pl.BlockSpec((bt, tn), lambda k, j, t: (t, k * nj + j))],
                   pl.BlockSpec((None, D, tn), lambda k, j, t: (k, 0, j)),
                   jax.ShapeDtypeStruct(p["w_in"].shape, BF), (D, tn))
    tril = jnp.tril(jnp.ones((CHUNK, CHUNK), bool))
    small = dict(norm_mix_pre=dg_mix_pre, norm_mix_post=dg_mix_post, norm_ffn_pre=dg_ffn_pre,
                 norm_ffn_post=dg_ffn_post, conv_a_w=dwa, conf_dw_w=dwc, conf_dw_b=dbc, conf_ln_g=dclg,
                 conf_ln_b=dclb, sgu_ln_g=dslg, sgu_ln_b=dslb,
                 sgu_ws=jnp.where(tril[None], dws, 0.0), sgu_b=dbst.T)
    big = dict(w_in=dw_in, w_branch=dw_br, w_out=dw_out, w_ff1=dw_ff1, w_ff2=dw_ff2)
    return dx, big, small


SMALL_NAMES = ("norm_mix_pre", "norm_mix_post", "norm_ffn_pre", "norm_ffn_post", "conv_a_w", "conf_dw_w",
               "conf_dw_b", "conf_ln_g", "conf_ln_b", "sgu_ln_g", "sgu_ln_b", "sgu_b", "sgu_ws")
SMALL_ROWS = dict(norm_mix_pre=1, norm_mix_post=1, norm_ffn_pre=1, norm_ffn_post=1, conv_a_w=KA, conf_dw_w=KC,
                  conf_dw_b=1, conf_ln_g=1, conf_ln_b=1, sgu_ln_g=1, sgu_ln_b=1, sgu_b=1, sgu_ws=CHUNK)
def _pad8(r):
    return -(-r // SUBLANES) * SUBLANES


PACK_ROWS = sum(_pad8(r) for r in SMALL_ROWS.values())


def _pack_small(d):
    parts = []
    for n in SMALL_NAMES:
        r = SMALL_ROWS[n]
        parts.append(jnp.pad(d[n].reshape(r, D).astype(F32), ((0, _pad8(r) - r), (0, 0))))
    return jnp.concatenate(parts, axis=0)


def _unpack_small(a, shapes):
    out, r = {}, 0
    for n in SMALL_NAMES:
        out[n] = a[:, r:r + SMALL_ROWS[n]].reshape((a.shape[0],) + tuple(shapes[n]))
        r += _pad8(SMALL_ROWS[n])
    return out


def _me():
    return lax.axis_index("x"), lax.axis_index("y"), lax.axis_index("c")


def _slab(ref, q, a, h=None):
    r = ref.shape[1]
    rows = slice(None) if h is None else pl.ds(h * (r // 2), r // 2)
    return ref.at[pl.ds(q * a, a), rows, :]


def _rows(ref, h):
    r = ref.shape[-2]
    lead = (slice(None),) * (len(ref.shape) - 2)
    return ref.at[lead + (pl.ds(h * (r // 2), r // 2), slice(None))]


def _rcopy(src, dst, sems, idx, dev):
    return pltpu.make_async_remote_copy(src_ref=src, dst_ref=dst, send_sem=sems[0].at[idx], recv_sem=sems[1].at[idx],
                                        device_id=dev, device_id_type=MESH)


def _send_halves_to_sibling(parts):
    n = len(parts)

    def body(*refs):
        src, dst = refs[:n], refs[n:2 * n]
        sems = refs[2 * n:2 * n + 2]
        x, y, c = _me()
        cps = [_rcopy(_rows(src[i], 1 - c), dst[i], sems, i, (x, y, 1 - c)) for i in range(n)]
        for cp in cps:
            cp.start()
        for cp in cps:
            cp.wait()

    outs = [jax.ShapeDtypeStruct((p.shape[0], p.shape[1] // 2, p.shape[2]), p.dtype) for p in parts]
    return pl.pallas_call(
        body, name="pair_exchange", in_specs=[ANY] * n, out_specs=[ANY] * n, out_shape=outs,
        scratch_shapes=[pltpu.SemaphoreType.DMA((n,)), pltpu.SemaphoreType.DMA((n,))],
    )(*parts)


def _pair_add(part, sib, c):
    A, R, C = part.shape
    hr = R // 2
    br = min(hr, 512)
    nb = hr // br

    def body(c_ref, p_ref, s_ref, o_ref):
        o_ref[...] = (p_ref[...].astype(F32) + s_ref[...].astype(F32)).astype(BF)

    return pl.pallas_call(
        body, name="pair_add",
        grid_spec=pltpu.PrefetchScalarGridSpec(
            num_scalar_prefetch=1, grid=(A, nb),
            in_specs=[pl.BlockSpec((None, br, C), lambda a, i, c_ref: (a, c_ref[0] * nb + i, 0)),
                      pl.BlockSpec((None, br, C), lambda a, i, c_ref: (a, i, 0))],
            out_specs=pl.BlockSpec((None, br, C), lambda a, i, c_ref: (a, i, 0))),
        out_shape=jax.ShapeDtypeStruct((A, hr, C), BF),
        compiler_params=_cp("arbitrary", "arbitrary"),
    )(c, part, sib)


def _other_chips(x, y):
    return [(1 - x, y), (x, 1 - y), (1 - x, 1 - y)]


def _split_call(name, copies, srcs, lands, sems=None, after=None):
    n, m = len(srcs), len(lands)
    hbm = lambda t: pltpu.HBM(t.shape, t.dtype)
    pin = lambda t: pltpu.with_memory_space_constraint(t, pltpu.HBM)
    thru = [hbm(t) for t in srcs] + [hbm(t) for t in lands]
    sem_spec = pl.BlockSpec(memory_space=pltpu.SEMAPHORE)
    effect = pltpu.CompilerParams(has_side_effects=pltpu.SideEffectType.DATAFLOW_SIDE_EFFECTING)
    if sems is None:
        def start_body(*refs):
            src, land = refs[:n], refs[n:n + m]
            ssem, rsem = refs[n + m], refs[n + m + 1]
            token = refs[-1]
            cps = copies(src, land, (ssem, rsem))
            for cp in cps:
                cp.start()
            token[...] = jnp.zeros_like(token)

        ncp = copies.count
        out = pl.pallas_call(
            start_body, name=name,
            out_shape=(pltpu.SemaphoreType.DMA((ncp,)), pltpu.SemaphoreType.DMA((ncp,)), *thru,
                       jax.ShapeDtypeStruct((8, 128), F32)),
            in_specs=[ANY] * (n + m),
            out_specs=(sem_spec, sem_spec, *([ANY] * (n + m)), pl.BlockSpec(memory_space=pltpu.VMEM)),
            input_output_aliases={i: 2 + i for i in range(n + m)},
            compiler_params=effect,
        )(*[pin(t) for t in srcs], *[pin(t) for t in lands])
        return out[0], out[1], list(out[2:2 + n]), list(out[2 + n:2 + n + m]), out[-1]

    def wait_body(*refs):
        src, land = refs[:n], refs[n:n + m]
        ssem, rsem = refs[n + m], refs[n + m + 1]
        for cp in copies(src, land, (ssem, rsem)):
            cp.wait_send()
            cp.wait_recv()

    out = pl.pallas_call(
        wait_body, name=name, out_shape=tuple(thru),
        in_specs=[ANY] * (n + m) + [sem_spec, sem_spec, ANY],
        out_specs=tuple([ANY] * (n + m)),
        input_output_aliases={i: i for i in range(n + m)},
        compiler_params=effect,
    )(*srcs, *lands, sems[0], sems[1], after)
    return list(out[:n]), list(out[n:])


def _cast_into(w, land, layer, kidx, dep):
    _, a, R, C = w.shape
    br = R
    while br * C > 256 * 1024 and br % 32 == 0:
        br //= 2

    def body(k_ref, w_ref, land_ref, dep_ref, o_ref):
        o_ref[...] = w_ref[...].astype(o_ref.dtype)

    return pl.pallas_call(
        body, name="cast_into",
        grid_spec=pltpu.PrefetchScalarGridSpec(
            num_scalar_prefetch=1, grid=(a, R // br),
            in_specs=[pl.BlockSpec((None, None, br, C), lambda e, i, k: (layer, e, i, 0)), ANY, ANY],
            out_specs=pl.BlockSpec((None, br, C), lambda e, i, k: (k[0] * a + e, i, 0))),
        out_shape=jax.ShapeDtypeStruct(land.shape, land.dtype), input_output_aliases={2: 0},
        compiler_params=_cp("arbitrary", "arbitrary"),
    )(kidx, w, land, dep)


class _GatherCopies:
    def __init__(self, n, halves=True):
        self.n, self.count, self.halves = n, 3 * n, halves

    def __call__(self, src, land, sems):
        x, y, c = _me()
        k = 2 * x + y
        cps = []
        for j, (qx, qy) in enumerate(_other_chips(x, y)):
            for i in range(self.n):
                mine = _slab(land[i], k, land[i].shape[0] // NSH, c if self.halves else None)
                cps.append(_rcopy(mine, mine, sems, j * self.n + i, (qx, qy, c)))
        return cps


def _gather_finish(lands):
    n = len(lands)

    def body(*refs):
        dst = refs[n:2 * n]
        sems = refs[2 * n:2 * n + 2]
        x, y, c = _me()
        av = [d.shape[0] // NSH for d in dst]
        cps = []
        for j, (qx, qy) in enumerate(_other_chips(x, y)):
            for i in range(n):
                got = _slab(dst[i], 2 * qx + qy, av[i], c)
                cps.append(_rcopy(got, got, sems, j * n + i, (x, y, 1 - c)))
        for cp in cps:
            cp.start()
        for j, (qx, qy) in enumerate(_other_chips(x, y)):
            for i in range(n):
                other = _slab(dst[i], 2 * qx + qy, av[i], 1 - c)
                _rcopy(other, other, sems, j * n + i, (x, y, c)).wait_recv()
        for cp in cps:
            cp.wait_send()

    return pl.pallas_call(
        body, name="gather_finish", in_specs=[ANY] * n, out_specs=[ANY] * n,
        out_shape=[jax.ShapeDtypeStruct(t.shape, t.dtype) for t in lands],
        input_output_aliases={i: i for i in range(n)},
        scratch_shapes=[pltpu.SemaphoreType.DMA((3 * n,)), pltpu.SemaphoreType.DMA((3 * n,))],
    )(*lands)


class _ScatterCopies:
    def __init__(self, n):
        self.n, self.count = n, 3 * n

    def __call__(self, src, land, sems):
        x, y, c = _me()
        k = 2 * x + y
        cps = []
        for j, (qx, qy) in enumerate(_other_chips(x, y)):
            for i in range(self.n):
                a = src[i].shape[0] // NSH
                cps.append(_rcopy(_slab(src[i], 2 * qx + qy, a), _slab(land[i], k, a), sems, j * self.n + i,
                                  (qx, qy, c)))
        return cps


def _sum_chips(own, rcv, acc, layer, nlayers, idx):
    A, hr, C = rcv.shape
    a = A // NSH
    br = min(hr, 512)
    nb = hr // br

    def body(*refs):
        r0, r1, r2, r3 = refs[1:5]
        o_ref = refs[-1]
        o_ref[...] = ((r0[...].astype(F32) + r1[...].astype(F32)) + r2[...].astype(F32)) + r3[...].astype(F32)

    slot = lambda s: pl.BlockSpec((None, br, C), lambda e, i, ix: (ix[s] * a + e, i, 0))
    ops = [own, rcv, rcv, rcv]
    in_specs = [slot(0), slot(1), slot(2), slot(3)]
    aliases = {}
    if acc is not None:
        ops.append(acc)
        in_specs.append(ANY)
        aliases = {5: 0}
    return pl.pallas_call(
        body, name="sum_chips",
        grid_spec=pltpu.PrefetchScalarGridSpec(
            num_scalar_prefetch=1, grid=(a, nb), in_specs=in_specs,
            out_specs=pl.BlockSpec((None, None, br, C), lambda e, i, ix: (layer, e, ix[4] * nb + i, 0))),
        out_shape=jax.ShapeDtypeStruct((nlayers, a, 2 * hr, C), F32), input_output_aliases=aliases,
        compiler_params=_cp("arbitrary", "arbitrary"),
    )(idx, *ops)


def _join_halves(fulls):
    n = len(fulls)

    def body(*refs):
        buf = refs[n:2 * n]
        sems = refs[2 * n:2 * n + 2]
        x, y, c = _me()
        cps = [_rcopy(_rows(buf[i], c), _rows(buf[i], c), sems, i, (x, y, 1 - c)) for i in range(n)]
        for cp in cps:
            cp.start()
        for i in range(n):
            _rcopy(_rows(buf[i], 1 - c), _rows(buf[i], 1 - c), sems, i, (x, y, c)).wait_recv()
        for cp in cps:
            cp.wait_send()

    return pl.pallas_call(
        body, name="join_halves", in_specs=[ANY] * n, out_specs=[ANY] * n,
        out_shape=[jax.ShapeDtypeStruct(t.shape, t.dtype) for t in fulls],
        input_output_aliases={i: i for i in range(n)},
        scratch_shapes=[pltpu.SemaphoreType.DMA((n,)), pltpu.SemaphoreType.DMA((n,))],
    )(*fulls)


def _small_blocks(hr):
    br = hr
    while br > 512 and br % 16 == 0:
        br //= 2
    return br, hr // br


def _pair_sum_slot(part, sib, ck):
    R, C = part.shape
    hr = R // 2
    br, nb = _small_blocks(hr)

    def body(ix, p_ref, s_ref, o_ref):
        o_ref[...] = p_ref[...] + s_ref[...]

    return pl.pallas_call(
        body, name="pair_sum_slot",
        grid_spec=pltpu.PrefetchScalarGridSpec(
            num_scalar_prefetch=1, grid=(nb,),
            in_specs=[pl.BlockSpec((br, C), lambda i, ix: (ix[0] * nb + i, 0)),
                      pl.BlockSpec((br, C), lambda i, ix: (i, 0))],
            out_specs=pl.BlockSpec((None, br, C), lambda i, ix: (ix[1], i, 0))),
        out_shape=jax.ShapeDtypeStruct((NSH, hr, C), F32),
        compiler_params=_cp("arbitrary"),
    )(ck, part, sib)


def _sum_slots(slots, ck):
    _, hr, C = slots.shape
    br, nb = _small_blocks(hr)

    def body(ix, s_ref, o_ref):
        o_ref[...] = ((s_ref[0] + s_ref[1]) + s_ref[2]) + s_ref[3]

    return pl.pallas_call(
        body, name="sum_slots",
        grid_spec=pltpu.PrefetchScalarGridSpec(
            num_scalar_prefetch=1, grid=(nb,),
            in_specs=[pl.BlockSpec((NSH, br, C), lambda i, ix: (0, i, 0))],
            out_specs=pl.BlockSpec((br, C), lambda i, ix: (ix[0] * nb + i, 0))),
        out_shape=jax.ShapeDtypeStruct((2 * hr, C), F32),
        compiler_params=_cp("arbitrary"),
    )(ck, slots)


def _adamw(w, g, m, v):
    shape = w.shape
    C = shape[-1]
    R = shape[-2]
    A = 1
    for s in shape[:-2]:
        A *= s
    br = R
    while br * C > 256 * 1024 and br % 16 == 0:
        br //= 2
    c1 = 1.0 / (1.0 - ADAM_B1 ** ADAM_STEP)
    c2 = 1.0 / (1.0 - ADAM_B2 ** ADAM_STEP)

    def body(w_ref, g_ref, m_ref, v_ref, d_ref, nm_ref, nv_ref):
        gv = g_ref[...]
        nm = ADAM_B1 * m_ref[...] + (1.0 - ADAM_B1) * gv
        nv = ADAM_B2 * v_ref[...] + (1.0 - ADAM_B2) * (gv * gv)
        nm_ref[...] = nm
        nv_ref[...] = nv
        d_ref[...] = -ADAM_LR * ((nm * c1) / (jnp.sqrt(nv * c2) + ADAM_EPS) + ADAM_WD * w_ref[...])

    blk = pl.BlockSpec((None, br, C), lambda a, i: (a, i, 0))
    outs = pl.pallas_call(
        body, name="adamw", grid=(A, R // br), in_specs=[blk] * 4, out_specs=[blk] * 3,
        out_shape=[jax.ShapeDtypeStruct((A, R, C), F32)] * 3,
        compiler_params=_cp("arbitrary", "arbitrary"),
    )(*(t.reshape(A, R, C) for t in (w, g, m, v)))
    return tuple(o.reshape(shape) for o in outs)


WEIGHTS = ("norm_mix_pre", "norm_mix_post", "norm_ffn_pre", "norm_ffn_post", "w_in", "conv_a_w", "conf_dw_w",
           "conf_dw_b", "conf_ln_g", "conf_ln_b", "sgu_ln_g", "sgu_ln_b", "sgu_ws", "sgu_b", "w_branch", "w_out",
           "w_ff1", "w_ff2")
BIG = ("w_in", "w_branch", "w_out", "w_ff1", "w_ff2")
CONV_ROWS = 48


def kernel(x, norm_mix_pre, norm_mix_post, norm_ffn_pre, norm_ffn_post, w_in, conv_a_w, conf_dw_w, conf_dw_b, conf_ln_g, conf_ln_b, sgu_ln_g, sgu_ln_b, sgu_ws, sgu_b, w_branch, w_out, w_ff1, w_ff2, loss_target, m_norm_mix_pre, m_norm_mix_post, m_norm_ffn_pre, m_norm_ffn_post, m_w_in, m_conv_a_w, m_conf_dw_w, m_conf_dw_b, m_conf_ln_g, m_conf_ln_b, m_sgu_ln_g, m_sgu_ln_b, m_sgu_ws, m_sgu_b, m_w_branch, m_w_out, m_w_ff1, m_w_ff2, v_norm_mix_pre, v_norm_mix_post, v_norm_ffn_pre, v_norm_ffn_post, v_w_in, v_conv_a_w, v_conf_dw_w, v_conf_dw_b, v_conf_ln_g, v_conf_ln_b, v_sgu_ln_g, v_sgu_ln_b, v_sgu_ws, v_sgu_b, v_w_branch, v_w_out, v_w_ff1, v_w_ff2):
    w = dict(norm_mix_pre=norm_mix_pre, norm_mix_post=norm_mix_post, norm_ffn_pre=norm_ffn_pre,
             norm_ffn_post=norm_ffn_post, w_in=w_in, conv_a_w=conv_a_w, conf_dw_w=conf_dw_w, conf_dw_b=conf_dw_b,
             conf_ln_g=conf_ln_g, conf_ln_b=conf_ln_b, sgu_ln_g=sgu_ln_g, sgu_ln_b=sgu_ln_b, sgu_ws=sgu_ws,
             sgu_b=sgu_b, w_branch=w_branch, w_out=w_out, w_ff1=w_ff1, w_ff2=w_ff2)
    mom = dict(norm_mix_pre=m_norm_mix_pre, norm_mix_post=m_norm_mix_post, norm_ffn_pre=m_norm_ffn_pre,
               norm_ffn_post=m_norm_ffn_post, w_in=m_w_in, conv_a_w=m_conv_a_w, conf_dw_w=m_conf_dw_w,
               conf_dw_b=m_conf_dw_b, conf_ln_g=m_conf_ln_g, conf_ln_b=m_conf_ln_b, sgu_ln_g=m_sgu_ln_g,
               sgu_ln_b=m_sgu_ln_b, sgu_ws=m_sgu_ws, sgu_b=m_sgu_b, w_branch=m_w_branch, w_out=m_w_out,
               w_ff1=m_w_ff1, w_ff2=m_w_ff2)
    var = dict(norm_mix_pre=v_norm_mix_pre, norm_mix_post=v_norm_mix_post, norm_ffn_pre=v_norm_ffn_pre,
               norm_ffn_post=v_norm_ffn_post, w_in=v_w_in, conv_a_w=v_conv_a_w, conf_dw_w=v_conf_dw_w,
               conf_dw_b=v_conf_dw_b, conf_ln_g=v_conf_ln_g, conf_ln_b=v_conf_ln_b, sgu_ln_g=v_sgu_ln_g,
               sgu_ln_b=v_sgu_ln_b, sgu_ws=v_sgu_ws, sgu_b=v_sgu_b, w_branch=v_w_branch, w_out=v_w_out,
               w_ff1=v_w_ff1, w_ff2=v_w_ff2)
    L = w_in.shape[0]
    nseq, S, _ = x.shape
    T = nseq * S
    rk = D // NSH
    mx, my, mc = _me()
    k_chip = 2 * mx + my

    big_src = [w_in.reshape(L, 1, D, w_in.shape[2]), w_branch, w_out.reshape(L, 1, rk, D),
               w_ff1.reshape(L, 1, D, w_ff1.shape[2]), w_ff2.reshape(L, 1, w_ff2.shape[1], D)]
    kidx = jnp.reshape(k_chip, (1,)).astype(jnp.int32)
    conv_src = jnp.concatenate(
        [jnp.pad(conv_a_w, ((0, 0), (0, SUBLANES - KA), (0, 0))), jnp.pad(conf_dw_w, ((0, 0), (0, 1), (0, 0))),
         jnp.zeros((L, CONV_ROWS - SUBLANES - KC - 1, rk), F32)], axis=1)[None]

    def early_params(l, g_in, conv_full):
        return dict(
            g_mix_pre=norm_mix_pre[l][None], g_mix_post=norm_mix_post[l][None], g_ffn_pre=norm_ffn_pre[l][None],
            g_ffn_post=norm_ffn_post[l][None], w_in=g_in, conv_a_w=conv_full[l, :KA],
            conf_dw_w=conv_full[l, SUBLANES:SUBLANES + KC], conf_dw_b=conf_dw_b[l][None],
            conf_ln_g=conf_ln_g[l][None], conf_ln_b=conf_ln_b[l][None], sgu_ln_g=sgu_ln_g[l][None],
            sgu_ln_b=sgu_ln_b[l][None], sgu_ws=sgu_ws[l], sgu_wst=jnp.swapaxes(sgu_ws[l], 1, 2),
            sgu_bt=sgu_b[l].T)

    def late_params(gathered):
        g_br, g_out, g_ff1, g_ff2 = gathered
        return dict(w_branch=g_br.reshape(NSH, 3, rk, D), w_out=g_out.reshape(D, D), w_ff1=g_ff1,
                    w_ff2=g_ff2.reshape(NSH * w_ff2.shape[1], D))

    def gather_start(name, srcs, l, dep):
        lands = [_cast_into(s, lax.empty((NSH * s.shape[1],) + s.shape[2:], F32 if s is conv_src else BF), l, kidx,
                            dep) for s in srcs]
        return _split_call(name, _GatherCopies(len(lands)), [], lands)

    def gather_land(name, flight, after):
        ssem, rsem, _, lands, _ = flight
        _, lands = _split_call(name, _GatherCopies(len(lands)), [], lands, (ssem, rsem), after)
        return _gather_finish(lands)

    zero_tok = jnp.zeros((8, 128), F32)
    xt = x.reshape(T, D)
    layers, saved = [], []
    head = gather_start("gather_start_0a", [big_src[0], conv_src], 0, kidx)
    conv_full = None
    for l in range(L):
        got = gather_land(f"gather_wait_{l}a", head, xt)
        g_in = got[0]
        if l == 0:
            conv_full = got[1].reshape(NSH, L, CONV_ROWS, rk).transpose(1, 2, 0, 3).reshape(L, CONV_ROWS, D)
        tail = gather_start(f"gather_start_{l}b", big_src[1:], l, g_in)
        nxt = {}

        def late(after, l=l, tail=tail, nxt=nxt):
            more = late_params(gather_land(f"gather_wait_{l}b", tail, after))
            if l + 1 == L:
                return more, zero_tok
            nxt["head"] = gather_start(f"gather_start_{l + 1}a", big_src[:1], l + 1, more["w_ff1"])
            return more, nxt["head"][4]

        p = early_params(l, g_in, conv_full)
        xt, sv = _layer_fwd(xt, p, S, tail[4], late)
        head = nxt.get("head")
        layers.append(p)
        saved.append(sv)
    dx, loss_row = _loss_head(xt, loss_target.reshape(T, D))
    loss = lax.psum(loss_row[0, 0], ("x", "y", "c"))

    c_arr = jnp.reshape(mc, (1,)).astype(jnp.int32)
    idx = jnp.stack([k_chip, k_chip ^ 2, k_chip ^ 1, k_chip ^ 3, mc]).astype(jnp.int32)
    fulls = {n: None for n in BIG}
    smalls = [None] * L

    def scatter_start(name, parts):
        sib = _send_halves_to_sibling(parts)
        sums = [_pair_add(p, s, c_arr) for p, s in zip(parts, sib)]
        rcv = [lax.empty(s.shape, s.dtype) for s in sums]
        return _split_call(name, _ScatterCopies(len(sums)), sums, rcv)

    def scatter_land(name, fl, names, l, after):
        ssem, rsem, sums, rcv, _ = fl
        sums, rcv = _split_call(name, _ScatterCopies(len(sums)), sums, rcv, (ssem, rsem), after)
        for n, o, r in zip(names, sums, rcv):
            fulls[n] = _sum_chips(o, r, fulls[n], l, L, idx)

    pending = []
    dep = zero_tok
    for l in reversed(range(L)):
        mine = {}

        def early(parts, l=l, mine=mine):
            br, rest = parts[0], parts[1:]
            mine["a"] = scatter_start(f"scatter_start_{l}a", [br.reshape(NSH * 3, rk, D), *rest])
            return mine["a"][4]

        dx, big, small = _layer_bwd(dx, layers[l], saved[l], S, dep, early)
        smalls[l] = _pack_small(small)
        for args in pending:
            scatter_land(*args, dx)
        last = scatter_start(f"scatter_start_{l}b", [big["w_in"]])
        pending = [(f"scatter_wait_{l}a", mine["a"], BIG[1:], l), (f"scatter_wait_{l}b", last, BIG[:1], l)]
        dep = last[4]
    packed = jnp.concatenate(smalls, axis=0)
    nrow = packed.shape[0]
    ck = jnp.stack([mc, k_chip]).astype(jnp.int32)
    (sib,) = _send_halves_to_sibling([packed.reshape(1, nrow, D)])
    slots = _pair_sum_slot(packed, sib.reshape(nrow // 2, D), ck)
    small_flight = _split_call("small_start", _GatherCopies(1, halves=False), [], [slots])

    for args in pending:
        scatter_land(*args, small_flight[4])
    full = _join_halves([fulls[n] for n in BIG])
    grads = {n: f.reshape(w[n].shape) for n, f in zip(BIG, full)}
    delta, new_m, new_v = {}, {}, {}
    for n in BIG:
        delta[n], new_m[n], new_v[n] = _adamw(w[n], grads[n], mom[n], var[n])

    _, (slots,) = _split_call("small_wait", _GatherCopies(1, halves=False), [], small_flight[3],
                              (small_flight[0], small_flight[1]), delta[BIG[-1]])
    (small_sum,) = _join_halves([_sum_slots(slots, ck).reshape(1, 1, nrow, D)])
    shapes = {n: (w[n].shape[1:] if n not in ("conv_a_w", "conf_dw_w") else (w[n].shape[1], D)) for n in SMALL_NAMES}
    sg = _unpack_small(small_sum.reshape(L, PACK_ROWS, D), shapes)
    for n in SMALL_NAMES:
        if n in ("conv_a_w", "conf_dw_w"):
            grads[n] = lax.dynamic_slice_in_dim(sg[n], k_chip * rk, rk, axis=2)
        else:
            grads[n] = sg[n]

    for n in SMALL_NAMES:
        sh = w[n].shape
        flat = (sh[0] * sh[1], sh[2]) if n in ("conv_a_w", "conf_dw_w") else (-1, D)
        d, nm, nv = _adamw(*(t.reshape(flat) for t in (w[n], grads[n], mom[n], var[n])))
        delta[n], new_m[n], new_v[n] = d.reshape(sh), nm.reshape(sh), nv.reshape(sh)

    return (loss, dx.reshape(x.shape), *[grads[n] for n in WEIGHTS], *[delta[n] for n in WEIGHTS],
            *[new_m[n] for n in WEIGHTS], *[new_v[n] for n in WEIGHTS])
```

```python
import functools

import jax
import jax.numpy as jnp
from jax import lax
from jax.experimental import pallas as pl
from jax.experimental.pallas import tpu as pltpu

D = 1024
HEADS = 8
CHUNK = 128
KA = 3
KC = 31
HALO = 32
SUBLANES = 8
MIX_TILE = 512
WGRAD_TILE = 1024
NSH = 4
NDEV = 8
EPS = 1e-6
BF = jnp.bfloat16
F32 = jnp.float32
VMEM_LIMIT = 56 * 1024 * 1024

ADAM_LR = 0.001
ADAM_B1 = 0.9
ADAM_B2 = 0.999
ADAM_EPS = 1e-08
ADAM_WD = 0.01
ADAM_STEP = 10

MESH = pl.DeviceIdType.MESH
ANY = pl.BlockSpec(memory_space=pl.ANY)


def _cp(*sem):
    return pltpu.CompilerParams(dimension_semantics=sem, vmem_limit_bytes=VMEM_LIMIT)


def _sig(x):
    return 1.0 / (1.0 + jnp.exp(-x))


_GC = 0.7978845608028654


def _gelu(x):
    x2 = x * x
    t = jnp.tanh(_GC * x * (1.0 + 0.044715 * x2))
    y = 0.5 * x * (1.0 + t)
    dy = 0.5 * (1.0 + t) + 0.5 * x * (1.0 - t * t) * _GC * (1.0 + 3.0 * 0.044715 * x2)
    return y, dy


def _rms_fwd(x, g):
    r = lax.rsqrt(jnp.mean(x * x, axis=-1, keepdims=True) + EPS)
    return x * r * g


def _rms_bwd(dy, x, g):
    r = lax.rsqrt(jnp.mean(x * x, axis=-1, keepdims=True) + EPS)
    xn = x * r
    dyg = dy * g
    dx = r * (dyg - xn * jnp.mean(dyg * xn, axis=-1, keepdims=True))
    return dx, jnp.sum(dy * xn, axis=0, keepdims=True)


def _ln_stats(x):
    mu = jnp.mean(x, axis=-1, keepdims=True)
    xc = x - mu
    r = lax.rsqrt(jnp.mean(xc * xc, axis=-1, keepdims=True) + EPS)
    return xc * r, r


def _ln_bwd(dn, n, r):
    return r * (dn - jnp.mean(dn, axis=-1, keepdims=True) - n * jnp.mean(dn * n, axis=-1, keepdims=True))


def _dot(a, b):
    return jnp.dot(a, b, preferred_element_type=F32)


def _dot_nt(a, b):
    return lax.dot_general(a, b, (((1,), (1,)), ((), ())), preferred_element_type=F32)


def _dot_tn(a, b):
    return lax.dot_general(a, b, (((0,), (0,)), ((), ())), preferred_element_type=F32)


def _in_proj(x, g, w, dep):
    T = x.shape[0]
    nc = w.shape[2]
    tm = min(T, 1024)
    tn = nc
    nj = nc // tn

    def body(x_ref, g_ref, w_ref, dep_ref, h_ref, z_ref, h_scr):
        @pl.when((pl.program_id(1) == 0) & (pl.program_id(2) == 0))
        def _():
            h = _rms_fwd(x_ref[...], g_ref[...]).astype(BF)
            h_scr[...] = h
            h_ref[...] = h
        z_ref[...] = _dot(h_scr[...], w_ref[...]).astype(BF)

    return pl.pallas_call(
        body, name="in_proj", grid=(T // tm, NSH, nj),
        in_specs=[pl.BlockSpec((tm, D), lambda i, k, j: (i, 0)),
                  pl.BlockSpec((1, D), lambda i, k, j: (0, 0)),
                  pl.BlockSpec((None, D, tn), lambda i, k, j: (k, 0, j)), ANY],
        out_specs=[pl.BlockSpec((tm, D), lambda i, k, j: (i, 0)),
                   pl.BlockSpec((tm, tn), lambda i, k, j: (i, k * nj + j))],
        out_shape=[jax.ShapeDtypeStruct((T, D), BF), jax.ShapeDtypeStruct((T, NSH * nc), BF)],
        scratch_shapes=[pltpu.VMEM((tm, D), BF)],
        compiler_params=_cp("arbitrary", "arbitrary", "arbitrary"),
    )(x, g, w, dep)


def _tile_specs(tt, nt_total, reverse):
    def tile(i):
        return (nt_total - 1 - i) if reverse else i

    def cur(c):
        return pl.BlockSpec((tt, D), lambda i, *_: (tile(i), c))

    def halo(c):
        return pl.BlockSpec((HALO, D), lambda i, *_: (jnp.maximum(tile(i) * (tt // HALO) - 1, 0), c))

    def row(r=1):
        return pl.BlockSpec((r, D), lambda i, *_: (0, 0))

    return tile, cur, halo, row


RC = 16


def _chunks(tt, fn, group=2):
    def step(c, carry):
        for u in range(group):
            fn(pl.multiple_of((c * group + u) * RC, RC))
        return carry
    lax.fori_loop(0, tt // (RC * group), step, 0)


def _chunk_pairs(tt, fn):
    def step(c, carry):
        fn(pl.multiple_of(c * 2 * RC, RC), pl.multiple_of(c * 2 * RC + RC, RC))
        return carry
    lax.fori_loop(0, tt // (2 * RC), step, 0)


ALL_SHIFTS = tuple(range(SUBLANES))


def _shifts_of(offs):
    return tuple(sorted({o % SUBLANES for o in offs}))


def _shifted_copies(ext, sh, nrows, shifts=ALL_SHIFTS):
    for i, s in enumerate(shifts):
        sh[i] = ext[pl.ds(s, nrows), :]


def _window(sh, o, r0, shifts=ALL_SHIFTS):
    return sh[shifts.index(o % SUBLANES), pl.ds(r0 + (o // SUBLANES) * SUBLANES, RC), :]


def _fill_taps(wb, w_ref, ntap):
    for k in range(ntap):
        wb[k * SUBLANES:(k + 1) * SUBLANES, :] = jnp.broadcast_to(w_ref[k:k + 1, :], (SUBLANES, D))


def _conv_chunks(sh, wb, offs, r0s, shifts=ALL_SHIFTS):
    accs = []
    for r0 in r0s:
        acc = None
        for k, o in enumerate(offs):
            wk = wb[k * SUBLANES:(k + 1) * SUBLANES, :]
            term = jnp.concatenate([wk] * (RC // SUBLANES), axis=0) * _window(sh, o, r0, shifts)
            acc = term if acc is None else acc + term
        accs.append(acc)
    return accs


WG_TAPS = 5


def _conv_wgrad_chunked(dw_ref, d_ref, sh, offs, tt, shifts=ALL_SHIFTS):
    for g0 in range(0, len(offs), WG_TAPS):
        grp = offs[g0:g0 + WG_TAPS]

        def step(c, accs, grp=grp):
            for u in range(2):
                r0 = pl.multiple_of((2 * c + u) * SUBLANES, SUBLANES)
                d = d_ref[pl.ds(r0, SUBLANES), :]
                accs = tuple(
                    a + d * sh[shifts.index(o % SUBLANES), pl.ds(r0 + (o // SUBLANES) * SUBLANES, SUBLANES), :]
                    for a, o in zip(accs, grp))
            return accs
        accs = lax.fori_loop(0, tt // (2 * SUBLANES), step,
                             tuple(jnp.zeros((SUBLANES, D), F32) for _ in grp))
        for j, a in enumerate(accs):
            dw_ref[g0 + j:g0 + j + 1, :] += jnp.sum(a, axis=0, keepdims=True)


def _causal_offsets(ntap):
    return [HALO - (ntap - 1) + k for k in range(ntap)]


def _anticausal_offsets(ntap):
    return [ntap - 1 - k for k in range(ntap)]


def _mix_a_fwd(z, wa, S):
    T = z.shape[0]
    tt = min(S, MIX_TILE)
    nt = S // tt
    _, cur, halo, row = _tile_specs(tt, T // tt, False)

    nrows = HALO + tt
    offs = _causal_offsets(KA)
    shifts = _shifts_of(offs)

    def body(ah, ab, ac, ah_h, ac_h, w_ref, y_ref, ext, sh, wb):
        @pl.when(pl.program_id(0) == 0)
        def _():
            _fill_taps(wb, w_ref, KA)
            ext[nrows:, :] = jnp.zeros((SUBLANES, D), F32)

        first = (pl.program_id(0) % nt) == 0
        ph = ah_h[...].astype(F32) * ac_h[...].astype(F32)
        ext[0:HALO, :] = jnp.where(first, 0.0, ph)

        def prod(r0):
            rows = pl.ds(r0, RC)
            ext[pl.ds(HALO + r0, RC), :] = ah[rows, :].astype(F32) * ac[rows, :].astype(F32)
        _chunks(tt, prod)
        _shifted_copies(ext, sh, nrows, shifts)

        def conv(*r0s):
            for r0, q in zip(r0s, _conv_chunks(sh, wb, offs, r0s, shifts)):
                rows = pl.ds(r0, RC)
                y_ref[rows, :] = (ab[rows, :].astype(F32) * q).astype(BF)
        _chunk_pairs(tt, conv)

    return pl.pallas_call(
        body, name="mix_a_fwd", grid=(T // tt,),
        in_specs=[cur(0), cur(1), cur(2), halo(0), halo(2), row(KA)],
        out_specs=pl.BlockSpec((tt, D), lambda i: (i, 0)),
        out_shape=jax.ShapeDtypeStruct((T, D), BF),
        scratch_shapes=[pltpu.VMEM((nrows + SUBLANES, D), F32), pltpu.VMEM((len(shifts), nrows, D), F32),
                        pltpu.VMEM((KA * SUBLANES, D), F32)],
        compiler_params=_cp("arbitrary"),
    )(z, z, z, z, z, wa)


def _mix_b_fwd(z, wc, bc, lg, lb, S):
    T = z.shape[0]
    tt = min(S, MIX_TILE)
    nt = S // tt
    _, cur, halo, row = _tile_specs(tt, T // tt, False)

    nrows = HALO + tt
    offs = _causal_offsets(KC)

    def body(ca, cg, ca_h, cg_h, w_ref, bc_ref, lg_ref, lb_ref, y_ref, s_ref, ext, sh, wb):
        @pl.when(pl.program_id(0) == 0)
        def _():
            _fill_taps(wb, w_ref, KC)
            ext[nrows:, :] = jnp.zeros((SUBLANES, D), F32)

        first = (pl.program_id(0) % nt) == 0
        rh = ca_h[...].astype(F32) * _sig(cg_h[...].astype(F32))
        ext[0:HALO, :] = jnp.where(first, 0.0, rh)

        def glu(r0):
            rows = pl.ds(r0, RC)
            ext[pl.ds(HALO + r0, RC), :] = ca[rows, :].astype(F32) * _sig(cg[rows, :].astype(F32))
        _chunks(tt, glu)
        _shifted_copies(ext, sh, nrows)

        def conv(*r0s):
            for r0, q in zip(r0s, _conv_chunks(sh, wb, offs, r0s)):
                rows = pl.ds(r0, RC)
                s = q + bc_ref[...]
                s_ref[rows, :] = s.astype(BF)
                n, _ = _ln_stats(s)
                t = n * lg_ref[...] + lb_ref[...]
                y_ref[rows, :] = (t * _sig(t)).astype(BF)
        _chunk_pairs(tt, conv)

    return pl.pallas_call(
        body, name="mix_b_fwd", grid=(T // tt,),
        in_specs=[cur(3), cur(4), halo(3), halo(4), row(KC), row(), row(), row()],
        out_specs=[pl.BlockSpec((tt, D), lambda i: (i, 0))] * 2,
        out_shape=[jax.ShapeDtypeStruct((T, D), BF)] * 2,
        scratch_shapes=[pltpu.VMEM((nrows + SUBLANES, D), F32), pltpu.VMEM((SUBLANES, nrows, D), F32),
                        pltpu.VMEM((KC * SUBLANES, D), F32)],
        compiler_params=_cp("arbitrary"),
    )(z, z, z, z, wc, bc, lg, lb)


def _causal_mask(transposed):
    r = lax.broadcasted_iota(jnp.int32, (CHUNK, CHUNK), 0)
    c = lax.broadcasted_iota(jnp.int32, (CHUNK, CHUNK), 1)
    return (c >= r) if transposed else (r >= c)


def _mix_s_fwd(z, lg, lb, ws, bst, S):
    T = z.shape[0]
    tt = min(S, MIX_TILE)
    _, cur, _, row = _tile_specs(tt, T // tt, False)

    def body(su, sv, lg_ref, lb_ref, ws_ref, bst_ref, y_ref, u_scr, vn_scr):
        u_scr[...] = _gelu(su[...].astype(F32))[0]
        n, _ = _ln_stats(_gelu(sv[...].astype(F32))[0])
        vn_scr[...] = (n * lg_ref[...] + lb_ref[...]).astype(BF)
        mask = _causal_mask(False)
        for h in range(HEADS):
            wm = jnp.where(mask, ws_ref[h], 0.0).astype(BF)
            cols = slice(h * CHUNK, (h + 1) * CHUNK)
            for c in range(tt // CHUNK):
                rows = slice(c * CHUNK, (c + 1) * CHUNK)
                mixed = _dot(wm, vn_scr[rows, cols]) + bst_ref[:, h:h + 1]
                y_ref[rows, cols] = (u_scr[rows, cols] * mixed).astype(BF)

    return pl.pallas_call(
        body, name="mix_s_fwd", grid=(T // tt,),
        in_specs=[cur(5), cur(6), row(), row(),
                  pl.BlockSpec((HEADS, CHUNK, CHUNK), lambda i: (0, 0, 0)),
                  pl.BlockSpec((CHUNK, HEADS), lambda i: (0, 0))],
        out_specs=pl.BlockSpec((tt, D), lambda i: (i, 0)),
        out_shape=jax.ShapeDtypeStruct((T, D), BF),
        scratch_shapes=[pltpu.VMEM((tt, D), F32), pltpu.VMEM((tt, D), BF)],
        compiler_params=_cp("arbitrary"),
    )(z, z, lg, lb, ws, bst)


def _mix_out_fwd(ya, yc, ys, z, x, wb, wo, gp, dep):
    T = x.shape[0]
    tm = min(T, 256)
    rk = D // NSH

    def body(ya_ref, yc_ref, ys_ref, ga, gc, gs, x_ref, wb_ref, wo_ref, gp_ref, dep_ref,
             p_ref, mg_ref, m_ref, x1_ref):
        acc = None
        for b, (y_ref, g_ref) in enumerate(((ya_ref, ga), (yc_ref, gc), (ys_ref, gs))):
            pb = None
            for k in range(NSH):
                part = _dot(y_ref[:, k * rk:(k + 1) * rk], wb_ref[k, b])
                pb = part if pb is None else pb + part
            p_ref[b] = pb.astype(BF)
            term = _sig(g_ref[...].astype(F32)) * pb
            acc = term if acc is None else acc + term
        mg = acc.astype(BF)
        mg_ref[...] = mg
        m = _dot(mg, wo_ref[...])
        m_ref[...] = m.astype(BF)
        x1_ref[...] = x_ref[...] + _rms_fwd(m, gp_ref[...])

    rowblk = pl.BlockSpec((tm, D), lambda i: (i, 0))
    return pl.pallas_call(
        body, name="mix_out_fwd", grid=(T // tm,),
        in_specs=[rowblk, rowblk, rowblk,
                  pl.BlockSpec((tm, D), lambda i: (i, 7)), pl.BlockSpec((tm, D), lambda i: (i, 8)),
                  pl.BlockSpec((tm, D), lambda i: (i, 9)), rowblk,
                  pl.BlockSpec((NSH, 3, rk, D), lambda i: (0, 0, 0, 0)),
                  pl.BlockSpec((D, D), lambda i: (0, 0)),
                  pl.BlockSpec((1, D), lambda i: (0, 0)), ANY],
        out_specs=[pl.BlockSpec((3, tm, D), lambda i: (0, i, 0)), rowblk, rowblk, rowblk],
        out_shape=[jax.ShapeDtypeStruct((3, T, D), BF), jax.ShapeDtypeStruct((T, D), BF),
                   jax.ShapeDtypeStruct((T, D), BF), jax.ShapeDtypeStruct((T, D), F32)],
        compiler_params=_cp("arbitrary"),
    )(ya, yc, ys, z, z, z, x, wb, wo, gp, dep)


def _ffn_fwd(x1, g3, w1, w2, g4):
    T = x1.shape[0]
    tm = min(T, 512)

    def body(x_ref, g3_ref, w1_ref, w2_ref, g4_ref, h_ref, a_ref, f_ref, x2_ref, h_scr, acc):
        k = pl.program_id(1)

        @pl.when(k == 0)
        def _():
            h = _rms_fwd(x_ref[...], g3_ref[...]).astype(BF)
            h_scr[...] = h
            h_ref[...] = h
            acc[...] = jnp.zeros_like(acc)

        a = _dot(h_scr[...], w1_ref[...])
        a_ref[...] = a.astype(BF)
        r = jnp.maximum(a, 0.0)
        acc[...] += _dot((r * r).astype(BF), w2_ref[...])

        @pl.when(k == NSH - 1)
        def _():
            f = acc[...]
            f_ref[...] = f.astype(BF)
            x2_ref[...] = x_ref[...] + _rms_fwd(f, g4_ref[...])

    rowblk = pl.BlockSpec((tm, D), lambda i, k: (i, 0))
    vec = pl.BlockSpec((1, D), lambda i, k: (0, 0))
    return pl.pallas_call(
        body, name="ffn_fwd", grid=(T // tm, NSH),
        in_specs=[rowblk, vec, pl.BlockSpec((None, D, D), lambda i, k: (k, 0, 0)),
                  pl.BlockSpec((D, D), lambda i, k: (k, 0)), vec],
        out_specs=[rowblk, pl.BlockSpec((tm, D), lambda i, k: (i, k)), rowblk, rowblk],
        out_shape=[jax.ShapeDtypeStruct((T, D), BF), jax.ShapeDtypeStruct((T, NSH * D), BF),
                   jax.ShapeDtypeStruct((T, D), BF), jax.ShapeDtypeStruct((T, D), F32)],
        scratch_shapes=[pltpu.VMEM((tm, D), BF), pltpu.VMEM((tm, D), F32)],
        compiler_params=_cp("arbitrary", "arbitrary"),
    )(x1, g3, w1, w2, g4)


def _loss_head(y, target):
    T = y.shape[0]
    tm = min(T, 512)

    def body(y_ref, t_ref, dy_ref, l_ref):
        @pl.when(pl.program_id(0) == 0)
        def _():
            l_ref[...] = jnp.zeros_like(l_ref)
        e = y_ref[...] - t_ref[...]
        dy_ref[...] = e * (1.0 / D)
        l_ref[...] += jnp.sum(e * e) * (0.5 / D)

    rowblk = pl.BlockSpec((tm, D), lambda i: (i, 0))
    return pl.pallas_call(
        body, name="loss_head", grid=(T // tm,),
        in_specs=[rowblk, rowblk],
        out_specs=[rowblk, pl.BlockSpec((1, 128), lambda i: (0, 0))],
        out_shape=[jax.ShapeDtypeStruct((T, D), F32), jax.ShapeDtypeStruct((1, 128), F32)],
        compiler_params=_cp("arbitrary"),
    )(y, target)


def _ffn_bwd(dx2, f, g4, a, w2, w1, x1, g3, dep):
    T = dx2.shape[0]
    tm = min(T, 512)

    def body(dx2_ref, f_ref, g4_ref, a_ref, w2_ref, w1_ref, x1_ref, g3_ref, dep_ref,
             df_ref, da_ref, dx1_ref, dg4_ref, dg3_ref, df_scr, acc):
        i, k = pl.program_id(0), pl.program_id(1)

        @pl.when((i == 0) & (k == 0))
        def _():
            dg4_ref[...] = jnp.zeros_like(dg4_ref)
            dg3_ref[...] = jnp.zeros_like(dg3_ref)

        @pl.when(k == 0)
        def _():
            df, dg = _rms_bwd(dx2_ref[...], f_ref[...].astype(F32), g4_ref[...])
            dg4_ref[...] += dg
            dfb = df.astype(BF)
            df_scr[...] = dfb
            df_ref[...] = dfb
            acc[...] = jnp.zeros_like(acc)

        av = a_ref[...].astype(F32)
        da = (_dot_nt(df_scr[...], w2_ref[...]) * (2.0 * jnp.maximum(av, 0.0))).astype(BF)
        da_ref[...] = da
        acc[...] += _dot_nt(da, w1_ref[...])

        @pl.when(k == NSH - 1)
        def _():
            dx, dg = _rms_bwd(acc[...], x1_ref[...], g3_ref[...])
            dg3_ref[...] += dg
            dx1_ref[...] = dx2_ref[...] + dx

    rowblk = pl.BlockSpec((tm, D), lambda i, k: (i, 0))
    vec = pl.BlockSpec((1, D), lambda i, k: (0, 0))
    return pl.pallas_call(
        body, name="ffn_bwd", grid=(T // tm, NSH),
        in_specs=[rowblk, rowblk, vec, pl.BlockSpec((tm, D), lambda i, k: (i, k)),
                  pl.BlockSpec((D, D), lambda i, k: (k, 0)),
                  pl.BlockSpec((None, D, D), lambda i, k: (k, 0, 0)), rowblk, vec, ANY],
        out_specs=[rowblk, pl.BlockSpec((tm, D), lambda i, k: (i, k)), rowblk, vec, vec],
        out_shape=[jax.ShapeDtypeStruct((T, D), BF), jax.ShapeDtypeStruct((T, NSH * D), BF),
                   jax.ShapeDtypeStruct((T, D), F32), jax.ShapeDtypeStruct((1, D), F32),
                   jax.ShapeDtypeStruct((1, D), F32)],
        scratch_shapes=[pltpu.VMEM((tm, D), BF), pltpu.VMEM((tm, D), F32)],
        compiler_params=_cp("arbitrary", "arbitrary"),
    )(dx2, f, g4, a, w2, w1, x1, g3, dep)


def _wgrad(name, ops, grid, in_specs, out_spec, out_shape, acc_shape, pick=None, relu2=False):
    nt = grid[-1]
    na = len(ops) - 1

    def body(*refs):
        a_refs, b_ref, o_ref, acc = refs[:na], refs[na], refs[na + 1], refs[na + 2]
        t = pl.program_id(len(grid) - 1)

        @pl.when(t == 0)
        def _():
            acc[...] = jnp.zeros_like(acc)

        def add(a_ref):
            av = a_ref[...]
            if relu2:
                r = jnp.maximum(av.astype(F32), 0.0)
                av = (r * r).astype(BF)
            acc[...] += _dot_tn(av, b_ref[...])

        if na == 1:
            add(a_refs[0])
        else:
            sel = pick()
            for n in range(na):
                pl.when(sel == n)(functools.partial(add, a_refs[n]))

        @pl.when(t == nt - 1)
        def _():
            if len(o_ref.shape) == 2:
                o_ref[...] = acc[...].astype(o_ref.dtype)
            else:
                rs = o_ref.shape[1]
                for q in range(o_ref.shape[0]):
                    o_ref[q] = acc[q * rs:(q + 1) * rs, :].astype(o_ref.dtype)

    return pl.pallas_call(
        body, name=name, grid=grid, in_specs=in_specs, out_specs=out_spec, out_shape=out_shape,
        scratch_shapes=[pltpu.VMEM(acc_shape, F32)],
        compiler_params=_cp(*(["arbitrary"] * len(grid))),
    )(*ops)


def _mix_out_bwd(dx1, m, gp, wo, p3, z, wb):
    T = dx1.shape[0]
    tm = min(T, 512)
    rk = D // NSH

    def body(dx1_ref, m_ref, gp_ref, wo_ref, p_ref, g_ref, wb_ref,
             dm_ref, dp_ref, dy_ref, dz_ref, dgp_ref, dmg):
        i, b = pl.program_id(0), pl.program_id(1)

        @pl.when((i == 0) & (b == 0))
        def _():
            dgp_ref[...] = jnp.zeros_like(dgp_ref)

        @pl.when(b == 0)
        def _():
            dm, dg = _rms_bwd(dx1_ref[...], m_ref[...].astype(F32), gp_ref[...])
            dgp_ref[...] += dg
            dmb = dm.astype(BF)
            dm_ref[...] = dmb
            dmg[...] = _dot_nt(dmb, wo_ref[...])

        gate = _sig(g_ref[...].astype(F32))
        d = dmg[...]
        dp = (d * gate).astype(BF)
        dp_ref[...] = dp
        dz_ref[...] = (d * p_ref[...].astype(F32) * gate * (1.0 - gate)).astype(BF)
        for k in range(NSH):
            dy_ref[:, k * rk:(k + 1) * rk] = _dot_nt(dp, wb_ref[k, b]).astype(BF)

    rowblk = pl.BlockSpec((tm, D), lambda i, b: (i, 0))
    br = pl.BlockSpec((None, tm, D), lambda i, b: (b, i, 0))
    vec = pl.BlockSpec((1, D), lambda i, b: (0, 0))
    return pl.pallas_call(
        body, name="mix_out_bwd", grid=(T // tm, 3),
        in_specs=[rowblk, rowblk, vec, pl.BlockSpec((D, D), lambda i, b: (0, 0)), br,
                  pl.BlockSpec((tm, D), lambda i, b: (i, 7 + b)),
                  pl.BlockSpec((NSH, 3, rk, D), lambda i, b: (0, 0, 0, 0))],
        out_specs=[rowblk, br, br, pl.BlockSpec((tm, D), lambda i, b: (i, 7 + b)), vec],
        out_shape=[jax.ShapeDtypeStruct((T, D), BF), jax.ShapeDtypeStruct((3, T, D), BF),
                   jax.ShapeDtypeStruct((3, T, D), BF), jax.ShapeDtypeStruct((T, 10 * D), BF),
                   jax.ShapeDtypeStruct((1, D), F32)],
        scratch_shapes=[pltpu.VMEM((tm, D), F32)],
        compiler_params=_cp("arbitrary", "arbitrary"),
    )(dx1, m, gp, wo, p3, z, wb)


def _mix_a_bwd(dz, dy3, z, wa, S):
    T = z.shape[0]
    tt = min(S, MIX_TILE)
    nt = S // tt
    ntt = T // tt
    tile, cur, halo, row = _tile_specs(tt, ntt, True)

    nrows = HALO + tt
    coffs, aoffs = _causal_offsets(KA), _anticausal_offsets(KA)
    cshifts, ashifts = _shifts_of(coffs), _shifts_of(aoffs)

    def body(dz_in, dy_ref, ah, ab, ac, ah_h, ac_h, w_ref, dz_ref, dw_ref, ext_p, ext_d, sh, wb, stage):
        i, b = pl.program_id(0), pl.program_id(1)
        ti = ntt - 1 - i

        @pl.when((i == 0) & (b == 0))
        def _():
            dw_ref[...] = jnp.zeros_like(dw_ref)
            ext_d[...] = jnp.zeros_like(ext_d)
            ext_p[nrows:, :] = jnp.zeros((SUBLANES, D), F32)
            _fill_taps(wb, w_ref, KA)

        @pl.when(b == 0)
        def _():
            first = (ti % nt) == 0
            last = (ti % nt) == nt - 1
            ext_p[0:HALO, :] = jnp.where(first, 0.0, ah_h[...].astype(F32) * ac_h[...].astype(F32))
            ext_d[tt:nrows, :] = jnp.where(last, 0.0, ext_d[0:HALO, :])

            def prod(r0):
                rows = pl.ds(r0, RC)
                ext_p[pl.ds(HALO + r0, RC), :] = ah[rows, :].astype(F32) * ac[rows, :].astype(F32)
            _chunks(tt, prod)
            _shifted_copies(ext_p, sh, nrows, cshifts)

            def mid(*r0s):
                for r0, q in zip(r0s, _conv_chunks(sh, wb, coffs, r0s, cshifts)):
                    rows = pl.ds(r0, RC)
                    dy = dy_ref[rows, :].astype(F32)
                    stage[1, rows, :] = (dy * q).astype(BF)
                    ext_d[rows, :] = dy * ab[rows, :].astype(F32)
            _chunk_pairs(tt, mid)
            _conv_wgrad_chunked(dw_ref, ext_d, sh, coffs, tt, cshifts)
            _shifted_copies(ext_d, sh, nrows, ashifts)

            def fin(*r0s):
                for r0, dp in zip(r0s, _conv_chunks(sh, wb, aoffs, r0s, ashifts)):
                    rows = pl.ds(r0, RC)
                    stage[0, rows, :] = (dp * ac[rows, :].astype(F32)).astype(BF)
                    stage[2, rows, :] = (dp * ah[rows, :].astype(F32)).astype(BF)
            _chunk_pairs(tt, fin)

        dz_ref[...] = stage[b]

    return pl.pallas_call(
        body, name="mix_a_bwd", grid=(ntt, 3),
        in_specs=[ANY, pl.BlockSpec((None, tt, D), lambda i, b: (0, tile(i), 0)),
                  cur(0), cur(1), cur(2), halo(0), halo(2), row(KA)],
        out_specs=[pl.BlockSpec((tt, D), lambda i, b: (tile(i), b)), pl.BlockSpec((KA, D), lambda i, b: (0, 0))],
        out_shape=[jax.ShapeDtypeStruct(dz.shape, BF), jax.ShapeDtypeStruct((KA, D), F32)],
        scratch_shapes=[pltpu.VMEM((nrows + SUBLANES, D), F32), pltpu.VMEM((nrows + SUBLANES, D), F32),
                        pltpu.VMEM((max(len(cshifts), len(ashifts)), nrows, D), F32),
                        pltpu.VMEM((KA * SUBLANES, D), F32), pltpu.VMEM((3, tt, D), BF)],
        input_output_aliases={0: 0},
        compiler_params=_cp("arbitrary", "arbitrary"),
    )(dz, dy3, z, z, z, z, z, wa)


def _mix_b_bwd(dz, dy3, s, z, wc, lg, lb, S):
    T = z.shape[0]
    tt = min(S, MIX_TILE)
    nt = S // tt
    ntt = T // tt
    tile, cur, halo, row = _tile_specs(tt, ntt, True)

    nrows = HALO + tt

    def body(dz_in, dy_ref, s_ref, ca, cg, ca_h, cg_h, w_ref, lg_ref, lb_ref,
             dz_ref, dw_ref, dbc_ref, dlg_ref, dlb_ref, ext_r, ext_d, sh, wb, accs, stage):
        i, b = pl.program_id(0), pl.program_id(1)
        ti = ntt - 1 - i

        @pl.when((i == 0) & (b == 0))
        def _():
            dw_ref[...] = jnp.zeros_like(dw_ref)
            dbc_ref[...] = jnp.zeros_like(dbc_ref)
            dlg_ref[...] = jnp.zeros_like(dlg_ref)
            dlb_ref[...] = jnp.zeros_like(dlb_ref)
            ext_d[...] = jnp.zeros_like(ext_d)
            ext_r[nrows:, :] = jnp.zeros((SUBLANES, D), F32)
            _fill_taps(wb, w_ref, KC)

        @pl.when(b == 0)
        def _():
            first = (ti % nt) == 0
            last = (ti % nt) == nt - 1
            ext_r[0:HALO, :] = jnp.where(first, 0.0, ca_h[...].astype(F32) * _sig(cg_h[...].astype(F32)))
            ext_d[tt:nrows, :] = jnp.where(last, 0.0, ext_d[0:HALO, :])
            accs[...] = jnp.zeros_like(accs)

            def point(r0):
                rows = pl.ds(r0, RC)
                n, r = _ln_stats(s_ref[rows, :].astype(F32))
                t = n * lg_ref[...] + lb_ref[...]
                sg = _sig(t)
                dt = dy_ref[rows, :].astype(F32) * (sg * (1.0 + t * (1.0 - sg)))
                accs[0] += dt * n
                accs[1] += dt
                ds = _ln_bwd(dt * lg_ref[...], n, r)
                accs[2] += ds
                ext_d[rows, :] = ds
                ext_r[pl.ds(HALO + r0, RC), :] = ca[rows, :].astype(F32) * _sig(cg[rows, :].astype(F32))
            _chunks(tt, point)
            dlg_ref[...] += jnp.sum(accs[0], axis=0, keepdims=True)
            dlb_ref[...] += jnp.sum(accs[1], axis=0, keepdims=True)
            dbc_ref[...] += jnp.sum(accs[2], axis=0, keepdims=True)

            _shifted_copies(ext_r, sh, nrows)
            _conv_wgrad_chunked(dw_ref, ext_d, sh, _causal_offsets(KC), tt)
            _shifted_copies(ext_d, sh, nrows)

            def conv(*r0s):
                for r0, dr in zip(r0s, _conv_chunks(sh, wb, _anticausal_offsets(KC), r0s)):
                    rows = pl.ds(r0, RC)
                    cav = ca[rows, :].astype(F32)
                    sgc = _sig(cg[rows, :].astype(F32))
                    stage[0, rows, :] = (dr * sgc).astype(BF)
                    stage[1, rows, :] = (dr * cav * sgc * (1.0 - sgc)).astype(BF)
            _chunk_pairs(tt, conv)

        dz_ref[...] = stage[b]

    vec = pl.BlockSpec((1, D), lambda i, b: (0, 0))
    return pl.pallas_call(
        body, name="mix_b_bwd", grid=(ntt, 2),
        in_specs=[ANY, pl.BlockSpec((None, tt, D), lambda i, b: (1, tile(i), 0)),
                  pl.BlockSpec((tt, D), lambda i, b: (tile(i), 0)),
                  cur(3), cur(4), halo(3), halo(4), row(KC), row(), row()],
        out_specs=[pl.BlockSpec((tt, D), lambda i, b: (tile(i), 3 + b)),
                   pl.BlockSpec((KC, D), lambda i, b: (0, 0)), vec, vec, vec],
        out_shape=[jax.ShapeDtypeStruct(dz.shape, BF), jax.ShapeDtypeStruct((KC, D), F32)]
        + [jax.ShapeDtypeStruct((1, D), F32)] * 3,
        scratch_shapes=[pltpu.VMEM((nrows + SUBLANES, D), F32), pltpu.VMEM((nrows + SUBLANES, D), F32),
                        pltpu.VMEM((SUBLANES, nrows, D), F32), pltpu.VMEM((KC * SUBLANES, D), F32),
                        pltpu.VMEM((3, RC, D), F32), pltpu.VMEM((2, tt, D), BF)],
        input_output_aliases={0: 0},
        compiler_params=_cp("arbitrary", "arbitrary"),
    )(dz, dy3, s, z, z, z, z, wc, lg, lb)


def _mix_s_bwd(dz, dy3, z, lg, lb, ws, wst, bst, S):
    T = z.shape[0]
    tt = min(S, MIX_TILE)
    ntt = T // tt
    _, cur, _, row = _tile_specs(tt, ntt, False)

    def body(dz_in, dy_ref, su, sv, lg_ref, lb_ref, ws_ref, wst_ref, bst_ref,
             dz_ref, dws_ref, dbst_ref, dlg_ref, dlb_ref, u_scr, vn_scr, dvn_scr, stage):
        i, b = pl.program_id(0), pl.program_id(1)

        @pl.when((i == 0) & (b == 0))
        def _():
            dws_ref[...] = jnp.zeros_like(dws_ref)
            dbst_ref[...] = jnp.zeros_like(dbst_ref)
            dlg_ref[...] = jnp.zeros_like(dlg_ref)
            dlb_ref[...] = jnp.zeros_like(dlb_ref)

        @pl.when(b == 0)
        def _():
            u, du_dx = _gelu(su[...].astype(F32))
            v, dv_dx = _gelu(sv[...].astype(F32))
            u_scr[...] = u
            n, r = _ln_stats(v)
            vn_scr[...] = (n * lg_ref[...] + lb_ref[...]).astype(BF)
            mask = _causal_mask(False)
            mask_t = _causal_mask(True)
            for h in range(HEADS):
                wm = jnp.where(mask, ws_ref[h], 0.0).astype(BF)
                wmt = jnp.where(mask_t, wst_ref[h], 0.0).astype(BF)
                cols = slice(h * CHUNK, (h + 1) * CHUNK)
                for c in range(tt // CHUNK):
                    rows = slice(c * CHUNK, (c + 1) * CHUNK)
                    vb = vn_scr[rows, cols]
                    mixed = _dot(wm, vb) + bst_ref[:, h:h + 1]
                    dy = dy_ref[rows, cols].astype(F32)
                    dmix = dy * u_scr[rows, cols]
                    u_scr[rows, cols] = dy * mixed
                    dbst_ref[:, h:h + 1] += jnp.sum(dmix, axis=1, keepdims=True)
                    dmb = dmix.astype(BF)
                    dws_ref[h] += _dot_nt(dmb, vb)
                    dvn_scr[rows, cols] = _dot(wmt, dmb)
            stage[0] = (u_scr[...] * du_dx).astype(BF)
            dvn = dvn_scr[...]
            dlg_ref[...] += jnp.sum(dvn * n, axis=0, keepdims=True)
            dlb_ref[...] += jnp.sum(dvn, axis=0, keepdims=True)
            stage[1] = (_ln_bwd(dvn * lg_ref[...], n, r) * dv_dx).astype(BF)

        dz_ref[...] = stage[b]

    vec = pl.BlockSpec((1, D), lambda i, b: (0, 0))
    wsp = pl.BlockSpec((HEADS, CHUNK, CHUNK), lambda i, b: (0, 0, 0))
    bsp = pl.BlockSpec((CHUNK, HEADS), lambda i, b: (0, 0))
    return pl.pallas_call(
        body, name="mix_s_bwd", grid=(ntt, 2),
        in_specs=[ANY, pl.BlockSpec((None, tt, D), lambda i, b: (2, i, 0)),
                  cur(5), cur(6), row(), row(), wsp, wsp, bsp],
        out_specs=[pl.BlockSpec((tt, D), lambda i, b: (i, 5 + b)), wsp, bsp, vec, vec],
        out_shape=[jax.ShapeDtypeStruct(dz.shape, BF), jax.ShapeDtypeStruct((HEADS, CHUNK, CHUNK), F32),
                   jax.ShapeDtypeStruct((CHUNK, HEADS), F32), jax.ShapeDtypeStruct((1, D), F32),
                   jax.ShapeDtypeStruct((1, D), F32)],
        scratch_shapes=[pltpu.VMEM((tt, D), F32), pltpu.VMEM((tt, D), BF), pltpu.VMEM((tt, D), F32),
                        pltpu.VMEM((2, tt, D), BF)],
        input_output_aliases={0: 0},
        compiler_params=_cp("arbitrary", "arbitrary"),
    )(dz, dy3, z, z, lg, lb, ws, wst, bst)


def _in_proj_bwd(dz, w, x, g, dx1, dep):
    T = x.shape[0]
    nc = w.shape[2]
    tm = min(T, 1024)
    tn = 1280
    nj = nc // tn
    ep = min(tm, 128)

    def body(dz_ref, w_ref, x_ref, g_ref, dx1_ref, dep_ref, dx_ref, dg_ref, acc):
        i, k, j = pl.program_id(0), pl.program_id(1), pl.program_id(2)

        @pl.when((i == 0) & (k == 0) & (j == 0))
        def _():
            dg_ref[...] = jnp.zeros_like(dg_ref)

        @pl.when((k == 0) & (j == 0))
        def _():
            acc[...] = jnp.zeros_like(acc)

        acc[...] += _dot_nt(dz_ref[...], w_ref[...])

        @pl.when((k == NSH - 1) & (j == nj - 1))
        def _():
            def step(c, dg):
                rows = pl.ds(pl.multiple_of(c * ep, ep), ep)
                dx, dgc = _rms_bwd(acc[rows, :], x_ref[rows, :], g_ref[...])
                dx_ref[rows, :] = dx1_ref[rows, :] + dx
                return dg + dgc
            dg_ref[...] += lax.fori_loop(0, tm // ep, step, jnp.zeros((1, D), F32))

    rowblk = pl.BlockSpec((tm, D), lambda i, k, j: (i, 0))
    vec = pl.BlockSpec((1, D), lambda i, k, j: (0, 0))
    return pl.pallas_call(
        body, name="in_proj_bwd", grid=(T // tm, NSH, nj),
        in_specs=[pl.BlockSpec((tm, tn), lambda i, k, j: (i, k * nj + j)),
                  pl.BlockSpec((None, D, tn), lambda i, k, j: (k, 0, j)), rowblk, vec, rowblk, ANY],
        out_specs=[rowblk, vec],
        out_shape=[jax.ShapeDtypeStruct((T, D), F32), jax.ShapeDtypeStruct((1, D), F32)],
        scratch_shapes=[pltpu.VMEM((tm, D), F32)],
        compiler_params=_cp("arbitrary", "arbitrary", "arbitrary"),
    )(dz, w, x, g, dx1, dep)


def _layer_fwd(x, p, S, dep, late):
    h, z = _in_proj(x, p["g_mix_pre"], p["w_in"], dep)
    ya = _mix_a_fwd(z, p["conv_a_w"], S)
    yc, s = _mix_b_fwd(z, p["conf_dw_w"], p["conf_dw_b"], p["conf_ln_g"], p["conf_ln_b"], S)
    ys = _mix_s_fwd(z, p["sgu_ln_g"], p["sgu_ln_b"], p["sgu_ws"], p["sgu_bt"], S)
    more, dep2 = late(ys)
    p.update(more)
    p3, merged, m, x1 = _mix_out_fwd(ya, yc, ys, z, x, p["w_branch"], p["w_out"], p["g_mix_post"], dep2)
    h2, a, f, x2 = _ffn_fwd(x1, p["g_ffn_pre"], p["w_ff1"], p["w_ff2"], p["g_ffn_post"])
    saved = dict(x=x, h=h, z=z, ya=ya, yc=yc, ys=ys, s=s, p3=p3, merged=merged, m=m, x1=x1, h2=h2, a=a, f=f)
    return x2, saved


def _layer_bwd(dx2, p, sv, S, dep, early):
    T = dx2.shape[0]
    bt = min(T, WGRAD_TILE)
    nt = T // bt
    rk = D // NSH
    df, da, dx1, dg_ffn_post, dg_ffn_pre = _ffn_bwd(dx2, sv["f"], p["g_ffn_post"], sv["a"], p["w_ff2"],
                                                    p["w_ff1"], sv["x1"], p["g_ffn_pre"], dep)
    dw_ff2 = _wgrad("wgrad_ff2", (sv["a"], df), (NSH, nt),
                    [pl.BlockSpec((bt, D), lambda k, t: (t, k)), pl.BlockSpec((bt, D), lambda k, t: (t, 0))],
                    pl.BlockSpec((None, D, D), lambda k, t: (k, 0, 0)),
                    jax.ShapeDtypeStruct((NSH, D, D), BF), (D, D), relu2=True)
    dw_ff1 = _wgrad("wgrad_ff1", (sv["h2"], da), (NSH, nt),
                    [pl.BlockSpec((bt, D), lambda k, t: (t, 0)), pl.BlockSpec((bt, D), lambda k, t: (t, k))],
                    pl.BlockSpec((None, D, D), lambda k, t: (k, 0, 0)),
                    jax.ShapeDtypeStruct((NSH, D, D), BF), (D, D))
    dm, dp3, dy3, dz, dg_mix_post = _mix_out_bwd(dx1, sv["m"], p["g_mix_post"], p["w_out"], sv["p3"], sv["z"],
                                                 p["w_branch"])
    dw_out = _wgrad("wgrad_out", (sv["merged"], dm), (nt,),
                    [pl.BlockSpec((bt, D), lambda t: (t, 0)), pl.BlockSpec((bt, D), lambda t: (t, 0))],
                    pl.BlockSpec((D, D), lambda t: (0, 0)),
                    jax.ShapeDtypeStruct((D, D), BF), (D, D)).reshape(NSH, rk, D)
    ysp = lambda n: pl.BlockSpec((bt, D), lambda b, t: (jnp.where(b == n, t, 0), 0))
    dw_br = _wgrad("wgrad_branch", (sv["ya"], sv["yc"], sv["ys"], dp3), (3, nt),
                   [ysp(0), ysp(1), ysp(2), pl.BlockSpec((None, bt, D), lambda b, t: (b, t, 0))],
                   pl.BlockSpec((NSH, None, rk, D), lambda b, t: (0, b, 0, 0)),
                   jax.ShapeDtypeStruct((NSH, 3, rk, D), BF), (D, D), pick=lambda: pl.program_id(0))
    dep2 = early([dw_br, dw_out, dw_ff1, dw_ff2])
    dz, dwa = _mix_a_bwd(dz, dy3, sv["z"], p["conv_a_w"], S)
    dz, dwc, dbc, dclg, dclb = _mix_b_bwd(dz, dy3, sv["s"], sv["z"], p["conf_dw_w"], p["conf_ln_g"],
                                          p["conf_ln_b"], S)
    dz, dws, dbst, dslg, dslb = _mix_s_bwd(dz, dy3, sv["z"], p["sgu_ln_g"], p["sgu_ln_b"], p["sgu_ws"],
                                           p["sgu_wst"], p["sgu_bt"], S)
    dx, dg_mix_pre = _in_proj_bwd(dz, p["w_in"], sv["x"], p["g_mix_pre"], dx1, dep2)
    tn = 1280
    nj = p["w_in"].shape[2] // tn
    dw_in = _wgrad("wgrad_in", (sv["h"], dz), (NSH, nj, nt),
                   [pl.BlockSpec((bt, D), lambda k, j, t: (t, 0)),
                    pl.BlockSpec((bt, tn), lambda k, j, t: (t, k * nj + j))],
                   pl.BlockSpec((None, D, tn), lambda k, j, t: (k, 0, j)),
                   jax.ShapeDtypeStruct(p["w_in"].shape, BF), (D, tn))
    tril = jnp.tril(jnp.ones((CHUNK, CHUNK), bool))
    small = dict(norm_mix_pre=dg_mix_pre, norm_mix_post=dg_mix_post, norm_ffn_pre=dg_ffn_pre,
                 norm_ffn_post=dg_ffn_post, conv_a_w=dwa, conf_dw_w=dwc, conf_dw_b=dbc, conf_ln_g=dclg,
                 conf_ln_b=dclb, sgu_ln_g=dslg, sgu_ln_b=dslb,
                 sgu_ws=jnp.where(tril[None], dws, 0.0), sgu_b=dbst.T)
    big = dict(w_in=dw_in, w_branch=dw_br, w_out=dw_out, w_ff1=dw_ff1, w_ff2=dw_ff2)
    return dx, big, small


SMALL_NAMES = ("norm_mix_pre", "norm_mix_post", "norm_ffn_pre", "norm_ffn_post", "conv_a_w", "conf_dw_w",
               "conf_dw_b", "conf_ln_g", "conf_ln_b", "sgu_ln_g", "sgu_ln_b", "sgu_b", "sgu_ws")
SMALL_ROWS = dict(norm_mix_pre=1, norm_mix_post=1, norm_ffn_pre=1, norm_ffn_post=1, conv_a_w=KA, conf_dw_w=KC,
                  conf_dw_b=1, conf_ln_g=1, conf_ln_b=1, sgu_ln_g=1, sgu_ln_b=1, sgu_b=1, sgu_ws=CHUNK)
def _pad8(r):
    return -(-r // SUBLANES) * SUBLANES


PACK_ROWS = sum(_pad8(r) for r in SMALL_ROWS.values())


def _pack_small(d):
    parts = []
    for n in SMALL_NAMES:
        r = SMALL_ROWS[n]
        parts.append(jnp.pad(d[n].reshape(r, D).astype(F32), ((0, _pad8(r) - r), (0, 0))))
    return jnp.concatenate(parts, axis=0)


def _unpack_small(a, shapes):
    out, r = {}, 0
    for n in SMALL_NAMES:
        out[n] = a[:, r:r + SMALL_ROWS[n]].reshape((a.shape[0],) + tuple(shapes[n]))
        r += _pad8(SMALL_ROWS[n])
    return out


def _me():
    return lax.axis_index("x"), lax.axis_index("y"), lax.axis_index("c")


def _slab(ref, q, a, h=None):
    r = ref.shape[1]
    rows = slice(None) if h is None else pl.ds(h * (r // 2), r // 2)
    return ref.at[pl.ds(q * a, a), rows, :]


def _rows(ref, h):
    r = ref.shape[-2]
    lead = (slice(None),) * (len(ref.shape) - 2)
    return ref.at[lead + (pl.ds(h * (r // 2), r // 2), slice(None))]


def _rcopy(src, dst, sems, idx, dev):
    return pltpu.make_async_remote_copy(src_ref=src, dst_ref=dst, send_sem=sems[0].at[idx], recv_sem=sems[1].at[idx],
                                        device_id=dev, device_id_type=MESH)


def _send_halves_to_sibling(parts):
    n = len(parts)

    def body(*refs):
        src, dst = refs[:n], refs[n:2 * n]
        sems = refs[2 * n:2 * n + 2]
        x, y, c = _me()
        cps = [_rcopy(_rows(src[i], 1 - c), dst[i], sems, i, (x, y, 1 - c)) for i in range(n)]
        for cp in cps:
            cp.start()
        for cp in cps:
            cp.wait()

    outs = [jax.ShapeDtypeStruct((p.shape[0], p.shape[1] // 2, p.shape[2]), p.dtype) for p in parts]
    return pl.pallas_call(
        body, name="pair_exchange", in_specs=[ANY] * n, out_specs=[ANY] * n, out_shape=outs,
        scratch_shapes=[pltpu.SemaphoreType.DMA((n,)), pltpu.SemaphoreType.DMA((n,))],
    )(*parts)


def _pair_add(part, sib, c):
    A, R, C = part.shape
    hr = R // 2
    br = min(hr, 512)
    nb = hr // br

    def body(c_ref, p_ref, s_ref, o_ref):
        o_ref[...] = (p_ref[...].astype(F32) + s_ref[...].astype(F32)).astype(BF)

    return pl.pallas_call(
        body, name="pair_add",
        grid_spec=pltpu.PrefetchScalarGridSpec(
            num_scalar_prefetch=1, grid=(A, nb),
            in_specs=[pl.BlockSpec((None, br, C), lambda a, i, c_ref: (a, c_ref[0] * nb + i, 0)),
                      pl.BlockSpec((None, br, C), lambda a, i, c_ref: (a, i, 0))],
            out_specs=pl.BlockSpec((None, br, C), lambda a, i, c_ref: (a, i, 0))),
        out_shape=jax.ShapeDtypeStruct((A, hr, C), BF),
        compiler_params=_cp("arbitrary", "arbitrary"),
    )(c, part, sib)


def _other_chips(x, y):
    return [(1 - x, y), (x, 1 - y), (1 - x, 1 - y)]


def _split_call(name, copies, srcs, lands, sems=None, after=None):
    n, m = len(srcs), len(lands)
    hbm = lambda t: pltpu.HBM(t.shape, t.dtype)
    pin = lambda t: pltpu.with_memory_space_constraint(t, pltpu.HBM)
    thru = [hbm(t) for t in srcs] + [hbm(t) for t in lands]
    sem_spec = pl.BlockSpec(memory_space=pltpu.SEMAPHORE)
    effect = pltpu.CompilerParams(has_side_effects=pltpu.SideEffectType.DATAFLOW_SIDE_EFFECTING)
    if sems is None:
        def start_body(*refs):
            src, land = refs[:n], refs[n:n + m]
            ssem, rsem = refs[n + m], refs[n + m + 1]
            token = refs[-1]
            cps = copies(src, land, (ssem, rsem))
            for cp in cps:
                cp.start()
            token[...] = jnp.zeros_like(token)

        ncp = copies.count
        out = pl.pallas_call(
            start_body, name=name,
            out_shape=(pltpu.SemaphoreType.DMA((ncp,)), pltpu.SemaphoreType.DMA((ncp,)), *thru,
                       jax.ShapeDtypeStruct((8, 128), F32)),
            in_specs=[ANY] * (n + m),
            out_specs=(sem_spec, sem_spec, *([ANY] * (n + m)), pl.BlockSpec(memory_space=pltpu.VMEM)),
            input_output_aliases={i: 2 + i for i in range(n + m)},
            compiler_params=effect,
        )(*[pin(t) for t in srcs], *[pin(t) for t in lands])
        return out[0], out[1], list(out[2:2 + n]), list(out[2 + n:2 + n + m]), out[-1]

    def wait_body(*refs):
        src, land = refs[:n], refs[n:n + m]
        ssem, rsem = refs[n + m], refs[n + m + 1]
        for cp in copies(src, land, (ssem, rsem)):
            cp.wait_send()
            cp.wait_recv()

    out = pl.pallas_call(
        wait_body, name=name, out_shape=tuple(thru),
        in_specs=[ANY] * (n + m) + [sem_spec, sem_spec, ANY],
        out_specs=tuple([ANY] * (n + m)),
        input_output_aliases={i: i for i in range(n + m)},
        compiler_params=effect,
    )(*srcs, *lands, sems[0], sems[1], after)
    return list(out[:n]), list(out[n:])


def _cast_into(w, land, layer, kidx, dep):
    _, a, R, C = w.shape
    br = R
    while br * C > 512 * 1024 and br % 32 == 0:
        br //= 2

    def body(k_ref, w_ref, land_ref, dep_ref, o_ref):
        o_ref[...] = w_ref[...].astype(o_ref.dtype)

    return pl.pallas_call(
        body, name="cast_into",
        grid_spec=pltpu.PrefetchScalarGridSpec(
            num_scalar_prefetch=1, grid=(a, R // br),
            in_specs=[pl.BlockSpec((None, None, br, C), lambda e, i, k: (layer, e, i, 0)), ANY, ANY],
            out_specs=pl.BlockSpec((None, br, C), lambda e, i, k: (k[0] * a + e, i, 0))),
        out_shape=jax.ShapeDtypeStruct(land.shape, land.dtype), input_output_aliases={2: 0},
        compiler_params=_cp("arbitrary", "arbitrary"),
    )(kidx, w, land, dep)


class _GatherCopies:
    def __init__(self, n, halves=True):
        self.n, self.count, self.halves = n, 3 * n, halves

    def __call__(self, src, land, sems):
        x, y, c = _me()
        k = 2 * x + y
        cps = []
        for j, (qx, qy) in enumerate(_other_chips(x, y)):
            for i in range(self.n):
                mine = _slab(land[i], k, land[i].shape[0] // NSH, c if self.halves else None)
                cps.append(_rcopy(mine, mine, sems, j * self.n + i, (qx, qy, c)))
        return cps


def _gather_finish(lands):
    n = len(lands)

    def body(*refs):
        dst = refs[n:2 * n]
        sems = refs[2 * n:2 * n + 2]
        x, y, c = _me()
        av = [d.shape[0] // NSH for d in dst]
        cps = []
        for j, (qx, qy) in enumerate(_other_chips(x, y)):
            for i in range(n):
                got = _slab(dst[i], 2 * qx + qy, av[i], c)
                cps.append(_rcopy(got, got, sems, j * n + i, (x, y, 1 - c)))
        for cp in cps:
            cp.start()
        for j, (qx, qy) in enumerate(_other_chips(x, y)):
            for i in range(n):
                other = _slab(dst[i], 2 * qx + qy, av[i], 1 - c)
                _rcopy(other, other, sems, j * n + i, (x, y, c)).wait_recv()
        for cp in cps:
            cp.wait_send()

    return pl.pallas_call(
        body, name="gather_finish", in_specs=[ANY] * n, out_specs=[ANY] * n,
        out_shape=[jax.ShapeDtypeStruct(t.shape, t.dtype) for t in lands],
        input_output_aliases={i: i for i in range(n)},
        scratch_shapes=[pltpu.SemaphoreType.DMA((3 * n,)), pltpu.SemaphoreType.DMA((3 * n,))],
    )(*lands)


class _ScatterCopies:
    def __init__(self, n):
        self.n, self.count = n, 3 * n

    def __call__(self, src, land, sems):
        x, y, c = _me()
        k = 2 * x + y
        cps = []
        for j, (qx, qy) in enumerate(_other_chips(x, y)):
            for i in range(self.n):
                a = src[i].shape[0] // NSH
                cps.append(_rcopy(_slab(src[i], 2 * qx + qy, a), _slab(land[i], k, a), sems, j * self.n + i,
                                  (qx, qy, c)))
        return cps


def _sum_chips(own, rcv, acc, layer, nlayers, idx):
    A, hr, C = rcv.shape
    a = A // NSH
    br = min(hr, 512)
    nb = hr // br

    def body(*refs):
        r0, r1, r2, r3 = refs[1:5]
        o_ref = refs[-1]
        o_ref[...] = ((r0[...].astype(F32) + r1[...].astype(F32)) + r2[...].astype(F32)) + r3[...].astype(F32)

    slot = lambda s: pl.BlockSpec((None, br, C), lambda e, i, ix: (ix[s] * a + e, i, 0))
    ops = [own, rcv, rcv, rcv]
    in_specs = [slot(0), slot(1), slot(2), slot(3)]
    aliases = {}
    if acc is not None:
        ops.append(acc)
        in_specs.append(ANY)
        aliases = {5: 0}
    return pl.pallas_call(
        body, name="sum_chips",
        grid_spec=pltpu.PrefetchScalarGridSpec(
            num_scalar_prefetch=1, grid=(a, nb), in_specs=in_specs,
            out_specs=pl.BlockSpec((None, None, br, C), lambda e, i, ix: (layer, e, ix[4] * nb + i, 0))),
        out_shape=jax.ShapeDtypeStruct((nlayers, a, 2 * hr, C), F32), input_output_aliases=aliases,
        compiler_params=_cp("arbitrary", "arbitrary"),
    )(idx, *ops)


def _join_halves(fulls):
    n = len(fulls)

    def body(*refs):
        buf = refs[n:2 * n]
        sems = refs[2 * n:2 * n + 2]
        x, y, c = _me()
        cps = [_rcopy(_rows(buf[i], c), _rows(buf[i], c), sems, i, (x, y, 1 - c)) for i in range(n)]
        for cp in cps:
            cp.start()
        for i in range(n):
            _rcopy(_rows(buf[i], 1 - c), _rows(buf[i], 1 - c), sems, i, (x, y, c)).wait_recv()
        for cp in cps:
            cp.wait_send()

    return pl.pallas_call(
        body, name="join_halves", in_specs=[ANY] * n, out_specs=[ANY] * n,
        out_shape=[jax.ShapeDtypeStruct(t.shape, t.dtype) for t in fulls],
        input_output_aliases={i: i for i in range(n)},
        scratch_shapes=[pltpu.SemaphoreType.DMA((n,)), pltpu.SemaphoreType.DMA((n,))],
    )(*fulls)


def _small_blocks(hr):
    br = hr
    while br > 512 and br % 16 == 0:
        br //= 2
    return br, hr // br


def _pair_sum_slot(part, sib, ck):
    R, C = part.shape
    hr = R // 2
    br, nb = _small_blocks(hr)

    def body(ix, p_ref, s_ref, o_ref):
        o_ref[...] = p_ref[...] + s_ref[...]

    return pl.pallas_call(
        body, name="pair_sum_slot",
        grid_spec=pltpu.PrefetchScalarGridSpec(
            num_scalar_prefetch=1, grid=(nb,),
            in_specs=[pl.BlockSpec((br, C), lambda i, ix: (ix[0] * nb + i, 0)),
                      pl.BlockSpec((br, C), lambda i, ix: (i, 0))],
            out_specs=pl.BlockSpec((None, br, C), lambda i, ix: (ix[1], i, 0))),
        out_shape=jax.ShapeDtypeStruct((NSH, hr, C), F32),
        compiler_params=_cp("arbitrary"),
    )(ck, part, sib)


def _sum_slots(slots, ck):
    _, hr, C = slots.shape
    br, nb = _small_blocks(hr)

    def body(ix, s_ref, o_ref):
        o_ref[...] = ((s_ref[0] + s_ref[1]) + s_ref[2]) + s_ref[3]

    return pl.pallas_call(
        body, name="sum_slots",
        grid_spec=pltpu.PrefetchScalarGridSpec(
            num_scalar_prefetch=1, grid=(nb,),
            in_specs=[pl.BlockSpec((NSH, br, C), lambda i, ix: (0, i, 0))],
            out_specs=pl.BlockSpec((br, C), lambda i, ix: (ix[0] * nb + i, 0))),
        out_shape=jax.ShapeDtypeStruct((2 * hr, C), F32),
        compiler_params=_cp("arbitrary"),
    )(ck, slots)


def _adamw(w, g, m, v):
    shape = w.shape
    C = shape[-1]
    R = shape[-2]
    A = 1
    for s in shape[:-2]:
        A *= s
    br = R
    while br * C > 256 * 1024 and br % 16 == 0:
        br //= 2
    c1 = 1.0 / (1.0 - ADAM_B1 ** ADAM_STEP)
    c2 = 1.0 / (1.0 - ADAM_B2 ** ADAM_STEP)

    def body(w_ref, g_ref, m_ref, v_ref, og_ref, d_ref, nm_ref, nv_ref):
        gv = g_ref[...]
        og_ref[...] = gv
        nm = ADAM_B1 * m_ref[...] + (1.0 - ADAM_B1) * gv
        nv = ADAM_B2 * v_ref[...] + (1.0 - ADAM_B2) * (gv * gv)
        nm_ref[...] = nm
        nv_ref[...] = nv
        d_ref[...] = -ADAM_LR * ((nm * c1) / (jnp.sqrt(nv * c2) + ADAM_EPS) + ADAM_WD * w_ref[...])

    blk = pl.BlockSpec((None, br, C), lambda a, i: (a, i, 0))
    outs = pl.pallas_call(
        body, name="adamw", grid=(A, R // br), in_specs=[blk] * 4, out_specs=[blk] * 4,
        out_shape=[jax.ShapeDtypeStruct((A, R, C), F32)] * 4,
        compiler_params=_cp("arbitrary", "arbitrary"),
    )(*(t.reshape(A, R, C) for t in (w, g, m, v)))
    return tuple(o.reshape(shape) for o in outs)


WEIGHTS = ("norm_mix_pre", "norm_mix_post", "norm_ffn_pre", "norm_ffn_post", "w_in", "conv_a_w", "conf_dw_w",
           "conf_dw_b", "conf_ln_g", "conf_ln_b", "sgu_ln_g", "sgu_ln_b", "sgu_ws", "sgu_b", "w_branch", "w_out",
           "w_ff1", "w_ff2")
BIG = ("w_in", "w_branch", "w_out", "w_ff1", "w_ff2")
CONV_ROWS = 48


def kernel(x, norm_mix_pre, norm_mix_post, norm_ffn_pre, norm_ffn_post, w_in, conv_a_w, conf_dw_w, conf_dw_b, conf_ln_g, conf_ln_b, sgu_ln_g, sgu_ln_b, sgu_ws, sgu_b, w_branch, w_out, w_ff1, w_ff2, loss_target, m_norm_mix_pre, m_norm_mix_post, m_norm_ffn_pre, m_norm_ffn_post, m_w_in, m_conv_a_w, m_conf_dw_w, m_conf_dw_b, m_conf_ln_g, m_conf_ln_b, m_sgu_ln_g, m_sgu_ln_b, m_sgu_ws, m_sgu_b, m_w_branch, m_w_out, m_w_ff1, m_w_ff2, v_norm_mix_pre, v_norm_mix_post, v_norm_ffn_pre, v_norm_ffn_post, v_w_in, v_conv_a_w, v_conf_dw_w, v_conf_dw_b, v_conf_ln_g, v_conf_ln_b, v_sgu_ln_g, v_sgu_ln_b, v_sgu_ws, v_sgu_b, v_w_branch, v_w_out, v_w_ff1, v_w_ff2):
    w = dict(norm_mix_pre=norm_mix_pre, norm_mix_post=norm_mix_post, norm_ffn_pre=norm_ffn_pre,
             norm_ffn_post=norm_ffn_post, w_in=w_in, conv_a_w=conv_a_w, conf_dw_w=conf_dw_w, conf_dw_b=conf_dw_b,
             conf_ln_g=conf_ln_g, conf_ln_b=conf_ln_b, sgu_ln_g=sgu_ln_g, sgu_ln_b=sgu_ln_b, sgu_ws=sgu_ws,
             sgu_b=sgu_b, w_branch=w_branch, w_out=w_out, w_ff1=w_ff1, w_ff2=w_ff2)
    mom = dict(norm_mix_pre=m_norm_mix_pre, norm_mix_post=m_norm_mix_post, norm_ffn_pre=m_norm_ffn_pre,
               norm_ffn_post=m_norm_ffn_post, w_in=m_w_in, conv_a_w=m_conv_a_w, conf_dw_w=m_conf_dw_w,
               conf_dw_b=m_conf_dw_b, conf_ln_g=m_conf_ln_g, conf_ln_b=m_conf_ln_b, sgu_ln_g=m_sgu_ln_g,
               sgu_ln_b=m_sgu_ln_b, sgu_ws=m_sgu_ws, sgu_b=m_sgu_b, w_branch=m_w_branch, w_out=m_w_out,
               w_ff1=m_w_ff1, w_ff2=m_w_ff2)
    var = dict(norm_mix_pre=v_norm_mix_pre, norm_mix_post=v_norm_mix_post, norm_ffn_pre=v_norm_ffn_pre,
               norm_ffn_post=v_norm_ffn_post, w_in=v_w_in, conv_a_w=v_conv_a_w, conf_dw_w=v_conf_dw_w,
               conf_dw_b=v_conf_dw_b, conf_ln_g=v_conf_ln_g, conf_ln_b=v_conf_ln_b, sgu_ln_g=v_sgu_ln_g,
               sgu_ln_b=v_sgu_ln_b, sgu_ws=v_sgu_ws, sgu_b=v_sgu_b, w_branch=v_w_branch, w_out=v_w_out,
               w_ff1=v_w_ff1, w_ff2=v_w_ff2)
    L = w_in.shape[0]
    nseq, S, _ = x.shape
    T = nseq * S
    rk = D // NSH
    mx, my, mc = _me()
    k_chip = 2 * mx + my

    big_src = [w_in.reshape(L, 1, D, w_in.shape[2]), w_branch, w_out.reshape(L, 1, rk, D),
               w_ff1.reshape(L, 1, D, w_ff1.shape[2]), w_ff2.reshape(L, 1, w_ff2.shape[1], D)]
    kidx = jnp.reshape(k_chip, (1,)).astype(jnp.int32)
    conv_src = jnp.concatenate(
        [jnp.pad(conv_a_w, ((0, 0), (0, SUBLANES - KA), (0, 0))), jnp.pad(conf_dw_w, ((0, 0), (0, 1), (0, 0))),
         jnp.zeros((L, CONV_ROWS - SUBLANES - KC - 1, rk), F32)], axis=1)[None]

    def early_params(l, g_in, conv_full):
        return dict(
            g_mix_pre=norm_mix_pre[l][None], g_mix_post=norm_mix_post[l][None], g_ffn_pre=norm_ffn_pre[l][None],
            g_ffn_post=norm_ffn_post[l][None], w_in=g_in, conv_a_w=conv_full[l, :KA],
            conf_dw_w=conv_full[l, SUBLANES:SUBLANES + KC], conf_dw_b=conf_dw_b[l][None],
            conf_ln_g=conf_ln_g[l][None], conf_ln_b=conf_ln_b[l][None], sgu_ln_g=sgu_ln_g[l][None],
            sgu_ln_b=sgu_ln_b[l][None], sgu_ws=sgu_ws[l], sgu_wst=jnp.swapaxes(sgu_ws[l], 1, 2),
            sgu_bt=sgu_b[l].T)

    def late_params(gathered):
        g_br, g_out, g_ff1, g_ff2 = gathered
        return dict(w_branch=g_br.reshape(NSH, 3, rk, D), w_out=g_out.reshape(D, D), w_ff1=g_ff1,
                    w_ff2=g_ff2.reshape(NSH * w_ff2.shape[1], D))

    def gather_start(name, srcs, l, dep):
        lands = [_cast_into(s, lax.empty((NSH * s.shape[1],) + s.shape[2:], F32 if s is conv_src else BF), l, kidx,
                            dep) for s in srcs]
        return _split_call(name, _GatherCopies(len(lands)), [], lands)

    def gather_land(name, flight, after):
        ssem, rsem, _, lands, _ = flight
        _, lands = _split_call(name, _GatherCopies(len(lands)), [], lands, (ssem, rsem), after)
        return _gather_finish(lands)

    zero_tok = jnp.zeros((8, 128), F32)
    xt = x.reshape(T, D)
    layers, saved = [], []
    head = gather_start("gather_start_0a", [big_src[0], conv_src], 0, kidx)
    conv_full = None
    for l in range(L):
        got = gather_land(f"gather_wait_{l}a", head, xt)
        g_in = got[0]
        if l == 0:
            conv_full = got[1].reshape(NSH, L, CONV_ROWS, rk).transpose(1, 2, 0, 3).reshape(L, CONV_ROWS, D)
        tail = gather_start(f"gather_start_{l}b", big_src[1:], l, g_in)
        nxt = {}

        def late(after, l=l, tail=tail, nxt=nxt):
            more = late_params(gather_land(f"gather_wait_{l}b", tail, after))
            if l + 1 == L:
                return more, zero_tok
            nxt["head"] = gather_start(f"gather_start_{l + 1}a", big_src[:1], l + 1, more["w_ff1"])
            return more, nxt["head"][4]

        p = early_params(l, g_in, conv_full)
        xt, sv = _layer_fwd(xt, p, S, tail[4], late)
        head = nxt.get("head")
        layers.append(p)
        saved.append(sv)
    dx, loss_row = _loss_head(xt, loss_target.reshape(T, D))
    loss = lax.psum(loss_row[0, 0], ("x", "y", "c"))

    c_arr = jnp.reshape(mc, (1,)).astype(jnp.int32)
    idx = jnp.stack([k_chip, k_chip ^ 2, k_chip ^ 1, k_chip ^ 3, mc]).astype(jnp.int32)
    fulls = {n: None for n in BIG}
    smalls = [None] * L

    def scatter_start(name, parts):
        sib = _send_halves_to_sibling(parts)
        sums = [_pair_add(p, s, c_arr) for p, s in zip(parts, sib)]
        rcv = [lax.empty(s.shape, s.dtype) for s in sums]
        return _split_call(name, _ScatterCopies(len(sums)), sums, rcv)

    def scatter_land(name, fl, names, l, after):
        ssem, rsem, sums, rcv, _ = fl
        sums, rcv = _split_call(name, _ScatterCopies(len(sums)), sums, rcv, (ssem, rsem), after)
        for n, o, r in zip(names, sums, rcv):
            fulls[n] = _sum_chips(o, r, fulls[n], l, L, idx)

    pending = []
    dep = zero_tok
    for l in reversed(range(L)):
        mine = {}

        def early(parts, l=l, mine=mine):
            br, rest = parts[0], parts[1:]
            mine["a"] = scatter_start(f"scatter_start_{l}a", [br.reshape(NSH * 3, rk, D), *rest])
            return mine["a"][4]

        dx, big, small = _layer_bwd(dx, layers[l], saved[l], S, dep, early)
        smalls[l] = _pack_small(small)
        for args in pending:
            scatter_land(*args, dx)
        last = scatter_start(f"scatter_start_{l}b", [big["w_in"]])
        pending = [(f"scatter_wait_{l}a", mine["a"], BIG[1:], l), (f"scatter_wait_{l}b", last, BIG[:1], l)]
        dep = last[4]
    packed = jnp.concatenate(smalls, axis=0)
    nrow = packed.shape[0]
    ck = jnp.stack([mc, k_chip]).astype(jnp.int32)
    (sib,) = _send_halves_to_sibling([packed.reshape(1, nrow, D)])
    slots = _pair_sum_slot(packed, sib.reshape(nrow // 2, D), ck)
    small_flight = _split_call("small_start", _GatherCopies(1, halves=False), [], [slots])

    for args in pending:
        scatter_land(*args, small_flight[4])
    full = _join_halves([fulls[n] for n in BIG])
    grads = {n: f.reshape(w[n].shape) for n, f in zip(BIG, full)}
    delta, new_m, new_v = {}, {}, {}
    for n in BIG:
        grads[n], delta[n], new_m[n], new_v[n] = _adamw(w[n], grads[n], mom[n], var[n])

    _, (slots,) = _split_call("small_wait", _GatherCopies(1, halves=False), [], small_flight[3],
                              (small_flight[0], small_flight[1]), delta[BIG[-1]])
    (small_sum,) = _join_halves([_sum_slots(slots, ck).reshape(1, 1, nrow, D)])
    shapes = {n: (w[n].shape[1:] if n not in ("conv_a_w", "conf_dw_w") else (w[n].shape[1], D)) for n in SMALL_NAMES}
    sg = _unpack_small(small_sum.reshape(L, PACK_ROWS, D), shapes)
    for n in SMALL_NAMES:
        if n in ("conv_a_w", "conf_dw_w"):
            grads[n] = lax.dynamic_slice_in_dim(sg[n], k_chip * rk, rk, axis=2)
        else:
            grads[n] = sg[n]

    for n in SMALL_NAMES:
        sh = w[n].shape
        flat = (sh[0] * sh[1], sh[2]) if n in ("conv_a_w", "conf_dw_w") else (-1, D)
        g, d, nm, nv = _adamw(*(t.reshape(flat) for t in (w[n], grads[n], mom[n], var[n])))
        grads[n], delta[n], new_m[n], new_v[n] = g.reshape(sh), d.reshape(sh), nm.reshape(sh), nv.reshape(sh)

    return (loss, dx.reshape(x.shape), *[grads[n] for n in WEIGHTS], *[delta[n] for n in WEIGHTS],
            *[new_m[n] for n in WEIGHTS], *[new_v[n] for n in WEIGHTS])
```

```python
import functools

import jax
import jax.numpy as jnp
from jax import lax
from jax.experimental import pallas as pl
from jax.experimental.pallas import tpu as pltpu

D = 1024
HEADS = 8
CHUNK = 128
KA = 3
KC = 31
HALO = 32
SUBLANES = 8
MIX_TILE = 512
WGRAD_TILE = 1024
NSH = 4
NDEV = 8
EPS = 1e-6
BF = jnp.bfloat16
F32 = jnp.float32
VMEM_LIMIT = 56 * 1024 * 1024

ADAM_LR = 0.001
ADAM_B1 = 0.9
ADAM_B2 = 0.999
ADAM_EPS = 1e-08
ADAM_WD = 0.01
ADAM_STEP = 10

MESH = pl.DeviceIdType.MESH
ANY = pl.BlockSpec(memory_space=pl.ANY)


def _cp(*sem):
    return pltpu.CompilerParams(dimension_semantics=sem, vmem_limit_bytes=VMEM_LIMIT)


def _sig(x):
    return 1.0 / (1.0 + jnp.exp(-x))


_GC = 0.7978845608028654


def _gelu(x):
    x2 = x * x
    t = jnp.tanh(_GC * x * (1.0 + 0.044715 * x2))
    y = 0.5 * x * (1.0 + t)
    dy = 0.5 * (1.0 + t) + 0.5 * x * (1.0 - t * t) * _GC * (1.0 + 3.0 * 0.044715 * x2)
    return y, dy


def _rms_fwd(x, g):
    r = lax.rsqrt(jnp.mean(x * x, axis=-1, keepdims=True) + EPS)
    return x * r * g


def _rms_bwd(dy, x, g):
    r = lax.rsqrt(jnp.mean(x * x, axis=-1, keepdims=True) + EPS)
    xn = x * r
    dyg = dy * g
    dx = r * (dyg - xn * jnp.mean(dyg * xn, axis=-1, keepdims=True))
    return dx, jnp.sum(dy * xn, axis=0, keepdims=True)


def _ln_stats(x):
    mu = jnp.mean(x, axis=-1, keepdims=True)
    xc = x - mu
    r = lax.rsqrt(jnp.mean(xc * xc, axis=-1, keepdims=True) + EPS)
    return xc * r, r


def _ln_bwd(dn, n, r):
    return r * (dn - jnp.mean(dn, axis=-1, keepdims=True) - n * jnp.mean(dn * n, axis=-1, keepdims=True))


def _dot(a, b):
    return jnp.dot(a, b, preferred_element_type=F32)


def _dot_nt(a, b):
    return lax.dot_general(a, b, (((1,), (1,)), ((), ())), preferred_element_type=F32)


def _dot_tn(a, b):
    return lax.dot_general(a, b, (((0,), (0,)), ((), ())), preferred_element_type=F32)


def _in_proj(x, g, w, dep):
    T = x.shape[0]
    nc = w.shape[2]
    tm = min(T, 1024)
    tn = nc
    nj = nc // tn

    def body(x_ref, g_ref, w_ref, dep_ref, h_ref, z_ref, h_scr):
        @pl.when((pl.program_id(1) == 0) & (pl.program_id(2) == 0))
        def _():
            h = _rms_fwd(x_ref[...], g_ref[...]).astype(BF)
            h_scr[...] = h
            h_ref[...] = h
        z_ref[...] = _dot(h_scr[...], w_ref[...]).astype(BF)

    return pl.pallas_call(
        body, name="in_proj", grid=(T // tm, NSH, nj),
        in_specs=[pl.BlockSpec((tm, D), lambda i, k, j: (i, 0)),
                  pl.BlockSpec((1, D), lambda i, k, j: (0, 0)),
                  pl.BlockSpec((None, D, tn), lambda i, k, j: (k, 0, j)), ANY],
        out_specs=[pl.BlockSpec((tm, D), lambda i, k, j: (i, 0)),
                   pl.BlockSpec((tm, tn), lambda i, k, j: (i, k * nj + j))],
        out_shape=[jax.ShapeDtypeStruct((T, D), BF), jax.ShapeDtypeStruct((T, NSH * nc), BF)],
        scratch_shapes=[pltpu.VMEM((tm, D), BF)],
        compiler_params=_cp("arbitrary", "arbitrary", "arbitrary"),
    )(x, g, w, dep)


def _tile_specs(tt, nt_total, reverse):
    def tile(i):
        return (nt_total - 1 - i) if reverse else i

    def cur(c):
        return pl.BlockSpec((tt, D), lambda i, *_: (tile(i), c))

    def halo(c):
        return pl.BlockSpec((HALO, D), lambda i, *_: (jnp.maximum(tile(i) * (tt // HALO) - 1, 0), c))

    def row(r=1):
        return pl.BlockSpec((r, D), lambda i, *_: (0, 0))

    return tile, cur, halo, row


RC = 16


def _chunks(tt, fn, group=2):
    def step(c, carry):
        for u in range(group):
            fn(pl.multiple_of((c * group + u) * RC, RC))
        return carry
    lax.fori_loop(0, tt // (RC * group), step, 0)


def _chunk_pairs(tt, fn):
    def step(c, carry):
        fn(pl.multiple_of(c * 2 * RC, RC), pl.multiple_of(c * 2 * RC + RC, RC))
        return carry
    lax.fori_loop(0, tt // (2 * RC), step, 0)


ALL_SHIFTS = tuple(range(SUBLANES))


def _shifts_of(offs):
    return tuple(sorted({o % SUBLANES for o in offs}))


def _shifted_copies(ext, sh, nrows, shifts=ALL_SHIFTS):
    for i, s in enumerate(shifts):
        sh[i] = ext[pl.ds(s, nrows), :]


def _window(sh, o, r0, shifts=ALL_SHIFTS):
    return sh[shifts.index(o % SUBLANES), pl.ds(r0 + (o // SUBLANES) * SUBLANES, RC), :]


def _fill_taps(wb, w_ref, ntap):
    for k in range(ntap):
        wb[k * SUBLANES:(k + 1) * SUBLANES, :] = jnp.broadcast_to(w_ref[k:k + 1, :], (SUBLANES, D))


def _conv_chunks(sh, wb, offs, r0s, shifts=ALL_SHIFTS):
    accs = []
    for r0 in r0s:
        acc = None
        for k, o in enumerate(offs):
            wk = wb[k * SUBLANES:(k + 1) * SUBLANES, :]
            term = jnp.concatenate([wk] * (RC // SUBLANES), axis=0) * _window(sh, o, r0, shifts)
            acc = term if acc is None else acc + term
        accs.append(acc)
    return accs


WG_TAPS = 5


def _conv_wgrad_chunked(dw_ref, d_ref, sh, offs, tt, shifts=ALL_SHIFTS):
    for g0 in range(0, len(offs), WG_TAPS):
        grp = offs[g0:g0 + WG_TAPS]

        def step(c, accs, grp=grp):
            for u in range(2):
                r0 = pl.multiple_of((2 * c + u) * SUBLANES, SUBLANES)
                d = d_ref[pl.ds(r0, SUBLANES), :]
                accs = tuple(
                    a + d * sh[shifts.index(o % SUBLANES), pl.ds(r0 + (o // SUBLANES) * SUBLANES, SUBLANES), :]
                    for a, o in zip(accs, grp))
            return accs
        accs = lax.fori_loop(0, tt // (2 * SUBLANES), step,
                             tuple(jnp.zeros((SUBLANES, D), F32) for _ in grp))
        for j, a in enumerate(accs):
            dw_ref[g0 + j:g0 + j + 1, :] += jnp.sum(a, axis=0, keepdims=True)


def _causal_offsets(ntap):
    return [HALO - (ntap - 1) + k for k in range(ntap)]


def _anticausal_offsets(ntap):
    return [ntap - 1 - k for k in range(ntap)]


def _mix_a_fwd(z, wa, S):
    T = z.shape[0]
    tt = min(S, MIX_TILE)
    nt = S // tt
    _, cur, halo, row = _tile_specs(tt, T // tt, False)

    nrows = HALO + tt
    offs = _causal_offsets(KA)
    shifts = _shifts_of(offs)

    def body(ah, ab, ac, ah_h, ac_h, w_ref, y_ref, ext, sh, wb):
        @pl.when(pl.program_id(0) == 0)
        def _():
            _fill_taps(wb, w_ref, KA)
            ext[nrows:, :] = jnp.zeros((SUBLANES, D), F32)

        first = (pl.program_id(0) % nt) == 0
        ph = ah_h[...].astype(F32) * ac_h[...].astype(F32)
        ext[0:HALO, :] = jnp.where(first, 0.0, ph)

        def prod(r0):
            rows = pl.ds(r0, RC)
            ext[pl.ds(HALO + r0, RC), :] = ah[rows, :].astype(F32) * ac[rows, :].astype(F32)
        _chunks(tt, prod)
        _shifted_copies(ext, sh, nrows, shifts)

        def conv(*r0s):
            for r0, q in zip(r0s, _conv_chunks(sh, wb, offs, r0s, shifts)):
                rows = pl.ds(r0, RC)
                y_ref[rows, :] = (ab[rows, :].astype(F32) * q).astype(BF)
        _chunk_pairs(tt, conv)

    return pl.pallas_call(
        body, name="mix_a_fwd", grid=(T // tt,),
        in_specs=[cur(0), cur(1), cur(2), halo(0), halo(2), row(KA)],
        out_specs=pl.BlockSpec((tt, D), lambda i: (i, 0)),
        out_shape=jax.ShapeDtypeStruct((T, D), BF),
        scratch_shapes=[pltpu.VMEM((nrows + SUBLANES, D), F32), pltpu.VMEM((len(shifts), nrows, D), F32),
                        pltpu.VMEM((KA * SUBLANES, D), F32)],
        compiler_params=_cp("arbitrary"),
    )(z, z, z, z, z, wa)


def _mix_b_fwd(z, wc, bc, lg, lb, S):
    T = z.shape[0]
    tt = min(S, MIX_TILE)
    nt = S // tt
    _, cur, halo, row = _tile_specs(tt, T // tt, False)

    nrows = HALO + tt
    offs = _causal_offsets(KC)

    def body(ca, cg, ca_h, cg_h, w_ref, bc_ref, lg_ref, lb_ref, y_ref, s_ref, ext, sh, wb):
        @pl.when(pl.program_id(0) == 0)
        def _():
            _fill_taps(wb, w_ref, KC)
            ext[nrows:, :] = jnp.zeros((SUBLANES, D), F32)

        first = (pl.program_id(0) % nt) == 0
        rh = ca_h[...].astype(F32) * _sig(cg_h[...].astype(F32))
        ext[0:HALO, :] = jnp.where(first, 0.0, rh)

        def glu(r0):
            rows = pl.ds(r0, RC)
            ext[pl.ds(HALO + r0, RC), :] = ca[rows, :].astype(F32) * _sig(cg[rows, :].astype(F32))
        _chunks(tt, glu)
        _shifted_copies(ext, sh, nrows)

        def conv(*r0s):
            for r0, q in zip(r0s, _conv_chunks(sh, wb, offs, r0s)):
                rows = pl.ds(r0, RC)
                s = q + bc_ref[...]
                s_ref[rows, :] = s.astype(BF)
                n, _ = _ln_stats(s)
                t = n * lg_ref[...] + lb_ref[...]
                y_ref[rows, :] = (t * _sig(t)).astype(BF)
        _chunk_pairs(tt, conv)

    return pl.pallas_call(
        body, name="mix_b_fwd", grid=(T // tt,),
        in_specs=[cur(3), cur(4), halo(3), halo(4), row(KC), row(), row(), row()],
        out_specs=[pl.BlockSpec((tt, D), lambda i: (i, 0))] * 2,
        out_shape=[jax.ShapeDtypeStruct((T, D), BF)] * 2,
        scratch_shapes=[pltpu.VMEM((nrows + SUBLANES, D), F32), pltpu.VMEM((SUBLANES, nrows, D), F32),
                        pltpu.VMEM((KC * SUBLANES, D), F32)],
        compiler_params=_cp("arbitrary"),
    )(z, z, z, z, wc, bc, lg, lb)


def _causal_mask(transposed):
    r = lax.broadcasted_iota(jnp.int32, (CHUNK, CHUNK), 0)
    c = lax.broadcasted_iota(jnp.int32, (CHUNK, CHUNK), 1)
    return (c >= r) if transposed else (r >= c)


def _mix_s_fwd(z, lg, lb, ws, bst, S):
    T = z.shape[0]
    tt = min(S, MIX_TILE)
    _, cur, _, row = _tile_specs(tt, T // tt, False)

    def body(su, sv, lg_ref, lb_ref, ws_ref, bst_ref, y_ref, u_scr, vn_scr):
        u_scr[...] = _gelu(su[...].astype(F32))[0]
        n, _ = _ln_stats(_gelu(sv[...].astype(F32))[0])
        vn_scr[...] = (n * lg_ref[...] + lb_ref[...]).astype(BF)
        mask = _causal_mask(False)
        for h in range(HEADS):
            wm = jnp.where(mask, ws_ref[h], 0.0).astype(BF)
            cols = slice(h * CHUNK, (h + 1) * CHUNK)
            for c in range(tt // CHUNK):
                rows = slice(c * CHUNK, (c + 1) * CHUNK)
                mixed = _dot(wm, vn_scr[rows, cols]) + bst_ref[:, h:h + 1]
                y_ref[rows, cols] = (u_scr[rows, cols] * mixed).astype(BF)

    return pl.pallas_call(
        body, name="mix_s_fwd", grid=(T // tt,),
        in_specs=[cur(5), cur(6), row(), row(),
                  pl.BlockSpec((HEADS, CHUNK, CHUNK), lambda i: (0, 0, 0)),
                  pl.BlockSpec((CHUNK, HEADS), lambda i: (0, 0))],
        out_specs=pl.BlockSpec((tt, D), lambda i: (i, 0)),
        out_shape=jax.ShapeDtypeStruct((T, D), BF),
        scratch_shapes=[pltpu.VMEM((tt, D), F32), pltpu.VMEM((tt, D), BF)],
        compiler_params=_cp("arbitrary"),
    )(z, z, lg, lb, ws, bst)


def _mix_out_fwd(ya, yc, ys, z, x, wb, wo, gp, dep):
    T = x.shape[0]
    tm = min(T, 256)
    rk = D // NSH

    def body(ya_ref, yc_ref, ys_ref, ga, gc, gs, x_ref, wb_ref, wo_ref, gp_ref, dep_ref,
             p_ref, mg_ref, m_ref, x1_ref):
        acc = None
        for b, (y_ref, g_ref) in enumerate(((ya_ref, ga), (yc_ref, gc), (ys_ref, gs))):
            pb = None
            for k in range(NSH):
                part = _dot(y_ref[:, k * rk:(k + 1) * rk], wb_ref[k, b])
                pb = part if pb is None else pb + part
            p_ref[b] = pb.astype(BF)
            term = _sig(g_ref[...].astype(F32)) * pb
            acc = term if acc is None else acc + term
        mg = acc.astype(BF)
        mg_ref[...] = mg
        m = _dot(mg, wo_ref[...])
        m_ref[...] = m.astype(BF)
        x1_ref[...] = x_ref[...] + _rms_fwd(m, gp_ref[...])

    rowblk = pl.BlockSpec((tm, D), lambda i: (i, 0))
    return pl.pallas_call(
        body, name="mix_out_fwd", grid=(T // tm,),
        in_specs=[rowblk, rowblk, rowblk,
                  pl.BlockSpec((tm, D), lambda i: (i, 7)), pl.BlockSpec((tm, D), lambda i: (i, 8)),
                  pl.BlockSpec((tm, D), lambda i: (i, 9)), rowblk,
                  pl.BlockSpec((NSH, 3, rk, D), lambda i: (0, 0, 0, 0)),
                  pl.BlockSpec((D, D), lambda i: (0, 0)),
                  pl.BlockSpec((1, D), lambda i: (0, 0)), ANY],
        out_specs=[pl.BlockSpec((3, tm, D), lambda i: (0, i, 0)), rowblk, rowblk, rowblk],
        out_shape=[jax.ShapeDtypeStruct((3, T, D), BF), jax.ShapeDtypeStruct((T, D), BF),
                   jax.ShapeDtypeStruct((T, D), BF), jax.ShapeDtypeStruct((T, D), F32)],
        compiler_params=_cp("arbitrary"),
    )(ya, yc, ys, z, z, z, x, wb, wo, gp, dep)


def _ffn_fwd(x1, g3, w1, w2, g4):
    T = x1.shape[0]
    tm = min(T, 512)

    def body(x_ref, g3_ref, w1_ref, w2_ref, g4_ref, h_ref, a_ref, f_ref, x2_ref, h_scr, acc):
        k = pl.program_id(1)

        @pl.when(k == 0)
        def _():
            h = _rms_fwd(x_ref[...], g3_ref[...]).astype(BF)
            h_scr[...] = h
            h_ref[...] = h
            acc[...] = jnp.zeros_like(acc)

        a = _dot(h_scr[...], w1_ref[...])
        a_ref[...] = a.astype(BF)
        r = jnp.maximum(a, 0.0)
        acc[...] += _dot((r * r).astype(BF), w2_ref[...])

        @pl.when(k == NSH - 1)
        def _():
            f = acc[...]
            f_ref[...] = f.astype(BF)
            x2_ref[...] = x_ref[...] + _rms_fwd(f, g4_ref[...])

    rowblk = pl.BlockSpec((tm, D), lambda i, k: (i, 0))
    vec = pl.BlockSpec((1, D), lambda i, k: (0, 0))
    return pl.pallas_call(
        body, name="ffn_fwd", grid=(T // tm, NSH),
        in_specs=[rowblk, vec, pl.BlockSpec((None, D, D), lambda i, k: (k, 0, 0)),
                  pl.BlockSpec((D, D), lambda i, k: (k, 0)), vec],
        out_specs=[rowblk, pl.BlockSpec((tm, D), lambda i, k: (i, k)), rowblk, rowblk],
        out_shape=[jax.ShapeDtypeStruct((T, D), BF), jax.ShapeDtypeStruct((T, NSH * D), BF),
                   jax.ShapeDtypeStruct((T, D), BF), jax.ShapeDtypeStruct((T, D), F32)],
        scratch_shapes=[pltpu.VMEM((tm, D), BF), pltpu.VMEM((tm, D), F32)],
        compiler_params=_cp("arbitrary", "arbitrary"),
    )(x1, g3, w1, w2, g4)


def _loss_head(y, target):
    T = y.shape[0]
    tm = min(T, 512)

    def body(y_ref, t_ref, dy_ref, l_ref):
        @pl.when(pl.program_id(0) == 0)
        def _():
            l_ref[...] = jnp.zeros_like(l_ref)
        e = y_ref[...] - t_ref[...]
        dy_ref[...] = e * (1.0 / D)
        l_ref[...] += jnp.sum(e * e) * (0.5 / D)

    rowblk = pl.BlockSpec((tm, D), lambda i: (i, 0))
    return pl.pallas_call(
        body, name="loss_head", grid=(T // tm,),
        in_specs=[rowblk, rowblk],
        out_specs=[rowblk, pl.BlockSpec((1, 128), lambda i: (0, 0))],
        out_shape=[jax.ShapeDtypeStruct((T, D), F32), jax.ShapeDtypeStruct((1, 128), F32)],
        compiler_params=_cp("arbitrary"),
    )(y, target)


def _ffn_bwd(dx2, f, g4, a, w2, w1, x1, g3, dep):
    T = dx2.shape[0]
    tm = min(T, 512)

    def body(dx2_ref, f_ref, g4_ref, a_ref, w2_ref, w1_ref, x1_ref, g3_ref, dep_ref,
             df_ref, da_ref, dx1_ref, dg4_ref, dg3_ref, df_scr, acc):
        i, k = pl.program_id(0), pl.program_id(1)

        @pl.when((i == 0) & (k == 0))
        def _():
            dg4_ref[...] = jnp.zeros_like(dg4_ref)
            dg3_ref[...] = jnp.zeros_like(dg3_ref)

        @pl.when(k == 0)
        def _():
            df, dg = _rms_bwd(dx2_ref[...], f_ref[...].astype(F32), g4_ref[...])
            dg4_ref[...] += dg
            dfb = df.astype(BF)
            df_scr[...] = dfb
            df_ref[...] = dfb
            acc[...] = jnp.zeros_like(acc)

        av = a_ref[...].astype(F32)
        da = (_dot_nt(df_scr[...], w2_ref[...]) * (2.0 * jnp.maximum(av, 0.0))).astype(BF)
        da_ref[...] = da
        acc[...] += _dot_nt(da, w1_ref[...])

        @pl.when(k == NSH - 1)
        def _():
            dx, dg = _rms_bwd(acc[...], x1_ref[...], g3_ref[...])
            dg3_ref[...] += dg
            dx1_ref[...] = dx2_ref[...] + dx

    rowblk = pl.BlockSpec((tm, D), lambda i, k: (i, 0))
    vec = pl.BlockSpec((1, D), lambda i, k: (0, 0))
    return pl.pallas_call(
        body, name="ffn_bwd", grid=(T // tm, NSH),
        in_specs=[rowblk, rowblk, vec, pl.BlockSpec((tm, D), lambda i, k: (i, k)),
                  pl.BlockSpec((D, D), lambda i, k: (k, 0)),
                  pl.BlockSpec((None, D, D), lambda i, k: (k, 0, 0)), rowblk, vec, ANY],
        out_specs=[rowblk, pl.BlockSpec((tm, D), lambda i, k: (i, k)), rowblk, vec, vec],
        out_shape=[jax.ShapeDtypeStruct((T, D), BF), jax.ShapeDtypeStruct((T, NSH * D), BF),
                   jax.ShapeDtypeStruct((T, D), F32), jax.ShapeDtypeStruct((1, D), F32),
                   jax.ShapeDtypeStruct((1, D), F32)],
        scratch_shapes=[pltpu.VMEM((tm, D), BF), pltpu.VMEM((tm, D), F32)],
        compiler_params=_cp("arbitrary", "arbitrary"),
    )(dx2, f, g4, a, w2, w1, x1, g3, dep)


def _wgrad(name, ops, grid, in_specs, out_spec, out_shape, acc_shape, pick=None, relu2=False):
    nt = grid[-1]
    na = len(ops) - 1

    def body(*refs):
        a_refs, b_ref, o_ref, acc = refs[:na], refs[na], refs[na + 1], refs[na + 2]
        t = pl.program_id(len(grid) - 1)

        @pl.when(t == 0)
        def _():
            acc[...] = jnp.zeros_like(acc)

        def add(a_ref):
            av = a_ref[...]
            if relu2:
                r = jnp.maximum(av.astype(F32), 0.0)
                av = (r * r).astype(BF)
            acc[...] += _dot_tn(av, b_ref[...])

        if na == 1:
            add(a_refs[0])
        else:
            sel = pick()
            for n in range(na):
                pl.when(sel == n)(functools.partial(add, a_refs[n]))

        @pl.when(t == nt - 1)
        def _():
            if len(o_ref.shape) == 2:
                o_ref[...] = acc[...].astype(o_ref.dtype)
            else:
                rs = o_ref.shape[1]
                for q in range(o_ref.shape[0]):
                    o_ref[q] = acc[q * rs:(q + 1) * rs, :].astype(o_ref.dtype)

    return pl.pallas_call(
        body, name=name, grid=grid, in_specs=in_specs, out_specs=out_spec, out_shape=out_shape,
        scratch_shapes=[pltpu.VMEM(acc_shape, F32)],
        compiler_params=_cp(*(["arbitrary"] * len(grid))),
    )(*ops)


def _mix_out_bwd(dx1, m, gp, wo, p3, z, wb, dep):
    T = dx1.shape[0]
    tm = min(T, 512)
    rk = D // NSH

    def body(dx1_ref, m_ref, gp_ref, wo_ref, p_ref, g_ref, wb_ref, dep_ref,
             dm_ref, dp_ref, dy_ref, dz_ref, dgp_ref, dmg):
        i, b = pl.program_id(0), pl.program_id(1)

        @pl.when((i == 0) & (b == 0))
        def _():
            dgp_ref[...] = jnp.zeros_like(dgp_ref)

        @pl.when(b == 0)
        def _():
            dm, dg = _rms_bwd(dx1_ref[...], m_ref[...].astype(F32), gp_ref[...])
            dgp_ref[...] += dg
            dmb = dm.astype(BF)
            dm_ref[...] = dmb
            dmg[...] = _dot_nt(dmb, wo_ref[...])

        gate = _sig(g_ref[...].astype(F32))
        d = dmg[...]
        dp = (d * gate).astype(BF)
        dp_ref[...] = dp
        dz_ref[...] = (d * p_ref[...].astype(F32) * gate * (1.0 - gate)).astype(BF)
        for k in range(NSH):
            dy_ref[:, k * rk:(k + 1) * rk] = _dot_nt(dp, wb_ref[k, b]).astype(BF)

    rowblk = pl.BlockSpec((tm, D), lambda i, b: (i, 0))
    br = pl.BlockSpec((None, tm, D), lambda i, b: (b, i, 0))
    vec = pl.BlockSpec((1, D), lambda i, b: (0, 0))
    return pl.pallas_call(
        body, name="mix_out_bwd", grid=(T // tm, 3),
        in_specs=[rowblk, rowblk, vec, pl.BlockSpec((D, D), lambda i, b: (0, 0)), br,
                  pl.BlockSpec((tm, D), lambda i, b: (i, 7 + b)),
                  pl.BlockSpec((NSH, 3, rk, D), lambda i, b: (0, 0, 0, 0)), ANY],
        out_specs=[rowblk, br, br, pl.BlockSpec((tm, D), lambda i, b: (i, 7 + b)), vec],
        out_shape=[jax.ShapeDtypeStruct((T, D), BF), jax.ShapeDtypeStruct((3, T, D), BF),
                   jax.ShapeDtypeStruct((3, T, D), BF), jax.ShapeDtypeStruct((T, 10 * D), BF),
                   jax.ShapeDtypeStruct((1, D), F32)],
        scratch_shapes=[pltpu.VMEM((tm, D), F32)],
        compiler_params=_cp("arbitrary", "arbitrary"),
    )(dx1, m, gp, wo, p3, z, wb, dep)


def _mix_a_bwd(dz, dy3, z, wa, S, dep):
    T = z.shape[0]
    tt = min(S, MIX_TILE)
    nt = S // tt
    ntt = T // tt
    tile, cur, halo, row = _tile_specs(tt, ntt, True)

    nrows = HALO + tt
    coffs, aoffs = _causal_offsets(KA), _anticausal_offsets(KA)
    cshifts, ashifts = _shifts_of(coffs), _shifts_of(aoffs)

    def body(dz_in, dy_ref, ah, ab, ac, ah_h, ac_h, w_ref, dep_ref, dz_ref, dw_ref, ext_p, ext_d, sh, wb, stage):
        i, b = pl.program_id(0), pl.program_id(1)
        ti = ntt - 1 - i

        @pl.when((i == 0) & (b == 0))
        def _():
            dw_ref[...] = jnp.zeros_like(dw_ref)
            ext_d[...] = jnp.zeros_like(ext_d)
            ext_p[nrows:, :] = jnp.zeros((SUBLANES, D), F32)
            _fill_taps(wb, w_ref, KA)

        @pl.when(b == 0)
        def _():
            first = (ti % nt) == 0
            last = (ti % nt) == nt - 1
            ext_p[0:HALO, :] = jnp.where(first, 0.0, ah_h[...].astype(F32) * ac_h[...].astype(F32))
            ext_d[tt:nrows, :] = jnp.where(last, 0.0, ext_d[0:HALO, :])

            def prod(r0):
                rows = pl.ds(r0, RC)
                ext_p[pl.ds(HALO + r0, RC), :] = ah[rows, :].astype(F32) * ac[rows, :].astype(F32)
            _chunks(tt, prod)
            _shifted_copies(ext_p, sh, nrows, cshifts)

            def mid(*r0s):
                for r0, q in zip(r0s, _conv_chunks(sh, wb, coffs, r0s, cshifts)):
                    rows = pl.ds(r0, RC)
                    dy = dy_ref[rows, :].astype(F32)
                    stage[1, rows, :] = (dy * q).astype(BF)
                    ext_d[rows, :] = dy * ab[rows, :].astype(F32)
            _chunk_pairs(tt, mid)
            _conv_wgrad_chunked(dw_ref, ext_d, sh, coffs, tt, cshifts)
            _shifted_copies(ext_d, sh, nrows, ashifts)

            def fin(*r0s):
                for r0, dp in zip(r0s, _conv_chunks(sh, wb, aoffs, r0s, ashifts)):
                    rows = pl.ds(r0, RC)
                    stage[0, rows, :] = (dp * ac[rows, :].astype(F32)).astype(BF)
                    stage[2, rows, :] = (dp * ah[rows, :].astype(F32)).astype(BF)
            _chunk_pairs(tt, fin)

        dz_ref[...] = stage[b]

    return pl.pallas_call(
        body, name="mix_a_bwd", grid=(ntt, 3),
        in_specs=[ANY, pl.BlockSpec((None, tt, D), lambda i, b: (0, tile(i), 0)),
                  cur(0), cur(1), cur(2), halo(0), halo(2), row(KA), ANY],
        out_specs=[pl.BlockSpec((tt, D), lambda i, b: (tile(i), b)), pl.BlockSpec((KA, D), lambda i, b: (0, 0))],
        out_shape=[jax.ShapeDtypeStruct(dz.shape, BF), jax.ShapeDtypeStruct((KA, D), F32)],
        scratch_shapes=[pltpu.VMEM((nrows + SUBLANES, D), F32), pltpu.VMEM((nrows + SUBLANES, D), F32),
                        pltpu.VMEM((max(len(cshifts), len(ashifts)), nrows, D), F32),
                        pltpu.VMEM((KA * SUBLANES, D), F32), pltpu.VMEM((3, tt, D), BF)],
        input_output_aliases={0: 0},
        compiler_params=_cp("arbitrary", "arbitrary"),
    )(dz, dy3, z, z, z, z, z, wa, dep)


def _mix_b_bwd(dz, dy3, s, z, wc, lg, lb, S):
    T = z.shape[0]
    tt = min(S, MIX_TILE)
    nt = S // tt
    ntt = T // tt
    tile, cur, halo, row = _tile_specs(tt, ntt, True)

    nrows = HALO + tt

    def body(dz_in, dy_ref, s_ref, ca, cg, ca_h, cg_h, w_ref, lg_ref, lb_ref,
             dz_ref, dw_ref, dbc_ref, dlg_ref, dlb_ref, ext_r, ext_d, sh, wb, accs, stage):
        i, b = pl.program_id(0), pl.program_id(1)
        ti = ntt - 1 - i

        @pl.when((i == 0) & (b == 0))
        def _():
            dw_ref[...] = jnp.zeros_like(dw_ref)
            dbc_ref[...] = jnp.zeros_like(dbc_ref)
            dlg_ref[...] = jnp.zeros_like(dlg_ref)
            dlb_ref[...] = jnp.zeros_like(dlb_ref)
            ext_d[...] = jnp.zeros_like(ext_d)
            ext_r[nrows:, :] = jnp.zeros((SUBLANES, D), F32)
            _fill_taps(wb, w_ref, KC)

        @pl.when(b == 0)
        def _():
            first = (ti % nt) == 0
            last = (ti % nt) == nt - 1
            ext_r[0:HALO, :] = jnp.where(first, 0.0, ca_h[...].astype(F32) * _sig(cg_h[...].astype(F32)))
            ext_d[tt:nrows, :] = jnp.where(last, 0.0, ext_d[0:HALO, :])
            accs[...] = jnp.zeros_like(accs)

            def point(r0):
                rows = pl.ds(r0, RC)
                n, r = _ln_stats(s_ref[rows, :].astype(F32))
                t = n * lg_ref[...] + lb_ref[...]
                sg = _sig(t)
                dt = dy_ref[rows, :].astype(F32) * (sg * (1.0 + t * (1.0 - sg)))
                accs[0] += dt * n
                accs[1] += dt
                ds = _ln_bwd(dt * lg_ref[...], n, r)
                accs[2] += ds
                ext_d[rows, :] = ds
                ext_r[pl.ds(HALO + r0, RC), :] = ca[rows, :].astype(F32) * _sig(cg[rows, :].astype(F32))
            _chunks(tt, point)
            dlg_ref[...] += jnp.sum(accs[0], axis=0, keepdims=True)
            dlb_ref[...] += jnp.sum(accs[1], axis=0, keepdims=True)
            dbc_ref[...] += jnp.sum(accs[2], axis=0, keepdims=True)

            _shifted_copies(ext_r, sh, nrows)
            _conv_wgrad_chunked(dw_ref, ext_d, sh, _causal_offsets(KC), tt)
            _shifted_copies(ext_d, sh, nrows)

            def conv(*r0s):
                for r0, dr in zip(r0s, _conv_chunks(sh, wb, _anticausal_offsets(KC), r0s)):
                    rows = pl.ds(r0, RC)
                    cav = ca[rows, :].astype(F32)
                    sgc = _sig(cg[rows, :].astype(F32))
                    stage[0, rows, :] = (dr * sgc).astype(BF)
                    stage[1, rows, :] = (dr * cav * sgc * (1.0 - sgc)).astype(BF)
            _chunk_pairs(tt, conv)

        dz_ref[...] = stage[b]

    vec = pl.BlockSpec((1, D), lambda i, b: (0, 0))
    return pl.pallas_call(
        body, name="mix_b_bwd", grid=(ntt, 2),
        in_specs=[ANY, pl.BlockSpec((None, tt, D), lambda i, b: (1, tile(i), 0)),
                  pl.BlockSpec((tt, D), lambda i, b: (tile(i), 0)),
                  cur(3), cur(4), halo(3), halo(4), row(KC), row(), row()],
        out_specs=[pl.BlockSpec((tt, D), lambda i, b: (tile(i), 3 + b)),
                   pl.BlockSpec((KC, D), lambda i, b: (0, 0)), vec, vec, vec],
        out_shape=[jax.ShapeDtypeStruct(dz.shape, BF), jax.ShapeDtypeStruct((KC, D), F32)]
        + [jax.ShapeDtypeStruct((1, D), F32)] * 3,
        scratch_shapes=[pltpu.VMEM((nrows + SUBLANES, D), F32), pltpu.VMEM((nrows + SUBLANES, D), F32),
                        pltpu.VMEM((SUBLANES, nrows, D), F32), pltpu.VMEM((KC * SUBLANES, D), F32),
                        pltpu.VMEM((3, RC, D), F32), pltpu.VMEM((2, tt, D), BF)],
        input_output_aliases={0: 0},
        compiler_params=_cp("arbitrary", "arbitrary"),
    )(dz, dy3, s, z, z, z, z, wc, lg, lb)


def _mix_s_bwd(dz, dy3, z, lg, lb, ws, wst, bst, S):
    T = z.shape[0]
    tt = min(S, MIX_TILE)
    ntt = T // tt
    _, cur, _, row = _tile_specs(tt, ntt, False)

    def body(dz_in, dy_ref, su, sv, lg_ref, lb_ref, ws_ref, wst_ref, bst_ref,
             dz_ref, dws_ref, dbst_ref, dlg_ref, dlb_ref, u_scr, vn_scr, dvn_scr, stage):
        i, b = pl.program_id(0), pl.program_id(1)

        @pl.when((i == 0) & (b == 0))
        def _():
            dws_ref[...] = jnp.zeros_like(dws_ref)
            dbst_ref[...] = jnp.zeros_like(dbst_ref)
            dlg_ref[...] = jnp.zeros_like(dlg_ref)
            dlb_ref[...] = jnp.zeros_like(dlb_ref)

        @pl.when(b == 0)
        def _():
            u, du_dx = _gelu(su[...].astype(F32))
            v, dv_dx = _gelu(sv[...].astype(F32))
            u_scr[...] = u
            n, r = _ln_stats(v)
            vn_scr[...] = (n * lg_ref[...] + lb_ref[...]).astype(BF)
            mask = _causal_mask(False)
            mask_t = _causal_mask(True)
            for h in range(HEADS):
                wm = jnp.where(mask, ws_ref[h], 0.0).astype(BF)
                wmt = jnp.where(mask_t, wst_ref[h], 0.0).astype(BF)
                cols = slice(h * CHUNK, (h + 1) * CHUNK)
                for c in range(tt // CHUNK):
                    rows = slice(c * CHUNK, (c + 1) * CHUNK)
                    vb = vn_scr[rows, cols]
                    mixed = _dot(wm, vb) + bst_ref[:, h:h + 1]
                    dy = dy_ref[rows, cols].astype(F32)
                    dmix = dy * u_scr[rows, cols]
                    u_scr[rows, cols] = dy * mixed
                    dbst_ref[:, h:h + 1] += jnp.sum(dmix, axis=1, keepdims=True)
                    dmb = dmix.astype(BF)
                    dws_ref[h] += _dot_nt(dmb, vb)
                    dvn_scr[rows, cols] = _dot(wmt, dmb)
            stage[0] = (u_scr[...] * du_dx).astype(BF)
            dvn = dvn_scr[...]
            dlg_ref[...] += jnp.sum(dvn * n, axis=0, keepdims=True)
            dlb_ref[...] += jnp.sum(dvn, axis=0, keepdims=True)
            stage[1] = (_ln_bwd(dvn * lg_ref[...], n, r) * dv_dx).astype(BF)

        dz_ref[...] = stage[b]

    vec = pl.BlockSpec((1, D), lambda i, b: (0, 0))
    wsp = pl.BlockSpec((HEADS, CHUNK, CHUNK), lambda i, b: (0, 0, 0))
    bsp = pl.BlockSpec((CHUNK, HEADS), lambda i, b: (0, 0))
    return pl.pallas_call(
        body, name="mix_s_bwd", grid=(ntt, 2),
        in_specs=[ANY, pl.BlockSpec((None, tt, D), lambda i, b: (2, i, 0)),
                  cur(5), cur(6), row(), row(), wsp, wsp, bsp],
        out_specs=[pl.BlockSpec((tt, D), lambda i, b: (i, 5 + b)), wsp, bsp, vec, vec],
        out_shape=[jax.ShapeDtypeStruct(dz.shape, BF), jax.ShapeDtypeStruct((HEADS, CHUNK, CHUNK), F32),
                   jax.ShapeDtypeStruct((CHUNK, HEADS), F32), jax.ShapeDtypeStruct((1, D), F32),
                   jax.ShapeDtypeStruct((1, D), F32)],
        scratch_shapes=[pltpu.VMEM((tt, D), F32), pltpu.VMEM((tt, D), BF), pltpu.VMEM((tt, D), F32),
                        pltpu.VMEM((2, tt, D), BF)],
        input_output_aliases={0: 0},
        compiler_params=_cp("arbitrary", "arbitrary"),
    )(dz, dy3, z, z, lg, lb, ws, wst, bst)


def _in_proj_bwd(dz, w, x, g, dx1, dep):
    T = x.shape[0]
    nc = w.shape[2]
    tm = min(T, 1024)
    tn = 1280
    nj = nc // tn
    ep = min(tm, 128)

    def body(dz_ref, w_ref, x_ref, g_ref, dx1_ref, dep_ref, dx_ref, dg_ref, acc):
        i, k, j = pl.program_id(0), pl.program_id(1), pl.program_id(2)

        @pl.when((i == 0) & (k == 0) & (j == 0))
        def _():
            dg_ref[...] = jnp.zeros_like(dg_ref)

        @pl.when((k == 0) & (j == 0))
        def _():
            acc[...] = jnp.zeros_like(acc)

        acc[...] += _dot_nt(dz_ref[...], w_ref[...])

        @pl.when((k == NSH - 1) & (j == nj - 1))
        def _():
            def step(c, dg):
                rows = pl.ds(pl.multiple_of(c * ep, ep), ep)
                dx, dgc = _rms_bwd(acc[rows, :], x_ref[rows, :], g_ref[...])
                dx_ref[rows, :] = dx1_ref[rows, :] + dx
                return dg + dgc
            dg_ref[...] += lax.fori_loop(0, tm // ep, step, jnp.zeros((1, D), F32))

    rowblk = pl.BlockSpec((tm, D), lambda i, k, j: (i, 0))
    vec = pl.BlockSpec((1, D), lambda i, k, j: (0, 0))
    return pl.pallas_call(
        body, name="in_proj_bwd", grid=(T // tm, NSH, nj),
        in_specs=[pl.BlockSpec((tm, tn), lambda i, k, j: (i, k * nj + j)),
                  pl.BlockSpec((None, D, tn), lambda i, k, j: (k, 0, j)), rowblk, vec, rowblk, ANY],
        out_specs=[rowblk, vec],
        out_shape=[jax.ShapeDtypeStruct((T, D), F32), jax.ShapeDtypeStruct((1, D), F32)],
        scratch_shapes=[pltpu.VMEM((tm, D), F32)],
        compiler_params=_cp("arbitrary", "arbitrary", "arbitrary"),
    )(dz, w, x, g, dx1, dep)


def _layer_fwd(x, p, S, dep, late):
    h, z = _in_proj(x, p["g_mix_pre"], p["w_in"], dep)
    ya = _mix_a_fwd(z, p["conv_a_w"], S)
    yc, s = _mix_b_fwd(z, p["conf_dw_w"], p["conf_dw_b"], p["conf_ln_g"], p["conf_ln_b"], S)
    ys = _mix_s_fwd(z, p["sgu_ln_g"], p["sgu_ln_b"], p["sgu_ws"], p["sgu_bt"], S)
    more, dep2 = late(ys)
    p.update(more)
    p3, merged, m, x1 = _mix_out_fwd(ya, yc, ys, z, x, p["w_branch"], p["w_out"], p["g_mix_post"], dep2)
    h2, a, f, x2 = _ffn_fwd(x1, p["g_ffn_pre"], p["w_ff1"], p["w_ff2"], p["g_ffn_post"])
    saved = dict(x=x, h=h, z=z, ya=ya, yc=yc, ys=ys, s=s, p3=p3, merged=merged, m=m, x1=x1, h2=h2, a=a, f=f)
    return x2, saved


def _layer_bwd(dx2, p, sv, S, dep, hooks):
    after_ffn, early, mid = hooks
    T = dx2.shape[0]
    bt = min(T, WGRAD_TILE)
    nt = T // bt
    rk = D // NSH
    df, da, dx1, dg_ffn_post, dg_ffn_pre = _ffn_bwd(dx2, sv["f"], p["g_ffn_post"], sv["a"], p["w_ff2"],
                                                    p["w_ff1"], sv["x1"], p["g_ffn_pre"], dep)
    dw_ff2 = _wgrad("wgrad_ff2", (sv["a"], df), (NSH, nt),
                    [pl.BlockSpec((bt, D), lambda k, t: (t, k)), pl.BlockSpec((bt, D), lambda k, t: (t, 0))],
                    pl.BlockSpec((None, D, D), lambda k, t: (k, 0, 0)),
                    jax.ShapeDtypeStruct((NSH, D, D), BF), (D, D), relu2=True)
    dw_ff1 = _wgrad("wgrad_ff1", (sv["h2"], da), (NSH, nt),
                    [pl.BlockSpec((bt, D), lambda k, t: (t, 0)), pl.BlockSpec((bt, D), lambda k, t: (t, k))],
                    pl.BlockSpec((None, D, D), lambda k, t: (k, 0, 0)),
                    jax.ShapeDtypeStruct((NSH, D, D), BF), (D, D))
    dm, dp3, dy3, dz, dg_mix_post = _mix_out_bwd(dx1, sv["m"], p["g_mix_post"], p["w_out"], sv["p3"], sv["z"],
                                                 p["w_branch"], after_ffn(dx1))
    dw_out = _wgrad("wgrad_out", (sv["merged"], dm), (nt,),
                    [pl.BlockSpec((bt, D), lambda t: (t, 0)), pl.BlockSpec((bt, D), lambda t: (t, 0))],
                    pl.BlockSpec((D, D), lambda t: (0, 0)),
                    jax.ShapeDtypeStruct((D, D), BF), (D, D)).reshape(NSH, rk, D)
    ysp = lambda n: pl.BlockSpec((bt, D), lambda b, t: (jnp.where(b == n, t, 0), 0))
    dw_br = _wgrad("wgrad_branch", (sv["ya"], sv["yc"], sv["ys"], dp3), (3, nt),
                   [ysp(0), ysp(1), ysp(2), pl.BlockSpec((None, bt, D), lambda b, t: (b, t, 0))],
                   pl.BlockSpec((NSH, None, rk, D), lambda b, t: (0, b, 0, 0)),
                   jax.ShapeDtypeStruct((NSH, 3, rk, D), BF), (D, D), pick=lambda: pl.program_id(0))
    dz, dwa = _mix_a_bwd(dz, dy3, sv["z"], p["conv_a_w"], S, early([dw_br, dw_out, dw_ff1, dw_ff2]))
    dz, dwc, dbc, dclg, dclb = _mix_b_bwd(dz, dy3, sv["s"], sv["z"], p["conf_dw_w"], p["conf_ln_g"],
                                          p["conf_ln_b"], S)
    dz, dws, dbst, dslg, dslb = _mix_s_bwd(dz, dy3, sv["z"], p["sgu_ln_g"], p["sgu_ln_b"], p["sgu_ws"],
                                           p["sgu_wst"], p["sgu_bt"], S)
    dx, dg_mix_pre = _in_proj_bwd(dz, p["w_in"], sv["x"], p["g_mix_pre"], dx1, mid(dz))
    tn = 1280
    nj = p["w_in"].shape[2] // tn
    dw_in = _wgrad("wgrad_in", (sv["h"], dz), (NSH, nj, nt),
                   [pl.BlockSpec((bt, D), lambda k, j, t: (t, 0)),
                    pl.BlockSpec((bt, tn), lambda k, j, t: (t, k * nj + j))],
                   pl.BlockSpec((None, D, tn), lambda k, j, t: (k, 0, j)),
                   jax.ShapeDtypeStruct(p["w_in"].shape, BF), (D, tn))
    tril = jnp.tril(jnp.ones((CHUNK, CHUNK), bool))
    small = dict(norm_mix_pre=dg_mix_pre, norm_mix_post=dg_mix_post, norm_ffn_pre=dg_ffn_pre,
                 norm_ffn_post=dg_ffn_post, conv_a_w=dwa, conf_dw_w=dwc, conf_dw_b=dbc, conf_ln_g=dclg,
                 conf_ln_b=dclb, sgu_ln_g=dslg, sgu_ln_b=dslb,
                 sgu_ws=jnp.where(tril[None], dws, 0.0), sgu_b=dbst.T)
    big = dict(w_in=dw_in, w_branch=dw_br, w_out=dw_out, w_ff1=dw_ff1, w_ff2=dw_ff2)
    return dx, big, small


SMALL_NAMES = ("norm_mix_pre", "norm_mix_post", "norm_ffn_pre", "norm_ffn_post", "conv_a_w", "conf_dw_w",
               "conf_dw_b", "conf_ln_g", "conf_ln_b", "sgu_ln_g", "sgu_ln_b", "sgu_b", "sgu_ws")
SMALL_ROWS = dict(norm_mix_pre=1, norm_mix_post=1, norm_ffn_pre=1, norm_ffn_post=1, conv_a_w=KA, conf_dw_w=KC,
                  conf_dw_b=1, conf_ln_g=1, conf_ln_b=1, sgu_ln_g=1, sgu_ln_b=1, sgu_b=1, sgu_ws=CHUNK)
def _pad8(r):
    return -(-r // SUBLANES) * SUBLANES


PACK_ROWS = sum(_pad8(r) for r in SMALL_ROWS.values())


def _pack_small(d):
    parts = []
    for n in SMALL_NAMES:
        r = SMALL_ROWS[n]
        parts.append(jnp.pad(d[n].reshape(r, D).astype(F32), ((0, _pad8(r) - r), (0, 0))))
    return jnp.concatenate(parts, axis=0)


def _unpack_small(a, shapes):
    out, r = {}, 0
    for n in SMALL_NAMES:
        out[n] = a[:, r:r + SMALL_ROWS[n]].reshape((a.shape[0],) + tuple(shapes[n]))
        r += _pad8(SMALL_ROWS[n])
    return out


def _me():
    return lax.axis_index("x"), lax.axis_index("y"), lax.axis_index("c")


def _slab(ref, q, a, h=None):
    r = ref.shape[1]
    rows = slice(None) if h is None else pl.ds(h * (r // 2), r // 2)
    return ref.at[pl.ds(q * a, a), rows, :]


def _rows(ref, h):
    r = ref.shape[-2]
    lead = (slice(None),) * (len(ref.shape) - 2)
    return ref.at[lead + (pl.ds(h * (r // 2), r // 2), slice(None))]


def _rcopy(src, dst, sems, idx, dev):
    return pltpu.make_async_remote_copy(src_ref=src, dst_ref=dst, send_sem=sems[0].at[idx], recv_sem=sems[1].at[idx],
                                        device_id=dev, device_id_type=MESH)


def _send_halves_to_sibling(parts):
    n = len(parts)

    def body(*refs):
        src, dst = refs[:n], refs[n:2 * n]
        sems = refs[2 * n:2 * n + 2]
        x, y, c = _me()
        cps = [_rcopy(_rows(src[i], 1 - c), dst[i], sems, i, (x, y, 1 - c)) for i in range(n)]
        for cp in cps:
            cp.start()
        for cp in cps:
            cp.wait()

    outs = [jax.ShapeDtypeStruct((p.shape[0], p.shape[1] // 2, p.shape[2]), p.dtype) for p in parts]
    return pl.pallas_call(
        body, name="pair_exchange", in_specs=[ANY] * n, out_specs=[ANY] * n, out_shape=outs,
        scratch_shapes=[pltpu.SemaphoreType.DMA((n,)), pltpu.SemaphoreType.DMA((n,))],
    )(*parts)


def _pair_add(part, sib, c):
    A, R, C = part.shape
    hr = R // 2
    br = min(hr, 512)
    nb = hr // br

    def body(c_ref, p_ref, s_ref, o_ref):
        o_ref[...] = (p_ref[...].astype(F32) + s_ref[...].astype(F32)).astype(BF)

    return pl.pallas_call(
        body, name="pair_add",
        grid_spec=pltpu.PrefetchScalarGridSpec(
            num_scalar_prefetch=1, grid=(A, nb),
            in_specs=[pl.BlockSpec((None, br, C), lambda a, i, c_ref: (a, c_ref[0] * nb + i, 0)),
                      pl.BlockSpec((None, br, C), lambda a, i, c_ref: (a, i, 0))],
            out_specs=pl.BlockSpec((None, br, C), lambda a, i, c_ref: (a, i, 0))),
        out_shape=jax.ShapeDtypeStruct((A, hr, C), BF),
        compiler_params=_cp("arbitrary", "arbitrary"),
    )(c, part, sib)


def _other_chips(x, y):
    return [(1 - x, y), (x, 1 - y), (1 - x, 1 - y)]


def _split_call(name, copies, srcs, lands, sems=None, after=()):
    n, m = len(srcs), len(lands)
    hbm = lambda t: pltpu.HBM(t.shape, t.dtype)
    pin = lambda t: pltpu.with_memory_space_constraint(t, pltpu.HBM)
    thru = [hbm(t) for t in srcs] + [hbm(t) for t in lands]
    sem_spec = pl.BlockSpec(memory_space=pltpu.SEMAPHORE)
    effect = pltpu.CompilerParams(has_side_effects=pltpu.SideEffectType.DATAFLOW_SIDE_EFFECTING)
    if sems is None:
        def start_body(*refs):
            src, land = refs[:n], refs[n:n + m]
            ssem, rsem = refs[n + m + len(after)], refs[n + m + len(after) + 1]
            token = refs[-1]
            cps = copies(src, land, (ssem, rsem))
            for cp in cps:
                cp.start()
            token[...] = jnp.zeros_like(token)

        ncp = copies.count
        out = pl.pallas_call(
            start_body, name=name,
            out_shape=(pltpu.SemaphoreType.DMA((ncp,)), pltpu.SemaphoreType.DMA((ncp,)), *thru,
                       jax.ShapeDtypeStruct((8, 128), F32)),
            in_specs=[ANY] * (n + m + len(after)),
            out_specs=(sem_spec, sem_spec, *([ANY] * (n + m)), pl.BlockSpec(memory_space=pltpu.VMEM)),
            input_output_aliases={i: 2 + i for i in range(n + m)},
            compiler_params=effect,
        )(*[pin(t) for t in srcs], *[pin(t) for t in lands], *after)
        return out[0], out[1], list(out[2:2 + n]), list(out[2 + n:2 + n + m]), out[-1]

    def wait_body(*refs):
        src, land = refs[:n], refs[n:n + m]
        ssem, rsem = refs[n + m], refs[n + m + 1]
        for cp in copies(src, land, (ssem, rsem)):
            cp.wait_send()
            cp.wait_recv()

    out = pl.pallas_call(
        wait_body, name=name, out_shape=tuple(thru),
        in_specs=[ANY] * (n + m) + [sem_spec, sem_spec] + [ANY] * len(after),
        out_specs=tuple([ANY] * (n + m)),
        input_output_aliases={i: i for i in range(n + m)},
        compiler_params=effect,
    )(*srcs, *lands, sems[0], sems[1], *after)
    return list(out[:n]), list(out[n:])


def _cast_into(w, land, layer, kidx, dep):
    _, a, R, C = w.shape
    br = R
    while br * C > 512 * 1024 and br % 32 == 0:
        br //= 2

    def body(k_ref, w_ref, land_ref, dep_ref, o_ref):
        o_ref[...] = w_ref[...].astype(o_ref.dtype)

    return pl.pallas_call(
        body, name="cast_into",
        grid_spec=pltpu.PrefetchScalarGridSpec(
            num_scalar_prefetch=1, grid=(a, R // br),
            in_specs=[pl.BlockSpec((None, None, br, C), lambda e, i, k: (layer, e, i, 0)), ANY, ANY],
            out_specs=pl.BlockSpec((None, br, C), lambda e, i, k: (k[0] * a + e, i, 0))),
        out_shape=jax.ShapeDtypeStruct(land.shape, land.dtype), input_output_aliases={2: 0},
        compiler_params=_cp("arbitrary", "arbitrary"),
    )(kidx, w, land, dep)


class _GatherCopies:
    def __init__(self, n, halves=True):
        self.n, self.count, self.halves = n, 3 * n, halves

    def __call__(self, src, land, sems):
        x, y, c = _me()
        k = 2 * x + y
        cps = []
        for j, (qx, qy) in enumerate(_other_chips(x, y)):
            for i in range(self.n):
                mine = _slab(land[i], k, land[i].shape[0] // NSH, c if self.halves else None)
                cps.append(_rcopy(mine, mine, sems, j * self.n + i, (qx, qy, c)))
        return cps


def _gather_finish(lands):
    n = len(lands)

    def body(*refs):
        dst = refs[n:2 * n]
        sems = refs[2 * n:2 * n + 2]
        x, y, c = _me()
        av = [d.shape[0] // NSH for d in dst]
        cps = []
        for j, (qx, qy) in enumerate(_other_chips(x, y)):
            for i in range(n):
                got = _slab(dst[i], 2 * qx + qy, av[i], c)
                cps.append(_rcopy(got, got, sems, j * n + i, (x, y, 1 - c)))
        for cp in cps:
            cp.start()
        for j, (qx, qy) in enumerate(_other_chips(x, y)):
            for i in range(n):
                other = _slab(dst[i], 2 * qx + qy, av[i], 1 - c)
                _rcopy(other, other, sems, j * n + i, (x, y, c)).wait_recv()
        for cp in cps:
            cp.wait_send()

    return pl.pallas_call(
        body, name="gather_finish", in_specs=[ANY] * n, out_specs=[ANY] * n,
        out_shape=[jax.ShapeDtypeStruct(t.shape, t.dtype) for t in lands],
        input_output_aliases={i: i for i in range(n)},
        scratch_shapes=[pltpu.SemaphoreType.DMA((3 * n,)), pltpu.SemaphoreType.DMA((3 * n,))],
    )(*lands)


class _PairCopies:
    def __init__(self, n):
        self.n, self.count = n, n

    def __call__(self, src, land, sems):
        x, y, c = _me()
        return [_rcopy(_rows(src[i], 1 - c), land[i], sems, i, (x, y, 1 - c)) for i in range(self.n)]


class _ScatterCopies:
    def __init__(self, n):
        self.n, self.count = n, 3 * n

    def __call__(self, src, land, sems):
        x, y, c = _me()
        k = 2 * x + y
        cps = []
        for j, (qx, qy) in enumerate(_other_chips(x, y)):
            for i in range(self.n):
                a = src[i].shape[0] // NSH
                cps.append(_rcopy(_slab(src[i], 2 * qx + qy, a), _slab(land[i], k, a), sems, j * self.n + i,
                                  (qx, qy, c)))
        return cps


def _sum_chips(own, rcv, acc, layer, nlayers, idx):
    A, hr, C = rcv.shape
    a = A // NSH
    br = min(hr, 512)
    nb = hr // br

    def body(*refs):
        r0, r1, r2, r3 = refs[1:5]
        o_ref = refs[-1]
        o_ref[...] = ((r0[...].astype(F32) + r1[...].astype(F32)) + r2[...].astype(F32)) + r3[...].astype(F32)

    slot = lambda s: pl.BlockSpec((None, br, C), lambda e, i, ix: (ix[s] * a + e, i, 0))
    ops = [own, rcv, rcv, rcv]
    in_specs = [slot(0), slot(1), slot(2), slot(3)]
    aliases = {}
    if acc is not None:
        ops.append(acc)
        in_specs.append(ANY)
        aliases = {5: 0}
    return pl.pallas_call(
        body, name="sum_chips",
        grid_spec=pltpu.PrefetchScalarGridSpec(
            num_scalar_prefetch=1, grid=(a, nb), in_specs=in_specs,
            out_specs=pl.BlockSpec((None, None, br, C), lambda e, i, ix: (layer, e, ix[4] * nb + i, 0))),
        out_shape=jax.ShapeDtypeStruct((nlayers, a, 2 * hr, C), F32), input_output_aliases=aliases,
        compiler_params=_cp("arbitrary", "arbitrary"),
    )(idx, *ops)


def _join_halves(fulls):
    n = len(fulls)

    def body(*refs):
        buf = refs[n:2 * n]
        sems = refs[2 * n:2 * n + 2]
        x, y, c = _me()
        cps = [_rcopy(_rows(buf[i], c), _rows(buf[i], c), sems, i, (x, y, 1 - c)) for i in range(n)]
        for cp in cps:
            cp.start()
        for i in range(n):
            _rcopy(_rows(buf[i], 1 - c), _rows(buf[i], 1 - c), sems, i, (x, y, c)).wait_recv()
        for cp in cps:
            cp.wait_send()

    return pl.pallas_call(
        body, name="join_halves", in_specs=[ANY] * n, out_specs=[ANY] * n,
        out_shape=[jax.ShapeDtypeStruct(t.shape, t.dtype) for t in fulls],
        input_output_aliases={i: i for i in range(n)},
        scratch_shapes=[pltpu.SemaphoreType.DMA((n,)), pltpu.SemaphoreType.DMA((n,))],
    )(*fulls)


def _small_blocks(hr):
    br = hr
    while br > 512 and br % 16 == 0:
        br //= 2
    return br, hr // br


def _pair_sum_slot(part, sib, ck):
    R, C = part.shape
    hr = R // 2
    br, nb = _small_blocks(hr)

    def body(ix, p_ref, s_ref, o_ref):
        o_ref[...] = p_ref[...] + s_ref[...]

    return pl.pallas_call(
        body, name="pair_sum_slot",
        grid_spec=pltpu.PrefetchScalarGridSpec(
            num_scalar_prefetch=1, grid=(nb,),
            in_specs=[pl.BlockSpec((br, C), lambda i, ix: (ix[0] * nb + i, 0)),
                      pl.BlockSpec((br, C), lambda i, ix: (i, 0))],
            out_specs=pl.BlockSpec((None, br, C), lambda i, ix: (ix[1], i, 0))),
        out_shape=jax.ShapeDtypeStruct((NSH, hr, C), F32),
        compiler_params=_cp("arbitrary"),
    )(ck, part, sib)


def _sum_slots(slots, ck):
    _, hr, C = slots.shape
    br, nb = _small_blocks(hr)

    def body(ix, s_ref, o_ref):
        o_ref[...] = ((s_ref[0] + s_ref[1]) + s_ref[2]) + s_ref[3]

    return pl.pallas_call(
        body, name="sum_slots",
        grid_spec=pltpu.PrefetchScalarGridSpec(
            num_scalar_prefetch=1, grid=(nb,),
            in_specs=[pl.BlockSpec((NSH, br, C), lambda i, ix: (0, i, 0))],
            out_specs=pl.BlockSpec((br, C), lambda i, ix: (ix[0] * nb + i, 0))),
        out_shape=jax.ShapeDtypeStruct((2 * hr, C), F32),
        compiler_params=_cp("arbitrary"),
    )(ck, slots)


def _adamw(w, g, m, v):
    shape = w.shape
    C = shape[-1]
    R = shape[-2]
    A = 1
    for s in shape[:-2]:
        A *= s
    br = R
    while br * C > 256 * 1024 and br % 16 == 0:
        br //= 2
    c1 = 1.0 / (1.0 - ADAM_B1 ** ADAM_STEP)
    c2 = 1.0 / (1.0 - ADAM_B2 ** ADAM_STEP)

    def body(w_ref, g_ref, m_ref, v_ref, og_ref, d_ref, nm_ref, nv_ref):
        gv = g_ref[...]
        og_ref[...] = gv
        nm = ADAM_B1 * m_ref[...] + (1.0 - ADAM_B1) * gv
        nv = ADAM_B2 * v_ref[...] + (1.0 - ADAM_B2) * (gv * gv)
        nm_ref[...] = nm
        nv_ref[...] = nv
        d_ref[...] = -ADAM_LR * ((nm * c1) / (jnp.sqrt(nv * c2) + ADAM_EPS) + ADAM_WD * w_ref[...])

    blk = pl.BlockSpec((None, br, C), lambda a, i: (a, i, 0))
    outs = pl.pallas_call(
        body, name="adamw", grid=(A, R // br), in_specs=[blk] * 4, out_specs=[blk] * 4,
        out_shape=[jax.ShapeDtypeStruct((A, R, C), F32)] * 4,
        compiler_params=_cp("arbitrary", "arbitrary"),
    )(*(t.reshape(A, R, C) for t in (w, g, m, v)))
    return tuple(o.reshape(shape) for o in outs)


WEIGHTS = ("norm_mix_pre", "norm_mix_post", "norm_ffn_pre", "norm_ffn_post", "w_in", "conv_a_w", "conf_dw_w",
           "conf_dw_b", "conf_ln_g", "conf_ln_b", "sgu_ln_g", "sgu_ln_b", "sgu_ws", "sgu_b", "w_branch", "w_out",
           "w_ff1", "w_ff2")
BIG = ("w_in", "w_branch", "w_out", "w_ff1", "w_ff2")
CONV_ROWS = 48


def kernel(x, norm_mix_pre, norm_mix_post, norm_ffn_pre, norm_ffn_post, w_in, conv_a_w, conf_dw_w, conf_dw_b, conf_ln_g, conf_ln_b, sgu_ln_g, sgu_ln_b, sgu_ws, sgu_b, w_branch, w_out, w_ff1, w_ff2, loss_target, m_norm_mix_pre, m_norm_mix_post, m_norm_ffn_pre, m_norm_ffn_post, m_w_in, m_conv_a_w, m_conf_dw_w, m_conf_dw_b, m_conf_ln_g, m_conf_ln_b, m_sgu_ln_g, m_sgu_ln_b, m_sgu_ws, m_sgu_b, m_w_branch, m_w_out, m_w_ff1, m_w_ff2, v_norm_mix_pre, v_norm_mix_post, v_norm_ffn_pre, v_norm_ffn_post, v_w_in, v_conv_a_w, v_conf_dw_w, v_conf_dw_b, v_conf_ln_g, v_conf_ln_b, v_sgu_ln_g, v_sgu_ln_b, v_sgu_ws, v_sgu_b, v_w_branch, v_w_out, v_w_ff1, v_w_ff2):
    w = dict(norm_mix_pre=norm_mix_pre, norm_mix_post=norm_mix_post, norm_ffn_pre=norm_ffn_pre,
             norm_ffn_post=norm_ffn_post, w_in=w_in, conv_a_w=conv_a_w, conf_dw_w=conf_dw_w, conf_dw_b=conf_dw_b,
             conf_ln_g=conf_ln_g, conf_ln_b=conf_ln_b, sgu_ln_g=sgu_ln_g, sgu_ln_b=sgu_ln_b, sgu_ws=sgu_ws,
             sgu_b=sgu_b, w_branch=w_branch, w_out=w_out, w_ff1=w_ff1, w_ff2=w_ff2)
    mom = dict(norm_mix_pre=m_norm_mix_pre, norm_mix_post=m_norm_mix_post, norm_ffn_pre=m_norm_ffn_pre,
               norm_ffn_post=m_norm_ffn_post, w_in=m_w_in, conv_a_w=m_conv_a_w, conf_dw_w=m_conf_dw_w,
               conf_dw_b=m_conf_dw_b, conf_ln_g=m_conf_ln_g, conf_ln_b=m_conf_ln_b, sgu_ln_g=m_sgu_ln_g,
               sgu_ln_b=m_sgu_ln_b, sgu_ws=m_sgu_ws, sgu_b=m_sgu_b, w_branch=m_w_branch, w_out=m_w_out,
               w_ff1=m_w_ff1, w_ff2=m_w_ff2)
    var = dict(norm_mix_pre=v_norm_mix_pre, norm_mix_post=v_norm_mix_post, norm_ffn_pre=v_norm_ffn_pre,
               norm_ffn_post=v_norm_ffn_post, w_in=v_w_in, conv_a_w=v_conv_a_w, conf_dw_w=v_conf_dw_w,
               conf_dw_b=v_conf_dw_b, conf_ln_g=v_conf_ln_g, conf_ln_b=v_conf_ln_b, sgu_ln_g=v_sgu_ln_g,
               sgu_ln_b=v_sgu_ln_b, sgu_ws=v_sgu_ws, sgu_b=v_sgu_b, w_branch=v_w_branch, w_out=v_w_out,
               w_ff1=v_w_ff1, w_ff2=v_w_ff2)
    L = w_in.shape[0]
    nseq, S, _ = x.shape
    T = nseq * S
    rk = D // NSH
    mx, my, mc = _me()
    k_chip = 2 * mx + my

    big_src = [w_in.reshape(L, 1, D, w_in.shape[2]), w_branch, w_out.reshape(L, 1, rk, D),
               w_ff1.reshape(L, 1, D, w_ff1.shape[2]), w_ff2.reshape(L, 1, w_ff2.shape[1], D)]
    kidx = jnp.reshape(k_chip, (1,)).astype(jnp.int32)
    conv_src = jnp.concatenate(
        [jnp.pad(conv_a_w, ((0, 0), (0, SUBLANES - KA), (0, 0))), jnp.pad(conf_dw_w, ((0, 0), (0, 1), (0, 0))),
         jnp.zeros((L, CONV_ROWS - SUBLANES - KC - 1, rk), F32)], axis=1)[None]

    def early_params(l, g_in, conv_full):
        return dict(
            g_mix_pre=norm_mix_pre[l][None], g_mix_post=norm_mix_post[l][None], g_ffn_pre=norm_ffn_pre[l][None],
            g_ffn_post=norm_ffn_post[l][None], w_in=g_in, conv_a_w=conv_full[l, :KA],
            conf_dw_w=conv_full[l, SUBLANES:SUBLANES + KC], conf_dw_b=conf_dw_b[l][None],
            conf_ln_g=conf_ln_g[l][None], conf_ln_b=conf_ln_b[l][None], sgu_ln_g=sgu_ln_g[l][None],
            sgu_ln_b=sgu_ln_b[l][None], sgu_ws=sgu_ws[l], sgu_wst=jnp.swapaxes(sgu_ws[l], 1, 2),
            sgu_bt=sgu_b[l].T)

    def late_params(gathered):
        g_br, g_out, g_ff1, g_ff2 = gathered
        return dict(w_branch=g_br.reshape(NSH, 3, rk, D), w_out=g_out.reshape(D, D), w_ff1=g_ff1,
                    w_ff2=g_ff2.reshape(NSH * w_ff2.shape[1], D))

    def cast_lands(srcs, l, dep):
        return [_cast_into(s, lax.empty((NSH * s.shape[1],) + s.shape[2:], F32 if s is conv_src else BF), l, kidx,
                           dep) for s in srcs]

    def gather_start(name, lands, after):
        return _split_call(name, _GatherCopies(len(lands)), [], lands, after=after)

    def gather_land(name, flight, after):
        ssem, rsem, _, lands, _ = flight
        _, lands = _split_call(name, _GatherCopies(len(lands)), [], lands, (ssem, rsem), after)
        return _gather_finish(lands)

    zero_tok = jnp.zeros((8, 128), F32)
    xt = x.reshape(T, D)
    layers, saved = [], []
    head = gather_start("gather_start_0a", cast_lands([big_src[0], conv_src], 0, kidx), [])
    tails = [cast_lands(big_src[1:], l, head[4]) for l in range(L)]
    heads = [None] + [cast_lands(big_src[:1], l, head[4]) for l in range(1, L)]
    behind = [xt] + [t for ls in tails + heads[1:] for t in ls]
    conv_full = None
    for l in range(L):
        got = gather_land(f"gather_wait_{l}a", head, behind if l == 0 else [xt])
        g_in = got[0]
        if l == 0:
            conv_full = got[1].reshape(NSH, L, CONV_ROWS, rk).transpose(1, 2, 0, 3).reshape(L, CONV_ROWS, D)
        tail = gather_start(f"gather_start_{l}b", tails[l], [g_in])
        nxt = {}

        def late(after, l=l, tail=tail, nxt=nxt):
            more = late_params(gather_land(f"gather_wait_{l}b", tail, [after]))
            if l + 1 == L:
                return more, zero_tok
            nxt["head"] = gather_start(f"gather_start_{l + 1}a", heads[l + 1], [more["w_ff1"]])
            return more, nxt["head"][4]

        p = early_params(l, g_in, conv_full)
        xt, sv = _layer_fwd(xt, p, S, tail[4], late)
        head = nxt.get("head")
        layers.append(p)
        saved.append(sv)
    dx, loss_row = _loss_head(xt, loss_target.reshape(T, D))
    loss = lax.psum(loss_row[0, 0], ("x", "y", "c"))

    c_arr = jnp.reshape(mc, (1,)).astype(jnp.int32)
    idx = jnp.stack([k_chip, k_chip ^ 2, k_chip ^ 1, k_chip ^ 3, mc]).astype(jnp.int32)
    fulls = {n: None for n in BIG}
    smalls = [None] * L

    def pair_start(tag, parts):
        lands = [lax.empty((p.shape[0], p.shape[1] // 2, p.shape[2]), p.dtype) for p in parts]
        return _split_call(f"pair_start_{tag}", _PairCopies(len(parts)), parts, lands)

    def pair_land_scatter_start(tag, fl, after):
        ssem, rsem, parts, sib, _ = fl
        parts, sib = _split_call(f"pair_wait_{tag}", _PairCopies(len(parts)), parts, sib, (ssem, rsem), after)
        sums = [_pair_add(p, s, c_arr) for p, s in zip(parts, sib)]
        rcv = [lax.empty(s.shape, s.dtype) for s in sums]
        return _split_call(f"scatter_start_{tag}", _ScatterCopies(len(sums)), sums, rcv)

    def scatter_land(tag, fl, names, l, after):
        ssem, rsem, sums, rcv, _ = fl
        sums, rcv = _split_call(f"scatter_wait_{tag}", _ScatterCopies(len(sums)), sums, rcv, (ssem, rsem), after)
        for n, o, r in zip(names, sums, rcv):
            fulls[n] = _sum_chips(o, r, fulls[n], l, L, idx)

    pending = []
    pair_b = None
    dep = zero_tok
    for l in reversed(range(L)):
        mine = {}

        def after_ffn(arr, l=l, mine=mine, pair_b=pair_b):
            if pair_b is None:
                return zero_tok
            mine["prev_b"] = pair_land_scatter_start(f"{l + 1}b", pair_b, [arr])
            return mine["prev_b"][4]

        def early(parts, l=l, mine=mine):
            br, rest = parts[0], parts[1:]
            mine["pair_a"] = pair_start(f"{l}a", [br.reshape(NSH * 3, rk, D), *rest])
            return mine["pair_a"][4]

        def mid(arr, l=l, mine=mine):
            mine["a"] = pair_land_scatter_start(f"{l}a", mine["pair_a"], [arr])
            return mine["a"][4]

        dx, big, small = _layer_bwd(dx, layers[l], saved[l], S, dep, (after_ffn, early, mid))
        smalls[l] = _pack_small(small)
        for args in pending:
            scatter_land(*args, [dx])
        pending = [(f"{l}a", mine["a"], BIG[1:], l)]
        if "prev_b" in mine:
            pending.append((f"{l + 1}b", mine["prev_b"], BIG[:1], l + 1))
        pair_b = pair_start(f"{l}b", [big["w_in"]])
        dep = pair_b[4]
    pending.append(("0b", pair_land_scatter_start("0b", pair_b, [dx]), BIG[:1], 0))

    packed = jnp.concatenate(smalls, axis=0)
    nrow = packed.shape[0]
    ck = jnp.stack([mc, k_chip]).astype(jnp.int32)
    (sib,) = _send_halves_to_sibling([packed.reshape(1, nrow, D)])
    slots = _pair_sum_slot(packed, sib.reshape(nrow // 2, D), ck)
    small_flight = _split_call("small_start", _GatherCopies(1, halves=False), [], [slots])

    for args in pending:
        scatter_land(*args, [small_flight[4]])
    full = _join_halves([fulls[n] for n in BIG])
    grads = {n: f.reshape(w[n].shape) for n, f in zip(BIG, full)}
    delta, new_m, new_v = {}, {}, {}
    for n in BIG:
        grads[n], delta[n], new_m[n], new_v[n] = _adamw(w[n], grads[n], mom[n], var[n])

    _, (slots,) = _split_call("small_wait", _GatherCopies(1, halves=False), [], small_flight[3],
                              (small_flight[0], small_flight[1]), [delta[BIG[-1]]])
    (small_sum,) = _join_halves([_sum_slots(slots, ck).reshape(1, 1, nrow, D)])
    shapes = {n: (w[n].shape[1:] if n not in ("conv_a_w", "conf_dw_w") else (w[n].shape[1], D)) for n in SMALL_NAMES}
    sg = _unpack_small(small_sum.reshape(L, PACK_ROWS, D), shapes)
    for n in SMALL_NAMES:
        if n in ("conv_a_w", "conf_dw_w"):
            grads[n] = lax.dynamic_slice_in_dim(sg[n], k_chip * rk, rk, axis=2)
        else:
            grads[n] = sg[n]

    for n in SMALL_NAMES:
        sh = w[n].shape
        flat = (sh[0] * sh[1], sh[2]) if n in ("conv_a_w", "conf_dw_w") else (-1, D)
        g, d, nm, nv = _adamw(*(t.reshape(flat) for t in (w[n], grads[n], mom[n], var[n])))
        grads[n], delta[n], new_m[n], new_v[n] = g.reshape(sh), d.reshape(sh), nm.reshape(sh), nv.reshape(sh)

    return (loss, dx.reshape(x.shape), *[grads[n] for n in WEIGHTS], *[delta[n] for n in WEIGHTS],
            *[new_m[n] for n in WEIGHTS], *[new_v[n] for n in WEIGHTS])
```

```python
import functools

import jax
import jax.numpy as jnp
from jax import lax
from jax.experimental import pallas as pl
from jax.experimental.pallas import tpu as pltpu

D = 1024
HEADS = 8
CHUNK = 128
KA = 3
KC = 31
HALO = 32
SUBLANES = 8
MIX_TILE = 512
WGRAD_TILE = 1024
NSH = 4
NDEV = 8
EPS = 1e-6
BF = jnp.bfloat16
F32 = jnp.float32
VMEM_LIMIT = 56 * 1024 * 1024

ADAM_LR = 0.001
ADAM_B1 = 0.9
ADAM_B2 = 0.999
ADAM_EPS = 1e-08
ADAM_WD = 0.01
ADAM_STEP = 10

MESH = pl.DeviceIdType.MESH
ANY = pl.BlockSpec(memory_space=pl.ANY)


def _cp(*sem):
    return pltpu.CompilerParams(dimension_semantics=sem, vmem_limit_bytes=VMEM_LIMIT)


def _sig(x):
    return 1.0 / (1.0 + jnp.exp(-x))


_GC = 0.7978845608028654


def _gelu(x):
    x2 = x * x
    t = jnp.tanh(_GC * x * (1.0 + 0.044715 * x2))
    y = 0.5 * x * (1.0 + t)
    dy = 0.5 * (1.0 + t) + 0.5 * x * (1.0 - t * t) * _GC * (1.0 + 3.0 * 0.044715 * x2)
    return y, dy


def _rms_fwd(x, g):
    r = lax.rsqrt(jnp.mean(x * x, axis=-1, keepdims=True) + EPS)
    return x * r * g


def _rms_bwd(dy, x, g):
    r = lax.rsqrt(jnp.mean(x * x, axis=-1, keepdims=True) + EPS)
    xn = x * r
    dyg = dy * g
    dx = r * (dyg - xn * jnp.mean(dyg * xn, axis=-1, keepdims=True))
    return dx, jnp.sum(dy * xn, axis=0, keepdims=True)


def _ln_stats(x):
    mu = jnp.mean(x, axis=-1, keepdims=True)
    xc = x - mu
    r = lax.rsqrt(jnp.mean(xc * xc, axis=-1, keepdims=True) + EPS)
    return xc * r, r


def _ln_bwd(dn, n, r):
    return r * (dn - jnp.mean(dn, axis=-1, keepdims=True) - n * jnp.mean(dn * n, axis=-1, keepdims=True))


def _dot(a, b):
    return jnp.dot(a, b, preferred_element_type=F32)


def _dot_nt(a, b):
    return lax.dot_general(a, b, (((1,), (1,)), ((), ())), preferred_element_type=F32)


def _dot_tn(a, b):
    return lax.dot_general(a, b, (((0,), (0,)), ((), ())), preferred_element_type=F32)


def _in_proj(x, g, w, dep):
    T = x.shape[0]
    nc = w.shape[2]
    tm = min(T, 1024)
    tn = nc
    nj = nc // tn

    def body(x_ref, g_ref, w_ref, dep_ref, h_ref, z_ref, h_scr):
        @pl.when((pl.program_id(1) == 0) & (pl.program_id(2) == 0))
        def _():
            h = _rms_fwd(x_ref[...], g_ref[...]).astype(BF)
            h_scr[...] = h
            h_ref[...] = h
        z_ref[...] = _dot(h_scr[...], w_ref[...]).astype(BF)

    return pl.pallas_call(
        body, name="in_proj", grid=(T // tm, NSH, nj),
        in_specs=[pl.BlockSpec((tm, D), lambda i, k, j: (i, 0)),
                  pl.BlockSpec((1, D), lambda i, k, j: (0, 0)),
                  pl.BlockSpec((None, D, tn), lambda i, k, j: (k, 0, j)), ANY],
        out_specs=[pl.BlockSpec((tm, D), lambda i, k, j: (i, 0)),
                   pl.BlockSpec((tm, tn), lambda i, k, j: (i, k * nj + j))],
        out_shape=[jax.ShapeDtypeStruct((T, D), BF), jax.ShapeDtypeStruct((T, NSH * nc), BF)],
        scratch_shapes=[pltpu.VMEM((tm, D), BF)],
        compiler_params=_cp("arbitrary", "arbitrary", "arbitrary"),
    )(x, g, w, dep)


def _tile_specs(tt, nt_total, reverse):
    def tile(i):
        return (nt_total - 1 - i) if reverse else i

    def cur(c):
        return pl.BlockSpec((tt, D), lambda i, *_: (tile(i), c))

    def halo(c):
        return pl.BlockSpec((HALO, D), lambda i, *_: (jnp.maximum(tile(i) * (tt // HALO) - 1, 0), c))

    def row(r=1):
        return pl.BlockSpec((r, D), lambda i, *_: (0, 0))

    return tile, cur, halo, row


RC = 16


def _chunks(tt, fn, group=2):
    def step(c, carry):
        for u in range(group):
            fn(pl.multiple_of((c * group + u) * RC, RC))
        return carry
    lax.fori_loop(0, tt // (RC * group), step, 0)


def _chunk_pairs(tt, fn):
    def step(c, carry):
        fn(pl.multiple_of(c * 2 * RC, RC), pl.multiple_of(c * 2 * RC + RC, RC))
        return carry
    lax.fori_loop(0, tt // (2 * RC), step, 0)


ALL_SHIFTS = tuple(range(SUBLANES))


def _shifts_of(offs):
    return tuple(sorted({o % SUBLANES for o in offs}))


def _shifted_copies(ext, sh, nrows, shifts=ALL_SHIFTS):
    for i, s in enumerate(shifts):
        sh[i] = ext[pl.ds(s, nrows), :]


def _window(sh, o, r0, shifts=ALL_SHIFTS):
    return sh[shifts.index(o % SUBLANES), pl.ds(r0 + (o // SUBLANES) * SUBLANES, RC), :]


def _fill_taps(wb, w_ref, ntap):
    for k in range(ntap):
        wb[k * SUBLANES:(k + 1) * SUBLANES, :] = jnp.broadcast_to(w_ref[k:k + 1, :], (SUBLANES, D))


def _conv_chunks(sh, wb, offs, r0s, shifts=ALL_SHIFTS):
    accs = []
    for r0 in r0s:
        acc = None
        for k, o in enumerate(offs):
            wk = wb[k * SUBLANES:(k + 1) * SUBLANES, :]
            term = jnp.concatenate([wk] * (RC // SUBLANES), axis=0) * _window(sh, o, r0, shifts)
            acc = term if acc is None else acc + term
        accs.append(acc)
    return accs


WG_TAPS = 5


def _conv_wgrad_chunked(dw_ref, d_ref, sh, offs, tt, shifts=ALL_SHIFTS):
    for g0 in range(0, len(offs), WG_TAPS):
        grp = offs[g0:g0 + WG_TAPS]

        def step(c, accs, grp=grp):
            for u in range(2):
                r0 = pl.multiple_of((2 * c + u) * SUBLANES, SUBLANES)
                d = d_ref[pl.ds(r0, SUBLANES), :]
                accs = tuple(
                    a + d * sh[shifts.index(o % SUBLANES), pl.ds(r0 + (o // SUBLANES) * SUBLANES, SUBLANES), :]
                    for a, o in zip(accs, grp))
            return accs
        accs = lax.fori_loop(0, tt // (2 * SUBLANES), step,
                             tuple(jnp.zeros((SUBLANES, D), F32) for _ in grp))
        for j, a in enumerate(accs):
            dw_ref[g0 + j:g0 + j + 1, :] += jnp.sum(a, axis=0, keepdims=True)


def _causal_offsets(ntap):
    return [HALO - (ntap - 1) + k for k in range(ntap)]


def _anticausal_offsets(ntap):
    return [ntap - 1 - k for k in range(ntap)]


def _mix_a_fwd(z, wa, S):
    T = z.shape[0]
    tt = min(S, MIX_TILE)
    nt = S // tt
    _, cur, halo, row = _tile_specs(tt, T // tt, False)

    nrows = HALO + tt
    offs = _causal_offsets(KA)
    shifts = _shifts_of(offs)

    def body(ah, ab, ac, ah_h, ac_h, w_ref, y_ref, ext, sh, wb):
        @pl.when(pl.program_id(0) == 0)
        def _():
            _fill_taps(wb, w_ref, KA)
            ext[nrows:, :] = jnp.zeros((SUBLANES, D), F32)

        first = (pl.program_id(0) % nt) == 0
        ph = ah_h[...].astype(F32) * ac_h[...].astype(F32)
        ext[0:HALO, :] = jnp.where(first, 0.0, ph)

        def prod(r0):
            rows = pl.ds(r0, RC)
            ext[pl.ds(HALO + r0, RC), :] = ah[rows, :].astype(F32) * ac[rows, :].astype(F32)
        _chunks(tt, prod)
        _shifted_copies(ext, sh, nrows, shifts)

        def conv(*r0s):
            for r0, q in zip(r0s, _conv_chunks(sh, wb, offs, r0s, shifts)):
                rows = pl.ds(r0, RC)
                y_ref[rows, :] = (ab[rows, :].astype(F32) * q).astype(BF)
        _chunk_pairs(tt, conv)

    return pl.pallas_call(
        body, name="mix_a_fwd", grid=(T // tt,),
        in_specs=[cur(0), cur(1), cur(2), halo(0), halo(2), row(KA)],
        out_specs=pl.BlockSpec((tt, D), lambda i: (i, 0)),
        out_shape=jax.ShapeDtypeStruct((T, D), BF),
        scratch_shapes=[pltpu.VMEM((nrows + SUBLANES, D), F32), pltpu.VMEM((len(shifts), nrows, D), F32),
                        pltpu.VMEM((KA * SUBLANES, D), F32)],
        compiler_params=_cp("arbitrary"),
    )(z, z, z, z, z, wa)


def _mix_b_fwd(z, wc, bc, lg, lb, S):
    T = z.shape[0]
    tt = min(S, MIX_TILE)
    nt = S // tt
    _, cur, halo, row = _tile_specs(tt, T // tt, False)

    nrows = HALO + tt
    offs = _causal_offsets(KC)

    def body(ca, cg, ca_h, cg_h, w_ref, bc_ref, lg_ref, lb_ref, y_ref, s_ref, ext, sh, wb):
        @pl.when(pl.program_id(0) == 0)
        def _():
            _fill_taps(wb, w_ref, KC)
            ext[nrows:, :] = jnp.zeros((SUBLANES, D), F32)

        first = (pl.program_id(0) % nt) == 0
        rh = ca_h[...].astype(F32) * _sig(cg_h[...].astype(F32))
        ext[0:HALO, :] = jnp.where(first, 0.0, rh)

        def glu(r0):
            rows = pl.ds(r0, RC)
            ext[pl.ds(HALO + r0, RC), :] = ca[rows, :].astype(F32) * _sig(cg[rows, :].astype(F32))
        _chunks(tt, glu)
        _shifted_copies(ext, sh, nrows)

        def conv(*r0s):
            for r0, q in zip(r0s, _conv_chunks(sh, wb, offs, r0s)):
                rows = pl.ds(r0, RC)
                s = q + bc_ref[...]
                s_ref[rows, :] = s.astype(BF)
                n, _ = _ln_stats(s)
                t = n * lg_ref[...] + lb_ref[...]
                y_ref[rows, :] = (t * _sig(t)).astype(BF)
        _chunk_pairs(tt, conv)

    return pl.pallas_call(
        body, name="mix_b_fwd", grid=(T // tt,),
        in_specs=[cur(3), cur(4), halo(3), halo(4), row(KC), row(), row(), row()],
        out_specs=[pl.BlockSpec((tt, D), lambda i: (i, 0))] * 2,
        out_shape=[jax.ShapeDtypeStruct((T, D), BF)] * 2,
        scratch_shapes=[pltpu.VMEM((nrows + SUBLANES, D), F32), pltpu.VMEM((SUBLANES, nrows, D), F32),
                        pltpu.VMEM((KC * SUBLANES, D), F32)],
        compiler_params=_cp("arbitrary"),
    )(z, z, z, z, wc, bc, lg, lb)


def _causal_mask(transposed):
    r = lax.broadcasted_iota(jnp.int32, (CHUNK, CHUNK), 0)
    c = lax.broadcasted_iota(jnp.int32, (CHUNK, CHUNK), 1)
    return (c >= r) if transposed else (r >= c)


def _mix_s_fwd(z, lg, lb, ws, bst, S):
    T = z.shape[0]
    tt = min(S, MIX_TILE)
    _, cur, _, row = _tile_specs(tt, T // tt, False)

    def body(su, sv, lg_ref, lb_ref, ws_ref, bst_ref, y_ref, u_scr, vn_scr):
        u_scr[...] = _gelu(su[...].astype(F32))[0]
        n, _ = _ln_stats(_gelu(sv[...].astype(F32))[0])
        vn_scr[...] = (n * lg_ref[...] + lb_ref[...]).astype(BF)
        mask = _causal_mask(False)
        for h in range(HEADS):
            wm = jnp.where(mask, ws_ref[h], 0.0).astype(BF)
            cols = slice(h * CHUNK, (h + 1) * CHUNK)
            for c in range(tt // CHUNK):
                rows = slice(c * CHUNK, (c + 1) * CHUNK)
                mixed = _dot(wm, vn_scr[rows, cols]) + bst_ref[:, h:h + 1]
                y_ref[rows, cols] = (u_scr[rows, cols] * mixed).astype(BF)

    return pl.pallas_call(
        body, name="mix_s_fwd", grid=(T // tt,),
        in_specs=[cur(5), cur(6), row(), row(),
                  pl.BlockSpec((HEADS, CHUNK, CHUNK), lambda i: (0, 0, 0)),
                  pl.BlockSpec((CHUNK, HEADS), lambda i: (0, 0))],
        out_specs=pl.BlockSpec((tt, D), lambda i: (i, 0)),
        out_shape=jax.ShapeDtypeStruct((T, D), BF),
        scratch_shapes=[pltpu.VMEM((tt, D), F32), pltpu.VMEM((tt, D), BF)],
        compiler_params=_cp("arbitrary"),
    )(z, z, lg, lb, ws, bst)


def _mix_out_fwd(ya, yc, ys, z, x, wb, wo, gp, dep):
    T = x.shape[0]
    tm = min(T, 256)
    rk = D // NSH

    def body(ya_ref, yc_ref, ys_ref, ga, gc, gs, x_ref, wb_ref, wo_ref, gp_ref, dep_ref,
             p_ref, mg_ref, m_ref, x1_ref):
        acc = None
        for b, (y_ref, g_ref) in enumerate(((ya_ref, ga), (yc_ref, gc), (ys_ref, gs))):
            pb = None
            for k in range(NSH):
                part = _dot(y_ref[:, k * rk:(k + 1) * rk], wb_ref[k, b])
                pb = part if pb is None else pb + part
            p_ref[b] = pb.astype(BF)
            term = _sig(g_ref[...].astype(F32)) * pb
            acc = term if acc is None else acc + term
        mg = acc.astype(BF)
        mg_ref[...] = mg
        m = _dot(mg, wo_ref[...])
        m_ref[...] = m.astype(BF)
        x1_ref[...] = x_ref[...] + _rms_fwd(m, gp_ref[...])

    rowblk = pl.BlockSpec((tm, D), lambda i: (i, 0))
    return pl.pallas_call(
        body, name="mix_out_fwd", grid=(T // tm,),
        in_specs=[rowblk, rowblk, rowblk,
                  pl.BlockSpec((tm, D), lambda i: (i, 7)), pl.BlockSpec((tm, D), lambda i: (i, 8)),
                  pl.BlockSpec((tm, D), lambda i: (i, 9)), rowblk,
                  pl.BlockSpec((NSH, 3, rk, D), lambda i: (0, 0, 0, 0)),
                  pl.BlockSpec((D, D), lambda i: (0, 0)),
                  pl.BlockSpec((1, D), lambda i: (0, 0)), ANY],
        out_specs=[pl.BlockSpec((3, tm, D), lambda i: (0, i, 0)), rowblk, rowblk, rowblk],
        out_shape=[jax.ShapeDtypeStruct((3, T, D), BF), jax.ShapeDtypeStruct((T, D), BF),
                   jax.ShapeDtypeStruct((T, D), BF), jax.ShapeDtypeStruct((T, D), F32)],
        compiler_params=_cp("arbitrary"),
    )(ya, yc, ys, z, z, z, x, wb, wo, gp, dep)


def _ffn_fwd(x1, g3, w1, w2, g4):
    T = x1.shape[0]
    tm = min(T, 512)

    def body(x_ref, g3_ref, w1_ref, w2_ref, g4_ref, h_ref, a_ref, f_ref, x2_ref, h_scr, acc):
        k = pl.program_id(1)

        @pl.when(k == 0)
        def _():
            h = _rms_fwd(x_ref[...], g3_ref[...]).astype(BF)
            h_scr[...] = h
            h_ref[...] = h
            acc[...] = jnp.zeros_like(acc)

        a = _dot(h_scr[...], w1_ref[...])
        a_ref[...] = a.astype(BF)
        r = jnp.maximum(a, 0.0)
        acc[...] += _dot((r * r).astype(BF), w2_ref[...])

        @pl.when(k == NSH - 1)
        def _():
            f = acc[...]
            f_ref[...] = f.astype(BF)
            x2_ref[...] = x_ref[...] + _rms_fwd(f, g4_ref[...])

    rowblk = pl.BlockSpec((tm, D), lambda i, k: (i, 0))
    vec = pl.BlockSpec((1, D), lambda i, k: (0, 0))
    return pl.pallas_call(
        body, name="ffn_fwd", grid=(T // tm, NSH),
        in_specs=[rowblk, vec, pl.BlockSpec((None, D, D), lambda i, k: (k, 0, 0)),
                  pl.BlockSpec((D, D), lambda i, k: (k, 0)), vec],
        out_specs=[rowblk, pl.BlockSpec((tm, D), lambda i, k: (i, k)), rowblk, rowblk],
        out_shape=[jax.ShapeDtypeStruct((T, D), BF), jax.ShapeDtypeStruct((T, NSH * D), BF),
                   jax.ShapeDtypeStruct((T, D), BF), jax.ShapeDtypeStruct((T, D), F32)],
        scratch_shapes=[pltpu.VMEM((tm, D), BF), pltpu.VMEM((tm, D), F32)],
        compiler_params=_cp("arbitrary", "arbitrary"),
    )(x1, g3, w1, w2, g4)


def _loss_head(y, target):
    T = y.shape[0]
    tm = min(T, 512)

    def body(y_ref, t_ref, dy_ref, l_ref):
        @pl.when(pl.program_id(0) == 0)
        def _():
            l_ref[...] = jnp.zeros_like(l_ref)
        e = y_ref[...] - t_ref[...]
        dy_ref[...] = e * (1.0 / D)
        l_ref[...] += jnp.sum(e * e) * (0.5 / D)

    rowblk = pl.BlockSpec((tm, D), lambda i: (i, 0))
    return pl.pallas_call(
        body, name="loss_head", grid=(T // tm,),
        in_specs=[rowblk, rowblk],
        out_specs=[rowblk, pl.BlockSpec((1, 128), lambda i: (0, 0))],
        out_shape=[jax.ShapeDtypeStruct((T, D), F32), jax.ShapeDtypeStruct((1, 128), F32)],
        compiler_params=_cp("arbitrary"),
    )(y, target)


def _ffn_bwd(dx2, f, g4, a, w2, w1, x1, g3, dep):
    T = dx2.shape[0]
    tm = min(T, 512)

    def body(dx2_ref, f_ref, g4_ref, a_ref, w2_ref, w1_ref, x1_ref, g3_ref, dep_ref,
             df_ref, da_ref, dx1_ref, dg4_ref, dg3_ref, df_scr, acc):
        i, k = pl.program_id(0), pl.program_id(1)

        @pl.when((i == 0) & (k == 0))
        def _():
            dg4_ref[...] = jnp.zeros_like(dg4_ref)
            dg3_ref[...] = jnp.zeros_like(dg3_ref)

        @pl.when(k == 0)
        def _():
            df, dg = _rms_bwd(dx2_ref[...], f_ref[...].astype(F32), g4_ref[...])
            dg4_ref[...] += dg
            dfb = df.astype(BF)
            df_scr[...] = dfb
            df_ref[...] = dfb
            acc[...] = jnp.zeros_like(acc)

        av = a_ref[...].astype(F32)
        da = (_dot_nt(df_scr[...], w2_ref[...]) * (2.0 * jnp.maximum(av, 0.0))).astype(BF)
        da_ref[...] = da
        acc[...] += _dot_nt(da, w1_ref[...])

        @pl.when(k == NSH - 1)
        def _():
            dx, dg = _rms_bwd(acc[...], x1_ref[...], g3_ref[...])
            dg3_ref[...] += dg
            dx1_ref[...] = dx2_ref[...] + dx

    rowblk = pl.BlockSpec((tm, D), lambda i, k: (i, 0))
    vec = pl.BlockSpec((1, D), lambda i, k: (0, 0))
    return pl.pallas_call(
        body, name="ffn_bwd", grid=(T // tm, NSH),
        in_specs=[rowblk, rowblk, vec, pl.BlockSpec((tm, D), lambda i, k: (i, k)),
                  pl.BlockSpec((D, D), lambda i, k: (k, 0)),
                  pl.BlockSpec((None, D, D), lambda i, k: (k, 0, 0)), rowblk, vec, ANY],
        out_specs=[rowblk, pl.BlockSpec((tm, D), lambda i, k: (i, k)), rowblk, vec, vec],
        out_shape=[jax.ShapeDtypeStruct((T, D), BF), jax.ShapeDtypeStruct((T, NSH * D), BF),
                   jax.ShapeDtypeStruct((T, D), F32), jax.ShapeDtypeStruct((1, D), F32),
                   jax.ShapeDtypeStruct((1, D), F32)],
        scratch_shapes=[pltpu.VMEM((tm, D), BF), pltpu.VMEM((tm, D), F32)],
        compiler_params=_cp("arbitrary", "arbitrary"),
    )(dx2, f, g4, a, w2, w1, x1, g3, dep)


def _wgrad(name, ops, grid, in_specs, out_spec, out_shape, acc_shape, pick=None, relu2=False):
    nt = grid[-1]
    na = len(ops) - 1

    def body(*refs):
        a_refs, b_ref, o_ref, acc = refs[:na], refs[na], refs[na + 1], refs[na + 2]
        t = pl.program_id(len(grid) - 1)

        @pl.when(t == 0)
        def _():
            acc[...] = jnp.zeros_like(acc)

        def add(a_ref):
            av = a_ref[...]
            if relu2:
                r = jnp.maximum(av.astype(F32), 0.0)
                av = (r * r).astype(BF)
            acc[...] += _dot_tn(av, b_ref[...])

        if na == 1:
            add(a_refs[0])
        else:
            sel = pick()
            for n in range(na):
                pl.when(sel == n)(functools.partial(add, a_refs[n]))

        @pl.when(t == nt - 1)
        def _():
            if len(o_ref.shape) == 2:
                o_ref[...] = acc[...].astype(o_ref.dtype)
            else:
                rs = o_ref.shape[1]
                for q in range(o_ref.shape[0]):
                    o_ref[q] = acc[q * rs:(q + 1) * rs, :].astype(o_ref.dtype)

    return pl.pallas_call(
        body, name=name, grid=grid, in_specs=in_specs, out_specs=out_spec, out_shape=out_shape,
        scratch_shapes=[pltpu.VMEM(acc_shape, F32)],
        compiler_params=_cp(*(["arbitrary"] * len(grid))),
    )(*ops)


def _mix_out_bwd(dx1, m, gp, wo, p3, z, wb, dep):
    T = dx1.shape[0]
    tm = min(T, 512)
    rk = D // NSH

    def body(dx1_ref, m_ref, gp_ref, wo_ref, p_ref, g_ref, wb_ref, dep_ref,
             dm_ref, dp_ref, dy_ref, dz_ref, dgp_ref, dmg):
        i, b = pl.program_id(0), pl.program_id(1)

        @pl.when((i == 0) & (b == 0))
        def _():
            dgp_ref[...] = jnp.zeros_like(dgp_ref)

        @pl.when(b == 0)
        def _():
            dm, dg = _rms_bwd(dx1_ref[...], m_ref[...].astype(F32), gp_ref[...])
            dgp_ref[...] += dg
            dmb = dm.astype(BF)
            dm_ref[...] = dmb
            dmg[...] = _dot_nt(dmb, wo_ref[...])

        gate = _sig(g_ref[...].astype(F32))
        d = dmg[...]
        dp = (d * gate).astype(BF)
        dp_ref[...] = dp
        dz_ref[...] = (d * p_ref[...].astype(F32) * gate * (1.0 - gate)).astype(BF)
        for k in range(NSH):
            dy_ref[:, k * rk:(k + 1) * rk] = _dot_nt(dp, wb_ref[k, b]).astype(BF)

    rowblk = pl.BlockSpec((tm, D), lambda i, b: (i, 0))
    br = pl.BlockSpec((None, tm, D), lambda i, b: (b, i, 0))
    vec = pl.BlockSpec((1, D), lambda i, b: (0, 0))
    return pl.pallas_call(
        body, name="mix_out_bwd", grid=(T // tm, 3),
        in_specs=[rowblk, rowblk, vec, pl.BlockSpec((D, D), lambda i, b: (0, 0)), br,
                  pl.BlockSpec((tm, D), lambda i, b: (i, 7 + b)),
                  pl.BlockSpec((NSH, 3, rk, D), lambda i, b: (0, 0, 0, 0)), ANY],
        out_specs=[rowblk, br, br, pl.BlockSpec((tm, D), lambda i, b: (i, 7 + b)), vec],
        out_shape=[jax.ShapeDtypeStruct((T, D), BF), jax.ShapeDtypeStruct((3, T, D), BF),
                   jax.ShapeDtypeStruct((3, T, D), BF), jax.ShapeDtypeStruct((T, 10 * D), BF),
                   jax.ShapeDtypeStruct((1, D), F32)],
        scratch_shapes=[pltpu.VMEM((tm, D), F32)],
        compiler_params=_cp("arbitrary", "arbitrary"),
    )(dx1, m, gp, wo, p3, z, wb, dep)


def _mix_a_bwd(dz, dy3, z, wa, S, dep):
    T = z.shape[0]
    tt = min(S, MIX_TILE)
    nt = S // tt
    ntt = T // tt
    tile, cur, halo, row = _tile_specs(tt, ntt, True)

    nrows = HALO + tt
    coffs, aoffs = _causal_offsets(KA), _anticausal_offsets(KA)
    cshifts, ashifts = _shifts_of(coffs), _shifts_of(aoffs)

    def body(dz_in, dy_ref, ah, ab, ac, ah_h, ac_h, w_ref, dep_ref, dz_ref, dw_ref, ext_p, ext_d, sh, wb, stage):
        i, b = pl.program_id(0), pl.program_id(1)
        ti = ntt - 1 - i

        @pl.when((i == 0) & (b == 0))
        def _():
            dw_ref[...] = jnp.zeros_like(dw_ref)
            ext_d[...] = jnp.zeros_like(ext_d)
            ext_p[nrows:, :] = jnp.zeros((SUBLANES, D), F32)
            _fill_taps(wb, w_ref, KA)

        @pl.when(b == 0)
        def _():
            first = (ti % nt) == 0
            last = (ti % nt) == nt - 1
            ext_p[0:HALO, :] = jnp.where(first, 0.0, ah_h[...].astype(F32) * ac_h[...].astype(F32))
            ext_d[tt:nrows, :] = jnp.where(last, 0.0, ext_d[0:HALO, :])

            def prod(r0):
                rows = pl.ds(r0, RC)
                ext_p[pl.ds(HALO + r0, RC), :] = ah[rows, :].astype(F32) * ac[rows, :].astype(F32)
            _chunks(tt, prod)
            _shifted_copies(ext_p, sh, nrows, cshifts)

            def mid(*r0s):
                for r0, q in zip(r0s, _conv_chunks(sh, wb, coffs, r0s, cshifts)):
                    rows = pl.ds(r0, RC)
                    dy = dy_ref[rows, :].astype(F32)
                    stage[1, rows, :] = (dy * q).astype(BF)
                    ext_d[rows, :] = dy * ab[rows, :].astype(F32)
            _chunk_pairs(tt, mid)
            _conv_wgrad_chunked(dw_ref, ext_d, sh, coffs, tt, cshifts)
            _shifted_copies(ext_d, sh, nrows, ashifts)

            def fin(*r0s):
                for r0, dp in zip(r0s, _conv_chunks(sh, wb, aoffs, r0s, ashifts)):
                    rows = pl.ds(r0, RC)
                    stage[0, rows, :] = (dp * ac[rows, :].astype(F32)).astype(BF)
                    stage[2, rows, :] = (dp * ah[rows, :].astype(F32)).astype(BF)
            _chunk_pairs(tt, fin)

        dz_ref[...] = stage[b]

    return pl.pallas_call(
        body, name="mix_a_bwd", grid=(ntt, 3),
        in_specs=[ANY, pl.BlockSpec((None, tt, D), lambda i, b: (0, tile(i), 0)),
                  cur(0), cur(1), cur(2), halo(0), halo(2), row(KA), ANY],
        out_specs=[pl.BlockSpec((tt, D), lambda i, b: (tile(i), b)), pl.BlockSpec((KA, D), lambda i, b: (0, 0))],
        out_shape=[jax.ShapeDtypeStruct(dz.shape, BF), jax.ShapeDtypeStruct((KA, D), F32)],
        scratch_shapes=[pltpu.VMEM((nrows + SUBLANES, D), F32), pltpu.VMEM((nrows + SUBLANES, D), F32),
                        pltpu.VMEM((max(len(cshifts), len(ashifts)), nrows, D), F32),
                        pltpu.VMEM((KA * SUBLANES, D), F32), pltpu.VMEM((3, tt, D), BF)],
        input_output_aliases={0: 0},
        compiler_params=_cp("arbitrary", "arbitrary"),
    )(dz, dy3, z, z, z, z, z, wa, dep)


def _mix_b_bwd(dz, dy3, s, z, wc, lg, lb, S):
    T = z.shape[0]
    tt = min(S, MIX_TILE)
    nt = S // tt
    ntt = T // tt
    tile, cur, halo, row = _tile_specs(tt, ntt, True)

    nrows = HALO + tt

    def body(dz_in, dy_ref, s_ref, ca, cg, ca_h, cg_h, w_ref, lg_ref, lb_ref,
             dz_ref, dw_ref, dbc_ref, dlg_ref, dlb_ref, ext_r, ext_d, sh, wb, accs, stage):
        i, b = pl.program_id(0), pl.program_id(1)
        ti = ntt - 1 - i

        @pl.when((i == 0) & (b == 0))
        def _():
            dw_ref[...] = jnp.zeros_like(dw_ref)
            dbc_ref[...] = jnp.zeros_like(dbc_ref)
            dlg_ref[...] = jnp.zeros_like(dlg_ref)
            dlb_ref[...] = jnp.zeros_like(dlb_ref)
            ext_d[...] = jnp.zeros_like(ext_d)
            ext_r[nrows:, :] = jnp.zeros((SUBLANES, D), F32)
            _fill_taps(wb, w_ref, KC)

        @pl.when(b == 0)
        def _():
            first = (ti % nt) == 0
            last = (ti % nt) == nt - 1
            ext_r[0:HALO, :] = jnp.where(first, 0.0, ca_h[...].astype(F32) * _sig(cg_h[...].astype(F32)))
            ext_d[tt:nrows, :] = jnp.where(last, 0.0, ext_d[0:HALO, :])
            accs[...] = jnp.zeros_like(accs)

            def point(r0):
                rows = pl.ds(r0, RC)
                n, r = _ln_stats(s_ref[rows, :].astype(F32))
                t = n * lg_ref[...] + lb_ref[...]
                sg = _sig(t)
                dt = dy_ref[rows, :].astype(F32) * (sg * (1.0 + t * (1.0 - sg)))
                accs[0] += dt * n
                accs[1] += dt
                ds = _ln_bwd(dt * lg_ref[...], n, r)
                accs[2] += ds
                ext_d[rows, :] = ds
                ext_r[pl.ds(HALO + r0, RC), :] = ca[rows, :].astype(F32) * _sig(cg[rows, :].astype(F32))
            _chunks(tt, point)
            dlg_ref[...] += jnp.sum(accs[0], axis=0, keepdims=True)
            dlb_ref[...] += jnp.sum(accs[1], axis=0, keepdims=True)
            dbc_ref[...] += jnp.sum(accs[2], axis=0, keepdims=True)

            _shifted_copies(ext_r, sh, nrows)
            _conv_wgrad_chunked(dw_ref, ext_d, sh, _causal_offsets(KC), tt)
            _shifted_copies(ext_d, sh, nrows)

            def conv(*r0s):
                for r0, dr in zip(r0s, _conv_chunks(sh, wb, _anticausal_offsets(KC), r0s)):
                    rows = pl.ds(r0, RC)
                    cav = ca[rows, :].astype(F32)
                    sgc = _sig(cg[rows, :].astype(F32))
                    stage[0, rows, :] = (dr * sgc).astype(BF)
                    stage[1, rows, :] = (dr * cav * sgc * (1.0 - sgc)).astype(BF)
            _chunk_pairs(tt, conv)

        dz_ref[...] = stage[b]

    vec = pl.BlockSpec((1, D), lambda i, b: (0, 0))
    return pl.pallas_call(
        body, name="mix_b_bwd", grid=(ntt, 2),
        in_specs=[ANY, pl.BlockSpec((None, tt, D), lambda i, b: (1, tile(i), 0)),
                  pl.BlockSpec((tt, D), lambda i, b: (tile(i), 0)),
                  cur(3), cur(4), halo(3), halo(4), row(KC), row(), row()],
        out_specs=[pl.BlockSpec((tt, D), lambda i, b: (tile(i), 3 + b)),
                   pl.BlockSpec((KC, D), lambda i, b: (0, 0)), vec, vec, vec],
        out_shape=[jax.ShapeDtypeStruct(dz.shape, BF), jax.ShapeDtypeStruct((KC, D), F32)]
        + [jax.ShapeDtypeStruct((1, D), F32)] * 3,
        scratch_shapes=[pltpu.VMEM((nrows + SUBLANES, D), F32), pltpu.VMEM((nrows + SUBLANES, D), F32),
                        pltpu.VMEM((SUBLANES, nrows, D), F32), pltpu.VMEM((KC * SUBLANES, D), F32),
                        pltpu.VMEM((3, RC, D), F32), pltpu.VMEM((2, tt, D), BF)],
        input_output_aliases={0: 0},
        compiler_params=_cp("arbitrary", "arbitrary"),
    )(dz, dy3, s, z, z, z, z, wc, lg, lb)


def _mix_s_bwd(dz, dy3, z, lg, lb, ws, wst, bst, S):
    T = z.shape[0]
    tt = min(S, MIX_TILE)
    ntt = T // tt
    _, cur, _, row = _tile_specs(tt, ntt, False)

    def body(dz_in, dy_ref, su, sv, lg_ref, lb_ref, ws_ref, wst_ref, bst_ref,
             dz_ref, dws_ref, dbst_ref, dlg_ref, dlb_ref, u_scr, vn_scr, dvn_scr, stage):
        i, b = pl.program_id(0), pl.program_id(1)

        @pl.when((i == 0) & (b == 0))
        def _():
            dws_ref[...] = jnp.zeros_like(dws_ref)
            dbst_ref[...] = jnp.zeros_like(dbst_ref)
            dlg_ref[...] = jnp.zeros_like(dlg_ref)
            dlb_ref[...] = jnp.zeros_like(dlb_ref)

        @pl.when(b == 0)
        def _():
            u, du_dx = _gelu(su[...].astype(F32))
            v, dv_dx = _gelu(sv[...].astype(F32))
            u_scr[...] = u
            n, r = _ln_stats(v)
            vn_scr[...] = (n * lg_ref[...] + lb_ref[...]).astype(BF)
            mask = _causal_mask(False)
            mask_t = _causal_mask(True)
            for h in range(HEADS):
                wm = jnp.where(mask, ws_ref[h], 0.0).astype(BF)
                wmt = jnp.where(mask_t, wst_ref[h], 0.0).astype(BF)
                cols = slice(h * CHUNK, (h + 1) * CHUNK)
                for c in range(tt // CHUNK):
                    rows = slice(c * CHUNK, (c + 1) * CHUNK)
                    vb = vn_scr[rows, cols]
                    mixed = _dot(wm, vb) + bst_ref[:, h:h + 1]
                    dy = dy_ref[rows, cols].astype(F32)
                    dmix = dy * u_scr[rows, cols]
                    u_scr[rows, cols] = dy * mixed
                    dbst_ref[:, h:h + 1] += jnp.sum(dmix, axis=1, keepdims=True)
                    dmb = dmix.astype(BF)
                    dws_ref[h] += _dot_nt(dmb, vb)
                    dvn_scr[rows, cols] = _dot(wmt, dmb)
            stage[0] = (u_scr[...] * du_dx).astype(BF)
            dvn = dvn_scr[...]
            dlg_ref[...] += jnp.sum(dvn * n, axis=0, keepdims=True)
            dlb_ref[...] += jnp.sum(dvn, axis=0, keepdims=True)
            stage[1] = (_ln_bwd(dvn * lg_ref[...], n, r) * dv_dx).astype(BF)

        dz_ref[...] = stage[b]

    vec = pl.BlockSpec((1, D), lambda i, b: (0, 0))
    wsp = pl.BlockSpec((HEADS, CHUNK, CHUNK), lambda i, b: (0, 0, 0))
    bsp = pl.BlockSpec((CHUNK, HEADS), lambda i, b: (0, 0))
    return pl.pallas_call(
        body, name="mix_s_bwd", grid=(ntt, 2),
        in_specs=[ANY, pl.BlockSpec((None, tt, D), lambda i, b: (2, i, 0)),
                  cur(5), cur(6), row(), row(), wsp, wsp, bsp],
        out_specs=[pl.BlockSpec((tt, D), lambda i, b: (i, 5 + b)), wsp, bsp, vec, vec],
        out_shape=[jax.ShapeDtypeStruct(dz.shape, BF), jax.ShapeDtypeStruct((HEADS, CHUNK, CHUNK), F32),
                   jax.ShapeDtypeStruct((CHUNK, HEADS), F32), jax.ShapeDtypeStruct((1, D), F32),
                   jax.ShapeDtypeStruct((1, D), F32)],
        scratch_shapes=[pltpu.VMEM((tt, D), F32), pltpu.VMEM((tt, D), BF), pltpu.VMEM((tt, D), F32),
                        pltpu.VMEM((2, tt, D), BF)],
        input_output_aliases={0: 0},
        compiler_params=_cp("arbitrary", "arbitrary"),
    )(dz, dy3, z, z, lg, lb, ws, wst, bst)


def _in_proj_bwd(dz, w, x, g, dx1, dep):
    T = x.shape[0]
    nc = w.shape[2]
    tm = min(T, 1024)
    tn = 1280
    nj = nc // tn
    ep = min(tm, 128)

    def body(dz_ref, w_ref, x_ref, g_ref, dx1_ref, dep_ref, dx_ref, dg_ref, acc):
        i, k, j = pl.program_id(0), pl.program_id(1), pl.program_id(2)

        @pl.when((i == 0) & (k == 0) & (j == 0))
        def _():
            dg_ref[...] = jnp.zeros_like(dg_ref)

        @pl.when((k == 0) & (j == 0))
        def _():
            acc[...] = jnp.zeros_like(acc)

        acc[...] += _dot_nt(dz_ref[...], w_ref[...])

        @pl.when((k == NSH - 1) & (j == nj - 1))
        def _():
            def step(c, dg):
                rows = pl.ds(pl.multiple_of(c * ep, ep), ep)
                dx, dgc = _rms_bwd(acc[rows, :], x_ref[rows, :], g_ref[...])
                dx_ref[rows, :] = dx1_ref[rows, :] + dx
                return dg + dgc
            dg_ref[...] += lax.fori_loop(0, tm // ep, step, jnp.zeros((1, D), F32))

    rowblk = pl.BlockSpec((tm, D), lambda i, k, j: (i, 0))
    vec = pl.BlockSpec((1, D), lambda i, k, j: (0, 0))
    return pl.pallas_call(
        body, name="in_proj_bwd", grid=(T // tm, NSH, nj),
        in_specs=[pl.BlockSpec((tm, tn), lambda i, k, j: (i, k * nj + j)),
                  pl.BlockSpec((None, D, tn), lambda i, k, j: (k, 0, j)), rowblk, vec, rowblk, ANY],
        out_specs=[rowblk, vec],
        out_shape=[jax.ShapeDtypeStruct((T, D), F32), jax.ShapeDtypeStruct((1, D), F32)],
        scratch_shapes=[pltpu.VMEM((tm, D), F32)],
        compiler_params=_cp("arbitrary", "arbitrary", "arbitrary"),
    )(dz, w, x, g, dx1, dep)


def _layer_fwd(x, p, S, dep, late):
    h, z = _in_proj(x, p["g_mix_pre"], p["w_in"], dep)
    ya = _mix_a_fwd(z, p["conv_a_w"], S)
    yc, s = _mix_b_fwd(z, p["conf_dw_w"], p["conf_dw_b"], p["conf_ln_g"], p["conf_ln_b"], S)
    ys = _mix_s_fwd(z, p["sgu_ln_g"], p["sgu_ln_b"], p["sgu_ws"], p["sgu_bt"], S)
    more, dep2 = late(ys)
    p.update(more)
    p3, merged, m, x1 = _mix_out_fwd(ya, yc, ys, z, x, p["w_branch"], p["w_out"], p["g_mix_post"], dep2)
    h2, a, f, x2 = _ffn_fwd(x1, p["g_ffn_pre"], p["w_ff1"], p["w_ff2"], p["g_ffn_post"])
    saved = dict(x=x, h=h, z=z, ya=ya, yc=yc, ys=ys, s=s, p3=p3, merged=merged, m=m, x1=x1, h2=h2, a=a, f=f)
    return x2, saved


def _layer_bwd(dx2, p, sv, S, dep, hooks):
    after_ffn, early, mid = hooks
    T = dx2.shape[0]
    bt = min(T, WGRAD_TILE)
    nt = T // bt
    rk = D // NSH
    df, da, dx1, dg_ffn_post, dg_ffn_pre = _ffn_bwd(dx2, sv["f"], p["g_ffn_post"], sv["a"], p["w_ff2"],
                                                    p["w_ff1"], sv["x1"], p["g_ffn_pre"], dep)
    dw_ff2 = _wgrad("wgrad_ff2", (sv["a"], df), (NSH, nt),
                    [pl.BlockSpec((bt, D), lambda k, t: (t, k)), pl.BlockSpec((bt, D), lambda k, t: (t, 0))],
                    pl.BlockSpec((None, D, D), lambda k, t: (k, 0, 0)),
                    jax.ShapeDtypeStruct((NSH, D, D), BF), (D, D), relu2=True)
    dw_ff1 = _wgrad("wgrad_ff1", (sv["h2"], da), (NSH, nt),
                    [pl.BlockSpec((bt, D), lambda k, t: (t, 0)), pl.BlockSpec((bt, D), lambda k, t: (t, k))],
                    pl.BlockSpec((None, D, D), lambda k, t: (k, 0, 0)),
                    jax.ShapeDtypeStruct((NSH, D, D), BF), (D, D))
    dm, dp3, dy3, dz, dg_mix_post = _mix_out_bwd(dx1, sv["m"], p["g_mix_post"], p["w_out"], sv["p3"], sv["z"],
                                                 p["w_branch"], after_ffn(dx1))
    dw_out = _wgrad("wgrad_out", (sv["merged"], dm), (nt,),
                    [pl.BlockSpec((bt, D), lambda t: (t, 0)), pl.BlockSpec((bt, D), lambda t: (t, 0))],
                    pl.BlockSpec((D, D), lambda t: (0, 0)),
                    jax.ShapeDtypeStruct((D, D), BF), (D, D)).reshape(NSH, rk, D)
    ysp = lambda n: pl.BlockSpec((bt, D), lambda b, t: (jnp.where(b == n, t, 0), 0))
    dw_br = _wgrad("wgrad_branch", (sv["ya"], sv["yc"], sv["ys"], dp3), (3, nt),
                   [ysp(0), ysp(1), ysp(2), pl.BlockSpec((None, bt, D), lambda b, t: (b, t, 0))],
                   pl.BlockSpec((NSH, None, rk, D), lambda b, t: (0, b, 0, 0)),
                   jax.ShapeDtypeStruct((NSH, 3, rk, D), BF), (D, D), pick=lambda: pl.program_id(0))
    dz, dwa = _mix_a_bwd(dz, dy3, sv["z"], p["conv_a_w"], S, early([dw_br, dw_out, dw_ff1, dw_ff2]))
    dz, dwc, dbc, dclg, dclb = _mix_b_bwd(dz, dy3, sv["s"], sv["z"], p["conf_dw_w"], p["conf_ln_g"],
                                          p["conf_ln_b"], S)
    dz, dws, dbst, dslg, dslb = _mix_s_bwd(dz, dy3, sv["z"], p["sgu_ln_g"], p["sgu_ln_b"], p["sgu_ws"],
                                           p["sgu_wst"], p["sgu_bt"], S)
    dx, dg_mix_pre = _in_proj_bwd(dz, p["w_in"], sv["x"], p["g_mix_pre"], dx1, mid(dz))
    tn = 1280
    nj = p["w_in"].shape[2] // tn
    dw_in = _wgrad("wgrad_in", (sv["h"], dz), (NSH, nj, nt),
                   [pl.BlockSpec((bt, D), lambda k, j, t: (t, 0)),
                    pl.BlockSpec((bt, tn), lambda k, j, t: (t, k * nj + j))],
                   pl.BlockSpec((None, D, tn), lambda k, j, t: (k, 0, j)),
                   jax.ShapeDtypeStruct(p["w_in"].shape, BF), (D, tn))
    tril = jnp.tril(jnp.ones((CHUNK, CHUNK), bool))
    small = dict(norm_mix_pre=dg_mix_pre, norm_mix_post=dg_mix_post, norm_ffn_pre=dg_ffn_pre,
                 norm_ffn_post=dg_ffn_post, conv_a_w=dwa, conf_dw_w=dwc, conf_dw_b=dbc, conf_ln_g=dclg,
                 conf_ln_b=dclb, sgu_ln_g=dslg, sgu_ln_b=dslb,
                 sgu_ws=jnp.where(tril[None], dws, 0.0), sgu_b=dbst.T)
    big = dict(w_in=dw_in, w_branch=dw_br, w_out=dw_out, w_ff1=dw_ff1, w_ff2=dw_ff2)
    return dx, big, small


SMALL_NAMES = ("norm_mix_pre", "norm_mix_post", "norm_ffn_pre", "norm_ffn_post", "conv_a_w", "conf_dw_w",
               "conf_dw_b", "conf_ln_g", "conf_ln_b", "sgu_ln_g", "sgu_ln_b", "sgu_b", "sgu_ws")
SMALL_ROWS = dict(norm_mix_pre=1, norm_mix_post=1, norm_ffn_pre=1, norm_ffn_post=1, conv_a_w=KA, conf_dw_w=KC,
                  conf_dw_b=1, conf_ln_g=1, conf_ln_b=1, sgu_ln_g=1, sgu_ln_b=1, sgu_b=1, sgu_ws=CHUNK)
def _pad8(r):
    return -(-r // SUBLANES) * SUBLANES


PACK_ROWS = sum(_pad8(r) for r in SMALL_ROWS.values())


def _pack_small(d):
    parts = []
    for n in SMALL_NAMES:
        r = SMALL_ROWS[n]
        parts.append(jnp.pad(d[n].reshape(r, D).astype(F32), ((0, _pad8(r) - r), (0, 0))))
    return jnp.concatenate(parts, axis=0)


def _unpack_small(a, shapes):
    out, r = {}, 0
    for n in SMALL_NAMES:
        out[n] = a[:, r:r + SMALL_ROWS[n]].reshape((a.shape[0],) + tuple(shapes[n]))
        r += _pad8(SMALL_ROWS[n])
    return out


def _me():
    return lax.axis_index("x"), lax.axis_index("y"), lax.axis_index("c")


def _slab(ref, q, a, h=None):
    r = ref.shape[1]
    rows = slice(None) if h is None else pl.ds(h * (r // 2), r // 2)
    return ref.at[pl.ds(q * a, a), rows, :]


def _rows(ref, h):
    r = ref.shape[-2]
    lead = (slice(None),) * (len(ref.shape) - 2)
    return ref.at[lead + (pl.ds(h * (r // 2), r // 2), slice(None))]


def _rcopy(src, dst, sems, idx, dev):
    return pltpu.make_async_remote_copy(src_ref=src, dst_ref=dst, send_sem=sems[0].at[idx], recv_sem=sems[1].at[idx],
                                        device_id=dev, device_id_type=MESH)


def _send_halves_to_sibling(parts):
    n = len(parts)

    def body(*refs):
        src, dst = refs[:n], refs[n:2 * n]
        sems = refs[2 * n:2 * n + 2]
        x, y, c = _me()
        cps = [_rcopy(_rows(src[i], 1 - c), dst[i], sems, i, (x, y, 1 - c)) for i in range(n)]
        for cp in cps:
            cp.start()
        for cp in cps:
            cp.wait()

    outs = [jax.ShapeDtypeStruct((p.shape[0], p.shape[1] // 2, p.shape[2]), p.dtype) for p in parts]
    return pl.pallas_call(
        body, name="pair_exchange", in_specs=[ANY] * n, out_specs=[ANY] * n, out_shape=outs,
        scratch_shapes=[pltpu.SemaphoreType.DMA((n,)), pltpu.SemaphoreType.DMA((n,))],
    )(*parts)


PAIR_BLOCK_BYTES = 3 * 512 * 1024


def _pair_add(parts, sibs, c):
    n = len(parts)
    steps = 1
    while any(p.shape[0] * (p.shape[1] // 2 // steps) * p.shape[2] * 2 > PAIR_BLOCK_BYTES for p in parts):
        steps *= 2

    def body(c_ref, *refs):
        for p_ref, s_ref, o_ref in zip(refs[:n], refs[n:2 * n], refs[2 * n:]):
            o_ref[...] = (p_ref[...].astype(F32) + s_ref[...].astype(F32)).astype(BF)

    def blk(p):
        return (p.shape[0], p.shape[1] // 2 // steps, p.shape[2])

    mine = [pl.BlockSpec(blk(p), lambda g, c_ref: (0, c_ref[0] * steps + g, 0)) for p in parts]
    same = [pl.BlockSpec(blk(p), lambda g, c_ref: (0, g, 0)) for p in parts]
    return pl.pallas_call(
        body, name="pair_add",
        grid_spec=pltpu.PrefetchScalarGridSpec(
            num_scalar_prefetch=1, grid=(steps,), in_specs=mine + same, out_specs=same),
        out_shape=[jax.ShapeDtypeStruct(s.shape, BF) for s in sibs],
        compiler_params=_cp("arbitrary"),
    )(c, *parts, *sibs)


def _other_chips(x, y):
    return [(1 - x, y), (x, 1 - y), (1 - x, 1 - y)]


def _split_call(name, copies, srcs, lands, sems=None, after=()):
    n, m = len(srcs), len(lands)
    hbm = lambda t: pltpu.HBM(t.shape, t.dtype)
    pin = lambda t: pltpu.with_memory_space_constraint(t, pltpu.HBM)
    thru = [hbm(t) for t in srcs] + [hbm(t) for t in lands]
    sem_spec = pl.BlockSpec(memory_space=pltpu.SEMAPHORE)
    effect = pltpu.CompilerParams(has_side_effects=pltpu.SideEffectType.DATAFLOW_SIDE_EFFECTING)
    if sems is None:
        def start_body(*refs):
            src, land = refs[:n], refs[n:n + m]
            ssem, rsem = refs[n + m + len(after)], refs[n + m + len(after) + 1]
            token = refs[-1]
            cps = copies(src, land, (ssem, rsem))
            for cp in cps:
                cp.start()
            token[...] = jnp.zeros_like(token)

        ncp = copies.count
        out = pl.pallas_call(
            start_body, name=name,
            out_shape=(pltpu.SemaphoreType.DMA((ncp,)), pltpu.SemaphoreType.DMA((ncp,)), *thru,
                       jax.ShapeDtypeStruct((8, 128), F32)),
            in_specs=[ANY] * (n + m + len(after)),
            out_specs=(sem_spec, sem_spec, *([ANY] * (n + m)), pl.BlockSpec(memory_space=pltpu.VMEM)),
            input_output_aliases={i: 2 + i for i in range(n + m)},
            compiler_params=effect,
        )(*[pin(t) for t in srcs], *[pin(t) for t in lands], *after)
        return out[0], out[1], list(out[2:2 + n]), list(out[2 + n:2 + n + m]), out[-1]

    def wait_body(*refs):
        src, land = refs[:n], refs[n:n + m]
        ssem, rsem = refs[n + m], refs[n + m + 1]
        for cp in copies(src, land, (ssem, rsem)):
            cp.wait_send()
            cp.wait_recv()

    out = pl.pallas_call(
        wait_body, name=name, out_shape=tuple(thru),
        in_specs=[ANY] * (n + m) + [sem_spec, sem_spec] + [ANY] * len(after),
        out_specs=tuple([ANY] * (n + m)),
        input_output_aliases={i: i for i in range(n + m)},
        compiler_params=effect,
    )(*srcs, *lands, sems[0], sems[1], *after)
    return list(out[:n]), list(out[n:])


def _cast_into(w, land, layer, kidx, dep):
    _, a, R, C = w.shape
    br = R
    while br * C > 512 * 1024 and br % 32 == 0:
        br //= 2

    def body(k_ref, w_ref, land_ref, dep_ref, o_ref):
        o_ref[...] = w_ref[...].astype(o_ref.dtype)

    return pl.pallas_call(
        body, name="cast_into",
        grid_spec=pltpu.PrefetchScalarGridSpec(
            num_scalar_prefetch=1, grid=(a, R // br),
            in_specs=[pl.BlockSpec((None, None, br, C), lambda e, i, k: (layer, e, i, 0)), ANY, ANY],
            out_specs=pl.BlockSpec((None, br, C), lambda e, i, k: (k[0] * a + e, i, 0))),
        out_shape=jax.ShapeDtypeStruct(land.shape, land.dtype), input_output_aliases={2: 0},
        compiler_params=_cp("arbitrary", "arbitrary"),
    )(kidx, w, land, dep)


class _GatherCopies:
    def __init__(self, n, halves=True):
        self.n, self.count, self.halves = n, 3 * n, halves

    def __call__(self, src, land, sems):
        x, y, c = _me()
        k = 2 * x + y
        cps = []
        for j, (qx, qy) in enumerate(_other_chips(x, y)):
            for i in range(self.n):
                mine = _slab(land[i], k, land[i].shape[0] // NSH, c if self.halves else None)
                cps.append(_rcopy(mine, mine, sems, j * self.n + i, (qx, qy, c)))
        return cps


def _gather_finish(lands):
    n = len(lands)

    def body(*refs):
        dst = refs[n:2 * n]
        sems = refs[2 * n:2 * n + 2]
        x, y, c = _me()
        av = [d.shape[0] // NSH for d in dst]
        cps = []
        for j, (qx, qy) in enumerate(_other_chips(x, y)):
            for i in range(n):
                got = _slab(dst[i], 2 * qx + qy, av[i], c)
                cps.append(_rcopy(got, got, sems, j * n + i, (x, y, 1 - c)))
        for cp in cps:
            cp.start()
        for j, (qx, qy) in enumerate(_other_chips(x, y)):
            for i in range(n):
                other = _slab(dst[i], 2 * qx + qy, av[i], 1 - c)
                _rcopy(other, other, sems, j * n + i, (x, y, c)).wait_recv()
        for cp in cps:
            cp.wait_send()

    return pl.pallas_call(
        body, name="gather_finish", in_specs=[ANY] * n, out_specs=[ANY] * n,
        out_shape=[jax.ShapeDtypeStruct(t.shape, t.dtype) for t in lands],
        input_output_aliases={i: i for i in range(n)},
        scratch_shapes=[pltpu.SemaphoreType.DMA((3 * n,)), pltpu.SemaphoreType.DMA((3 * n,))],
    )(*lands)


class _PairCopies:
    def __init__(self, n):
        self.n, self.count = n, n

    def __call__(self, src, land, sems):
        x, y, c = _me()
        return [_rcopy(_rows(src[i], 1 - c), land[i], sems, i, (x, y, 1 - c)) for i in range(self.n)]


class _ScatterCopies:
    def __init__(self, n):
        self.n, self.count = n, 3 * n

    def __call__(self, src, land, sems):
        x, y, c = _me()
        k = 2 * x + y
        cps = []
        for j, (qx, qy) in enumerate(_other_chips(x, y)):
            for i in range(self.n):
                a = src[i].shape[0] // NSH
                cps.append(_rcopy(_slab(src[i], 2 * qx + qy, a), _slab(land[i], k, a), sems, j * self.n + i,
                                  (qx, qy, c)))
        return cps


def _sum_chips(own, rcv, acc, layer, nlayers, idx):
    A, hr, C = rcv.shape
    a = A // NSH
    br = min(hr, 512)
    nb = hr // br

    def body(*refs):
        r0, r1, r2, r3 = refs[1:5]
        o_ref = refs[-1]
        o_ref[...] = ((r0[...].astype(F32) + r1[...].astype(F32)) + r2[...].astype(F32)) + r3[...].astype(F32)

    slot = lambda s: pl.BlockSpec((None, br, C), lambda e, i, ix: (ix[s] * a + e, i, 0))
    ops = [own, rcv, rcv, rcv]
    in_specs = [slot(0), slot(1), slot(2), slot(3)]
    aliases = {}
    if acc is not None:
        ops.append(acc)
        in_specs.append(ANY)
        aliases = {5: 0}
    return pl.pallas_call(
        body, name="sum_chips",
        grid_spec=pltpu.PrefetchScalarGridSpec(
            num_scalar_prefetch=1, grid=(a, nb), in_specs=in_specs,
            out_specs=pl.BlockSpec((None, None, br, C), lambda e, i, ix: (layer, e, ix[4] * nb + i, 0))),
        out_shape=jax.ShapeDtypeStruct((nlayers, a, 2 * hr, C), F32), input_output_aliases=aliases,
        compiler_params=_cp("arbitrary", "arbitrary"),
    )(idx, *ops)


def _join_halves(fulls):
    n = len(fulls)

    def body(*refs):
        buf = refs[n:2 * n]
        sems = refs[2 * n:2 * n + 2]
        x, y, c = _me()
        cps = [_rcopy(_rows(buf[i], c), _rows(buf[i], c), sems, i, (x, y, 1 - c)) for i in range(n)]
        for cp in cps:
            cp.start()
        for i in range(n):
            _rcopy(_rows(buf[i], 1 - c), _rows(buf[i], 1 - c), sems, i, (x, y, c)).wait_recv()
        for cp in cps:
            cp.wait_send()

    return pl.pallas_call(
        body, name="join_halves", in_specs=[ANY] * n, out_specs=[ANY] * n,
        out_shape=[jax.ShapeDtypeStruct(t.shape, t.dtype) for t in fulls],
        input_output_aliases={i: i for i in range(n)},
        scratch_shapes=[pltpu.SemaphoreType.DMA((n,)), pltpu.SemaphoreType.DMA((n,))],
    )(*fulls)


def _small_blocks(hr):
    br = hr
    while br > 512 and br % 16 == 0:
        br //= 2
    return br, hr // br


def _pair_sum_slot(part, sib, ck):
    R, C = part.shape
    hr = R // 2
    br, nb = _small_blocks(hr)

    def body(ix, p_ref, s_ref, o_ref):
        o_ref[...] = p_ref[...] + s_ref[...]

    return pl.pallas_call(
        body, name="pair_sum_slot",
        grid_spec=pltpu.PrefetchScalarGridSpec(
            num_scalar_prefetch=1, grid=(nb,),
            in_specs=[pl.BlockSpec((br, C), lambda i, ix: (ix[0] * nb + i, 0)),
                      pl.BlockSpec((br, C), lambda i, ix: (i, 0))],
            out_specs=pl.BlockSpec((None, br, C), lambda i, ix: (ix[1], i, 0))),
        out_shape=jax.ShapeDtypeStruct((NSH, hr, C), F32),
        compiler_params=_cp("arbitrary"),
    )(ck, part, sib)


def _sum_slots(slots, ck):
    _, hr, C = slots.shape
    br, nb = _small_blocks(hr)

    def body(ix, s_ref, o_ref):
        o_ref[...] = ((s_ref[0] + s_ref[1]) + s_ref[2]) + s_ref[3]

    return pl.pallas_call(
        body, name="sum_slots",
        grid_spec=pltpu.PrefetchScalarGridSpec(
            num_scalar_prefetch=1, grid=(nb,),
            in_specs=[pl.BlockSpec((NSH, br, C), lambda i, ix: (0, i, 0))],
            out_specs=pl.BlockSpec((br, C), lambda i, ix: (ix[0] * nb + i, 0))),
        out_shape=jax.ShapeDtypeStruct((2 * hr, C), F32),
        compiler_params=_cp("arbitrary"),
    )(ck, slots)


def _adamw(w, g, m, v):
    shape = w.shape
    C = shape[-1]
    R = shape[-2]
    A = 1
    for s in shape[:-2]:
        A *= s
    br = R
    while br * C > 256 * 1024 and br % 16 == 0:
        br //= 2
    c1 = 1.0 / (1.0 - ADAM_B1 ** ADAM_STEP)
    c2 = 1.0 / (1.0 - ADAM_B2 ** ADAM_STEP)

    def body(w_ref, g_ref, m_ref, v_ref, og_ref, d_ref, nm_ref, nv_ref):
        gv = g_ref[...]
        og_ref[...] = gv
        nm = ADAM_B1 * m_ref[...] + (1.0 - ADAM_B1) * gv
        nv = ADAM_B2 * v_ref[...] + (1.0 - ADAM_B2) * (gv * gv)
        nm_ref[...] = nm
        nv_ref[...] = nv
        d_ref[...] = -ADAM_LR * ((nm * c1) / (jnp.sqrt(nv * c2) + ADAM_EPS) + ADAM_WD * w_ref[...])

    blk = pl.BlockSpec((None, br, C), lambda a, i: (a, i, 0))
    outs = pl.pallas_call(
        body, name="adamw", grid=(A, R // br), in_specs=[blk] * 4, out_specs=[blk] * 4,
        out_shape=[jax.ShapeDtypeStruct((A, R, C), F32)] * 4,
        compiler_params=_cp("arbitrary", "arbitrary"),
    )(*(t.reshape(A, R, C) for t in (w, g, m, v)))
    return tuple(o.reshape(shape) for o in outs)


WEIGHTS = ("norm_mix_pre", "norm_mix_post", "norm_ffn_pre", "norm_ffn_post", "w_in", "conv_a_w", "conf_dw_w",
           "conf_dw_b", "conf_ln_g", "conf_ln_b", "sgu_ln_g", "sgu_ln_b", "sgu_ws", "sgu_b", "w_branch", "w_out",
           "w_ff1", "w_ff2")
BIG = ("w_in", "w_branch", "w_out", "w_ff1", "w_ff2")
CONV_ROWS = 48


def kernel(x, norm_mix_pre, norm_mix_post, norm_ffn_pre, norm_ffn_post, w_in, conv_a_w, conf_dw_w, conf_dw_b, conf_ln_g, conf_ln_b, sgu_ln_g, sgu_ln_b, sgu_ws, sgu_b, w_branch, w_out, w_ff1, w_ff2, loss_target, m_norm_mix_pre, m_norm_mix_post, m_norm_ffn_pre, m_norm_ffn_post, m_w_in, m_conv_a_w, m_conf_dw_w, m_conf_dw_b, m_conf_ln_g, m_conf_ln_b, m_sgu_ln_g, m_sgu_ln_b, m_sgu_ws, m_sgu_b, m_w_branch, m_w_out, m_w_ff1, m_w_ff2, v_norm_mix_pre, v_norm_mix_post, v_norm_ffn_pre, v_norm_ffn_post, v_w_in, v_conv_a_w, v_conf_dw_w, v_conf_dw_b, v_conf_ln_g, v_conf_ln_b, v_sgu_ln_g, v_sgu_ln_b, v_sgu_ws, v_sgu_b, v_w_branch, v_w_out, v_w_ff1, v_w_ff2):
    w = dict(norm_mix_pre=norm_mix_pre, norm_mix_post=norm_mix_post, norm_ffn_pre=norm_ffn_pre,
             norm_ffn_post=norm_ffn_post, w_in=w_in, conv_a_w=conv_a_w, conf_dw_w=conf_dw_w, conf_dw_b=conf_dw_b,
             conf_ln_g=conf_ln_g, conf_ln_b=conf_ln_b, sgu_ln_g=sgu_ln_g, sgu_ln_b=sgu_ln_b, sgu_ws=sgu_ws,
             sgu_b=sgu_b, w_branch=w_branch, w_out=w_out, w_ff1=w_ff1, w_ff2=w_ff2)
    mom = dict(norm_mix_pre=m_norm_mix_pre, norm_mix_post=m_norm_mix_post, norm_ffn_pre=m_norm_ffn_pre,
               norm_ffn_post=m_norm_ffn_post, w_in=m_w_in, conv_a_w=m_conv_a_w, conf_dw_w=m_conf_dw_w,
               conf_dw_b=m_conf_dw_b, conf_ln_g=m_conf_ln_g, conf_ln_b=m_conf_ln_b, sgu_ln_g=m_sgu_ln_g,
               sgu_ln_b=m_sgu_ln_b, sgu_ws=m_sgu_ws, sgu_b=m_sgu_b, w_branch=m_w_branch, w_out=m_w_out,
               w_ff1=m_w_ff1, w_ff2=m_w_ff2)
    var = dict(norm_mix_pre=v_norm_mix_pre, norm_mix_post=v_norm_mix_post, norm_ffn_pre=v_norm_ffn_pre,
               norm_ffn_post=v_norm_ffn_post, w_in=v_w_in, conv_a_w=v_conv_a_w, conf_dw_w=v_conf_dw_w,
               conf_dw_b=v_conf_dw_b, conf_ln_g=v_conf_ln_g, conf_ln_b=v_conf_ln_b, sgu_ln_g=v_sgu_ln_g,
               sgu_ln_b=v_sgu_ln_b, sgu_ws=v_sgu_ws, sgu_b=v_sgu_b, w_branch=v_w_branch, w_out=v_w_out,
               w_ff1=v_w_ff1, w_ff2=v_w_ff2)
    L = w_in.shape[0]
    nseq, S, _ = x.shape
    T = nseq * S
    rk = D // NSH
    mx, my, mc = _me()
    k_chip = 2 * mx + my

    big_src = [w_in.reshape(L, 1, D, w_in.shape[2]), w_branch, w_out.reshape(L, 1, rk, D),
               w_ff1.reshape(L, 1, D, w_ff1.shape[2]), w_ff2.reshape(L, 1, w_ff2.shape[1], D)]
    kidx = jnp.reshape(k_chip, (1,)).astype(jnp.int32)
    conv_src = jnp.concatenate(
        [jnp.pad(conv_a_w, ((0, 0), (0, SUBLANES - KA), (0, 0))), jnp.pad(conf_dw_w, ((0, 0), (0, 1), (0, 0))),
         jnp.zeros((L, CONV_ROWS - SUBLANES - KC - 1, rk), F32)], axis=1)[None]

    def early_params(l, g_in, conv_full):
        return dict(
            g_mix_pre=norm_mix_pre[l][None], g_mix_post=norm_mix_post[l][None], g_ffn_pre=norm_ffn_pre[l][None],
            g_ffn_post=norm_ffn_post[l][None], w_in=g_in, conv_a_w=conv_full[l, :KA],
            conf_dw_w=conv_full[l, SUBLANES:SUBLANES + KC], conf_dw_b=conf_dw_b[l][None],
            conf_ln_g=conf_ln_g[l][None], conf_ln_b=conf_ln_b[l][None], sgu_ln_g=sgu_ln_g[l][None],
            sgu_ln_b=sgu_ln_b[l][None], sgu_ws=sgu_ws[l], sgu_wst=jnp.swapaxes(sgu_ws[l], 1, 2),
            sgu_bt=sgu_b[l].T)

    def late_params(gathered):
        g_br, g_out, g_ff1, g_ff2 = gathered
        return dict(w_branch=g_br.reshape(NSH, 3, rk, D), w_out=g_out.reshape(D, D), w_ff1=g_ff1,
                    w_ff2=g_ff2.reshape(NSH * w_ff2.shape[1], D))

    def cast_lands(srcs, l, dep):
        return [_cast_into(s, lax.empty((NSH * s.shape[1],) + s.shape[2:], F32 if s is conv_src else BF), l, kidx,
                           dep) for s in srcs]

    def gather_start(name, lands, after):
        return _split_call(name, _GatherCopies(len(lands)), [], lands, after=after)

    def gather_land(name, flight, after):
        ssem, rsem, _, lands, _ = flight
        _, lands = _split_call(name, _GatherCopies(len(lands)), [], lands, (ssem, rsem), after)
        return _gather_finish(lands)

    zero_tok = jnp.zeros((8, 128), F32)
    xt = x.reshape(T, D)
    layers, saved = [], []
    head = gather_start("gather_start_0a", cast_lands([big_src[0], conv_src], 0, kidx), [])
    tails = [cast_lands(big_src[1:], l, head[4]) for l in range(L)]
    heads = [None] + [cast_lands(big_src[:1], l, head[4]) for l in range(1, L)]
    behind = [xt] + [t for ls in tails + heads[1:] for t in ls]
    conv_full = None
    for l in range(L):
        got = gather_land(f"gather_wait_{l}a", head, behind if l == 0 else [xt])
        g_in = got[0]
        if l == 0:
            conv_full = got[1].reshape(NSH, L, CONV_ROWS, rk).transpose(1, 2, 0, 3).reshape(L, CONV_ROWS, D)
        tail = gather_start(f"gather_start_{l}b", tails[l], [g_in])
        nxt = {}

        def late(after, l=l, tail=tail, nxt=nxt):
            more = late_params(gather_land(f"gather_wait_{l}b", tail, [after]))
            if l + 1 == L:
                return more, zero_tok
            nxt["head"] = gather_start(f"gather_start_{l + 1}a", heads[l + 1], [more["w_ff1"]])
            return more, nxt["head"][4]

        p = early_params(l, g_in, conv_full)
        xt, sv = _layer_fwd(xt, p, S, tail[4], late)
        head = nxt.get("head")
        layers.append(p)
        saved.append(sv)
    dx, loss_row = _loss_head(xt, loss_target.reshape(T, D))
    loss = lax.psum(loss_row[0, 0], ("x", "y", "c"))

    c_arr = jnp.reshape(mc, (1,)).astype(jnp.int32)
    idx = jnp.stack([k_chip, k_chip ^ 2, k_chip ^ 1, k_chip ^ 3, mc]).astype(jnp.int32)
    fulls = {n: None for n in BIG}
    smalls = [None] * L

    def pair_start(tag, parts):
        lands = [lax.empty((p.shape[0], p.shape[1] // 2, p.shape[2]), p.dtype) for p in parts]
        return _split_call(f"pair_start_{tag}", _PairCopies(len(parts)), parts, lands)

    def pair_land_scatter_start(tag, fl, after):
        ssem, rsem, parts, sib, _ = fl
        parts, sib = _split_call(f"pair_wait_{tag}", _PairCopies(len(parts)), parts, sib, (ssem, rsem), after)
        sums = _pair_add(parts, sib, c_arr)
        rcv = [lax.empty(s.shape, s.dtype) for s in sums]
        return _split_call(f"scatter_start_{tag}", _ScatterCopies(len(sums)), sums, rcv)

    def scatter_land(tag, fl, names, l, after):
        ssem, rsem, sums, rcv, _ = fl
        sums, rcv = _split_call(f"scatter_wait_{tag}", _ScatterCopies(len(sums)), sums, rcv, (ssem, rsem), after)
        for n, o, r in zip(names, sums, rcv):
            fulls[n] = _sum_chips(o, r, fulls[n], l, L, idx)

    pending = []
    pair_b = None
    dep = zero_tok
    for l in reversed(range(L)):
        mine = {}

        def after_ffn(arr, l=l, mine=mine, pair_b=pair_b):
            if pair_b is None:
                return zero_tok
            mine["prev_b"] = pair_land_scatter_start(f"{l + 1}b", pair_b, [arr])
            return mine["prev_b"][4]

        def early(parts, l=l, mine=mine):
            br, rest = parts[0], parts[1:]
            mine["pair_a"] = pair_start(f"{l}a", [br.reshape(NSH * 3, rk, D), *rest])
            return mine["pair_a"][4]

        def mid(arr, l=l, mine=mine):
            mine["a"] = pair_land_scatter_start(f"{l}a", mine["pair_a"], [arr])
            return mine["a"][4]

        dx, big, small = _layer_bwd(dx, layers[l], saved[l], S, dep, (after_ffn, early, mid))
        smalls[l] = _pack_small(small)
        for args in pending:
            scatter_land(*args, [dx])
        pending = [(f"{l}a", mine["a"], BIG[1:], l)]
        if "prev_b" in mine:
            pending.append((f"{l + 1}b", mine["prev_b"], BIG[:1], l + 1))
        pair_b = pair_start(f"{l}b", [big["w_in"]])
        dep = pair_b[4]
    last_b = ("0b", pair_land_scatter_start("0b", pair_b, [dx]), BIG[:1], 0)

    packed = jnp.concatenate(smalls, axis=0)
    nrow = packed.shape[0]
    ck = jnp.stack([mc, k_chip]).astype(jnp.int32)
    (sib,) = _send_halves_to_sibling([packed.reshape(1, nrow, D)])
    slots = _pair_sum_slot(packed, sib.reshape(nrow // 2, D), ck)
    small_flight = _split_call("small_start", _GatherCopies(1, halves=False), [], [slots])

    for args in pending:
        scatter_land(*args, [small_flight[4]])
    grads, delta, new_m, new_v = {}, {}, {}, {}
    for n, f in zip(BIG[1:], _join_halves([fulls[n] for n in BIG[1:]])):
        grads[n], delta[n], new_m[n], new_v[n] = _adamw(w[n], f.reshape(w[n].shape), mom[n], var[n])
    scatter_land(*last_b, [delta[BIG[-1]]])
    (f,) = _join_halves([fulls[BIG[0]]])
    n = BIG[0]
    grads[n], delta[n], new_m[n], new_v[n] = _adamw(w[n], f.reshape(w[n].shape), mom[n], var[n])

    _, (slots,) = _split_call("small_wait", _GatherCopies(1, halves=False), [], small_flight[3],
                              (small_flight[0], small_flight[1]), [delta[BIG[0]]])
    (small_sum,) = _join_halves([_sum_slots(slots, ck).reshape(1, 1, nrow, D)])
    shapes = {n: (w[n].shape[1:] if n not in ("conv_a_w", "conf_dw_w") else (w[n].shape[1], D)) for n in SMALL_NAMES}
    sg = _unpack_small(small_sum.reshape(L, PACK_ROWS, D), shapes)
    for n in SMALL_NAMES:
        if n in ("conv_a_w", "conf_dw_w"):
            grads[n] = lax.dynamic_slice_in_dim(sg[n], k_chip * rk, rk, axis=2)
        else:
            grads[n] = sg[n]

    for n in SMALL_NAMES:
        sh = w[n].shape
        flat = (sh[0] * sh[1], sh[2]) if n in ("conv_a_w", "conf_dw_w") else (-1, D)
        g, d, nm, nv = _adamw(*(t.reshape(flat) for t in (w[n], grads[n], mom[n], var[n])))
        grads[n], delta[n], new_m[n], new_v[n] = g.reshape(sh), d.reshape(sh), nm.reshape(sh), nv.reshape(sh)

    return (loss, dx.reshape(x.shape), *[grads[n] for n in WEIGHTS], *[delta[n] for n in WEIGHTS],
            *[new_m[n] for n in WEIGHTS], *[new_v[n] for n in WEIGHTS])
```

```python
import functools

import jax
import jax.numpy as jnp
from jax import lax
from jax.experimental import pallas as pl
from jax.experimental.pallas import tpu as pltpu

D = 1024
HEADS = 8
CHUNK = 128
KA = 3
KC = 31
HALO = 32
SUBLANES = 8
MIX_TILE = 512
WGRAD_TILE = 1024
NSH = 4
NDEV = 8
EPS = 1e-6
BF = jnp.bfloat16
F32 = jnp.float32
VMEM_LIMIT = 56 * 1024 * 1024

ADAM_LR = 0.001
ADAM_B1 = 0.9
ADAM_B2 = 0.999
ADAM_EPS = 1e-08
ADAM_WD = 0.01
ADAM_STEP = 10

MESH = pl.DeviceIdType.MESH
ANY = pl.BlockSpec(memory_space=pl.ANY)


def _cp(*sem):
    return pltpu.CompilerParams(dimension_semantics=sem, vmem_limit_bytes=VMEM_LIMIT)


def _sig(x):
    return 1.0 / (1.0 + jnp.exp(-x))


_GC = 0.7978845608028654


def _gelu(x):
    x2 = x * x
    t = jnp.tanh(_GC * x * (1.0 + 0.044715 * x2))
    y = 0.5 * x * (1.0 + t)
    dy = 0.5 * (1.0 + t) + 0.5 * x * (1.0 - t * t) * _GC * (1.0 + 3.0 * 0.044715 * x2)
    return y, dy


def _rms_fwd(x, g):
    r = lax.rsqrt(jnp.mean(x * x, axis=-1, keepdims=True) + EPS)
    return x * r * g


def _rms_bwd(dy, x, g):
    r = lax.rsqrt(jnp.mean(x * x, axis=-1, keepdims=True) + EPS)
    xn = x * r
    dyg = dy * g
    dx = r * (dyg - xn * jnp.mean(dyg * xn, axis=-1, keepdims=True))
    return dx, jnp.sum(dy * xn, axis=0, keepdims=True)


def _ln_stats(x):
    mu = jnp.mean(x, axis=-1, keepdims=True)
    xc = x - mu
    r = lax.rsqrt(jnp.mean(xc * xc, axis=-1, keepdims=True) + EPS)
    return xc * r, r


def _ln_bwd(dn, n, r):
    return r * (dn - jnp.mean(dn, axis=-1, keepdims=True) - n * jnp.mean(dn * n, axis=-1, keepdims=True))


def _dot(a, b):
    return jnp.dot(a, b, preferred_element_type=F32)


def _dot_nt(a, b):
    return lax.dot_general(a, b, (((1,), (1,)), ((), ())), preferred_element_type=F32)


def _dot_tn(a, b):
    return lax.dot_general(a, b, (((0,), (0,)), ((), ())), preferred_element_type=F32)


def _in_proj(x, g, w, dep):
    T = x.shape[0]
    nc = w.shape[2]
    tm = min(T, 1024)
    tn = nc
    nj = nc // tn

    def body(x_ref, g_ref, w_ref, dep_ref, h_ref, z_ref, h_scr):
        @pl.when((pl.program_id(1) == 0) & (pl.program_id(2) == 0))
        def _():
            h = _rms_fwd(x_ref[...], g_ref[...]).astype(BF)
            h_scr[...] = h
            h_ref[...] = h
        z_ref[...] = _dot(h_scr[...], w_ref[...]).astype(BF)

    return pl.pallas_call(
        body, name="in_proj", grid=(T // tm, NSH, nj),
        in_specs=[pl.BlockSpec((tm, D), lambda i, k, j: (i, 0)),
                  pl.BlockSpec((1, D), lambda i, k, j: (0, 0)),
                  pl.BlockSpec((None, D, tn), lambda i, k, j: (k, 0, j)), ANY],
        out_specs=[pl.BlockSpec((tm, D), lambda i, k, j: (i, 0)),
                   pl.BlockSpec((tm, tn), lambda i, k, j: (i, k * nj + j))],
        out_shape=[jax.ShapeDtypeStruct((T, D), BF), jax.ShapeDtypeStruct((T, NSH * nc), BF)],
        scratch_shapes=[pltpu.VMEM((tm, D), BF)],
        compiler_params=_cp("arbitrary", "arbitrary", "arbitrary"),
    )(x, g, w, dep)


def _tile_specs(tt, nt_total, reverse):
    def tile(i):
        return (nt_total - 1 - i) if reverse else i

    def cur(c):
        return pl.BlockSpec((tt, D), lambda i, *_: (tile(i), c))

    def halo(c):
        return pl.BlockSpec((HALO, D), lambda i, *_: (jnp.maximum(tile(i) * (tt // HALO) - 1, 0), c))

    def row(r=1):
        return pl.BlockSpec((r, D), lambda i, *_: (0, 0))

    return tile, cur, halo, row


RC = 16


def _chunks(tt, fn, group=2):
    def step(c, carry):
        for u in range(group):
            fn(pl.multiple_of((c * group + u) * RC, RC))
        return carry
    lax.fori_loop(0, tt // (RC * group), step, 0)


def _chunk_pairs(tt, fn):
    def step(c, carry):
        fn(pl.multiple_of(c * 2 * RC, RC), pl.multiple_of(c * 2 * RC + RC, RC))
        return carry
    lax.fori_loop(0, tt // (2 * RC), step, 0)


ALL_SHIFTS = tuple(range(SUBLANES))


def _shifts_of(offs):
    return tuple(sorted({o % SUBLANES for o in offs}))


def _shifted_copies(ext, sh, nrows, shifts=ALL_SHIFTS):
    for i, s in enumerate(shifts):
        sh[i] = ext[pl.ds(s, nrows), :]


def _window(sh, o, r0, shifts=ALL_SHIFTS):
    return sh[shifts.index(o % SUBLANES), pl.ds(r0 + (o // SUBLANES) * SUBLANES, RC), :]


def _fill_taps(wb, w_ref, ntap):
    for k in range(ntap):
        wb[k * SUBLANES:(k + 1) * SUBLANES, :] = jnp.broadcast_to(w_ref[k:k + 1, :], (SUBLANES, D))


def _conv_chunks(sh, wb, offs, r0s, shifts=ALL_SHIFTS):
    accs = []
    for r0 in r0s:
        acc = None
        for k, o in enumerate(offs):
            wk = wb[k * SUBLANES:(k + 1) * SUBLANES, :]
            term = jnp.concatenate([wk] * (RC // SUBLANES), axis=0) * _window(sh, o, r0, shifts)
            acc = term if acc is None else acc + term
        accs.append(acc)
    return accs


WG_TAPS = 5


def _conv_wgrad_chunked(dw_ref, d_ref, sh, offs, tt, shifts=ALL_SHIFTS):
    for g0 in range(0, len(offs), WG_TAPS):
        grp = offs[g0:g0 + WG_TAPS]

        def step(c, accs, grp=grp):
            for u in range(2):
                r0 = pl.multiple_of((2 * c + u) * SUBLANES, SUBLANES)
                d = d_ref[pl.ds(r0, SUBLANES), :]
                accs = tuple(
                    a + d * sh[shifts.index(o % SUBLANES), pl.ds(r0 + (o // SUBLANES) * SUBLANES, SUBLANES), :]
                    for a, o in zip(accs, grp))
            return accs
        accs = lax.fori_loop(0, tt // (2 * SUBLANES), step,
                             tuple(jnp.zeros((SUBLANES, D), F32) for _ in grp))
        for j, a in enumerate(accs):
            dw_ref[g0 + j:g0 + j + 1, :] += jnp.sum(a, axis=0, keepdims=True)


def _causal_offsets(ntap):
    return [HALO - (ntap - 1) + k for k in range(ntap)]


def _anticausal_offsets(ntap):
    return [ntap - 1 - k for k in range(ntap)]


def _mix_a_fwd(z, wa, S):
    T = z.shape[0]
    tt = min(S, MIX_TILE)
    nt = S // tt
    _, cur, halo, row = _tile_specs(tt, T // tt, False)

    nrows = HALO + tt
    offs = _causal_offsets(KA)
    shifts = _shifts_of(offs)

    def body(ah, ab, ac, ah_h, ac_h, w_ref, y_ref, ext, sh, wb):
        @pl.when(pl.program_id(0) == 0)
        def _():
            _fill_taps(wb, w_ref, KA)
            ext[nrows:, :] = jnp.zeros((SUBLANES, D), F32)

        first = (pl.program_id(0) % nt) == 0
        ph = ah_h[...].astype(F32) * ac_h[...].astype(F32)
        ext[0:HALO, :] = jnp.where(first, 0.0, ph)

        def prod(r0):
            rows = pl.ds(r0, RC)
            ext[pl.ds(HALO + r0, RC), :] = ah[rows, :].astype(F32) * ac[rows, :].astype(F32)
        _chunks(tt, prod)
        _shifted_copies(ext, sh, nrows, shifts)

        def conv(*r0s):
            for r0, q in zip(r0s, _conv_chunks(sh, wb, offs, r0s, shifts)):
                rows = pl.ds(r0, RC)
                y_ref[rows, :] = (ab[rows, :].astype(F32) * q).astype(BF)
        _chunk_pairs(tt, conv)

    return pl.pallas_call(
        body, name="mix_a_fwd", grid=(T // tt,),
        in_specs=[cur(0), cur(1), cur(2), halo(0), halo(2), row(KA)],
        out_specs=pl.BlockSpec((tt, D), lambda i: (i, 0)),
        out_shape=jax.ShapeDtypeStruct((T, D), BF),
        scratch_shapes=[pltpu.VMEM((nrows + SUBLANES, D), F32), pltpu.VMEM((len(shifts), nrows, D), F32),
                        pltpu.VMEM((KA * SUBLANES, D), F32)],
        compiler_params=_cp("arbitrary"),
    )(z, z, z, z, z, wa)


def _mix_b_fwd(z, wc, bc, lg, lb, S):
    T = z.shape[0]
    tt = min(S, MIX_TILE)
    nt = S // tt
    _, cur, halo, row = _tile_specs(tt, T // tt, False)

    nrows = HALO + tt
    offs = _causal_offsets(KC)

    def body(ca, cg, ca_h, cg_h, w_ref, bc_ref, lg_ref, lb_ref, y_ref, s_ref, ext, sh, wb):
        @pl.when(pl.program_id(0) == 0)
        def _():
            _fill_taps(wb, w_ref, KC)
            ext[nrows:, :] = jnp.zeros((SUBLANES, D), F32)

        first = (pl.program_id(0) % nt) == 0
        rh = ca_h[...].astype(F32) * _sig(cg_h[...].astype(F32))
        ext[0:HALO, :] = jnp.where(first, 0.0, rh)

        def glu(r0):
            rows = pl.ds(r0, RC)
            ext[pl.ds(HALO + r0, RC), :] = ca[rows, :].astype(F32) * _sig(cg[rows, :].astype(F32))
        _chunks(tt, glu)
        _shifted_copies(ext, sh, nrows)

        def conv(*r0s):
            for r0, q in zip(r0s, _conv_chunks(sh, wb, offs, r0s)):
                rows = pl.ds(r0, RC)
                s = q + bc_ref[...]
                s_ref[rows, :] = s.astype(BF)
                n, _ = _ln_stats(s)
                t = n * lg_ref[...] + lb_ref[...]
                y_ref[rows, :] = (t * _sig(t)).astype(BF)
        _chunk_pairs(tt, conv)

    return pl.pallas_call(
        body, name="mix_b_fwd", grid=(T // tt,),
        in_specs=[cur(3), cur(4), halo(3), halo(4), row(KC), row(), row(), row()],
        out_specs=[pl.BlockSpec((tt, D), lambda i: (i, 0))] * 2,
        out_shape=[jax.ShapeDtypeStruct((T, D), BF)] * 2,
        scratch_shapes=[pltpu.VMEM((nrows + SUBLANES, D), F32), pltpu.VMEM((SUBLANES, nrows, D), F32),
                        pltpu.VMEM((KC * SUBLANES, D), F32)],
        compiler_params=_cp("arbitrary"),
    )(z, z, z, z, wc, bc, lg, lb)


def _causal_mask(transposed):
    r = lax.broadcasted_iota(jnp.int32, (CHUNK, CHUNK), 0)
    c = lax.broadcasted_iota(jnp.int32, (CHUNK, CHUNK), 1)
    return (c >= r) if transposed else (r >= c)


def _mix_s_fwd(z, lg, lb, ws, bst, S):
    T = z.shape[0]
    tt = min(S, MIX_TILE)
    _, cur, _, row = _tile_specs(tt, T // tt, False)

    def body(su, sv, lg_ref, lb_ref, ws_ref, bst_ref, y_ref, u_scr, vn_scr):
        u_scr[...] = _gelu(su[...].astype(F32))[0]
        n, _ = _ln_stats(_gelu(sv[...].astype(F32))[0])
        vn_scr[...] = (n * lg_ref[...] + lb_ref[...]).astype(BF)
        mask = _causal_mask(False)
        for h in range(HEADS):
            wm = jnp.where(mask, ws_ref[h], 0.0).astype(BF)
            cols = slice(h * CHUNK, (h + 1) * CHUNK)
            for c in range(tt // CHUNK):
                rows = slice(c * CHUNK, (c + 1) * CHUNK)
                mixed = _dot(wm, vn_scr[rows, cols]) + bst_ref[:, h:h + 1]
                y_ref[rows, cols] = (u_scr[rows, cols] * mixed).astype(BF)

    return pl.pallas_call(
        body, name="mix_s_fwd", grid=(T // tt,),
        in_specs=[cur(5), cur(6), row(), row(),
                  pl.BlockSpec((HEADS, CHUNK, CHUNK), lambda i: (0, 0, 0)),
                  pl.BlockSpec((CHUNK, HEADS), lambda i: (0, 0))],
        out_specs=pl.BlockSpec((tt, D), lambda i: (i, 0)),
        out_shape=jax.ShapeDtypeStruct((T, D), BF),
        scratch_shapes=[pltpu.VMEM((tt, D), F32), pltpu.VMEM((tt, D), BF)],
        compiler_params=_cp("arbitrary"),
    )(z, z, lg, lb, ws, bst)


def _mix_out_fwd(ya, yc, ys, z, x, wb, wo, gp, dep):
    T = x.shape[0]
    tm = min(T, 256)
    rk = D // NSH

    def body(ya_ref, yc_ref, ys_ref, ga, gc, gs, x_ref, wb_ref, wo_ref, gp_ref, dep_ref,
             p_ref, mg_ref, m_ref, x1_ref):
        acc = None
        for b, (y_ref, g_ref) in enumerate(((ya_ref, ga), (yc_ref, gc), (ys_ref, gs))):
            pb = None
            for k in range(NSH):
                part = _dot(y_ref[:, k * rk:(k + 1) * rk], wb_ref[k, b])
                pb = part if pb is None else pb + part
            p_ref[b] = pb.astype(BF)
            term = _sig(g_ref[...].astype(F32)) * pb
            acc = term if acc is None else acc + term
        mg = acc.astype(BF)
        mg_ref[...] = mg
        m = _dot(mg, wo_ref[...])
        m_ref[...] = m.astype(BF)
        x1_ref[...] = x_ref[...] + _rms_fwd(m, gp_ref[...])

    rowblk = pl.BlockSpec((tm, D), lambda i: (i, 0))
    return pl.pallas_call(
        body, name="mix_out_fwd", grid=(T // tm,),
        in_specs=[rowblk, rowblk, rowblk,
                  pl.BlockSpec((tm, D), lambda i: (i, 7)), pl.BlockSpec((tm, D), lambda i: (i, 8)),
                  pl.BlockSpec((tm, D), lambda i: (i, 9)), rowblk,
                  pl.BlockSpec((NSH, 3, rk, D), lambda i: (0, 0, 0, 0)),
                  pl.BlockSpec((D, D), lambda i: (0, 0)),
                  pl.BlockSpec((1, D), lambda i: (0, 0)), ANY],
        out_specs=[pl.BlockSpec((3, tm, D), lambda i: (0, i, 0)), rowblk, rowblk, rowblk],
        out_shape=[jax.ShapeDtypeStruct((3, T, D), BF), jax.ShapeDtypeStruct((T, D), BF),
                   jax.ShapeDtypeStruct((T, D), BF), jax.ShapeDtypeStruct((T, D), F32)],
        compiler_params=_cp("arbitrary"),
    )(ya, yc, ys, z, z, z, x, wb, wo, gp, dep)


def _ffn_fwd(x1, g3, w1, w2, g4):
    T = x1.shape[0]
    tm = min(T, 512)

    def body(x_ref, g3_ref, w1_ref, w2_ref, g4_ref, h_ref, a_ref, f_ref, x2_ref, h_scr, acc):
        k = pl.program_id(1)

        @pl.when(k == 0)
        def _():
            h = _rms_fwd(x_ref[...], g3_ref[...]).astype(BF)
            h_scr[...] = h
            h_ref[...] = h
            acc[...] = jnp.zeros_like(acc)

        a = _dot(h_scr[...], w1_ref[...])
        a_ref[...] = a.astype(BF)
        r = jnp.maximum(a, 0.0)
        acc[...] += _dot((r * r).astype(BF), w2_ref[...])

        @pl.when(k == NSH - 1)
        def _():
            f = acc[...]
            f_ref[...] = f.astype(BF)
            x2_ref[...] = x_ref[...] + _rms_fwd(f, g4_ref[...])

    rowblk = pl.BlockSpec((tm, D), lambda i, k: (i, 0))
    vec = pl.BlockSpec((1, D), lambda i, k: (0, 0))
    return pl.pallas_call(
        body, name="ffn_fwd", grid=(T // tm, NSH),
        in_specs=[rowblk, vec, pl.BlockSpec((None, D, D), lambda i, k: (k, 0, 0)),
                  pl.BlockSpec((D, D), lambda i, k: (k, 0)), vec],
        out_specs=[rowblk, pl.BlockSpec((tm, D), lambda i, k: (i, k)), rowblk, rowblk],
        out_shape=[jax.ShapeDtypeStruct((T, D), BF), jax.ShapeDtypeStruct((T, NSH * D), BF),
                   jax.ShapeDtypeStruct((T, D), BF), jax.ShapeDtypeStruct((T, D), F32)],
        scratch_shapes=[pltpu.VMEM((tm, D), BF), pltpu.VMEM((tm, D), F32)],
        compiler_params=_cp("arbitrary", "arbitrary"),
    )(x1, g3, w1, w2, g4)


def _loss_head(y, target):
    T = y.shape[0]
    tm = min(T, 512)

    def body(y_ref, t_ref, dy_ref, l_ref):
        @pl.when(pl.program_id(0) == 0)
        def _():
            l_ref[...] = jnp.zeros_like(l_ref)
        e = y_ref[...] - t_ref[...]
        dy_ref[...] = e * (1.0 / D)
        l_ref[...] += jnp.sum(e * e) * (0.5 / D)

    rowblk = pl.BlockSpec((tm, D), lambda i: (i, 0))
    return pl.pallas_call(
        body, name="loss_head", grid=(T // tm,),
        in_specs=[rowblk, rowblk],
        out_specs=[rowblk, pl.BlockSpec((1, 128), lambda i: (0, 0))],
        out_shape=[jax.ShapeDtypeStruct((T, D), F32), jax.ShapeDtypeStruct((1, 128), F32)],
        compiler_params=_cp("arbitrary"),
    )(y, target)


def _ffn_bwd(dx2, f, g4, a, w2, w1, x1, g3, dep):
    T = dx2.shape[0]
    tm = min(T, 512)

    def body(dx2_ref, f_ref, g4_ref, a_ref, w2_ref, w1_ref, x1_ref, g3_ref, dep_ref,
             df_ref, da_ref, dx1_ref, dg4_ref, dg3_ref, df_scr, acc):
        i, k = pl.program_id(0), pl.program_id(1)

        @pl.when((i == 0) & (k == 0))
        def _():
            dg4_ref[...] = jnp.zeros_like(dg4_ref)
            dg3_ref[...] = jnp.zeros_like(dg3_ref)

        @pl.when(k == 0)
        def _():
            df, dg = _rms_bwd(dx2_ref[...], f_ref[...].astype(F32), g4_ref[...])
            dg4_ref[...] += dg
            dfb = df.astype(BF)
            df_scr[...] = dfb
            df_ref[...] = dfb
            acc[...] = jnp.zeros_like(acc)

        av = a_ref[...].astype(F32)
        da = (_dot_nt(df_scr[...], w2_ref[...]) * (2.0 * jnp.maximum(av, 0.0))).astype(BF)
        da_ref[...] = da
        acc[...] += _dot_nt(da, w1_ref[...])

        @pl.when(k == NSH - 1)
        def _():
            dx, dg = _rms_bwd(acc[...], x1_ref[...], g3_ref[...])
            dg3_ref[...] += dg
            dx1_ref[...] = dx2_ref[...] + dx

    rowblk = pl.BlockSpec((tm, D), lambda i, k: (i, 0))
    vec = pl.BlockSpec((1, D), lambda i, k: (0, 0))
    return pl.pallas_call(
        body, name="ffn_bwd", grid=(T // tm, NSH),
        in_specs=[rowblk, rowblk, vec, pl.BlockSpec((tm, D), lambda i, k: (i, k)),
                  pl.BlockSpec((D, D), lambda i, k: (k, 0)),
                  pl.BlockSpec((None, D, D), lambda i, k: (k, 0, 0)), rowblk, vec, ANY],
        out_specs=[rowblk, pl.BlockSpec((tm, D), lambda i, k: (i, k)), rowblk, vec, vec],
        out_shape=[jax.ShapeDtypeStruct((T, D), BF), jax.ShapeDtypeStruct((T, NSH * D), BF),
                   jax.ShapeDtypeStruct((T, D), F32), jax.ShapeDtypeStruct((1, D), F32),
                   jax.ShapeDtypeStruct((1, D), F32)],
        scratch_shapes=[pltpu.VMEM((tm, D), BF), pltpu.VMEM((tm, D), F32)],
        compiler_params=_cp("arbitrary", "arbitrary"),
    )(dx2, f, g4, a, w2, w1, x1, g3, dep)


def _wgrad(name, ops, grid, in_specs, out_spec, out_shape, acc_shape, pick=None, relu2=False):
    nt = grid[-1]
    na = len(ops) - 1

    def body(*refs):
        a_refs, b_ref, o_ref, acc = refs[:na], refs[na], refs[na + 1], refs[na + 2]
        t = pl.program_id(len(grid) - 1)

        @pl.when(t == 0)
        def _():
            acc[...] = jnp.zeros_like(acc)

        def add(a_ref):
            av = a_ref[...]
            if relu2:
                r = jnp.maximum(av.astype(F32), 0.0)
                av = (r * r).astype(BF)
            acc[...] += _dot_tn(av, b_ref[...])

        if na == 1:
            add(a_refs[0])
        else:
            sel = pick()
            for n in range(na):
                pl.when(sel == n)(functools.partial(add, a_refs[n]))

        @pl.when(t == nt - 1)
        def _():
            if len(o_ref.shape) == 2:
                o_ref[...] = acc[...].astype(o_ref.dtype)
            else:
                rs = o_ref.shape[1]
                for q in range(o_ref.shape[0]):
                    o_ref[q] = acc[q * rs:(q + 1) * rs, :].astype(o_ref.dtype)

    return pl.pallas_call(
        body, name=name, grid=grid, in_specs=in_specs, out_specs=out_spec, out_shape=out_shape,
        scratch_shapes=[pltpu.VMEM(acc_shape, F32)],
        compiler_params=_cp(*(["arbitrary"] * len(grid))),
    )(*ops)


def _mix_out_bwd(dx1, m, gp, wo, p3, z, wb, dep):
    T = dx1.shape[0]
    tm = min(T, 512)
    rk = D // NSH

    def body(dx1_ref, m_ref, gp_ref, wo_ref, p_ref, g_ref, wb_ref, dep_ref,
             dm_ref, dp_ref, dy_ref, dz_ref, dgp_ref, dmg):
        i, b = pl.program_id(0), pl.program_id(1)

        @pl.when((i == 0) & (b == 0))
        def _():
            dgp_ref[...] = jnp.zeros_like(dgp_ref)

        @pl.when(b == 0)
        def _():
            dm, dg = _rms_bwd(dx1_ref[...], m_ref[...].astype(F32), gp_ref[...])
            dgp_ref[...] += dg
            dmb = dm.astype(BF)
            dm_ref[...] = dmb
            dmg[...] = _dot_nt(dmb, wo_ref[...])

        gate = _sig(g_ref[...].astype(F32))
        d = dmg[...]
        dp = (d * gate).astype(BF)
        dp_ref[...] = dp
        dz_ref[...] = (d * p_ref[...].astype(F32) * gate * (1.0 - gate)).astype(BF)
        for k in range(NSH):
            dy_ref[:, k * rk:(k + 1) * rk] = _dot_nt(dp, wb_ref[k, b]).astype(BF)

    rowblk = pl.BlockSpec((tm, D), lambda i, b: (i, 0))
    br = pl.BlockSpec((None, tm, D), lambda i, b: (b, i, 0))
    vec = pl.BlockSpec((1, D), lambda i, b: (0, 0))
    return pl.pallas_call(
        body, name="mix_out_bwd", grid=(T // tm, 3),
        in_specs=[rowblk, rowblk, vec, pl.BlockSpec((D, D), lambda i, b: (0, 0)), br,
                  pl.BlockSpec((tm, D), lambda i, b: (i, 7 + b)),
                  pl.BlockSpec((NSH, 3, rk, D), lambda i, b: (0, 0, 0, 0)), ANY],
        out_specs=[rowblk, br, br, pl.BlockSpec((tm, D), lambda i, b: (i, 7 + b)), vec],
        out_shape=[jax.ShapeDtypeStruct((T, D), BF), jax.ShapeDtypeStruct((3, T, D), BF),
                   jax.ShapeDtypeStruct((3, T, D), BF), jax.ShapeDtypeStruct((T, 10 * D), BF),
                   jax.ShapeDtypeStruct((1, D), F32)],
        scratch_shapes=[pltpu.VMEM((tm, D), F32)],
        compiler_params=_cp("arbitrary", "arbitrary"),
    )(dx1, m, gp, wo, p3, z, wb, dep)


def _mix_a_bwd(dz, dy3, z, wa, S, dep):
    T = z.shape[0]
    tt = min(S, MIX_TILE)
    nt = S // tt
    ntt = T // tt
    tile, cur, halo, row = _tile_specs(tt, ntt, True)

    nrows = HALO + tt
    coffs, aoffs = _causal_offsets(KA), _anticausal_offsets(KA)
    cshifts, ashifts = _shifts_of(coffs), _shifts_of(aoffs)

    def body(dz_in, dy_ref, ah, ab, ac, ah_h, ac_h, w_ref, dep_ref, dz_ref, dw_ref, ext_p, ext_d, sh, wb, stage):
        i, b = pl.program_id(0), pl.program_id(1)
        ti = ntt - 1 - i

        @pl.when((i == 0) & (b == 0))
        def _():
            dw_ref[...] = jnp.zeros_like(dw_ref)
            ext_d[...] = jnp.zeros_like(ext_d)
            ext_p[nrows:, :] = jnp.zeros((SUBLANES, D), F32)
            _fill_taps(wb, w_ref, KA)

        @pl.when(b == 0)
        def _():
            first = (ti % nt) == 0
            last = (ti % nt) == nt - 1
            ext_p[0:HALO, :] = jnp.where(first, 0.0, ah_h[...].astype(F32) * ac_h[...].astype(F32))
            ext_d[tt:nrows, :] = jnp.where(last, 0.0, ext_d[0:HALO, :])

            def prod(r0):
                rows = pl.ds(r0, RC)
                ext_p[pl.ds(HALO + r0, RC), :] = ah[rows, :].astype(F32) * ac[rows, :].astype(F32)
            _chunks(tt, prod)
            _shifted_copies(ext_p, sh, nrows, cshifts)

            def mid(*r0s):
                for r0, q in zip(r0s, _conv_chunks(sh, wb, coffs, r0s, cshifts)):
                    rows = pl.ds(r0, RC)
                    dy = dy_ref[rows, :].astype(F32)
                    stage[1, rows, :] = (dy * q).astype(BF)
                    ext_d[rows, :] = dy * ab[rows, :].astype(F32)
            _chunk_pairs(tt, mid)
            _conv_wgrad_chunked(dw_ref, ext_d, sh, coffs, tt, cshifts)
            _shifted_copies(ext_d, sh, nrows, ashifts)

            def fin(*r0s):
                for r0, dp in zip(r0s, _conv_chunks(sh, wb, aoffs, r0s, ashifts)):
                    rows = pl.ds(r0, RC)
                    stage[0, rows, :] = (dp * ac[rows, :].astype(F32)).astype(BF)
                    stage[2, rows, :] = (dp * ah[rows, :].astype(F32)).astype(BF)
            _chunk_pairs(tt, fin)

        dz_ref[...] = stage[b]

    return pl.pallas_call(
        body, name="mix_a_bwd", grid=(ntt, 3),
        in_specs=[ANY, pl.BlockSpec((None, tt, D), lambda i, b: (0, tile(i), 0)),
                  cur(0), cur(1), cur(2), halo(0), halo(2), row(KA), ANY],
        out_specs=[pl.BlockSpec((tt, D), lambda i, b: (tile(i), b)), pl.BlockSpec((KA, D), lambda i, b: (0, 0))],
        out_shape=[jax.ShapeDtypeStruct(dz.shape, BF), jax.ShapeDtypeStruct((KA, D), F32)],
        scratch_shapes=[pltpu.VMEM((nrows + SUBLANES, D), F32), pltpu.VMEM((nrows + SUBLANES, D), F32),
                        pltpu.VMEM((max(len(cshifts), len(ashifts)), nrows, D), F32),
                        pltpu.VMEM((KA * SUBLANES, D), F32), pltpu.VMEM((3, tt, D), BF)],
        input_output_aliases={0: 0},
        compiler_params=_cp("arbitrary", "arbitrary"),
    )(dz, dy3, z, z, z, z, z, wa, dep)


def _mix_b_bwd(dz, dy3, s, z, wc, lg, lb, S):
    T = z.shape[0]
    tt = min(S, MIX_TILE)
    nt = S // tt
    ntt = T // tt
    tile, cur, halo, row = _tile_specs(tt, ntt, True)

    nrows = HALO + tt

    def body(dz_in, dy_ref, s_ref, ca, cg, ca_h, cg_h, w_ref, lg_ref, lb_ref,
             dz_ref, dw_ref, dbc_ref, dlg_ref, dlb_ref, ext_r, ext_d, sh, wb, accs, stage):
        i, b = pl.program_id(0), pl.program_id(1)
        ti = ntt - 1 - i

        @pl.when((i == 0) & (b == 0))
        def _():
            dw_ref[...] = jnp.zeros_like(dw_ref)
            dbc_ref[...] = jnp.zeros_like(dbc_ref)
            dlg_ref[...] = jnp.zeros_like(dlg_ref)
            dlb_ref[...] = jnp.zeros_like(dlb_ref)
            ext_d[...] = jnp.zeros_like(ext_d)
            ext_r[nrows:, :] = jnp.zeros((SUBLANES, D), F32)
            _fill_taps(wb, w_ref, KC)

        @pl.when(b == 0)
        def _():
            first = (ti % nt) == 0
            last = (ti % nt) == nt - 1
            ext_r[0:HALO, :] = jnp.where(first, 0.0, ca_h[...].astype(F32) * _sig(cg_h[...].astype(F32)))
            ext_d[tt:nrows, :] = jnp.where(last, 0.0, ext_d[0:HALO, :])
            accs[...] = jnp.zeros_like(accs)

            def point(r0):
                rows = pl.ds(r0, RC)
                n, r = _ln_stats(s_ref[rows, :].astype(F32))
                t = n * lg_ref[...] + lb_ref[...]
                sg = _sig(t)
                dt = dy_ref[rows, :].astype(F32) * (sg * (1.0 + t * (1.0 - sg)))
                accs[0] += dt * n
                accs[1] += dt
                ds = _ln_bwd(dt * lg_ref[...], n, r)
                accs[2] += ds
                ext_d[rows, :] = ds
                ext_r[pl.ds(HALO + r0, RC), :] = ca[rows, :].astype(F32) * _sig(cg[rows, :].astype(F32))
            _chunks(tt, point)
            dlg_ref[...] += jnp.sum(accs[0], axis=0, keepdims=True)
            dlb_ref[...] += jnp.sum(accs[1], axis=0, keepdims=True)
            dbc_ref[...] += jnp.sum(accs[2], axis=0, keepdims=True)

            _shifted_copies(ext_r, sh, nrows)
            _conv_wgrad_chunked(dw_ref, ext_d, sh, _causal_offsets(KC), tt)
            _shifted_copies(ext_d, sh, nrows)

            def conv(*r0s):
                for r0, dr in zip(r0s, _conv_chunks(sh, wb, _anticausal_offsets(KC), r0s)):
                    rows = pl.ds(r0, RC)
                    cav = ca[rows, :].astype(F32)
                    sgc = _sig(cg[rows, :].astype(F32))
                    stage[0, rows, :] = (dr * sgc).astype(BF)
                    stage[1, rows, :] = (dr * cav * sgc * (1.0 - sgc)).astype(BF)
            _chunk_pairs(tt, conv)

        dz_ref[...] = stage[b]

    vec = pl.BlockSpec((1, D), lambda i, b: (0, 0))
    return pl.pallas_call(
        body, name="mix_b_bwd", grid=(ntt, 2),
        in_specs=[ANY, pl.BlockSpec((None, tt, D), lambda i, b: (1, tile(i), 0)),
                  pl.BlockSpec((tt, D), lambda i, b: (tile(i), 0)),
                  cur(3), cur(4), halo(3), halo(4), row(KC), row(), row()],
        out_specs=[pl.BlockSpec((tt, D), lambda i, b: (tile(i), 3 + b)),
                   pl.BlockSpec((KC, D), lambda i, b: (0, 0)), vec, vec, vec],
        out_shape=[jax.ShapeDtypeStruct(dz.shape, BF), jax.ShapeDtypeStruct((KC, D), F32)]
        + [jax.ShapeDtypeStruct((1, D), F32)] * 3,
        scratch_shapes=[pltpu.VMEM((nrows + SUBLANES, D), F32), pltpu.VMEM((nrows + SUBLANES, D), F32),
                        pltpu.VMEM((SUBLANES, nrows, D), F32), pltpu.VMEM((KC * SUBLANES, D), F32),
                        pltpu.VMEM((3, RC, D), F32), pltpu.VMEM((2, tt, D), BF)],
        input_output_aliases={0: 0},
        compiler_params=_cp("arbitrary", "arbitrary"),
    )(dz, dy3, s, z, z, z, z, wc, lg, lb)


def _mix_s_bwd(dz, dy3, z, lg, lb, ws, wst, bst, S):
    T = z.shape[0]
    tt = min(S, MIX_TILE)
    ntt = T // tt
    _, cur, _, row = _tile_specs(tt, ntt, False)

    def body(dz_in, dy_ref, su, sv, lg_ref, lb_ref, ws_ref, wst_ref, bst_ref,
             dz_ref, dws_ref, dbst_ref, dlg_ref, dlb_ref, u_scr, vn_scr, dvn_scr, stage):
        i, b = pl.program_id(0), pl.program_id(1)

        @pl.when((i == 0) & (b == 0))
        def _():
            dws_ref[...] = jnp.zeros_like(dws_ref)
            dbst_ref[...] = jnp.zeros_like(dbst_ref)
            dlg_ref[...] = jnp.zeros_like(dlg_ref)
            dlb_ref[...] = jnp.zeros_like(dlb_ref)

        @pl.when(b == 0)
        def _():
            u, du_dx = _gelu(su[...].astype(F32))
            v, dv_dx = _gelu(sv[...].astype(F32))
            u_scr[...] = u
            n, r = _ln_stats(v)
            vn_scr[...] = (n * lg_ref[...] + lb_ref[...]).astype(BF)
            mask = _causal_mask(False)
            mask_t = _causal_mask(True)
            for h in range(HEADS):
                wm = jnp.where(mask, ws_ref[h], 0.0).astype(BF)
                wmt = jnp.where(mask_t, wst_ref[h], 0.0).astype(BF)
                cols = slice(h * CHUNK, (h + 1) * CHUNK)
                for c in range(tt // CHUNK):
                    rows = slice(c * CHUNK, (c + 1) * CHUNK)
                    vb = vn_scr[rows, cols]
                    mixed = _dot(wm, vb) + bst_ref[:, h:h + 1]
                    dy = dy_ref[rows, cols].astype(F32)
                    dmix = dy * u_scr[rows, cols]
                    u_scr[rows, cols] = dy * mixed
                    dbst_ref[:, h:h + 1] += jnp.sum(dmix, axis=1, keepdims=True)
                    dmb = dmix.astype(BF)
                    dws_ref[h] += _dot_nt(dmb, vb)
                    dvn_scr[rows, cols] = _dot(wmt, dmb)
            stage[0] = (u_scr[...] * du_dx).astype(BF)
            dvn = dvn_scr[...]
            dlg_ref[...] += jnp.sum(dvn * n, axis=0, keepdims=True)
            dlb_ref[...] += jnp.sum(dvn, axis=0, keepdims=True)
            stage[1] = (_ln_bwd(dvn * lg_ref[...], n, r) * dv_dx).astype(BF)

        dz_ref[...] = stage[b]

    vec = pl.BlockSpec((1, D), lambda i, b: (0, 0))
    wsp = pl.BlockSpec((HEADS, CHUNK, CHUNK), lambda i, b: (0, 0, 0))
    bsp = pl.BlockSpec((CHUNK, HEADS), lambda i, b: (0, 0))
    return pl.pallas_call(
        body, name="mix_s_bwd", grid=(ntt, 2),
        in_specs=[ANY, pl.BlockSpec((None, tt, D), lambda i, b: (2, i, 0)),
                  cur(5), cur(6), row(), row(), wsp, wsp, bsp],
        out_specs=[pl.BlockSpec((tt, D), lambda i, b: (i, 5 + b)), wsp, bsp, vec, vec],
        out_shape=[jax.ShapeDtypeStruct(dz.shape, BF), jax.ShapeDtypeStruct((HEADS, CHUNK, CHUNK), F32),
                   jax.ShapeDtypeStruct((CHUNK, HEADS), F32), jax.ShapeDtypeStruct((1, D), F32),
                   jax.ShapeDtypeStruct((1, D), F32)],
        scratch_shapes=[pltpu.VMEM((tt, D), F32), pltpu.VMEM((tt, D), BF), pltpu.VMEM((tt, D), F32),
                        pltpu.VMEM((2, tt, D), BF)],
        input_output_aliases={0: 0},
        compiler_params=_cp("arbitrary", "arbitrary"),
    )(dz, dy3, z, z, lg, lb, ws, wst, bst)


def _in_proj_bwd(dz, w, x, g, dx1, dep):
    T = x.shape[0]
    nc = w.shape[2]
    tm = min(T, 1024)
    tn = 1280
    nj = nc // tn
    ep = min(tm, 128)

    def body(dz_ref, w_ref, x_ref, g_ref, dx1_ref, dep_ref, dx_ref, dg_ref, acc):
        i, k, j = pl.program_id(0), pl.program_id(1), pl.program_id(2)

        @pl.when((i == 0) & (k == 0) & (j == 0))
        def _():
            dg_ref[...] = jnp.zeros_like(dg_ref)

        @pl.when((k == 0) & (j == 0))
        def _():
            acc[...] = jnp.zeros_like(acc)

        acc[...] += _dot_nt(dz_ref[...], w_ref[...])

        @pl.when((k == NSH - 1) & (j == nj - 1))
        def _():
            def step(c, dg):
                rows = pl.ds(pl.multiple_of(c * ep, ep), ep)
                dx, dgc = _rms_bwd(acc[rows, :], x_ref[rows, :], g_ref[...])
                dx_ref[rows, :] = dx1_ref[rows, :] + dx
                return dg + dgc
            dg_ref[...] += lax.fori_loop(0, tm // ep, step, jnp.zeros((1, D), F32))

    rowblk = pl.BlockSpec((tm, D), lambda i, k, j: (i, 0))
    vec = pl.BlockSpec((1, D), lambda i, k, j: (0, 0))
    return pl.pallas_call(
        body, name="in_proj_bwd", grid=(T // tm, NSH, nj),
        in_specs=[pl.BlockSpec((tm, tn), lambda i, k, j: (i, k * nj + j)),
                  pl.BlockSpec((None, D, tn), lambda i, k, j: (k, 0, j)), rowblk, vec, rowblk, ANY],
        out_specs=[rowblk, vec],
        out_shape=[jax.ShapeDtypeStruct((T, D), F32), jax.ShapeDtypeStruct((1, D), F32)],
        scratch_shapes=[pltpu.VMEM((tm, D), F32)],
        compiler_params=_cp("arbitrary", "arbitrary", "arbitrary"),
    )(dz, w, x, g, dx1, dep)


def _layer_fwd(x, p, S, dep, late):
    h, z = _in_proj(x, p["g_mix_pre"], p["w_in"], dep)
    ya = _mix_a_fwd(z, p["conv_a_w"], S)
    yc, s = _mix_b_fwd(z, p["conf_dw_w"], p["conf_dw_b"], p["conf_ln_g"], p["conf_ln_b"], S)
    ys = _mix_s_fwd(z, p["sgu_ln_g"], p["sgu_ln_b"], p["sgu_ws"], p["sgu_bt"], S)
    more, dep2 = late(ys)
    p.update(more)
    p3, merged, m, x1 = _mix_out_fwd(ya, yc, ys, z, x, p["w_branch"], p["w_out"], p["g_mix_post"], dep2)
    h2, a, f, x2 = _ffn_fwd(x1, p["g_ffn_pre"], p["w_ff1"], p["w_ff2"], p["g_ffn_post"])
    saved = dict(x=x, h=h, z=z, ya=ya, yc=yc, ys=ys, s=s, p3=p3, merged=merged, m=m, x1=x1, h2=h2, a=a, f=f)
    return x2, saved


def _layer_bwd(dx2, p, sv, S, dep, hooks):
    after_ffn, early, mid = hooks
    T = dx2.shape[0]
    bt = min(T, WGRAD_TILE)
    nt = T // bt
    rk = D // NSH
    df, da, dx1, dg_ffn_post, dg_ffn_pre = _ffn_bwd(dx2, sv["f"], p["g_ffn_post"], sv["a"], p["w_ff2"],
                                                    p["w_ff1"], sv["x1"], p["g_ffn_pre"], dep)
    dw_ff2 = _wgrad("wgrad_ff2", (sv["a"], df), (NSH, nt),
                    [pl.BlockSpec((bt, D), lambda k, t: (t, k)), pl.BlockSpec((bt, D), lambda k, t: (t, 0))],
                    pl.BlockSpec((None, D, D), lambda k, t: (k, 0, 0)),
                    jax.ShapeDtypeStruct((NSH, D, D), BF), (D, D), relu2=True)
    dw_ff1 = _wgrad("wgrad_ff1", (sv["h2"], da), (NSH, nt),
                    [pl.BlockSpec((bt, D), lambda k, t: (t, 0)), pl.BlockSpec((bt, D), lambda k, t: (t, k))],
                    pl.BlockSpec((None, D, D), lambda k, t: (k, 0, 0)),
                    jax.ShapeDtypeStruct((NSH, D, D), BF), (D, D))
    dm, dp3, dy3, dz, dg_mix_post = _mix_out_bwd(dx1, sv["m"], p["g_mix_post"], p["w_out"], sv["p3"], sv["z"],
                                                 p["w_branch"], after_ffn(dx1))
    dw_out = _wgrad("wgrad_out", (sv["merged"], dm), (nt,),
                    [pl.BlockSpec((bt, D), lambda t: (t, 0)), pl.BlockSpec((bt, D), lambda t: (t, 0))],
                    pl.BlockSpec((D, D), lambda t: (0, 0)),
                    jax.ShapeDtypeStruct((D, D), BF), (D, D)).reshape(NSH, rk, D)
    ysp = lambda n: pl.BlockSpec((bt, D), lambda b, t: (jnp.where(b == n, t, 0), 0))
    dw_br = _wgrad("wgrad_branch", (sv["ya"], sv["yc"], sv["ys"], dp3), (3, nt),
                   [ysp(0), ysp(1), ysp(2), pl.BlockSpec((None, bt, D), lambda b, t: (b, t, 0))],
                   pl.BlockSpec((NSH, None, rk, D), lambda b, t: (0, b, 0, 0)),
                   jax.ShapeDtypeStruct((NSH, 3, rk, D), BF), (D, D), pick=lambda: pl.program_id(0))
    dz, dwa = _mix_a_bwd(dz, dy3, sv["z"], p["conv_a_w"], S, early([dw_br, dw_out, dw_ff1, dw_ff2]))
    dz, dwc, dbc, dclg, dclb = _mix_b_bwd(dz, dy3, sv["s"], sv["z"], p["conf_dw_w"], p["conf_ln_g"],
                                          p["conf_ln_b"], S)
    dz, dws, dbst, dslg, dslb = _mix_s_bwd(dz, dy3, sv["z"], p["sgu_ln_g"], p["sgu_ln_b"], p["sgu_ws"],
                                           p["sgu_wst"], p["sgu_bt"], S)
    dx, dg_mix_pre = _in_proj_bwd(dz, p["w_in"], sv["x"], p["g_mix_pre"], dx1, mid(dz))
    tn = 1280
    nj = p["w_in"].shape[2] // tn
    dw_in = _wgrad("wgrad_in", (sv["h"], dz), (NSH, nj, nt),
                   [pl.BlockSpec((bt, D), lambda k, j, t: (t, 0)),
                    pl.BlockSpec((bt, tn), lambda k, j, t: (t, k * nj + j))],
                   pl.BlockSpec((None, D, tn), lambda k, j, t: (k, 0, j)),
                   jax.ShapeDtypeStruct(p["w_in"].shape, BF), (D, tn))
    tril = jnp.tril(jnp.ones((CHUNK, CHUNK), bool))
    small = dict(norm_mix_pre=dg_mix_pre, norm_mix_post=dg_mix_post, norm_ffn_pre=dg_ffn_pre,
                 norm_ffn_post=dg_ffn_post, conv_a_w=dwa, conf_dw_w=dwc, conf_dw_b=dbc, conf_ln_g=dclg,
                 conf_ln_b=dclb, sgu_ln_g=dslg, sgu_ln_b=dslb,
                 sgu_ws=jnp.where(tril[None], dws, 0.0), sgu_b=dbst.T)
    big = dict(w_in=dw_in, w_branch=dw_br, w_out=dw_out, w_ff1=dw_ff1, w_ff2=dw_ff2)
    return dx, big, small


SMALL_NAMES = ("norm_mix_pre", "norm_mix_post", "norm_ffn_pre", "norm_ffn_post", "conv_a_w", "conf_dw_w",
               "conf_dw_b", "conf_ln_g", "conf_ln_b", "sgu_ln_g", "sgu_ln_b", "sgu_b", "sgu_ws")
SMALL_ROWS = dict(norm_mix_pre=1, norm_mix_post=1, norm_ffn_pre=1, norm_ffn_post=1, conv_a_w=KA, conf_dw_w=KC,
                  conf_dw_b=1, conf_ln_g=1, conf_ln_b=1, sgu_ln_g=1, sgu_ln_b=1, sgu_b=1, sgu_ws=CHUNK)
def _pad8(r):
    return -(-r // SUBLANES) * SUBLANES


PACK_ROWS = sum(_pad8(r) for r in SMALL_ROWS.values())


def _pack_small(d):
    parts = []
    for n in SMALL_NAMES:
        r = SMALL_ROWS[n]
        parts.append(jnp.pad(d[n].reshape(r, D).astype(F32), ((0, _pad8(r) - r), (0, 0))))
    return jnp.concatenate(parts, axis=0)


def _unpack_small(a, shapes):
    out, r = {}, 0
    for n in SMALL_NAMES:
        out[n] = a[:, r:r + SMALL_ROWS[n]].reshape((a.shape[0],) + tuple(shapes[n]))
        r += _pad8(SMALL_ROWS[n])
    return out


def _me():
    return lax.axis_index("x"), lax.axis_index("y"), lax.axis_index("c")


def _slab(ref, q, a, h=None):
    r = ref.shape[1]
    rows = slice(None) if h is None else pl.ds(h * (r // 2), r // 2)
    return ref.at[pl.ds(q * a, a), rows, :]


def _rows(ref, h):
    r = ref.shape[-2]
    lead = (slice(None),) * (len(ref.shape) - 2)
    return ref.at[lead + (pl.ds(h * (r // 2), r // 2), slice(None))]


def _rcopy(src, dst, sems, idx, dev):
    return pltpu.make_async_remote_copy(src_ref=src, dst_ref=dst, send_sem=sems[0].at[idx], recv_sem=sems[1].at[idx],
                                        device_id=dev, device_id_type=MESH)


def _send_halves_to_sibling(parts):
    n = len(parts)

    def body(*refs):
        src, dst = refs[:n], refs[n:2 * n]
        sems = refs[2 * n:2 * n + 2]
        x, y, c = _me()
        cps = [_rcopy(_rows(src[i], 1 - c), dst[i], sems, i, (x, y, 1 - c)) for i in range(n)]
        for cp in cps:
            cp.start()
        for cp in cps:
            cp.wait()

    outs = [jax.ShapeDtypeStruct((p.shape[0], p.shape[1] // 2, p.shape[2]), p.dtype) for p in parts]
    return pl.pallas_call(
        body, name="pair_exchange", in_specs=[ANY] * n, out_specs=[ANY] * n, out_shape=outs,
        scratch_shapes=[pltpu.SemaphoreType.DMA((n,)), pltpu.SemaphoreType.DMA((n,))],
    )(*parts)


PAIR_BLOCK_BYTES = 3 * 512 * 1024


def _pair_add(parts, sibs, c):
    n = len(parts)
    steps = 1
    while any(p.shape[0] * (p.shape[1] // 2 // steps) * p.shape[2] * 2 > PAIR_BLOCK_BYTES for p in parts):
        steps *= 2

    def body(c_ref, *refs):
        for p_ref, s_ref, o_ref in zip(refs[:n], refs[n:2 * n], refs[2 * n:]):
            o_ref[...] = (p_ref[...].astype(F32) + s_ref[...].astype(F32)).astype(BF)

    def blk(p):
        return (p.shape[0], p.shape[1] // 2 // steps, p.shape[2])

    mine = [pl.BlockSpec(blk(p), lambda g, c_ref: (0, c_ref[0] * steps + g, 0)) for p in parts]
    same = [pl.BlockSpec(blk(p), lambda g, c_ref: (0, g, 0)) for p in parts]
    return pl.pallas_call(
        body, name="pair_add",
        grid_spec=pltpu.PrefetchScalarGridSpec(
            num_scalar_prefetch=1, grid=(steps,), in_specs=mine + same, out_specs=same),
        out_shape=[jax.ShapeDtypeStruct(s.shape, BF) for s in sibs],
        compiler_params=_cp("arbitrary"),
    )(c, *parts, *sibs)


def _other_chips(x, y):
    return [(1 - x, y), (x, 1 - y), (1 - x, 1 - y)]


def _split_call(name, copies, srcs, lands, sems=None, after=()):
    n, m = len(srcs), len(lands)
    hbm = lambda t: pltpu.HBM(t.shape, t.dtype)
    pin = lambda t: pltpu.with_memory_space_constraint(t, pltpu.HBM)
    thru = [hbm(t) for t in srcs] + [hbm(t) for t in lands]
    sem_spec = pl.BlockSpec(memory_space=pltpu.SEMAPHORE)
    effect = pltpu.CompilerParams(has_side_effects=pltpu.SideEffectType.DATAFLOW_SIDE_EFFECTING)
    if sems is None:
        def start_body(*refs):
            src, land = refs[:n], refs[n:n + m]
            ssem, rsem = refs[n + m + len(after)], refs[n + m + len(after) + 1]
            token = refs[-1]
            cps = copies(src, land, (ssem, rsem))
            for cp in cps:
                cp.start()
            token[...] = jnp.zeros_like(token)

        ncp = copies.count
        out = pl.pallas_call(
            start_body, name=name,
            out_shape=(pltpu.SemaphoreType.DMA((ncp,)), pltpu.SemaphoreType.DMA((ncp,)), *thru,
                       jax.ShapeDtypeStruct((8, 128), F32)),
            in_specs=[ANY] * (n + m + len(after)),
            out_specs=(sem_spec, sem_spec, *([ANY] * (n + m)), pl.BlockSpec(memory_space=pltpu.VMEM)),
            input_output_aliases={i: 2 + i for i in range(n + m)},
            compiler_params=effect,
        )(*[pin(t) for t in srcs], *[pin(t) for t in lands], *after)
        return out[0], out[1], list(out[2:2 + n]), list(out[2 + n:2 + n + m]), out[-1]

    def wait_body(*refs):
        src, land = refs[:n], refs[n:n + m]
        ssem, rsem = refs[n + m], refs[n + m + 1]
        for cp in copies(src, land, (ssem, rsem)):
            cp.wait_send()
            cp.wait_recv()

    out = pl.pallas_call(
        wait_body, name=name, out_shape=tuple(thru),
        in_specs=[ANY] * (n + m) + [sem_spec, sem_spec] + [ANY] * len(after),
        out_specs=tuple([ANY] * (n + m)),
        input_output_aliases={i: i for i in range(n + m)},
        compiler_params=effect,
    )(*srcs, *lands, sems[0], sems[1], *after)
    return list(out[:n]), list(out[n:])


def _cast_into(w, land, layer, kidx, dep):
    _, a, R, C = w.shape
    br = R
    while br * C > 512 * 1024 and br % 32 == 0:
        br //= 2

    def body(k_ref, w_ref, land_ref, dep_ref, o_ref):
        o_ref[...] = w_ref[...].astype(o_ref.dtype)

    return pl.pallas_call(
        body, name="cast_into",
        grid_spec=pltpu.PrefetchScalarGridSpec(
            num_scalar_prefetch=1, grid=(a, R // br),
            in_specs=[pl.BlockSpec((None, None, br, C), lambda e, i, k: (layer, e, i, 0)), ANY, ANY],
            out_specs=pl.BlockSpec((None, br, C), lambda e, i, k: (k[0] * a + e, i, 0))),
        out_shape=jax.ShapeDtypeStruct(land.shape, land.dtype), input_output_aliases={2: 0},
        compiler_params=_cp("arbitrary", "arbitrary"),
    )(kidx, w, land, dep)


class _GatherCopies:
    def __init__(self, n, halves=True):
        self.n, self.count, self.halves = n, 3 * n, halves

    def __call__(self, src, land, sems):
        x, y, c = _me()
        k = 2 * x + y
        cps = []
        for j, (qx, qy) in enumerate(_other_chips(x, y)):
            for i in range(self.n):
                mine = _slab(land[i], k, land[i].shape[0] // NSH, c if self.halves else None)
                cps.append(_rcopy(mine, mine, sems, j * self.n + i, (qx, qy, c)))
        return cps


def _gather_finish(lands):
    n = len(lands)

    def body(*refs):
        dst = refs[n:2 * n]
        sems = refs[2 * n:2 * n + 2]
        x, y, c = _me()
        av = [d.shape[0] // NSH for d in dst]
        cps = []
        for j, (qx, qy) in enumerate(_other_chips(x, y)):
            for i in range(n):
                got = _slab(dst[i], 2 * qx + qy, av[i], c)
                cps.append(_rcopy(got, got, sems, j * n + i, (x, y, 1 - c)))
        for cp in cps:
            cp.start()
        for j, (qx, qy) in enumerate(_other_chips(x, y)):
            for i in range(n):
                other = _slab(dst[i], 2 * qx + qy, av[i], 1 - c)
                _rcopy(other, other, sems, j * n + i, (x, y, c)).wait_recv()
        for cp in cps:
            cp.wait_send()

    return pl.pallas_call(
        body, name="gather_finish", in_specs=[ANY] * n, out_specs=[ANY] * n,
        out_shape=[jax.ShapeDtypeStruct(t.shape, t.dtype) for t in lands],
        input_output_aliases={i: i for i in range(n)},
        scratch_shapes=[pltpu.SemaphoreType.DMA((3 * n,)), pltpu.SemaphoreType.DMA((3 * n,))],
    )(*lands)


class _PairCopies:
    def __init__(self, n):
        self.n, self.count = n, n

    def __call__(self, src, land, sems):
        x, y, c = _me()
        return [_rcopy(_rows(src[i], 1 - c), land[i], sems, i, (x, y, 1 - c)) for i in range(self.n)]


class _ScatterCopies:
    def __init__(self, n):
        self.n, self.count = n, 3 * n

    def __call__(self, src, land, sems):
        x, y, c = _me()
        k = 2 * x + y
        cps = []
        for j, (qx, qy) in enumerate(_other_chips(x, y)):
            for i in range(self.n):
                a = src[i].shape[0] // NSH
                cps.append(_rcopy(_slab(src[i], 2 * qx + qy, a), _slab(land[i], k, a), sems, j * self.n + i,
                                  (qx, qy, c)))
        return cps


def _sum_chips(own, rcv, acc, layer, nlayers, idx):
    A, hr, C = rcv.shape
    a = A // NSH
    br = min(hr, 512)
    nb = hr // br

    def body(*refs):
        r0, r1, r2, r3 = refs[1:5]
        o_ref = refs[-1]
        o_ref[...] = ((r0[...].astype(F32) + r1[...].astype(F32)) + r2[...].astype(F32)) + r3[...].astype(F32)

    slot = lambda s: pl.BlockSpec((None, br, C), lambda e, i, ix: (ix[s] * a + e, i, 0))
    ops = [own, rcv, rcv, rcv]
    in_specs = [slot(0), slot(1), slot(2), slot(3)]
    aliases = {}
    if acc is not None:
        ops.append(acc)
        in_specs.append(ANY)
        aliases = {5: 0}
    return pl.pallas_call(
        body, name="sum_chips",
        grid_spec=pltpu.PrefetchScalarGridSpec(
            num_scalar_prefetch=1, grid=(a, nb), in_specs=in_specs,
            out_specs=pl.BlockSpec((None, None, br, C), lambda e, i, ix: (layer, e, ix[4] * nb + i, 0))),
        out_shape=jax.ShapeDtypeStruct((nlayers, a, 2 * hr, C), F32), input_output_aliases=aliases,
        compiler_params=_cp("arbitrary", "arbitrary"),
    )(idx, *ops)


def _join_halves(fulls):
    n = len(fulls)

    def body(*refs):
        buf = refs[n:2 * n]
        sems = refs[2 * n:2 * n + 2]
        x, y, c = _me()
        cps = [_rcopy(_rows(buf[i], c), _rows(buf[i], c), sems, i, (x, y, 1 - c)) for i in range(n)]
        for cp in cps:
            cp.start()
        for i in range(n):
            _rcopy(_rows(buf[i], 1 - c), _rows(buf[i], 1 - c), sems, i, (x, y, c)).wait_recv()
        for cp in cps:
            cp.wait_send()

    return pl.pallas_call(
        body, name="join_halves", in_specs=[ANY] * n, out_specs=[ANY] * n,
        out_shape=[jax.ShapeDtypeStruct(t.shape, t.dtype) for t in fulls],
        input_output_aliases={i: i for i in range(n)},
        scratch_shapes=[pltpu.SemaphoreType.DMA((n,)), pltpu.SemaphoreType.DMA((n,))],
    )(*fulls)


def _small_blocks(hr):
    br = hr
    while br > 512 and br % 16 == 0:
        br //= 2
    return br, hr // br


def _pair_sum_slot(part, sib, ck):
    R, C = part.shape
    hr = R // 2
    br, nb = _small_blocks(hr)

    def body(ix, p_ref, s_ref, o_ref):
        o_ref[...] = p_ref[...] + s_ref[...]

    return pl.pallas_call(
        body, name="pair_sum_slot",
        grid_spec=pltpu.PrefetchScalarGridSpec(
            num_scalar_prefetch=1, grid=(nb,),
            in_specs=[pl.BlockSpec((br, C), lambda i, ix: (ix[0] * nb + i, 0)),
                      pl.BlockSpec((br, C), lambda i, ix: (i, 0))],
            out_specs=pl.BlockSpec((None, br, C), lambda i, ix: (ix[1], i, 0))),
        out_shape=jax.ShapeDtypeStruct((NSH, hr, C), F32),
        compiler_params=_cp("arbitrary"),
    )(ck, part, sib)


def _sum_slots(slots, ck):
    _, hr, C = slots.shape
    br, nb = _small_blocks(hr)

    def body(ix, s_ref, o_ref):
        o_ref[...] = ((s_ref[0] + s_ref[1]) + s_ref[2]) + s_ref[3]

    return pl.pallas_call(
        body, name="sum_slots",
        grid_spec=pltpu.PrefetchScalarGridSpec(
            num_scalar_prefetch=1, grid=(nb,),
            in_specs=[pl.BlockSpec((NSH, br, C), lambda i, ix: (0, i, 0))],
            out_specs=pl.BlockSpec((br, C), lambda i, ix: (ix[0] * nb + i, 0))),
        out_shape=jax.ShapeDtypeStruct((2 * hr, C), F32),
        compiler_params=_cp("arbitrary"),
    )(ck, slots)


def _adamw(w, g, m, v):
    shape = w.shape
    C = shape[-1]
    R = shape[-2]
    A = 1
    for s in shape[:-2]:
        A *= s
    br = R
    while br * C > 256 * 1024 and br % 16 == 0:
        br //= 2
    c1 = 1.0 / (1.0 - ADAM_B1 ** ADAM_STEP)
    c2 = 1.0 / (1.0 - ADAM_B2 ** ADAM_STEP)

    def body(w_ref, g_ref, m_ref, v_ref, og_ref, d_ref, nm_ref, nv_ref):
        gv = g_ref[...]
        og_ref[...] = gv
        nm = ADAM_B1 * m_ref[...] + (1.0 - ADAM_B1) * gv
        nv = ADAM_B2 * v_ref[...] + (1.0 - ADAM_B2) * (gv * gv)
        nm_ref[...] = nm
        nv_ref[...] = nv
        d_ref[...] = -ADAM_LR * ((nm * c1) / (jnp.sqrt(nv * c2) + ADAM_EPS) + ADAM_WD * w_ref[...])

    blk = pl.BlockSpec((None, br, C), lambda a, i: (a, i, 0))
    outs = pl.pallas_call(
        body, name="adamw", grid=(A, R // br), in_specs=[blk] * 4, out_specs=[blk] * 4,
        out_shape=[jax.ShapeDtypeStruct((A, R, C), F32)] * 4,
        compiler_params=_cp("arbitrary", "arbitrary"),
    )(*(t.reshape(A, R, C) for t in (w, g, m, v)))
    return tuple(o.reshape(shape) for o in outs)


WEIGHTS = ("norm_mix_pre", "norm_mix_post", "norm_ffn_pre", "norm_ffn_post", "w_in", "conv_a_w", "conf_dw_w",
           "conf_dw_b", "conf_ln_g", "conf_ln_b", "sgu_ln_g", "sgu_ln_b", "sgu_ws", "sgu_b", "w_branch", "w_out",
           "w_ff1", "w_ff2")
BIG = ("w_in", "w_branch", "w_out", "w_ff1", "w_ff2")
CONV_ROWS = 48


def kernel(x, norm_mix_pre, norm_mix_post, norm_ffn_pre, norm_ffn_post, w_in, conv_a_w, conf_dw_w, conf_dw_b, conf_ln_g, conf_ln_b, sgu_ln_g, sgu_ln_b, sgu_ws, sgu_b, w_branch, w_out, w_ff1, w_ff2, loss_target, m_norm_mix_pre, m_norm_mix_post, m_norm_ffn_pre, m_norm_ffn_post, m_w_in, m_conv_a_w, m_conf_dw_w, m_conf_dw_b, m_conf_ln_g, m_conf_ln_b, m_sgu_ln_g, m_sgu_ln_b, m_sgu_ws, m_sgu_b, m_w_branch, m_w_out, m_w_ff1, m_w_ff2, v_norm_mix_pre, v_norm_mix_post, v_norm_ffn_pre, v_norm_ffn_post, v_w_in, v_conv_a_w, v_conf_dw_w, v_conf_dw_b, v_conf_ln_g, v_conf_ln_b, v_sgu_ln_g, v_sgu_ln_b, v_sgu_ws, v_sgu_b, v_w_branch, v_w_out, v_w_ff1, v_w_ff2):
    w = dict(norm_mix_pre=norm_mix_pre, norm_mix_post=norm_mix_post, norm_ffn_pre=norm_ffn_pre,
             norm_ffn_post=norm_ffn_post, w_in=w_in, conv_a_w=conv_a_w, conf_dw_w=conf_dw_w, conf_dw_b=conf_dw_b,
             conf_ln_g=conf_ln_g, conf_ln_b=conf_ln_b, sgu_ln_g=sgu_ln_g, sgu_ln_b=sgu_ln_b, sgu_ws=sgu_ws,
             sgu_b=sgu_b, w_branch=w_branch, w_out=w_out, w_ff1=w_ff1, w_ff2=w_ff2)
    mom = dict(norm_mix_pre=m_norm_mix_pre, norm_mix_post=m_norm_mix_post, norm_ffn_pre=m_norm_ffn_pre,
               norm_ffn_post=m_norm_ffn_post, w_in=m_w_in, conv_a_w=m_conv_a_w, conf_dw_w=m_conf_dw_w,
               conf_dw_b=m_conf_dw_b, conf_ln_g=m_conf_ln_g, conf_ln_b=m_conf_ln_b, sgu_ln_g=m_sgu_ln_g,
               sgu_ln_b=m_sgu_ln_b, sgu_ws=m_sgu_ws, sgu_b=m_sgu_b, w_branch=m_w_branch, w_out=m_w_out,
               w_ff1=m_w_ff1, w_ff2=m_w_ff2)
    var = dict(norm_mix_pre=v_norm_mix_pre, norm_mix_post=v_norm_mix_post, norm_ffn_pre=v_norm_ffn_pre,
               norm_ffn_post=v_norm_ffn_post, w_in=v_w_in, conv_a_w=v_conv_a_w, conf_dw_w=v_conf_dw_w,
               conf_dw_b=v_conf_dw_b, conf_ln_g=v_conf_ln_g, conf_ln_b=v_conf_ln_b, sgu_ln_g=v_sgu_ln_g,
               sgu_ln_b=v_sgu_ln_b, sgu_ws=v_sgu_ws, sgu_b=v_sgu_b, w_branch=v_w_branch, w_out=v_w_out,
               w_ff1=v_w_ff1, w_ff2=v_w_ff2)
    L = w_in.shape[0]
    nseq, S, _ = x.shape
    T = nseq * S
    rk = D // NSH
    mx, my, mc = _me()
    k_chip = 2 * mx + my

    big_src = [w_in.reshape(L, 1, D, w_in.shape[2]), w_branch, w_out.reshape(L, 1, rk, D),
               w_ff1.reshape(L, 1, D, w_ff1.shape[2]), w_ff2.reshape(L, 1, w_ff2.shape[1], D)]
    kidx = jnp.reshape(k_chip, (1,)).astype(jnp.int32)
    conv_src = jnp.concatenate(
        [jnp.pad(conv_a_w, ((0, 0), (0, SUBLANES - KA), (0, 0))), jnp.pad(conf_dw_w, ((0, 0), (0, 1), (0, 0))),
         jnp.zeros((L, CONV_ROWS - SUBLANES - KC - 1, rk), F32)], axis=1)[None]

    def early_params(l, g_in, conv_full):
        return dict(
            g_mix_pre=norm_mix_pre[l][None], g_mix_post=norm_mix_post[l][None], g_ffn_pre=norm_ffn_pre[l][None],
            g_ffn_post=norm_ffn_post[l][None], w_in=g_in, conv_a_w=conv_full[l, :KA],
            conf_dw_w=conv_full[l, SUBLANES:SUBLANES + KC], conf_dw_b=conf_dw_b[l][None],
            conf_ln_g=conf_ln_g[l][None], conf_ln_b=conf_ln_b[l][None], sgu_ln_g=sgu_ln_g[l][None],
            sgu_ln_b=sgu_ln_b[l][None], sgu_ws=sgu_ws[l], sgu_wst=jnp.swapaxes(sgu_ws[l], 1, 2),
            sgu_bt=sgu_b[l].T)

    def late_params(gathered):
        g_br, g_out, g_ff1, g_ff2 = gathered
        return dict(w_branch=g_br.reshape(NSH, 3, rk, D), w_out=g_out.reshape(D, D), w_ff1=g_ff1,
                    w_ff2=g_ff2.reshape(NSH * w_ff2.shape[1], D))

    def cast_lands(srcs, l, dep):
        return [_cast_into(s, lax.empty((NSH * s.shape[1],) + s.shape[2:], F32 if s is conv_src else BF), l, kidx,
                           dep) for s in srcs]

    def gather_start(name, lands, after):
        return _split_call(name, _GatherCopies(len(lands)), [], lands, after=after)

    def gather_land(name, flight, after):
        ssem, rsem, _, lands, _ = flight
        _, lands = _split_call(name, _GatherCopies(len(lands)), [], lands, (ssem, rsem), after)
        return _gather_finish(lands)

    zero_tok = jnp.zeros((8, 128), F32)
    xt = x.reshape(T, D)
    layers, saved = [], []
    head = gather_start("gather_start_0a", cast_lands([big_src[0], conv_src], 0, kidx), [])
    tails = [cast_lands(big_src[1:], l, head[4]) for l in range(L)]
    heads = [None] + [cast_lands(big_src[:1], l, head[4]) for l in range(1, L)]
    behind = [xt] + [t for ls in tails + heads[1:] for t in ls]
    conv_full = None
    for l in range(L):
        got = gather_land(f"gather_wait_{l}a", head, behind if l == 0 else [xt])
        g_in = got[0]
        if l == 0:
            conv_full = got[1].reshape(NSH, L, CONV_ROWS, rk).transpose(1, 2, 0, 3).reshape(L, CONV_ROWS, D)
        tail = gather_start(f"gather_start_{l}b", tails[l], [g_in])
        nxt = {}

        def late(after, l=l, tail=tail, nxt=nxt):
            more = late_params(gather_land(f"gather_wait_{l}b", tail, [after]))
            if l + 1 == L:
                return more, zero_tok
            nxt["head"] = gather_start(f"gather_start_{l + 1}a", heads[l + 1], [more["w_ff1"]])
            return more, nxt["head"][4]

        p = early_params(l, g_in, conv_full)
        xt, sv = _layer_fwd(xt, p, S, tail[4], late)
        head = nxt.get("head")
        layers.append(p)
        saved.append(sv)
    dx, loss_row = _loss_head(xt, loss_target.reshape(T, D))
    loss = lax.psum(loss_row[0, 0], ("x", "y", "c"))

    c_arr = jnp.reshape(mc, (1,)).astype(jnp.int32)
    idx = jnp.stack([k_chip, k_chip ^ 2, k_chip ^ 1, k_chip ^ 3, mc]).astype(jnp.int32)
    fulls = {n: None for n in BIG}
    smalls = [None] * L

    def pair_start(tag, parts):
        lands = [lax.empty((p.shape[0], p.shape[1] // 2, p.shape[2]), p.dtype) for p in parts]
        return _split_call(f"pair_start_{tag}", _PairCopies(len(parts)), parts, lands)

    def pair_land_scatter_start(tag, fl, after):
        ssem, rsem, parts, sib, _ = fl
        parts, sib = _split_call(f"pair_wait_{tag}", _PairCopies(len(parts)), parts, sib, (ssem, rsem), after)
        sums = _pair_add(parts, sib, c_arr)
        rcv = [lax.empty(s.shape, s.dtype) for s in sums]
        return _split_call(f"scatter_start_{tag}", _ScatterCopies(len(sums)), sums, rcv)

    def scatter_land(tag, fl, names, l, after):
        ssem, rsem, sums, rcv, _ = fl
        sums, rcv = _split_call(f"scatter_wait_{tag}", _ScatterCopies(len(sums)), sums, rcv, (ssem, rsem), after)
        for n, o, r in zip(names, sums, rcv):
            fulls[n] = _sum_chips(o, r, fulls[n], l, L, idx)

    pending = []
    pair_b = None
    dep = zero_tok
    for l in reversed(range(L)):
        mine = {}

        def after_ffn(arr, l=l, mine=mine, pair_b=pair_b):
            if pair_b is None:
                return zero_tok
            mine["prev_b"] = pair_land_scatter_start(f"{l + 1}b", pair_b, [arr])
            return mine["prev_b"][4]

        def early(parts, l=l, mine=mine):
            br, rest = parts[0], parts[1:]
            mine["pair_a"] = pair_start(f"{l}a", [br.reshape(NSH * 3, rk, D), *rest])
            return mine["pair_a"][4]

        def mid(arr, l=l, mine=mine):
            mine["a"] = pair_land_scatter_start(f"{l}a", mine["pair_a"], [arr])
            return mine["a"][4]

        dx, big, small = _layer_bwd(dx, layers[l], saved[l], S, dep, (after_ffn, early, mid))
        smalls[l] = _pack_small(small)
        for args in pending:
            scatter_land(*args, [dx])
        pending = [(f"{l}a", mine["a"], BIG[1:], l)]
        if "prev_b" in mine:
            pending.append((f"{l + 1}b", mine["prev_b"], BIG[:1], l + 1))
        pair_b = pair_start(f"{l}b", [big["w_in"]])
        dep = pair_b[4]
    last_b = ("0b", pair_land_scatter_start("0b", pair_b, [dx]), BIG[:1], 0)

    packed = jnp.concatenate(smalls, axis=0)
    nrow = packed.shape[0]
    ck = jnp.stack([mc, k_chip]).astype(jnp.int32)
    (sib,) = _send_halves_to_sibling([packed.reshape(1, nrow, D)])
    slots = _pair_sum_slot(packed, sib.reshape(nrow // 2, D), ck)
    small_flight = _split_call("small_start", _GatherCopies(1, halves=False), [], [slots])

    for args in pending:
        scatter_land(*args, [small_flight[4], last_b[1][4]])
    grads, delta, new_m, new_v = {}, {}, {}, {}
    for n, f in zip(BIG[1:], _join_halves([fulls[n] for n in BIG[1:]])):
        grads[n], delta[n], new_m[n], new_v[n] = _adamw(w[n], f.reshape(w[n].shape), mom[n], var[n])
    scatter_land(*last_b, [delta[BIG[-1]]])
    (f,) = _join_halves([fulls[BIG[0]]])
    n = BIG[0]
    grads[n], delta[n], new_m[n], new_v[n] = _adamw(w[n], f.reshape(w[n].shape), mom[n], var[n])

    _, (slots,) = _split_call("small_wait", _GatherCopies(1, halves=False), [], small_flight[3],
                              (small_flight[0], small_flight[1]), [delta[BIG[0]]])
    (small_sum,) = _join_halves([_sum_slots(slots, ck).reshape(1, 1, nrow, D)])
    shapes = {n: (w[n].shape[1:] if n not in ("conv_a_w", "conf_dw_w") else (w[n].shape[1], D)) for n in SMALL_NAMES}
    sg = _unpack_small(small_sum.reshape(L, PACK_ROWS, D), shapes)
    for n in SMALL_NAMES:
        if n in ("conv_a_w", "conf_dw_w"):
            grads[n] = lax.dynamic_slice_in_dim(sg[n], k_chip * rk, rk, axis=2)
        else:
            grads[n] = sg[n]

    for n in SMALL_NAMES:
        sh = w[n].shape
        flat = (sh[0] * sh[1], sh[2]) if n in ("conv_a_w", "conf_dw_w") else (-1, D)
        g, d, nm, nv = _adamw(*(t.reshape(flat) for t in (w[n], grads[n], mom[n], var[n])))
        grads[n], delta[n], new_m[n], new_v[n] = g.reshape(sh), d.reshape(sh), nm.reshape(sh), nv.reshape(sh)

    return (loss, dx.reshape(x.shape), *[grads[n] for n in WEIGHTS], *[delta[n] for n in WEIGHTS],
            *[new_m[n] for n in WEIGHTS], *[new_v[n] for n in WEIGHTS])
```

```python
import functools

import jax
import jax.numpy as jnp
from jax import lax
from jax.experimental import pallas as pl
from jax.experimental.pallas import tpu as pltpu

D = 1024
HEADS = 8
CHUNK = 128
KA = 3
KC = 31
HALO = 32
SUBLANES = 8
MIX_TILE = 512
WGRAD_TILE = 1024
NSH = 4
NDEV = 8
EPS = 1e-6
BF = jnp.bfloat16
F32 = jnp.float32
VMEM_LIMIT = 56 * 1024 * 1024

ADAM_LR = 0.001
ADAM_B1 = 0.9
ADAM_B2 = 0.999
ADAM_EPS = 1e-08
ADAM_WD = 0.01
ADAM_STEP = 10

MESH = pl.DeviceIdType.MESH
ANY = pl.BlockSpec(memory_space=pl.ANY)


def _cp(*sem):
    return pltpu.CompilerParams(dimension_semantics=sem, vmem_limit_bytes=VMEM_LIMIT)


def _sig(x):
    return 1.0 / (1.0 + jnp.exp(-x))


_GC = 0.7978845608028654


def _gelu(x):
    x2 = x * x
    t = jnp.tanh(_GC * x * (1.0 + 0.044715 * x2))
    y = 0.5 * x * (1.0 + t)
    dy = 0.5 * (1.0 + t) + 0.5 * x * (1.0 - t * t) * _GC * (1.0 + 3.0 * 0.044715 * x2)
    return y, dy


def _rms_fwd(x, g):
    r = lax.rsqrt(jnp.mean(x * x, axis=-1, keepdims=True) + EPS)
    return x * r * g


def _rms_bwd(dy, x, g):
    r = lax.rsqrt(jnp.mean(x * x, axis=-1, keepdims=True) + EPS)
    xn = x * r
    dyg = dy * g
    dx = r * (dyg - xn * jnp.mean(dyg * xn, axis=-1, keepdims=True))
    return dx, jnp.sum(dy * xn, axis=0, keepdims=True)


def _ln_stats(x):
    mu = jnp.mean(x, axis=-1, keepdims=True)
    xc = x - mu
    r = lax.rsqrt(jnp.mean(xc * xc, axis=-1, keepdims=True) + EPS)
    return xc * r, r


def _ln_bwd(dn, n, r):
    return r * (dn - jnp.mean(dn, axis=-1, keepdims=True) - n * jnp.mean(dn * n, axis=-1, keepdims=True))


def _dot(a, b):
    return jnp.dot(a, b, preferred_element_type=F32)


def _dot_nt(a, b):
    return lax.dot_general(a, b, (((1,), (1,)), ((), ())), preferred_element_type=F32)


def _dot_tn(a, b):
    return lax.dot_general(a, b, (((0,), (0,)), ((), ())), preferred_element_type=F32)


def _in_proj(x, g, w, dep):
    T = x.shape[0]
    nc = w.shape[2]
    tm = min(T, 1024)
    tn = nc
    nj = nc // tn

    def body(x_ref, g_ref, w_ref, dep_ref, h_ref, z_ref, h_scr):
        @pl.when((pl.program_id(1) == 0) & (pl.program_id(2) == 0))
        def _():
            h = _rms_fwd(x_ref[...], g_ref[...]).astype(BF)
            h_scr[...] = h
            h_ref[...] = h
        z_ref[...] = _dot(h_scr[...], w_ref[...]).astype(BF)

    return pl.pallas_call(
        body, name="in_proj", grid=(T // tm, NSH, nj),
        in_specs=[pl.BlockSpec((tm, D), lambda i, k, j: (i, 0)),
                  pl.BlockSpec((1, D), lambda i, k, j: (0, 0)),
                  pl.BlockSpec((None, D, tn), lambda i, k, j: (k, 0, j)), ANY],
        out_specs=[pl.BlockSpec((tm, D), lambda i, k, j: (i, 0)),
                   pl.BlockSpec((tm, tn), lambda i, k, j: (i, k * nj + j))],
        out_shape=[jax.ShapeDtypeStruct((T, D), BF), jax.ShapeDtypeStruct((T, NSH * nc), BF)],
        scratch_shapes=[pltpu.VMEM((tm, D), BF)],
        compiler_params=_cp("arbitrary", "arbitrary", "arbitrary"),
    )(x, g, w, dep)


def _tile_specs(tt, nt_total, reverse):
    def tile(i):
        return (nt_total - 1 - i) if reverse else i

    def cur(c):
        return pl.BlockSpec((tt, D), lambda i, *_: (tile(i), c))

    def halo(c):
        return pl.BlockSpec((HALO, D), lambda i, *_: (jnp.maximum(tile(i) * (tt // HALO) - 1, 0), c))

    def row(r=1):
        return pl.BlockSpec((r, D), lambda i, *_: (0, 0))

    return tile, cur, halo, row


RC = 16


def _chunks(tt, fn, group=2):
    def step(c, carry):
        for u in range(group):
            fn(pl.multiple_of((c * group + u) * RC, RC))
        return carry
    lax.fori_loop(0, tt // (RC * group), step, 0)


def _chunk_pairs(tt, fn):
    def step(c, carry):
        fn(pl.multiple_of(c * 2 * RC, RC), pl.multiple_of(c * 2 * RC + RC, RC))
        return carry
    lax.fori_loop(0, tt // (2 * RC), step, 0)


ALL_SHIFTS = tuple(range(SUBLANES))


def _shifts_of(offs):
    return tuple(sorted({o % SUBLANES for o in offs}))


def _shifted_copies(ext, sh, nrows, shifts=ALL_SHIFTS):
    for i, s in enumerate(shifts):
        sh[i] = ext[pl.ds(s, nrows), :]


def _window(sh, o, r0, shifts=ALL_SHIFTS):
    return sh[shifts.index(o % SUBLANES), pl.ds(r0 + (o // SUBLANES) * SUBLANES, RC), :]


def _fill_taps(wb, w_ref, ntap):
    for k in range(ntap):
        wb[k * SUBLANES:(k + 1) * SUBLANES, :] = jnp.broadcast_to(w_ref[k:k + 1, :], (SUBLANES, D))


def _conv_chunks(sh, wb, offs, r0s, shifts=ALL_SHIFTS):
    accs = []
    for r0 in r0s:
        acc = None
        for k, o in enumerate(offs):
            wk = wb[k * SUBLANES:(k + 1) * SUBLANES, :]
            term = jnp.concatenate([wk] * (RC // SUBLANES), axis=0) * _window(sh, o, r0, shifts)
            acc = term if acc is None else acc + term
        accs.append(acc)
    return accs


WG_TAPS = 5


def _conv_wgrad_chunked(dw_ref, d_ref, sh, offs, tt, shifts=ALL_SHIFTS):
    for g0 in range(0, len(offs), WG_TAPS):
        grp = offs[g0:g0 + WG_TAPS]

        def step(c, accs, grp=grp):
            for u in range(2):
                r0 = pl.multiple_of((2 * c + u) * SUBLANES, SUBLANES)
                d = d_ref[pl.ds(r0, SUBLANES), :]
                accs = tuple(
                    a + d * sh[shifts.index(o % SUBLANES), pl.ds(r0 + (o // SUBLANES) * SUBLANES, SUBLANES), :]
                    for a, o in zip(accs, grp))
            return accs
        accs = lax.fori_loop(0, tt // (2 * SUBLANES), step,
                             tuple(jnp.zeros((SUBLANES, D), F32) for _ in grp))
        for j, a in enumerate(accs):
            dw_ref[g0 + j:g0 + j + 1, :] += jnp.sum(a, axis=0, keepdims=True)


def _causal_offsets(ntap):
    return [HALO - (ntap - 1) + k for k in range(ntap)]


def _anticausal_offsets(ntap):
    return [ntap - 1 - k for k in range(ntap)]


def _mix_a_fwd(z, wa, S):
    T = z.shape[0]
    tt = min(S, MIX_TILE)
    nt = S // tt
    _, cur, halo, row = _tile_specs(tt, T // tt, False)

    nrows = HALO + tt
    offs = _causal_offsets(KA)
    shifts = _shifts_of(offs)

    def body(ah, ab, ac, ah_h, ac_h, w_ref, y_ref, ext, sh, wb):
        @pl.when(pl.program_id(0) == 0)
        def _():
            _fill_taps(wb, w_ref, KA)
            ext[nrows:, :] = jnp.zeros((SUBLANES, D), F32)

        first = (pl.program_id(0) % nt) == 0
        ph = ah_h[...].astype(F32) * ac_h[...].astype(F32)
        ext[0:HALO, :] = jnp.where(first, 0.0, ph)

        def prod(r0):
            rows = pl.ds(r0, RC)
            ext[pl.ds(HALO + r0, RC), :] = ah[rows, :].astype(F32) * ac[rows, :].astype(F32)
        _chunks(tt, prod)
        _shifted_copies(ext, sh, nrows, shifts)

        def conv(*r0s):
            for r0, q in zip(r0s, _conv_chunks(sh, wb, offs, r0s, shifts)):
                rows = pl.ds(r0, RC)
                y_ref[rows, :] = (ab[rows, :].astype(F32) * q).astype(BF)
        _chunk_pairs(tt, conv)

    return pl.pallas_call(
        body, name="mix_a_fwd", grid=(T // tt,),
        in_specs=[cur(0), cur(1), cur(2), halo(0), halo(2), row(KA)],
        out_specs=pl.BlockSpec((tt, D), lambda i: (i, 0)),
        out_shape=jax.ShapeDtypeStruct((T, D), BF),
        scratch_shapes=[pltpu.VMEM((nrows + SUBLANES, D), F32), pltpu.VMEM((len(shifts), nrows, D), F32),
                        pltpu.VMEM((KA * SUBLANES, D), F32)],
        compiler_params=_cp("arbitrary"),
    )(z, z, z, z, z, wa)


def _mix_b_fwd(z, wc, bc, lg, lb, S):
    T = z.shape[0]
    tt = min(S, MIX_TILE)
    nt = S // tt
    _, cur, halo, row = _tile_specs(tt, T // tt, False)

    nrows = HALO + tt
    offs = _causal_offsets(KC)

    def body(ca, cg, ca_h, cg_h, w_ref, bc_ref, lg_ref, lb_ref, y_ref, s_ref, ext, sh, wb):
        @pl.when(pl.program_id(0) == 0)
        def _():
            _fill_taps(wb, w_ref, KC)
            ext[nrows:, :] = jnp.zeros((SUBLANES, D), F32)

        first = (pl.program_id(0) % nt) == 0
        rh = ca_h[...].astype(F32) * _sig(cg_h[...].astype(F32))
        ext[0:HALO, :] = jnp.where(first, 0.0, rh)

        def glu(r0):
            rows = pl.ds(r0, RC)
            ext[pl.ds(HALO + r0, RC), :] = ca[rows, :].astype(F32) * _sig(cg[rows, :].astype(F32))
        _chunks(tt, glu)
        _shifted_copies(ext, sh, nrows)

        def conv(*r0s):
            for r0, q in zip(r0s, _conv_chunks(sh, wb, offs, r0s)):
                rows = pl.ds(r0, RC)
                s = q + bc_ref[...]
                s_ref[rows, :] = s.astype(BF)
                n, _ = _ln_stats(s)
                t = n * lg_ref[...] + lb_ref[...]
                y_ref[rows, :] = (t * _sig(t)).astype(BF)
        _chunk_pairs(tt, conv)

    return pl.pallas_call(
        body, name="mix_b_fwd", grid=(T // tt,),
        in_specs=[cur(3), cur(4), halo(3), halo(4), row(KC), row(), row(), row()],
        out_specs=[pl.BlockSpec((tt, D), lambda i: (i, 0))] * 2,
        out_shape=[jax.ShapeDtypeStruct((T, D), BF)] * 2,
        scratch_shapes=[pltpu.VMEM((nrows + SUBLANES, D), F32), pltpu.VMEM((SUBLANES, nrows, D), F32),
                        pltpu.VMEM((KC * SUBLANES, D), F32)],
        compiler_params=_cp("arbitrary"),
    )(z, z, z, z, wc, bc, lg, lb)


def _causal_mask(transposed):
    r = lax.broadcasted_iota(jnp.int32, (CHUNK, CHUNK), 0)
    c = lax.broadcasted_iota(jnp.int32, (CHUNK, CHUNK), 1)
    return (c >= r) if transposed else (r >= c)


def _mix_s_fwd(z, lg, lb, ws, bst, S):
    T = z.shape[0]
    tt = min(S, MIX_TILE)
    _, cur, _, row = _tile_specs(tt, T // tt, False)

    def body(su, sv, lg_ref, lb_ref, ws_ref, bst_ref, y_ref, u_scr, vn_scr):
        u_scr[...] = _gelu(su[...].astype(F32))[0]
        n, _ = _ln_stats(_gelu(sv[...].astype(F32))[0])
        vn_scr[...] = (n * lg_ref[...] + lb_ref[...]).astype(BF)
        mask = _causal_mask(False)
        for h in range(HEADS):
            wm = jnp.where(mask, ws_ref[h], 0.0).astype(BF)
            cols = slice(h * CHUNK, (h + 1) * CHUNK)
            for c in range(tt // CHUNK):
                rows = slice(c * CHUNK, (c + 1) * CHUNK)
                mixed = _dot(wm, vn_scr[rows, cols]) + bst_ref[:, h:h + 1]
                y_ref[rows, cols] = (u_scr[rows, cols] * mixed).astype(BF)

    return pl.pallas_call(
        body, name="mix_s_fwd", grid=(T // tt,),
        in_specs=[cur(5), cur(6), row(), row(),
                  pl.BlockSpec((HEADS, CHUNK, CHUNK), lambda i: (0, 0, 0)),
                  pl.BlockSpec((CHUNK, HEADS), lambda i: (0, 0))],
        out_specs=pl.BlockSpec((tt, D), lambda i: (i, 0)),
        out_shape=jax.ShapeDtypeStruct((T, D), BF),
        scratch_shapes=[pltpu.VMEM((tt, D), F32), pltpu.VMEM((tt, D), BF)],
        compiler_params=_cp("arbitrary"),
    )(z, z, lg, lb, ws, bst)


def _mix_out_fwd(ya, yc, ys, z, x, wb, wo, gp, dep):
    T = x.shape[0]
    tm = min(T, 512)
    rk = D // NSH

    def body(ya_ref, yc_ref, ys_ref, ga, gc, gs, x_ref, wb_ref, wo_ref, gp_ref, dep_ref,
             p_ref, mg_ref, m_ref, x1_ref):
        acc = None
        for b, (y_ref, g_ref) in enumerate(((ya_ref, ga), (yc_ref, gc), (ys_ref, gs))):
            pb = None
            for k in range(NSH):
                part = _dot(y_ref[:, k * rk:(k + 1) * rk], wb_ref[k, b])
                pb = part if pb is None else pb + part
            p_ref[b] = pb.astype(BF)
            term = _sig(g_ref[...].astype(F32)) * pb
            acc = term if acc is None else acc + term
        mg = acc.astype(BF)
        mg_ref[...] = mg
        m = _dot(mg, wo_ref[...])
        m_ref[...] = m.astype(BF)
        x1_ref[...] = x_ref[...] + _rms_fwd(m, gp_ref[...])

    rowblk = pl.BlockSpec((tm, D), lambda i: (i, 0))
    return pl.pallas_call(
        body, name="mix_out_fwd", grid=(T // tm,),
        in_specs=[rowblk, rowblk, rowblk,
                  pl.BlockSpec((tm, D), lambda i: (i, 7)), pl.BlockSpec((tm, D), lambda i: (i, 8)),
                  pl.BlockSpec((tm, D), lambda i: (i, 9)), rowblk,
                  pl.BlockSpec((NSH, 3, rk, D), lambda i: (0, 0, 0, 0)),
                  pl.BlockSpec((D, D), lambda i: (0, 0)),
                  pl.BlockSpec((1, D), lambda i: (0, 0)), ANY],
        out_specs=[pl.BlockSpec((3, tm, D), lambda i: (0, i, 0)), rowblk, rowblk, rowblk],
        out_shape=[jax.ShapeDtypeStruct((3, T, D), BF), jax.ShapeDtypeStruct((T, D), BF),
                   jax.ShapeDtypeStruct((T, D), BF), jax.ShapeDtypeStruct((T, D), F32)],
        compiler_params=_cp("arbitrary"),
    )(ya, yc, ys, z, z, z, x, wb, wo, gp, dep)


def _ffn_fwd(x1, g3, w1, w2, g4):
    T = x1.shape[0]
    tm = min(T, 1024)

    def body(x_ref, g3_ref, w1_ref, w2_ref, g4_ref, h_ref, a_ref, f_ref, x2_ref, h_scr, acc):
        k = pl.program_id(1)

        @pl.when(k == 0)
        def _():
            h = _rms_fwd(x_ref[...], g3_ref[...]).astype(BF)
            h_scr[...] = h
            h_ref[...] = h
            acc[...] = jnp.zeros_like(acc)

        a = _dot(h_scr[...], w1_ref[...])
        a_ref[...] = a.astype(BF)
        r = jnp.maximum(a, 0.0)
        acc[...] += _dot((r * r).astype(BF), w2_ref[...])

        @pl.when(k == NSH - 1)
        def _():
            f = acc[...]
            f_ref[...] = f.astype(BF)
            x2_ref[...] = x_ref[...] + _rms_fwd(f, g4_ref[...])

    rowblk = pl.BlockSpec((tm, D), lambda i, k: (i, 0))
    vec = pl.BlockSpec((1, D), lambda i, k: (0, 0))
    return pl.pallas_call(
        body, name="ffn_fwd", grid=(T // tm, NSH),
        in_specs=[rowblk, vec, pl.BlockSpec((None, D, D), lambda i, k: (k, 0, 0)),
                  pl.BlockSpec((D, D), lambda i, k: (k, 0)), vec],
        out_specs=[rowblk, pl.BlockSpec((tm, D), lambda i, k: (i, k)), rowblk, rowblk],
        out_shape=[jax.ShapeDtypeStruct((T, D), BF), jax.ShapeDtypeStruct((T, NSH * D), BF),
                   jax.ShapeDtypeStruct((T, D), BF), jax.ShapeDtypeStruct((T, D), F32)],
        scratch_shapes=[pltpu.VMEM((tm, D), BF), pltpu.VMEM((tm, D), F32)],
        compiler_params=_cp("arbitrary", "arbitrary"),
    )(x1, g3, w1, w2, g4)


def _loss_head(y, target):
    T = y.shape[0]
    tm = min(T, 512)

    def body(y_ref, t_ref, dy_ref, l_ref):
        @pl.when(pl.program_id(0) == 0)
        def _():
            l_ref[...] = jnp.zeros_like(l_ref)
        e = y_ref[...] - t_ref[...]
        dy_ref[...] = e * (1.0 / D)
        l_ref[...] += jnp.sum(e * e) * (0.5 / D)

    rowblk = pl.BlockSpec((tm, D), lambda i: (i, 0))
    return pl.pallas_call(
        body, name="loss_head", grid=(T // tm,),
        in_specs=[rowblk, rowblk],
        out_specs=[rowblk, pl.BlockSpec((1, 128), lambda i: (0, 0))],
        out_shape=[jax.ShapeDtypeStruct((T, D), F32), jax.ShapeDtypeStruct((1, 128), F32)],
        compiler_params=_cp("arbitrary"),
    )(y, target)


def _ffn_bwd(dx2, f, g4, a, w2, w1, x1, g3, dep):
    T = dx2.shape[0]
    tm = min(T, 512)

    def body(dx2_ref, f_ref, g4_ref, a_ref, w2_ref, w1_ref, x1_ref, g3_ref, dep_ref,
             df_ref, da_ref, dx1_ref, dg4_ref, dg3_ref, df_scr, acc):
        i, k = pl.program_id(0), pl.program_id(1)

        @pl.when((i == 0) & (k == 0))
        def _():
            dg4_ref[...] = jnp.zeros_like(dg4_ref)
            dg3_ref[...] = jnp.zeros_like(dg3_ref)

        @pl.when(k == 0)
        def _():
            df, dg = _rms_bwd(dx2_ref[...], f_ref[...].astype(F32), g4_ref[...])
            dg4_ref[...] += dg
            dfb = df.astype(BF)
            df_scr[...] = dfb
            df_ref[...] = dfb
            acc[...] = jnp.zeros_like(acc)

        av = a_ref[...].astype(F32)
        da = (_dot_nt(df_scr[...], w2_ref[...]) * (2.0 * jnp.maximum(av, 0.0))).astype(BF)
        da_ref[...] = da
        acc[...] += _dot_nt(da, w1_ref[...])

        @pl.when(k == NSH - 1)
        def _():
            dx, dg = _rms_bwd(acc[...], x1_ref[...], g3_ref[...])
            dg3_ref[...] += dg
            dx1_ref[...] = dx2_ref[...] + dx

    rowblk = pl.BlockSpec((tm, D), lambda i, k: (i, 0))
    vec = pl.BlockSpec((1, D), lambda i, k: (0, 0))
    return pl.pallas_call(
        body, name="ffn_bwd", grid=(T // tm, NSH),
        in_specs=[rowblk, rowblk, vec, pl.BlockSpec((tm, D), lambda i, k: (i, k)),
                  pl.BlockSpec((D, D), lambda i, k: (k, 0)),
                  pl.BlockSpec((None, D, D), lambda i, k: (k, 0, 0)), rowblk, vec, ANY],
        out_specs=[rowblk, pl.BlockSpec((tm, D), lambda i, k: (i, k)), rowblk, vec, vec],
        out_shape=[jax.ShapeDtypeStruct((T, D), BF), jax.ShapeDtypeStruct((T, NSH * D), BF),
                   jax.ShapeDtypeStruct((T, D), F32), jax.ShapeDtypeStruct((1, D), F32),
                   jax.ShapeDtypeStruct((1, D), F32)],
        scratch_shapes=[pltpu.VMEM((tm, D), BF), pltpu.VMEM((tm, D), F32)],
        compiler_params=_cp("arbitrary", "arbitrary"),
    )(dx2, f, g4, a, w2, w1, x1, g3, dep)


def _wgrad(name, ops, grid, in_specs, out_spec, out_shape, acc_shape, pick=None, relu2=False):
    nt = grid[-1]
    na = len(ops) - 1

    def body(*refs):
        a_refs, b_ref, o_ref, acc = refs[:na], refs[na], refs[na + 1], refs[na + 2]
        t = pl.program_id(len(grid) - 1)

        @pl.when(t == 0)
        def _():
            acc[...] = jnp.zeros_like(acc)

        def add(a_ref):
            av = a_ref[...]
            if relu2:
                r = jnp.maximum(av.astype(F32), 0.0)
                av = (r * r).astype(BF)
            acc[...] += _dot_tn(av, b_ref[...])

        if na == 1:
            add(a_refs[0])
        else:
            sel = pick()
            for n in range(na):
                pl.when(sel == n)(functools.partial(add, a_refs[n]))

        @pl.when(t == nt - 1)
        def _():
            if len(o_ref.shape) == 2:
                o_ref[...] = acc[...].astype(o_ref.dtype)
            else:
                rs = o_ref.shape[1]
                for q in range(o_ref.shape[0]):
                    o_ref[q] = acc[q * rs:(q + 1) * rs, :].astype(o_ref.dtype)

    return pl.pallas_call(
        body, name=name, grid=grid, in_specs=in_specs, out_specs=out_spec, out_shape=out_shape,
        scratch_shapes=[pltpu.VMEM(acc_shape, F32)],
        compiler_params=_cp(*(["arbitrary"] * len(grid))),
    )(*ops)


def _mix_out_bwd(dx1, m, gp, wo, p3, z, wb, dep):
    T = dx1.shape[0]
    tm = min(T, 512)
    rk = D // NSH

    def body(dx1_ref, m_ref, gp_ref, wo_ref, p_ref, g_ref, wb_ref, dep_ref,
             dm_ref, dp_ref, dy_ref, dz_ref, dgp_ref, dmg):
        i, b = pl.program_id(0), pl.program_id(1)

        @pl.when((i == 0) & (b == 0))
        def _():
            dgp_ref[...] = jnp.zeros_like(dgp_ref)

        @pl.when(b == 0)
        def _():
            dm, dg = _rms_bwd(dx1_ref[...], m_ref[...].astype(F32), gp_ref[...])
            dgp_ref[...] += dg
            dmb = dm.astype(BF)
            dm_ref[...] = dmb
            dmg[...] = _dot_nt(dmb, wo_ref[...])

        gate = _sig(g_ref[...].astype(F32))
        d = dmg[...]
        dp = (d * gate).astype(BF)
        dp_ref[...] = dp
        dz_ref[...] = (d * p_ref[...].astype(F32) * gate * (1.0 - gate)).astype(BF)
        for k in range(NSH):
            dy_ref[:, k * rk:(k + 1) * rk] = _dot_nt(dp, wb_ref[k, b]).astype(BF)

    rowblk = pl.BlockSpec((tm, D), lambda i, b: (i, 0))
    br = pl.BlockSpec((None, tm, D), lambda i, b: (b, i, 0))
    vec = pl.BlockSpec((1, D), lambda i, b: (0, 0))
    return pl.pallas_call(
        body, name="mix_out_bwd", grid=(T // tm, 3),
        in_specs=[rowblk, rowblk, vec, pl.BlockSpec((D, D), lambda i, b: (0, 0)), br,
                  pl.BlockSpec((tm, D), lambda i, b: (i, 7 + b)),
                  pl.BlockSpec((NSH, 3, rk, D), lambda i, b: (0, 0, 0, 0)), ANY],
        out_specs=[rowblk, br, br, pl.BlockSpec((tm, D), lambda i, b: (i, 7 + b)), vec],
        out_shape=[jax.ShapeDtypeStruct((T, D), BF), jax.ShapeDtypeStruct((3, T, D), BF),
                   jax.ShapeDtypeStruct((3, T, D), BF), jax.ShapeDtypeStruct((T, 10 * D), BF),
                   jax.ShapeDtypeStruct((1, D), F32)],
        scratch_shapes=[pltpu.VMEM((tm, D), F32)],
        compiler_params=_cp("arbitrary", "arbitrary"),
    )(dx1, m, gp, wo, p3, z, wb, dep)


def _mix_a_bwd(dz, dy3, z, wa, S, dep):
    T = z.shape[0]
    tt = min(S, MIX_TILE)
    nt = S // tt
    ntt = T // tt
    tile, cur, halo, row = _tile_specs(tt, ntt, True)

    nrows = HALO + tt
    coffs, aoffs = _causal_offsets(KA), _anticausal_offsets(KA)
    cshifts, ashifts = _shifts_of(coffs), _shifts_of(aoffs)

    def body(dz_in, dy_ref, ah, ab, ac, ah_h, ac_h, w_ref, dep_ref, dz_ref, dw_ref, ext_p, ext_d, sh, wb, stage):
        i, b = pl.program_id(0), pl.program_id(1)
        ti = ntt - 1 - i

        @pl.when((i == 0) & (b == 0))
        def _():
            dw_ref[...] = jnp.zeros_like(dw_ref)
            ext_d[...] = jnp.zeros_like(ext_d)
            ext_p[nrows:, :] = jnp.zeros((SUBLANES, D), F32)
            _fill_taps(wb, w_ref, KA)

        @pl.when(b == 0)
        def _():
            first = (ti % nt) == 0
            last = (ti % nt) == nt - 1
            ext_p[0:HALO, :] = jnp.where(first, 0.0, ah_h[...].astype(F32) * ac_h[...].astype(F32))
            ext_d[tt:nrows, :] = jnp.where(last, 0.0, ext_d[0:HALO, :])

            def prod(r0):
                rows = pl.ds(r0, RC)
                ext_p[pl.ds(HALO + r0, RC), :] = ah[rows, :].astype(F32) * ac[rows, :].astype(F32)
            _chunks(tt, prod)
            _shifted_copies(ext_p, sh, nrows, cshifts)

            def mid(*r0s):
                for r0, q in zip(r0s, _conv_chunks(sh, wb, coffs, r0s, cshifts)):
                    rows = pl.ds(r0, RC)
                    dy = dy_ref[rows, :].astype(F32)
                    stage[1, rows, :] = (dy * q).astype(BF)
                    ext_d[rows, :] = dy * ab[rows, :].astype(F32)
            _chunk_pairs(tt, mid)
            _conv_wgrad_chunked(dw_ref, ext_d, sh, coffs, tt, cshifts)
            _shifted_copies(ext_d, sh, nrows, ashifts)

            def fin(*r0s):
                for r0, dp in zip(r0s, _conv_chunks(sh, wb, aoffs, r0s, ashifts)):
                    rows = pl.ds(r0, RC)
                    stage[0, rows, :] = (dp * ac[rows, :].astype(F32)).astype(BF)
                    stage[2, rows, :] = (dp * ah[rows, :].astype(F32)).astype(BF)
            _chunk_pairs(tt, fin)

        dz_ref[...] = stage[b]

    return pl.pallas_call(
        body, name="mix_a_bwd", grid=(ntt, 3),
        in_specs=[ANY, pl.BlockSpec((None, tt, D), lambda i, b: (0, tile(i), 0)),
                  cur(0), cur(1), cur(2), halo(0), halo(2), row(KA), ANY],
        out_specs=[pl.BlockSpec((tt, D), lambda i, b: (tile(i), b)), pl.BlockSpec((KA, D), lambda i, b: (0, 0))],
        out_shape=[jax.ShapeDtypeStruct(dz.shape, BF), jax.ShapeDtypeStruct((KA, D), F32)],
        scratch_shapes=[pltpu.VMEM((nrows + SUBLANES, D), F32), pltpu.VMEM((nrows + SUBLANES, D), F32),
                        pltpu.VMEM((max(len(cshifts), len(ashifts)), nrows, D), F32),
                        pltpu.VMEM((KA * SUBLANES, D), F32), pltpu.VMEM((3, tt, D), BF)],
        input_output_aliases={0: 0},
        compiler_params=_cp("arbitrary", "arbitrary"),
    )(dz, dy3, z, z, z, z, z, wa, dep)


def _mix_b_bwd(dz, dy3, s, z, wc, lg, lb, S):
    T = z.shape[0]
    tt = min(S, MIX_TILE)
    nt = S // tt
    ntt = T // tt
    tile, cur, halo, row = _tile_specs(tt, ntt, True)

    nrows = HALO + tt

    def body(dz_in, dy_ref, s_ref, ca, cg, ca_h, cg_h, w_ref, lg_ref, lb_ref,
             dz_ref, dw_ref, dbc_ref, dlg_ref, dlb_ref, ext_r, ext_d, sh, wb, accs, stage):
        i, b = pl.program_id(0), pl.program_id(1)
        ti = ntt - 1 - i

        @pl.when((i == 0) & (b == 0))
        def _():
            dw_ref[...] = jnp.zeros_like(dw_ref)
            dbc_ref[...] = jnp.zeros_like(dbc_ref)
            dlg_ref[...] = jnp.zeros_like(dlg_ref)
            dlb_ref[...] = jnp.zeros_like(dlb_ref)
            ext_d[...] = jnp.zeros_like(ext_d)
            ext_r[nrows:, :] = jnp.zeros((SUBLANES, D), F32)
            _fill_taps(wb, w_ref, KC)

        @pl.when(b == 0)
        def _():
            first = (ti % nt) == 0
            last = (ti % nt) == nt - 1
            ext_r[0:HALO, :] = jnp.where(first, 0.0, ca_h[...].astype(F32) * _sig(cg_h[...].astype(F32)))
            ext_d[tt:nrows, :] = jnp.where(last, 0.0, ext_d[0:HALO, :])
            accs[...] = jnp.zeros_like(accs)

            def point(r0):
                rows = pl.ds(r0, RC)
                n, r = _ln_stats(s_ref[rows, :].astype(F32))
                t = n * lg_ref[...] + lb_ref[...]
                sg = _sig(t)
                dt = dy_ref[rows, :].astype(F32) * (sg * (1.0 + t * (1.0 - sg)))
                accs[0] += dt * n
                accs[1] += dt
                ds = _ln_bwd(dt * lg_ref[...], n, r)
                accs[2] += ds
                ext_d[rows, :] = ds
                ext_r[pl.ds(HALO + r0, RC), :] = ca[rows, :].astype(F32) * _sig(cg[rows, :].astype(F32))
            _chunks(tt, point)
            dlg_ref[...] += jnp.sum(accs[0], axis=0, keepdims=True)
            dlb_ref[...] += jnp.sum(accs[1], axis=0, keepdims=True)
            dbc_ref[...] += jnp.sum(accs[2], axis=0, keepdims=True)

            _shifted_copies(ext_r, sh, nrows)
            _conv_wgrad_chunked(dw_ref, ext_d, sh, _causal_offsets(KC), tt)
            _shifted_copies(ext_d, sh, nrows)

            def conv(*r0s):
                for r0, dr in zip(r0s, _conv_chunks(sh, wb, _anticausal_offsets(KC), r0s)):
                    rows = pl.ds(r0, RC)
                    cav = ca[rows, :].astype(F32)
                    sgc = _sig(cg[rows, :].astype(F32))
                    stage[0, rows, :] = (dr * sgc).astype(BF)
                    stage[1, rows, :] = (dr * cav * sgc * (1.0 - sgc)).astype(BF)
            _chunk_pairs(tt, conv)

        dz_ref[...] = stage[b]

    vec = pl.BlockSpec((1, D), lambda i, b: (0, 0))
    return pl.pallas_call(
        body, name="mix_b_bwd", grid=(ntt, 2),
        in_specs=[ANY, pl.BlockSpec((None, tt, D), lambda i, b: (1, tile(i), 0)),
                  pl.BlockSpec((tt, D), lambda i, b: (tile(i), 0)),
                  cur(3), cur(4), halo(3), halo(4), row(KC), row(), row()],
        out_specs=[pl.BlockSpec((tt, D), lambda i, b: (tile(i), 3 + b)),
                   pl.BlockSpec((KC, D), lambda i, b: (0, 0)), vec, vec, vec],
        out_shape=[jax.ShapeDtypeStruct(dz.shape, BF), jax.ShapeDtypeStruct((KC, D), F32)]
        + [jax.ShapeDtypeStruct((1, D), F32)] * 3,
        scratch_shapes=[pltpu.VMEM((nrows + SUBLANES, D), F32), pltpu.VMEM((nrows + SUBLANES, D), F32),
                        pltpu.VMEM((SUBLANES, nrows, D), F32), pltpu.VMEM((KC * SUBLANES, D), F32),
                        pltpu.VMEM((3, RC, D), F32), pltpu.VMEM((2, tt, D), BF)],
        input_output_aliases={0: 0},
        compiler_params=_cp("arbitrary", "arbitrary"),
    )(dz, dy3, s, z, z, z, z, wc, lg, lb)


def _mix_s_bwd(dz, dy3, z, lg, lb, ws, wst, bst, S):
    T = z.shape[0]
    tt = min(S, MIX_TILE)
    ntt = T // tt
    _, cur, _, row = _tile_specs(tt, ntt, False)

    def body(dz_in, dy_ref, su, sv, lg_ref, lb_ref, ws_ref, wst_ref, bst_ref,
             dz_ref, dws_ref, dbst_ref, dlg_ref, dlb_ref, u_scr, vn_scr, dvn_scr, stage):
        i, b = pl.program_id(0), pl.program_id(1)

        @pl.when((i == 0) & (b == 0))
        def _():
            dws_ref[...] = jnp.zeros_like(dws_ref)
            dbst_ref[...] = jnp.zeros_like(dbst_ref)
            dlg_ref[...] = jnp.zeros_like(dlg_ref)
            dlb_ref[...] = jnp.zeros_like(dlb_ref)

        @pl.when(b == 0)
        def _():
            u, du_dx = _gelu(su[...].astype(F32))
            v, dv_dx = _gelu(sv[...].astype(F32))
            u_scr[...] = u
            n, r = _ln_stats(v)
            vn_scr[...] = (n * lg_ref[...] + lb_ref[...]).astype(BF)
            mask = _causal_mask(False)
            mask_t = _causal_mask(True)
            for h in range(HEADS):
                wm = jnp.where(mask, ws_ref[h], 0.0).astype(BF)
                wmt = jnp.where(mask_t, wst_ref[h], 0.0).astype(BF)
                cols = slice(h * CHUNK, (h + 1) * CHUNK)
                for c in range(tt // CHUNK):
                    rows = slice(c * CHUNK, (c + 1) * CHUNK)
                    vb = vn_scr[rows, cols]
                    mixed = _dot(wm, vb) + bst_ref[:, h:h + 1]
                    dy = dy_ref[rows, cols].astype(F32)
                    dmix = dy * u_scr[rows, cols]
                    u_scr[rows, cols] = dy * mixed
                    dbst_ref[:, h:h + 1] += jnp.sum(dmix, axis=1, keepdims=True)
                    dmb = dmix.astype(BF)
                    dws_ref[h] += _dot_nt(dmb, vb)
                    dvn_scr[rows, cols] = _dot(wmt, dmb)
            stage[0] = (u_scr[...] * du_dx).astype(BF)
            dvn = dvn_scr[...]
            dlg_ref[...] += jnp.sum(dvn * n, axis=0, keepdims=True)
            dlb_ref[...] += jnp.sum(dvn, axis=0, keepdims=True)
            stage[1] = (_ln_bwd(dvn * lg_ref[...], n, r) * dv_dx).astype(BF)

        dz_ref[...] = stage[b]

    vec = pl.BlockSpec((1, D), lambda i, b: (0, 0))
    wsp = pl.BlockSpec((HEADS, CHUNK, CHUNK), lambda i, b: (0, 0, 0))
    bsp = pl.BlockSpec((CHUNK, HEADS), lambda i, b: (0, 0))
    return pl.pallas_call(
        body, name="mix_s_bwd", grid=(ntt, 2),
        in_specs=[ANY, pl.BlockSpec((None, tt, D), lambda i, b: (2, i, 0)),
                  cur(5), cur(6), row(), row(), wsp, wsp, bsp],
        out_specs=[pl.BlockSpec((tt, D), lambda i, b: (i, 5 + b)), wsp, bsp, vec, vec],
        out_shape=[jax.ShapeDtypeStruct(dz.shape, BF), jax.ShapeDtypeStruct((HEADS, CHUNK, CHUNK), F32),
                   jax.ShapeDtypeStruct((CHUNK, HEADS), F32), jax.ShapeDtypeStruct((1, D), F32),
                   jax.ShapeDtypeStruct((1, D), F32)],
        scratch_shapes=[pltpu.VMEM((tt, D), F32), pltpu.VMEM((tt, D), BF), pltpu.VMEM((tt, D), F32),
                        pltpu.VMEM((2, tt, D), BF)],
        input_output_aliases={0: 0},
        compiler_params=_cp("arbitrary", "arbitrary"),
    )(dz, dy3, z, z, lg, lb, ws, wst, bst)


def _in_proj_bwd(dz, w, x, g, dx1, dep):
    T = x.shape[0]
    nc = w.shape[2]
    tm = min(T, 1024)
    tn = nc
    nj = nc // tn
    ep = min(tm, 128)

    def body(dz_ref, w_ref, x_ref, g_ref, dx1_ref, dep_ref, dx_ref, dg_ref, acc):
        i, k, j = pl.program_id(0), pl.program_id(1), pl.program_id(2)

        @pl.when((i == 0) & (k == 0) & (j == 0))
        def _():
            dg_ref[...] = jnp.zeros_like(dg_ref)

        @pl.when((k == 0) & (j == 0))
        def _():
            acc[...] = jnp.zeros_like(acc)

        acc[...] += _dot_nt(dz_ref[...], w_ref[...])

        @pl.when((k == NSH - 1) & (j == nj - 1))
        def _():
            def step(c, dg):
                rows = pl.ds(pl.multiple_of(c * ep, ep), ep)
                dx, dgc = _rms_bwd(acc[rows, :], x_ref[rows, :], g_ref[...])
                dx_ref[rows, :] = dx1_ref[rows, :] + dx
                return dg + dgc
            dg_ref[...] += lax.fori_loop(0, tm // ep, step, jnp.zeros((1, D), F32))

    rowblk = pl.BlockSpec((tm, D), lambda i, k, j: (i, 0))
    vec = pl.BlockSpec((1, D), lambda i, k, j: (0, 0))
    return pl.pallas_call(
        body, name="in_proj_bwd", grid=(T // tm, NSH, nj),
        in_specs=[pl.BlockSpec((tm, tn), lambda i, k, j: (i, k * nj + j)),
                  pl.BlockSpec((None, D, tn), lambda i, k, j: (k, 0, j)), rowblk, vec, rowblk, ANY],
        out_specs=[rowblk, vec],
        out_shape=[jax.ShapeDtypeStruct((T, D), F32), jax.ShapeDtypeStruct((1, D), F32)],
        scratch_shapes=[pltpu.VMEM((tm, D), F32)],
        compiler_params=_cp("arbitrary", "arbitrary", "arbitrary"),
    )(dz, w, x, g, dx1, dep)


def _layer_fwd(x, p, S, dep, late):
    h, z = _in_proj(x, p["g_mix_pre"], p["w_in"], dep)
    ya = _mix_a_fwd(z, p["conv_a_w"], S)
    yc, s = _mix_b_fwd(z, p["conf_dw_w"], p["conf_dw_b"], p["conf_ln_g"], p["conf_ln_b"], S)
    ys = _mix_s_fwd(z, p["sgu_ln_g"], p["sgu_ln_b"], p["sgu_ws"], p["sgu_bt"], S)
    more, dep2 = late(ys)
    p.update(more)
    p3, merged, m, x1 = _mix_out_fwd(ya, yc, ys, z, x, p["w_branch"], p["w_out"], p["g_mix_post"], dep2)
    h2, a, f, x2 = _ffn_fwd(x1, p["g_ffn_pre"], p["w_ff1"], p["w_ff2"], p["g_ffn_post"])
    saved = dict(x=x, h=h, z=z, ya=ya, yc=yc, ys=ys, s=s, p3=p3, merged=merged, m=m, x1=x1, h2=h2, a=a, f=f)
    return x2, saved


def _layer_bwd(dx2, p, sv, S, dep, hooks):
    after_ffn, early, mid = hooks
    T = dx2.shape[0]
    bt = min(T, WGRAD_TILE)
    nt = T // bt
    rk = D // NSH
    df, da, dx1, dg_ffn_post, dg_ffn_pre = _ffn_bwd(dx2, sv["f"], p["g_ffn_post"], sv["a"], p["w_ff2"],
                                                    p["w_ff1"], sv["x1"], p["g_ffn_pre"], dep)
    dw_ff2 = _wgrad("wgrad_ff2", (sv["a"], df), (NSH, nt),
                    [pl.BlockSpec((bt, D), lambda k, t: (t, k)), pl.BlockSpec((bt, D), lambda k, t: (t, 0))],
                    pl.BlockSpec((None, D, D), lambda k, t: (k, 0, 0)),
                    jax.ShapeDtypeStruct((NSH, D, D), BF), (D, D), relu2=True)
    dw_ff1 = _wgrad("wgrad_ff1", (sv["h2"], da), (NSH, nt),
                    [pl.BlockSpec((bt, D), lambda k, t: (t, 0)), pl.BlockSpec((bt, D), lambda k, t: (t, k))],
                    pl.BlockSpec((None, D, D), lambda k, t: (k, 0, 0)),
                    jax.ShapeDtypeStruct((NSH, D, D), BF), (D, D))
    dm, dp3, dy3, dz, dg_mix_post = _mix_out_bwd(dx1, sv["m"], p["g_mix_post"], p["w_out"], sv["p3"], sv["z"],
                                                 p["w_branch"], after_ffn(dx1))
    dw_out = _wgrad("wgrad_out", (sv["merged"], dm), (nt,),
                    [pl.BlockSpec((bt, D), lambda t: (t, 0)), pl.BlockSpec((bt, D), lambda t: (t, 0))],
                    pl.BlockSpec((D, D), lambda t: (0, 0)),
                    jax.ShapeDtypeStruct((D, D), BF), (D, D)).reshape(NSH, rk, D)
    ysp = lambda n: pl.BlockSpec((bt, D), lambda b, t: (jnp.where(b == n, t, 0), 0))
    dw_br = _wgrad("wgrad_branch", (sv["ya"], sv["yc"], sv["ys"], dp3), (3, nt),
                   [ysp(0), ysp(1), ysp(2), pl.BlockSpec((None, bt, D), lambda b, t: (b, t, 0))],
                   pl.BlockSpec((NSH, None, rk, D), lambda b, t: (0, b, 0, 0)),
                   jax.ShapeDtypeStruct((NSH, 3, rk, D), BF), (D, D), pick=lambda: pl.program_id(0))
    dz, dwa = _mix_a_bwd(dz, dy3, sv["z"], p["conv_a_w"], S, early([dw_br, dw_out, dw_ff1, dw_ff2]))
    dz, dwc, dbc, dclg, dclb = _mix_b_bwd(dz, dy3, sv["s"], sv["z"], p["conf_dw_w"], p["conf_ln_g"],
                                          p["conf_ln_b"], S)
    dz, dws, dbst, dslg, dslb = _mix_s_bwd(dz, dy3, sv["z"], p["sgu_ln_g"], p["sgu_ln_b"], p["sgu_ws"],
                                           p["sgu_wst"], p["sgu_bt"], S)
    dx, dg_mix_pre = _in_proj_bwd(dz, p["w_in"], sv["x"], p["g_mix_pre"], dx1, mid(dz))
    tn = p["w_in"].shape[2]
    nj = p["w_in"].shape[2] // tn
    dw_in = _wgrad("wgrad_in", (sv["h"], dz), (NSH, nj, nt),
                   [pl.BlockSpec((bt, D), lambda k, j, t: (t, 0)),
                    pl.BlockSpec((bt, tn), lambda k, j, t: (t, k * nj + j))],
                   pl.BlockSpec((None, D, tn), lambda k, j, t: (k, 0, j)),
                   jax.ShapeDtypeStruct(p["w_in"].shape, BF), (D, tn))
    tril = jnp.tril(jnp.ones((CHUNK, CHUNK), bool))
    small = dict(norm_mix_pre=dg_mix_pre, norm_mix_post=dg_mix_post, norm_ffn_pre=dg_ffn_pre,
                 norm_ffn_post=dg_ffn_post, conv_a_w=dwa, conf_dw_w=dwc, conf_dw_b=dbc, conf_ln_g=dclg,
                 conf_ln_b=dclb, sgu_ln_g=dslg, sgu_ln_b=dslb,
                 sgu_ws=jnp.where(tril[None], dws, 0.0), sgu_b=dbst.T)
    big = dict(w_in=dw_in, w_branch=dw_br, w_out=dw_out, w_ff1=dw_ff1, w_ff2=dw_ff2)
    return dx, big, small


SMALL_NAMES = ("norm_mix_pre", "norm_mix_post", "norm_ffn_pre", "norm_ffn_post", "conv_a_w", "conf_dw_w",
               "conf_dw_b", "conf_ln_g", "conf_ln_b", "sgu_ln_g", "sgu_ln_b", "sgu_b", "sgu_ws")
SMALL_ROWS = dict(norm_mix_pre=1, norm_mix_post=1, norm_ffn_pre=1, norm_ffn_post=1, conv_a_w=KA, conf_dw_w=KC,
                  conf_dw_b=1, conf_ln_g=1, conf_ln_b=1, sgu_ln_g=1, sgu_ln_b=1, sgu_b=1, sgu_ws=CHUNK)
def _pad8(r):
    return -(-r // SUBLANES) * SUBLANES


PACK_ROWS = sum(_pad8(r) for r in SMALL_ROWS.values())


def _pack_small(d):
    parts = []
    for n in SMALL_NAMES:
        r = SMALL_ROWS[n]
        parts.append(jnp.pad(d[n].reshape(r, D).astype(F32), ((0, _pad8(r) - r), (0, 0))))
    return jnp.concatenate(parts, axis=0)


def _unpack_small(a, shapes):
    out, r = {}, 0
    for n in SMALL_NAMES:
        out[n] = a[:, r:r + SMALL_ROWS[n]].reshape((a.shape[0],) + tuple(shapes[n]))
        r += _pad8(SMALL_ROWS[n])
    return out


def _me():
    return lax.axis_index("x"), lax.axis_index("y"), lax.axis_index("c")


def _slab(ref, q, a, h=None):
    r = ref.shape[1]
    rows = slice(None) if h is None else pl.ds(h * (r // 2), r // 2)
    return ref.at[pl.ds(q * a, a), rows, :]


def _rows(ref, h):
    r = ref.shape[-2]
    lead = (slice(None),) * (len(ref.shape) - 2)
    return ref.at[lead + (pl.ds(h * (r // 2), r // 2), slice(None))]


def _rcopy(src, dst, sems, idx, dev):
    return pltpu.make_async_remote_copy(src_ref=src, dst_ref=dst, send_sem=sems[0].at[idx], recv_sem=sems[1].at[idx],
                                        device_id=dev, device_id_type=MESH)


def _send_halves_to_sibling(parts):
    n = len(parts)

    def body(*refs):
        src, dst = refs[:n], refs[n:2 * n]
        sems = refs[2 * n:2 * n + 2]
        x, y, c = _me()
        cps = [_rcopy(_rows(src[i], 1 - c), dst[i], sems, i, (x, y, 1 - c)) for i in range(n)]
        for cp in cps:
            cp.start()
        for cp in cps:
            cp.wait()

    outs = [jax.ShapeDtypeStruct((p.shape[0], p.shape[1] // 2, p.shape[2]), p.dtype) for p in parts]
    return pl.pallas_call(
        body, name="pair_exchange", in_specs=[ANY] * n, out_specs=[ANY] * n, out_shape=outs,
        scratch_shapes=[pltpu.SemaphoreType.DMA((n,)), pltpu.SemaphoreType.DMA((n,))],
    )(*parts)


PAIR_BLOCK_BYTES = 3 * 512 * 1024


def _pair_add(parts, sibs, c):
    n = len(parts)
    steps = 1
    while any(p.shape[0] * (p.shape[1] // 2 // steps) * p.shape[2] * 2 > PAIR_BLOCK_BYTES for p in parts):
        steps *= 2

    def body(c_ref, *refs):
        for p_ref, s_ref, o_ref in zip(refs[:n], refs[n:2 * n], refs[2 * n:]):
            o_ref[...] = (p_ref[...].astype(F32) + s_ref[...].astype(F32)).astype(BF)

    def blk(p):
        return (p.shape[0], p.shape[1] // 2 // steps, p.shape[2])

    mine = [pl.BlockSpec(blk(p), lambda g, c_ref: (0, c_ref[0] * steps + g, 0)) for p in parts]
    same = [pl.BlockSpec(blk(p), lambda g, c_ref: (0, g, 0)) for p in parts]
    return pl.pallas_call(
        body, name="pair_add",
        grid_spec=pltpu.PrefetchScalarGridSpec(
            num_scalar_prefetch=1, grid=(steps,), in_specs=mine + same, out_specs=same),
        out_shape=[jax.ShapeDtypeStruct(s.shape, BF) for s in sibs],
        compiler_params=_cp("arbitrary"),
    )(c, *parts, *sibs)


def _other_chips(x, y):
    return [(1 - x, y), (x, 1 - y), (1 - x, 1 - y)]


def _split_call(name, copies, srcs, lands, sems=None, after=()):
    n, m = len(srcs), len(lands)
    hbm = lambda t: pltpu.HBM(t.shape, t.dtype)
    pin = lambda t: pltpu.with_memory_space_constraint(t, pltpu.HBM)
    thru = [hbm(t) for t in srcs] + [hbm(t) for t in lands]
    sem_spec = pl.BlockSpec(memory_space=pltpu.SEMAPHORE)
    effect = pltpu.CompilerParams(has_side_effects=pltpu.SideEffectType.DATAFLOW_SIDE_EFFECTING)
    if sems is None:
        def start_body(*refs):
            src, land = refs[:n], refs[n:n + m]
            ssem, rsem = refs[n + m + len(after)], refs[n + m + len(after) + 1]
            token = refs[-1]
            cps = copies(src, land, (ssem, rsem))
            for cp in cps:
                cp.start()
            token[...] = jnp.zeros_like(token)

        ncp = copies.count
        out = pl.pallas_call(
            start_body, name=name,
            out_shape=(pltpu.SemaphoreType.DMA((ncp,)), pltpu.SemaphoreType.DMA((ncp,)), *thru,
                       jax.ShapeDtypeStruct((8, 128), F32)),
            in_specs=[ANY] * (n + m + len(after)),
            out_specs=(sem_spec, sem_spec, *([ANY] * (n + m)), pl.BlockSpec(memory_space=pltpu.VMEM)),
            input_output_aliases={i: 2 + i for i in range(n + m)},
            compiler_params=effect,
        )(*[pin(t) for t in srcs], *[pin(t) for t in lands], *after)
        return out[0], out[1], list(out[2:2 + n]), list(out[2 + n:2 + n + m]), out[-1]

    def wait_body(*refs):
        src, land = refs[:n], refs[n:n + m]
        ssem, rsem = refs[n + m], refs[n + m + 1]
        for cp in copies(src, land, (ssem, rsem)):
            cp.wait_send()
            cp.wait_recv()

    out = pl.pallas_call(
        wait_body, name=name, out_shape=tuple(thru),
        in_specs=[ANY] * (n + m) + [sem_spec, sem_spec] + [ANY] * len(after),
        out_specs=tuple([ANY] * (n + m)),
        input_output_aliases={i: i for i in range(n + m)},
        compiler_params=effect,
    )(*srcs, *lands, sems[0], sems[1], *after)
    return list(out[:n]), list(out[n:])


def _cast_into(w, land, layer, kidx, dep):
    _, a, R, C = w.shape
    br = R
    while br * C > 512 * 1024 and br % 32 == 0:
        br //= 2

    def body(k_ref, w_ref, land_ref, dep_ref, o_ref):
        o_ref[...] = w_ref[...].astype(o_ref.dtype)

    return pl.pallas_call(
        body, name="cast_into",
        grid_spec=pltpu.PrefetchScalarGridSpec(
            num_scalar_prefetch=1, grid=(a, R // br),
            in_specs=[pl.BlockSpec((None, None, br, C), lambda e, i, k: (layer, e, i, 0)), ANY, ANY],
            out_specs=pl.BlockSpec((None, br, C), lambda e, i, k: (k[0] * a + e, i, 0))),
        out_shape=jax.ShapeDtypeStruct(land.shape, land.dtype), input_output_aliases={2: 0},
        compiler_params=_cp("arbitrary", "arbitrary"),
    )(kidx, w, land, dep)


class _GatherCopies:
    def __init__(self, n, halves=True):
        self.n, self.count, self.halves = n, 3 * n, halves

    def __call__(self, src, land, sems):
        x, y, c = _me()
        k = 2 * x + y
        cps = []
        for j, (qx, qy) in enumerate(_other_chips(x, y)):
            for i in range(self.n):
                mine = _slab(land[i], k, land[i].shape[0] // NSH, c if self.halves else None)
                cps.append(_rcopy(mine, mine, sems, j * self.n + i, (qx, qy, c)))
        return cps


def _gather_finish(lands):
    n = len(lands)

    def body(*refs):
        dst = refs[n:2 * n]
        sems = refs[2 * n:2 * n + 2]
        x, y, c = _me()
        av = [d.shape[0] // NSH for d in dst]
        cps = []
        for j, (qx, qy) in enumerate(_other_chips(x, y)):
            for i in range(n):
                got = _slab(dst[i], 2 * qx + qy, av[i], c)
                cps.append(_rcopy(got, got, sems, j * n + i, (x, y, 1 - c)))
        for cp in cps:
            cp.start()
        for j, (qx, qy) in enumerate(_other_chips(x, y)):
            for i in range(n):
                other = _slab(dst[i], 2 * qx + qy, av[i], 1 - c)
                _rcopy(other, other, sems, j * n + i, (x, y, c)).wait_recv()
        for cp in cps:
            cp.wait_send()

    return pl.pallas_call(
        body, name="gather_finish", in_specs=[ANY] * n, out_specs=[ANY] * n,
        out_shape=[jax.ShapeDtypeStruct(t.shape, t.dtype) for t in lands],
        input_output_aliases={i: i for i in range(n)},
        scratch_shapes=[pltpu.SemaphoreType.DMA((3 * n,)), pltpu.SemaphoreType.DMA((3 * n,))],
    )(*lands)


class _PairCopies:
    def __init__(self, n):
        self.n, self.count = n, n

    def __call__(self, src, land, sems):
        x, y, c = _me()
        return [_rcopy(_rows(src[i], 1 - c), land[i], sems, i, (x, y, 1 - c)) for i in range(self.n)]


class _ScatterCopies:
    def __init__(self, n):
        self.n, self.count = n, 3 * n

    def __call__(self, src, land, sems):
        x, y, c = _me()
        k = 2 * x + y
        cps = []
        for j, (qx, qy) in enumerate(_other_chips(x, y)):
            for i in range(self.n):
                a = src[i].shape[0] // NSH
                cps.append(_rcopy(_slab(src[i], 2 * qx + qy, a), _slab(land[i], k, a), sems, j * self.n + i,
                                  (qx, qy, c)))
        return cps


def _sum_chips(own, rcv, acc, layer, nlayers, idx):
    A, hr, C = rcv.shape
    a = A // NSH
    br = min(hr, 512)
    nb = hr // br

    def body(*refs):
        r0, r1, r2, r3 = refs[1:5]
        o_ref = refs[-1]
        o_ref[...] = ((r0[...].astype(F32) + r1[...].astype(F32)) + r2[...].astype(F32)) + r3[...].astype(F32)

    slot = lambda s: pl.BlockSpec((None, br, C), lambda e, i, ix: (ix[s] * a + e, i, 0))
    ops = [own, rcv, rcv, rcv]
    in_specs = [slot(0), slot(1), slot(2), slot(3)]
    aliases = {}
    if acc is not None:
        ops.append(acc)
        in_specs.append(ANY)
        aliases = {5: 0}
    return pl.pallas_call(
        body, name="sum_chips",
        grid_spec=pltpu.PrefetchScalarGridSpec(
            num_scalar_prefetch=1, grid=(a, nb), in_specs=in_specs,
            out_specs=pl.BlockSpec((None, None, br, C), lambda e, i, ix: (layer, e, ix[4] * nb + i, 0))),
        out_shape=jax.ShapeDtypeStruct((nlayers, a, 2 * hr, C), F32), input_output_aliases=aliases,
        compiler_params=_cp("arbitrary", "arbitrary"),
    )(idx, *ops)


def _join_halves(fulls):
    n = len(fulls)

    def body(*refs):
        buf = refs[n:2 * n]
        sems = refs[2 * n:2 * n + 2]
        x, y, c = _me()
        cps = [_rcopy(_rows(buf[i], c), _rows(buf[i], c), sems, i, (x, y, 1 - c)) for i in range(n)]
        for cp in cps:
            cp.start()
        for i in range(n):
            _rcopy(_rows(buf[i], 1 - c), _rows(buf[i], 1 - c), sems, i, (x, y, c)).wait_recv()
        for cp in cps:
            cp.wait_send()

    return pl.pallas_call(
        body, name="join_halves", in_specs=[ANY] * n, out_specs=[ANY] * n,
        out_shape=[jax.ShapeDtypeStruct(t.shape, t.dtype) for t in fulls],
        input_output_aliases={i: i for i in range(n)},
        scratch_shapes=[pltpu.SemaphoreType.DMA((n,)), pltpu.SemaphoreType.DMA((n,))],
    )(*fulls)


def _small_blocks(hr):
    br = hr
    while br > 512 and br % 16 == 0:
        br //= 2
    return br, hr // br


def _pair_sum_slot(part, sib, ck):
    R, C = part.shape
    hr = R // 2
    br, nb = _small_blocks(hr)

    def body(ix, p_ref, s_ref, o_ref):
        o_ref[...] = p_ref[...] + s_ref[...]

    return pl.pallas_call(
        body, name="pair_sum_slot",
        grid_spec=pltpu.PrefetchScalarGridSpec(
            num_scalar_prefetch=1, grid=(nb,),
            in_specs=[pl.BlockSpec((br, C), lambda i, ix: (ix[0] * nb + i, 0)),
                      pl.BlockSpec((br, C), lambda i, ix: (i, 0))],
            out_specs=pl.BlockSpec((None, br, C), lambda i, ix: (ix[1], i, 0))),
        out_shape=jax.ShapeDtypeStruct((NSH, hr, C), F32),
        compiler_params=_cp("arbitrary"),
    )(ck, part, sib)


def _sum_slots(slots, ck):
    _, hr, C = slots.shape
    br, nb = _small_blocks(hr)

    def body(ix, s_ref, o_ref):
        o_ref[...] = ((s_ref[0] + s_ref[1]) + s_ref[2]) + s_ref[3]

    return pl.pallas_call(
        body, name="sum_slots",
        grid_spec=pltpu.PrefetchScalarGridSpec(
            num_scalar_prefetch=1, grid=(nb,),
            in_specs=[pl.BlockSpec((NSH, br, C), lambda i, ix: (0, i, 0))],
            out_specs=pl.BlockSpec((br, C), lambda i, ix: (ix[0] * nb + i, 0))),
        out_shape=jax.ShapeDtypeStruct((2 * hr, C), F32),
        compiler_params=_cp("arbitrary"),
    )(ck, slots)


def _adamw(w, g, m, v):
    shape = w.shape
    C = shape[-1]
    R = shape[-2]
    A = 1
    for s in shape[:-2]:
        A *= s
    br = R
    while br * C > 256 * 1024 and br % 16 == 0:
        br //= 2
    c1 = 1.0 / (1.0 - ADAM_B1 ** ADAM_STEP)
    c2 = 1.0 / (1.0 - ADAM_B2 ** ADAM_STEP)

    def body(w_ref, g_ref, m_ref, v_ref, og_ref, d_ref, nm_ref, nv_ref):
        gv = g_ref[...]
        og_ref[...] = gv
        nm = ADAM_B1 * m_ref[...] + (1.0 - ADAM_B1) * gv
        nv = ADAM_B2 * v_ref[...] + (1.0 - ADAM_B2) * (gv * gv)
        nm_ref[...] = nm
        nv_ref[...] = nv
        d_ref[...] = -ADAM_LR * ((nm * c1) / (jnp.sqrt(nv * c2) + ADAM_EPS) + ADAM_WD * w_ref[...])

    blk = pl.BlockSpec((None, br, C), lambda a, i: (a, i, 0))
    outs = pl.pallas_call(
        body, name="adamw", grid=(A, R // br), in_specs=[blk] * 4, out_specs=[blk] * 4,
        out_shape=[jax.ShapeDtypeStruct((A, R, C), F32)] * 4,
        compiler_params=_cp("arbitrary", "arbitrary"),
    )(*(t.reshape(A, R, C) for t in (w, g, m, v)))
    return tuple(o.reshape(shape) for o in outs)


WEIGHTS = ("norm_mix_pre", "norm_mix_post", "norm_ffn_pre", "norm_ffn_post", "w_in", "conv_a_w", "conf_dw_w",
           "conf_dw_b", "conf_ln_g", "conf_ln_b", "sgu_ln_g", "sgu_ln_b", "sgu_ws", "sgu_b", "w_branch", "w_out",
           "w_ff1", "w_ff2")
BIG = ("w_in", "w_branch", "w_out", "w_ff1", "w_ff2")
CONV_ROWS = 48


def kernel(x, norm_mix_pre, norm_mix_post, norm_ffn_pre, norm_ffn_post, w_in, conv_a_w, conf_dw_w, conf_dw_b, conf_ln_g, conf_ln_b, sgu_ln_g, sgu_ln_b, sgu_ws, sgu_b, w_branch, w_out, w_ff1, w_ff2, loss_target, m_norm_mix_pre, m_norm_mix_post, m_norm_ffn_pre, m_norm_ffn_post, m_w_in, m_conv_a_w, m_conf_dw_w, m_conf_dw_b, m_conf_ln_g, m_conf_ln_b, m_sgu_ln_g, m_sgu_ln_b, m_sgu_ws, m_sgu_b, m_w_branch, m_w_out, m_w_ff1, m_w_ff2, v_norm_mix_pre, v_norm_mix_post, v_norm_ffn_pre, v_norm_ffn_post, v_w_in, v_conv_a_w, v_conf_dw_w, v_conf_dw_b, v_conf_ln_g, v_conf_ln_b, v_sgu_ln_g, v_sgu_ln_b, v_sgu_ws, v_sgu_b, v_w_branch, v_w_out, v_w_ff1, v_w_ff2):
    w = dict(norm_mix_pre=norm_mix_pre, norm_mix_post=norm_mix_post, norm_ffn_pre=norm_ffn_pre,
             norm_ffn_post=norm_ffn_post, w_in=w_in, conv_a_w=conv_a_w, conf_dw_w=conf_dw_w, conf_dw_b=conf_dw_b,
             conf_ln_g=conf_ln_g, conf_ln_b=conf_ln_b, sgu_ln_g=sgu_ln_g, sgu_ln_b=sgu_ln_b, sgu_ws=sgu_ws,
             sgu_b=sgu_b, w_branch=w_branch, w_out=w_out, w_ff1=w_ff1, w_ff2=w_ff2)
    mom = dict(norm_mix_pre=m_norm_mix_pre, norm_mix_post=m_norm_mix_post, norm_ffn_pre=m_norm_ffn_pre,
               norm_ffn_post=m_norm_ffn_post, w_in=m_w_in, conv_a_w=m_conv_a_w, conf_dw_w=m_conf_dw_w,
               conf_dw_b=m_conf_dw_b, conf_ln_g=m_conf_ln_g, conf_ln_b=m_conf_ln_b, sgu_ln_g=m_sgu_ln_g,
               sgu_ln_b=m_sgu_ln_b, sgu_ws=m_sgu_ws, sgu_b=m_sgu_b, w_branch=m_w_branch, w_out=m_w_out,
               w_ff1=m_w_ff1, w_ff2=m_w_ff2)
    var = dict(norm_mix_pre=v_norm_mix_pre, norm_mix_post=v_norm_mix_post, norm_ffn_pre=v_norm_ffn_pre,
               norm_ffn_post=v_norm_ffn_post, w_in=v_w_in, conv_a_w=v_conv_a_w, conf_dw_w=v_conf_dw_w,
               conf_dw_b=v_conf_dw_b, conf_ln_g=v_conf_ln_g, conf_ln_b=v_conf_ln_b, sgu_ln_g=v_sgu_ln_g,
               sgu_ln_b=v_sgu_ln_b, sgu_ws=v_sgu_ws, sgu_b=v_sgu_b, w_branch=v_w_branch, w_out=v_w_out,
               w_ff1=v_w_ff1, w_ff2=v_w_ff2)
    L = w_in.shape[0]
    nseq, S, _ = x.shape
    T = nseq * S
    rk = D // NSH
    mx, my, mc = _me()
    k_chip = 2 * mx + my

    big_src = [w_in.reshape(L, 1, D, w_in.shape[2]), w_branch, w_out.reshape(L, 1, rk, D),
               w_ff1.reshape(L, 1, D, w_ff1.shape[2]), w_ff2.reshape(L, 1, w_ff2.shape[1], D)]
    kidx = jnp.reshape(k_chip, (1,)).astype(jnp.int32)
    conv_src = jnp.concatenate(
        [jnp.pad(conv_a_w, ((0, 0), (0, SUBLANES - KA), (0, 0))), jnp.pad(conf_dw_w, ((0, 0), (0, 1), (0, 0))),
         jnp.zeros((L, CONV_ROWS - SUBLANES - KC - 1, rk), F32)], axis=1)[None]

    def early_params(l, g_in, conv_full):
        return dict(
            g_mix_pre=norm_mix_pre[l][None], g_mix_post=norm_mix_post[l][None], g_ffn_pre=norm_ffn_pre[l][None],
            g_ffn_post=norm_ffn_post[l][None], w_in=g_in, conv_a_w=conv_full[l, :KA],
            conf_dw_w=conv_full[l, SUBLANES:SUBLANES + KC], conf_dw_b=conf_dw_b[l][None],
            conf_ln_g=conf_ln_g[l][None], conf_ln_b=conf_ln_b[l][None], sgu_ln_g=sgu_ln_g[l][None],
            sgu_ln_b=sgu_ln_b[l][None], sgu_ws=sgu_ws[l], sgu_wst=jnp.swapaxes(sgu_ws[l], 1, 2),
            sgu_bt=sgu_b[l].T)

    def late_params(gathered):
        g_br, g_out, g_ff1, g_ff2 = gathered
        return dict(w_branch=g_br.reshape(NSH, 3, rk, D), w_out=g_out.reshape(D, D), w_ff1=g_ff1,
                    w_ff2=g_ff2.reshape(NSH * w_ff2.shape[1], D))

    def cast_lands(srcs, l, dep):
        return [_cast_into(s, lax.empty((NSH * s.shape[1],) + s.shape[2:], F32 if s is conv_src else BF), l, kidx,
                           dep) for s in srcs]

    def gather_start(name, lands, after):
        return _split_call(name, _GatherCopies(len(lands)), [], lands, after=after)

    def gather_land(name, flight, after):
        ssem, rsem, _, lands, _ = flight
        _, lands = _split_call(name, _GatherCopies(len(lands)), [], lands, (ssem, rsem), after)
        return _gather_finish(lands)

    zero_tok = jnp.zeros((8, 128), F32)
    xt = x.reshape(T, D)
    layers, saved = [], []
    head = gather_start("gather_start_0a", cast_lands([big_src[0], conv_src], 0, kidx), [])
    tails = [cast_lands(big_src[1:], l, head[4]) for l in range(L)]
    heads = [None] + [cast_lands(big_src[:1], l, head[4]) for l in range(1, L)]
    behind = [xt] + [t for ls in tails + heads[1:] for t in ls]
    conv_full = None
    for l in range(L):
        got = gather_land(f"gather_wait_{l}a", head, behind if l == 0 else [xt])
        g_in = got[0]
        if l == 0:
            conv_full = got[1].reshape(NSH, L, CONV_ROWS, rk).transpose(1, 2, 0, 3).reshape(L, CONV_ROWS, D)
        tail = gather_start(f"gather_start_{l}b", tails[l], [g_in])
        nxt = {}

        def late(after, l=l, tail=tail, nxt=nxt):
            more = late_params(gather_land(f"gather_wait_{l}b", tail, [after]))
            if l + 1 == L:
                return more, zero_tok
            nxt["head"] = gather_start(f"gather_start_{l + 1}a", heads[l + 1], [more["w_ff1"]])
            return more, nxt["head"][4]

        p = early_params(l, g_in, conv_full)
        xt, sv = _layer_fwd(xt, p, S, tail[4], late)
        head = nxt.get("head")
        layers.append(p)
        saved.append(sv)
    dx, loss_row = _loss_head(xt, loss_target.reshape(T, D))
    loss = lax.psum(loss_row[0, 0], ("x", "y", "c"))

    c_arr = jnp.reshape(mc, (1,)).astype(jnp.int32)
    idx = jnp.stack([k_chip, k_chip ^ 2, k_chip ^ 1, k_chip ^ 3, mc]).astype(jnp.int32)
    fulls = {n: None for n in BIG}
    smalls = [None] * L

    def pair_start(tag, parts):
        lands = [lax.empty((p.shape[0], p.shape[1] // 2, p.shape[2]), p.dtype) for p in parts]
        return _split_call(f"pair_start_{tag}", _PairCopies(len(parts)), parts, lands)

    def pair_land_scatter_start(tag, fl, after):
        ssem, rsem, parts, sib, _ = fl
        parts, sib = _split_call(f"pair_wait_{tag}", _PairCopies(len(parts)), parts, sib, (ssem, rsem), after)
        sums = _pair_add(parts, sib, c_arr)
        rcv = [lax.empty(s.shape, s.dtype) for s in sums]
        return _split_call(f"scatter_start_{tag}", _ScatterCopies(len(sums)), sums, rcv)

    def scatter_land(tag, fl, names, l, after):
        ssem, rsem, sums, rcv, _ = fl
        sums, rcv = _split_call(f"scatter_wait_{tag}", _ScatterCopies(len(sums)), sums, rcv, (ssem, rsem), after)
        for n, o, r in zip(names, sums, rcv):
            fulls[n] = _sum_chips(o, r, fulls[n], l, L, idx)

    pending = []
    pair_b = None
    dep = zero_tok
    for l in reversed(range(L)):
        mine = {}

        def after_ffn(arr, l=l, mine=mine, pair_b=pair_b):
            if pair_b is None:
                return zero_tok
            mine["prev_b"] = pair_land_scatter_start(f"{l + 1}b", pair_b, [arr])
            return mine["prev_b"][4]

        def early(parts, l=l, mine=mine):
            br, rest = parts[0], parts[1:]
            mine["pair_a"] = pair_start(f"{l}a", [br.reshape(NSH * 3, rk, D), *rest])
            return mine["pair_a"][4]

        def mid(arr, l=l, mine=mine):
            mine["a"] = pair_land_scatter_start(f"{l}a", mine["pair_a"], [arr])
            return mine["a"][4]

        dx, big, small = _layer_bwd(dx, layers[l], saved[l], S, dep, (after_ffn, early, mid))
        smalls[l] = _pack_small(small)
        for args in pending:
            scatter_land(*args, [dx])
        pending = [(f"{l}a", mine["a"], BIG[1:], l)]
        if "prev_b" in mine:
            pending.append((f"{l + 1}b", mine["prev_b"], BIG[:1], l + 1))
        pair_b = pair_start(f"{l}b", [big["w_in"]])
        dep = pair_b[4]
    last_b = ("0b", pair_land_scatter_start("0b", pair_b, [dx]), BIG[:1], 0)

    packed = jnp.concatenate(smalls, axis=0)
    nrow = packed.shape[0]
    ck = jnp.stack([mc, k_chip]).astype(jnp.int32)
    (sib,) = _send_halves_to_sibling([packed.reshape(1, nrow, D)])
    slots = _pair_sum_slot(packed, sib.reshape(nrow // 2, D), ck)
    small_flight = _split_call("small_start", _GatherCopies(1, halves=False), [], [slots])

    for args in pending:
        scatter_land(*args, [small_flight[4], last_b[1][4]])
    grads, delta, new_m, new_v = {}, {}, {}, {}
    for n, f in zip(BIG[1:], _join_halves([fulls[n] for n in BIG[1:]])):
        grads[n], delta[n], new_m[n], new_v[n] = _adamw(w[n], f.reshape(w[n].shape), mom[n], var[n])
    scatter_land(*last_b, [delta[BIG[-1]]])
    (f,) = _join_halves([fulls[BIG[0]]])
    n = BIG[0]
    grads[n], delta[n], new_m[n], new_v[n] = _adamw(w[n], f.reshape(w[n].shape), mom[n], var[n])

    _, (slots,) = _split_call("small_wait", _GatherCopies(1, halves=False), [], small_flight[3],
                              (small_flight[0], small_flight[1]), [delta[BIG[0]]])
    (small_sum,) = _join_halves([_sum_slots(slots, ck).reshape(1, 1, nrow, D)])
    shapes = {n: (w[n].shape[1:] if n not in ("conv_a_w", "conf_dw_w") else (w[n].shape[1], D)) for n in SMALL_NAMES}
    sg = _unpack_small(small_sum.reshape(L, PACK_ROWS, D), shapes)
    for n in SMALL_NAMES:
        if n in ("conv_a_w", "conf_dw_w"):
            grads[n] = lax.dynamic_slice_in_dim(sg[n], k_chip * rk, rk, axis=2)
        else:
            grads[n] = sg[n]

    for n in SMALL_NAMES:
        sh = w[n].shape
        flat = (sh[0] * sh[1], sh[2]) if n in ("conv_a_w", "conf_dw_w") else (-1, D)
        g, d, nm, nv = _adamw(*(t.reshape(flat) for t in (w[n], grads[n], mom[n], var[n])))
        grads[n], delta[n], new_m[n], new_v[n] = g.reshape(sh), d.reshape(sh), nm.reshape(sh), nv.reshape(sh)

    return (loss, dx.reshape(x.shape), *[grads[n] for n in WEIGHTS], *[delta[n] for n in WEIGHTS],
            *[new_m[n] for n in WEIGHTS], *[new_v[n] for n in WEIGHTS])
```

```python
import functools

import jax
import jax.numpy as jnp
from jax import lax
from jax.experimental import pallas as pl
from jax.experimental.pallas import tpu as pltpu

D = 1024
HEADS = 8
CHUNK = 128
KA = 3
KC = 31
HALO = 32
SUBLANES = 8
MIX_TILE = 512
WGRAD_TILE = 1024
NSH = 4
NDEV = 8
EPS = 1e-6
BF = jnp.bfloat16
F32 = jnp.float32
VMEM_LIMIT = 56 * 1024 * 1024

ADAM_LR = 0.001
ADAM_B1 = 0.9
ADAM_B2 = 0.999
ADAM_EPS = 1e-08
ADAM_WD = 0.01
ADAM_STEP = 10

MESH = pl.DeviceIdType.MESH
ANY = pl.BlockSpec(memory_space=pl.ANY)


def _cp(*sem):
    return pltpu.CompilerParams(dimension_semantics=sem, vmem_limit_bytes=VMEM_LIMIT)


def _sig(x):
    return 1.0 / (1.0 + jnp.exp(-x))


_GC = 0.7978845608028654


def _gelu(x):
    x2 = x * x
    t = jnp.tanh(_GC * x * (1.0 + 0.044715 * x2))
    y = 0.5 * x * (1.0 + t)
    dy = 0.5 * (1.0 + t) + 0.5 * x * (1.0 - t * t) * _GC * (1.0 + 3.0 * 0.044715 * x2)
    return y, dy


def _rms_fwd(x, g):
    r = lax.rsqrt(jnp.mean(x * x, axis=-1, keepdims=True) + EPS)
    return x * r * g


def _rms_bwd(dy, x, g):
    r = lax.rsqrt(jnp.mean(x * x, axis=-1, keepdims=True) + EPS)
    xn = x * r
    dyg = dy * g
    dx = r * (dyg - xn * jnp.mean(dyg * xn, axis=-1, keepdims=True))
    return dx, jnp.sum(dy * xn, axis=0, keepdims=True)


def _ln_stats(x):
    mu = jnp.mean(x, axis=-1, keepdims=True)
    xc = x - mu
    r = lax.rsqrt(jnp.mean(xc * xc, axis=-1, keepdims=True) + EPS)
    return xc * r, r


def _ln_bwd(dn, n, r):
    return r * (dn - jnp.mean(dn, axis=-1, keepdims=True) - n * jnp.mean(dn * n, axis=-1, keepdims=True))


def _dot(a, b):
    return jnp.dot(a, b, preferred_element_type=F32)


def _dot_nt(a, b):
    return lax.dot_general(a, b, (((1,), (1,)), ((), ())), preferred_element_type=F32)


def _dot_tn(a, b):
    return lax.dot_general(a, b, (((0,), (0,)), ((), ())), preferred_element_type=F32)


def _in_proj(x, g, w, dep):
    T = x.shape[0]
    nc = w.shape[2]
    tm = min(T, 1024)
    tn = nc
    nj = nc // tn

    def body(x_ref, g_ref, w_ref, dep_ref, h_ref, z_ref, h_scr):
        @pl.when((pl.program_id(1) == 0) & (pl.program_id(2) == 0))
        def _():
            h = _rms_fwd(x_ref[...], g_ref[...]).astype(BF)
            h_scr[...] = h
            h_ref[...] = h
        z_ref[...] = _dot(h_scr[...], w_ref[...]).astype(BF)

    return pl.pallas_call(
        body, name="in_proj", grid=(T // tm, NSH, nj),
        in_specs=[pl.BlockSpec((tm, D), lambda i, k, j: (i, 0)),
                  pl.BlockSpec((1, D), lambda i, k, j: (0, 0)),
                  pl.BlockSpec((None, D, tn), lambda i, k, j: (k, 0, j)), ANY],
        out_specs=[pl.BlockSpec((tm, D), lambda i, k, j: (i, 0)),
                   pl.BlockSpec((tm, tn), lambda i, k, j: (i, k * nj + j))],
        out_shape=[jax.ShapeDtypeStruct((T, D), BF), jax.ShapeDtypeStruct((T, NSH * nc), BF)],
        scratch_shapes=[pltpu.VMEM((tm, D), BF)],
        compiler_params=_cp("arbitrary", "arbitrary", "arbitrary"),
    )(x, g, w, dep)


def _tile_specs(tt, nt_total, reverse):
    def tile(i):
        return (nt_total - 1 - i) if reverse else i

    def cur(c):
        return pl.BlockSpec((tt, D), lambda i, *_: (tile(i), c))

    def halo(c):
        return pl.BlockSpec((HALO, D), lambda i, *_: (jnp.maximum(tile(i) * (tt // HALO) - 1, 0), c))

    def row(r=1):
        return pl.BlockSpec((r, D), lambda i, *_: (0, 0))

    return tile, cur, halo, row


RC = 16


def _chunks(tt, fn, group=2):
    def step(c, carry):
        for u in range(group):
            fn(pl.multiple_of((c * group + u) * RC, RC))
        return carry
    lax.fori_loop(0, tt // (RC * group), step, 0)


def _chunk_pairs(tt, fn):
    def step(c, carry):
        fn(pl.multiple_of(c * 2 * RC, RC), pl.multiple_of(c * 2 * RC + RC, RC))
        return carry
    lax.fori_loop(0, tt // (2 * RC), step, 0)


ALL_SHIFTS = tuple(range(SUBLANES))


def _shifts_of(offs):
    return tuple(sorted({o % SUBLANES for o in offs}))


def _shifted_copies(ext, sh, nrows, shifts=ALL_SHIFTS):
    for i, s in enumerate(shifts):
        sh[i] = ext[pl.ds(s, nrows), :]


def _window(sh, o, r0, shifts=ALL_SHIFTS):
    return sh[shifts.index(o % SUBLANES), pl.ds(r0 + (o // SUBLANES) * SUBLANES, RC), :]


def _fill_taps(wb, w_ref, ntap):
    for k in range(ntap):
        wb[k * SUBLANES:(k + 1) * SUBLANES, :] = jnp.broadcast_to(w_ref[k:k + 1, :], (SUBLANES, D))


def _conv_chunks(sh, wb, offs, r0s, shifts=ALL_SHIFTS):
    accs = []
    for r0 in r0s:
        acc = None
        for k, o in enumerate(offs):
            wk = wb[k * SUBLANES:(k + 1) * SUBLANES, :]
            term = jnp.concatenate([wk] * (RC // SUBLANES), axis=0) * _window(sh, o, r0, shifts)
            acc = term if acc is None else acc + term
        accs.append(acc)
    return accs


WG_TAPS = 5


def _conv_wgrad_chunked(dw_ref, d_ref, sh, offs, tt, shifts=ALL_SHIFTS):
    for g0 in range(0, len(offs), WG_TAPS):
        grp = offs[g0:g0 + WG_TAPS]

        def step(c, accs, grp=grp):
            for u in range(2):
                r0 = pl.multiple_of((2 * c + u) * SUBLANES, SUBLANES)
                d = d_ref[pl.ds(r0, SUBLANES), :]
                accs = tuple(
                    a + d * sh[shifts.index(o % SUBLANES), pl.ds(r0 + (o // SUBLANES) * SUBLANES, SUBLANES), :]
                    for a, o in zip(accs, grp))
            return accs
        accs = lax.fori_loop(0, tt // (2 * SUBLANES), step,
                             tuple(jnp.zeros((SUBLANES, D), F32) for _ in grp))
        for j, a in enumerate(accs):
            dw_ref[g0 + j:g0 + j + 1, :] += jnp.sum(a, axis=0, keepdims=True)


def _causal_offsets(ntap):
    return [HALO - (ntap - 1) + k for k in range(ntap)]


def _anticausal_offsets(ntap):
    return [ntap - 1 - k for k in range(ntap)]


def _mix_a_fwd(z, wa, S):
    T = z.shape[0]
    tt = min(S, MIX_TILE)
    nt = S // tt
    _, cur, halo, row = _tile_specs(tt, T // tt, False)

    nrows = HALO + tt
    offs = _causal_offsets(KA)
    shifts = _shifts_of(offs)

    def body(ah, ab, ac, ah_h, ac_h, w_ref, y_ref, ext, sh, wb):
        @pl.when(pl.program_id(0) == 0)
        def _():
            _fill_taps(wb, w_ref, KA)
            ext[nrows:, :] = jnp.zeros((SUBLANES, D), F32)

        first = (pl.program_id(0) % nt) == 0
        ph = ah_h[...].astype(F32) * ac_h[...].astype(F32)
        ext[0:HALO, :] = jnp.where(first, 0.0, ph)

        def prod(r0):
            rows = pl.ds(r0, RC)
            ext[pl.ds(HALO + r0, RC), :] = ah[rows, :].astype(F32) * ac[rows, :].astype(F32)
        _chunks(tt, prod)
        _shifted_copies(ext, sh, nrows, shifts)

        def conv(*r0s):
            for r0, q in zip(r0s, _conv_chunks(sh, wb, offs, r0s, shifts)):
                rows = pl.ds(r0, RC)
                y_ref[rows, :] = (ab[rows, :].astype(F32) * q).astype(BF)
        _chunk_pairs(tt, conv)

    return pl.pallas_call(
        body, name="mix_a_fwd", grid=(T // tt,),
        in_specs=[cur(0), cur(1), cur(2), halo(0), halo(2), row(KA)],
        out_specs=pl.BlockSpec((tt, D), lambda i: (i, 0)),
        out_shape=jax.ShapeDtypeStruct((T, D), BF),
        scratch_shapes=[pltpu.VMEM((nrows + SUBLANES, D), F32), pltpu.VMEM((len(shifts), nrows, D), F32),
                        pltpu.VMEM((KA * SUBLANES, D), F32)],
        compiler_params=_cp("arbitrary"),
    )(z, z, z, z, z, wa)


def _mix_b_fwd(z, wc, bc, lg, lb, S):
    T = z.shape[0]
    tt = min(S, MIX_TILE)
    nt = S // tt
    _, cur, halo, row = _tile_specs(tt, T // tt, False)

    nrows = HALO + tt
    offs = _causal_offsets(KC)

    def body(ca, cg, ca_h, cg_h, w_ref, bc_ref, lg_ref, lb_ref, y_ref, s_ref, ext, sh, wb):
        @pl.when(pl.program_id(0) == 0)
        def _():
            _fill_taps(wb, w_ref, KC)
            ext[nrows:, :] = jnp.zeros((SUBLANES, D), F32)

        first = (pl.program_id(0) % nt) == 0
        rh = ca_h[...].astype(F32) * _sig(cg_h[...].astype(F32))
        ext[0:HALO, :] = jnp.where(first, 0.0, rh)

        def glu(r0):
            rows = pl.ds(r0, RC)
            ext[pl.ds(HALO + r0, RC), :] = ca[rows, :].astype(F32) * _sig(cg[rows, :].astype(F32))
        _chunks(tt, glu)
        _shifted_copies(ext, sh, nrows)

        def conv(*r0s):
            for r0, q in zip(r0s, _conv_chunks(sh, wb, offs, r0s)):
                rows = pl.ds(r0, RC)
                s = q + bc_ref[...]
                s_ref[rows, :] = s.astype(BF)
                n, _ = _ln_stats(s)
                t = n * lg_ref[...] + lb_ref[...]
                y_ref[rows, :] = (t * _sig(t)).astype(BF)
        _chunk_pairs(tt, conv)

    return pl.pallas_call(
        body, name="mix_b_fwd", grid=(T // tt,),
        in_specs=[cur(3), cur(4), halo(3), halo(4), row(KC), row(), row(), row()],
        out_specs=[pl.BlockSpec((tt, D), lambda i: (i, 0))] * 2,
        out_shape=[jax.ShapeDtypeStruct((T, D), BF)] * 2,
        scratch_shapes=[pltpu.VMEM((nrows + SUBLANES, D), F32), pltpu.VMEM((SUBLANES, nrows, D), F32),
                        pltpu.VMEM((KC * SUBLANES, D), F32)],
        compiler_params=_cp("arbitrary"),
    )(z, z, z, z, wc, bc, lg, lb)


def _causal_mask(transposed):
    r = lax.broadcasted_iota(jnp.int32, (CHUNK, CHUNK), 0)
    c = lax.broadcasted_iota(jnp.int32, (CHUNK, CHUNK), 1)
    return (c >= r) if transposed else (r >= c)


def _mix_s_fwd(z, lg, lb, ws, bst, S):
    T = z.shape[0]
    tt = min(S, MIX_TILE)
    _, cur, _, row = _tile_specs(tt, T // tt, False)

    def body(su, sv, lg_ref, lb_ref, ws_ref, bst_ref, y_ref, u_scr, vn_scr):
        u_scr[...] = _gelu(su[...].astype(F32))[0]
        n, _ = _ln_stats(_gelu(sv[...].astype(F32))[0])
        vn_scr[...] = (n * lg_ref[...] + lb_ref[...]).astype(BF)
        mask = _causal_mask(False)
        for h in range(HEADS):
            wm = jnp.where(mask, ws_ref[h], 0.0).astype(BF)
            cols = slice(h * CHUNK, (h + 1) * CHUNK)
            for c in range(tt // CHUNK):
                rows = slice(c * CHUNK, (c + 1) * CHUNK)
                mixed = _dot(wm, vn_scr[rows, cols]) + bst_ref[:, h:h + 1]
                y_ref[rows, cols] = (u_scr[rows, cols] * mixed).astype(BF)

    return pl.pallas_call(
        body, name="mix_s_fwd", grid=(T // tt,),
        in_specs=[cur(5), cur(6), row(), row(),
                  pl.BlockSpec((HEADS, CHUNK, CHUNK), lambda i: (0, 0, 0)),
                  pl.BlockSpec((CHUNK, HEADS), lambda i: (0, 0))],
        out_specs=pl.BlockSpec((tt, D), lambda i: (i, 0)),
        out_shape=jax.ShapeDtypeStruct((T, D), BF),
        scratch_shapes=[pltpu.VMEM((tt, D), F32), pltpu.VMEM((tt, D), BF)],
        compiler_params=_cp("arbitrary"),
    )(z, z, lg, lb, ws, bst)


def _mix_out_fwd(ya, yc, ys, z, x, wb, wo, gp, dep):
    T = x.shape[0]
    tm = min(T, 512)
    rk = D // NSH

    def body(ya_ref, yc_ref, ys_ref, ga, gc, gs, x_ref, wb_ref, wo_ref, gp_ref, dep_ref,
             p_ref, mg_ref, m_ref, x1_ref):
        acc = None
        for b, (y_ref, g_ref) in enumerate(((ya_ref, ga), (yc_ref, gc), (ys_ref, gs))):
            pb = None
            for k in range(NSH):
                part = _dot(y_ref[:, k * rk:(k + 1) * rk], wb_ref[k, b])
                pb = part if pb is None else pb + part
            p_ref[b] = pb.astype(BF)
            term = _sig(g_ref[...].astype(F32)) * pb
            acc = term if acc is None else acc + term
        mg = acc.astype(BF)
        mg_ref[...] = mg
        m = _dot(mg, wo_ref[...])
        m_ref[...] = m.astype(BF)
        x1_ref[...] = x_ref[...] + _rms_fwd(m, gp_ref[...])

    rowblk = pl.BlockSpec((tm, D), lambda i: (i, 0))
    return pl.pallas_call(
        body, name="mix_out_fwd", grid=(T // tm,),
        in_specs=[rowblk, rowblk, rowblk,
                  pl.BlockSpec((tm, D), lambda i: (i, 7)), pl.BlockSpec((tm, D), lambda i: (i, 8)),
                  pl.BlockSpec((tm, D), lambda i: (i, 9)), rowblk,
                  pl.BlockSpec((NSH, 3, rk, D), lambda i: (0, 0, 0, 0)),
                  pl.BlockSpec((D, D), lambda i: (0, 0)),
                  pl.BlockSpec((1, D), lambda i: (0, 0)), ANY],
        out_specs=[pl.BlockSpec((3, tm, D), lambda i: (0, i, 0)), rowblk, rowblk, rowblk],
        out_shape=[jax.ShapeDtypeStruct((3, T, D), BF), jax.ShapeDtypeStruct((T, D), BF),
                   jax.ShapeDtypeStruct((T, D), BF), jax.ShapeDtypeStruct((T, D), F32)],
        compiler_params=_cp("arbitrary"),
    )(ya, yc, ys, z, z, z, x, wb, wo, gp, dep)


def _ffn_fwd(x1, g3, w1, w2, g4):
    T = x1.shape[0]
    tm = min(T, 1024)

    def body(x_ref, g3_ref, w1_ref, w2_ref, g4_ref, h_ref, a_ref, f_ref, x2_ref, h_scr, acc):
        k = pl.program_id(1)

        @pl.when(k == 0)
        def _():
            h = _rms_fwd(x_ref[...], g3_ref[...]).astype(BF)
            h_scr[...] = h
            h_ref[...] = h
            acc[...] = jnp.zeros_like(acc)

        a = _dot(h_scr[...], w1_ref[...])
        a_ref[...] = a.astype(BF)
        r = jnp.maximum(a, 0.0)
        acc[...] += _dot((r * r).astype(BF), w2_ref[...])

        @pl.when(k == NSH - 1)
        def _():
            f = acc[...]
            f_ref[...] = f.astype(BF)
            x2_ref[...] = x_ref[...] + _rms_fwd(f, g4_ref[...])

    rowblk = pl.BlockSpec((tm, D), lambda i, k: (i, 0))
    vec = pl.BlockSpec((1, D), lambda i, k: (0, 0))
    return pl.pallas_call(
        body, name="ffn_fwd", grid=(T // tm, NSH),
        in_specs=[rowblk, vec, pl.BlockSpec((None, D, D), lambda i, k: (k, 0, 0)),
                  pl.BlockSpec((D, D), lambda i, k: (k, 0)), vec],
        out_specs=[rowblk, pl.BlockSpec((tm, D), lambda i, k: (i, k)), rowblk, rowblk],
        out_shape=[jax.ShapeDtypeStruct((T, D), BF), jax.ShapeDtypeStruct((T, NSH * D), BF),
                   jax.ShapeDtypeStruct((T, D), BF), jax.ShapeDtypeStruct((T, D), F32)],
        scratch_shapes=[pltpu.VMEM((tm, D), BF), pltpu.VMEM((tm, D), F32)],
        compiler_params=_cp("arbitrary", "arbitrary"),
    )(x1, g3, w1, w2, g4)


def _loss_head(y, target):
    T = y.shape[0]
    tm = min(T, 512)

    def body(y_ref, t_ref, dy_ref, l_ref):
        @pl.when(pl.program_id(0) == 0)
        def _():
            l_ref[...] = jnp.zeros_like(l_ref)
        e = y_ref[...] - t_ref[...]
        dy_ref[...] = e * (1.0 / D)
        l_ref[...] += jnp.sum(e * e) * (0.5 / D)

    rowblk = pl.BlockSpec((tm, D), lambda i: (i, 0))
    return pl.pallas_call(
        body, name="loss_head", grid=(T // tm,),
        in_specs=[rowblk, rowblk],
        out_specs=[rowblk, pl.BlockSpec((1, 128), lambda i: (0, 0))],
        out_shape=[jax.ShapeDtypeStruct((T, D), F32), jax.ShapeDtypeStruct((1, 128), F32)],
        compiler_params=_cp("arbitrary"),
    )(y, target)


def _ffn_bwd(dx2, f, g4, a, w2, w1, x1, g3, dep):
    T = dx2.shape[0]
    tm = min(T, 1024)
    tf = 512
    per = w1.shape[2] // tf
    nk = NSH * per

    def body(dx2_ref, f_ref, g4_ref, a_ref, w2_ref, w1_ref, x1_ref, g3_ref, dep_ref,
             df_ref, da_ref, dx1_ref, dg4_ref, dg3_ref, df_scr, acc):
        i, k = pl.program_id(0), pl.program_id(1)

        @pl.when((i == 0) & (k == 0))
        def _():
            dg4_ref[...] = jnp.zeros_like(dg4_ref)
            dg3_ref[...] = jnp.zeros_like(dg3_ref)

        @pl.when(k == 0)
        def _():
            df, dg = _rms_bwd(dx2_ref[...], f_ref[...].astype(F32), g4_ref[...])
            dg4_ref[...] += dg
            dfb = df.astype(BF)
            df_scr[...] = dfb
            df_ref[...] = dfb
            acc[...] = jnp.zeros_like(acc)

        av = a_ref[...].astype(F32)
        da = (_dot_nt(df_scr[...], w2_ref[...]) * (2.0 * jnp.maximum(av, 0.0))).astype(BF)
        da_ref[...] = da
        acc[...] += _dot_nt(da, w1_ref[...])

        @pl.when(k == nk - 1)
        def _():
            dx, dg = _rms_bwd(acc[...], x1_ref[...], g3_ref[...])
            dg3_ref[...] += dg
            dx1_ref[...] = dx2_ref[...] + dx

    rowblk = pl.BlockSpec((tm, D), lambda i, k: (i, 0))
    vec = pl.BlockSpec((1, D), lambda i, k: (0, 0))
    return pl.pallas_call(
        body, name="ffn_bwd", grid=(T // tm, nk),
        in_specs=[rowblk, rowblk, vec, pl.BlockSpec((tm, tf), lambda i, k: (i, k)),
                  pl.BlockSpec((tf, D), lambda i, k: (k, 0)),
                  pl.BlockSpec((None, D, tf), lambda i, k: (k // per, 0, k % per)), rowblk, vec, ANY],
        out_specs=[rowblk, pl.BlockSpec((tm, tf), lambda i, k: (i, k)), rowblk, vec, vec],
        out_shape=[jax.ShapeDtypeStruct((T, D), BF), jax.ShapeDtypeStruct((T, NSH * D), BF),
                   jax.ShapeDtypeStruct((T, D), F32), jax.ShapeDtypeStruct((1, D), F32),
                   jax.ShapeDtypeStruct((1, D), F32)],
        scratch_shapes=[pltpu.VMEM((tm, D), BF), pltpu.VMEM((tm, D), F32)],
        compiler_params=_cp("arbitrary", "arbitrary"),
    )(dx2, f, g4, a, w2, w1, x1, g3, dep)


def _wgrad(name, ops, grid, in_specs, out_spec, out_shape, acc_shape, pick=None, relu2=False):
    nt = grid[-1]
    na = len(ops) - 1

    def body(*refs):
        a_refs, b_ref, o_ref, acc = refs[:na], refs[na], refs[na + 1], refs[na + 2]
        t = pl.program_id(len(grid) - 1)

        @pl.when(t == 0)
        def _():
            acc[...] = jnp.zeros_like(acc)

        def add(a_ref):
            av = a_ref[...]
            if relu2:
                r = jnp.maximum(av.astype(F32), 0.0)
                av = (r * r).astype(BF)
            acc[...] += _dot_tn(av, b_ref[...])

        if na == 1:
            add(a_refs[0])
        else:
            sel = pick()
            for n in range(na):
                pl.when(sel == n)(functools.partial(add, a_refs[n]))

        @pl.when(t == nt - 1)
        def _():
            if len(o_ref.shape) == 2:
                o_ref[...] = acc[...].astype(o_ref.dtype)
            else:
                rs = o_ref.shape[1]
                for q in range(o_ref.shape[0]):
                    o_ref[q] = acc[q * rs:(q + 1) * rs, :].astype(o_ref.dtype)

    return pl.pallas_call(
        body, name=name, grid=grid, in_specs=in_specs, out_specs=out_spec, out_shape=out_shape,
        scratch_shapes=[pltpu.VMEM(acc_shape, F32)],
        compiler_params=_cp(*(["arbitrary"] * len(grid))),
    )(*ops)


def _mix_out_bwd(dx1, m, gp, wo, p3, z, wb, dep):
    T = dx1.shape[0]
    tm = min(T, 512)
    rk = D // NSH

    def body(dx1_ref, m_ref, gp_ref, wo_ref, p_ref, g_ref, wb_ref, dep_ref,
             dm_ref, dp_ref, dy_ref, dz_ref, dgp_ref, dmg):
        i, b = pl.program_id(0), pl.program_id(1)

        @pl.when((i == 0) & (b == 0))
        def _():
            dgp_ref[...] = jnp.zeros_like(dgp_ref)

        @pl.when(b == 0)
        def _():
            dm, dg = _rms_bwd(dx1_ref[...], m_ref[...].astype(F32), gp_ref[...])
            dgp_ref[...] += dg
            dmb = dm.astype(BF)
            dm_ref[...] = dmb
            dmg[...] = _dot_nt(dmb, wo_ref[...])

        gate = _sig(g_ref[...].astype(F32))
        d = dmg[...]
        dp = (d * gate).astype(BF)
        dp_ref[...] = dp
        dz_ref[...] = (d * p_ref[...].astype(F32) * gate * (1.0 - gate)).astype(BF)
        for k in range(NSH):
            dy_ref[:, k * rk:(k + 1) * rk] = _dot_nt(dp, wb_ref[k, b]).astype(BF)

    rowblk = pl.BlockSpec((tm, D), lambda i, b: (i, 0))
    br = pl.BlockSpec((None, tm, D), lambda i, b: (b, i, 0))
    vec = pl.BlockSpec((1, D), lambda i, b: (0, 0))
    return pl.pallas_call(
        body, name="mix_out_bwd", grid=(T // tm, 3),
        in_specs=[rowblk, rowblk, vec, pl.BlockSpec((D, D), lambda i, b: (0, 0)), br,
                  pl.BlockSpec((tm, D), lambda i, b: (i, 7 + b)),
                  pl.BlockSpec((NSH, 3, rk, D), lambda i, b: (0, 0, 0, 0)), ANY],
        out_specs=[rowblk, br, br, pl.BlockSpec((tm, D), lambda i, b: (i, 7 + b)), vec],
        out_shape=[jax.ShapeDtypeStruct((T, D), BF), jax.ShapeDtypeStruct((3, T, D), BF),
                   jax.ShapeDtypeStruct((3, T, D), BF), jax.ShapeDtypeStruct((T, 10 * D), BF),
                   jax.ShapeDtypeStruct((1, D), F32)],
        scratch_shapes=[pltpu.VMEM((tm, D), F32)],
        compiler_params=_cp("arbitrary", "arbitrary"),
    )(dx1, m, gp, wo, p3, z, wb, dep)


def _mix_a_bwd(dz, dy3, z, wa, S, dep):
    T = z.shape[0]
    tt = min(S, MIX_TILE)
    nt = S // tt
    ntt = T // tt
    tile, cur, halo, row = _tile_specs(tt, ntt, True)

    nrows = HALO + tt
    coffs, aoffs = _causal_offsets(KA), _anticausal_offsets(KA)
    cshifts, ashifts = _shifts_of(coffs), _shifts_of(aoffs)

    def body(dz_in, dy_ref, ah, ab, ac, ah_h, ac_h, w_ref, dep_ref, dz_ref, dw_ref, ext_p, ext_d, sh, wb, stage):
        i, b = pl.program_id(0), pl.program_id(1)
        ti = ntt - 1 - i

        @pl.when((i == 0) & (b == 0))
        def _():
            dw_ref[...] = jnp.zeros_like(dw_ref)
            ext_d[...] = jnp.zeros_like(ext_d)
            ext_p[nrows:, :] = jnp.zeros((SUBLANES, D), F32)
            _fill_taps(wb, w_ref, KA)

        @pl.when(b == 0)
        def _():
            first = (ti % nt) == 0
            last = (ti % nt) == nt - 1
            ext_p[0:HALO, :] = jnp.where(first, 0.0, ah_h[...].astype(F32) * ac_h[...].astype(F32))
            ext_d[tt:nrows, :] = jnp.where(last, 0.0, ext_d[0:HALO, :])

            def prod(r0):
                rows = pl.ds(r0, RC)
                ext_p[pl.ds(HALO + r0, RC), :] = ah[rows, :].astype(F32) * ac[rows, :].astype(F32)
            _chunks(tt, prod)
            _shifted_copies(ext_p, sh, nrows, cshifts)

            def mid(*r0s):
                for r0, q in zip(r0s, _conv_chunks(sh, wb, coffs, r0s, cshifts)):
                    rows = pl.ds(r0, RC)
                    dy = dy_ref[rows, :].astype(F32)
                    stage[1, rows, :] = (dy * q).astype(BF)
                    ext_d[rows, :] = dy * ab[rows, :].astype(F32)
            _chunk_pairs(tt, mid)
            _conv_wgrad_chunked(dw_ref, ext_d, sh, coffs, tt, cshifts)
            _shifted_copies(ext_d, sh, nrows, ashifts)

            def fin(*r0s):
                for r0, dp in zip(r0s, _conv_chunks(sh, wb, aoffs, r0s, ashifts)):
                    rows = pl.ds(r0, RC)
                    stage[0, rows, :] = (dp * ac[rows, :].astype(F32)).astype(BF)
                    stage[2, rows, :] = (dp * ah[rows, :].astype(F32)).astype(BF)
            _chunk_pairs(tt, fin)

        dz_ref[...] = stage[b]

    return pl.pallas_call(
        body, name="mix_a_bwd", grid=(ntt, 3),
        in_specs=[ANY, pl.BlockSpec((None, tt, D), lambda i, b: (0, tile(i), 0)),
                  cur(0), cur(1), cur(2), halo(0), halo(2), row(KA), ANY],
        out_specs=[pl.BlockSpec((tt, D), lambda i, b: (tile(i), b)), pl.BlockSpec((KA, D), lambda i, b: (0, 0))],
        out_shape=[jax.ShapeDtypeStruct(dz.shape, BF), jax.ShapeDtypeStruct((KA, D), F32)],
        scratch_shapes=[pltpu.VMEM((nrows + SUBLANES, D), F32), pltpu.VMEM((nrows + SUBLANES, D), F32),
                        pltpu.VMEM((max(len(cshifts), len(ashifts)), nrows, D), F32),
                        pltpu.VMEM((KA * SUBLANES, D), F32), pltpu.VMEM((3, tt, D), BF)],
        input_output_aliases={0: 0},
        compiler_params=_cp("arbitrary", "arbitrary"),
    )(dz, dy3, z, z, z, z, z, wa, dep)


def _mix_b_bwd(dz, dy3, s, z, wc, lg, lb, S):
    T = z.shape[0]
    tt = min(S, MIX_TILE)
    nt = S // tt
    ntt = T // tt
    tile, cur, halo, row = _tile_specs(tt, ntt, True)

    nrows = HALO + tt

    def body(dz_in, dy_ref, s_ref, ca, cg, ca_h, cg_h, w_ref, lg_ref, lb_ref,
             dz_ref, dw_ref, dbc_ref, dlg_ref, dlb_ref, ext_r, ext_d, sh, wb, accs, stage):
        i, b = pl.program_id(0), pl.program_id(1)
        ti = ntt - 1 - i

        @pl.when((i == 0) & (b == 0))
        def _():
            dw_ref[...] = jnp.zeros_like(dw_ref)
            dbc_ref[...] = jnp.zeros_like(dbc_ref)
            dlg_ref[...] = jnp.zeros_like(dlg_ref)
            dlb_ref[...] = jnp.zeros_like(dlb_ref)
            ext_d[...] = jnp.zeros_like(ext_d)
            ext_r[nrows:, :] = jnp.zeros((SUBLANES, D), F32)
            _fill_taps(wb, w_ref, KC)

        @pl.when(b == 0)
        def _():
            first = (ti % nt) == 0
            last = (ti % nt) == nt - 1
            ext_r[0:HALO, :] = jnp.where(first, 0.0, ca_h[...].astype(F32) * _sig(cg_h[...].astype(F32)))
            ext_d[tt:nrows, :] = jnp.where(last, 0.0, ext_d[0:HALO, :])
            accs[...] = jnp.zeros_like(accs)

            def point(r0):
                rows = pl.ds(r0, RC)
                n, r = _ln_stats(s_ref[rows, :].astype(F32))
                t = n * lg_ref[...] + lb_ref[...]
                sg = _sig(t)
                dt = dy_ref[rows, :].astype(F32) * (sg * (1.0 + t * (1.0 - sg)))
                accs[0] += dt * n
                accs[1] += dt
                ds = _ln_bwd(dt * lg_ref[...], n, r)
                accs[2] += ds
                ext_d[rows, :] = ds
                ext_r[pl.ds(HALO + r0, RC), :] = ca[rows, :].astype(F32) * _sig(cg[rows, :].astype(F32))
            _chunks(tt, point, group=4)
            dlg_ref[...] += jnp.sum(accs[0], axis=0, keepdims=True)
            dlb_ref[...] += jnp.sum(accs[1], axis=0, keepdims=True)
            dbc_ref[...] += jnp.sum(accs[2], axis=0, keepdims=True)

            _shifted_copies(ext_r, sh, nrows)
            _conv_wgrad_chunked(dw_ref, ext_d, sh, _causal_offsets(KC), tt)
            _shifted_copies(ext_d, sh, nrows)

            def conv(*r0s):
                for r0, dr in zip(r0s, _conv_chunks(sh, wb, _anticausal_offsets(KC), r0s)):
                    rows = pl.ds(r0, RC)
                    cav = ca[rows, :].astype(F32)
                    sgc = _sig(cg[rows, :].astype(F32))
                    stage[0, rows, :] = (dr * sgc).astype(BF)
                    stage[1, rows, :] = (dr * cav * sgc * (1.0 - sgc)).astype(BF)
            _chunk_pairs(tt, conv)

        dz_ref[...] = stage[b]

    vec = pl.BlockSpec((1, D), lambda i, b: (0, 0))
    return pl.pallas_call(
        body, name="mix_b_bwd", grid=(ntt, 2),
        in_specs=[ANY, pl.BlockSpec((None, tt, D), lambda i, b: (1, tile(i), 0)),
                  pl.BlockSpec((tt, D), lambda i, b: (tile(i), 0)),
                  cur(3), cur(4), halo(3), halo(4), row(KC), row(), row()],
        out_specs=[pl.BlockSpec((tt, D), lambda i, b: (tile(i), 3 + b)),
                   pl.BlockSpec((KC, D), lambda i, b: (0, 0)), vec, vec, vec],
        out_shape=[jax.ShapeDtypeStruct(dz.shape, BF), jax.ShapeDtypeStruct((KC, D), F32)]
        + [jax.ShapeDtypeStruct((1, D), F32)] * 3,
        scratch_shapes=[pltpu.VMEM((nrows + SUBLANES, D), F32), pltpu.VMEM((nrows + SUBLANES, D), F32),
                        pltpu.VMEM((SUBLANES, nrows, D), F32), pltpu.VMEM((KC * SUBLANES, D), F32),
                        pltpu.VMEM((3, RC, D), F32), pltpu.VMEM((2, tt, D), BF)],
        input_output_aliases={0: 0},
        compiler_params=_cp("arbitrary", "arbitrary"),
    )(dz, dy3, s, z, z, z, z, wc, lg, lb)


def _mix_s_bwd(dz, dy3, z, lg, lb, ws, wst, bst, S):
    T = z.shape[0]
    tt = min(S, MIX_TILE)
    ntt = T // tt
    _, cur, _, row = _tile_specs(tt, ntt, False)

    def body(dz_in, dy_ref, su, sv, lg_ref, lb_ref, ws_ref, wst_ref, bst_ref,
             dz_ref, dws_ref, dbst_ref, dlg_ref, dlb_ref, u_scr, vn_scr, dvn_scr, stage):
        i, b = pl.program_id(0), pl.program_id(1)

        @pl.when((i == 0) & (b == 0))
        def _():
            dws_ref[...] = jnp.zeros_like(dws_ref)
            dbst_ref[...] = jnp.zeros_like(dbst_ref)
            dlg_ref[...] = jnp.zeros_like(dlg_ref)
            dlb_ref[...] = jnp.zeros_like(dlb_ref)

        @pl.when(b == 0)
        def _():
            u, du_dx = _gelu(su[...].astype(F32))
            v, dv_dx = _gelu(sv[...].astype(F32))
            u_scr[...] = u
            n, r = _ln_stats(v)
            vn_scr[...] = (n * lg_ref[...] + lb_ref[...]).astype(BF)
            mask = _causal_mask(False)
            mask_t = _causal_mask(True)
            for h in range(HEADS):
                wm = jnp.where(mask, ws_ref[h], 0.0).astype(BF)
                wmt = jnp.where(mask_t, wst_ref[h], 0.0).astype(BF)
                cols = slice(h * CHUNK, (h + 1) * CHUNK)
                for c in range(tt // CHUNK):
                    rows = slice(c * CHUNK, (c + 1) * CHUNK)
                    vb = vn_scr[rows, cols]
                    mixed = _dot(wm, vb) + bst_ref[:, h:h + 1]
                    dy = dy_ref[rows, cols].astype(F32)
                    dmix = dy * u_scr[rows, cols]
                    u_scr[rows, cols] = dy * mixed
                    dbst_ref[:, h:h + 1] += jnp.sum(dmix, axis=1, keepdims=True)
                    dmb = dmix.astype(BF)
                    dws_ref[h] += _dot_nt(dmb, vb)
                    dvn_scr[rows, cols] = _dot(wmt, dmb)
            stage[0] = (u_scr[...] * du_dx).astype(BF)
            dvn = dvn_scr[...]
            dlg_ref[...] += jnp.sum(dvn * n, axis=0, keepdims=True)
            dlb_ref[...] += jnp.sum(dvn, axis=0, keepdims=True)
            stage[1] = (_ln_bwd(dvn * lg_ref[...], n, r) * dv_dx).astype(BF)

        dz_ref[...] = stage[b]

    vec = pl.BlockSpec((1, D), lambda i, b: (0, 0))
    wsp = pl.BlockSpec((HEADS, CHUNK, CHUNK), lambda i, b: (0, 0, 0))
    bsp = pl.BlockSpec((CHUNK, HEADS), lambda i, b: (0, 0))
    return pl.pallas_call(
        body, name="mix_s_bwd", grid=(ntt, 2),
        in_specs=[ANY, pl.BlockSpec((None, tt, D), lambda i, b: (2, i, 0)),
                  cur(5), cur(6), row(), row(), wsp, wsp, bsp],
        out_specs=[pl.BlockSpec((tt, D), lambda i, b: (i, 5 + b)), wsp, bsp, vec, vec],
        out_shape=[jax.ShapeDtypeStruct(dz.shape, BF), jax.ShapeDtypeStruct((HEADS, CHUNK, CHUNK), F32),
                   jax.ShapeDtypeStruct((CHUNK, HEADS), F32), jax.ShapeDtypeStruct((1, D), F32),
                   jax.ShapeDtypeStruct((1, D), F32)],
        scratch_shapes=[pltpu.VMEM((tt, D), F32), pltpu.VMEM((tt, D), BF), pltpu.VMEM((tt, D), F32),
                        pltpu.VMEM((2, tt, D), BF)],
        input_output_aliases={0: 0},
        compiler_params=_cp("arbitrary", "arbitrary"),
    )(dz, dy3, z, z, lg, lb, ws, wst, bst)


def _in_proj_bwd(dz, w, x, g, dx1, dep):
    T = x.shape[0]
    nc = w.shape[2]
    tm = min(T, 1024)
    tn = nc
    nj = nc // tn
    ep = min(tm, 128)

    def body(dz_ref, w_ref, x_ref, g_ref, dx1_ref, dep_ref, dx_ref, dg_ref, acc):
        i, k, j = pl.program_id(0), pl.program_id(1), pl.program_id(2)

        @pl.when((i == 0) & (k == 0) & (j == 0))
        def _():
            dg_ref[...] = jnp.zeros_like(dg_ref)

        @pl.when((k == 0) & (j == 0))
        def _():
            acc[...] = jnp.zeros_like(acc)

        acc[...] += _dot_nt(dz_ref[...], w_ref[...])

        @pl.when((k == NSH - 1) & (j == nj - 1))
        def _():
            def step(c, dg):
                rows = pl.ds(pl.multiple_of(c * ep, ep), ep)
                dx, dgc = _rms_bwd(acc[rows, :], x_ref[rows, :], g_ref[...])
                dx_ref[rows, :] = dx1_ref[rows, :] + dx
                return dg + dgc
            dg_ref[...] += lax.fori_loop(0, tm // ep, step, jnp.zeros((1, D), F32))

    rowblk = pl.BlockSpec((tm, D), lambda i, k, j: (i, 0))
    vec = pl.BlockSpec((1, D), lambda i, k, j: (0, 0))
    return pl.pallas_call(
        body, name="in_proj_bwd", grid=(T // tm, NSH, nj),
        in_specs=[pl.BlockSpec((tm, tn), lambda i, k, j: (i, k * nj + j)),
                  pl.BlockSpec((None, D, tn), lambda i, k, j: (k, 0, j)), rowblk, vec, rowblk, ANY],
        out_specs=[rowblk, vec],
        out_shape=[jax.ShapeDtypeStruct((T, D), F32), jax.ShapeDtypeStruct((1, D), F32)],
        scratch_shapes=[pltpu.VMEM((tm, D), F32)],
        compiler_params=_cp("arbitrary", "arbitrary", "arbitrary"),
    )(dz, w, x, g, dx1, dep)


def _layer_fwd(x, p, S, dep, late):
    h, z = _in_proj(x, p["g_mix_pre"], p["w_in"], dep)
    ya = _mix_a_fwd(z, p["conv_a_w"], S)
    yc, s = _mix_b_fwd(z, p["conf_dw_w"], p["conf_dw_b"], p["conf_ln_g"], p["conf_ln_b"], S)
    ys = _mix_s_fwd(z, p["sgu_ln_g"], p["sgu_ln_b"], p["sgu_ws"], p["sgu_bt"], S)
    more, dep2 = late(ys)
    p.update(more)
    p3, merged, m, x1 = _mix_out_fwd(ya, yc, ys, z, x, p["w_branch"], p["w_out"], p["g_mix_post"], dep2)
    h2, a, f, x2 = _ffn_fwd(x1, p["g_ffn_pre"], p["w_ff1"], p["w_ff2"], p["g_ffn_post"])
    saved = dict(x=x, h=h, z=z, ya=ya, yc=yc, ys=ys, s=s, p3=p3, merged=merged, m=m, x1=x1, h2=h2, a=a, f=f)
    return x2, saved


def _layer_bwd(dx2, p, sv, S, dep, hooks):
    after_ffn, early, mid = hooks
    T = dx2.shape[0]
    bt = min(T, WGRAD_TILE)
    nt = T // bt
    rk = D // NSH
    df, da, dx1, dg_ffn_post, dg_ffn_pre = _ffn_bwd(dx2, sv["f"], p["g_ffn_post"], sv["a"], p["w_ff2"],
                                                    p["w_ff1"], sv["x1"], p["g_ffn_pre"], dep)
    dw_ff2 = _wgrad("wgrad_ff2", (sv["a"], df), (NSH, nt),
                    [pl.BlockSpec((bt, D), lambda k, t: (t, k)), pl.BlockSpec((bt, D), lambda k, t: (t, 0))],
                    pl.BlockSpec((None, D, D), lambda k, t: (k, 0, 0)),
                    jax.ShapeDtypeStruct((NSH, D, D), BF), (D, D), relu2=True)
    dw_ff1 = _wgrad("wgrad_ff1", (sv["h2"], da), (NSH, nt),
                    [pl.BlockSpec((bt, D), lambda k, t: (t, 0)), pl.BlockSpec((bt, D), lambda k, t: (t, k))],
                    pl.BlockSpec((None, D, D), lambda k, t: (k, 0, 0)),
                    jax.ShapeDtypeStruct((NSH, D, D), BF), (D, D))
    dm, dp3, dy3, dz, dg_mix_post = _mix_out_bwd(dx1, sv["m"], p["g_mix_post"], p["w_out"], sv["p3"], sv["z"],
                                                 p["w_branch"], after_ffn(dx1))
    dw_out = _wgrad("wgrad_out", (sv["merged"], dm), (nt,),
                    [pl.BlockSpec((bt, D), lambda t: (t, 0)), pl.BlockSpec((bt, D), lambda t: (t, 0))],
                    pl.BlockSpec((D, D), lambda t: (0, 0)),
                    jax.ShapeDtypeStruct((D, D), BF), (D, D)).reshape(NSH, rk, D)
    ysp = lambda n: pl.BlockSpec((bt, D), lambda b, t: (jnp.where(b == n, t, 0), 0))
    dw_br = _wgrad("wgrad_branch", (sv["ya"], sv["yc"], sv["ys"], dp3), (3, nt),
                   [ysp(0), ysp(1), ysp(2), pl.BlockSpec((None, bt, D), lambda b, t: (b, t, 0))],
                   pl.BlockSpec((NSH, None, rk, D), lambda b, t: (0, b, 0, 0)),
                   jax.ShapeDtypeStruct((NSH, 3, rk, D), BF), (D, D), pick=lambda: pl.program_id(0))
    dz, dwa = _mix_a_bwd(dz, dy3, sv["z"], p["conv_a_w"], S, early([dw_br, dw_out, dw_ff1, dw_ff2]))
    dz, dwc, dbc, dclg, dclb = _mix_b_bwd(dz, dy3, sv["s"], sv["z"], p["conf_dw_w"], p["conf_ln_g"],
                                          p["conf_ln_b"], S)
    dz, dws, dbst, dslg, dslb = _mix_s_bwd(dz, dy3, sv["z"], p["sgu_ln_g"], p["sgu_ln_b"], p["sgu_ws"],
                                           p["sgu_wst"], p["sgu_bt"], S)
    dx, dg_mix_pre = _in_proj_bwd(dz, p["w_in"], sv["x"], p["g_mix_pre"], dx1, mid(dz))
    tn = p["w_in"].shape[2]
    nj = p["w_in"].shape[2] // tn
    dw_in = _wgrad("wgrad_in", (sv["h"], dz), (NSH, nj, nt),
                   [pl.BlockSpec((bt, D), lambda k, j, t: (t, 0)),
                    pl.BlockSpec((bt, tn), lambda k, j, t: (t, k * nj + j))],
                   pl.BlockSpec((None, D, tn), lambda k, j, t: (k, 0, j)),
                   jax.ShapeDtypeStruct(p["w_in"].shape, BF), (D, tn))
    tril = jnp.tril(jnp.ones((CHUNK, CHUNK), bool))
    small = dict(norm_mix_pre=dg_mix_pre, norm_mix_post=dg_mix_post, norm_ffn_pre=dg_ffn_pre,
                 norm_ffn_post=dg_ffn_post, conv_a_w=dwa, conf_dw_w=dwc, conf_dw_b=dbc, conf_ln_g=dclg,
                 conf_ln_b=dclb, sgu_ln_g=dslg, sgu_ln_b=dslb,
                 sgu_ws=jnp.where(tril[None], dws, 0.0), sgu_b=dbst.T)
    big = dict(w_in=dw_in, w_branch=dw_br, w_out=dw_out, w_ff1=dw_ff1, w_ff2=dw_ff2)
    return dx, big, small


SMALL_NAMES = ("norm_mix_pre", "norm_mix_post", "norm_ffn_pre", "norm_ffn_post", "conv_a_w", "conf_dw_w",
               "conf_dw_b", "conf_ln_g", "conf_ln_b", "sgu_ln_g", "sgu_ln_b", "sgu_b", "sgu_ws")
SMALL_ROWS = dict(norm_mix_pre=1, norm_mix_post=1, norm_ffn_pre=1, norm_ffn_post=1, conv_a_w=KA, conf_dw_w=KC,
                  conf_dw_b=1, conf_ln_g=1, conf_ln_b=1, sgu_ln_g=1, sgu_ln_b=1, sgu_b=1, sgu_ws=CHUNK)
def _pad8(r):
    return -(-r // SUBLANES) * SUBLANES


PACK_ROWS = sum(_pad8(r) for r in SMALL_ROWS.values())


def _pack_small(d):
    parts = []
    for n in SMALL_NAMES:
        r = SMALL_ROWS[n]
        parts.append(jnp.pad(d[n].reshape(r, D).astype(F32), ((0, _pad8(r) - r), (0, 0))))
    return jnp.concatenate(parts, axis=0)


def _unpack_small(a, shapes):
    out, r = {}, 0
    for n in SMALL_NAMES:
        out[n] = a[:, r:r + SMALL_ROWS[n]].reshape((a.shape[0],) + tuple(shapes[n]))
        r += _pad8(SMALL_ROWS[n])
    return out


def _me():
    return lax.axis_index("x"), lax.axis_index("y"), lax.axis_index("c")


def _slab(ref, q, a, h=None):
    r = ref.shape[1]
    rows = slice(None) if h is None else pl.ds(h * (r // 2), r // 2)
    return ref.at[pl.ds(q * a, a), rows, :]


def _rows(ref, h):
    r = ref.shape[-2]
    lead = (slice(None),) * (len(ref.shape) - 2)
    return ref.at[lead + (pl.ds(h * (r // 2), r // 2), slice(None))]


def _rcopy(src, dst, sems, idx, dev):
    return pltpu.make_async_remote_copy(src_ref=src, dst_ref=dst, send_sem=sems[0].at[idx], recv_sem=sems[1].at[idx],
                                        device_id=dev, device_id_type=MESH)


def _send_halves_to_sibling(parts):
    n = len(parts)

    def body(*refs):
        src, dst = refs[:n], refs[n:2 * n]
        sems = refs[2 * n:2 * n + 2]
        x, y, c = _me()
        cps = [_rcopy(_rows(src[i], 1 - c), dst[i], sems, i, (x, y, 1 - c)) for i in range(n)]
        for cp in cps:
            cp.start()
        for cp in cps:
            cp.wait()

    outs = [jax.ShapeDtypeStruct((p.shape[0], p.shape[1] // 2, p.shape[2]), p.dtype) for p in parts]
    return pl.pallas_call(
        body, name="pair_exchange", in_specs=[ANY] * n, out_specs=[ANY] * n, out_shape=outs,
        scratch_shapes=[pltpu.SemaphoreType.DMA((n,)), pltpu.SemaphoreType.DMA((n,))],
    )(*parts)


PAIR_BLOCK_BYTES = 3 * 512 * 1024


def _pair_add(parts, sibs, c):
    n = len(parts)
    steps = 1
    while any(p.shape[0] * (p.shape[1] // 2 // steps) * p.shape[2] * 2 > PAIR_BLOCK_BYTES for p in parts):
        steps *= 2

    def body(c_ref, *refs):
        for p_ref, s_ref, o_ref in zip(refs[:n], refs[n:2 * n], refs[2 * n:]):
            o_ref[...] = (p_ref[...].astype(F32) + s_ref[...].astype(F32)).astype(BF)

    def blk(p):
        return (p.shape[0], p.shape[1] // 2 // steps, p.shape[2])

    mine = [pl.BlockSpec(blk(p), lambda g, c_ref: (0, c_ref[0] * steps + g, 0)) for p in parts]
    same = [pl.BlockSpec(blk(p), lambda g, c_ref: (0, g, 0)) for p in parts]
    return pl.pallas_call(
        body, name="pair_add",
        grid_spec=pltpu.PrefetchScalarGridSpec(
            num_scalar_prefetch=1, grid=(steps,), in_specs=mine + same, out_specs=same),
        out_shape=[jax.ShapeDtypeStruct(s.shape, BF) for s in sibs],
        compiler_params=_cp("arbitrary"),
    )(c, *parts, *sibs)


def _other_chips(x, y):
    return [(1 - x, y), (x, 1 - y), (1 - x, 1 - y)]


def _split_call(name, copies, srcs, lands, sems=None, after=()):
    n, m = len(srcs), len(lands)
    hbm = lambda t: pltpu.HBM(t.shape, t.dtype)
    pin = lambda t: pltpu.with_memory_space_constraint(t, pltpu.HBM)
    thru = [hbm(t) for t in srcs] + [hbm(t) for t in lands]
    sem_spec = pl.BlockSpec(memory_space=pltpu.SEMAPHORE)
    effect = pltpu.CompilerParams(has_side_effects=pltpu.SideEffectType.DATAFLOW_SIDE_EFFECTING)
    if sems is None:
        def start_body(*refs):
            src, land = refs[:n], refs[n:n + m]
            ssem, rsem = refs[n + m + len(after)], refs[n + m + len(after) + 1]
            token = refs[-1]
            cps = copies(src, land, (ssem, rsem))
            for cp in cps:
                cp.start()
            token[...] = jnp.zeros_like(token)

        ncp = copies.count
        out = pl.pallas_call(
            start_body, name=name,
            out_shape=(pltpu.SemaphoreType.DMA((ncp,)), pltpu.SemaphoreType.DMA((ncp,)), *thru,
                       jax.ShapeDtypeStruct((8, 128), F32)),
            in_specs=[ANY] * (n + m + len(after)),
            out_specs=(sem_spec, sem_spec, *([ANY] * (n + m)), pl.BlockSpec(memory_space=pltpu.VMEM)),
            input_output_aliases={i: 2 + i for i in range(n + m)},
            compiler_params=effect,
        )(*[pin(t) for t in srcs], *[pin(t) for t in lands], *after)
        return out[0], out[1], list(out[2:2 + n]), list(out[2 + n:2 + n + m]), out[-1]

    def wait_body(*refs):
        src, land = refs[:n], refs[n:n + m]
        ssem, rsem = refs[n + m], refs[n + m + 1]
        for cp in copies(src, land, (ssem, rsem)):
            cp.wait_send()
            cp.wait_recv()

    out = pl.pallas_call(
        wait_body, name=name, out_shape=tuple(thru),
        in_specs=[ANY] * (n + m) + [sem_spec, sem_spec] + [ANY] * len(after),
        out_specs=tuple([ANY] * (n + m)),
        input_output_aliases={i: i for i in range(n + m)},
        compiler_params=effect,
    )(*srcs, *lands, sems[0], sems[1], *after)
    return list(out[:n]), list(out[n:])


def _cast_into(w, land, layer, kidx, dep):
    _, a, R, C = w.shape
    br = R
    while br * C > 512 * 1024 and br % 32 == 0:
        br //= 2

    def body(k_ref, w_ref, land_ref, dep_ref, o_ref):
        o_ref[...] = w_ref[...].astype(o_ref.dtype)

    return pl.pallas_call(
        body, name="cast_into",
        grid_spec=pltpu.PrefetchScalarGridSpec(
            num_scalar_prefetch=1, grid=(a, R // br),
            in_specs=[pl.BlockSpec((None, None, br, C), lambda e, i, k: (layer, e, i, 0)), ANY, ANY],
            out_specs=pl.BlockSpec((None, br, C), lambda e, i, k: (k[0] * a + e, i, 0))),
        out_shape=jax.ShapeDtypeStruct(land.shape, land.dtype), input_output_aliases={2: 0},
        compiler_params=_cp("arbitrary", "arbitrary"),
    )(kidx, w, land, dep)


class _GatherCopies:
    def __init__(self, n, halves=True):
        self.n, self.count, self.halves = n, 3 * n, halves

    def __call__(self, src, land, sems):
        x, y, c = _me()
        k = 2 * x + y
        cps = []
        for j, (qx, qy) in enumerate(_other_chips(x, y)):
            for i in range(self.n):
                mine = _slab(land[i], k, land[i].shape[0] // NSH, c if self.halves else None)
                cps.append(_rcopy(mine, mine, sems, j * self.n + i, (qx, qy, c)))
        return cps


def _gather_finish(lands):
    n = len(lands)

    def body(*refs):
        dst = refs[n:2 * n]
        sems = refs[2 * n:2 * n + 2]
        x, y, c = _me()
        av = [d.shape[0] // NSH for d in dst]
        cps = []
        for j, (qx, qy) in enumerate(_other_chips(x, y)):
            for i in range(n):
                got = _slab(dst[i], 2 * qx + qy, av[i], c)
                cps.append(_rcopy(got, got, sems, j * n + i, (x, y, 1 - c)))
        for cp in cps:
            cp.start()
        for j, (qx, qy) in enumerate(_other_chips(x, y)):
            for i in range(n):
                other = _slab(dst[i], 2 * qx + qy, av[i], 1 - c)
                _rcopy(other, other, sems, j * n + i, (x, y, c)).wait_recv()
        for cp in cps:
            cp.wait_send()

    return pl.pallas_call(
        body, name="gather_finish", in_specs=[ANY] * n, out_specs=[ANY] * n,
        out_shape=[jax.ShapeDtypeStruct(t.shape, t.dtype) for t in lands],
        input_output_aliases={i: i for i in range(n)},
        scratch_shapes=[pltpu.SemaphoreType.DMA((3 * n,)), pltpu.SemaphoreType.DMA((3 * n,))],
    )(*lands)


class _PairCopies:
    def __init__(self, n):
        self.n, self.count = n, n

    def __call__(self, src, land, sems):
        x, y, c = _me()
        return [_rcopy(_rows(src[i], 1 - c), land[i], sems, i, (x, y, 1 - c)) for i in range(self.n)]


class _ScatterCopies:
    def __init__(self, n):
        self.n, self.count = n, 3 * n

    def __call__(self, src, land, sems):
        x, y, c = _me()
        k = 2 * x + y
        cps = []
        for j, (qx, qy) in enumerate(_other_chips(x, y)):
            for i in range(self.n):
                a = src[i].shape[0] // NSH
                cps.append(_rcopy(_slab(src[i], 2 * qx + qy, a), _slab(land[i], k, a), sems, j * self.n + i,
                                  (qx, qy, c)))
        return cps


def _sum_chips(own, rcv, acc, layer, nlayers, idx):
    A, hr, C = rcv.shape
    a = A // NSH
    br = min(hr, 512)
    nb = hr // br

    def body(*refs):
        r0, r1, r2, r3 = refs[1:5]
        o_ref = refs[-1]
        o_ref[...] = ((r0[...].astype(F32) + r1[...].astype(F32)) + r2[...].astype(F32)) + r3[...].astype(F32)

    slot = lambda s: pl.BlockSpec((None, br, C), lambda e, i, ix: (ix[s] * a + e, i, 0))
    ops = [own, rcv, rcv, rcv]
    in_specs = [slot(0), slot(1), slot(2), slot(3)]
    aliases = {}
    if acc is not None:
        ops.append(acc)
        in_specs.append(ANY)
        aliases = {5: 0}
    return pl.pallas_call(
        body, name="sum_chips",
        grid_spec=pltpu.PrefetchScalarGridSpec(
            num_scalar_prefetch=1, grid=(a, nb), in_specs=in_specs,
            out_specs=pl.BlockSpec((None, None, br, C), lambda e, i, ix: (layer, e, ix[4] * nb + i, 0))),
        out_shape=jax.ShapeDtypeStruct((nlayers, a, 2 * hr, C), F32), input_output_aliases=aliases,
        compiler_params=_cp("arbitrary", "arbitrary"),
    )(idx, *ops)


def _join_halves(fulls):
    n = len(fulls)

    def body(*refs):
        buf = refs[n:2 * n]
        sems = refs[2 * n:2 * n + 2]
        x, y, c = _me()
        cps = [_rcopy(_rows(buf[i], c), _rows(buf[i], c), sems, i, (x, y, 1 - c)) for i in range(n)]
        for cp in cps:
            cp.start()
        for i in range(n):
            _rcopy(_rows(buf[i], 1 - c), _rows(buf[i], 1 - c), sems, i, (x, y, c)).wait_recv()
        for cp in cps:
            cp.wait_send()

    return pl.pallas_call(
        body, name="join_halves", in_specs=[ANY] * n, out_specs=[ANY] * n,
        out_shape=[jax.ShapeDtypeStruct(t.shape, t.dtype) for t in fulls],
        input_output_aliases={i: i for i in range(n)},
        scratch_shapes=[pltpu.SemaphoreType.DMA((n,)), pltpu.SemaphoreType.DMA((n,))],
    )(*fulls)


def _small_blocks(hr):
    br = hr
    while br > 512 and br % 16 == 0:
        br //= 2
    return br, hr // br


def _pair_sum_slot(part, sib, ck):
    R, C = part.shape
    hr = R // 2
    br, nb = _small_blocks(hr)

    def body(ix, p_ref, s_ref, o_ref):
        o_ref[...] = p_ref[...] + s_ref[...]

    return pl.pallas_call(
        body, name="pair_sum_slot",
        grid_spec=pltpu.PrefetchScalarGridSpec(
            num_scalar_prefetch=1, grid=(nb,),
            in_specs=[pl.BlockSpec((br, C), lambda i, ix: (ix[0] * nb + i, 0)),
                      pl.BlockSpec((br, C), lambda i, ix: (i, 0))],
            out_specs=pl.BlockSpec((None, br, C), lambda i, ix: (ix[1], i, 0))),
        out_shape=jax.ShapeDtypeStruct((NSH, hr, C), F32),
        compiler_params=_cp("arbitrary"),
    )(ck, part, sib)


def _sum_slots(slots, ck):
    _, hr, C = slots.shape
    br, nb = _small_blocks(hr)

    def body(ix, s_ref, o_ref):
        o_ref[...] = ((s_ref[0] + s_ref[1]) + s_ref[2]) + s_ref[3]

    return pl.pallas_call(
        body, name="sum_slots",
        grid_spec=pltpu.PrefetchScalarGridSpec(
            num_scalar_prefetch=1, grid=(nb,),
            in_specs=[pl.BlockSpec((NSH, br, C), lambda i, ix: (0, i, 0))],
            out_specs=pl.BlockSpec((br, C), lambda i, ix: (ix[0] * nb + i, 0))),
        out_shape=jax.ShapeDtypeStruct((2 * hr, C), F32),
        compiler_params=_cp("arbitrary"),
    )(ck, slots)


def _adamw(w, g, m, v):
    shape = w.shape
    C = shape[-1]
    R = shape[-2]
    A = 1
    for s in shape[:-2]:
        A *= s
    br = R
    while br * C > 256 * 1024 and br % 16 == 0:
        br //= 2
    c1 = 1.0 / (1.0 - ADAM_B1 ** ADAM_STEP)
    c2 = 1.0 / (1.0 - ADAM_B2 ** ADAM_STEP)

    def body(w_ref, g_ref, m_ref, v_ref, og_ref, d_ref, nm_ref, nv_ref):
        gv = g_ref[...]
        og_ref[...] = gv
        nm = ADAM_B1 * m_ref[...] + (1.0 - ADAM_B1) * gv
        nv = ADAM_B2 * v_ref[...] + (1.0 - ADAM_B2) * (gv * gv)
        nm_ref[...] = nm
        nv_ref[...] = nv
        d_ref[...] = -ADAM_LR * ((nm * c1) / (jnp.sqrt(nv * c2) + ADAM_EPS) + ADAM_WD * w_ref[...])

    blk = pl.BlockSpec((None, br, C), lambda a, i: (a, i, 0))
    outs = pl.pallas_call(
        body, name="adamw", grid=(A, R // br), in_specs=[blk] * 4, out_specs=[blk] * 4,
        out_shape=[jax.ShapeDtypeStruct((A, R, C), F32)] * 4,
        compiler_params=_cp("arbitrary", "arbitrary"),
    )(*(t.reshape(A, R, C) for t in (w, g, m, v)))
    return tuple(o.reshape(shape) for o in outs)


WEIGHTS = ("norm_mix_pre", "norm_mix_post", "norm_ffn_pre", "norm_ffn_post", "w_in", "conv_a_w", "conf_dw_w",
           "conf_dw_b", "conf_ln_g", "conf_ln_b", "sgu_ln_g", "sgu_ln_b", "sgu_ws", "sgu_b", "w_branch", "w_out",
           "w_ff1", "w_ff2")
BIG = ("w_in", "w_branch", "w_out", "w_ff1", "w_ff2")
CONV_ROWS = 48


def kernel(x, norm_mix_pre, norm_mix_post, norm_ffn_pre, norm_ffn_post, w_in, conv_a_w, conf_dw_w, conf_dw_b, conf_ln_g, conf_ln_b, sgu_ln_g, sgu_ln_b, sgu_ws, sgu_b, w_branch, w_out, w_ff1, w_ff2, loss_target, m_norm_mix_pre, m_norm_mix_post, m_norm_ffn_pre, m_norm_ffn_post, m_w_in, m_conv_a_w, m_conf_dw_w, m_conf_dw_b, m_conf_ln_g, m_conf_ln_b, m_sgu_ln_g, m_sgu_ln_b, m_sgu_ws, m_sgu_b, m_w_branch, m_w_out, m_w_ff1, m_w_ff2, v_norm_mix_pre, v_norm_mix_post, v_norm_ffn_pre, v_norm_ffn_post, v_w_in, v_conv_a_w, v_conf_dw_w, v_conf_dw_b, v_conf_ln_g, v_conf_ln_b, v_sgu_ln_g, v_sgu_ln_b, v_sgu_ws, v_sgu_b, v_w_branch, v_w_out, v_w_ff1, v_w_ff2):
    w = dict(norm_mix_pre=norm_mix_pre, norm_mix_post=norm_mix_post, norm_ffn_pre=norm_ffn_pre,
             norm_ffn_post=norm_ffn_post, w_in=w_in, conv_a_w=conv_a_w, conf_dw_w=conf_dw_w, conf_dw_b=conf_dw_b,
             conf_ln_g=conf_ln_g, conf_ln_b=conf_ln_b, sgu_ln_g=sgu_ln_g, sgu_ln_b=sgu_ln_b, sgu_ws=sgu_ws,
             sgu_b=sgu_b, w_branch=w_branch, w_out=w_out, w_ff1=w_ff1, w_ff2=w_ff2)
    mom = dict(norm_mix_pre=m_norm_mix_pre, norm_mix_post=m_norm_mix_post, norm_ffn_pre=m_norm_ffn_pre,
               norm_ffn_post=m_norm_ffn_post, w_in=m_w_in, conv_a_w=m_conv_a_w, conf_dw_w=m_conf_dw_w,
               conf_dw_b=m_conf_dw_b, conf_ln_g=m_conf_ln_g, conf_ln_b=m_conf_ln_b, sgu_ln_g=m_sgu_ln_g,
               sgu_ln_b=m_sgu_ln_b, sgu_ws=m_sgu_ws, sgu_b=m_sgu_b, w_branch=m_w_branch, w_out=m_w_out,
               w_ff1=m_w_ff1, w_ff2=m_w_ff2)
    var = dict(norm_mix_pre=v_norm_mix_pre, norm_mix_post=v_norm_mix_post, norm_ffn_pre=v_norm_ffn_pre,
               norm_ffn_post=v_norm_ffn_post, w_in=v_w_in, conv_a_w=v_conv_a_w, conf_dw_w=v_conf_dw_w,
               conf_dw_b=v_conf_dw_b, conf_ln_g=v_conf_ln_g, conf_ln_b=v_conf_ln_b, sgu_ln_g=v_sgu_ln_g,
               sgu_ln_b=v_sgu_ln_b, sgu_ws=v_sgu_ws, sgu_b=v_sgu_b, w_branch=v_w_branch, w_out=v_w_out,
               w_ff1=v_w_ff1, w_ff2=v_w_ff2)
    L = w_in.shape[0]
    nseq, S, _ = x.shape
    T = nseq * S
    rk = D // NSH
    mx, my, mc = _me()
    k_chip = 2 * mx + my

    big_src = [w_in.reshape(L, 1, D, w_in.shape[2]), w_branch, w_out.reshape(L, 1, rk, D),
               w_ff1.reshape(L, 1, D, w_ff1.shape[2]), w_ff2.reshape(L, 1, w_ff2.shape[1], D)]
    kidx = jnp.reshape(k_chip, (1,)).astype(jnp.int32)
    conv_src = jnp.concatenate(
        [jnp.pad(conv_a_w, ((0, 0), (0, SUBLANES - KA), (0, 0))), jnp.pad(conf_dw_w, ((0, 0), (0, 1), (0, 0))),
         jnp.zeros((L, CONV_ROWS - SUBLANES - KC - 1, rk), F32)], axis=1)[None]

    def early_params(l, g_in, conv_full):
        return dict(
            g_mix_pre=norm_mix_pre[l][None], g_mix_post=norm_mix_post[l][None], g_ffn_pre=norm_ffn_pre[l][None],
            g_ffn_post=norm_ffn_post[l][None], w_in=g_in, conv_a_w=conv_full[l, :KA],
            conf_dw_w=conv_full[l, SUBLANES:SUBLANES + KC], conf_dw_b=conf_dw_b[l][None],
            conf_ln_g=conf_ln_g[l][None], conf_ln_b=conf_ln_b[l][None], sgu_ln_g=sgu_ln_g[l][None],
            sgu_ln_b=sgu_ln_b[l][None], sgu_ws=sgu_ws[l], sgu_wst=jnp.swapaxes(sgu_ws[l], 1, 2),
            sgu_bt=sgu_b[l].T)

    def late_params(gathered):
        g_br, g_out, g_ff1, g_ff2 = gathered
        return dict(w_branch=g_br.reshape(NSH, 3, rk, D), w_out=g_out.reshape(D, D), w_ff1=g_ff1,
                    w_ff2=g_ff2.reshape(NSH * w_ff2.shape[1], D))

    def cast_lands(srcs, l, dep):
        return [_cast_into(s, lax.empty((NSH * s.shape[1],) + s.shape[2:], F32 if s is conv_src else BF), l, kidx,
                           dep) for s in srcs]

    def gather_start(name, lands, after):
        return _split_call(name, _GatherCopies(len(lands)), [], lands, after=after)

    def gather_land(name, flight, after):
        ssem, rsem, _, lands, _ = flight
        _, lands = _split_call(name, _GatherCopies(len(lands)), [], lands, (ssem, rsem), after)
        return _gather_finish(lands)

    zero_tok = jnp.zeros((8, 128), F32)
    xt = x.reshape(T, D)
    layers, saved = [], []
    head = gather_start("gather_start_0a", cast_lands([big_src[0], conv_src], 0, kidx), [])
    tails = [cast_lands(big_src[1:], l, head[4]) for l in range(L)]
    heads = [None] + [cast_lands(big_src[:1], l, head[4]) for l in range(1, L)]
    behind = [xt] + [t for ls in tails + heads[1:] for t in ls]
    conv_full = None
    for l in range(L):
        got = gather_land(f"gather_wait_{l}a", head, behind if l == 0 else [xt])
        g_in = got[0]
        if l == 0:
            conv_full = got[1].reshape(NSH, L, CONV_ROWS, rk).transpose(1, 2, 0, 3).reshape(L, CONV_ROWS, D)
        tail = gather_start(f"gather_start_{l}b", tails[l], [g_in])
        nxt = {}

        def late(after, l=l, tail=tail, nxt=nxt):
            more = late_params(gather_land(f"gather_wait_{l}b", tail, [after]))
            if l + 1 == L:
                return more, zero_tok
            nxt["head"] = gather_start(f"gather_start_{l + 1}a", heads[l + 1], [more["w_ff1"]])
            return more, nxt["head"][4]

        p = early_params(l, g_in, conv_full)
        xt, sv = _layer_fwd(xt, p, S, tail[4], late)
        head = nxt.get("head")
        layers.append(p)
        saved.append(sv)
    dx, loss_row = _loss_head(xt, loss_target.reshape(T, D))
    loss = lax.psum(loss_row[0, 0], ("x", "y", "c"))

    c_arr = jnp.reshape(mc, (1,)).astype(jnp.int32)
    idx = jnp.stack([k_chip, k_chip ^ 2, k_chip ^ 1, k_chip ^ 3, mc]).astype(jnp.int32)
    fulls = {n: None for n in BIG}
    smalls = [None] * L

    def pair_start(tag, parts):
        lands = [lax.empty((p.shape[0], p.shape[1] // 2, p.shape[2]), p.dtype) for p in parts]
        return _split_call(f"pair_start_{tag}", _PairCopies(len(parts)), parts, lands)

    def pair_land_scatter_start(tag, fl, after):
        ssem, rsem, parts, sib, _ = fl
        parts, sib = _split_call(f"pair_wait_{tag}", _PairCopies(len(parts)), parts, sib, (ssem, rsem), after)
        sums = _pair_add(parts, sib, c_arr)
        rcv = [lax.empty(s.shape, s.dtype) for s in sums]
        return _split_call(f"scatter_start_{tag}", _ScatterCopies(len(sums)), sums, rcv)

    def scatter_land(tag, fl, names, l, after):
        ssem, rsem, sums, rcv, _ = fl
        sums, rcv = _split_call(f"scatter_wait_{tag}", _ScatterCopies(len(sums)), sums, rcv, (ssem, rsem), after)
        for n, o, r in zip(names, sums, rcv):
            fulls[n] = _sum_chips(o, r, fulls[n], l, L, idx)

    pending = []
    pair_b = None
    dep = zero_tok
    for l in reversed(range(L)):
        mine = {}

        def after_ffn(arr, l=l, mine=mine, pair_b=pair_b):
            if pair_b is None:
                return zero_tok
            mine["prev_b"] = pair_land_scatter_start(f"{l + 1}b", pair_b, [arr])
            return mine["prev_b"][4]

        def early(parts, l=l, mine=mine):
            br, rest = parts[0], parts[1:]
            mine["pair_a"] = pair_start(f"{l}a", [br.reshape(NSH * 3, rk, D), *rest])
            return mine["pair_a"][4]

        def mid(arr, l=l, mine=mine):
            mine["a"] = pair_land_scatter_start(f"{l}a", mine["pair_a"], [arr])
            return mine["a"][4]

        dx, big, small = _layer_bwd(dx, layers[l], saved[l], S, dep, (after_ffn, early, mid))
        smalls[l] = _pack_small(small)
        for args in pending:
            scatter_land(*args, [dx])
        pending = [(f"{l}a", mine["a"], BIG[1:], l)]
        if "prev_b" in mine:
            pending.append((f"{l + 1}b", mine["prev_b"], BIG[:1], l + 1))
        pair_b = pair_start(f"{l}b", [big["w_in"]])
        dep = pair_b[4]
    last_b = ("0b", pair_land_scatter_start("0b", pair_b, [dx]), BIG[:1], 0)

    packed = jnp.concatenate(smalls, axis=0)
    nrow = packed.shape[0]
    ck = jnp.stack([mc, k_chip]).astype(jnp.int32)
    (sib,) = _send_halves_to_sibling([packed.reshape(1, nrow, D)])
    slots = _pair_sum_slot(packed, sib.reshape(nrow // 2, D), ck)
    small_flight = _split_call("small_start", _GatherCopies(1, halves=False), [], [slots])

    for args in pending:
        scatter_land(*args, [small_flight[4], last_b[1][4]])
    grads, delta, new_m, new_v = {}, {}, {}, {}
    for n, f in zip(BIG[1:], _join_halves([fulls[n] for n in BIG[1:]])):
        grads[n], delta[n], new_m[n], new_v[n] = _adamw(w[n], f.reshape(w[n].shape), mom[n], var[n])
    scatter_land(*last_b, [delta[BIG[-1]]])
    (f,) = _join_halves([fulls[BIG[0]]])
    n = BIG[0]
    grads[n], delta[n], new_m[n], new_v[n] = _adamw(w[n], f.reshape(w[n].shape), mom[n], var[n])

    _, (slots,) = _split_call("small_wait", _GatherCopies(1, halves=False), [], small_flight[3],
                              (small_flight[0], small_flight[1]), [delta[BIG[0]]])
    (small_sum,) = _join_halves([_sum_slots(slots, ck).reshape(1, 1, nrow, D)])
    shapes = {n: (w[n].shape[1:] if n not in ("conv_a_w", "conf_dw_w") else (w[n].shape[1], D)) for n in SMALL_NAMES}
    sg = _unpack_small(small_sum.reshape(L, PACK_ROWS, D), shapes)
    for n in SMALL_NAMES:
        if n in ("conv_a_w", "conf_dw_w"):
            grads[n] = lax.dynamic_slice_in_dim(sg[n], k_chip * rk, rk, axis=2)
        else:
            grads[n] = sg[n]

    for n in SMALL_NAMES:
        sh = w[n].shape
        flat = (sh[0] * sh[1], sh[2]) if n in ("conv_a_w", "conf_dw_w") else (-1, D)
        g, d, nm, nv = _adamw(*(t.reshape(flat) for t in (w[n], grads[n], mom[n], var[n])))
        grads[n], delta[n], new_m[n], new_v[n] = g.reshape(sh), d.reshape(sh), nm.reshape(sh), nv.reshape(sh)

    return (loss, dx.reshape(x.shape), *[grads[n] for n in WEIGHTS], *[delta[n] for n in WEIGHTS],
            *[new_m[n] for n in WEIGHTS], *[new_v[n] for n in WEIGHTS])
```

```python
import functools

import jax
import jax.numpy as jnp
from jax import lax
from jax.experimental import pallas as pl
from jax.experimental.pallas import tpu as pltpu

D = 1024
HEADS = 8
CHUNK = 128
KA = 3
KC = 31
HALO = 32
SUBLANES = 8
MIX_TILE = 512
WGRAD_TILE = 1024
NSH = 4
NDEV = 8
EPS = 1e-6
BF = jnp.bfloat16
F32 = jnp.float32
VMEM_LIMIT = 56 * 1024 * 1024

ADAM_LR = 0.001
ADAM_B1 = 0.9
ADAM_B2 = 0.999
ADAM_EPS = 1e-08
ADAM_WD = 0.01
ADAM_STEP = 10

MESH = pl.DeviceIdType.MESH
ANY = pl.BlockSpec(memory_space=pl.ANY)


def _cp(*sem):
    return pltpu.CompilerParams(dimension_semantics=sem, vmem_limit_bytes=VMEM_LIMIT)


def _sig(x):
    return 1.0 / (1.0 + jnp.exp(-x))


_GC = 0.7978845608028654


def _gelu(x):
    x2 = x * x
    t = jnp.tanh(_GC * x * (1.0 + 0.044715 * x2))
    y = 0.5 * x * (1.0 + t)
    dy = 0.5 * (1.0 + t) + 0.5 * x * (1.0 - t * t) * _GC * (1.0 + 3.0 * 0.044715 * x2)
    return y, dy


def _rms_fwd(x, g):
    r = lax.rsqrt(jnp.mean(x * x, axis=-1, keepdims=True) + EPS)
    return x * r * g


def _rms_bwd(dy, x, g):
    r = lax.rsqrt(jnp.mean(x * x, axis=-1, keepdims=True) + EPS)
    xn = x * r
    dyg = dy * g
    dx = r * (dyg - xn * jnp.mean(dyg * xn, axis=-1, keepdims=True))
    return dx, jnp.sum(dy * xn, axis=0, keepdims=True)


def _ln_stats(x):
    mu = jnp.mean(x, axis=-1, keepdims=True)
    xc = x - mu
    r = lax.rsqrt(jnp.mean(xc * xc, axis=-1, keepdims=True) + EPS)
    return xc * r, r


def _ln_bwd(dn, n, r):
    return r * (dn - jnp.mean(dn, axis=-1, keepdims=True) - n * jnp.mean(dn * n, axis=-1, keepdims=True))


def _dot(a, b):
    return jnp.dot(a, b, preferred_element_type=F32)


def _dot_nt(a, b):
    return lax.dot_general(a, b, (((1,), (1,)), ((), ())), preferred_element_type=F32)


def _dot_tn(a, b):
    return lax.dot_general(a, b, (((0,), (0,)), ((), ())), preferred_element_type=F32)


def _in_proj(x, g, w, dep):
    T = x.shape[0]
    nc = w.shape[2]
    tm = min(T, 1024)
    tn = nc
    nj = nc // tn

    def body(x_ref, g_ref, w_ref, dep_ref, h_ref, z_ref, h_scr):
        @pl.when((pl.program_id(1) == 0) & (pl.program_id(2) == 0))
        def _():
            h = _rms_fwd(x_ref[...], g_ref[...]).astype(BF)
            h_scr[...] = h
            h_ref[...] = h
        z_ref[...] = _dot(h_scr[...], w_ref[...]).astype(BF)

    return pl.pallas_call(
        body, name="in_proj", grid=(T // tm, NSH, nj),
        in_specs=[pl.BlockSpec((tm, D), lambda i, k, j: (i, 0)),
                  pl.BlockSpec((1, D), lambda i, k, j: (0, 0)),
                  pl.BlockSpec((None, D, tn), lambda i, k, j: (k, 0, j)), ANY],
        out_specs=[pl.BlockSpec((tm, D), lambda i, k, j: (i, 0)),
                   pl.BlockSpec((tm, tn), lambda i, k, j: (i, k * nj + j))],
        out_shape=[jax.ShapeDtypeStruct((T, D), BF), jax.ShapeDtypeStruct((T, NSH * nc), BF)],
        scratch_shapes=[pltpu.VMEM((tm, D), BF)],
        compiler_params=_cp("arbitrary", "arbitrary", "arbitrary"),
    )(x, g, w, dep)


def _tile_specs(tt, nt_total, reverse):
    def tile(i):
        return (nt_total - 1 - i) if reverse else i

    def cur(c):
        return pl.BlockSpec((tt, D), lambda i, *_: (tile(i), c))

    def halo(c):
        return pl.BlockSpec((HALO, D), lambda i, *_: (jnp.maximum(tile(i) * (tt // HALO) - 1, 0), c))

    def row(r=1):
        return pl.BlockSpec((r, D), lambda i, *_: (0, 0))

    return tile, cur, halo, row


RC = 16


def _chunks(tt, fn, group=2):
    def step(c, carry):
        for u in range(group):
            fn(pl.multiple_of((c * group + u) * RC, RC))
        return carry
    lax.fori_loop(0, tt // (RC * group), step, 0)


def _chunk_pairs(tt, fn):
    def step(c, carry):
        fn(pl.multiple_of(c * 2 * RC, RC), pl.multiple_of(c * 2 * RC + RC, RC))
        return carry
    lax.fori_loop(0, tt // (2 * RC), step, 0)


ALL_SHIFTS = tuple(range(SUBLANES))


def _shifts_of(offs):
    return tuple(sorted({o % SUBLANES for o in offs}))


def _shifted_copies(ext, sh, nrows, shifts=ALL_SHIFTS):
    for i, s in enumerate(shifts):
        sh[i] = ext[pl.ds(s, nrows), :]


def _window(sh, o, r0, shifts=ALL_SHIFTS):
    return sh[shifts.index(o % SUBLANES), pl.ds(r0 + (o // SUBLANES) * SUBLANES, RC), :]


def _fill_taps(wb, w_ref, ntap):
    for k in range(ntap):
        wb[k * SUBLANES:(k + 1) * SUBLANES, :] = jnp.broadcast_to(w_ref[k:k + 1, :], (SUBLANES, D))


def _conv_chunks(sh, wb, offs, r0s, shifts=ALL_SHIFTS):
    accs = []
    for r0 in r0s:
        acc = None
        for k, o in enumerate(offs):
            wk = wb[k * SUBLANES:(k + 1) * SUBLANES, :]
            term = jnp.concatenate([wk] * (RC // SUBLANES), axis=0) * _window(sh, o, r0, shifts)
            acc = term if acc is None else acc + term
        accs.append(acc)
    return accs


WG_TAPS = 5


def _conv_wgrad_chunked(dw_ref, d_ref, sh, offs, tt, shifts=ALL_SHIFTS):
    for g0 in range(0, len(offs), WG_TAPS):
        grp = offs[g0:g0 + WG_TAPS]

        def step(c, accs, grp=grp):
            for u in range(2):
                r0 = pl.multiple_of((2 * c + u) * SUBLANES, SUBLANES)
                d = d_ref[pl.ds(r0, SUBLANES), :]
                accs = tuple(
                    a + d * sh[shifts.index(o % SUBLANES), pl.ds(r0 + (o // SUBLANES) * SUBLANES, SUBLANES), :]
                    for a, o in zip(accs, grp))
            return accs
        accs = lax.fori_loop(0, tt // (2 * SUBLANES), step,
                             tuple(jnp.zeros((SUBLANES, D), F32) for _ in grp))
        for j, a in enumerate(accs):
            dw_ref[g0 + j:g0 + j + 1, :] += jnp.sum(a, axis=0, keepdims=True)


def _causal_offsets(ntap):
    return [HALO - (ntap - 1) + k for k in range(ntap)]


def _anticausal_offsets(ntap):
    return [ntap - 1 - k for k in range(ntap)]


def _mix_a_fwd(z, wa, S):
    T = z.shape[0]
    tt = min(S, MIX_TILE)
    nt = S // tt
    _, cur, halo, row = _tile_specs(tt, T // tt, False)

    nrows = HALO + tt
    offs = _causal_offsets(KA)
    shifts = _shifts_of(offs)

    def body(ah, ab, ac, ah_h, ac_h, w_ref, y_ref, ext, sh, wb):
        @pl.when(pl.program_id(0) == 0)
        def _():
            _fill_taps(wb, w_ref, KA)
            ext[nrows:, :] = jnp.zeros((SUBLANES, D), F32)

        first = (pl.program_id(0) % nt) == 0
        ph = ah_h[...].astype(F32) * ac_h[...].astype(F32)
        ext[0:HALO, :] = jnp.where(first, 0.0, ph)

        def prod(r0):
            rows = pl.ds(r0, RC)
            ext[pl.ds(HALO + r0, RC), :] = ah[rows, :].astype(F32) * ac[rows, :].astype(F32)
        _chunks(tt, prod)
        _shifted_copies(ext, sh, nrows, shifts)

        def conv(*r0s):
            for r0, q in zip(r0s, _conv_chunks(sh, wb, offs, r0s, shifts)):
                rows = pl.ds(r0, RC)
                y_ref[rows, :] = (ab[rows, :].astype(F32) * q).astype(BF)
        _chunk_pairs(tt, conv)

    return pl.pallas_call(
        body, name="mix_a_fwd", grid=(T // tt,),
        in_specs=[cur(0), cur(1), cur(2), halo(0), halo(2), row(KA)],
        out_specs=pl.BlockSpec((tt, D), lambda i: (i, 0)),
        out_shape=jax.ShapeDtypeStruct((T, D), BF),
        scratch_shapes=[pltpu.VMEM((nrows + SUBLANES, D), F32), pltpu.VMEM((len(shifts), nrows, D), F32),
                        pltpu.VMEM((KA * SUBLANES, D), F32)],
        compiler_params=_cp("arbitrary"),
    )(z, z, z, z, z, wa)


def _mix_b_fwd(z, wc, bc, lg, lb, S):
    T = z.shape[0]
    tt = min(S, MIX_TILE)
    nt = S // tt
    _, cur, halo, row = _tile_specs(tt, T // tt, False)

    nrows = HALO + tt
    offs = _causal_offsets(KC)

    def body(ca, cg, ca_h, cg_h, w_ref, bc_ref, lg_ref, lb_ref, y_ref, s_ref, ext, sh, wb):
        @pl.when(pl.program_id(0) == 0)
        def _():
            _fill_taps(wb, w_ref, KC)
            ext[nrows:, :] = jnp.zeros((SUBLANES, D), F32)

        first = (pl.program_id(0) % nt) == 0
        rh = ca_h[...].astype(F32) * _sig(cg_h[...].astype(F32))
        ext[0:HALO, :] = jnp.where(first, 0.0, rh)

        def glu(r0):
            rows = pl.ds(r0, RC)
            ext[pl.ds(HALO + r0, RC), :] = ca[rows, :].astype(F32) * _sig(cg[rows, :].astype(F32))
        _chunks(tt, glu)
        _shifted_copies(ext, sh, nrows)

        def conv(*r0s):
            for r0, q in zip(r0s, _conv_chunks(sh, wb, offs, r0s)):
                rows = pl.ds(r0, RC)
                s = q + bc_ref[...]
                s_ref[rows, :] = s.astype(BF)
                n, _ = _ln_stats(s)
                t = n * lg_ref[...] + lb_ref[...]
                y_ref[rows, :] = (t * _sig(t)).astype(BF)
        _chunk_pairs(tt, conv)

    return pl.pallas_call(
        body, name="mix_b_fwd", grid=(T // tt,),
        in_specs=[cur(3), cur(4), halo(3), halo(4), row(KC), row(), row(), row()],
        out_specs=[pl.BlockSpec((tt, D), lambda i: (i, 0))] * 2,
        out_shape=[jax.ShapeDtypeStruct((T, D), BF)] * 2,
        scratch_shapes=[pltpu.VMEM((nrows + SUBLANES, D), F32), pltpu.VMEM((SUBLANES, nrows, D), F32),
                        pltpu.VMEM((KC * SUBLANES, D), F32)],
        compiler_params=_cp("arbitrary"),
    )(z, z, z, z, wc, bc, lg, lb)


def _causal_mask(transposed):
    r = lax.broadcasted_iota(jnp.int32, (CHUNK, CHUNK), 0)
    c = lax.broadcasted_iota(jnp.int32, (CHUNK, CHUNK), 1)
    return (c >= r) if transposed else (r >= c)


def _mix_s_fwd(z, lg, lb, ws, bst, S):
    T = z.shape[0]
    tt = min(S, MIX_TILE)
    _, cur, _, row = _tile_specs(tt, T // tt, False)

    def body(su, sv, lg_ref, lb_ref, ws_ref, bst_ref, y_ref, u_scr, vn_scr):
        u_scr[...] = _gelu(su[...].astype(F32))[0]
        n, _ = _ln_stats(_gelu(sv[...].astype(F32))[0])
        vn_scr[...] = (n * lg_ref[...] + lb_ref[...]).astype(BF)
        mask = _causal_mask(False)
        for h in range(HEADS):
            wm = jnp.where(mask, ws_ref[h], 0.0).astype(BF)
            cols = slice(h * CHUNK, (h + 1) * CHUNK)
            for c in range(tt // CHUNK):
                rows = slice(c * CHUNK, (c + 1) * CHUNK)
                mixed = _dot(wm, vn_scr[rows, cols]) + bst_ref[:, h:h + 1]
                y_ref[rows, cols] = (u_scr[rows, cols] * mixed).astype(BF)

    return pl.pallas_call(
        body, name="mix_s_fwd", grid=(T // tt,),
        in_specs=[cur(5), cur(6), row(), row(),
                  pl.BlockSpec((HEADS, CHUNK, CHUNK), lambda i: (0, 0, 0)),
                  pl.BlockSpec((CHUNK, HEADS), lambda i: (0, 0))],
        out_specs=pl.BlockSpec((tt, D), lambda i: (i, 0)),
        out_shape=jax.ShapeDtypeStruct((T, D), BF),
        scratch_shapes=[pltpu.VMEM((tt, D), F32), pltpu.VMEM((tt, D), BF)],
        compiler_params=_cp("arbitrary"),
    )(z, z, lg, lb, ws, bst)


def _mix_out_fwd(ya, yc, ys, z, x, wb, wo, gp, dep):
    T = x.shape[0]
    tm = min(T, 512)
    rk = D // NSH

    def body(ya_ref, yc_ref, ys_ref, ga, gc, gs, x_ref, wb_ref, wo_ref, gp_ref, dep_ref,
             p_ref, mg_ref, m_ref, x1_ref):
        acc = None
        for b, (y_ref, g_ref) in enumerate(((ya_ref, ga), (yc_ref, gc), (ys_ref, gs))):
            pb = _dot(y_ref[...], wb_ref[:, b].reshape(D, D))
            p_ref[b] = pb.astype(BF)
            term = _sig(g_ref[...].astype(F32)) * pb
            acc = term if acc is None else acc + term
        mg = acc.astype(BF)
        mg_ref[...] = mg
        m = _dot(mg, wo_ref[...])
        m_ref[...] = m.astype(BF)
        x1_ref[...] = x_ref[...] + _rms_fwd(m, gp_ref[...])

    rowblk = pl.BlockSpec((tm, D), lambda i: (i, 0))
    return pl.pallas_call(
        body, name="mix_out_fwd", grid=(T // tm,),
        in_specs=[rowblk, rowblk, rowblk,
                  pl.BlockSpec((tm, D), lambda i: (i, 7)), pl.BlockSpec((tm, D), lambda i: (i, 8)),
                  pl.BlockSpec((tm, D), lambda i: (i, 9)), rowblk,
                  pl.BlockSpec((NSH, 3, rk, D), lambda i: (0, 0, 0, 0)),
                  pl.BlockSpec((D, D), lambda i: (0, 0)),
                  pl.BlockSpec((1, D), lambda i: (0, 0)), ANY],
        out_specs=[pl.BlockSpec((3, tm, D), lambda i: (0, i, 0)), rowblk, rowblk, rowblk],
        out_shape=[jax.ShapeDtypeStruct((3, T, D), BF), jax.ShapeDtypeStruct((T, D), BF),
                   jax.ShapeDtypeStruct((T, D), BF), jax.ShapeDtypeStruct((T, D), F32)],
        compiler_params=_cp("arbitrary"),
    )(ya, yc, ys, z, z, z, x, wb, wo, gp, dep)


def _ffn_fwd(x1, g3, w1, w2, g4):
    T = x1.shape[0]
    tm = min(T, 1024)

    def body(x_ref, g3_ref, w1_ref, w2_ref, g4_ref, h_ref, a_ref, f_ref, x2_ref, h_scr, acc):
        k = pl.program_id(1)

        @pl.when(k == 0)
        def _():
            h = _rms_fwd(x_ref[...], g3_ref[...]).astype(BF)
            h_scr[...] = h
            h_ref[...] = h
            acc[...] = jnp.zeros_like(acc)

        a = _dot(h_scr[...], w1_ref[...])
        a_ref[...] = a.astype(BF)
        r = jnp.maximum(a, 0.0)
        acc[...] += _dot((r * r).astype(BF), w2_ref[...])

        @pl.when(k == NSH - 1)
        def _():
            f = acc[...]
            f_ref[...] = f.astype(BF)
            x2_ref[...] = x_ref[...] + _rms_fwd(f, g4_ref[...])

    rowblk = pl.BlockSpec((tm, D), lambda i, k: (i, 0))
    vec = pl.BlockSpec((1, D), lambda i, k: (0, 0))
    return pl.pallas_call(
        body, name="ffn_fwd", grid=(T // tm, NSH),
        in_specs=[rowblk, vec, pl.BlockSpec((None, D, D), lambda i, k: (k, 0, 0)),
                  pl.BlockSpec((D, D), lambda i, k: (k, 0)), vec],
        out_specs=[rowblk, pl.BlockSpec((tm, D), lambda i, k: (i, k)), rowblk, rowblk],
        out_shape=[jax.ShapeDtypeStruct((T, D), BF), jax.ShapeDtypeStruct((T, NSH * D), BF),
                   jax.ShapeDtypeStruct((T, D), BF), jax.ShapeDtypeStruct((T, D), F32)],
        scratch_shapes=[pltpu.VMEM((tm, D), BF), pltpu.VMEM((tm, D), F32)],
        compiler_params=_cp("arbitrary", "arbitrary"),
    )(x1, g3, w1, w2, g4)


def _loss_head(y, target):
    T = y.shape[0]
    tm = min(T, 512)

    def body(y_ref, t_ref, dy_ref, l_ref):
        @pl.when(pl.program_id(0) == 0)
        def _():
            l_ref[...] = jnp.zeros_like(l_ref)
        e = y_ref[...] - t_ref[...]
        dy_ref[...] = e * (1.0 / D)
        l_ref[...] += jnp.sum(e * e) * (0.5 / D)

    rowblk = pl.BlockSpec((tm, D), lambda i: (i, 0))
    return pl.pallas_call(
        body, name="loss_head", grid=(T // tm,),
        in_specs=[rowblk, rowblk],
        out_specs=[rowblk, pl.BlockSpec((1, 128), lambda i: (0, 0))],
        out_shape=[jax.ShapeDtypeStruct((T, D), F32), jax.ShapeDtypeStruct((1, 128), F32)],
        compiler_params=_cp("arbitrary"),
    )(y, target)


def _ffn_bwd(dx2, f, g4, a, w2, w1, x1, g3, dep):
    T = dx2.shape[0]
    tm = min(T, 1024)
    tf = 512
    per = w1.shape[2] // tf
    nk = NSH * per

    def body(dx2_ref, f_ref, g4_ref, a_ref, w2_ref, w1_ref, x1_ref, g3_ref, dep_ref,
             df_ref, da_ref, dx1_ref, dg4_ref, dg3_ref, df_scr, acc):
        i, k = pl.program_id(0), pl.program_id(1)

        @pl.when((i == 0) & (k == 0))
        def _():
            dg4_ref[...] = jnp.zeros_like(dg4_ref)
            dg3_ref[...] = jnp.zeros_like(dg3_ref)

        @pl.when(k == 0)
        def _():
            df, dg = _rms_bwd(dx2_ref[...], f_ref[...].astype(F32), g4_ref[...])
            dg4_ref[...] += dg
            dfb = df.astype(BF)
            df_scr[...] = dfb
            df_ref[...] = dfb
            acc[...] = jnp.zeros_like(acc)

        av = a_ref[...].astype(F32)
        da = (_dot_nt(df_scr[...], w2_ref[...]) * (2.0 * jnp.maximum(av, 0.0))).astype(BF)
        da_ref[...] = da
        acc[...] += _dot_nt(da, w1_ref[...])

        @pl.when(k == nk - 1)
        def _():
            dx, dg = _rms_bwd(acc[...], x1_ref[...], g3_ref[...])
            dg3_ref[...] += dg
            dx1_ref[...] = dx2_ref[...] + dx

    rowblk = pl.BlockSpec((tm, D), lambda i, k: (i, 0))
    vec = pl.BlockSpec((1, D), lambda i, k: (0, 0))
    return pl.pallas_call(
        body, name="ffn_bwd", grid=(T // tm, nk),
        in_specs=[rowblk, rowblk, vec, pl.BlockSpec((tm, tf), lambda i, k: (i, k)),
                  pl.BlockSpec((tf, D), lambda i, k: (k, 0)),
                  pl.BlockSpec((None, D, tf), lambda i, k: (k // per, 0, k % per)), rowblk, vec, ANY],
        out_specs=[rowblk, pl.BlockSpec((tm, tf), lambda i, k: (i, k)), rowblk, vec, vec],
        out_shape=[jax.ShapeDtypeStruct((T, D), BF), jax.ShapeDtypeStruct((T, NSH * D), BF),
                   jax.ShapeDtypeStruct((T, D), F32), jax.ShapeDtypeStruct((1, D), F32),
                   jax.ShapeDtypeStruct((1, D), F32)],
        scratch_shapes=[pltpu.VMEM((tm, D), BF), pltpu.VMEM((tm, D), F32)],
        compiler_params=_cp("arbitrary", "arbitrary"),
    )(dx2, f, g4, a, w2, w1, x1, g3, dep)


def _wgrad(name, ops, grid, in_specs, out_spec, out_shape, acc_shape, pick=None, relu2=False):
    nt = grid[-1]
    na = len(ops) - 1

    def body(*refs):
        a_refs, b_ref, o_ref, acc = refs[:na], refs[na], refs[na + 1], refs[na + 2]
        t = pl.program_id(len(grid) - 1)

        @pl.when(t == 0)
        def _():
            acc[...] = jnp.zeros_like(acc)

        def add(a_ref):
            av = a_ref[...]
            if relu2:
                r = jnp.maximum(av.astype(F32), 0.0)
                av = (r * r).astype(BF)
            acc[...] += _dot_tn(av, b_ref[...])

        if na == 1:
            add(a_refs[0])
        else:
            sel = pick()
            for n in range(na):
                pl.when(sel == n)(functools.partial(add, a_refs[n]))

        @pl.when(t == nt - 1)
        def _():
            if len(o_ref.shape) == 2:
                o_ref[...] = acc[...].astype(o_ref.dtype)
            else:
                rs = o_ref.shape[1]
                for q in range(o_ref.shape[0]):
                    o_ref[q] = acc[q * rs:(q + 1) * rs, :].astype(o_ref.dtype)

    return pl.pallas_call(
        body, name=name, grid=grid, in_specs=in_specs, out_specs=out_spec, out_shape=out_shape,
        scratch_shapes=[pltpu.VMEM(acc_shape, F32)],
        compiler_params=_cp(*(["arbitrary"] * len(grid))),
    )(*ops)


def _mix_out_bwd(dx1, m, gp, wo, p3, z, wb, dep):
    T = dx1.shape[0]
    tm = min(T, 512)
    rk = D // NSH
    ch = 64

    def body(dx1_ref, m_ref, gp_ref, wo_ref, p_ref, g_ref, wb_ref, dep_ref,
             dm_ref, dp_ref, dy_ref, dz_ref, dgp_ref, dmg):
        i, b = pl.program_id(0), pl.program_id(1)

        @pl.when((i == 0) & (b == 0))
        def _():
            dgp_ref[...] = jnp.zeros_like(dgp_ref)

        @pl.when(b == 0)
        def _():
            def norm(c, dg):
                rows = pl.ds(pl.multiple_of(c * ch, ch), ch)
                dm, dgc = _rms_bwd(dx1_ref[rows, :], m_ref[rows, :].astype(F32), gp_ref[...])
                dm_ref[rows, :] = dm.astype(BF)
                return dg + dgc
            dgp_ref[...] += lax.fori_loop(0, tm // ch, norm, jnp.zeros((1, D), F32))
            dmg[...] = _dot_nt(dm_ref[...], wo_ref[...])

        def gating(c, carry):
            rows = pl.ds(pl.multiple_of(c * ch, ch), ch)
            gate = _sig(g_ref[rows, :].astype(F32))
            d = dmg[rows, :]
            dp_ref[rows, :] = (d * gate).astype(BF)
            dz_ref[rows, :] = (d * p_ref[rows, :].astype(F32) * gate * (1.0 - gate)).astype(BF)
            return carry
        lax.fori_loop(0, tm // ch, gating, 0)
        dy_ref[...] = _dot_nt(dp_ref[...], wb_ref[:, b].reshape(D, D)).astype(BF)

    rowblk = pl.BlockSpec((tm, D), lambda i, b: (i, 0))
    br = pl.BlockSpec((None, tm, D), lambda i, b: (b, i, 0))
    vec = pl.BlockSpec((1, D), lambda i, b: (0, 0))
    return pl.pallas_call(
        body, name="mix_out_bwd", grid=(T // tm, 3),
        in_specs=[rowblk, rowblk, vec, pl.BlockSpec((D, D), lambda i, b: (0, 0)), br,
                  pl.BlockSpec((tm, D), lambda i, b: (i, 7 + b)),
                  pl.BlockSpec((NSH, 3, rk, D), lambda i, b: (0, 0, 0, 0)), ANY],
        out_specs=[rowblk, br, br, pl.BlockSpec((tm, D), lambda i, b: (i, 7 + b)), vec],
        out_shape=[jax.ShapeDtypeStruct((T, D), BF), jax.ShapeDtypeStruct((3, T, D), BF),
                   jax.ShapeDtypeStruct((3, T, D), BF), jax.ShapeDtypeStruct((T, 10 * D), BF),
                   jax.ShapeDtypeStruct((1, D), F32)],
        scratch_shapes=[pltpu.VMEM((tm, D), F32)],
        compiler_params=_cp("arbitrary", "arbitrary"),
    )(dx1, m, gp, wo, p3, z, wb, dep)


def _mix_a_bwd(dz, dy3, z, wa, S, dep):
    T = z.shape[0]
    tt = min(S, MIX_TILE)
    nt = S // tt
    ntt = T // tt
    tile, cur, halo, row = _tile_specs(tt, ntt, True)

    nrows = HALO + tt
    coffs, aoffs = _causal_offsets(KA), _anticausal_offsets(KA)
    cshifts, ashifts = _shifts_of(coffs), _shifts_of(aoffs)

    def body(dz_in, dy_ref, ah, ab, ac, ah_h, ac_h, w_ref, dep_ref, dz_ref, dw_ref, ext_p, ext_d, sh, wb, stage):
        i, b = pl.program_id(0), pl.program_id(1)
        ti = ntt - 1 - i

        @pl.when((i == 0) & (b == 0))
        def _():
            dw_ref[...] = jnp.zeros_like(dw_ref)
            ext_d[...] = jnp.zeros_like(ext_d)
            ext_p[nrows:, :] = jnp.zeros((SUBLANES, D), F32)
            _fill_taps(wb, w_ref, KA)

        @pl.when(b == 0)
        def _():
            first = (ti % nt) == 0
            last = (ti % nt) == nt - 1
            ext_p[0:HALO, :] = jnp.where(first, 0.0, ah_h[...].astype(F32) * ac_h[...].astype(F32))
            ext_d[tt:nrows, :] = jnp.where(last, 0.0, ext_d[0:HALO, :])

            def prod(r0):
                rows = pl.ds(r0, RC)
                ext_p[pl.ds(HALO + r0, RC), :] = ah[rows, :].astype(F32) * ac[rows, :].astype(F32)
            _chunks(tt, prod)
            _shifted_copies(ext_p, sh, nrows, cshifts)

            def mid(*r0s):
                for r0, q in zip(r0s, _conv_chunks(sh, wb, coffs, r0s, cshifts)):
                    rows = pl.ds(r0, RC)
                    dy = dy_ref[rows, :].astype(F32)
                    stage[1, rows, :] = (dy * q).astype(BF)
                    ext_d[rows, :] = dy * ab[rows, :].astype(F32)
            _chunk_pairs(tt, mid)
            _conv_wgrad_chunked(dw_ref, ext_d, sh, coffs, tt, cshifts)
            _shifted_copies(ext_d, sh, nrows, ashifts)

            def fin(*r0s):
                for r0, dp in zip(r0s, _conv_chunks(sh, wb, aoffs, r0s, ashifts)):
                    rows = pl.ds(r0, RC)
                    stage[0, rows, :] = (dp * ac[rows, :].astype(F32)).astype(BF)
                    stage[2, rows, :] = (dp * ah[rows, :].astype(F32)).astype(BF)
            _chunk_pairs(tt, fin)

        dz_ref[...] = stage[b]

    return pl.pallas_call(
        body, name="mix_a_bwd", grid=(ntt, 3),
        in_specs=[ANY, pl.BlockSpec((None, tt, D), lambda i, b: (0, tile(i), 0)),
                  cur(0), cur(1), cur(2), halo(0), halo(2), row(KA), ANY],
        out_specs=[pl.BlockSpec((tt, D), lambda i, b: (tile(i), b)), pl.BlockSpec((KA, D), lambda i, b: (0, 0))],
        out_shape=[jax.ShapeDtypeStruct(dz.shape, BF), jax.ShapeDtypeStruct((KA, D), F32)],
        scratch_shapes=[pltpu.VMEM((nrows + SUBLANES, D), F32), pltpu.VMEM((nrows + SUBLANES, D), F32),
                        pltpu.VMEM((max(len(cshifts), len(ashifts)), nrows, D), F32),
                        pltpu.VMEM((KA * SUBLANES, D), F32), pltpu.VMEM((3, tt, D), BF)],
        input_output_aliases={0: 0},
        compiler_params=_cp("arbitrary", "arbitrary"),
    )(dz, dy3, z, z, z, z, z, wa, dep)


def _mix_b_bwd(dz, dy3, s, z, wc, lg, lb, S):
    T = z.shape[0]
    tt = min(S, MIX_TILE)
    nt = S // tt
    ntt = T // tt
    tile, cur, halo, row = _tile_specs(tt, ntt, True)

    nrows = HALO + tt

    def body(dz_in, dy_ref, s_ref, ca, cg, ca_h, cg_h, w_ref, lg_ref, lb_ref,
             dz_ref, dw_ref, dbc_ref, dlg_ref, dlb_ref, ext_r, ext_d, sh, wb, accs, stage):
        i, b = pl.program_id(0), pl.program_id(1)
        ti = ntt - 1 - i

        @pl.when((i == 0) & (b == 0))
        def _():
            dw_ref[...] = jnp.zeros_like(dw_ref)
            dbc_ref[...] = jnp.zeros_like(dbc_ref)
            dlg_ref[...] = jnp.zeros_like(dlg_ref)
            dlb_ref[...] = jnp.zeros_like(dlb_ref)
            ext_d[...] = jnp.zeros_like(ext_d)
            ext_r[nrows:, :] = jnp.zeros((SUBLANES, D), F32)
            _fill_taps(wb, w_ref, KC)

        @pl.when(b == 0)
        def _():
            first = (ti % nt) == 0
            last = (ti % nt) == nt - 1
            ext_r[0:HALO, :] = jnp.where(first, 0.0, ca_h[...].astype(F32) * _sig(cg_h[...].astype(F32)))
            ext_d[tt:nrows, :] = jnp.where(last, 0.0, ext_d[0:HALO, :])
            accs[...] = jnp.zeros_like(accs)

            def point(r0):
                rows = pl.ds(r0, RC)
                n, r = _ln_stats(s_ref[rows, :].astype(F32))
                t = n * lg_ref[...] + lb_ref[...]
                sg = _sig(t)
                dt = dy_ref[rows, :].astype(F32) * (sg * (1.0 + t * (1.0 - sg)))
                accs[0] += dt * n
                accs[1] += dt
                ds = _ln_bwd(dt * lg_ref[...], n, r)
                accs[2] += ds
                ext_d[rows, :] = ds
                ext_r[pl.ds(HALO + r0, RC), :] = ca[rows, :].astype(F32) * _sig(cg[rows, :].astype(F32))
            _chunks(tt, point, group=4)
            dlg_ref[...] += jnp.sum(accs[0], axis=0, keepdims=True)
            dlb_ref[...] += jnp.sum(accs[1], axis=0, keepdims=True)
            dbc_ref[...] += jnp.sum(accs[2], axis=0, keepdims=True)

            _shifted_copies(ext_r, sh, nrows)
            _conv_wgrad_chunked(dw_ref, ext_d, sh, _causal_offsets(KC), tt)
            _shifted_copies(ext_d, sh, nrows)

            def conv(*r0s):
                for r0, dr in zip(r0s, _conv_chunks(sh, wb, _anticausal_offsets(KC), r0s)):
                    rows = pl.ds(r0, RC)
                    cav = ca[rows, :].astype(F32)
                    sgc = _sig(cg[rows, :].astype(F32))
                    stage[0, rows, :] = (dr * sgc).astype(BF)
                    stage[1, rows, :] = (dr * cav * sgc * (1.0 - sgc)).astype(BF)
            _chunk_pairs(tt, conv)

        dz_ref[...] = stage[b]

    vec = pl.BlockSpec((1, D), lambda i, b: (0, 0))
    return pl.pallas_call(
        body, name="mix_b_bwd", grid=(ntt, 2),
        in_specs=[ANY, pl.BlockSpec((None, tt, D), lambda i, b: (1, tile(i), 0)),
                  pl.BlockSpec((tt, D), lambda i, b: (tile(i), 0)),
                  cur(3), cur(4), halo(3), halo(4), row(KC), row(), row()],
        out_specs=[pl.BlockSpec((tt, D), lambda i, b: (tile(i), 3 + b)),
                   pl.BlockSpec((KC, D), lambda i, b: (0, 0)), vec, vec, vec],
        out_shape=[jax.ShapeDtypeStruct(dz.shape, BF), jax.ShapeDtypeStruct((KC, D), F32)]
        + [jax.ShapeDtypeStruct((1, D), F32)] * 3,
        scratch_shapes=[pltpu.VMEM((nrows + SUBLANES, D), F32), pltpu.VMEM((nrows + SUBLANES, D), F32),
                        pltpu.VMEM((SUBLANES, nrows, D), F32), pltpu.VMEM((KC * SUBLANES, D), F32),
                        pltpu.VMEM((3, RC, D), F32), pltpu.VMEM((2, tt, D), BF)],
        input_output_aliases={0: 0},
        compiler_params=_cp("arbitrary", "arbitrary"),
    )(dz, dy3, s, z, z, z, z, wc, lg, lb)


def _mix_s_bwd(dz, dy3, z, lg, lb, ws, wst, bst, S):
    T = z.shape[0]
    tt = min(S, MIX_TILE)
    ntt = T // tt
    _, cur, _, row = _tile_specs(tt, ntt, False)

    def body(dz_in, dy_ref, su, sv, lg_ref, lb_ref, ws_ref, wst_ref, bst_ref,
             dz_ref, dws_ref, dbst_ref, dlg_ref, dlb_ref, u_scr, vn_scr, dvn_scr, stage):
        i, b = pl.program_id(0), pl.program_id(1)

        @pl.when((i == 0) & (b == 0))
        def _():
            dws_ref[...] = jnp.zeros_like(dws_ref)
            dbst_ref[...] = jnp.zeros_like(dbst_ref)
            dlg_ref[...] = jnp.zeros_like(dlg_ref)
            dlb_ref[...] = jnp.zeros_like(dlb_ref)

        @pl.when(b == 0)
        def _():
            u, du_dx = _gelu(su[...].astype(F32))
            v, dv_dx = _gelu(sv[...].astype(F32))
            u_scr[...] = u
            n, r = _ln_stats(v)
            vn_scr[...] = (n * lg_ref[...] + lb_ref[...]).astype(BF)
            mask = _causal_mask(False)
            mask_t = _causal_mask(True)
            for h in range(HEADS):
                wm = jnp.where(mask, ws_ref[h], 0.0).astype(BF)
                wmt = jnp.where(mask_t, wst_ref[h], 0.0).astype(BF)
                cols = slice(h * CHUNK, (h + 1) * CHUNK)
                for c in range(tt // CHUNK):
                    rows = slice(c * CHUNK, (c + 1) * CHUNK)
                    vb = vn_scr[rows, cols]
                    mixed = _dot(wm, vb) + bst_ref[:, h:h + 1]
                    dy = dy_ref[rows, cols].astype(F32)
                    dmix = dy * u_scr[rows, cols]
                    u_scr[rows, cols] = dy * mixed
                    dbst_ref[:, h:h + 1] += jnp.sum(dmix, axis=1, keepdims=True)
                    dmb = dmix.astype(BF)
                    dws_ref[h] += _dot_nt(dmb, vb)
                    dvn_scr[rows, cols] = _dot(wmt, dmb)
            stage[0] = (u_scr[...] * du_dx).astype(BF)
            dvn = dvn_scr[...]
            dlg_ref[...] += jnp.sum(dvn * n, axis=0, keepdims=True)
            dlb_ref[...] += jnp.sum(dvn, axis=0, keepdims=True)
            stage[1] = (_ln_bwd(dvn * lg_ref[...], n, r) * dv_dx).astype(BF)

        dz_ref[...] = stage[b]

    vec = pl.BlockSpec((1, D), lambda i, b: (0, 0))
    wsp = pl.BlockSpec((HEADS, CHUNK, CHUNK), lambda i, b: (0, 0, 0))
    bsp = pl.BlockSpec((CHUNK, HEADS), lambda i, b: (0, 0))
    return pl.pallas_call(
        body, name="mix_s_bwd", grid=(ntt, 2),
        in_specs=[ANY, pl.BlockSpec((None, tt, D), lambda i, b: (2, i, 0)),
                  cur(5), cur(6), row(), row(), wsp, wsp, bsp],
        out_specs=[pl.BlockSpec((tt, D), lambda i, b: (i, 5 + b)), wsp, bsp, vec, vec],
        out_shape=[jax.ShapeDtypeStruct(dz.shape, BF), jax.ShapeDtypeStruct((HEADS, CHUNK, CHUNK), F32),
                   jax.ShapeDtypeStruct((CHUNK, HEADS), F32), jax.ShapeDtypeStruct((1, D), F32),
                   jax.ShapeDtypeStruct((1, D), F32)],
        scratch_shapes=[pltpu.VMEM((tt, D), F32), pltpu.VMEM((tt, D), BF), pltpu.VMEM((tt, D), F32),
                        pltpu.VMEM((2, tt, D), BF)],
        input_output_aliases={0: 0},
        compiler_params=_cp("arbitrary", "arbitrary"),
    )(dz, dy3, z, z, lg, lb, ws, wst, bst)


def _in_proj_bwd(dz, w, x, g, dx1, dep):
    T = x.shape[0]
    nc = w.shape[2]
    tm = min(T, 1024)
    tn = nc
    nj = nc // tn
    ep = min(tm, 128)

    def body(dz_ref, w_ref, x_ref, g_ref, dx1_ref, dep_ref, dx_ref, dg_ref, acc):
        i, k, j = pl.program_id(0), pl.program_id(1), pl.program_id(2)

        @pl.when((i == 0) & (k == 0) & (j == 0))
        def _():
            dg_ref[...] = jnp.zeros_like(dg_ref)

        @pl.when((k == 0) & (j == 0))
        def _():
            acc[...] = jnp.zeros_like(acc)

        acc[...] += _dot_nt(dz_ref[...], w_ref[...])

        @pl.when((k == NSH - 1) & (j == nj - 1))
        def _():
            def step(c, dg):
                rows = pl.ds(pl.multiple_of(c * ep, ep), ep)
                dx, dgc = _rms_bwd(acc[rows, :], x_ref[rows, :], g_ref[...])
                dx_ref[rows, :] = dx1_ref[rows, :] + dx
                return dg + dgc
            dg_ref[...] += lax.fori_loop(0, tm // ep, step, jnp.zeros((1, D), F32))

    rowblk = pl.BlockSpec((tm, D), lambda i, k, j: (i, 0))
    vec = pl.BlockSpec((1, D), lambda i, k, j: (0, 0))
    return pl.pallas_call(
        body, name="in_proj_bwd", grid=(T // tm, NSH, nj),
        in_specs=[pl.BlockSpec((tm, tn), lambda i, k, j: (i, k * nj + j)),
                  pl.BlockSpec((None, D, tn), lambda i, k, j: (k, 0, j)), rowblk, vec, rowblk, ANY],
        out_specs=[rowblk, vec],
        out_shape=[jax.ShapeDtypeStruct((T, D), F32), jax.ShapeDtypeStruct((1, D), F32)],
        scratch_shapes=[pltpu.VMEM((tm, D), F32)],
        compiler_params=_cp("arbitrary", "arbitrary", "arbitrary"),
    )(dz, w, x, g, dx1, dep)


def _layer_fwd(x, p, S, dep, late):
    h, z = _in_proj(x, p["g_mix_pre"], p["w_in"], dep)
    ya = _mix_a_fwd(z, p["conv_a_w"], S)
    yc, s = _mix_b_fwd(z, p["conf_dw_w"], p["conf_dw_b"], p["conf_ln_g"], p["conf_ln_b"], S)
    ys = _mix_s_fwd(z, p["sgu_ln_g"], p["sgu_ln_b"], p["sgu_ws"], p["sgu_bt"], S)
    more, dep2 = late(ys)
    p.update(more)
    p3, merged, m, x1 = _mix_out_fwd(ya, yc, ys, z, x, p["w_branch"], p["w_out"], p["g_mix_post"], dep2)
    h2, a, f, x2 = _ffn_fwd(x1, p["g_ffn_pre"], p["w_ff1"], p["w_ff2"], p["g_ffn_post"])
    saved = dict(x=x, h=h, z=z, ya=ya, yc=yc, ys=ys, s=s, p3=p3, merged=merged, m=m, x1=x1, h2=h2, a=a, f=f)
    return x2, saved


def _layer_bwd(dx2, p, sv, S, dep, hooks):
    after_ffn, early, mid = hooks
    T = dx2.shape[0]
    bt = min(T, WGRAD_TILE)
    nt = T // bt
    rk = D // NSH
    df, da, dx1, dg_ffn_post, dg_ffn_pre = _ffn_bwd(dx2, sv["f"], p["g_ffn_post"], sv["a"], p["w_ff2"],
                                                    p["w_ff1"], sv["x1"], p["g_ffn_pre"], dep)
    dw_ff2 = _wgrad("wgrad_ff2", (sv["a"], df), (NSH, nt),
                    [pl.BlockSpec((bt, D), lambda k, t: (t, k)), pl.BlockSpec((bt, D), lambda k, t: (t, 0))],
                    pl.BlockSpec((None, D, D), lambda k, t: (k, 0, 0)),
                    jax.ShapeDtypeStruct((NSH, D, D), BF), (D, D), relu2=True)
    dw_ff1 = _wgrad("wgrad_ff1", (sv["h2"], da), (NSH, nt),
                    [pl.BlockSpec((bt, D), lambda k, t: (t, 0)), pl.BlockSpec((bt, D), lambda k, t: (t, k))],
                    pl.BlockSpec((None, D, D), lambda k, t: (k, 0, 0)),
                    jax.ShapeDtypeStruct((NSH, D, D), BF), (D, D))
    dm, dp3, dy3, dz, dg_mix_post = _mix_out_bwd(dx1, sv["m"], p["g_mix_post"], p["w_out"], sv["p3"], sv["z"],
                                                 p["w_branch"], after_ffn(dx1))
    dw_out = _wgrad("wgrad_out", (sv["merged"], dm), (nt,),
                    [pl.BlockSpec((bt, D), lambda t: (t, 0)), pl.BlockSpec((bt, D), lambda t: (t, 0))],
                    pl.BlockSpec((D, D), lambda t: (0, 0)),
                    jax.ShapeDtypeStruct((D, D), BF), (D, D)).reshape(NSH, rk, D)
    ysp = lambda n: pl.BlockSpec((bt, D), lambda b, t: (jnp.where(b == n, t, 0), 0))
    dw_br = _wgrad("wgrad_branch", (sv["ya"], sv["yc"], sv["ys"], dp3), (3, nt),
                   [ysp(0), ysp(1), ysp(2), pl.BlockSpec((None, bt, D), lambda b, t: (b, t, 0))],
                   pl.BlockSpec((NSH, None, rk, D), lambda b, t: (0, b, 0, 0)),
                   jax.ShapeDtypeStruct((NSH, 3, rk, D), BF), (D, D), pick=lambda: pl.program_id(0))
    dz, dwa = _mix_a_bwd(dz, dy3, sv["z"], p["conv_a_w"], S, early([dw_br, dw_out, dw_ff1, dw_ff2]))
    dz, dwc, dbc, dclg, dclb = _mix_b_bwd(dz, dy3, sv["s"], sv["z"], p["conf_dw_w"], p["conf_ln_g"],
                                          p["conf_ln_b"], S)
    dz, dws, dbst, dslg, dslb = _mix_s_bwd(dz, dy3, sv["z"], p["sgu_ln_g"], p["sgu_ln_b"], p["sgu_ws"],
                                           p["sgu_wst"], p["sgu_bt"], S)
    dx, dg_mix_pre = _in_proj_bwd(dz, p["w_in"], sv["x"], p["g_mix_pre"], dx1, mid(dz))
    tn = p["w_in"].shape[2]
    nj = p["w_in"].shape[2] // tn
    dw_in = _wgrad("wgrad_in", (sv["h"], dz), (NSH, nj, nt),
                   [pl.BlockSpec((bt, D), lambda k, j, t: (t, 0)),
                    pl.BlockSpec((bt, tn), lambda k, j, t: (t, k * nj + j))],
                   pl.BlockSpec((None, D, tn), lambda k, j, t: (k, 0, j)),
                   jax.ShapeDtypeStruct(p["w_in"].shape, BF), (D, tn))
    tril = jnp.tril(jnp.ones((CHUNK, CHUNK), bool))
    small = dict(norm_mix_pre=dg_mix_pre, norm_mix_post=dg_mix_post, norm_ffn_pre=dg_ffn_pre,
                 norm_ffn_post=dg_ffn_post, conv_a_w=dwa, conf_dw_w=dwc, conf_dw_b=dbc, conf_ln_g=dclg,
                 conf_ln_b=dclb, sgu_ln_g=dslg, sgu_ln_b=dslb,
                 sgu_ws=jnp.where(tril[None], dws, 0.0), sgu_b=dbst.T)
    big = dict(w_in=dw_in, w_branch=dw_br, w_out=dw_out, w_ff1=dw_ff1, w_ff2=dw_ff2)
    return dx, big, small


SMALL_NAMES = ("norm_mix_pre", "norm_mix_post", "norm_ffn_pre", "norm_ffn_post", "conv_a_w", "conf_dw_w",
               "conf_dw_b", "conf_ln_g", "conf_ln_b", "sgu_ln_g", "sgu_ln_b", "sgu_b", "sgu_ws")
SMALL_ROWS = dict(norm_mix_pre=1, norm_mix_post=1, norm_ffn_pre=1, norm_ffn_post=1, conv_a_w=KA, conf_dw_w=KC,
                  conf_dw_b=1, conf_ln_g=1, conf_ln_b=1, sgu_ln_g=1, sgu_ln_b=1, sgu_b=1, sgu_ws=CHUNK)
def _pad8(r):
    return -(-r // SUBLANES) * SUBLANES


PACK_ROWS = sum(_pad8(r) for r in SMALL_ROWS.values())


def _pack_small(d):
    parts = []
    for n in SMALL_NAMES:
        r = SMALL_ROWS[n]
        parts.append(jnp.pad(d[n].reshape(r, D).astype(F32), ((0, _pad8(r) - r), (0, 0))))
    return jnp.concatenate(parts, axis=0)


def _unpack_small(a, shapes):
    out, r = {}, 0
    for n in SMALL_NAMES:
        out[n] = a[:, r:r + SMALL_ROWS[n]].reshape((a.shape[0],) + tuple(shapes[n]))
        r += _pad8(SMALL_ROWS[n])
    return out


def _me():
    return lax.axis_index("x"), lax.axis_index("y"), lax.axis_index("c")


def _slab(ref, q, a, h=None):
    r = ref.shape[1]
    rows = slice(None) if h is None else pl.ds(h * (r // 2), r // 2)
    return ref.at[pl.ds(q * a, a), rows, :]


def _rows(ref, h):
    r = ref.shape[-2]
    lead = (slice(None),) * (len(ref.shape) - 2)
    return ref.at[lead + (pl.ds(h * (r // 2), r // 2), slice(None))]


def _rcopy(src, dst, sems, idx, dev):
    return pltpu.make_async_remote_copy(src_ref=src, dst_ref=dst, send_sem=sems[0].at[idx], recv_sem=sems[1].at[idx],
                                        device_id=dev, device_id_type=MESH)


def _send_halves_to_sibling(parts):
    n = len(parts)

    def body(*refs):
        src, dst = refs[:n], refs[n:2 * n]
        sems = refs[2 * n:2 * n + 2]
        x, y, c = _me()
        cps = [_rcopy(_rows(src[i], 1 - c), dst[i], sems, i, (x, y, 1 - c)) for i in range(n)]
        for cp in cps:
            cp.start()
        for cp in cps:
            cp.wait()

    outs = [jax.ShapeDtypeStruct((p.shape[0], p.shape[1] // 2, p.shape[2]), p.dtype) for p in parts]
    return pl.pallas_call(
        body, name="pair_exchange", in_specs=[ANY] * n, out_specs=[ANY] * n, out_shape=outs,
        scratch_shapes=[pltpu.SemaphoreType.DMA((n,)), pltpu.SemaphoreType.DMA((n,))],
    )(*parts)


PAIR_BLOCK_BYTES = 3 * 512 * 1024


def _pair_add(parts, sibs, c):
    n = len(parts)
    steps = 1
    while any(p.shape[0] * (p.shape[1] // 2 // steps) * p.shape[2] * 2 > PAIR_BLOCK_BYTES for p in parts):
        steps *= 2

    def body(c_ref, *refs):
        for p_ref, s_ref, o_ref in zip(refs[:n], refs[n:2 * n], refs[2 * n:]):
            o_ref[...] = (p_ref[...].astype(F32) + s_ref[...].astype(F32)).astype(BF)

    def blk(p):
        return (p.shape[0], p.shape[1] // 2 // steps, p.shape[2])

    mine = [pl.BlockSpec(blk(p), lambda g, c_ref: (0, c_ref[0] * steps + g, 0)) for p in parts]
    same = [pl.BlockSpec(blk(p), lambda g, c_ref: (0, g, 0)) for p in parts]
    return pl.pallas_call(
        body, name="pair_add",
        grid_spec=pltpu.PrefetchScalarGridSpec(
            num_scalar_prefetch=1, grid=(steps,), in_specs=mine + same, out_specs=same),
        out_shape=[jax.ShapeDtypeStruct(s.shape, BF) for s in sibs],
        compiler_params=_cp("arbitrary"),
    )(c, *parts, *sibs)


def _other_chips(x, y):
    return [(1 - x, y), (x, 1 - y), (1 - x, 1 - y)]


def _split_call(name, copies, srcs, lands, sems=None, after=()):
    n, m = len(srcs), len(lands)
    hbm = lambda t: pltpu.HBM(t.shape, t.dtype)
    pin = lambda t: pltpu.with_memory_space_constraint(t, pltpu.HBM)
    thru = [hbm(t) for t in srcs] + [hbm(t) for t in lands]
    sem_spec = pl.BlockSpec(memory_space=pltpu.SEMAPHORE)
    effect = pltpu.CompilerParams(has_side_effects=pltpu.SideEffectType.DATAFLOW_SIDE_EFFECTING)
    if sems is None:
        def start_body(*refs):
            src, land = refs[:n], refs[n:n + m]
            ssem, rsem = refs[n + m + len(after)], refs[n + m + len(after) + 1]
            token = refs[-1]
            cps = copies(src, land, (ssem, rsem))
            for cp in cps:
                cp.start()
            token[...] = jnp.zeros_like(token)

        ncp = copies.count
        out = pl.pallas_call(
            start_body, name=name,
            out_shape=(pltpu.SemaphoreType.DMA((ncp,)), pltpu.SemaphoreType.DMA((ncp,)), *thru,
                       jax.ShapeDtypeStruct((8, 128), F32)),
            in_specs=[ANY] * (n + m + len(after)),
            out_specs=(sem_spec, sem_spec, *([ANY] * (n + m)), pl.BlockSpec(memory_space=pltpu.VMEM)),
            input_output_aliases={i: 2 + i for i in range(n + m)},
            compiler_params=effect,
        )(*[pin(t) for t in srcs], *[pin(t) for t in lands], *after)
        return out[0], out[1], list(out[2:2 + n]), list(out[2 + n:2 + n + m]), out[-1]

    def wait_body(*refs):
        src, land = refs[:n], refs[n:n + m]
        ssem, rsem = refs[n + m], refs[n + m + 1]
        for cp in copies(src, land, (ssem, rsem)):
            cp.wait_send()
            cp.wait_recv()

    out = pl.pallas_call(
        wait_body, name=name, out_shape=tuple(thru),
        in_specs=[ANY] * (n + m) + [sem_spec, sem_spec] + [ANY] * len(after),
        out_specs=tuple([ANY] * (n + m)),
        input_output_aliases={i: i for i in range(n + m)},
        compiler_params=effect,
    )(*srcs, *lands, sems[0], sems[1], *after)
    return list(out[:n]), list(out[n:])


def _cast_into(w, land, layer, kidx, dep):
    _, a, R, C = w.shape
    br = R
    while br * C > 512 * 1024 and br % 32 == 0:
        br //= 2

    def body(k_ref, w_ref, land_ref, dep_ref, o_ref):
        o_ref[...] = w_ref[...].astype(o_ref.dtype)

    return pl.pallas_call(
        body, name="cast_into",
        grid_spec=pltpu.PrefetchScalarGridSpec(
            num_scalar_prefetch=1, grid=(a, R // br),
            in_specs=[pl.BlockSpec((None, None, br, C), lambda e, i, k: (layer, e, i, 0)), ANY, ANY],
            out_specs=pl.BlockSpec((None, br, C), lambda e, i, k: (k[0] * a + e, i, 0))),
        out_shape=jax.ShapeDtypeStruct(land.shape, land.dtype), input_output_aliases={2: 0},
        compiler_params=_cp("arbitrary", "arbitrary"),
    )(kidx, w, land, dep)


class _GatherCopies:
    def __init__(self, n, halves=True):
        self.n, self.count, self.halves = n, 3 * n, halves

    def __call__(self, src, land, sems):
        x, y, c = _me()
        k = 2 * x + y
        cps = []
        for j, (qx, qy) in enumerate(_other_chips(x, y)):
            for i in range(self.n):
                mine = _slab(land[i], k, land[i].shape[0] // NSH, c if self.halves else None)
                cps.append(_rcopy(mine, mine, sems, j * self.n + i, (qx, qy, c)))
        return cps


def _gather_finish(lands):
    n = len(lands)

    def body(*refs):
        dst = refs[n:2 * n]
        sems = refs[2 * n:2 * n + 2]
        x, y, c = _me()
        av = [d.shape[0] // NSH for d in dst]
        cps = []
        for j, (qx, qy) in enumerate(_other_chips(x, y)):
            for i in range(n):
                got = _slab(dst[i], 2 * qx + qy, av[i], c)
                cps.append(_rcopy(got, got, sems, j * n + i, (x, y, 1 - c)))
        for cp in cps:
            cp.start()
        for j, (qx, qy) in enumerate(_other_chips(x, y)):
            for i in range(n):
                other = _slab(dst[i], 2 * qx + qy, av[i], 1 - c)
                _rcopy(other, other, sems, j * n + i, (x, y, c)).wait_recv()
        for cp in cps:
            cp.wait_send()

    return pl.pallas_call(
        body, name="gather_finish", in_specs=[ANY] * n, out_specs=[ANY] * n,
        out_shape=[jax.ShapeDtypeStruct(t.shape, t.dtype) for t in lands],
        input_output_aliases={i: i for i in range(n)},
        scratch_shapes=[pltpu.SemaphoreType.DMA((3 * n,)), pltpu.SemaphoreType.DMA((3 * n,))],
    )(*lands)


class _PairCopies:
    def __init__(self, n):
        self.n, self.count = n, n

    def __call__(self, src, land, sems):
        x, y, c = _me()
        return [_rcopy(_rows(src[i], 1 - c), land[i], sems, i, (x, y, 1 - c)) for i in range(self.n)]


class _ScatterCopies:
    def __init__(self, n):
        self.n, self.count = n, 3 * n

    def __call__(self, src, land, sems):
        x, y, c = _me()
        k = 2 * x + y
        cps = []
        for j, (qx, qy) in enumerate(_other_chips(x, y)):
            for i in range(self.n):
                a = src[i].shape[0] // NSH
                cps.append(_rcopy(_slab(src[i], 2 * qx + qy, a), _slab(land[i], k, a), sems, j * self.n + i,
                                  (qx, qy, c)))
        return cps


def _sum_chips(own, rcv, acc, layer, nlayers, idx):
    A, hr, C = rcv.shape
    a = A // NSH
    br = min(hr, 512)
    nb = hr // br

    def body(*refs):
        r0, r1, r2, r3 = refs[1:5]
        o_ref = refs[-1]
        o_ref[...] = ((r0[...].astype(F32) + r1[...].astype(F32)) + r2[...].astype(F32)) + r3[...].astype(F32)

    slot = lambda s: pl.BlockSpec((None, br, C), lambda e, i, ix: (ix[s] * a + e, i, 0))
    ops = [own, rcv, rcv, rcv]
    in_specs = [slot(0), slot(1), slot(2), slot(3)]
    aliases = {}
    if acc is not None:
        ops.append(acc)
        in_specs.append(ANY)
        aliases = {5: 0}
    return pl.pallas_call(
        body, name="sum_chips",
        grid_spec=pltpu.PrefetchScalarGridSpec(
            num_scalar_prefetch=1, grid=(a, nb), in_specs=in_specs,
            out_specs=pl.BlockSpec((None, None, br, C), lambda e, i, ix: (layer, e, ix[4] * nb + i, 0))),
        out_shape=jax.ShapeDtypeStruct((nlayers, a, 2 * hr, C), F32), input_output_aliases=aliases,
        compiler_params=_cp("arbitrary", "arbitrary"),
    )(idx, *ops)


def _join_halves(fulls):
    n = len(fulls)

    def body(*refs):
        buf = refs[n:2 * n]
        sems = refs[2 * n:2 * n + 2]
        x, y, c = _me()
        cps = [_rcopy(_rows(buf[i], c), _rows(buf[i], c), sems, i, (x, y, 1 - c)) for i in range(n)]
        for cp in cps:
            cp.start()
        for i in range(n):
            _rcopy(_rows(buf[i], 1 - c), _rows(buf[i], 1 - c), sems, i, (x, y, c)).wait_recv()
        for cp in cps:
            cp.wait_send()

    return pl.pallas_call(
        body, name="join_halves", in_specs=[ANY] * n, out_specs=[ANY] * n,
        out_shape=[jax.ShapeDtypeStruct(t.shape, t.dtype) for t in fulls],
        input_output_aliases={i: i for i in range(n)},
        scratch_shapes=[pltpu.SemaphoreType.DMA((n,)), pltpu.SemaphoreType.DMA((n,))],
    )(*fulls)


def _small_blocks(hr):
    br = hr
    while br > 512 and br % 16 == 0:
        br //= 2
    return br, hr // br


def _pair_sum_slot(part, sib, ck):
    R, C = part.shape
    hr = R // 2
    br, nb = _small_blocks(hr)

    def body(ix, p_ref, s_ref, o_ref):
        o_ref[...] = p_ref[...] + s_ref[...]

    return pl.pallas_call(
        body, name="pair_sum_slot",
        grid_spec=pltpu.PrefetchScalarGridSpec(
            num_scalar_prefetch=1, grid=(nb,),
            in_specs=[pl.BlockSpec((br, C), lambda i, ix: (ix[0] * nb + i, 0)),
                      pl.BlockSpec((br, C), lambda i, ix: (i, 0))],
            out_specs=pl.BlockSpec((None, br, C), lambda i, ix: (ix[1], i, 0))),
        out_shape=jax.ShapeDtypeStruct((NSH, hr, C), F32),
        compiler_params=_cp("arbitrary"),
    )(ck, part, sib)


def _sum_slots(slots, ck):
    _, hr, C = slots.shape
    br, nb = _small_blocks(hr)

    def body(ix, s_ref, o_ref):
        o_ref[...] = ((s_ref[0] + s_ref[1]) + s_ref[2]) + s_ref[3]

    return pl.pallas_call(
        body, name="sum_slots",
        grid_spec=pltpu.PrefetchScalarGridSpec(
            num_scalar_prefetch=1, grid=(nb,),
            in_specs=[pl.BlockSpec((NSH, br, C), lambda i, ix: (0, i, 0))],
            out_specs=pl.BlockSpec((br, C), lambda i, ix: (ix[0] * nb + i, 0))),
        out_shape=jax.ShapeDtypeStruct((2 * hr, C), F32),
        compiler_params=_cp("arbitrary"),
    )(ck, slots)


def _adamw(w, g, m, v):
    shape = w.shape
    C = shape[-1]
    R = shape[-2]
    A = 1
    for s in shape[:-2]:
        A *= s
    br = R
    while br * C > 256 * 1024 and br % 16 == 0:
        br //= 2
    c1 = 1.0 / (1.0 - ADAM_B1 ** ADAM_STEP)
    c2 = 1.0 / (1.0 - ADAM_B2 ** ADAM_STEP)

    def body(w_ref, g_ref, m_ref, v_ref, og_ref, d_ref, nm_ref, nv_ref):
        gv = g_ref[...]
        og_ref[...] = gv
        nm = ADAM_B1 * m_ref[...] + (1.0 - ADAM_B1) * gv
        nv = ADAM_B2 * v_ref[...] + (1.0 - ADAM_B2) * (gv * gv)
        nm_ref[...] = nm
        nv_ref[...] = nv
        d_ref[...] = -ADAM_LR * ((nm * c1) / (jnp.sqrt(nv * c2) + ADAM_EPS) + ADAM_WD * w_ref[...])

    blk = pl.BlockSpec((None, br, C), lambda a, i: (a, i, 0))
    outs = pl.pallas_call(
        body, name="adamw", grid=(A, R // br), in_specs=[blk] * 4, out_specs=[blk] * 4,
        out_shape=[jax.ShapeDtypeStruct((A, R, C), F32)] * 4,
        compiler_params=_cp("arbitrary", "arbitrary"),
    )(*(t.reshape(A, R, C) for t in (w, g, m, v)))
    return tuple(o.reshape(shape) for o in outs)


WEIGHTS = ("norm_mix_pre", "norm_mix_post", "norm_ffn_pre", "norm_ffn_post", "w_in", "conv_a_w", "conf_dw_w",
           "conf_dw_b", "conf_ln_g", "conf_ln_b", "sgu_ln_g", "sgu_ln_b", "sgu_ws", "sgu_b", "w_branch", "w_out",
           "w_ff1", "w_ff2")
BIG = ("w_in", "w_branch", "w_out", "w_ff1", "w_ff2")
CONV_ROWS = 48


def kernel(x, norm_mix_pre, norm_mix_post, norm_ffn_pre, norm_ffn_post, w_in, conv_a_w, conf_dw_w, conf_dw_b, conf_ln_g, conf_ln_b, sgu_ln_g, sgu_ln_b, sgu_ws, sgu_b, w_branch, w_out, w_ff1, w_ff2, loss_target, m_norm_mix_pre, m_norm_mix_post, m_norm_ffn_pre, m_norm_ffn_post, m_w_in, m_conv_a_w, m_conf_dw_w, m_conf_dw_b, m_conf_ln_g, m_conf_ln_b, m_sgu_ln_g, m_sgu_ln_b, m_sgu_ws, m_sgu_b, m_w_branch, m_w_out, m_w_ff1, m_w_ff2, v_norm_mix_pre, v_norm_mix_post, v_norm_ffn_pre, v_norm_ffn_post, v_w_in, v_conv_a_w, v_conf_dw_w, v_conf_dw_b, v_conf_ln_g, v_conf_ln_b, v_sgu_ln_g, v_sgu_ln_b, v_sgu_ws, v_sgu_b, v_w_branch, v_w_out, v_w_ff1, v_w_ff2):
    w = dict(norm_mix_pre=norm_mix_pre, norm_mix_post=norm_mix_post, norm_ffn_pre=norm_ffn_pre,
             norm_ffn_post=norm_ffn_post, w_in=w_in, conv_a_w=conv_a_w, conf_dw_w=conf_dw_w, conf_dw_b=conf_dw_b,
             conf_ln_g=conf_ln_g, conf_ln_b=conf_ln_b, sgu_ln_g=sgu_ln_g, sgu_ln_b=sgu_ln_b, sgu_ws=sgu_ws,
             sgu_b=sgu_b, w_branch=w_branch, w_out=w_out, w_ff1=w_ff1, w_ff2=w_ff2)
    mom = dict(norm_mix_pre=m_norm_mix_pre, norm_mix_post=m_norm_mix_post, norm_ffn_pre=m_norm_ffn_pre,
               norm_ffn_post=m_norm_ffn_post, w_in=m_w_in, conv_a_w=m_conv_a_w, conf_dw_w=m_conf_dw_w,
               conf_dw_b=m_conf_dw_b, conf_ln_g=m_conf_ln_g, conf_ln_b=m_conf_ln_b, sgu_ln_g=m_sgu_ln_g,
               sgu_ln_b=m_sgu_ln_b, sgu_ws=m_sgu_ws, sgu_b=m_sgu_b, w_branch=m_w_branch, w_out=m_w_out,
               w_ff1=m_w_ff1, w_ff2=m_w_ff2)
    var = dict(norm_mix_pre=v_norm_mix_pre, norm_mix_post=v_norm_mix_post, norm_ffn_pre=v_norm_ffn_pre,
               norm_ffn_post=v_norm_ffn_post, w_in=v_w_in, conv_a_w=v_conv_a_w, conf_dw_w=v_conf_dw_w,
               conf_dw_b=v_conf_dw_b, conf_ln_g=v_conf_ln_g, conf_ln_b=v_conf_ln_b, sgu_ln_g=v_sgu_ln_g,
               sgu_ln_b=v_sgu_ln_b, sgu_ws=v_sgu_ws, sgu_b=v_sgu_b, w_branch=v_w_branch, w_out=v_w_out,
               w_ff1=v_w_ff1, w_ff2=v_w_ff2)
    L = w_in.shape[0]
    nseq, S, _ = x.shape
    T = nseq * S
    rk = D // NSH
    mx, my, mc = _me()
    k_chip = 2 * mx + my

    big_src = [w_in.reshape(L, 1, D, w_in.shape[2]), w_branch, w_out.reshape(L, 1, rk, D),
               w_ff1.reshape(L, 1, D, w_ff1.shape[2]), w_ff2.reshape(L, 1, w_ff2.shape[1], D)]
    kidx = jnp.reshape(k_chip, (1,)).astype(jnp.int32)
    conv_src = jnp.concatenate(
        [jnp.pad(conv_a_w, ((0, 0), (0, SUBLANES - KA), (0, 0))), jnp.pad(conf_dw_w, ((0, 0), (0, 1), (0, 0))),
         jnp.zeros((L, CONV_ROWS - SUBLANES - KC - 1, rk), F32)], axis=1)[None]

    def early_params(l, g_in, conv_full):
        return dict(
            g_mix_pre=norm_mix_pre[l][None], g_mix_post=norm_mix_post[l][None], g_ffn_pre=norm_ffn_pre[l][None],
            g_ffn_post=norm_ffn_post[l][None], w_in=g_in, conv_a_w=conv_full[l, :KA],
            conf_dw_w=conv_full[l, SUBLANES:SUBLANES + KC], conf_dw_b=conf_dw_b[l][None],
            conf_ln_g=conf_ln_g[l][None], conf_ln_b=conf_ln_b[l][None], sgu_ln_g=sgu_ln_g[l][None],
            sgu_ln_b=sgu_ln_b[l][None], sgu_ws=sgu_ws[l], sgu_wst=jnp.swapaxes(sgu_ws[l], 1, 2),
            sgu_bt=sgu_b[l].T)

    def late_params(gathered):
        g_br, g_out, g_ff1, g_ff2 = gathered
        return dict(w_branch=g_br.reshape(NSH, 3, rk, D), w_out=g_out.reshape(D, D), w_ff1=g_ff1,
                    w_ff2=g_ff2.reshape(NSH * w_ff2.shape[1], D))

    def cast_lands(srcs, l, dep):
        return [_cast_into(s, lax.empty((NSH * s.shape[1],) + s.shape[2:], F32 if s is conv_src else BF), l, kidx,
                           dep) for s in srcs]

    def gather_start(name, lands, after):
        return _split_call(name, _GatherCopies(len(lands)), [], lands, after=after)

    def gather_land(name, flight, after):
        ssem, rsem, _, lands, _ = flight
        _, lands = _split_call(name, _GatherCopies(len(lands)), [], lands, (ssem, rsem), after)
        return _gather_finish(lands)

    zero_tok = jnp.zeros((8, 128), F32)
    xt = x.reshape(T, D)
    layers, saved = [], []
    head = gather_start("gather_start_0a", cast_lands([big_src[0], conv_src], 0, kidx), [])
    tails = [cast_lands(big_src[1:], l, head[4]) for l in range(L)]
    heads = [None] + [cast_lands(big_src[:1], l, head[4]) for l in range(1, L)]
    behind = [xt] + [t for ls in tails + heads[1:] for t in ls]
    conv_full = None
    for l in range(L):
        got = gather_land(f"gather_wait_{l}a", head, behind if l == 0 else [xt])
        g_in = got[0]
        if l == 0:
            conv_full = got[1].reshape(NSH, L, CONV_ROWS, rk).transpose(1, 2, 0, 3).reshape(L, CONV_ROWS, D)
        tail = gather_start(f"gather_start_{l}b", tails[l], [g_in])
        nxt = {}

        def late(after, l=l, tail=tail, nxt=nxt):
            more = late_params(gather_land(f"gather_wait_{l}b", tail, [after]))
            if l + 1 == L:
                return more, zero_tok
            nxt["head"] = gather_start(f"gather_start_{l + 1}a", heads[l + 1], [more["w_ff1"]])
            return more, nxt["head"][4]

        p = early_params(l, g_in, conv_full)
        xt, sv = _layer_fwd(xt, p, S, tail[4], late)
        head = nxt.get("head")
        layers.append(p)
        saved.append(sv)
    dx, loss_row = _loss_head(xt, loss_target.reshape(T, D))
    loss = lax.psum(loss_row[0, 0], ("x", "y", "c"))

    c_arr = jnp.reshape(mc, (1,)).astype(jnp.int32)
    idx = jnp.stack([k_chip, k_chip ^ 2, k_chip ^ 1, k_chip ^ 3, mc]).astype(jnp.int32)
    fulls = {n: None for n in BIG}
    smalls = [None] * L

    def pair_start(tag, parts):
        lands = [lax.empty((p.shape[0], p.shape[1] // 2, p.shape[2]), p.dtype) for p in parts]
        return _split_call(f"pair_start_{tag}", _PairCopies(len(parts)), parts, lands)

    def pair_land_scatter_start(tag, fl, after):
        ssem, rsem, parts, sib, _ = fl
        parts, sib = _split_call(f"pair_wait_{tag}", _PairCopies(len(parts)), parts, sib, (ssem, rsem), after)
        sums = _pair_add(parts, sib, c_arr)
        rcv = [lax.empty(s.shape, s.dtype) for s in sums]
        return _split_call(f"scatter_start_{tag}", _ScatterCopies(len(sums)), sums, rcv)

    def scatter_land(tag, fl, names, l, after):
        ssem, rsem, sums, rcv, _ = fl
        sums, rcv = _split_call(f"scatter_wait_{tag}", _ScatterCopies(len(sums)), sums, rcv, (ssem, rsem), after)
        for n, o, r in zip(names, sums, rcv):
            fulls[n] = _sum_chips(o, r, fulls[n], l, L, idx)

    pending = []
    pair_b = None
    dep = zero_tok
    for l in reversed(range(L)):
        mine = {}

        def after_ffn(arr, l=l, mine=mine, pair_b=pair_b):
            if pair_b is None:
                return zero_tok
            mine["prev_b"] = pair_land_scatter_start(f"{l + 1}b", pair_b, [arr])
            return mine["prev_b"][4]

        def early(parts, l=l, mine=mine):
            br, rest = parts[0], parts[1:]
            mine["pair_a"] = pair_start(f"{l}a", [br.reshape(NSH * 3, rk, D), *rest])
            return mine["pair_a"][4]

        def mid(arr, l=l, mine=mine):
            mine["a"] = pair_land_scatter_start(f"{l}a", mine["pair_a"], [arr])
            return mine["a"][4]

        dx, big, small = _layer_bwd(dx, layers[l], saved[l], S, dep, (after_ffn, early, mid))
        smalls[l] = _pack_small(small)
        for args in pending:
            scatter_land(*args, [dx])
        pending = [(f"{l}a", mine["a"], BIG[1:], l)]
        if "prev_b" in mine:
            pending.append((f"{l + 1}b", mine["prev_b"], BIG[:1], l + 1))
        pair_b = pair_start(f"{l}b", [big["w_in"]])
        dep = pair_b[4]
    last_b = ("0b", pair_land_scatter_start("0b", pair_b, [dx]), BIG[:1], 0)

    packed = jnp.concatenate(smalls, axis=0)
    nrow = packed.shape[0]
    ck = jnp.stack([mc, k_chip]).astype(jnp.int32)
    (sib,) = _send_halves_to_sibling([packed.reshape(1, nrow, D)])
    slots = _pair_sum_slot(packed, sib.reshape(nrow // 2, D), ck)
    small_flight = _split_call("small_start", _GatherCopies(1, halves=False), [], [slots])

    for args in pending:
        scatter_land(*args, [small_flight[4], last_b[1][4]])
    grads, delta, new_m, new_v = {}, {}, {}, {}
    for n, f in zip(BIG[1:], _join_halves([fulls[n] for n in BIG[1:]])):
        grads[n], delta[n], new_m[n], new_v[n] = _adamw(w[n], f.reshape(w[n].shape), mom[n], var[n])
    scatter_land(*last_b, [delta[BIG[-1]]])
    (f,) = _join_halves([fulls[BIG[0]]])
    n = BIG[0]
    grads[n], delta[n], new_m[n], new_v[n] = _adamw(w[n], f.reshape(w[n].shape), mom[n], var[n])

    _, (slots,) = _split_call("small_wait", _GatherCopies(1, halves=False), [], small_flight[3],
                              (small_flight[0], small_flight[1]), [delta[BIG[0]]])
    (small_sum,) = _join_halves([_sum_slots(slots, ck).reshape(1, 1, nrow, D)])
    shapes = {n: (w[n].shape[1:] if n not in ("conv_a_w", "conf_dw_w") else (w[n].shape[1], D)) for n in SMALL_NAMES}
    sg = _unpack_small(small_sum.reshape(L, PACK_ROWS, D), shapes)
    for n in SMALL_NAMES:
        if n in ("conv_a_w", "conf_dw_w"):
            grads[n] = lax.dynamic_slice_in_dim(sg[n], k_chip * rk, rk, axis=2)
        else:
            grads[n] = sg[n]

    for n in SMALL_NAMES:
        sh = w[n].shape
        flat = (sh[0] * sh[1], sh[2]) if n in ("conv_a_w", "conf_dw_w") else (-1, D)
        g, d, nm, nv = _adamw(*(t.reshape(flat) for t in (w[n], grads[n], mom[n], var[n])))
        grads[n], delta[n], new_m[n], new_v[n] = g.reshape(sh), d.reshape(sh), nm.reshape(sh), nv.reshape(sh)

    return (loss, dx.reshape(x.shape), *[grads[n] for n in WEIGHTS], *[delta[n] for n in WEIGHTS],
            *[new_m[n] for n in WEIGHTS], *[new_v[n] for n in WEIGHTS])
```

```python
import functools

import jax
import jax.numpy as jnp
from jax import lax
from jax.experimental import pallas as pl
from jax.experimental.pallas import tpu as pltpu

D = 1024
HEADS = 8
CHUNK = 128
KA = 3
KC = 31
HALO = 32
SUBLANES = 8
MIX_TILE = 512
WGRAD_TILE = 1024
NSH = 4
NDEV = 8
EPS = 1e-6
BF = jnp.bfloat16
F32 = jnp.float32
VMEM_LIMIT = 56 * 1024 * 1024

ADAM_LR = 0.001
ADAM_B1 = 0.9
ADAM_B2 = 0.999
ADAM_EPS = 1e-08
ADAM_WD = 0.01
ADAM_STEP = 10

MESH = pl.DeviceIdType.MESH
ANY = pl.BlockSpec(memory_space=pl.ANY)


def _cp(*sem):
    return pltpu.CompilerParams(dimension_semantics=sem, vmem_limit_bytes=VMEM_LIMIT)


def _sig(x):
    return 1.0 / (1.0 + jnp.exp(-x))


_GC = 0.7978845608028654


def _gelu(x):
    x2 = x * x
    t = jnp.tanh(_GC * x * (1.0 + 0.044715 * x2))
    y = 0.5 * x * (1.0 + t)
    dy = 0.5 * (1.0 + t) + 0.5 * x * (1.0 - t * t) * _GC * (1.0 + 3.0 * 0.044715 * x2)
    return y, dy


def _rms_fwd(x, g):
    r = lax.rsqrt(jnp.mean(x * x, axis=-1, keepdims=True) + EPS)
    return x * r * g


def _rms_bwd(dy, x, g):
    r = lax.rsqrt(jnp.mean(x * x, axis=-1, keepdims=True) + EPS)
    xn = x * r
    dyg = dy * g
    dx = r * (dyg - xn * jnp.mean(dyg * xn, axis=-1, keepdims=True))
    return dx, jnp.sum(dy * xn, axis=0, keepdims=True)


def _ln_stats(x):
    mu = jnp.mean(x, axis=-1, keepdims=True)
    xc = x - mu
    r = lax.rsqrt(jnp.mean(xc * xc, axis=-1, keepdims=True) + EPS)
    return xc * r, r


def _ln_bwd(dn, n, r):
    return r * (dn - jnp.mean(dn, axis=-1, keepdims=True) - n * jnp.mean(dn * n, axis=-1, keepdims=True))


def _dot(a, b):
    return jnp.dot(a, b, preferred_element_type=F32)


def _dot_nt(a, b):
    return lax.dot_general(a, b, (((1,), (1,)), ((), ())), preferred_element_type=F32)


def _dot_tn(a, b):
    return lax.dot_general(a, b, (((0,), (0,)), ((), ())), preferred_element_type=F32)


def _in_proj(x, g, w, dep):
    T = x.shape[0]
    nc = w.shape[2]
    tm = min(T, 1024)
    tn = nc
    nj = nc // tn

    def body(x_ref, g_ref, w_ref, dep_ref, h_ref, z_ref, h_scr):
        @pl.when((pl.program_id(1) == 0) & (pl.program_id(2) == 0))
        def _():
            h = _rms_fwd(x_ref[...], g_ref[...]).astype(BF)
            h_scr[...] = h
            h_ref[...] = h
        z_ref[...] = _dot(h_scr[...], w_ref[...]).astype(BF)

    return pl.pallas_call(
        body, name="in_proj", grid=(T // tm, NSH, nj),
        in_specs=[pl.BlockSpec((tm, D), lambda i, k, j: (i, 0)),
                  pl.BlockSpec((1, D), lambda i, k, j: (0, 0)),
                  pl.BlockSpec((None, D, tn), lambda i, k, j: (k, 0, j)), ANY],
        out_specs=[pl.BlockSpec((tm, D), lambda i, k, j: (i, 0)),
                   pl.BlockSpec((tm, tn), lambda i, k, j: (i, k * nj + j))],
        out_shape=[jax.ShapeDtypeStruct((T, D), BF), jax.ShapeDtypeStruct((T, NSH * nc), BF)],
        scratch_shapes=[pltpu.VMEM((tm, D), BF)],
        compiler_params=_cp("arbitrary", "arbitrary", "arbitrary"),
    )(x, g, w, dep)


def _tile_specs(tt, nt_total, reverse):
    def tile(i):
        return (nt_total - 1 - i) if reverse else i

    def cur(c):
        return pl.BlockSpec((tt, D), lambda i, *_: (tile(i), c))

    def halo(c):
        return pl.BlockSpec((HALO, D), lambda i, *_: (jnp.maximum(tile(i) * (tt // HALO) - 1, 0), c))

    def row(r=1):
        return pl.BlockSpec((r, D), lambda i, *_: (0, 0))

    return tile, cur, halo, row


RC = 16


def _chunks(tt, fn, group=2):
    def step(c, carry):
        for u in range(group):
            fn(pl.multiple_of((c * group + u) * RC, RC))
        return carry
    lax.fori_loop(0, tt // (RC * group), step, 0)


def _chunk_pairs(tt, fn):
    def step(c, carry):
        fn(pl.multiple_of(c * 2 * RC, RC), pl.multiple_of(c * 2 * RC + RC, RC))
        return carry
    lax.fori_loop(0, tt // (2 * RC), step, 0)


ALL_SHIFTS = tuple(range(SUBLANES))


def _shifts_of(offs):
    return tuple(sorted({o % SUBLANES for o in offs}))


def _shifted_copies(ext, sh, nrows, shifts=ALL_SHIFTS):
    for i, s in enumerate(shifts):
        sh[i] = ext[pl.ds(s, nrows), :]


def _window(sh, o, r0, shifts=ALL_SHIFTS):
    return sh[shifts.index(o % SUBLANES), pl.ds(r0 + (o // SUBLANES) * SUBLANES, RC), :]


def _fill_taps(wb, w_ref, ntap):
    for k in range(ntap):
        wb[k * SUBLANES:(k + 1) * SUBLANES, :] = jnp.broadcast_to(w_ref[k:k + 1, :], (SUBLANES, D))


def _conv_chunks(sh, wb, offs, r0s, shifts=ALL_SHIFTS):
    accs = []
    for r0 in r0s:
        acc = None
        for k, o in enumerate(offs):
            wk = wb[k * SUBLANES:(k + 1) * SUBLANES, :]
            term = jnp.concatenate([wk] * (RC // SUBLANES), axis=0) * _window(sh, o, r0, shifts)
            acc = term if acc is None else acc + term
        accs.append(acc)
    return accs


WG_TAPS = 5


def _conv_wgrad_chunked(dw_ref, d_ref, sh, offs, tt, shifts=ALL_SHIFTS):
    for g0 in range(0, len(offs), WG_TAPS):
        grp = offs[g0:g0 + WG_TAPS]

        def step(c, accs, grp=grp):
            for u in range(2):
                r0 = pl.multiple_of((2 * c + u) * SUBLANES, SUBLANES)
                d = d_ref[pl.ds(r0, SUBLANES), :]
                accs = tuple(
                    a + d * sh[shifts.index(o % SUBLANES), pl.ds(r0 + (o // SUBLANES) * SUBLANES, SUBLANES), :]
                    for a, o in zip(accs, grp))
            return accs
        accs = lax.fori_loop(0, tt // (2 * SUBLANES), step,
                             tuple(jnp.zeros((SUBLANES, D), F32) for _ in grp))
        for j, a in enumerate(accs):
            dw_ref[g0 + j:g0 + j + 1, :] += jnp.sum(a, axis=0, keepdims=True)


def _causal_offsets(ntap):
    return [HALO - (ntap - 1) + k for k in range(ntap)]


def _anticausal_offsets(ntap):
    return [ntap - 1 - k for k in range(ntap)]


def _mix_a_fwd(z, wa, S):
    T = z.shape[0]
    tt = min(S, MIX_TILE)
    nt = S // tt
    _, cur, halo, row = _tile_specs(tt, T // tt, False)

    nrows = HALO + tt
    offs = _causal_offsets(KA)
    shifts = _shifts_of(offs)

    def body(ah, ab, ac, ah_h, ac_h, w_ref, y_ref, ext, sh, wb):
        @pl.when(pl.program_id(0) == 0)
        def _():
            _fill_taps(wb, w_ref, KA)
            ext[nrows:, :] = jnp.zeros((SUBLANES, D), F32)

        first = (pl.program_id(0) % nt) == 0
        ph = ah_h[...].astype(F32) * ac_h[...].astype(F32)
        ext[0:HALO, :] = jnp.where(first, 0.0, ph)

        def prod(r0):
            rows = pl.ds(r0, RC)
            ext[pl.ds(HALO + r0, RC), :] = ah[rows, :].astype(F32) * ac[rows, :].astype(F32)
        _chunks(tt, prod)
        _shifted_copies(ext, sh, nrows, shifts)

        def conv(*r0s):
            for r0, q in zip(r0s, _conv_chunks(sh, wb, offs, r0s, shifts)):
                rows = pl.ds(r0, RC)
                y_ref[rows, :] = (ab[rows, :].astype(F32) * q).astype(BF)
        _chunk_pairs(tt, conv)

    return pl.pallas_call(
        body, name="mix_a_fwd", grid=(T // tt,),
        in_specs=[cur(0), cur(1), cur(2), halo(0), halo(2), row(KA)],
        out_specs=pl.BlockSpec((tt, D), lambda i: (i, 0)),
        out_shape=jax.ShapeDtypeStruct((T, D), BF),
        scratch_shapes=[pltpu.VMEM((nrows + SUBLANES, D), F32), pltpu.VMEM((len(shifts), nrows, D), F32),
                        pltpu.VMEM((KA * SUBLANES, D), F32)],
        compiler_params=_cp("arbitrary"),
    )(z, z, z, z, z, wa)


def _mix_b_fwd(z, wc, bc, lg, lb, S):
    T = z.shape[0]
    tt = min(S, MIX_TILE)
    nt = S // tt
    _, cur, halo, row = _tile_specs(tt, T // tt, False)

    nrows = HALO + tt
    offs = _causal_offsets(KC)

    def body(ca, cg, ca_h, cg_h, w_ref, bc_ref, lg_ref, lb_ref, y_ref, s_ref, ext, sh, wb):
        @pl.when(pl.program_id(0) == 0)
        def _():
            _fill_taps(wb, w_ref, KC)
            ext[nrows:, :] = jnp.zeros((SUBLANES, D), F32)

        first = (pl.program_id(0) % nt) == 0
        rh = ca_h[...].astype(F32) * _sig(cg_h[...].astype(F32))
        ext[0:HALO, :] = jnp.where(first, 0.0, rh)

        def glu(r0):
            rows = pl.ds(r0, RC)
            ext[pl.ds(HALO + r0, RC), :] = ca[rows, :].astype(F32) * _sig(cg[rows, :].astype(F32))
        _chunks(tt, glu)
        _shifted_copies(ext, sh, nrows)

        def conv(r0):
            rows = pl.ds(r0, RC)
            (q,) = _conv_chunks(sh, wb, offs, (r0,))
            s = q + bc_ref[...]
            s_ref[rows, :] = s.astype(BF)
            n, _ = _ln_stats(s)
            t = n * lg_ref[...] + lb_ref[...]
            y_ref[rows, :] = (t * _sig(t)).astype(BF)
        _chunks(tt, conv)

    return pl.pallas_call(
        body, name="mix_b_fwd", grid=(T // tt,),
        in_specs=[cur(3), cur(4), halo(3), halo(4), row(KC), row(), row(), row()],
        out_specs=[pl.BlockSpec((tt, D), lambda i: (i, 0))] * 2,
        out_shape=[jax.ShapeDtypeStruct((T, D), BF)] * 2,
        scratch_shapes=[pltpu.VMEM((nrows + SUBLANES, D), F32), pltpu.VMEM((SUBLANES, nrows, D), F32),
                        pltpu.VMEM((KC * SUBLANES, D), F32)],
        compiler_params=_cp("arbitrary"),
    )(z, z, z, z, wc, bc, lg, lb)


def _causal_mask(transposed):
    r = lax.broadcasted_iota(jnp.int32, (CHUNK, CHUNK), 0)
    c = lax.broadcasted_iota(jnp.int32, (CHUNK, CHUNK), 1)
    return (c >= r) if transposed else (r >= c)


def _mix_s_fwd(z, lg, lb, ws, bst, S):
    T = z.shape[0]
    tt = min(S, MIX_TILE)
    _, cur, _, row = _tile_specs(tt, T // tt, False)

    def body(su, sv, lg_ref, lb_ref, ws_ref, bst_ref, y_ref, u_scr, vn_scr):
        u_scr[...] = _gelu(su[...].astype(F32))[0]
        n, _ = _ln_stats(_gelu(sv[...].astype(F32))[0])
        vn_scr[...] = (n * lg_ref[...] + lb_ref[...]).astype(BF)
        mask = _causal_mask(False)
        for h in range(HEADS):
            wm = jnp.where(mask, ws_ref[h], 0.0).astype(BF)
            cols = slice(h * CHUNK, (h + 1) * CHUNK)
            for c in range(tt // CHUNK):
                rows = slice(c * CHUNK, (c + 1) * CHUNK)
                mixed = _dot(wm, vn_scr[rows, cols]) + bst_ref[:, h:h + 1]
                y_ref[rows, cols] = (u_scr[rows, cols] * mixed).astype(BF)

    return pl.pallas_call(
        body, name="mix_s_fwd", grid=(T // tt,),
        in_specs=[cur(5), cur(6), row(), row(),
                  pl.BlockSpec((HEADS, CHUNK, CHUNK), lambda i: (0, 0, 0)),
                  pl.BlockSpec((CHUNK, HEADS), lambda i: (0, 0))],
        out_specs=pl.BlockSpec((tt, D), lambda i: (i, 0)),
        out_shape=jax.ShapeDtypeStruct((T, D), BF),
        scratch_shapes=[pltpu.VMEM((tt, D), F32), pltpu.VMEM((tt, D), BF)],
        compiler_params=_cp("arbitrary"),
    )(z, z, lg, lb, ws, bst)


def _mix_out_fwd(ya, yc, ys, z, x, wb, wo, gp, dep):
    T = x.shape[0]
    tm = min(T, 512)
    rk = D // NSH

    def body(ya_ref, yc_ref, ys_ref, ga, gc, gs, x_ref, wb_ref, wo_ref, gp_ref, dep_ref,
             p_ref, mg_ref, m_ref, x1_ref):
        acc = None
        for b, (y_ref, g_ref) in enumerate(((ya_ref, ga), (yc_ref, gc), (ys_ref, gs))):
            pb = None
            for k in range(NSH):
                part = _dot(y_ref[:, k * rk:(k + 1) * rk], wb_ref[k, b])
                pb = part if pb is None else pb + part
            p_ref[b] = pb.astype(BF)
            term = _sig(g_ref[...].astype(F32)) * pb
            acc = term if acc is None else acc + term
        mg = acc.astype(BF)
        mg_ref[...] = mg
        m = _dot(mg, wo_ref[...])
        m_ref[...] = m.astype(BF)
        x1_ref[...] = x_ref[...] + _rms_fwd(m, gp_ref[...])

    rowblk = pl.BlockSpec((tm, D), lambda i: (i, 0))
    return pl.pallas_call(
        body, name="mix_out_fwd", grid=(T // tm,),
        in_specs=[rowblk, rowblk, rowblk,
                  pl.BlockSpec((tm, D), lambda i: (i, 7)), pl.BlockSpec((tm, D), lambda i: (i, 8)),
                  pl.BlockSpec((tm, D), lambda i: (i, 9)), rowblk,
                  pl.BlockSpec((NSH, 3, rk, D), lambda i: (0, 0, 0, 0)),
                  pl.BlockSpec((D, D), lambda i: (0, 0)),
                  pl.BlockSpec((1, D), lambda i: (0, 0)), ANY],
        out_specs=[pl.BlockSpec((3, tm, D), lambda i: (0, i, 0)), rowblk, rowblk, rowblk],
        out_shape=[jax.ShapeDtypeStruct((3, T, D), BF), jax.ShapeDtypeStruct((T, D), BF),
                   jax.ShapeDtypeStruct((T, D), BF), jax.ShapeDtypeStruct((T, D), F32)],
        compiler_params=_cp("arbitrary"),
    )(ya, yc, ys, z, z, z, x, wb, wo, gp, dep)


def _ffn_fwd(x1, g3, w1, w2, g4):
    T = x1.shape[0]
    tm = min(T, 1024)

    def body(x_ref, g3_ref, w1_ref, w2_ref, g4_ref, h_ref, a_ref, f_ref, x2_ref, h_scr, acc):
        k = pl.program_id(1)

        @pl.when(k == 0)
        def _():
            h = _rms_fwd(x_ref[...], g3_ref[...]).astype(BF)
            h_scr[...] = h
            h_ref[...] = h
            acc[...] = jnp.zeros_like(acc)

        a = _dot(h_scr[...], w1_ref[...])
        a_ref[...] = a.astype(BF)
        r = jnp.maximum(a, 0.0)
        acc[...] += _dot((r * r).astype(BF), w2_ref[...])

        @pl.when(k == NSH - 1)
        def _():
            f = acc[...]
            f_ref[...] = f.astype(BF)
            x2_ref[...] = x_ref[...] + _rms_fwd(f, g4_ref[...])

    rowblk = pl.BlockSpec((tm, D), lambda i, k: (i, 0))
    vec = pl.BlockSpec((1, D), lambda i, k: (0, 0))
    return pl.pallas_call(
        body, name="ffn_fwd", grid=(T // tm, NSH),
        in_specs=[rowblk, vec, pl.BlockSpec((None, D, D), lambda i, k: (k, 0, 0)),
                  pl.BlockSpec((D, D), lambda i, k: (k, 0)), vec],
        out_specs=[rowblk, pl.BlockSpec((tm, D), lambda i, k: (i, k)), rowblk, rowblk],
        out_shape=[jax.ShapeDtypeStruct((T, D), BF), jax.ShapeDtypeStruct((T, NSH * D), BF),
                   jax.ShapeDtypeStruct((T, D), BF), jax.ShapeDtypeStruct((T, D), F32)],
        scratch_shapes=[pltpu.VMEM((tm, D), BF), pltpu.VMEM((tm, D), F32)],
        compiler_params=_cp("arbitrary", "arbitrary"),
    )(x1, g3, w1, w2, g4)


def _loss_head(y, target):
    T = y.shape[0]
    tm = min(T, 512)

    def body(y_ref, t_ref, dy_ref, l_ref):
        @pl.when(pl.program_id(0) == 0)
        def _():
            l_ref[...] = jnp.zeros_like(l_ref)
        e = y_ref[...] - t_ref[...]
        dy_ref[...] = e * (1.0 / D)
        l_ref[...] += jnp.sum(e * e) * (0.5 / D)

    rowblk = pl.BlockSpec((tm, D), lambda i: (i, 0))
    return pl.pallas_call(
        body, name="loss_head", grid=(T // tm,),
        in_specs=[rowblk, rowblk],
        out_specs=[rowblk, pl.BlockSpec((1, 128), lambda i: (0, 0))],
        out_shape=[jax.ShapeDtypeStruct((T, D), F32), jax.ShapeDtypeStruct((1, 128), F32)],
        compiler_params=_cp("arbitrary"),
    )(y, target)


def _ffn_bwd(dx2, f, g4, a, w2, w1, x1, g3, dep):
    T = dx2.shape[0]
    tm = min(T, 1024)
    tf = 512
    per = w1.shape[2] // tf
    nk = NSH * per

    def body(dx2_ref, f_ref, g4_ref, a_ref, w2_ref, w1_ref, x1_ref, g3_ref, dep_ref,
             df_ref, da_ref, dx1_ref, dg4_ref, dg3_ref, df_scr, acc):
        i, k = pl.program_id(0), pl.program_id(1)

        @pl.when((i == 0) & (k == 0))
        def _():
            dg4_ref[...] = jnp.zeros_like(dg4_ref)
            dg3_ref[...] = jnp.zeros_like(dg3_ref)

        @pl.when(k == 0)
        def _():
            df, dg = _rms_bwd(dx2_ref[...], f_ref[...].astype(F32), g4_ref[...])
            dg4_ref[...] += dg
            dfb = df.astype(BF)
            df_scr[...] = dfb
            df_ref[...] = dfb
            acc[...] = jnp.zeros_like(acc)

        av = a_ref[...].astype(F32)
        da = (_dot_nt(df_scr[...], w2_ref[...]) * (2.0 * jnp.maximum(av, 0.0))).astype(BF)
        da_ref[...] = da
        acc[...] += _dot_nt(da, w1_ref[...])

        @pl.when(k == nk - 1)
        def _():
            dx, dg = _rms_bwd(acc[...], x1_ref[...], g3_ref[...])
            dg3_ref[...] += dg
            dx1_ref[...] = dx2_ref[...] + dx

    rowblk = pl.BlockSpec((tm, D), lambda i, k: (i, 0))
    vec = pl.BlockSpec((1, D), lambda i, k: (0, 0))
    return pl.pallas_call(
        body, name="ffn_bwd", grid=(T // tm, nk),
        in_specs=[rowblk, rowblk, vec, pl.BlockSpec((tm, tf), lambda i, k: (i, k)),
                  pl.BlockSpec((tf, D), lambda i, k: (k, 0)),
                  pl.BlockSpec((None, D, tf), lambda i, k: (k // per, 0, k % per)), rowblk, vec, ANY],
        out_specs=[rowblk, pl.BlockSpec((tm, tf), lambda i, k: (i, k)), rowblk, vec, vec],
        out_shape=[jax.ShapeDtypeStruct((T, D), BF), jax.ShapeDtypeStruct((T, NSH * D), BF),
                   jax.ShapeDtypeStruct((T, D), F32), jax.ShapeDtypeStruct((1, D), F32),
                   jax.ShapeDtypeStruct((1, D), F32)],
        scratch_shapes=[pltpu.VMEM((tm, D), BF), pltpu.VMEM((tm, D), F32)],
        compiler_params=_cp("arbitrary", "arbitrary"),
    )(dx2, f, g4, a, w2, w1, x1, g3, dep)


def _wgrad(name, ops, grid, in_specs, out_spec, out_shape, acc_shape, pick=None, relu2=False):
    nt = grid[-1]
    na = len(ops) - 1

    def body(*refs):
        a_refs, b_ref, o_ref, acc = refs[:na], refs[na], refs[na + 1], refs[na + 2]
        t = pl.program_id(len(grid) - 1)

        @pl.when(t == 0)
        def _():
            acc[...] = jnp.zeros_like(acc)

        def add(a_ref):
            av = a_ref[...]
            if relu2:
                r = jnp.maximum(av.astype(F32), 0.0)
                av = (r * r).astype(BF)
            acc[...] += _dot_tn(av, b_ref[...])

        if na == 1:
            add(a_refs[0])
        else:
            sel = pick()
            for n in range(na):
                pl.when(sel == n)(functools.partial(add, a_refs[n]))

        @pl.when(t == nt - 1)
        def _():
            if len(o_ref.shape) == 2:
                o_ref[...] = acc[...].astype(o_ref.dtype)
            else:
                rs = o_ref.shape[1]
                for q in range(o_ref.shape[0]):
                    o_ref[q] = acc[q * rs:(q + 1) * rs, :].astype(o_ref.dtype)

    return pl.pallas_call(
        body, name=name, grid=grid, in_specs=in_specs, out_specs=out_spec, out_shape=out_shape,
        scratch_shapes=[pltpu.VMEM(acc_shape, F32)],
        compiler_params=_cp(*(["arbitrary"] * len(grid))),
    )(*ops)


def _mix_out_bwd(dx1, m, gp, wo, p3, z, wb, dep):
    T = dx1.shape[0]
    tm = min(T, 512)
    rk = D // NSH

    def body(dx1_ref, m_ref, gp_ref, wo_ref, p_ref, g_ref, wb_ref, dep_ref,
             dm_ref, dp_ref, dy_ref, dz_ref, dgp_ref, dmg):
        i, b = pl.program_id(0), pl.program_id(1)

        @pl.when((i == 0) & (b == 0))
        def _():
            dgp_ref[...] = jnp.zeros_like(dgp_ref)

        @pl.when(b == 0)
        def _():
            dm, dg = _rms_bwd(dx1_ref[...], m_ref[...].astype(F32), gp_ref[...])
            dgp_ref[...] += dg
            dmb = dm.astype(BF)
            dm_ref[...] = dmb
            dmg[...] = _dot_nt(dmb, wo_ref[...])

        gate = _sig(g_ref[...].astype(F32))
        d = dmg[...]
        dp = (d * gate).astype(BF)
        dp_ref[...] = dp
        dz_ref[...] = (d * p_ref[...].astype(F32) * gate * (1.0 - gate)).astype(BF)
        for k in range(NSH):
            dy_ref[:, k * rk:(k + 1) * rk] = _dot_nt(dp, wb_ref[k, b]).astype(BF)

    rowblk = pl.BlockSpec((tm, D), lambda i, b: (i, 0))
    br = pl.BlockSpec((None, tm, D), lambda i, b: (b, i, 0))
    vec = pl.BlockSpec((1, D), lambda i, b: (0, 0))
    return pl.pallas_call(
        body, name="mix_out_bwd", grid=(T // tm, 3),
        in_specs=[rowblk, rowblk, vec, pl.BlockSpec((D, D), lambda i, b: (0, 0)), br,
                  pl.BlockSpec((tm, D), lambda i, b: (i, 7 + b)),
                  pl.BlockSpec((NSH, 3, rk, D), lambda i, b: (0, 0, 0, 0)), ANY],
        out_specs=[rowblk, br, br, pl.BlockSpec((tm, D), lambda i, b: (i, 7 + b)), vec],
        out_shape=[jax.ShapeDtypeStruct((T, D), BF), jax.ShapeDtypeStruct((3, T, D), BF),
                   jax.ShapeDtypeStruct((3, T, D), BF), jax.ShapeDtypeStruct((T, 10 * D), BF),
                   jax.ShapeDtypeStruct((1, D), F32)],
        scratch_shapes=[pltpu.VMEM((tm, D), F32)],
        compiler_params=_cp("arbitrary", "arbitrary"),
    )(dx1, m, gp, wo, p3, z, wb, dep)


def _mix_a_bwd(dz, dy3, z, wa, S, dep):
    T = z.shape[0]
    tt = min(S, MIX_TILE)
    nt = S // tt
    ntt = T // tt
    tile, cur, halo, row = _tile_specs(tt, ntt, True)

    nrows = HALO + tt
    coffs, aoffs = _causal_offsets(KA), _anticausal_offsets(KA)
    cshifts, ashifts = _shifts_of(coffs), _shifts_of(aoffs)

    def body(dz_in, dy_ref, ah, ab, ac, ah_h, ac_h, w_ref, dep_ref, dz_ref, dw_ref, ext_p, ext_d, sh, wb, stage):
        i, b = pl.program_id(0), pl.program_id(1)
        ti = ntt - 1 - i

        @pl.when((i == 0) & (b == 0))
        def _():
            dw_ref[...] = jnp.zeros_like(dw_ref)
            ext_d[...] = jnp.zeros_like(ext_d)
            ext_p[nrows:, :] = jnp.zeros((SUBLANES, D), F32)
            _fill_taps(wb, w_ref, KA)

        @pl.when(b == 0)
        def _():
            first = (ti % nt) == 0
            last = (ti % nt) == nt - 1
            ext_p[0:HALO, :] = jnp.where(first, 0.0, ah_h[...].astype(F32) * ac_h[...].astype(F32))
            ext_d[tt:nrows, :] = jnp.where(last, 0.0, ext_d[0:HALO, :])

            def prod(r0):
                rows = pl.ds(r0, RC)
                ext_p[pl.ds(HALO + r0, RC), :] = ah[rows, :].astype(F32) * ac[rows, :].astype(F32)
            _chunks(tt, prod)
            _shifted_copies(ext_p, sh, nrows, cshifts)

            def mid(*r0s):
                for r0, q in zip(r0s, _conv_chunks(sh, wb, coffs, r0s, cshifts)):
                    rows = pl.ds(r0, RC)
                    dy = dy_ref[rows, :].astype(F32)
                    stage[1, rows, :] = (dy * q).astype(BF)
                    ext_d[rows, :] = dy * ab[rows, :].astype(F32)
            _chunk_pairs(tt, mid)
            _conv_wgrad_chunked(dw_ref, ext_d, sh, coffs, tt, cshifts)
            _shifted_copies(ext_d, sh, nrows, ashifts)

            def fin(*r0s):
                for r0, dp in zip(r0s, _conv_chunks(sh, wb, aoffs, r0s, ashifts)):
                    rows = pl.ds(r0, RC)
                    stage[0, rows, :] = (dp * ac[rows, :].astype(F32)).astype(BF)
                    stage[2, rows, :] = (dp * ah[rows, :].astype(F32)).astype(BF)
            _chunk_pairs(tt, fin)

        dz_ref[...] = stage[b]

    return pl.pallas_call(
        body, name="mix_a_bwd", grid=(ntt, 3),
        in_specs=[ANY, pl.BlockSpec((None, tt, D), lambda i, b: (0, tile(i), 0)),
                  cur(0), cur(1), cur(2), halo(0), halo(2), row(KA), ANY],
        out_specs=[pl.BlockSpec((tt, D), lambda i, b: (tile(i), b)), pl.BlockSpec((KA, D), lambda i, b: (0, 0))],
        out_shape=[jax.ShapeDtypeStruct(dz.shape, BF), jax.ShapeDtypeStruct((KA, D), F32)],
        scratch_shapes=[pltpu.VMEM((nrows + SUBLANES, D), F32), pltpu.VMEM((nrows + SUBLANES, D), F32),
                        pltpu.VMEM((max(len(cshifts), len(ashifts)), nrows, D), F32),
                        pltpu.VMEM((KA * SUBLANES, D), F32), pltpu.VMEM((3, tt, D), BF)],
        input_output_aliases={0: 0},
        compiler_params=_cp("arbitrary", "arbitrary"),
    )(dz, dy3, z, z, z, z, z, wa, dep)


def _mix_b_bwd(dz, dy3, s, z, wc, lg, lb, S):
    T = z.shape[0]
    tt = min(S, MIX_TILE)
    nt = S // tt
    ntt = T // tt
    tile, cur, halo, row = _tile_specs(tt, ntt, True)

    nrows = HALO + tt

    def body(dz_in, dy_ref, s_ref, ca, cg, ca_h, cg_h, w_ref, lg_ref, lb_ref,
             dz_ref, dw_ref, dbc_ref, dlg_ref, dlb_ref, ext_r, ext_d, sh, wb, accs, stage):
        i, b = pl.program_id(0), pl.program_id(1)
        ti = ntt - 1 - i

        @pl.when((i == 0) & (b == 0))
        def _():
            dw_ref[...] = jnp.zeros_like(dw_ref)
            dbc_ref[...] = jnp.zeros_like(dbc_ref)
            dlg_ref[...] = jnp.zeros_like(dlg_ref)
            dlb_ref[...] = jnp.zeros_like(dlb_ref)
            ext_d[...] = jnp.zeros_like(ext_d)
            ext_r[nrows:, :] = jnp.zeros((SUBLANES, D), F32)
            _fill_taps(wb, w_ref, KC)

        @pl.when(b == 0)
        def _():
            first = (ti % nt) == 0
            last = (ti % nt) == nt - 1
            ext_r[0:HALO, :] = jnp.where(first, 0.0, ca_h[...].astype(F32) * _sig(cg_h[...].astype(F32)))
            ext_d[tt:nrows, :] = jnp.where(last, 0.0, ext_d[0:HALO, :])
            accs[...] = jnp.zeros_like(accs)

            def point(r0):
                rows = pl.ds(r0, RC)
                n, r = _ln_stats(s_ref[rows, :].astype(F32))
                t = n * lg_ref[...] + lb_ref[...]
                sg = _sig(t)
                dt = dy_ref[rows, :].astype(F32) * (sg * (1.0 + t * (1.0 - sg)))
                accs[0] += dt * n
                accs[1] += dt
                ds = _ln_bwd(dt * lg_ref[...], n, r)
                accs[2] += ds
                ext_d[rows, :] = ds
                ext_r[pl.ds(HALO + r0, RC), :] = ca[rows, :].astype(F32) * _sig(cg[rows, :].astype(F32))
            _chunks(tt, point, group=4)
            dlg_ref[...] += jnp.sum(accs[0], axis=0, keepdims=True)
            dlb_ref[...] += jnp.sum(accs[1], axis=0, keepdims=True)
            dbc_ref[...] += jnp.sum(accs[2], axis=0, keepdims=True)

            _shifted_copies(ext_r, sh, nrows)
            _conv_wgrad_chunked(dw_ref, ext_d, sh, _causal_offsets(KC), tt)
            _shifted_copies(ext_d, sh, nrows)

            def conv(r0):
                rows = pl.ds(r0, RC)
                (dr,) = _conv_chunks(sh, wb, _anticausal_offsets(KC), (r0,))
                cav = ca[rows, :].astype(F32)
                sgc = _sig(cg[rows, :].astype(F32))
                stage[0, rows, :] = (dr * sgc).astype(BF)
                stage[1, rows, :] = (dr * cav * sgc * (1.0 - sgc)).astype(BF)
            _chunks(tt, conv)

        dz_ref[...] = stage[b]

    vec = pl.BlockSpec((1, D), lambda i, b: (0, 0))
    return pl.pallas_call(
        body, name="mix_b_bwd", grid=(ntt, 2),
        in_specs=[ANY, pl.BlockSpec((None, tt, D), lambda i, b: (1, tile(i), 0)),
                  pl.BlockSpec((tt, D), lambda i, b: (tile(i), 0)),
                  cur(3), cur(4), halo(3), halo(4), row(KC), row(), row()],
        out_specs=[pl.BlockSpec((tt, D), lambda i, b: (tile(i), 3 + b)),
                   pl.BlockSpec((KC, D), lambda i, b: (0, 0)), vec, vec, vec],
        out_shape=[jax.ShapeDtypeStruct(dz.shape, BF), jax.ShapeDtypeStruct((KC, D), F32)]
        + [jax.ShapeDtypeStruct((1, D), F32)] * 3,
        scratch_shapes=[pltpu.VMEM((nrows + SUBLANES, D), F32), pltpu.VMEM((nrows + SUBLANES, D), F32),
                        pltpu.VMEM((SUBLANES, nrows, D), F32), pltpu.VMEM((KC * SUBLANES, D), F32),
                        pltpu.VMEM((3, RC, D), F32), pltpu.VMEM((2, tt, D), BF)],
        input_output_aliases={0: 0},
        compiler_params=_cp("arbitrary", "arbitrary"),
    )(dz, dy3, s, z, z, z, z, wc, lg, lb)


def _mix_s_bwd(dz, dy3, z, lg, lb, ws, wst, bst, S):
    T = z.shape[0]
    tt = min(S, MIX_TILE)
    ntt = T // tt
    _, cur, _, row = _tile_specs(tt, ntt, False)

    def body(dz_in, dy_ref, su, sv, lg_ref, lb_ref, ws_ref, wst_ref, bst_ref,
             dz_ref, dws_ref, dbst_ref, dlg_ref, dlb_ref, u_scr, vn_scr, dvn_scr, stage):
        i, b = pl.program_id(0), pl.program_id(1)

        @pl.when((i == 0) & (b == 0))
        def _():
            dws_ref[...] = jnp.zeros_like(dws_ref)
            dbst_ref[...] = jnp.zeros_like(dbst_ref)
            dlg_ref[...] = jnp.zeros_like(dlg_ref)
            dlb_ref[...] = jnp.zeros_like(dlb_ref)

        @pl.when(b == 0)
        def _():
            u, du_dx = _gelu(su[...].astype(F32))
            v, dv_dx = _gelu(sv[...].astype(F32))
            u_scr[...] = u
            n, r = _ln_stats(v)
            vn_scr[...] = (n * lg_ref[...] + lb_ref[...]).astype(BF)
            mask = _causal_mask(False)
            mask_t = _causal_mask(True)
            for h in range(HEADS):
                wm = jnp.where(mask, ws_ref[h], 0.0).astype(BF)
                wmt = jnp.where(mask_t, wst_ref[h], 0.0).astype(BF)
                cols = slice(h * CHUNK, (h + 1) * CHUNK)
                for c in range(tt // CHUNK):
                    rows = slice(c * CHUNK, (c + 1) * CHUNK)
                    vb = vn_scr[rows, cols]
                    mixed = _dot(wm, vb) + bst_ref[:, h:h + 1]
                    dy = dy_ref[rows, cols].astype(F32)
                    dmix = dy * u_scr[rows, cols]
                    u_scr[rows, cols] = dy * mixed
                    dbst_ref[:, h:h + 1] += jnp.sum(dmix, axis=1, keepdims=True)
                    dmb = dmix.astype(BF)
                    dws_ref[h] += _dot_nt(dmb, vb)
                    dvn_scr[rows, cols] = _dot(wmt, dmb)
            stage[0] = (u_scr[...] * du_dx).astype(BF)
            dvn = dvn_scr[...]
            dlg_ref[...] += jnp.sum(dvn * n, axis=0, keepdims=True)
            dlb_ref[...] += jnp.sum(dvn, axis=0, keepdims=True)
            stage[1] = (_ln_bwd(dvn * lg_ref[...], n, r) * dv_dx).astype(BF)

        dz_ref[...] = stage[b]

    vec = pl.BlockSpec((1, D), lambda i, b: (0, 0))
    wsp = pl.BlockSpec((HEADS, CHUNK, CHUNK), lambda i, b: (0, 0, 0))
    bsp = pl.BlockSpec((CHUNK, HEADS), lambda i, b: (0, 0))
    return pl.pallas_call(
        body, name="mix_s_bwd", grid=(ntt, 2),
        in_specs=[ANY, pl.BlockSpec((None, tt, D), lambda i, b: (2, i, 0)),
                  cur(5), cur(6), row(), row(), wsp, wsp, bsp],
        out_specs=[pl.BlockSpec((tt, D), lambda i, b: (i, 5 + b)), wsp, bsp, vec, vec],
        out_shape=[jax.ShapeDtypeStruct(dz.shape, BF), jax.ShapeDtypeStruct((HEADS, CHUNK, CHUNK), F32),
                   jax.ShapeDtypeStruct((CHUNK, HEADS), F32), jax.ShapeDtypeStruct((1, D), F32),
                   jax.ShapeDtypeStruct((1, D), F32)],
        scratch_shapes=[pltpu.VMEM((tt, D), F32), pltpu.VMEM((tt, D), BF), pltpu.VMEM((tt, D), F32),
                        pltpu.VMEM((2, tt, D), BF)],
        input_output_aliases={0: 0},
        compiler_params=_cp("arbitrary", "arbitrary"),
    )(dz, dy3, z, z, lg, lb, ws, wst, bst)


def _in_proj_bwd(dz, w, x, g, dx1, dep):
    T = x.shape[0]
    nc = w.shape[2]
    tm = min(T, 1024)
    tn = nc
    nj = nc // tn
    ep = min(tm, 128)

    def body(dz_ref, w_ref, x_ref, g_ref, dx1_ref, dep_ref, dx_ref, dg_ref, acc):
        i, k, j = pl.program_id(0), pl.program_id(1), pl.program_id(2)

        @pl.when((i == 0) & (k == 0) & (j == 0))
        def _():
            dg_ref[...] = jnp.zeros_like(dg_ref)

        @pl.when((k == 0) & (j == 0))
        def _():
            acc[...] = jnp.zeros_like(acc)

        acc[...] += _dot_nt(dz_ref[...], w_ref[...])

        @pl.when((k == NSH - 1) & (j == nj - 1))
        def _():
            def step(c, dg):
                rows = pl.ds(pl.multiple_of(c * ep, ep), ep)
                dx, dgc = _rms_bwd(acc[rows, :], x_ref[rows, :], g_ref[...])
                dx_ref[rows, :] = dx1_ref[rows, :] + dx
                return dg + dgc
            dg_ref[...] += lax.fori_loop(0, tm // ep, step, jnp.zeros((1, D), F32))

    rowblk = pl.BlockSpec((tm, D), lambda i, k, j: (i, 0))
    vec = pl.BlockSpec((1, D), lambda i, k, j: (0, 0))
    return pl.pallas_call(
        body, name="in_proj_bwd", grid=(T // tm, NSH, nj),
        in_specs=[pl.BlockSpec((tm, tn), lambda i, k, j: (i, k * nj + j)),
                  pl.BlockSpec((None, D, tn), lambda i, k, j: (k, 0, j)), rowblk, vec, rowblk, ANY],
        out_specs=[rowblk, vec],
        out_shape=[jax.ShapeDtypeStruct((T, D), F32), jax.ShapeDtypeStruct((1, D), F32)],
        scratch_shapes=[pltpu.VMEM((tm, D), F32)],
        compiler_params=_cp("arbitrary", "arbitrary", "arbitrary"),
    )(dz, w, x, g, dx1, dep)


def _layer_fwd(x, p, S, dep, late):
    h, z = _in_proj(x, p["g_mix_pre"], p["w_in"], dep)
    ya = _mix_a_fwd(z, p["conv_a_w"], S)
    yc, s = _mix_b_fwd(z, p["conf_dw_w"], p["conf_dw_b"], p["conf_ln_g"], p["conf_ln_b"], S)
    ys = _mix_s_fwd(z, p["sgu_ln_g"], p["sgu_ln_b"], p["sgu_ws"], p["sgu_bt"], S)
    more, dep2 = late(ys)
    p.update(more)
    p3, merged, m, x1 = _mix_out_fwd(ya, yc, ys, z, x, p["w_branch"], p["w_out"], p["g_mix_post"], dep2)
    h2, a, f, x2 = _ffn_fwd(x1, p["g_ffn_pre"], p["w_ff1"], p["w_ff2"], p["g_ffn_post"])
    saved = dict(x=x, h=h, z=z, ya=ya, yc=yc, ys=ys, s=s, p3=p3, merged=merged, m=m, x1=x1, h2=h2, a=a, f=f)
    return x2, saved


def _layer_bwd(dx2, p, sv, S, dep, hooks):
    after_ffn, early, mid = hooks
    T = dx2.shape[0]
    bt = min(T, WGRAD_TILE)
    nt = T // bt
    rk = D // NSH
    df, da, dx1, dg_ffn_post, dg_ffn_pre = _ffn_bwd(dx2, sv["f"], p["g_ffn_post"], sv["a"], p["w_ff2"],
                                                    p["w_ff1"], sv["x1"], p["g_ffn_pre"], dep)
    dw_ff2 = _wgrad("wgrad_ff2", (sv["a"], df), (NSH, nt),
                    [pl.BlockSpec((bt, D), lambda k, t: (t, k)), pl.BlockSpec((bt, D), lambda k, t: (t, 0))],
                    pl.BlockSpec((None, D, D), lambda k, t: (k, 0, 0)),
                    jax.ShapeDtypeStruct((NSH, D, D), BF), (D, D), relu2=True)
    dw_ff1 = _wgrad("wgrad_ff1", (sv["h2"], da), (NSH, nt),
                    [pl.BlockSpec((bt, D), lambda k, t: (t, 0)), pl.BlockSpec((bt, D), lambda k, t: (t, k))],
                    pl.BlockSpec((None, D, D), lambda k, t: (k, 0, 0)),
                    jax.ShapeDtypeStruct((NSH, D, D), BF), (D, D))
    dm, dp3, dy3, dz, dg_mix_post = _mix_out_bwd(dx1, sv["m"], p["g_mix_post"], p["w_out"], sv["p3"], sv["z"],
                                                 p["w_branch"], after_ffn(dx1))
    dw_out = _wgrad("wgrad_out", (sv["merged"], dm), (nt,),
                    [pl.BlockSpec((bt, D), lambda t: (t, 0)), pl.BlockSpec((bt, D), lambda t: (t, 0))],
                    pl.BlockSpec((D, D), lambda t: (0, 0)),
                    jax.ShapeDtypeStruct((D, D), BF), (D, D)).reshape(NSH, rk, D)
    ysp = lambda n: pl.BlockSpec((bt, D), lambda b, t: (jnp.where(b == n, t, 0), 0))
    dw_br = _wgrad("wgrad_branch", (sv["ya"], sv["yc"], sv["ys"], dp3), (3, nt),
                   [ysp(0), ysp(1), ysp(2), pl.BlockSpec((None, bt, D), lambda b, t: (b, t, 0))],
                   pl.BlockSpec((NSH, None, rk, D), lambda b, t: (0, b, 0, 0)),
                   jax.ShapeDtypeStruct((NSH, 3, rk, D), BF), (D, D), pick=lambda: pl.program_id(0))
    dz, dwa = _mix_a_bwd(dz, dy3, sv["z"], p["conv_a_w"], S, early([dw_br, dw_out, dw_ff1, dw_ff2]))
    dz, dwc, dbc, dclg, dclb = _mix_b_bwd(dz, dy3, sv["s"], sv["z"], p["conf_dw_w"], p["conf_ln_g"],
                                          p["conf_ln_b"], S)
    dz, dws, dbst, dslg, dslb = _mix_s_bwd(dz, dy3, sv["z"], p["sgu_ln_g"], p["sgu_ln_b"], p["sgu_ws"],
                                           p["sgu_wst"], p["sgu_bt"], S)
    dx, dg_mix_pre = _in_proj_bwd(dz, p["w_in"], sv["x"], p["g_mix_pre"], dx1, mid(dz))
    tn = p["w_in"].shape[2]
    nj = p["w_in"].shape[2] // tn
    dw_in = _wgrad("wgrad_in", (sv["h"], dz), (NSH, nj, nt),
                   [pl.BlockSpec((bt, D), lambda k, j, t: (t, 0)),
                    pl.BlockSpec((bt, tn), lambda k, j, t: (t, k * nj + j))],
                   pl.BlockSpec((None, D, tn), lambda k, j, t: (k, 0, j)),
                   jax.ShapeDtypeStruct(p["w_in"].shape, BF), (D, tn))
    tril = jnp.tril(jnp.ones((CHUNK, CHUNK), bool))
    small = dict(norm_mix_pre=dg_mix_pre, norm_mix_post=dg_mix_post, norm_ffn_pre=dg_ffn_pre,
                 norm_ffn_post=dg_ffn_post, conv_a_w=dwa, conf_dw_w=dwc, conf_dw_b=dbc, conf_ln_g=dclg,
                 conf_ln_b=dclb, sgu_ln_g=dslg, sgu_ln_b=dslb,
                 sgu_ws=jnp.where(tril[None], dws, 0.0), sgu_b=dbst.T)
    big = dict(w_in=dw_in, w_branch=dw_br, w_out=dw_out, w_ff1=dw_ff1, w_ff2=dw_ff2)
    return dx, big, small


SMALL_NAMES = ("norm_mix_pre", "norm_mix_post", "norm_ffn_pre", "norm_ffn_post", "conv_a_w", "conf_dw_w",
               "conf_dw_b", "conf_ln_g", "conf_ln_b", "sgu_ln_g", "sgu_ln_b", "sgu_b", "sgu_ws")
SMALL_ROWS = dict(norm_mix_pre=1, norm_mix_post=1, norm_ffn_pre=1, norm_ffn_post=1, conv_a_w=KA, conf_dw_w=KC,
                  conf_dw_b=1, conf_ln_g=1, conf_ln_b=1, sgu_ln_g=1, sgu_ln_b=1, sgu_b=1, sgu_ws=CHUNK)
def _pad8(r):
    return -(-r // SUBLANES) * SUBLANES


PACK_ROWS = sum(_pad8(r) for r in SMALL_ROWS.values())


def _pack_small(d):
    parts = []
    for n in SMALL_NAMES:
        r = SMALL_ROWS[n]
        parts.append(jnp.pad(d[n].reshape(r, D).astype(F32), ((0, _pad8(r) - r), (0, 0))))
    return jnp.concatenate(parts, axis=0)


def _unpack_small(a, shapes):
    out, r = {}, 0
    for n in SMALL_NAMES:
        out[n] = a[:, r:r + SMALL_ROWS[n]].reshape((a.shape[0],) + tuple(shapes[n]))
        r += _pad8(SMALL_ROWS[n])
    return out


def _me():
    return lax.axis_index("x"), lax.axis_index("y"), lax.axis_index("c")


def _slab(ref, q, a, h=None):
    r = ref.shape[1]
    rows = slice(None) if h is None else pl.ds(h * (r // 2), r // 2)
    return ref.at[pl.ds(q * a, a), rows, :]


def _rows(ref, h):
    r = ref.shape[-2]
    lead = (slice(None),) * (len(ref.shape) - 2)
    return ref.at[lead + (pl.ds(h * (r // 2), r // 2), slice(None))]


def _rcopy(src, dst, sems, idx, dev):
    return pltpu.make_async_remote_copy(src_ref=src, dst_ref=dst, send_sem=sems[0].at[idx], recv_sem=sems[1].at[idx],
                                        device_id=dev, device_id_type=MESH)


def _send_halves_to_sibling(parts):
    n = len(parts)

    def body(*refs):
        src, dst = refs[:n], refs[n:2 * n]
        sems = refs[2 * n:2 * n + 2]
        x, y, c = _me()
        cps = [_rcopy(_rows(src[i], 1 - c), dst[i], sems, i, (x, y, 1 - c)) for i in range(n)]
        for cp in cps:
            cp.start()
        for cp in cps:
            cp.wait()

    outs = [jax.ShapeDtypeStruct((p.shape[0], p.shape[1] // 2, p.shape[2]), p.dtype) for p in parts]
    return pl.pallas_call(
        body, name="pair_exchange", in_specs=[ANY] * n, out_specs=[ANY] * n, out_shape=outs,
        scratch_shapes=[pltpu.SemaphoreType.DMA((n,)), pltpu.SemaphoreType.DMA((n,))],
    )(*parts)


PAIR_BLOCK_BYTES = 3 * 512 * 1024


def _pair_add(parts, sibs, c):
    n = len(parts)
    steps = 1
    while any(p.shape[0] * (p.shape[1] // 2 // steps) * p.shape[2] * 2 > PAIR_BLOCK_BYTES for p in parts):
        steps *= 2

    def body(c_ref, *refs):
        for p_ref, s_ref, o_ref in zip(refs[:n], refs[n:2 * n], refs[2 * n:]):
            o_ref[...] = (p_ref[...].astype(F32) + s_ref[...].astype(F32)).astype(BF)

    def blk(p):
        return (p.shape[0], p.shape[1] // 2 // steps, p.shape[2])

    mine = [pl.BlockSpec(blk(p), lambda g, c_ref: (0, c_ref[0] * steps + g, 0)) for p in parts]
    same = [pl.BlockSpec(blk(p), lambda g, c_ref: (0, g, 0)) for p in parts]
    return pl.pallas_call(
        body, name="pair_add",
        grid_spec=pltpu.PrefetchScalarGridSpec(
            num_scalar_prefetch=1, grid=(steps,), in_specs=mine + same, out_specs=same),
        out_shape=[jax.ShapeDtypeStruct(s.shape, BF) for s in sibs],
        compiler_params=_cp("arbitrary"),
    )(c, *parts, *sibs)


def _other_chips(x, y):
    return [(1 - x, y), (x, 1 - y), (1 - x, 1 - y)]


def _split_call(name, copies, srcs, lands, sems=None, after=()):
    n, m = len(srcs), len(lands)
    hbm = lambda t: pltpu.HBM(t.shape, t.dtype)
    pin = lambda t: pltpu.with_memory_space_constraint(t, pltpu.HBM)
    thru = [hbm(t) for t in srcs] + [hbm(t) for t in lands]
    sem_spec = pl.BlockSpec(memory_space=pltpu.SEMAPHORE)
    effect = pltpu.CompilerParams(has_side_effects=pltpu.SideEffectType.DATAFLOW_SIDE_EFFECTING)
    if sems is None:
        def start_body(*refs):
            src, land = refs[:n], refs[n:n + m]
            ssem, rsem = refs[n + m + len(after)], refs[n + m + len(after) + 1]
            token = refs[-1]
            cps = copies(src, land, (ssem, rsem))
            for cp in cps:
                cp.start()
            token[...] = jnp.zeros_like(token)

        ncp = copies.count
        out = pl.pallas_call(
            start_body, name=name,
            out_shape=(pltpu.SemaphoreType.DMA((ncp,)), pltpu.SemaphoreType.DMA((ncp,)), *thru,
                       jax.ShapeDtypeStruct((8, 128), F32)),
            in_specs=[ANY] * (n + m + len(after)),
            out_specs=(sem_spec, sem_spec, *([ANY] * (n + m)), pl.BlockSpec(memory_space=pltpu.VMEM)),
            input_output_aliases={i: 2 + i for i in range(n + m)},
            compiler_params=effect,
        )(*[pin(t) for t in srcs], *[pin(t) for t in lands], *after)
        return out[0], out[1], list(out[2:2 + n]), list(out[2 + n:2 + n + m]), out[-1]

    def wait_body(*refs):
        src, land = refs[:n], refs[n:n + m]
        ssem, rsem = refs[n + m], refs[n + m + 1]
        for cp in copies(src, land, (ssem, rsem)):
            cp.wait_send()
            cp.wait_recv()

    out = pl.pallas_call(
        wait_body, name=name, out_shape=tuple(thru),
        in_specs=[ANY] * (n + m) + [sem_spec, sem_spec] + [ANY] * len(after),
        out_specs=tuple([ANY] * (n + m)),
        input_output_aliases={i: i for i in range(n + m)},
        compiler_params=effect,
    )(*srcs, *lands, sems[0], sems[1], *after)
    return list(out[:n]), list(out[n:])


def _cast_into(w, land, layer, kidx, dep):
    _, a, R, C = w.shape
    br = R
    while br * C > 512 * 1024 and br % 32 == 0:
        br //= 2

    def body(k_ref, w_ref, land_ref, dep_ref, o_ref):
        o_ref[...] = w_ref[...].astype(o_ref.dtype)

    return pl.pallas_call(
        body, name="cast_into",
        grid_spec=pltpu.PrefetchScalarGridSpec(
            num_scalar_prefetch=1, grid=(a, R // br),
            in_specs=[pl.BlockSpec((None, None, br, C), lambda e, i, k: (layer, e, i, 0)), ANY, ANY],
            out_specs=pl.BlockSpec((None, br, C), lambda e, i, k: (k[0] * a + e, i, 0))),
        out_shape=jax.ShapeDtypeStruct(land.shape, land.dtype), input_output_aliases={2: 0},
        compiler_params=_cp("arbitrary", "arbitrary"),
    )(kidx, w, land, dep)


class _GatherCopies:
    def __init__(self, n, halves=True):
        self.n, self.count, self.halves = n, 3 * n, halves

    def __call__(self, src, land, sems):
        x, y, c = _me()
        k = 2 * x + y
        cps = []
        for j, (qx, qy) in enumerate(_other_chips(x, y)):
            for i in range(self.n):
                mine = _slab(land[i], k, land[i].shape[0] // NSH, c if self.halves else None)
                cps.append(_rcopy(mine, mine, sems, j * self.n + i, (qx, qy, c)))
        return cps


def _gather_finish(lands):
    n = len(lands)

    def body(*refs):
        dst = refs[n:2 * n]
        sems = refs[2 * n:2 * n + 2]
        x, y, c = _me()
        av = [d.shape[0] // NSH for d in dst]
        cps = []
        for j, (qx, qy) in enumerate(_other_chips(x, y)):
            for i in range(n):
                got = _slab(dst[i], 2 * qx + qy, av[i], c)
                cps.append(_rcopy(got, got, sems, j * n + i, (x, y, 1 - c)))
        for cp in cps:
            cp.start()
        for j, (qx, qy) in enumerate(_other_chips(x, y)):
            for i in range(n):
                other = _slab(dst[i], 2 * qx + qy, av[i], 1 - c)
                _rcopy(other, other, sems, j * n + i, (x, y, c)).wait_recv()
        for cp in cps:
            cp.wait_send()

    return pl.pallas_call(
        body, name="gather_finish", in_specs=[ANY] * n, out_specs=[ANY] * n,
        out_shape=[jax.ShapeDtypeStruct(t.shape, t.dtype) for t in lands],
        input_output_aliases={i: i for i in range(n)},
        scratch_shapes=[pltpu.SemaphoreType.DMA((3 * n,)), pltpu.SemaphoreType.DMA((3 * n,))],
    )(*lands)


class _PairCopies:
    def __init__(self, n):
        self.n, self.count = n, n

    def __call__(self, src, land, sems):
        x, y, c = _me()
        return [_rcopy(_rows(src[i], 1 - c), land[i], sems, i, (x, y, 1 - c)) for i in range(self.n)]


class _ScatterCopies:
    def __init__(self, n):
        self.n, self.count = n, 3 * n

    def __call__(self, src, land, sems):
        x, y, c = _me()
        k = 2 * x + y
        cps = []
        for j, (qx, qy) in enumerate(_other_chips(x, y)):
            for i in range(self.n):
                a = src[i].shape[0] // NSH
                cps.append(_rcopy(_slab(src[i], 2 * qx + qy, a), _slab(land[i], k, a), sems, j * self.n + i,
                                  (qx, qy, c)))
        return cps


def _sum_chips(own, rcv, acc, layer, nlayers, idx):
    A, hr, C = rcv.shape
    a = A // NSH
    br = min(hr, 512)
    nb = hr // br

    def body(*refs):
        r0, r1, r2, r3 = refs[1:5]
        o_ref = refs[-1]
        o_ref[...] = ((r0[...].astype(F32) + r1[...].astype(F32)) + r2[...].astype(F32)) + r3[...].astype(F32)

    slot = lambda s: pl.BlockSpec((None, br, C), lambda e, i, ix: (ix[s] * a + e, i, 0))
    ops = [own, rcv, rcv, rcv]
    in_specs = [slot(0), slot(1), slot(2), slot(3)]
    aliases = {}
    if acc is not None:
        ops.append(acc)
        in_specs.append(ANY)
        aliases = {5: 0}
    return pl.pallas_call(
        body, name="sum_chips",
        grid_spec=pltpu.PrefetchScalarGridSpec(
            num_scalar_prefetch=1, grid=(a, nb), in_specs=in_specs,
            out_specs=pl.BlockSpec((None, None, br, C), lambda e, i, ix: (layer, e, ix[4] * nb + i, 0))),
        out_shape=jax.ShapeDtypeStruct((nlayers, a, 2 * hr, C), F32), input_output_aliases=aliases,
        compiler_params=_cp("arbitrary", "arbitrary"),
    )(idx, *ops)


def _join_halves(fulls):
    n = len(fulls)

    def body(*refs):
        buf = refs[n:2 * n]
        sems = refs[2 * n:2 * n + 2]
        x, y, c = _me()
        cps = [_rcopy(_rows(buf[i], c), _rows(buf[i], c), sems, i, (x, y, 1 - c)) for i in range(n)]
        for cp in cps:
            cp.start()
        for i in range(n):
            _rcopy(_rows(buf[i], 1 - c), _rows(buf[i], 1 - c), sems, i, (x, y, c)).wait_recv()
        for cp in cps:
            cp.wait_send()

    return pl.pallas_call(
        body, name="join_halves", in_specs=[ANY] * n, out_specs=[ANY] * n,
        out_shape=[jax.ShapeDtypeStruct(t.shape, t.dtype) for t in fulls],
        input_output_aliases={i: i for i in range(n)},
        scratch_shapes=[pltpu.SemaphoreType.DMA((n,)), pltpu.SemaphoreType.DMA((n,))],
    )(*fulls)


def _small_blocks(hr):
    br = hr
    while br > 512 and br % 16 == 0:
        br //= 2
    return br, hr // br


def _pair_sum_slot(part, sib, ck):
    R, C = part.shape
    hr = R // 2
    br, nb = _small_blocks(hr)

    def body(ix, p_ref, s_ref, o_ref):
        o_ref[...] = p_ref[...] + s_ref[...]

    return pl.pallas_call(
        body, name="pair_sum_slot",
        grid_spec=pltpu.PrefetchScalarGridSpec(
            num_scalar_prefetch=1, grid=(nb,),
            in_specs=[pl.BlockSpec((br, C), lambda i, ix: (ix[0] * nb + i, 0)),
                      pl.BlockSpec((br, C), lambda i, ix: (i, 0))],
            out_specs=pl.BlockSpec((None, br, C), lambda i, ix: (ix[1], i, 0))),
        out_shape=jax.ShapeDtypeStruct((NSH, hr, C), F32),
        compiler_params=_cp("arbitrary"),
    )(ck, part, sib)


def _sum_slots(slots, ck):
    _, hr, C = slots.shape
    br, nb = _small_blocks(hr)

    def body(ix, s_ref, o_ref):
        o_ref[...] = ((s_ref[0] + s_ref[1]) + s_ref[2]) + s_ref[3]

    return pl.pallas_call(
        body, name="sum_slots",
        grid_spec=pltpu.PrefetchScalarGridSpec(
            num_scalar_prefetch=1, grid=(nb,),
            in_specs=[pl.BlockSpec((NSH, br, C), lambda i, ix: (0, i, 0))],
            out_specs=pl.BlockSpec((br, C), lambda i, ix: (ix[0] * nb + i, 0))),
        out_shape=jax.ShapeDtypeStruct((2 * hr, C), F32),
        compiler_params=_cp("arbitrary"),
    )(ck, slots)


def _adamw(w, g, m, v):
    shape = w.shape
    C = shape[-1]
    R = shape[-2]
    A = 1
    for s in shape[:-2]:
        A *= s
    br = R
    while br * C > 256 * 1024 and br % 16 == 0:
        br //= 2
    c1 = 1.0 / (1.0 - ADAM_B1 ** ADAM_STEP)
    c2 = 1.0 / (1.0 - ADAM_B2 ** ADAM_STEP)

    def body(w_ref, g_ref, m_ref, v_ref, og_ref, d_ref, nm_ref, nv_ref):
        gv = g_ref[...]
        og_ref[...] = gv
        nm = ADAM_B1 * m_ref[...] + (1.0 - ADAM_B1) * gv
        nv = ADAM_B2 * v_ref[...] + (1.0 - ADAM_B2) * (gv * gv)
        nm_ref[...] = nm
        nv_ref[...] = nv
        d_ref[...] = -ADAM_LR * ((nm * c1) / (jnp.sqrt(nv * c2) + ADAM_EPS) + ADAM_WD * w_ref[...])

    blk = pl.BlockSpec((None, br, C), lambda a, i: (a, i, 0))
    outs = pl.pallas_call(
        body, name="adamw", grid=(A, R // br), in_specs=[blk] * 4, out_specs=[blk] * 4,
        out_shape=[jax.ShapeDtypeStruct((A, R, C), F32)] * 4,
        compiler_params=_cp("arbitrary", "arbitrary"),
    )(*(t.reshape(A, R, C) for t in (w, g, m, v)))
    return tuple(o.reshape(shape) for o in outs)


WEIGHTS = ("norm_mix_pre", "norm_mix_post", "norm_ffn_pre", "norm_ffn_post", "w_in", "conv_a_w", "conf_dw_w",
           "conf_dw_b", "conf_ln_g", "conf_ln_b", "sgu_ln_g", "sgu_ln_b", "sgu_ws", "sgu_b", "w_branch", "w_out",
           "w_ff1", "w_ff2")
BIG = ("w_in", "w_branch", "w_out", "w_ff1", "w_ff2")
CONV_ROWS = 48


def kernel(x, norm_mix_pre, norm_mix_post, norm_ffn_pre, norm_ffn_post, w_in, conv_a_w, conf_dw_w, conf_dw_b, conf_ln_g, conf_ln_b, sgu_ln_g, sgu_ln_b, sgu_ws, sgu_b, w_branch, w_out, w_ff1, w_ff2, loss_target, m_norm_mix_pre, m_norm_mix_post, m_norm_ffn_pre, m_norm_ffn_post, m_w_in, m_conv_a_w, m_conf_dw_w, m_conf_dw_b, m_conf_ln_g, m_conf_ln_b, m_sgu_ln_g, m_sgu_ln_b, m_sgu_ws, m_sgu_b, m_w_branch, m_w_out, m_w_ff1, m_w_ff2, v_norm_mix_pre, v_norm_mix_post, v_norm_ffn_pre, v_norm_ffn_post, v_w_in, v_conv_a_w, v_conf_dw_w, v_conf_dw_b, v_conf_ln_g, v_conf_ln_b, v_sgu_ln_g, v_sgu_ln_b, v_sgu_ws, v_sgu_b, v_w_branch, v_w_out, v_w_ff1, v_w_ff2):
    w = dict(norm_mix_pre=norm_mix_pre, norm_mix_post=norm_mix_post, norm_ffn_pre=norm_ffn_pre,
             norm_ffn_post=norm_ffn_post, w_in=w_in, conv_a_w=conv_a_w, conf_dw_w=conf_dw_w, conf_dw_b=conf_dw_b,
             conf_ln_g=conf_ln_g, conf_ln_b=conf_ln_b, sgu_ln_g=sgu_ln_g, sgu_ln_b=sgu_ln_b, sgu_ws=sgu_ws,
             sgu_b=sgu_b, w_branch=w_branch, w_out=w_out, w_ff1=w_ff1, w_ff2=w_ff2)
    mom = dict(norm_mix_pre=m_norm_mix_pre, norm_mix_post=m_norm_mix_post, norm_ffn_pre=m_norm_ffn_pre,
               norm_ffn_post=m_norm_ffn_post, w_in=m_w_in, conv_a_w=m_conv_a_w, conf_dw_w=m_conf_dw_w,
               conf_dw_b=m_conf_dw_b, conf_ln_g=m_conf_ln_g, conf_ln_b=m_conf_ln_b, sgu_ln_g=m_sgu_ln_g,
               sgu_ln_b=m_sgu_ln_b, sgu_ws=m_sgu_ws, sgu_b=m_sgu_b, w_branch=m_w_branch, w_out=m_w_out,
               w_ff1=m_w_ff1, w_ff2=m_w_ff2)
    var = dict(norm_mix_pre=v_norm_mix_pre, norm_mix_post=v_norm_mix_post, norm_ffn_pre=v_norm_ffn_pre,
               norm_ffn_post=v_norm_ffn_post, w_in=v_w_in, conv_a_w=v_conv_a_w, conf_dw_w=v_conf_dw_w,
               conf_dw_b=v_conf_dw_b, conf_ln_g=v_conf_ln_g, conf_ln_b=v_conf_ln_b, sgu_ln_g=v_sgu_ln_g,
               sgu_ln_b=v_sgu_ln_b, sgu_ws=v_sgu_ws, sgu_b=v_sgu_b, w_branch=v_w_branch, w_out=v_w_out,
               w_ff1=v_w_ff1, w_ff2=v_w_ff2)
    L = w_in.shape[0]
    nseq, S, _ = x.shape
    T = nseq * S
    rk = D // NSH
    mx, my, mc = _me()
    k_chip = 2 * mx + my

    big_src = [w_in.reshape(L, 1, D, w_in.shape[2]), w_branch, w_out.reshape(L, 1, rk, D),
               w_ff1.reshape(L, 1, D, w_ff1.shape[2]), w_ff2.reshape(L, 1, w_ff2.shape[1], D)]
    kidx = jnp.reshape(k_chip, (1,)).astype(jnp.int32)
    conv_src = jnp.concatenate(
        [jnp.pad(conv_a_w, ((0, 0), (0, SUBLANES - KA), (0, 0))), jnp.pad(conf_dw_w, ((0, 0), (0, 1), (0, 0))),
         jnp.zeros((L, CONV_ROWS - SUBLANES - KC - 1, rk), F32)], axis=1)[None]

    def early_params(l, g_in, conv_full):
        return dict(
            g_mix_pre=norm_mix_pre[l][None], g_mix_post=norm_mix_post[l][None], g_ffn_pre=norm_ffn_pre[l][None],
            g_ffn_post=norm_ffn_post[l][None], w_in=g_in, conv_a_w=conv_full[l, :KA],
            conf_dw_w=conv_full[l, SUBLANES:SUBLANES + KC], conf_dw_b=conf_dw_b[l][None],
            conf_ln_g=conf_ln_g[l][None], conf_ln_b=conf_ln_b[l][None], sgu_ln_g=sgu_ln_g[l][None],
            sgu_ln_b=sgu_ln_b[l][None], sgu_ws=sgu_ws[l], sgu_wst=jnp.swapaxes(sgu_ws[l], 1, 2),
            sgu_bt=sgu_b[l].T)

    def late_params(gathered):
        g_br, g_out, g_ff1, g_ff2 = gathered
        return dict(w_branch=g_br.reshape(NSH, 3, rk, D), w_out=g_out.reshape(D, D), w_ff1=g_ff1,
                    w_ff2=g_ff2.reshape(NSH * w_ff2.shape[1], D))

    def cast_lands(srcs, l, dep):
        return [_cast_into(s, lax.empty((NSH * s.shape[1],) + s.shape[2:], F32 if s is conv_src else BF), l, kidx,
                           dep) for s in srcs]

    def gather_start(name, lands, after):
        return _split_call(name, _GatherCopies(len(lands)), [], lands, after=after)

    def gather_land(name, flight, after):
        ssem, rsem, _, lands, _ = flight
        _, lands = _split_call(name, _GatherCopies(len(lands)), [], lands, (ssem, rsem), after)
        return _gather_finish(lands)

    zero_tok = jnp.zeros((8, 128), F32)
    xt = x.reshape(T, D)
    layers, saved = [], []
    head = gather_start("gather_start_0a", cast_lands([big_src[0], conv_src], 0, kidx), [])
    tails = [cast_lands(big_src[1:], l, head[4]) for l in range(L)]
    heads = [None] + [cast_lands(big_src[:1], l, head[4]) for l in range(1, L)]
    behind = [xt] + [t for ls in tails + heads[1:] for t in ls]
    conv_full = None
    for l in range(L):
        got = gather_land(f"gather_wait_{l}a", head, behind if l == 0 else [xt])
        g_in = got[0]
        if l == 0:
            conv_full = got[1].reshape(NSH, L, CONV_ROWS, rk).transpose(1, 2, 0, 3).reshape(L, CONV_ROWS, D)
        tail = gather_start(f"gather_start_{l}b", tails[l], [g_in])
        nxt = {}

        def late(after, l=l, tail=tail, nxt=nxt):
            more = late_params(gather_land(f"gather_wait_{l}b", tail, [after]))
            if l + 1 == L:
                return more, zero_tok
            nxt["head"] = gather_start(f"gather_start_{l + 1}a", heads[l + 1], [more["w_ff1"]])
            return more, nxt["head"][4]

        p = early_params(l, g_in, conv_full)
        xt, sv = _layer_fwd(xt, p, S, tail[4], late)
        head = nxt.get("head")
        layers.append(p)
        saved.append(sv)
    dx, loss_row = _loss_head(xt, loss_target.reshape(T, D))
    loss = lax.psum(loss_row[0, 0], ("x", "y", "c"))

    c_arr = jnp.reshape(mc, (1,)).astype(jnp.int32)
    idx = jnp.stack([k_chip, k_chip ^ 2, k_chip ^ 1, k_chip ^ 3, mc]).astype(jnp.int32)
    fulls = {n: None for n in BIG}
    smalls = [None] * L

    def pair_start(tag, parts):
        lands = [lax.empty((p.shape[0], p.shape[1] // 2, p.shape[2]), p.dtype) for p in parts]
        return _split_call(f"pair_start_{tag}", _PairCopies(len(parts)), parts, lands)

    def pair_land_scatter_start(tag, fl, after):
        ssem, rsem, parts, sib, _ = fl
        parts, sib = _split_call(f"pair_wait_{tag}", _PairCopies(len(parts)), parts, sib, (ssem, rsem), after)
        sums = _pair_add(parts, sib, c_arr)
        rcv = [lax.empty(s.shape, s.dtype) for s in sums]
        return _split_call(f"scatter_start_{tag}", _ScatterCopies(len(sums)), sums, rcv)

    def scatter_land(tag, fl, names, l, after):
        ssem, rsem, sums, rcv, _ = fl
        sums, rcv = _split_call(f"scatter_wait_{tag}", _ScatterCopies(len(sums)), sums, rcv, (ssem, rsem), after)
        for n, o, r in zip(names, sums, rcv):
            fulls[n] = _sum_chips(o, r, fulls[n], l, L, idx)

    pending = []
    pair_b = None
    dep = zero_tok
    for l in reversed(range(L)):
        mine = {}

        def after_ffn(arr, l=l, mine=mine, pair_b=pair_b):
            if pair_b is None:
                return zero_tok
            mine["prev_b"] = pair_land_scatter_start(f"{l + 1}b", pair_b, [arr])
            return mine["prev_b"][4]

        def early(parts, l=l, mine=mine):
            br, rest = parts[0], parts[1:]
            mine["pair_a"] = pair_start(f"{l}a", [br.reshape(NSH * 3, rk, D), *rest])
            return mine["pair_a"][4]

        def mid(arr, l=l, mine=mine):
            mine["a"] = pair_land_scatter_start(f"{l}a", mine["pair_a"], [arr])
            return mine["a"][4]

        dx, big, small = _layer_bwd(dx, layers[l], saved[l], S, dep, (after_ffn, early, mid))
        smalls[l] = _pack_small(small)
        for args in pending:
            scatter_land(*args, [dx])
        pending = [(f"{l}a", mine["a"], BIG[1:], l)]
        if "prev_b" in mine:
            pending.append((f"{l + 1}b", mine["prev_b"], BIG[:1], l + 1))
        pair_b = pair_start(f"{l}b", [big["w_in"]])
        dep = pair_b[4]
    last_b = ("0b", pair_land_scatter_start("0b", pair_b, [dx]), BIG[:1], 0)

    packed = jnp.concatenate(smalls, axis=0)
    nrow = packed.shape[0]
    ck = jnp.stack([mc, k_chip]).astype(jnp.int32)
    (sib,) = _send_halves_to_sibling([packed.reshape(1, nrow, D)])
    slots = _pair_sum_slot(packed, sib.reshape(nrow // 2, D), ck)
    small_flight = _split_call("small_start", _GatherCopies(1, halves=False), [], [slots])

    for args in pending:
        scatter_land(*args, [small_flight[4], last_b[1][4]])
    grads, delta, new_m, new_v = {}, {}, {}, {}
    for n, f in zip(BIG[1:], _join_halves([fulls[n] for n in BIG[1:]])):
        grads[n], delta[n], new_m[n], new_v[n] = _adamw(w[n], f.reshape(w[n].shape), mom[n], var[n])
    scatter_land(*last_b, [delta[BIG[-1]]])
    (f,) = _join_halves([fulls[BIG[0]]])
    n = BIG[0]
    grads[n], delta[n], new_m[n], new_v[n] = _adamw(w[n], f.reshape(w[n].shape), mom[n], var[n])

    _, (slots,) = _split_call("small_wait", _GatherCopies(1, halves=False), [], small_flight[3],
                              (small_flight[0], small_flight[1]), [delta[BIG[0]]])
    (small_sum,) = _join_halves([_sum_slots(slots, ck).reshape(1, 1, nrow, D)])
    shapes = {n: (w[n].shape[1:] if n not in ("conv_a_w", "conf_dw_w") else (w[n].shape[1], D)) for n in SMALL_NAMES}
    sg = _unpack_small(small_sum.reshape(L, PACK_ROWS, D), shapes)
    for n in SMALL_NAMES:
        if n in ("conv_a_w", "conf_dw_w"):
            grads[n] = lax.dynamic_slice_in_dim(sg[n], k_chip * rk, rk, axis=2)
        else:
            grads[n] = sg[n]

    for n in SMALL_NAMES:
        sh = w[n].shape
        flat = (sh[0] * sh[1], sh[2]) if n in ("conv_a_w", "conf_dw_w") else (-1, D)
        g, d, nm, nv = _adamw(*(t.reshape(flat) for t in (w[n], grads[n], mom[n], var[n])))
        grads[n], delta[n], new_m[n], new_v[n] = g.reshape(sh), d.reshape(sh), nm.reshape(sh), nv.reshape(sh)

    return (loss, dx.reshape(x.shape), *[grads[n] for n in WEIGHTS], *[delta[n] for n in WEIGHTS],
            *[new_m[n] for n in WEIGHTS], *[new_v[n] for n in WEIGHTS])
```

```python
import functools

import jax
import jax.numpy as jnp
from jax import lax
from jax.experimental import pallas as pl
from jax.experimental.pallas import tpu as pltpu

D = 1024
HEADS = 8
CHUNK = 128
KA = 3
KC = 31
HALO = 32
SUBLANES = 8
MIX_TILE = 512
WGRAD_TILE = 1024
NSH = 4
NDEV = 8
EPS = 1e-6
BF = jnp.bfloat16
F32 = jnp.float32
VMEM_LIMIT = 56 * 1024 * 1024

ADAM_LR = 0.001
ADAM_B1 = 0.9
ADAM_B2 = 0.999
ADAM_EPS = 1e-08
ADAM_WD = 0.01
ADAM_STEP = 10

MESH = pl.DeviceIdType.MESH
ANY = pl.BlockSpec(memory_space=pl.ANY)


def _cp(*sem):
    return pltpu.CompilerParams(dimension_semantics=sem, vmem_limit_bytes=VMEM_LIMIT)


def _sig(x):
    return 1.0 / (1.0 + jnp.exp(-x))


_GC = 0.7978845608028654


def _gelu(x):
    x2 = x * x
    t = jnp.tanh(_GC * x * (1.0 + 0.044715 * x2))
    y = 0.5 * x * (1.0 + t)
    dy = 0.5 * (1.0 + t) + 0.5 * x * (1.0 - t * t) * _GC * (1.0 + 3.0 * 0.044715 * x2)
    return y, dy


def _rms_fwd(x, g):
    r = lax.rsqrt(jnp.mean(x * x, axis=-1, keepdims=True) + EPS)
    return x * r * g


def _rms_bwd(dy, x, g):
    r = lax.rsqrt(jnp.mean(x * x, axis=-1, keepdims=True) + EPS)
    xn = x * r
    dyg = dy * g
    dx = r * (dyg - xn * jnp.mean(dyg * xn, axis=-1, keepdims=True))
    return dx, jnp.sum(dy * xn, axis=0, keepdims=True)


def _ln_stats(x):
    mu = jnp.mean(x, axis=-1, keepdims=True)
    xc = x - mu
    r = lax.rsqrt(jnp.mean(xc * xc, axis=-1, keepdims=True) + EPS)
    return xc * r, r


def _ln_bwd(dn, n, r):
    return r * (dn - jnp.mean(dn, axis=-1, keepdims=True) - n * jnp.mean(dn * n, axis=-1, keepdims=True))


def _dot(a, b):
    return jnp.dot(a, b, preferred_element_type=F32)


def _dot_nt(a, b):
    return lax.dot_general(a, b, (((1,), (1,)), ((), ())), preferred_element_type=F32)


def _dot_tn(a, b):
    return lax.dot_general(a, b, (((0,), (0,)), ((), ())), preferred_element_type=F32)


def _in_proj(x, g, w, dep):
    T = x.shape[0]
    nc = w.shape[2]
    tm = min(T, 1024)
    tn = nc
    nj = nc // tn

    def body(x_ref, g_ref, w_ref, dep_ref, h_ref, z_ref, h_scr):
        @pl.when((pl.program_id(1) == 0) & (pl.program_id(2) == 0))
        def _():
            h = _rms_fwd(x_ref[...], g_ref[...]).astype(BF)
            h_scr[...] = h
            h_ref[...] = h
        z_ref[...] = _dot(h_scr[...], w_ref[...]).astype(BF)

    return pl.pallas_call(
        body, name="in_proj", grid=(T // tm, NSH, nj),
        in_specs=[pl.BlockSpec((tm, D), lambda i, k, j: (i, 0)),
                  pl.BlockSpec((1, D), lambda i, k, j: (0, 0)),
                  pl.BlockSpec((None, D, tn), lambda i, k, j: (k, 0, j)), ANY],
        out_specs=[pl.BlockSpec((tm, D), lambda i, k, j: (i, 0)),
                   pl.BlockSpec((tm, tn), lambda i, k, j: (i, k * nj + j))],
        out_shape=[jax.ShapeDtypeStruct((T, D), BF), jax.ShapeDtypeStruct((T, NSH * nc), BF)],
        scratch_shapes=[pltpu.VMEM((tm, D), BF)],
        compiler_params=_cp("arbitrary", "arbitrary", "arbitrary"),
    )(x, g, w, dep)


def _tile_specs(tt, nt_total, reverse):
    def tile(i):
        return (nt_total - 1 - i) if reverse else i

    def cur(c):
        return pl.BlockSpec((tt, D), lambda i, *_: (tile(i), c))

    def halo(c):
        return pl.BlockSpec((HALO, D), lambda i, *_: (jnp.maximum(tile(i) * (tt // HALO) - 1, 0), c))

    def row(r=1):
        return pl.BlockSpec((r, D), lambda i, *_: (0, 0))

    return tile, cur, halo, row


RC = 16


def _chunks(tt, fn, group=2):
    def step(c, carry):
        for u in range(group):
            fn(pl.multiple_of((c * group + u) * RC, RC))
        return carry
    lax.fori_loop(0, tt // (RC * group), step, 0)


ALL_SHIFTS = tuple(range(SUBLANES))


def _shifts_of(offs):
    return tuple(sorted({o % SUBLANES for o in offs}))


def _shifted_copies(ext, sh, nrows, shifts=ALL_SHIFTS):
    for i, s in enumerate(shifts):
        sh[i] = ext[pl.ds(s, nrows), :]


def _window(sh, o, r0, shifts=ALL_SHIFTS):
    return sh[shifts.index(o % SUBLANES), pl.ds(r0 + (o // SUBLANES) * SUBLANES, RC), :]


def _fill_taps(wb, w_ref, ntap):
    for k in range(ntap):
        wb[k * SUBLANES:(k + 1) * SUBLANES, :] = jnp.broadcast_to(w_ref[k:k + 1, :], (SUBLANES, D))


def _conv_chunks(sh, wb, offs, r0s, shifts=ALL_SHIFTS):
    accs = []
    for r0 in r0s:
        acc = None
        for k, o in enumerate(offs):
            wk = wb[k * SUBLANES:(k + 1) * SUBLANES, :]
            term = jnp.concatenate([wk] * (RC // SUBLANES), axis=0) * _window(sh, o, r0, shifts)
            acc = term if acc is None else acc + term
        accs.append(acc)
    return accs


WG_TAPS = 5


def _conv_wgrad_chunked(dw_ref, d_ref, sh, offs, tt, shifts=ALL_SHIFTS):
    for g0 in range(0, len(offs), WG_TAPS):
        grp = offs[g0:g0 + WG_TAPS]

        def step(c, accs, grp=grp):
            for u in range(2):
                r0 = pl.multiple_of((2 * c + u) * SUBLANES, SUBLANES)
                d = d_ref[pl.ds(r0, SUBLANES), :]
                accs = tuple(
                    a + d * sh[shifts.index(o % SUBLANES), pl.ds(r0 + (o // SUBLANES) * SUBLANES, SUBLANES), :]
                    for a, o in zip(accs, grp))
            return accs
        accs = lax.fori_loop(0, tt // (2 * SUBLANES), step,
                             tuple(jnp.zeros((SUBLANES, D), F32) for _ in grp))
        for j, a in enumerate(accs):
            dw_ref[g0 + j:g0 + j + 1, :] += jnp.sum(a, axis=0, keepdims=True)


def _causal_offsets(ntap):
    return [HALO - (ntap - 1) + k for k in range(ntap)]


def _anticausal_offsets(ntap):
    return [ntap - 1 - k for k in range(ntap)]


def _mix_a_fwd(z, wa, S):
    T = z.shape[0]
    tt = min(S, MIX_TILE)
    nt = S // tt
    _, cur, halo, row = _tile_specs(tt, T // tt, False)

    nrows = HALO + tt
    offs = _causal_offsets(KA)
    shifts = _shifts_of(offs)

    def body(ah, ab, ac, ah_h, ac_h, w_ref, y_ref, ext, sh, wb):
        @pl.when(pl.program_id(0) == 0)
        def _():
            _fill_taps(wb, w_ref, KA)
            ext[nrows:, :] = jnp.zeros((SUBLANES, D), F32)

        first = (pl.program_id(0) % nt) == 0
        ph = ah_h[...].astype(F32) * ac_h[...].astype(F32)
        ext[0:HALO, :] = jnp.where(first, 0.0, ph)

        def prod(r0):
            rows = pl.ds(r0, RC)
            ext[pl.ds(HALO + r0, RC), :] = ah[rows, :].astype(F32) * ac[rows, :].astype(F32)
        _chunks(tt, prod)
        _shifted_copies(ext, sh, nrows, shifts)

        def conv(r0):
            rows = pl.ds(r0, RC)
            (q,) = _conv_chunks(sh, wb, offs, (r0,), shifts)
            y_ref[rows, :] = (ab[rows, :].astype(F32) * q).astype(BF)
        _chunks(tt, conv)

    return pl.pallas_call(
        body, name="mix_a_fwd", grid=(T // tt,),
        in_specs=[cur(0), cur(1), cur(2), halo(0), halo(2), row(KA)],
        out_specs=pl.BlockSpec((tt, D), lambda i: (i, 0)),
        out_shape=jax.ShapeDtypeStruct((T, D), BF),
        scratch_shapes=[pltpu.VMEM((nrows + SUBLANES, D), F32), pltpu.VMEM((len(shifts), nrows, D), F32),
                        pltpu.VMEM((KA * SUBLANES, D), F32)],
        compiler_params=_cp("arbitrary"),
    )(z, z, z, z, z, wa)


def _mix_b_fwd(z, wc, bc, lg, lb, S):
    T = z.shape[0]
    tt = min(S, MIX_TILE)
    nt = S // tt
    _, cur, halo, row = _tile_specs(tt, T // tt, False)

    nrows = HALO + tt
    offs = _causal_offsets(KC)

    def body(ca, cg, ca_h, cg_h, w_ref, bc_ref, lg_ref, lb_ref, y_ref, s_ref, ext, sh, wb):
        @pl.when(pl.program_id(0) == 0)
        def _():
            _fill_taps(wb, w_ref, KC)
            ext[nrows:, :] = jnp.zeros((SUBLANES, D), F32)

        first = (pl.program_id(0) % nt) == 0
        rh = ca_h[...].astype(F32) * _sig(cg_h[...].astype(F32))
        ext[0:HALO, :] = jnp.where(first, 0.0, rh)

        def glu(r0):
            rows = pl.ds(r0, RC)
            ext[pl.ds(HALO + r0, RC), :] = ca[rows, :].astype(F32) * _sig(cg[rows, :].astype(F32))
        _chunks(tt, glu)
        _shifted_copies(ext, sh, nrows)

        def conv(r0):
            rows = pl.ds(r0, RC)
            (q,) = _conv_chunks(sh, wb, offs, (r0,))
            s = q + bc_ref[...]
            s_ref[rows, :] = s.astype(BF)
            n, _ = _ln_stats(s)
            t = n * lg_ref[...] + lb_ref[...]
            y_ref[rows, :] = (t * _sig(t)).astype(BF)
        _chunks(tt, conv)

    return pl.pallas_call(
        body, name="mix_b_fwd", grid=(T // tt,),
        in_specs=[cur(3), cur(4), halo(3), halo(4), row(KC), row(), row(), row()],
        out_specs=[pl.BlockSpec((tt, D), lambda i: (i, 0))] * 2,
        out_shape=[jax.ShapeDtypeStruct((T, D), BF)] * 2,
        scratch_shapes=[pltpu.VMEM((nrows + SUBLANES, D), F32), pltpu.VMEM((SUBLANES, nrows, D), F32),
                        pltpu.VMEM((KC * SUBLANES, D), F32)],
        compiler_params=_cp("arbitrary"),
    )(z, z, z, z, wc, bc, lg, lb)


def _causal_mask(transposed):
    r = lax.broadcasted_iota(jnp.int32, (CHUNK, CHUNK), 0)
    c = lax.broadcasted_iota(jnp.int32, (CHUNK, CHUNK), 1)
    return (c >= r) if transposed else (r >= c)


def _mix_s_fwd(z, lg, lb, ws, bst, S):
    T = z.shape[0]
    tt = min(S, MIX_TILE)
    _, cur, _, row = _tile_specs(tt, T // tt, False)

    def body(su, sv, lg_ref, lb_ref, ws_ref, bst_ref, y_ref, u_scr, vn_scr):
        u_scr[...] = _gelu(su[...].astype(F32))[0]
        n, _ = _ln_stats(_gelu(sv[...].astype(F32))[0])
        vn_scr[...] = (n * lg_ref[...] + lb_ref[...]).astype(BF)
        mask = _causal_mask(False)
        for h in range(HEADS):
            wm = jnp.where(mask, ws_ref[h], 0.0).astype(BF)
            cols = slice(h * CHUNK, (h + 1) * CHUNK)
            for c in range(tt // CHUNK):
                rows = slice(c * CHUNK, (c + 1) * CHUNK)
                mixed = _dot(wm, vn_scr[rows, cols]) + bst_ref[:, h:h + 1]
                y_ref[rows, cols] = (u_scr[rows, cols] * mixed).astype(BF)

    return pl.pallas_call(
        body, name="mix_s_fwd", grid=(T // tt,),
        in_specs=[cur(5), cur(6), row(), row(),
                  pl.BlockSpec((HEADS, CHUNK, CHUNK), lambda i: (0, 0, 0)),
                  pl.BlockSpec((CHUNK, HEADS), lambda i: (0, 0))],
        out_specs=pl.BlockSpec((tt, D), lambda i: (i, 0)),
        out_shape=jax.ShapeDtypeStruct((T, D), BF),
        scratch_shapes=[pltpu.VMEM((tt, D), F32), pltpu.VMEM((tt, D), BF)],
        compiler_params=_cp("arbitrary"),
    )(z, z, lg, lb, ws, bst)


def _mix_out_fwd(ya, yc, ys, z, x, wb, wo, gp, dep):
    T = x.shape[0]
    tm = min(T, 512)
    rk = D // NSH

    def body(ya_ref, yc_ref, ys_ref, ga, gc, gs, x_ref, wb_ref, wo_ref, gp_ref, dep_ref,
             p_ref, mg_ref, m_ref, x1_ref):
        acc = None
        for b, (y_ref, g_ref) in enumerate(((ya_ref, ga), (yc_ref, gc), (ys_ref, gs))):
            pb = None
            for k in range(NSH):
                part = _dot(y_ref[:, k * rk:(k + 1) * rk], wb_ref[k, b])
                pb = part if pb is None else pb + part
            p_ref[b] = pb.astype(BF)
            term = _sig(g_ref[...].astype(F32)) * pb
            acc = term if acc is None else acc + term
        mg = acc.astype(BF)
        mg_ref[...] = mg
        m = _dot(mg, wo_ref[...])
        m_ref[...] = m.astype(BF)
        x1_ref[...] = x_ref[...] + _rms_fwd(m, gp_ref[...])

    rowblk = pl.BlockSpec((tm, D), lambda i: (i, 0))
    return pl.pallas_call(
        body, name="mix_out_fwd", grid=(T // tm,),
        in_specs=[rowblk, rowblk, rowblk,
                  pl.BlockSpec((tm, D), lambda i: (i, 7)), pl.BlockSpec((tm, D), lambda i: (i, 8)),
                  pl.BlockSpec((tm, D), lambda i: (i, 9)), rowblk,
                  pl.BlockSpec((NSH, 3, rk, D), lambda i: (0, 0, 0, 0)),
                  pl.BlockSpec((D, D), lambda i: (0, 0)),
                  pl.BlockSpec((1, D), lambda i: (0, 0)), ANY],
        out_specs=[pl.BlockSpec((3, tm, D), lambda i: (0, i, 0)), rowblk, rowblk, rowblk],
        out_shape=[jax.ShapeDtypeStruct((3, T, D), BF), jax.ShapeDtypeStruct((T, D), BF),
                   jax.ShapeDtypeStruct((T, D), BF), jax.ShapeDtypeStruct((T, D), F32)],
        compiler_params=_cp("arbitrary"),
    )(ya, yc, ys, z, z, z, x, wb, wo, gp, dep)


def _ffn_fwd(x1, g3, w1, w2, g4):
    T = x1.shape[0]
    tm = min(T, 1024)

    def body(x_ref, g3_ref, w1_ref, w2_ref, g4_ref, h_ref, a_ref, f_ref, x2_ref, h_scr, acc):
        k = pl.program_id(1)

        @pl.when(k == 0)
        def _():
            h = _rms_fwd(x_ref[...], g3_ref[...]).astype(BF)
            h_scr[...] = h
            h_ref[...] = h
            acc[...] = jnp.zeros_like(acc)

        a = _dot(h_scr[...], w1_ref[...])
        a_ref[...] = a.astype(BF)
        r = jnp.maximum(a, 0.0)
        acc[...] += _dot((r * r).astype(BF), w2_ref[...])

        @pl.when(k == NSH - 1)
        def _():
            f = acc[...]
            f_ref[...] = f.astype(BF)
            x2_ref[...] = x_ref[...] + _rms_fwd(f, g4_ref[...])

    rowblk = pl.BlockSpec((tm, D), lambda i, k: (i, 0))
    vec = pl.BlockSpec((1, D), lambda i, k: (0, 0))
    return pl.pallas_call(
        body, name="ffn_fwd", grid=(T // tm, NSH),
        in_specs=[rowblk, vec, pl.BlockSpec((None, D, D), lambda i, k: (k, 0, 0)),
                  pl.BlockSpec((D, D), lambda i, k: (k, 0)), vec],
        out_specs=[rowblk, pl.BlockSpec((tm, D), lambda i, k: (i, k)), rowblk, rowblk],
        out_shape=[jax.ShapeDtypeStruct((T, D), BF), jax.ShapeDtypeStruct((T, NSH * D), BF),
                   jax.ShapeDtypeStruct((T, D), BF), jax.ShapeDtypeStruct((T, D), F32)],
        scratch_shapes=[pltpu.VMEM((tm, D), BF), pltpu.VMEM((tm, D), F32)],
        compiler_params=_cp("arbitrary", "arbitrary"),
    )(x1, g3, w1, w2, g4)


def _loss_head(y, target):
    T = y.shape[0]
    tm = min(T, 512)

    def body(y_ref, t_ref, dy_ref, l_ref):
        @pl.when(pl.program_id(0) == 0)
        def _():
            l_ref[...] = jnp.zeros_like(l_ref)
        e = y_ref[...] - t_ref[...]
        dy_ref[...] = e * (1.0 / D)
        l_ref[...] += jnp.sum(e * e) * (0.5 / D)

    rowblk = pl.BlockSpec((tm, D), lambda i: (i, 0))
    return pl.pallas_call(
        body, name="loss_head", grid=(T // tm,),
        in_specs=[rowblk, rowblk],
        out_specs=[rowblk, pl.BlockSpec((1, 128), lambda i: (0, 0))],
        out_shape=[jax.ShapeDtypeStruct((T, D), F32), jax.ShapeDtypeStruct((1, 128), F32)],
        compiler_params=_cp("arbitrary"),
    )(y, target)


def _ffn_bwd(dx2, f, g4, a, w2, w1, x1, g3, dep):
    T = dx2.shape[0]
    tm = min(T, 1024)
    tf = 512
    per = w1.shape[2] // tf
    nk = NSH * per

    def body(dx2_ref, f_ref, g4_ref, a_ref, w2_ref, w1_ref, x1_ref, g3_ref, dep_ref,
             df_ref, da_ref, dx1_ref, dg4_ref, dg3_ref, df_scr, acc):
        i, k = pl.program_id(0), pl.program_id(1)

        @pl.when((i == 0) & (k == 0))
        def _():
            dg4_ref[...] = jnp.zeros_like(dg4_ref)
            dg3_ref[...] = jnp.zeros_like(dg3_ref)

        @pl.when(k == 0)
        def _():
            df, dg = _rms_bwd(dx2_ref[...], f_ref[...].astype(F32), g4_ref[...])
            dg4_ref[...] += dg
            dfb = df.astype(BF)
            df_scr[...] = dfb
            df_ref[...] = dfb
            acc[...] = jnp.zeros_like(acc)

        av = a_ref[...].astype(F32)
        da = (_dot_nt(df_scr[...], w2_ref[...]) * (2.0 * jnp.maximum(av, 0.0))).astype(BF)
        da_ref[...] = da
        acc[...] += _dot_nt(da, w1_ref[...])

        @pl.when(k == nk - 1)
        def _():
            dx, dg = _rms_bwd(acc[...], x1_ref[...], g3_ref[...])
            dg3_ref[...] += dg
            dx1_ref[...] = dx2_ref[...] + dx

    rowblk = pl.BlockSpec((tm, D), lambda i, k: (i, 0))
    vec = pl.BlockSpec((1, D), lambda i, k: (0, 0))
    return pl.pallas_call(
        body, name="ffn_bwd", grid=(T // tm, nk),
        in_specs=[rowblk, rowblk, vec, pl.BlockSpec((tm, tf), lambda i, k: (i, k)),
                  pl.BlockSpec((tf, D), lambda i, k: (k, 0)),
                  pl.BlockSpec((None, D, tf), lambda i, k: (k // per, 0, k % per)), rowblk, vec, ANY],
        out_specs=[rowblk, pl.BlockSpec((tm, tf), lambda i, k: (i, k)), rowblk, vec, vec],
        out_shape=[jax.ShapeDtypeStruct((T, D), BF), jax.ShapeDtypeStruct((T, NSH * D), BF),
                   jax.ShapeDtypeStruct((T, D), F32), jax.ShapeDtypeStruct((1, D), F32),
                   jax.ShapeDtypeStruct((1, D), F32)],
        scratch_shapes=[pltpu.VMEM((tm, D), BF), pltpu.VMEM((tm, D), F32)],
        compiler_params=_cp("arbitrary", "arbitrary"),
    )(dx2, f, g4, a, w2, w1, x1, g3, dep)


def _wgrad(name, ops, grid, in_specs, out_spec, out_shape, acc_shape, pick=None, relu2=False):
    nt = grid[-1]
    na = len(ops) - 1

    def body(*refs):
        a_refs, b_ref, o_ref, acc = refs[:na], refs[na], refs[na + 1], refs[na + 2]
        t = pl.program_id(len(grid) - 1)

        @pl.when(t == 0)
        def _():
            acc[...] = jnp.zeros_like(acc)

        def add(a_ref):
            av = a_ref[...]
            if relu2:
                r = jnp.maximum(av.astype(F32), 0.0)
                av = (r * r).astype(BF)
            acc[...] += _dot_tn(av, b_ref[...])

        if na == 1:
            add(a_refs[0])
        else:
            sel = pick()
            for n in range(na):
                pl.when(sel == n)(functools.partial(add, a_refs[n]))

        @pl.when(t == nt - 1)
        def _():
            if len(o_ref.shape) == 2:
                o_ref[...] = acc[...].astype(o_ref.dtype)
            else:
                rs = o_ref.shape[1]
                for q in range(o_ref.shape[0]):
                    o_ref[q] = acc[q * rs:(q + 1) * rs, :].astype(o_ref.dtype)

    return pl.pallas_call(
        body, name=name, grid=grid, in_specs=in_specs, out_specs=out_spec, out_shape=out_shape,
        scratch_shapes=[pltpu.VMEM(acc_shape, F32)],
        compiler_params=_cp(*(["arbitrary"] * len(grid))),
    )(*ops)


def _mix_out_bwd(dx1, m, gp, wo, p3, z, wb, dep):
    T = dx1.shape[0]
    tm = min(T, 512)
    rk = D // NSH

    def body(dx1_ref, m_ref, gp_ref, wo_ref, p_ref, g_ref, wb_ref, dep_ref,
             dm_ref, dp_ref, dy_ref, dz_ref, dgp_ref, dmg):
        i, b = pl.program_id(0), pl.program_id(1)

        @pl.when((i == 0) & (b == 0))
        def _():
            dgp_ref[...] = jnp.zeros_like(dgp_ref)

        @pl.when(b == 0)
        def _():
            dm, dg = _rms_bwd(dx1_ref[...], m_ref[...].astype(F32), gp_ref[...])
            dgp_ref[...] += dg
            dmb = dm.astype(BF)
            dm_ref[...] = dmb
            dmg[...] = _dot_nt(dmb, wo_ref[...])

        gate = _sig(g_ref[...].astype(F32))
        d = dmg[...]
        dp = (d * gate).astype(BF)
        dp_ref[...] = dp
        dz_ref[...] = (d * p_ref[...].astype(F32) * gate * (1.0 - gate)).astype(BF)
        for k in range(NSH):
            dy_ref[:, k * rk:(k + 1) * rk] = _dot_nt(dp, wb_ref[k, b]).astype(BF)

    rowblk = pl.BlockSpec((tm, D), lambda i, b: (i, 0))
    br = pl.BlockSpec((None, tm, D), lambda i, b: (b, i, 0))
    vec = pl.BlockSpec((1, D), lambda i, b: (0, 0))
    return pl.pallas_call(
        body, name="mix_out_bwd", grid=(T // tm, 3),
        in_specs=[rowblk, rowblk, vec, pl.BlockSpec((D, D), lambda i, b: (0, 0)), br,
                  pl.BlockSpec((tm, D), lambda i, b: (i, 7 + b)),
                  pl.BlockSpec((NSH, 3, rk, D), lambda i, b: (0, 0, 0, 0)), ANY],
        out_specs=[rowblk, br, br, pl.BlockSpec((tm, D), lambda i, b: (i, 7 + b)), vec],
        out_shape=[jax.ShapeDtypeStruct((T, D), BF), jax.ShapeDtypeStruct((3, T, D), BF),
                   jax.ShapeDtypeStruct((3, T, D), BF), jax.ShapeDtypeStruct((T, 10 * D), BF),
                   jax.ShapeDtypeStruct((1, D), F32)],
        scratch_shapes=[pltpu.VMEM((tm, D), F32)],
        compiler_params=_cp("arbitrary", "arbitrary"),
    )(dx1, m, gp, wo, p3, z, wb, dep)


def _mix_a_bwd(dz, dy3, z, wa, S, dep):
    T = z.shape[0]
    tt = min(S, MIX_TILE)
    nt = S // tt
    ntt = T // tt
    tile, cur, halo, row = _tile_specs(tt, ntt, True)

    nrows = HALO + tt
    coffs, aoffs = _causal_offsets(KA), _anticausal_offsets(KA)
    cshifts, ashifts = _shifts_of(coffs), _shifts_of(aoffs)

    def body(dz_in, dy_ref, ah, ab, ac, ah_h, ac_h, w_ref, dep_ref, dz_ref, dw_ref, ext_p, ext_d, sh, wb, stage):
        i, b = pl.program_id(0), pl.program_id(1)
        ti = ntt - 1 - i

        @pl.when((i == 0) & (b == 0))
        def _():
            dw_ref[...] = jnp.zeros_like(dw_ref)
            ext_d[...] = jnp.zeros_like(ext_d)
            ext_p[nrows:, :] = jnp.zeros((SUBLANES, D), F32)
            _fill_taps(wb, w_ref, KA)

        @pl.when(b == 0)
        def _():
            first = (ti % nt) == 0
            last = (ti % nt) == nt - 1
            ext_p[0:HALO, :] = jnp.where(first, 0.0, ah_h[...].astype(F32) * ac_h[...].astype(F32))
            ext_d[tt:nrows, :] = jnp.where(last, 0.0, ext_d[0:HALO, :])

            def prod(r0):
                rows = pl.ds(r0, RC)
                ext_p[pl.ds(HALO + r0, RC), :] = ah[rows, :].astype(F32) * ac[rows, :].astype(F32)
            _chunks(tt, prod)
            _shifted_copies(ext_p, sh, nrows, cshifts)

            def mid(r0):
                rows = pl.ds(r0, RC)
                (q,) = _conv_chunks(sh, wb, coffs, (r0,), cshifts)
                dy = dy_ref[rows, :].astype(F32)
                stage[1, rows, :] = (dy * q).astype(BF)
                ext_d[rows, :] = dy * ab[rows, :].astype(F32)
            _chunks(tt, mid)
            _conv_wgrad_chunked(dw_ref, ext_d, sh, coffs, tt, cshifts)
            _shifted_copies(ext_d, sh, nrows, ashifts)

            def fin(r0):
                rows = pl.ds(r0, RC)
                (dp,) = _conv_chunks(sh, wb, aoffs, (r0,), ashifts)
                stage[0, rows, :] = (dp * ac[rows, :].astype(F32)).astype(BF)
                stage[2, rows, :] = (dp * ah[rows, :].astype(F32)).astype(BF)
            _chunks(tt, fin)

        dz_ref[...] = stage[b]

    return pl.pallas_call(
        body, name="mix_a_bwd", grid=(ntt, 3),
        in_specs=[ANY, pl.BlockSpec((None, tt, D), lambda i, b: (0, tile(i), 0)),
                  cur(0), cur(1), cur(2), halo(0), halo(2), row(KA), ANY],
        out_specs=[pl.BlockSpec((tt, D), lambda i, b: (tile(i), b)), pl.BlockSpec((KA, D), lambda i, b: (0, 0))],
        out_shape=[jax.ShapeDtypeStruct(dz.shape, BF), jax.ShapeDtypeStruct((KA, D), F32)],
        scratch_shapes=[pltpu.VMEM((nrows + SUBLANES, D), F32), pltpu.VMEM((nrows + SUBLANES, D), F32),
                        pltpu.VMEM((max(len(cshifts), len(ashifts)), nrows, D), F32),
                        pltpu.VMEM((KA * SUBLANES, D), F32), pltpu.VMEM((3, tt, D), BF)],
        input_output_aliases={0: 0},
        compiler_params=_cp("arbitrary", "arbitrary"),
    )(dz, dy3, z, z, z, z, z, wa, dep)


def _mix_b_bwd(dz, dy3, s, z, wc, lg, lb, S):
    T = z.shape[0]
    tt = min(S, MIX_TILE)
    nt = S // tt
    ntt = T // tt
    tile, cur, halo, row = _tile_specs(tt, ntt, True)

    nrows = HALO + tt

    def body(dz_in, dy_ref, s_ref, ca, cg, ca_h, cg_h, w_ref, lg_ref, lb_ref,
             dz_ref, dw_ref, dbc_ref, dlg_ref, dlb_ref, ext_r, ext_d, sh, wb, accs, stage):
        i, b = pl.program_id(0), pl.program_id(1)
        ti = ntt - 1 - i

        @pl.when((i == 0) & (b == 0))
        def _():
            dw_ref[...] = jnp.zeros_like(dw_ref)
            dbc_ref[...] = jnp.zeros_like(dbc_ref)
            dlg_ref[...] = jnp.zeros_like(dlg_ref)
            dlb_ref[...] = jnp.zeros_like(dlb_ref)
            ext_d[...] = jnp.zeros_like(ext_d)
            ext_r[nrows:, :] = jnp.zeros((SUBLANES, D), F32)
            _fill_taps(wb, w_ref, KC)

        @pl.when(b == 0)
        def _():
            first = (ti % nt) == 0
            last = (ti % nt) == nt - 1
            ext_r[0:HALO, :] = jnp.where(first, 0.0, ca_h[...].astype(F32) * _sig(cg_h[...].astype(F32)))
            ext_d[tt:nrows, :] = jnp.where(last, 0.0, ext_d[0:HALO, :])
            accs[...] = jnp.zeros_like(accs)

            def point(r0):
                rows = pl.ds(r0, RC)
                n, r = _ln_stats(s_ref[rows, :].astype(F32))
                t = n * lg_ref[...] + lb_ref[...]
                sg = _sig(t)
                dt = dy_ref[rows, :].astype(F32) * (sg * (1.0 + t * (1.0 - sg)))
                accs[0] += dt * n
                accs[1] += dt
                ds = _ln_bwd(dt * lg_ref[...], n, r)
                accs[2] += ds
                ext_d[rows, :] = ds
                ext_r[pl.ds(HALO + r0, RC), :] = ca[rows, :].astype(F32) * _sig(cg[rows, :].astype(F32))
            _chunks(tt, point, group=4)
            dlg_ref[...] += jnp.sum(accs[0], axis=0, keepdims=True)
            dlb_ref[...] += jnp.sum(accs[1], axis=0, keepdims=True)
            dbc_ref[...] += jnp.sum(accs[2], axis=0, keepdims=True)

            _shifted_copies(ext_r, sh, nrows)
            _conv_wgrad_chunked(dw_ref, ext_d, sh, _causal_offsets(KC), tt)
            _shifted_copies(ext_d, sh, nrows)

            def conv(r0):
                rows = pl.ds(r0, RC)
                (dr,) = _conv_chunks(sh, wb, _anticausal_offsets(KC), (r0,))
                cav = ca[rows, :].astype(F32)
                sgc = _sig(cg[rows, :].astype(F32))
                stage[0, rows, :] = (dr * sgc).astype(BF)
                stage[1, rows, :] = (dr * cav * sgc * (1.0 - sgc)).astype(BF)
            _chunks(tt, conv)

        dz_ref[...] = stage[b]

    vec = pl.BlockSpec((1, D), lambda i, b: (0, 0))
    return pl.pallas_call(
        body, name="mix_b_bwd", grid=(ntt, 2),
        in_specs=[ANY, pl.BlockSpec((None, tt, D), lambda i, b: (1, tile(i), 0)),
                  pl.BlockSpec((tt, D), lambda i, b: (tile(i), 0)),
                  cur(3), cur(4), halo(3), halo(4), row(KC), row(), row()],
        out_specs=[pl.BlockSpec((tt, D), lambda i, b: (tile(i), 3 + b)),
                   pl.BlockSpec((KC, D), lambda i, b: (0, 0)), vec, vec, vec],
        out_shape=[jax.ShapeDtypeStruct(dz.shape, BF), jax.ShapeDtypeStruct((KC, D), F32)]
        + [jax.ShapeDtypeStruct((1, D), F32)] * 3,
        scratch_shapes=[pltpu.VMEM((nrows + SUBLANES, D), F32), pltpu.VMEM((nrows + SUBLANES, D), F32),
                        pltpu.VMEM((SUBLANES, nrows, D), F32), pltpu.VMEM((KC * SUBLANES, D), F32),
                        pltpu.VMEM((3, RC, D), F32), pltpu.VMEM((2, tt, D), BF)],
        input_output_aliases={0: 0},
        compiler_params=_cp("arbitrary", "arbitrary"),
    )(dz, dy3, s, z, z, z, z, wc, lg, lb)


def _mix_s_bwd(dz, dy3, z, lg, lb, ws, wst, bst, S):
    T = z.shape[0]
    tt = min(S, MIX_TILE)
    ntt = T // tt
    _, cur, _, row = _tile_specs(tt, ntt, False)

    def body(dz_in, dy_ref, su, sv, lg_ref, lb_ref, ws_ref, wst_ref, bst_ref,
             dz_ref, dws_ref, dbst_ref, dlg_ref, dlb_ref, u_scr, vn_scr, dvn_scr, stage):
        i, b = pl.program_id(0), pl.program_id(1)

        @pl.when((i == 0) & (b == 0))
        def _():
            dws_ref[...] = jnp.zeros_like(dws_ref)
            dbst_ref[...] = jnp.zeros_like(dbst_ref)
            dlg_ref[...] = jnp.zeros_like(dlg_ref)
            dlb_ref[...] = jnp.zeros_like(dlb_ref)

        @pl.when(b == 0)
        def _():
            u, du_dx = _gelu(su[...].astype(F32))
            v, dv_dx = _gelu(sv[...].astype(F32))
            u_scr[...] = u
            n, r = _ln_stats(v)
            vn_scr[...] = (n * lg_ref[...] + lb_ref[...]).astype(BF)
            mask = _causal_mask(False)
            mask_t = _causal_mask(True)
            for h in range(HEADS):
                wm = jnp.where(mask, ws_ref[h], 0.0).astype(BF)
                wmt = jnp.where(mask_t, wst_ref[h], 0.0).astype(BF)
                cols = slice(h * CHUNK, (h + 1) * CHUNK)
                for c in range(tt // CHUNK):
                    rows = slice(c * CHUNK, (c + 1) * CHUNK)
                    vb = vn_scr[rows, cols]
                    mixed = _dot(wm, vb) + bst_ref[:, h:h + 1]
                    dy = dy_ref[rows, cols].astype(F32)
                    dmix = dy * u_scr[rows, cols]
                    u_scr[rows, cols] = dy * mixed
                    dbst_ref[:, h:h + 1] += jnp.sum(dmix, axis=1, keepdims=True)
                    dmb = dmix.astype(BF)
                    dws_ref[h] += _dot_nt(dmb, vb)
                    dvn_scr[rows, cols] = _dot(wmt, dmb)
            stage[0] = (u_scr[...] * du_dx).astype(BF)
            dvn = dvn_scr[...]
            dlg_ref[...] += jnp.sum(dvn * n, axis=0, keepdims=True)
            dlb_ref[...] += jnp.sum(dvn, axis=0, keepdims=True)
            stage[1] = (_ln_bwd(dvn * lg_ref[...], n, r) * dv_dx).astype(BF)

        dz_ref[...] = stage[b]

    vec = pl.BlockSpec((1, D), lambda i, b: (0, 0))
    wsp = pl.BlockSpec((HEADS, CHUNK, CHUNK), lambda i, b: (0, 0, 0))
    bsp = pl.BlockSpec((CHUNK, HEADS), lambda i, b: (0, 0))
    return pl.pallas_call(
        body, name="mix_s_bwd", grid=(ntt, 2),
        in_specs=[ANY, pl.BlockSpec((None, tt, D), lambda i, b: (2, i, 0)),
                  cur(5), cur(6), row(), row(), wsp, wsp, bsp],
        out_specs=[pl.BlockSpec((tt, D), lambda i, b: (i, 5 + b)), wsp, bsp, vec, vec],
        out_shape=[jax.ShapeDtypeStruct(dz.shape, BF), jax.ShapeDtypeStruct((HEADS, CHUNK, CHUNK), F32),
                   jax.ShapeDtypeStruct((CHUNK, HEADS), F32), jax.ShapeDtypeStruct((1, D), F32),
                   jax.ShapeDtypeStruct((1, D), F32)],
        scratch_shapes=[pltpu.VMEM((tt, D), F32), pltpu.VMEM((tt, D), BF), pltpu.VMEM((tt, D), F32),
                        pltpu.VMEM((2, tt, D), BF)],
        input_output_aliases={0: 0},
        compiler_params=_cp("arbitrary", "arbitrary"),
    )(dz, dy3, z, z, lg, lb, ws, wst, bst)


def _in_proj_bwd(dz, w, x, g, dx1, dep):
    T = x.shape[0]
    nc = w.shape[2]
    tm = min(T, 1024)
    tn = nc
    nj = nc // tn
    ep = min(tm, 128)

    def body(dz_ref, w_ref, x_ref, g_ref, dx1_ref, dep_ref, dx_ref, dg_ref, acc):
        i, k, j = pl.program_id(0), pl.program_id(1), pl.program_id(2)

        @pl.when((i == 0) & (k == 0) & (j == 0))
        def _():
            dg_ref[...] = jnp.zeros_like(dg_ref)

        @pl.when((k == 0) & (j == 0))
        def _():
            acc[...] = jnp.zeros_like(acc)

        acc[...] += _dot_nt(dz_ref[...], w_ref[...])

        @pl.when((k == NSH - 1) & (j == nj - 1))
        def _():
            def step(c, dg):
                rows = pl.ds(pl.multiple_of(c * ep, ep), ep)
                dx, dgc = _rms_bwd(acc[rows, :], x_ref[rows, :], g_ref[...])
                dx_ref[rows, :] = dx1_ref[rows, :] + dx
                return dg + dgc
            dg_ref[...] += lax.fori_loop(0, tm // ep, step, jnp.zeros((1, D), F32))

    rowblk = pl.BlockSpec((tm, D), lambda i, k, j: (i, 0))
    vec = pl.BlockSpec((1, D), lambda i, k, j: (0, 0))
    return pl.pallas_call(
        body, name="in_proj_bwd", grid=(T // tm, NSH, nj),
        in_specs=[pl.BlockSpec((tm, tn), lambda i, k, j: (i, k * nj + j)),
                  pl.BlockSpec((None, D, tn), lambda i, k, j: (k, 0, j)), rowblk, vec, rowblk, ANY],
        out_specs=[rowblk, vec],
        out_shape=[jax.ShapeDtypeStruct((T, D), F32), jax.ShapeDtypeStruct((1, D), F32)],
        scratch_shapes=[pltpu.VMEM((tm, D), F32)],
        compiler_params=_cp("arbitrary", "arbitrary", "arbitrary"),
    )(dz, w, x, g, dx1, dep)


def _layer_fwd(x, p, S, dep, late):
    h, z = _in_proj(x, p["g_mix_pre"], p["w_in"], dep)
    ya = _mix_a_fwd(z, p["conv_a_w"], S)
    yc, s = _mix_b_fwd(z, p["conf_dw_w"], p["conf_dw_b"], p["conf_ln_g"], p["conf_ln_b"], S)
    ys = _mix_s_fwd(z, p["sgu_ln_g"], p["sgu_ln_b"], p["sgu_ws"], p["sgu_bt"], S)
    more, dep2 = late(ys)
    p.update(more)
    p3, merged, m, x1 = _mix_out_fwd(ya, yc, ys, z, x, p["w_branch"], p["w_out"], p["g_mix_post"], dep2)
    h2, a, f, x2 = _ffn_fwd(x1, p["g_ffn_pre"], p["w_ff1"], p["w_ff2"], p["g_ffn_post"])
    saved = dict(x=x, h=h, z=z, ya=ya, yc=yc, ys=ys, s=s, p3=p3, merged=merged, m=m, x1=x1, h2=h2, a=a, f=f)
    return x2, saved


def _layer_bwd(dx2, p, sv, S, dep, hooks):
    after_ffn, early, mid = hooks
    T = dx2.shape[0]
    bt = min(T, WGRAD_TILE)
    nt = T // bt
    rk = D // NSH
    df, da, dx1, dg_ffn_post, dg_ffn_pre = _ffn_bwd(dx2, sv["f"], p["g_ffn_post"], sv["a"], p["w_ff2"],
                                                    p["w_ff1"], sv["x1"], p["g_ffn_pre"], dep)
    dw_ff2 = _wgrad("wgrad_ff2", (sv["a"], df), (NSH, nt),
                    [pl.BlockSpec((bt, D), lambda k, t: (t, k)), pl.BlockSpec((bt, D), lambda k, t: (t, 0))],
                    pl.BlockSpec((None, D, D), lambda k, t: (k, 0, 0)),
                    jax.ShapeDtypeStruct((NSH, D, D), BF), (D, D), relu2=True)
    dw_ff1 = _wgrad("wgrad_ff1", (sv["h2"], da), (NSH, nt),
                    [pl.BlockSpec((bt, D), lambda k, t: (t, 0)), pl.BlockSpec((bt, D), lambda k, t: (t, k))],
                    pl.BlockSpec((None, D, D), lambda k, t: (k, 0, 0)),
                    jax.ShapeDtypeStruct((NSH, D, D), BF), (D, D))
    dm, dp3, dy3, dz, dg_mix_post = _mix_out_bwd(dx1, sv["m"], p["g_mix_post"], p["w_out"], sv["p3"], sv["z"],
                                                 p["w_branch"], after_ffn(dx1))
    dw_out = _wgrad("wgrad_out", (sv["merged"], dm), (nt,),
                    [pl.BlockSpec((bt, D), lambda t: (t, 0)), pl.BlockSpec((bt, D), lambda t: (t, 0))],
                    pl.BlockSpec((D, D), lambda t: (0, 0)),
                    jax.ShapeDtypeStruct((D, D), BF), (D, D)).reshape(NSH, rk, D)
    ysp = lambda n: pl.BlockSpec((bt, D), lambda b, t: (jnp.where(b == n, t, 0), 0))
    dw_br = _wgrad("wgrad_branch", (sv["ya"], sv["yc"], sv["ys"], dp3), (3, nt),
                   [ysp(0), ysp(1), ysp(2), pl.BlockSpec((None, bt, D), lambda b, t: (b, t, 0))],
                   pl.BlockSpec((NSH, None, rk, D), lambda b, t: (0, b, 0, 0)),
                   jax.ShapeDtypeStruct((NSH, 3, rk, D), BF), (D, D), pick=lambda: pl.program_id(0))
    dz, dwa = _mix_a_bwd(dz, dy3, sv["z"], p["conv_a_w"], S, early([dw_br, dw_out, dw_ff1, dw_ff2]))
    dz, dwc, dbc, dclg, dclb = _mix_b_bwd(dz, dy3, sv["s"], sv["z"], p["conf_dw_w"], p["conf_ln_g"],
                                          p["conf_ln_b"], S)
    dz, dws, dbst, dslg, dslb = _mix_s_bwd(dz, dy3, sv["z"], p["sgu_ln_g"], p["sgu_ln_b"], p["sgu_ws"],
                                           p["sgu_wst"], p["sgu_bt"], S)
    dx, dg_mix_pre = _in_proj_bwd(dz, p["w_in"], sv["x"], p["g_mix_pre"], dx1, mid(dz))
    tn = p["w_in"].shape[2]
    nj = p["w_in"].shape[2] // tn
    dw_in = _wgrad("wgrad_in", (sv["h"], dz), (NSH, nj, nt),
                   [pl.BlockSpec((bt, D), lambda k, j, t: (t, 0)),
                    pl.BlockSpec((bt, tn), lambda k, j, t: (t, k * nj + j))],
                   pl.BlockSpec((None, D, tn), lambda k, j, t: (k, 0, j)),
                   jax.ShapeDtypeStruct(p["w_in"].shape, BF), (D, tn))
    tril = jnp.tril(jnp.ones((CHUNK, CHUNK), bool))
    small = dict(norm_mix_pre=dg_mix_pre, norm_mix_post=dg_mix_post, norm_ffn_pre=dg_ffn_pre,
                 norm_ffn_post=dg_ffn_post, conv_a_w=dwa, conf_dw_w=dwc, conf_dw_b=dbc, conf_ln_g=dclg,
                 conf_ln_b=dclb, sgu_ln_g=dslg, sgu_ln_b=dslb,
                 sgu_ws=jnp.where(tril[None], dws, 0.0), sgu_b=dbst.T)
    big = dict(w_in=dw_in, w_branch=dw_br, w_out=dw_out, w_ff1=dw_ff1, w_ff2=dw_ff2)
    return dx, big, small


SMALL_NAMES = ("norm_mix_pre", "norm_mix_post", "norm_ffn_pre", "norm_ffn_post", "conv_a_w", "conf_dw_w",
               "conf_dw_b", "conf_ln_g", "conf_ln_b", "sgu_ln_g", "sgu_ln_b", "sgu_b", "sgu_ws")
SMALL_ROWS = dict(norm_mix_pre=1, norm_mix_post=1, norm_ffn_pre=1, norm_ffn_post=1, conv_a_w=KA, conf_dw_w=KC,
                  conf_dw_b=1, conf_ln_g=1, conf_ln_b=1, sgu_ln_g=1, sgu_ln_b=1, sgu_b=1, sgu_ws=CHUNK)
def _pad8(r):
    return -(-r // SUBLANES) * SUBLANES


PACK_ROWS = sum(_pad8(r) for r in SMALL_ROWS.values())


def _pack_small(d):
    parts = []
    for n in SMALL_NAMES:
        r = SMALL_ROWS[n]
        parts.append(jnp.pad(d[n].reshape(r, D).astype(F32), ((0, _pad8(r) - r), (0, 0))))
    return jnp.concatenate(parts, axis=0)


def _unpack_small(a, shapes):
    out, r = {}, 0
    for n in SMALL_NAMES:
        out[n] = a[:, r:r + SMALL_ROWS[n]].reshape((a.shape[0],) + tuple(shapes[n]))
        r += _pad8(SMALL_ROWS[n])
    return out


def _me():
    return lax.axis_index("x"), lax.axis_index("y"), lax.axis_index("c")


def _slab(ref, q, a, h=None):
    r = ref.shape[1]
    rows = slice(None) if h is None else pl.ds(h * (r // 2), r // 2)
    return ref.at[pl.ds(q * a, a), rows, :]


def _rows(ref, h):
    r = ref.shape[-2]
    lead = (slice(None),) * (len(ref.shape) - 2)
    return ref.at[lead + (pl.ds(h * (r // 2), r // 2), slice(None))]


def _rcopy(src, dst, sems, idx, dev):
    return pltpu.make_async_remote_copy(src_ref=src, dst_ref=dst, send_sem=sems[0].at[idx], recv_sem=sems[1].at[idx],
                                        device_id=dev, device_id_type=MESH)


def _send_halves_to_sibling(parts):
    n = len(parts)

    def body(*refs):
        src, dst = refs[:n], refs[n:2 * n]
        sems = refs[2 * n:2 * n + 2]
        x, y, c = _me()
        cps = [_rcopy(_rows(src[i], 1 - c), dst[i], sems, i, (x, y, 1 - c)) for i in range(n)]
        for cp in cps:
            cp.start()
        for cp in cps:
            cp.wait()

    outs = [jax.ShapeDtypeStruct((p.shape[0], p.shape[1] // 2, p.shape[2]), p.dtype) for p in parts]
    return pl.pallas_call(
        body, name="pair_exchange", in_specs=[ANY] * n, out_specs=[ANY] * n, out_shape=outs,
        scratch_shapes=[pltpu.SemaphoreType.DMA((n,)), pltpu.SemaphoreType.DMA((n,))],
    )(*parts)


PAIR_BLOCK_BYTES = 3 * 512 * 1024


def _pair_add(parts, sibs, c):
    n = len(parts)
    steps = 1
    while any(p.shape[0] * (p.shape[1] // 2 // steps) * p.shape[2] * 2 > PAIR_BLOCK_BYTES for p in parts):
        steps *= 2

    def body(c_ref, *refs):
        for p_ref, s_ref, o_ref in zip(refs[:n], refs[n:2 * n], refs[2 * n:]):
            o_ref[...] = (p_ref[...].astype(F32) + s_ref[...].astype(F32)).astype(BF)

    def blk(p):
        return (p.shape[0], p.shape[1] // 2 // steps, p.shape[2])

    mine = [pl.BlockSpec(blk(p), lambda g, c_ref: (0, c_ref[0] * steps + g, 0)) for p in parts]
    same = [pl.BlockSpec(blk(p), lambda g, c_ref: (0, g, 0)) for p in parts]
    return pl.pallas_call(
        body, name="pair_add",
        grid_spec=pltpu.PrefetchScalarGridSpec(
            num_scalar_prefetch=1, grid=(steps,), in_specs=mine + same, out_specs=same),
        out_shape=[jax.ShapeDtypeStruct(s.shape, BF) for s in sibs],
        compiler_params=_cp("arbitrary"),
    )(c, *parts, *sibs)


def _other_chips(x, y):
    return [(1 - x, y), (x, 1 - y), (1 - x, 1 - y)]


def _split_call(name, copies, srcs, lands, sems=None, after=()):
    n, m = len(srcs), len(lands)
    hbm = lambda t: pltpu.HBM(t.shape, t.dtype)
    pin = lambda t: pltpu.with_memory_space_constraint(t, pltpu.HBM)
    thru = [hbm(t) for t in srcs] + [hbm(t) for t in lands]
    sem_spec = pl.BlockSpec(memory_space=pltpu.SEMAPHORE)
    effect = pltpu.CompilerParams(has_side_effects=pltpu.SideEffectType.DATAFLOW_SIDE_EFFECTING)
    if sems is None:
        def start_body(*refs):
            src, land = refs[:n], refs[n:n + m]
            ssem, rsem = refs[n + m + len(after)], refs[n + m + len(after) + 1]
            token = refs[-1]
            cps = copies(src, land, (ssem, rsem))
            for cp in cps:
                cp.start()
            token[...] = jnp.zeros_like(token)

        ncp = copies.count
        out = pl.pallas_call(
            start_body, name=name,
            out_shape=(pltpu.SemaphoreType.DMA((ncp,)), pltpu.SemaphoreType.DMA((ncp,)), *thru,
                       jax.ShapeDtypeStruct((8, 128), F32)),
            in_specs=[ANY] * (n + m + len(after)),
            out_specs=(sem_spec, sem_spec, *([ANY] * (n + m)), pl.BlockSpec(memory_space=pltpu.VMEM)),
            input_output_aliases={i: 2 + i for i in range(n + m)},
            compiler_params=effect,
        )(*[pin(t) for t in srcs], *[pin(t) for t in lands], *after)
        return out[0], out[1], list(out[2:2 + n]), list(out[2 + n:2 + n + m]), out[-1]

    def wait_body(*refs):
        src, land = refs[:n], refs[n:n + m]
        ssem, rsem = refs[n + m], refs[n + m + 1]
        for cp in copies(src, land, (ssem, rsem)):
            cp.wait_send()
            cp.wait_recv()

    out = pl.pallas_call(
        wait_body, name=name, out_shape=tuple(thru),
        in_specs=[ANY] * (n + m) + [sem_spec, sem_spec] + [ANY] * len(after),
        out_specs=tuple([ANY] * (n + m)),
        input_output_aliases={i: i for i in range(n + m)},
        compiler_params=effect,
    )(*srcs, *lands, sems[0], sems[1], *after)
    return list(out[:n]), list(out[n:])


def _cast_into(w, land, layer, kidx, dep):
    _, a, R, C = w.shape
    br = R
    while br * C > 512 * 1024 and br % 32 == 0:
        br //= 2

    def body(k_ref, w_ref, land_ref, dep_ref, o_ref):
        o_ref[...] = w_ref[...].astype(o_ref.dtype)

    return pl.pallas_call(
        body, name="cast_into",
        grid_spec=pltpu.PrefetchScalarGridSpec(
            num_scalar_prefetch=1, grid=(a, R // br),
            in_specs=[pl.BlockSpec((None, None, br, C), lambda e, i, k: (layer, e, i, 0)), ANY, ANY],
            out_specs=pl.BlockSpec((None, br, C), lambda e, i, k: (k[0] * a + e, i, 0))),
        out_shape=jax.ShapeDtypeStruct(land.shape, land.dtype), input_output_aliases={2: 0},
        compiler_params=_cp("arbitrary", "arbitrary"),
    )(kidx, w, land, dep)


class _GatherCopies:
    def __init__(self, n, halves=True):
        self.n, self.count, self.halves = n, 3 * n, halves

    def __call__(self, src, land, sems):
        x, y, c = _me()
        k = 2 * x + y
        cps = []
        for j, (qx, qy) in enumerate(_other_chips(x, y)):
            for i in range(self.n):
                mine = _slab(land[i], k, land[i].shape[0] // NSH, c if self.halves else None)
                cps.append(_rcopy(mine, mine, sems, j * self.n + i, (qx, qy, c)))
        return cps


def _gather_finish(lands):
    n = len(lands)

    def body(*refs):
        dst = refs[n:2 * n]
        sems = refs[2 * n:2 * n + 2]
        x, y, c = _me()
        av = [d.shape[0] // NSH for d in dst]
        cps = []
        for j, (qx, qy) in enumerate(_other_chips(x, y)):
            for i in range(n):
                got = _slab(dst[i], 2 * qx + qy, av[i], c)
                cps.append(_rcopy(got, got, sems, j * n + i, (x, y, 1 - c)))
        for cp in cps:
            cp.start()
        for j, (qx, qy) in enumerate(_other_chips(x, y)):
            for i in range(n):
                other = _slab(dst[i], 2 * qx + qy, av[i], 1 - c)
                _rcopy(other, other, sems, j * n + i, (x, y, c)).wait_recv()
        for cp in cps:
            cp.wait_send()

    return pl.pallas_call(
        body, name="gather_finish", in_specs=[ANY] * n, out_specs=[ANY] * n,
        out_shape=[jax.ShapeDtypeStruct(t.shape, t.dtype) for t in lands],
        input_output_aliases={i: i for i in range(n)},
        scratch_shapes=[pltpu.SemaphoreType.DMA((3 * n,)), pltpu.SemaphoreType.DMA((3 * n,))],
    )(*lands)


class _PairCopies:
    def __init__(self, n):
        self.n, self.count = n, n

    def __call__(self, src, land, sems):
        x, y, c = _me()
        return [_rcopy(_rows(src[i], 1 - c), land[i], sems, i, (x, y, 1 - c)) for i in range(self.n)]


class _ScatterCopies:
    def __init__(self, n):
        self.n, self.count = n, 3 * n

    def __call__(self, src, land, sems):
        x, y, c = _me()
        k = 2 * x + y
        cps = []
        for j, (qx, qy) in enumerate(_other_chips(x, y)):
            for i in range(self.n):
                a = src[i].shape[0] // NSH
                cps.append(_rcopy(_slab(src[i], 2 * qx + qy, a), _slab(land[i], k, a), sems, j * self.n + i,
                                  (qx, qy, c)))
        return cps


def _sum_chips(own, rcv, acc, layer, nlayers, idx):
    A, hr, C = rcv.shape
    a = A // NSH
    br = min(hr, 512)
    nb = hr // br

    def body(*refs):
        r0, r1, r2, r3 = refs[1:5]
        o_ref = refs[-1]
        o_ref[...] = ((r0[...].astype(F32) + r1[...].astype(F32)) + r2[...].astype(F32)) + r3[...].astype(F32)

    slot = lambda s: pl.BlockSpec((None, br, C), lambda e, i, ix: (ix[s] * a + e, i, 0))
    ops = [own, rcv, rcv, rcv]
    in_specs = [slot(0), slot(1), slot(2), slot(3)]
    aliases = {}
    if acc is not None:
        ops.append(acc)
        in_specs.append(ANY)
        aliases = {5: 0}
    return pl.pallas_call(
        body, name="sum_chips",
        grid_spec=pltpu.PrefetchScalarGridSpec(
            num_scalar_prefetch=1, grid=(a, nb), in_specs=in_specs,
            out_specs=pl.BlockSpec((None, None, br, C), lambda e, i, ix: (layer, e, ix[4] * nb + i, 0))),
        out_shape=jax.ShapeDtypeStruct((nlayers, a, 2 * hr, C), F32), input_output_aliases=aliases,
        compiler_params=_cp("arbitrary", "arbitrary"),
    )(idx, *ops)


def _join_halves(fulls):
    n = len(fulls)

    def body(*refs):
        buf = refs[n:2 * n]
        sems = refs[2 * n:2 * n + 2]
        x, y, c = _me()
        cps = [_rcopy(_rows(buf[i], c), _rows(buf[i], c), sems, i, (x, y, 1 - c)) for i in range(n)]
        for cp in cps:
            cp.start()
        for i in range(n):
            _rcopy(_rows(buf[i], 1 - c), _rows(buf[i], 1 - c), sems, i, (x, y, c)).wait_recv()
        for cp in cps:
            cp.wait_send()

    return pl.pallas_call(
        body, name="join_halves", in_specs=[ANY] * n, out_specs=[ANY] * n,
        out_shape=[jax.ShapeDtypeStruct(t.shape, t.dtype) for t in fulls],
        input_output_aliases={i: i for i in range(n)},
        scratch_shapes=[pltpu.SemaphoreType.DMA((n,)), pltpu.SemaphoreType.DMA((n,))],
    )(*fulls)


def _small_blocks(hr):
    br = hr
    while br > 512 and br % 16 == 0:
        br //= 2
    return br, hr // br


def _pair_sum_slot(part, sib, ck):
    R, C = part.shape
    hr = R // 2
    br, nb = _small_blocks(hr)

    def body(ix, p_ref, s_ref, o_ref):
        o_ref[...] = p_ref[...] + s_ref[...]

    return pl.pallas_call(
        body, name="pair_sum_slot",
        grid_spec=pltpu.PrefetchScalarGridSpec(
            num_scalar_prefetch=1, grid=(nb,),
            in_specs=[pl.BlockSpec((br, C), lambda i, ix: (ix[0] * nb + i, 0)),
                      pl.BlockSpec((br, C), lambda i, ix: (i, 0))],
            out_specs=pl.BlockSpec((None, br, C), lambda i, ix: (ix[1], i, 0))),
        out_shape=jax.ShapeDtypeStruct((NSH, hr, C), F32),
        compiler_params=_cp("arbitrary"),
    )(ck, part, sib)


def _sum_slots(slots, ck):
    _, hr, C = slots.shape
    br, nb = _small_blocks(hr)

    def body(ix, s_ref, o_ref):
        o_ref[...] = ((s_ref[0] + s_ref[1]) + s_ref[2]) + s_ref[3]

    return pl.pallas_call(
        body, name="sum_slots",
        grid_spec=pltpu.PrefetchScalarGridSpec(
            num_scalar_prefetch=1, grid=(nb,),
            in_specs=[pl.BlockSpec((NSH, br, C), lambda i, ix: (0, i, 0))],
            out_specs=pl.BlockSpec((br, C), lambda i, ix: (ix[0] * nb + i, 0))),
        out_shape=jax.ShapeDtypeStruct((2 * hr, C), F32),
        compiler_params=_cp("arbitrary"),
    )(ck, slots)


def _adamw(w, g, m, v):
    shape = w.shape
    C = shape[-1]
    R = shape[-2]
    A = 1
    for s in shape[:-2]:
        A *= s
    br = R
    while br * C > 256 * 1024 and br % 16 == 0:
        br //= 2
    c1 = 1.0 / (1.0 - ADAM_B1 ** ADAM_STEP)
    c2 = 1.0 / (1.0 - ADAM_B2 ** ADAM_STEP)

    def body(w_ref, g_ref, m_ref, v_ref, og_ref, d_ref, nm_ref, nv_ref):
        gv = g_ref[...]
        og_ref[...] = gv
        nm = ADAM_B1 * m_ref[...] + (1.0 - ADAM_B1) * gv
        nv = ADAM_B2 * v_ref[...] + (1.0 - ADAM_B2) * (gv * gv)
        nm_ref[...] = nm
        nv_ref[...] = nv
        d_ref[...] = -ADAM_LR * ((nm * c1) / (jnp.sqrt(nv * c2) + ADAM_EPS) + ADAM_WD * w_ref[...])

    blk = pl.BlockSpec((None, br, C), lambda a, i: (a, i, 0))
    outs = pl.pallas_call(
        body, name="adamw", grid=(A, R // br), in_specs=[blk] * 4, out_specs=[blk] * 4,
        out_shape=[jax.ShapeDtypeStruct((A, R, C), F32)] * 4,
        compiler_params=_cp("arbitrary", "arbitrary"),
    )(*(t.reshape(A, R, C) for t in (w, g, m, v)))
    return tuple(o.reshape(shape) for o in outs)


WEIGHTS = ("norm_mix_pre", "norm_mix_post", "norm_ffn_pre", "norm_ffn_post", "w_in", "conv_a_w", "conf_dw_w",
           "conf_dw_b", "conf_ln_g", "conf_ln_b", "sgu_ln_g", "sgu_ln_b", "sgu_ws", "sgu_b", "w_branch", "w_out",
           "w_ff1", "w_ff2")
BIG = ("w_in", "w_branch", "w_out", "w_ff1", "w_ff2")
CONV_ROWS = 48


def kernel(x, norm_mix_pre, norm_mix_post, norm_ffn_pre, norm_ffn_post, w_in, conv_a_w, conf_dw_w, conf_dw_b, conf_ln_g, conf_ln_b, sgu_ln_g, sgu_ln_b, sgu_ws, sgu_b, w_branch, w_out, w_ff1, w_ff2, loss_target, m_norm_mix_pre, m_norm_mix_post, m_norm_ffn_pre, m_norm_ffn_post, m_w_in, m_conv_a_w, m_conf_dw_w, m_conf_dw_b, m_conf_ln_g, m_conf_ln_b, m_sgu_ln_g, m_sgu_ln_b, m_sgu_ws, m_sgu_b, m_w_branch, m_w_out, m_w_ff1, m_w_ff2, v_norm_mix_pre, v_norm_mix_post, v_norm_ffn_pre, v_norm_ffn_post, v_w_in, v_conv_a_w, v_conf_dw_w, v_conf_dw_b, v_conf_ln_g, v_conf_ln_b, v_sgu_ln_g, v_sgu_ln_b, v_sgu_ws, v_sgu_b, v_w_branch, v_w_out, v_w_ff1, v_w_ff2):
    w = dict(norm_mix_pre=norm_mix_pre, norm_mix_post=norm_mix_post, norm_ffn_pre=norm_ffn_pre,
             norm_ffn_post=norm_ffn_post, w_in=w_in, conv_a_w=conv_a_w, conf_dw_w=conf_dw_w, conf_dw_b=conf_dw_b,
             conf_ln_g=conf_ln_g, conf_ln_b=conf_ln_b, sgu_ln_g=sgu_ln_g, sgu_ln_b=sgu_ln_b, sgu_ws=sgu_ws,
             sgu_b=sgu_b, w_branch=w_branch, w_out=w_out, w_ff1=w_ff1, w_ff2=w_ff2)
    mom = dict(norm_mix_pre=m_norm_mix_pre, norm_mix_post=m_norm_mix_post, norm_ffn_pre=m_norm_ffn_pre,
               norm_ffn_post=m_norm_ffn_post, w_in=m_w_in, conv_a_w=m_conv_a_w, conf_dw_w=m_conf_dw_w,
               conf_dw_b=m_conf_dw_b, conf_ln_g=m_conf_ln_g, conf_ln_b=m_conf_ln_b, sgu_ln_g=m_sgu_ln_g,
               sgu_ln_b=m_sgu_ln_b, sgu_ws=m_sgu_ws, sgu_b=m_sgu_b, w_branch=m_w_branch, w_out=m_w_out,
               w_ff1=m_w_ff1, w_ff2=m_w_ff2)
    var = dict(norm_mix_pre=v_norm_mix_pre, norm_mix_post=v_norm_mix_post, norm_ffn_pre=v_norm_ffn_pre,
               norm_ffn_post=v_norm_ffn_post, w_in=v_w_in, conv_a_w=v_conv_a_w, conf_dw_w=v_conf_dw_w,
               conf_dw_b=v_conf_dw_b, conf_ln_g=v_conf_ln_g, conf_ln_b=v_conf_ln_b, sgu_ln_g=v_sgu_ln_g,
               sgu_ln_b=v_sgu_ln_b, sgu_ws=v_sgu_ws, sgu_b=v_sgu_b, w_branch=v_w_branch, w_out=v_w_out,
               w_ff1=v_w_ff1, w_ff2=v_w_ff2)
    L = w_in.shape[0]
    nseq, S, _ = x.shape
    T = nseq * S
    rk = D // NSH
    mx, my, mc = _me()
    k_chip = 2 * mx + my

    big_src = [w_in.reshape(L, 1, D, w_in.shape[2]), w_branch, w_out.reshape(L, 1, rk, D),
               w_ff1.reshape(L, 1, D, w_ff1.shape[2]), w_ff2.reshape(L, 1, w_ff2.shape[1], D)]
    kidx = jnp.reshape(k_chip, (1,)).astype(jnp.int32)
    conv_src = jnp.concatenate(
        [jnp.pad(conv_a_w, ((0, 0), (0, SUBLANES - KA), (0, 0))), jnp.pad(conf_dw_w, ((0, 0), (0, 1), (0, 0))),
         jnp.zeros((L, CONV_ROWS - SUBLANES - KC - 1, rk), F32)], axis=1)[None]

    def early_params(l, g_in, conv_full):
        return dict(
            g_mix_pre=norm_mix_pre[l][None], g_mix_post=norm_mix_post[l][None], g_ffn_pre=norm_ffn_pre[l][None],
            g_ffn_post=norm_ffn_post[l][None], w_in=g_in, conv_a_w=conv_full[l, :KA],
            conf_dw_w=conv_full[l, SUBLANES:SUBLANES + KC], conf_dw_b=conf_dw_b[l][None],
            conf_ln_g=conf_ln_g[l][None], conf_ln_b=conf_ln_b[l][None], sgu_ln_g=sgu_ln_g[l][None],
            sgu_ln_b=sgu_ln_b[l][None], sgu_ws=sgu_ws[l], sgu_wst=jnp.swapaxes(sgu_ws[l], 1, 2),
            sgu_bt=sgu_b[l].T)

    def late_params(gathered):
        g_br, g_out, g_ff1, g_ff2 = gathered
        return dict(w_branch=g_br.reshape(NSH, 3, rk, D), w_out=g_out.reshape(D, D), w_ff1=g_ff1,
                    w_ff2=g_ff2.reshape(NSH * w_ff2.shape[1], D))

    def cast_lands(srcs, l, dep):
        return [_cast_into(s, lax.empty((NSH * s.shape[1],) + s.shape[2:], F32 if s is conv_src else BF), l, kidx,
                           dep) for s in srcs]

    def gather_start(name, lands, after):
        return _split_call(name, _GatherCopies(len(lands)), [], lands, after=after)

    def gather_land(name, flight, after):
        ssem, rsem, _, lands, _ = flight
        _, lands = _split_call(name, _GatherCopies(len(lands)), [], lands, (ssem, rsem), after)
        return _gather_finish(lands)

    zero_tok = jnp.zeros((8, 128), F32)
    xt = x.reshape(T, D)
    layers, saved = [], []
    head = gather_start("gather_start_0a", cast_lands([big_src[0], conv_src], 0, kidx), [])
    tails = [cast_lands(big_src[1:], l, head[4]) for l in range(L)]
    heads = [None] + [cast_lands(big_src[:1], l, head[4]) for l in range(1, L)]
    behind = [xt] + [t for ls in tails + heads[1:] for t in ls]
    conv_full = None
    for l in range(L):
        got = gather_land(f"gather_wait_{l}a", head, behind if l == 0 else [xt])
        g_in = got[0]
        if l == 0:
            conv_full = got[1].reshape(NSH, L, CONV_ROWS, rk).transpose(1, 2, 0, 3).reshape(L, CONV_ROWS, D)
        tail = gather_start(f"gather_start_{l}b", tails[l], [g_in])
        nxt = {}

        def late(after, l=l, tail=tail, nxt=nxt):
            more = late_params(gather_land(f"gather_wait_{l}b", tail, [after]))
            if l + 1 == L:
                return more, zero_tok
            nxt["head"] = gather_start(f"gather_start_{l + 1}a", heads[l + 1], [more["w_ff1"]])
            return more, nxt["head"][4]

        p = early_params(l, g_in, conv_full)
        xt, sv = _layer_fwd(xt, p, S, tail[4], late)
        head = nxt.get("head")
        layers.append(p)
        saved.append(sv)
    dx, loss_row = _loss_head(xt, loss_target.reshape(T, D))
    loss = lax.psum(loss_row[0, 0], ("x", "y", "c"))

    c_arr = jnp.reshape(mc, (1,)).astype(jnp.int32)
    idx = jnp.stack([k_chip, k_chip ^ 2, k_chip ^ 1, k_chip ^ 3, mc]).astype(jnp.int32)
    fulls = {n: None for n in BIG}
    smalls = [None] * L

    def pair_start(tag, parts):
        lands = [lax.empty((p.shape[0], p.shape[1] // 2, p.shape[2]), p.dtype) for p in parts]
        return _split_call(f"pair_start_{tag}", _PairCopies(len(parts)), parts, lands)

    def pair_land_scatter_start(tag, fl, after):
        ssem, rsem, parts, sib, _ = fl
        parts, sib = _split_call(f"pair_wait_{tag}", _PairCopies(len(parts)), parts, sib, (ssem, rsem), after)
        sums = _pair_add(parts, sib, c_arr)
        rcv = [lax.empty(s.shape, s.dtype) for s in sums]
        return _split_call(f"scatter_start_{tag}", _ScatterCopies(len(sums)), sums, rcv)

    def scatter_land(tag, fl, names, l, after):
        ssem, rsem, sums, rcv, _ = fl
        sums, rcv = _split_call(f"scatter_wait_{tag}", _ScatterCopies(len(sums)), sums, rcv, (ssem, rsem), after)
        for n, o, r in zip(names, sums, rcv):
            fulls[n] = _sum_chips(o, r, fulls[n], l, L, idx)

    pending = []
    pair_b = None
    dep = zero_tok
    for l in reversed(range(L)):
        mine = {}

        def after_ffn(arr, l=l, mine=mine, pair_b=pair_b):
            if pair_b is None:
                return zero_tok
            mine["prev_b"] = pair_land_scatter_start(f"{l + 1}b", pair_b, [arr])
            return mine["prev_b"][4]

        def early(parts, l=l, mine=mine):
            br, rest = parts[0], parts[1:]
            mine["pair_a"] = pair_start(f"{l}a", [br.reshape(NSH * 3, rk, D), *rest])
            return mine["pair_a"][4]

        def mid(arr, l=l, mine=mine):
            mine["a"] = pair_land_scatter_start(f"{l}a", mine["pair_a"], [arr])
            return mine["a"][4]

        dx, big, small = _layer_bwd(dx, layers[l], saved[l], S, dep, (after_ffn, early, mid))
        smalls[l] = _pack_small(small)
        for args in pending:
            scatter_land(*args, [dx])
        pending = [(f"{l}a", mine["a"], BIG[1:], l)]
        if "prev_b" in mine:
            pending.append((f"{l + 1}b", mine["prev_b"], BIG[:1], l + 1))
        pair_b = pair_start(f"{l}b", [big["w_in"]])
        dep = pair_b[4]
    last_b = ("0b", pair_land_scatter_start("0b", pair_b, [dx]), BIG[:1], 0)

    packed = jnp.concatenate(smalls, axis=0)
    nrow = packed.shape[0]
    ck = jnp.stack([mc, k_chip]).astype(jnp.int32)
    (sib,) = _send_halves_to_sibling([packed.reshape(1, nrow, D)])
    slots = _pair_sum_slot(packed, sib.reshape(nrow // 2, D), ck)
    small_flight = _split_call("small_start", _GatherCopies(1, halves=False), [], [slots])

    for args in pending:
        scatter_land(*args, [small_flight[4], last_b[1][4]])
    grads, delta, new_m, new_v = {}, {}, {}, {}
    for n, f in zip(BIG[1:], _join_halves([fulls[n] for n in BIG[1:]])):
        grads[n], delta[n], new_m[n], new_v[n] = _adamw(w[n], f.reshape(w[n].shape), mom[n], var[n])
    scatter_land(*last_b, [delta[BIG[-1]]])
    (f,) = _join_halves([fulls[BIG[0]]])
    n = BIG[0]
    grads[n], delta[n], new_m[n], new_v[n] = _adamw(w[n], f.reshape(w[n].shape), mom[n], var[n])

    _, (slots,) = _split_call("small_wait", _GatherCopies(1, halves=False), [], small_flight[3],
                              (small_flight[0], small_flight[1]), [delta[BIG[0]]])
    (small_sum,) = _join_halves([_sum_slots(slots, ck).reshape(1, 1, nrow, D)])
    shapes = {n: (w[n].shape[1:] if n not in ("conv_a_w", "conf_dw_w") else (w[n].shape[1], D)) for n in SMALL_NAMES}
    sg = _unpack_small(small_sum.reshape(L, PACK_ROWS, D), shapes)
    for n in SMALL_NAMES:
        if n in ("conv_a_w", "conf_dw_w"):
            grads[n] = lax.dynamic_slice_in_dim(sg[n], k_chip * rk, rk, axis=2)
        else:
            grads[n] = sg[n]

    for n in SMALL_NAMES:
        sh = w[n].shape
        flat = (sh[0] * sh[1], sh[2]) if n in ("conv_a_w", "conf_dw_w") else (-1, D)
        g, d, nm, nv = _adamw(*(t.reshape(flat) for t in (w[n], grads[n], mom[n], var[n])))
        grads[n], delta[n], new_m[n], new_v[n] = g.reshape(sh), d.reshape(sh), nm.reshape(sh), nv.reshape(sh)

    return (loss, dx.reshape(x.shape), *[grads[n] for n in WEIGHTS], *[delta[n] for n in WEIGHTS],
            *[new_m[n] for n in WEIGHTS], *[new_v[n] for n in WEIGHTS])
```
